```python
import jax, jax.numpy as jnp
from jax import lax
import numpy as np

D_MODEL = 1024
BATCH = 8
SEQ = 4096
DEPTH = 1

MEM_LEN = 256
EPS = 1e-6

SB_HEADS = 16
SB_HEAD_DIM = 64
SB_WIDTH = SB_HEADS * SB_HEAD_DIM
SB_BLOCK = 128

SSD_EXPAND = 2
SSD_INNER = SSD_EXPAND * D_MODEL
SSD_HEAD_DIM = 64
SSD_HEADS = SSD_INNER // SSD_HEAD_DIM
SSD_GROUPS = 4
SSD_HEADS_PER_GROUP = SSD_HEADS // SSD_GROUPS
SSD_STATE = 128
SSD_CONV = 4
SSD_CHUNK = 128
SSD_CONV_DIM = SSD_INNER + 2 * SSD_GROUPS * SSD_STATE

MEM_HEADS = 4
MEM_HEAD_DIM = 256
MEM_WIDTH = MEM_HEADS * MEM_HEAD_DIM

N_BRANCHES = 3
D_FF = 4 * D_MODEL

IN_SIZES = (3 * SB_WIDTH, SSD_INNER, SSD_CONV_DIM, SSD_HEADS, MEM_WIDTH, N_BRANCHES * D_MODEL)
D_IN_PROJ = int(sum(IN_SIZES))
SPLIT_POINTS = tuple(int(v) for v in np.cumsum(IN_SIZES)[:-1])

kernel_name = "hybrid_gated_stickbreak_ssd_memxattn_block"


def rms_norm(x, gain):
    xf = x.astype(jnp.float32)
    y = xf * lax.rsqrt(jnp.mean(xf * xf, axis=-1, keepdims=True) + EPS)
    return (y * gain.astype(jnp.float32)).astype(x.dtype)


def stick_breaking_attention(q, k, v):
    b, h, s, dh = q.shape
    nb = s // SB_BLOCK
    scale = dh ** -0.5
    qb = q.reshape(b, h, nb, SB_BLOCK, dh).transpose(2, 0, 1, 3, 4)
    key_pos = jnp.arange(s)

    def block(args):
        q_blk, blk_idx = args
        q_pos = blk_idx * SB_BLOCK + jnp.arange(SB_BLOCK)
        z = jnp.einsum("bhqd,bhkd->bhqk", q_blk, k).astype(jnp.float32) * scale
        causal = key_pos[None, :] < q_pos[:, None]
        log_beta = jax.nn.log_sigmoid(z)
        log_keep = jnp.where(causal, jax.nn.log_sigmoid(-z), 0.0)
        suffix = lax.cumsum(log_keep, axis=3, reverse=True) - log_keep
        w = jnp.where(causal, jnp.exp(log_beta + suffix), 0.0)
        return jnp.einsum("bhqk,bhkd->bhqd", w.astype(v.dtype), v)

    out = lax.map(block, (qb, jnp.arange(nb)))
    return out.transpose(1, 2, 0, 3, 4).reshape(b, h, s, dh)


def causal_depthwise_conv(x, w, bias):
    kw, c = w.shape
    y = lax.conv_general_dilated(
        x, w[:, None, :].astype(x.dtype), window_strides=(1,), padding=((kw - 1, 0),),
        dimension_numbers=("NWC", "WIO", "NWC"), feature_group_count=c)
    return y + bias.astype(x.dtype)


def ssd_chunked(xh, dt, a, bmat, cmat):
    b, s, g, r, p = xh.shape
    n = bmat.shape[-1]
    l = SSD_CHUNK
    c = s // l
    x = (xh * dt[..., None]).reshape(b, c, l, g, r, p)
    da = (dt.astype(jnp.float32) * a.astype(jnp.float32)).reshape(b, c, l, g, r)
    bm = bmat.reshape(b, c, l, g, n)
    cm = cmat.reshape(b, c, l, g, n)
    a_cs = jnp.cumsum(da, axis=2).transpose(0, 1, 3, 4, 2)

    tri = jnp.tril(jnp.ones((l, l), dtype=bool))
    seg = a_cs[..., :, None] - a_cs[..., None, :]
    decay = jnp.exp(jnp.where(tri, seg, -jnp.inf))
    cb = jnp.einsum("bclgn,bcsgn->bcgls", cm, bm).astype(jnp.float32)
    w_intra = cb[:, :, :, None] * decay
    y_diag = jnp.einsum("bcgrls,bcsgrp->bclgrp", w_intra, x.astype(jnp.float32))

    decay_to_end = jnp.exp(a_cs[..., -1:] - a_cs)
    states = jnp.einsum("bcsgn,bcgrs,bcsgrp->bcgrpn", bm.astype(jnp.float32), decay_to_end,
                        x.astype(jnp.float32))
    chunk_decay = jnp.exp(a_cs[..., -1])

    def step(hstate, inp):
        st, dec = inp
        return hstate * dec[..., None, None] + st, hstate

    h0 = jnp.zeros((b, g, r, p, n), jnp.float32)
    _, prev = lax.scan(step, h0, (states.transpose(1, 0, 2, 3, 4, 5), chunk_decay.transpose(1, 0, 2, 3)))
    prev = prev.transpose(1, 0, 2, 3, 4, 5)
    y_off = jnp.einsum("bclgn,bcgrpn,bcgrl->bclgrp", cm.astype(jnp.float32), prev, jnp.exp(a_cs))
    return (y_diag + y_off).reshape(b, s, g, r, p).astype(xh.dtype)


def ssd_branch(z, xbc_raw, dt_raw, conv_w, conv_b, dt_bias, a_log, d_skip, ssd_norm):
    b, s, _ = z.shape
    xbc = jax.nn.silu(causal_depthwise_conv(xbc_raw, conv_w, conv_b))
    xs, bmat, cmat = jnp.split(xbc, (SSD_INNER, SSD_INNER + SSD_GROUPS * SSD_STATE), axis=-1)
    xh = xs.reshape(b, s, SSD_GROUPS, SSD_HEADS_PER_GROUP, SSD_HEAD_DIM)
    bmat = bmat.reshape(b, s, SSD_GROUPS, SSD_STATE)
    cmat = cmat.reshape(b, s, SSD_GROUPS, SSD_STATE)
    dt = jax.nn.softplus((dt_raw + dt_bias).astype(jnp.float32)).astype(xs.dtype)
    dt = dt.reshape(b, s, SSD_GROUPS, SSD_HEADS_PER_GROUP)
    a = -jnp.exp(a_log.astype(jnp.float32)).reshape(SSD_GROUPS, SSD_HEADS_PER_GROUP)
    d = d_skip.reshape(SSD_GROUPS, SSD_HEADS_PER_GROUP)
    y = ssd_chunked(xh, dt, a, bmat, cmat) + d[..., None] * xh
    y = y.reshape(b, s, SSD_INNER) * jax.nn.silu(z)
    yg = y.reshape(b, s, SSD_GROUPS, SSD_INNER // SSD_GROUPS).astype(jnp.float32)
    yg = yg * lax.rsqrt(jnp.mean(yg * yg, axis=-1, keepdims=True) + EPS)
    return (yg.reshape(b, s, SSD_INNER) * ssd_norm.astype(jnp.float32)).astype(z.dtype)


def mem_cross_attention(q, mem, norm_mem, w_mem_kv):
    b, s, _ = q.shape
    m = mem.shape[1]
    kv = rms_norm(mem, norm_mem) @ w_mem_kv
    k, v = jnp.split(kv, 2, axis=-1)
    qh = q.reshape(b, s, MEM_HEADS, MEM_HEAD_DIM)
    kh = k.reshape(b, m, MEM_HEADS, MEM_HEAD_DIM)
    vh = v.reshape(b, m, MEM_HEADS, MEM_HEAD_DIM)
    scores = jnp.einsum("bshd,bmhd->bhsm", qh, kh).astype(jnp.float32) * (MEM_HEAD_DIM ** -0.5)
    probs = jax.nn.softmax(scores, axis=-1).astype(v.dtype)
    return jnp.einsum("bhsm,bmhd->bshd", probs, vh).reshape(b, s, MEM_WIDTH)


def _fwd_setup_inputs(seed: int = 0) -> dict:
    key = jax.random.key(seed)
    ks = jax.random.split(key, 24)
    nrm = jax.random.normal
    f32 = jnp.float32

    def gain(k, n):
        return 1.0 + 0.05 * nrm(k, (DEPTH, n), f32)

    dt0 = jnp.exp(jax.random.uniform(ks[6], (DEPTH, SSD_HEADS), f32, minval=np.log(1e-3), maxval=np.log(1e-1)))
    dt_bias = dt0 + jnp.log(-jnp.expm1(-dt0))
    return {
        "x": nrm(ks[0], (BATCH, SEQ, D_MODEL), f32),
        "mem": nrm(ks[1], (BATCH, MEM_LEN, D_MODEL), f32),
        "norm_mix_pre": gain(ks[2], D_MODEL),
        "w_in": nrm(ks[3], (DEPTH, D_MODEL, D_IN_PROJ), f32) * D_MODEL ** -0.5,
        "conv_w": nrm(ks[4], (DEPTH, SSD_CONV, SSD_CONV_DIM), f32) * SSD_CONV ** -0.5,
        "conv_b": 0.02 * nrm(ks[5], (DEPTH, SSD_CONV_DIM), f32),
        "dt_bias": dt_bias,
        "a_log": jnp.log(jax.random.uniform(ks[7], (DEPTH, SSD_HEADS), f32, minval=1.0, maxval=16.0)),
        "d_skip": 1.0 + 0.1 * nrm(ks[8], (DEPTH, SSD_HEADS), f32),
        "ssd_norm": gain(ks[9], SSD_INNER),
        "norm_mem": gain(ks[10], D_MODEL),
        "w_mem_kv": nrm(ks[11], (DEPTH, D_MODEL, 2 * MEM_WIDTH), f32) * D_MODEL ** -0.5,
        "w_sb_out": nrm(ks[12], (DEPTH, SB_WIDTH, D_MODEL), f32) * SB_WIDTH ** -0.5,
        "w_ssd_out": nrm(ks[13], (DEPTH, SSD_INNER, D_MODEL), f32) * SSD_INNER ** -0.5,
        "w_mem_out": nrm(ks[14], (DEPTH, MEM_WIDTH, D_MODEL), f32) * MEM_WIDTH ** -0.5,
        "w_o": nrm(ks[15], (DEPTH, D_MODEL, D_MODEL), f32) * D_MODEL ** -0.5,
        "norm_mix_post": gain(ks[16], D_MODEL),
        "norm_mlp_pre": gain(ks[17], D_MODEL),
        "w_up": nrm(ks[18], (DEPTH, D_MODEL, D_FF), f32) * D_MODEL ** -0.5,
        "w_down": nrm(ks[19], (DEPTH, D_FF, D_MODEL), f32) * D_FF ** -0.5,
        "norm_mlp_post": gain(ks[20], D_MODEL),
    }


def _fwd_reference(x, mem, norm_mix_pre, w_in, conv_w, conv_b, dt_bias, a_log, d_skip, ssd_norm,
              norm_mem, w_mem_kv, w_sb_out, w_ssd_out, w_mem_out, w_o, norm_mix_post,
              norm_mlp_pre, w_up, w_down, norm_mlp_post):
    b, s, _ = x.shape
    h = x
    for layer in range(DEPTH):
        u = rms_norm(h, norm_mix_pre[layer])
        proj = u @ w_in[layer]
        sb_qkv, z, xbc_raw, dt_raw, mem_q, gate_logits = jnp.split(proj, SPLIT_POINTS, axis=-1)

        q, k, v = jnp.split(sb_qkv, 3, axis=-1)
        to_heads = lambda t: t.reshape(b, s, SB_HEADS, SB_HEAD_DIM).transpose(0, 2, 1, 3)
        y_sb = stick_breaking_attention(to_heads(q), to_heads(k), to_heads(v))
        y_sb = y_sb.transpose(0, 2, 1, 3).reshape(b, s, SB_WIDTH)

        y_ssd = ssd_branch(z, xbc_raw, dt_raw, conv_w[layer], conv_b[layer], dt_bias[layer],
                           a_log[layer], d_skip[layer], ssd_norm[layer])

        y_mem = mem_cross_attention(mem_q, mem, norm_mem[layer], w_mem_kv[layer])

        gates = jax.nn.sigmoid(gate_logits.astype(jnp.float32)).astype(h.dtype)
        gates = gates.reshape(b, s, N_BRANCHES, D_MODEL)
        merged = (gates[:, :, 0] * (y_sb @ w_sb_out[layer])
                  + gates[:, :, 1] * (y_ssd @ w_ssd_out[layer])
                  + gates[:, :, 2] * (y_mem @ w_mem_out[layer]))
        mix = merged @ w_o[layer]
        h = h + rms_norm(mix, norm_mix_post[layer])

        u = rms_norm(h, norm_mlp_pre[layer])
        ff = jnp.square(jax.nn.relu(u @ w_up[layer])) @ w_down[layer]
        h = h + rms_norm(ff, norm_mlp_post[layer])
    return h


import jax as _jax
import jax.numpy as _jnp

TWIN_FORMAT = 'train_step'
FWD_PARAMS = ['x', 'mem', 'norm_mix_pre', 'w_in', 'conv_w', 'conv_b', 'dt_bias', 'a_log', 'd_skip', 'ssd_norm', 'norm_mem', 'w_mem_kv', 'w_sb_out', 'w_ssd_out', 'w_mem_out', 'w_o', 'norm_mix_post', 'norm_mlp_pre', 'w_up', 'w_down', 'norm_mlp_post']
TWIN_WEIGHTS = ['norm_mix_pre', 'w_in', 'conv_w', 'conv_b', 'dt_bias', 'a_log', 'd_skip', 'ssd_norm', 'norm_mem', 'w_mem_kv', 'w_sb_out', 'w_ssd_out', 'w_mem_out', 'w_o', 'norm_mix_post', 'norm_mlp_pre', 'w_up', 'w_down', 'norm_mlp_post']
TWIN_DIFF_INPUT = 'x'
TWIN_INPUTS = ['x', 'mem', 'norm_mix_pre', 'w_in', 'conv_w', 'conv_b', 'dt_bias', 'a_log', 'd_skip', 'ssd_norm', 'norm_mem', 'w_mem_kv', 'w_sb_out', 'w_ssd_out', 'w_mem_out', 'w_o', 'norm_mix_post', 'norm_mlp_pre', 'w_up', 'w_down', 'norm_mlp_post', 'loss_target', 'm_norm_mix_pre', 'm_w_in', 'm_conv_w', 'm_conv_b', 'm_dt_bias', 'm_a_log', 'm_d_skip', 'm_ssd_norm', 'm_norm_mem', 'm_w_mem_kv', 'm_w_sb_out', 'm_w_ssd_out', 'm_w_mem_out', 'm_w_o', 'm_norm_mix_post', 'm_norm_mlp_pre', 'm_w_up', 'm_w_down', 'm_norm_mlp_post', 'v_norm_mix_pre', 'v_w_in', 'v_conv_w', 'v_conv_b', 'v_dt_bias', 'v_a_log', 'v_d_skip', 'v_ssd_norm', 'v_norm_mem', 'v_w_mem_kv', 'v_w_sb_out', 'v_w_ssd_out', 'v_w_mem_out', 'v_w_o', 'v_norm_mix_post', 'v_norm_mlp_pre', 'v_w_up', 'v_w_down', 'v_norm_mlp_post']
TWIN_OUTPUTS = ['loss', 'grad_x', 'grad_norm_mix_pre', 'grad_w_in', 'grad_conv_w', 'grad_conv_b', 'grad_dt_bias', 'grad_a_log', 'grad_d_skip', 'grad_ssd_norm', 'grad_norm_mem', 'grad_w_mem_kv', 'grad_w_sb_out', 'grad_w_ssd_out', 'grad_w_mem_out', 'grad_w_o', 'grad_norm_mix_post', 'grad_norm_mlp_pre', 'grad_w_up', 'grad_w_down', 'grad_norm_mlp_post', 'delta_norm_mix_pre', 'delta_w_in', 'delta_conv_w', 'delta_conv_b', 'delta_dt_bias', 'delta_a_log', 'delta_d_skip', 'delta_ssd_norm', 'delta_norm_mem', 'delta_w_mem_kv', 'delta_w_sb_out', 'delta_w_ssd_out', 'delta_w_mem_out', 'delta_w_o', 'delta_norm_mix_post', 'delta_norm_mlp_pre', 'delta_w_up', 'delta_w_down', 'delta_norm_mlp_post', 'new_m_norm_mix_pre', 'new_m_w_in', 'new_m_conv_w', 'new_m_conv_b', 'new_m_dt_bias', 'new_m_a_log', 'new_m_d_skip', 'new_m_ssd_norm', 'new_m_norm_mem', 'new_m_w_mem_kv', 'new_m_w_sb_out', 'new_m_w_ssd_out', 'new_m_w_mem_out', 'new_m_w_o', 'new_m_norm_mix_post', 'new_m_norm_mlp_pre', 'new_m_w_up', 'new_m_w_down', 'new_m_norm_mlp_post', 'new_v_norm_mix_pre', 'new_v_w_in', 'new_v_conv_w', 'new_v_conv_b', 'new_v_dt_bias', 'new_v_a_log', 'new_v_d_skip', 'new_v_ssd_norm', 'new_v_norm_mem', 'new_v_w_mem_kv', 'new_v_w_sb_out', 'new_v_w_ssd_out', 'new_v_w_mem_out', 'new_v_w_o', 'new_v_norm_mix_post', 'new_v_norm_mlp_pre', 'new_v_w_up', 'new_v_w_down', 'new_v_norm_mlp_post']
TWIN_LEAF_KINDS = {'loss': 'loss', 'grad_x': 'grad_x', 'grad_norm_mix_pre': 'grad_w', 'grad_w_in': 'grad_w', 'grad_conv_w': 'grad_w', 'grad_conv_b': 'grad_w', 'grad_dt_bias': 'grad_w', 'grad_a_log': 'grad_w', 'grad_d_skip': 'grad_w', 'grad_ssd_norm': 'grad_w', 'grad_norm_mem': 'grad_w', 'grad_w_mem_kv': 'grad_w', 'grad_w_sb_out': 'grad_w', 'grad_w_ssd_out': 'grad_w', 'grad_w_mem_out': 'grad_w', 'grad_w_o': 'grad_w', 'grad_norm_mix_post': 'grad_w', 'grad_norm_mlp_pre': 'grad_w', 'grad_w_up': 'grad_w', 'grad_w_down': 'grad_w', 'grad_norm_mlp_post': 'grad_w', 'delta_norm_mix_pre': 'delta_w', 'delta_w_in': 'delta_w', 'delta_conv_w': 'delta_w', 'delta_conv_b': 'delta_w', 'delta_dt_bias': 'delta_w', 'delta_a_log': 'delta_w', 'delta_d_skip': 'delta_w', 'delta_ssd_norm': 'delta_w', 'delta_norm_mem': 'delta_w', 'delta_w_mem_kv': 'delta_w', 'delta_w_sb_out': 'delta_w', 'delta_w_ssd_out': 'delta_w', 'delta_w_mem_out': 'delta_w', 'delta_w_o': 'delta_w', 'delta_norm_mix_post': 'delta_w', 'delta_norm_mlp_pre': 'delta_w', 'delta_w_up': 'delta_w', 'delta_w_down': 'delta_w', 'delta_norm_mlp_post': 'delta_w', 'new_m_norm_mix_pre': 'new_m', 'new_m_w_in': 'new_m', 'new_m_conv_w': 'new_m', 'new_m_conv_b': 'new_m', 'new_m_dt_bias': 'new_m', 'new_m_a_log': 'new_m', 'new_m_d_skip': 'new_m', 'new_m_ssd_norm': 'new_m', 'new_m_norm_mem': 'new_m', 'new_m_w_mem_kv': 'new_m', 'new_m_w_sb_out': 'new_m', 'new_m_w_ssd_out': 'new_m', 'new_m_w_mem_out': 'new_m', 'new_m_w_o': 'new_m', 'new_m_norm_mix_post': 'new_m', 'new_m_norm_mlp_pre': 'new_m', 'new_m_w_up': 'new_m', 'new_m_w_down': 'new_m', 'new_m_norm_mlp_post': 'new_m', 'new_v_norm_mix_pre': 'new_v', 'new_v_w_in': 'new_v', 'new_v_conv_w': 'new_v', 'new_v_conv_b': 'new_v', 'new_v_dt_bias': 'new_v', 'new_v_a_log': 'new_v', 'new_v_d_skip': 'new_v', 'new_v_ssd_norm': 'new_v', 'new_v_norm_mem': 'new_v', 'new_v_w_mem_kv': 'new_v', 'new_v_w_sb_out': 'new_v', 'new_v_w_ssd_out': 'new_v', 'new_v_w_mem_out': 'new_v', 'new_v_w_o': 'new_v', 'new_v_norm_mix_post': 'new_v', 'new_v_norm_mlp_pre': 'new_v', 'new_v_w_up': 'new_v', 'new_v_w_down': 'new_v', 'new_v_norm_mlp_post': 'new_v'}


def _forward(args):
    return _fwd_reference(*[args[k] for k in FWD_PARAMS])


def _output_shape():
    def fwd():
        inp = _fwd_setup_inputs(0)
        return _fwd_reference(*[inp[k] for k in FWD_PARAMS])
    out = _jax.eval_shape(fwd)
    return out.shape, out.dtype

N_MICROBATCH = 1
ADAM_LR = 0.001
ADAM_B1 = 0.9
ADAM_B2 = 0.999
ADAM_EPS = 1e-08
ADAM_WD = 0.01
ADAM_STEP = 10
PER_EXAMPLE_BATCH_AXIS = {'x': 0, 'mem': 0, 'loss_target': 0}
SHARED_INPUTS = []
_WEIGHT_DTYPES = {'norm_mix_pre': _jnp.float32, 'w_in': _jnp.float32, 'conv_w': _jnp.float32, 'conv_b': _jnp.float32, 'dt_bias': _jnp.float32, 'a_log': _jnp.float32, 'd_skip': _jnp.float32, 'ssd_norm': _jnp.float32, 'norm_mem': _jnp.float32, 'w_mem_kv': _jnp.float32, 'w_sb_out': _jnp.float32, 'w_ssd_out': _jnp.float32, 'w_mem_out': _jnp.float32, 'w_o': _jnp.float32, 'norm_mix_post': _jnp.float32, 'norm_mlp_pre': _jnp.float32, 'w_up': _jnp.float32, 'w_down': _jnp.float32, 'norm_mlp_post': _jnp.float32}
MOMENT_SCALE = {'norm_mix_pre': 7.422693e-01, 'w_in': 2.008346e-01, 'conv_w': 7.914909e-01, 'conv_b': 2.675172e+00, 'dt_bias': 6.324931e-01, 'a_log': 3.960616e+00, 'd_skip': 4.373997e+00, 'ssd_norm': 1.484238e+00, 'norm_mem': 2.097454e-01, 'w_mem_kv': 1.463560e-01, 'w_sb_out': 2.684503e-01, 'w_ssd_out': 2.144633e+00, 'w_mem_out': 2.197192e-01, 'w_o': 2.216379e+00, 'norm_mix_post': 3.223175e+01, 'norm_mlp_pre': 1.066959e+00, 'w_up': 5.340385e-01, 'w_down': 2.390174e+00, 'norm_mlp_post': 3.313018e+01}


def _to_microbatches(a, axis):
    t = _jnp.moveaxis(a, axis, 0)
    t = t.reshape((N_MICROBATCH, t.shape[0] // N_MICROBATCH) + t.shape[1:])
    return _jnp.moveaxis(t, 1, axis + 1)


def setup_inputs(seed: int = 0) -> dict:
    inp = _fwd_setup_inputs(seed)
    key = _jax.random.fold_in(_jax.random.key(seed), 7919)
    shape, _ = _output_shape()
    out = dict(inp)
    out["loss_target"] = _jax.random.normal(_jax.random.fold_in(key, 0), shape, _jnp.float32)
    for i, name in enumerate(TWIN_WEIGHTS):
        w = inp[name].astype(_jnp.float32)
        if MOMENT_SCALE is None:
            s = _jnp.sqrt(_jnp.mean(_jnp.square(w)) + 1e-30)
        else:
            s = MOMENT_SCALE[name]
        km, kv = _jax.random.split(_jax.random.fold_in(key, i + 1))
        out[name] = w
        out["m_" + name] = s * _jax.random.normal(km, w.shape, _jnp.float32)
        out["v_" + name] = (s * s) * _jax.random.uniform(kv, w.shape, _jnp.float32, 0.5, 1.5)
    if N_MICROBATCH > 1:
        for name, axis in PER_EXAMPLE_BATCH_AXIS.items():
            out[name] = _to_microbatches(out[name], axis)
    return {'x': out['x'], 'mem': out['mem'], 'norm_mix_pre': out['norm_mix_pre'], 'w_in': out['w_in'], 'conv_w': out['conv_w'], 'conv_b': out['conv_b'], 'dt_bias': out['dt_bias'], 'a_log': out['a_log'], 'd_skip': out['d_skip'], 'ssd_norm': out['ssd_norm'], 'norm_mem': out['norm_mem'], 'w_mem_kv': out['w_mem_kv'], 'w_sb_out': out['w_sb_out'], 'w_ssd_out': out['w_ssd_out'], 'w_mem_out': out['w_mem_out'], 'w_o': out['w_o'], 'norm_mix_post': out['norm_mix_post'], 'norm_mlp_pre': out['norm_mlp_pre'], 'w_up': out['w_up'], 'w_down': out['w_down'], 'norm_mlp_post': out['norm_mlp_post'], 'loss_target': out['loss_target'], 'm_norm_mix_pre': out['m_norm_mix_pre'], 'm_w_in': out['m_w_in'], 'm_conv_w': out['m_conv_w'], 'm_conv_b': out['m_conv_b'], 'm_dt_bias': out['m_dt_bias'], 'm_a_log': out['m_a_log'], 'm_d_skip': out['m_d_skip'], 'm_ssd_norm': out['m_ssd_norm'], 'm_norm_mem': out['m_norm_mem'], 'm_w_mem_kv': out['m_w_mem_kv'], 'm_w_sb_out': out['m_w_sb_out'], 'm_w_ssd_out': out['m_w_ssd_out'], 'm_w_mem_out': out['m_w_mem_out'], 'm_w_o': out['m_w_o'], 'm_norm_mix_post': out['m_norm_mix_post'], 'm_norm_mlp_pre': out['m_norm_mlp_pre'], 'm_w_up': out['m_w_up'], 'm_w_down': out['m_w_down'], 'm_norm_mlp_post': out['m_norm_mlp_post'], 'v_norm_mix_pre': out['v_norm_mix_pre'], 'v_w_in': out['v_w_in'], 'v_conv_w': out['v_conv_w'], 'v_conv_b': out['v_conv_b'], 'v_dt_bias': out['v_dt_bias'], 'v_a_log': out['v_a_log'], 'v_d_skip': out['v_d_skip'], 'v_ssd_norm': out['v_ssd_norm'], 'v_norm_mem': out['v_norm_mem'], 'v_w_mem_kv': out['v_w_mem_kv'], 'v_w_sb_out': out['v_w_sb_out'], 'v_w_ssd_out': out['v_w_ssd_out'], 'v_w_mem_out': out['v_w_mem_out'], 'v_w_o': out['v_w_o'], 'v_norm_mix_post': out['v_norm_mix_post'], 'v_norm_mlp_pre': out['v_norm_mlp_pre'], 'v_w_up': out['v_w_up'], 'v_w_down': out['v_w_down'], 'v_norm_mlp_post': out['v_norm_mlp_post']}


def _loss(weights, diff, rest, loss_target):
    with _jax.named_scope("forward"):
        args = {**rest, TWIN_DIFF_INPUT: diff, **{k: w.astype(_WEIGHT_DTYPES[k]) for k, w in weights.items()}}
        y = _forward(args)
    with _jax.named_scope("loss_head"):
        err = _jnp.square(y.astype(_jnp.float32) - loss_target)
        return 0.5 * _jnp.sum(_jnp.mean(err, axis=-1)) if err.ndim else 0.5 * err


def _adamw(w, g, m, v):
    m = ADAM_B1 * m + (1.0 - ADAM_B1) * g
    v = ADAM_B2 * v + (1.0 - ADAM_B2) * _jnp.square(g)
    m_hat = m / (1.0 - ADAM_B1 ** ADAM_STEP)
    v_hat = v / (1.0 - ADAM_B2 ** ADAM_STEP)
    delta = -ADAM_LR * (m_hat / (_jnp.sqrt(v_hat) + ADAM_EPS) + ADAM_WD * w)
    return delta, m, v


def reference(x, mem, norm_mix_pre, w_in, conv_w, conv_b, dt_bias, a_log, d_skip, ssd_norm, norm_mem, w_mem_kv, w_sb_out, w_ssd_out, w_mem_out, w_o, norm_mix_post, norm_mlp_pre, w_up, w_down, norm_mlp_post, loss_target, m_norm_mix_pre, m_w_in, m_conv_w, m_conv_b, m_dt_bias, m_a_log, m_d_skip, m_ssd_norm, m_norm_mem, m_w_mem_kv, m_w_sb_out, m_w_ssd_out, m_w_mem_out, m_w_o, m_norm_mix_post, m_norm_mlp_pre, m_w_up, m_w_down, m_norm_mlp_post, v_norm_mix_pre, v_w_in, v_conv_w, v_conv_b, v_dt_bias, v_a_log, v_d_skip, v_ssd_norm, v_norm_mem, v_w_mem_kv, v_w_sb_out, v_w_ssd_out, v_w_mem_out, v_w_o, v_norm_mix_post, v_norm_mlp_pre, v_w_up, v_w_down, v_norm_mlp_post):
    given = dict(x=x, mem=mem, norm_mix_pre=norm_mix_pre, w_in=w_in, conv_w=conv_w, conv_b=conv_b, dt_bias=dt_bias, a_log=a_log, d_skip=d_skip, ssd_norm=ssd_norm, norm_mem=norm_mem, w_mem_kv=w_mem_kv, w_sb_out=w_sb_out, w_ssd_out=w_ssd_out, w_mem_out=w_mem_out, w_o=w_o, norm_mix_post=norm_mix_post, norm_mlp_pre=norm_mlp_pre, w_up=w_up, w_down=w_down, norm_mlp_post=norm_mlp_post, loss_target=loss_target, m_norm_mix_pre=m_norm_mix_pre, m_w_in=m_w_in, m_conv_w=m_conv_w, m_conv_b=m_conv_b, m_dt_bias=m_dt_bias, m_a_log=m_a_log, m_d_skip=m_d_skip, m_ssd_norm=m_ssd_norm, m_norm_mem=m_norm_mem, m_w_mem_kv=m_w_mem_kv, m_w_sb_out=m_w_sb_out, m_w_ssd_out=m_w_ssd_out, m_w_mem_out=m_w_mem_out, m_w_o=m_w_o, m_norm_mix_post=m_norm_mix_post, m_norm_mlp_pre=m_norm_mlp_pre, m_w_up=m_w_up, m_w_down=m_w_down, m_norm_mlp_post=m_norm_mlp_post, v_norm_mix_pre=v_norm_mix_pre, v_w_in=v_w_in, v_conv_w=v_conv_w, v_conv_b=v_conv_b, v_dt_bias=v_dt_bias, v_a_log=v_a_log, v_d_skip=v_d_skip, v_ssd_norm=v_ssd_norm, v_norm_mem=v_norm_mem, v_w_mem_kv=v_w_mem_kv, v_w_sb_out=v_w_sb_out, v_w_ssd_out=v_w_ssd_out, v_w_mem_out=v_w_mem_out, v_w_o=v_w_o, v_norm_mix_post=v_norm_mix_post, v_norm_mlp_pre=v_norm_mlp_pre, v_w_up=v_w_up, v_w_down=v_w_down, v_norm_mlp_post=v_norm_mlp_post)
    weights = {n: given[n] for n in TWIN_WEIGHTS}
    shared = {n: given[n] for n in SHARED_INPUTS}
    per_example = {n: given[n] for n in ['x', 'mem']}
    grad_fn = _jax.value_and_grad(_loss, argnums=(0, 1))

    def one_microbatch(ex, loss_target):
        ex = dict(ex)
        diff = ex.pop(TWIN_DIFF_INPUT)
        return grad_fn(weights, diff, {**shared, **ex}, loss_target)

    if N_MICROBATCH == 1:
        loss, (grad_w, grad_x) = one_microbatch(per_example, given["loss_target"])
    else:
        def body(carry, xs):
            loss_sum, grad_sum = carry
            l_k, (gw_k, gx_k) = one_microbatch(xs[0], xs[1])
            with _jax.named_scope("update"):
                return (loss_sum + l_k, _jax.tree.map(_jnp.add, grad_sum, gw_k)), gx_k

        init = (_jnp.zeros((), _jnp.float32), _jax.tree.map(_jnp.zeros_like, weights))
        (loss, grad_w), grad_x = _jax.lax.scan(body, init, (per_example, given["loss_target"]))
    with _jax.named_scope("update"):
        delta_w, new_m, new_v = {}, {}, {}
        for n in TWIN_WEIGHTS:
            delta_w[n], new_m[n], new_v[n] = _adamw(weights[n], grad_w[n], given["m_" + n], given["v_" + n])
    return (loss, grad_x, *[grad_w[n] for n in TWIN_WEIGHTS], *[delta_w[n] for n in TWIN_WEIGHTS],
            *[new_m[n] for n in TWIN_WEIGHTS], *[new_v[n] for n in TWIN_WEIGHTS])
```

```python
import functools

import jax
import jax.numpy as jnp
from jax import lax
from jax.experimental import pallas as pl
from jax.experimental.pallas import tpu as pltpu

F32 = jnp.float32
BF16 = jnp.bfloat16
CDT = jnp.bfloat16
EPS = 1e-6
VMEM_LIMIT = 56 * 1024 * 1024

N_DEV = 8
D_MODEL = 1024
SB_H, SB_DH = 16, 64
SSD_G, SSD_R, SSD_P, SSD_N, SSD_L = 4, 8, 64, 128, 128
SSD_H = SSD_G * SSD_R
SSD_INNER = SSD_H * SSD_P
CONV_K = 4
CONV_DIM = SSD_INNER + 2 * SSD_G * SSD_N
MEM_H, MEM_DH = 4, 256
DT_PAD = 128
SB_TQ, SB_BK = 256, 128
CONV_PAD = 8

ADAM_LR, ADAM_B1, ADAM_B2, ADAM_EPS, ADAM_WD, ADAM_STEP = 0.001, 0.9, 0.999, 1e-08, 0.01, 10

NT = (((1,), (1,)), ((), ()))
TN = (((0,), (0,)), ((), ()))
NN = (((1,), (0,)), ((), ()))


def _cparams(sem=None):
    return pltpu.CompilerParams(dimension_semantics=sem, vmem_limit_bytes=VMEM_LIMIT)


def _pick(n, cands):
    for c in cands:
        if n % c == 0:
            return c
    return n


def _dot(a, b, dims=NN):
    return lax.dot_general(a.astype(CDT), b.astype(CDT), dims, preferred_element_type=F32)


def _split_dot(x, t, left, pieces):
    if CDT == F32:
        return lax.dot_general(t, x, NN, preferred_element_type=F32) if left else \
            lax.dot_general(x, t, NN, preferred_element_type=F32)
    acc = None
    rem = x
    for _ in range(pieces):
        hi = rem.astype(BF16)
        rem = rem - hi.astype(F32)
        d = lax.dot_general(t, hi, NN, preferred_element_type=F32) if left else \
            lax.dot_general(hi, t, NN, preferred_element_type=F32)
        acc = d if acc is None else acc + d
    return acc


def _iota(shape, dim):
    return lax.broadcasted_iota(jnp.int32, shape, dim)


def _sigmoid(x):
    return 1.0 / (1.0 + jnp.exp(-x))


def _silu(x):
    return x * _sigmoid(x)


def _dsilu(x):
    s = _sigmoid(x)
    return s * (1.0 + x * (1.0 - s))


def _softplus(x):
    return jnp.maximum(x, 0.0) + jnp.log(1.0 + jnp.exp(-jnp.abs(x)))


def _rms(x, g):
    r = lax.rsqrt(jnp.mean(x * x, axis=-1, keepdims=True) + EPS)
    return x * r * g


def _rms_bwd(x, g, dy):
    r = lax.rsqrt(jnp.mean(x * x, axis=-1, keepdims=True) + EPS)
    n = x * r
    dn = dy * g
    dx = r * (dn - n * jnp.mean(dn * n, axis=-1, keepdims=True))
    dg = jnp.sum(dy * n, axis=0, keepdims=True)
    return dx, dg


def _mm(a, b, *, ta=False, tb=False, out_dtype=F32, name):
    m = a.shape[1] if ta else a.shape[0]
    k = a.shape[0] if ta else a.shape[1]
    n = b.shape[0] if tb else b.shape[1]
    assert k == (b.shape[1] if tb else b.shape[0])
    bm = _pick(m, (512, 256, 128))
    bn = _pick(n, (512, 256, 128))
    bk = _pick(k, (512, 256, 128))
    nk = k // bk
    dims = (((0 if ta else 1,), (1 if tb else 0,)), ((), ()))

    def body(a_ref, b_ref, o_ref, acc_ref):
        kk = pl.program_id(2)

        @pl.when(kk == 0)
        def _():
            acc_ref[...] = jnp.zeros_like(acc_ref)

        acc_ref[...] += _dot(a_ref[...], b_ref[...], dims)

        @pl.when(kk == nk - 1)
        def _():
            o_ref[...] = acc_ref[...].astype(o_ref.dtype)

    a_spec = pl.BlockSpec((bk, bm), lambda i, j, kk: (kk, i)) if ta else \
        pl.BlockSpec((bm, bk), lambda i, j, kk: (i, kk))
    b_spec = pl.BlockSpec((bn, bk), lambda i, j, kk: (j, kk)) if tb else \
        pl.BlockSpec((bk, bn), lambda i, j, kk: (kk, j))
    return pl.pallas_call(
        body, name=name,
        grid=(m // bm, n // bn, nk),
        in_specs=[a_spec, b_spec],
        out_specs=pl.BlockSpec((bm, bn), lambda i, j, kk: (i, j)),
        out_shape=jax.ShapeDtypeStruct((m, n), out_dtype),
        scratch_shapes=[pltpu.VMEM((bm, bn), F32)],
        compiler_params=_cparams(("parallel", "parallel", "arbitrary")),
    )(a, b)


def _rows(fn, tiled, params, outs, accs=(), *, ts, name):
    s = tiled[0].shape[0]
    ts = min(ts, s)
    assert s % ts == 0
    nt, npar, no, na = len(tiled), len(params), len(outs), len(accs)

    def body(*refs):
        i = pl.program_id(0)
        vals = [r[...] for r in refs[:nt + npar]]
        res = fn(*vals)
        if not isinstance(res, (tuple, list)):
            res = (res,)
        orefs = refs[nt + npar:nt + npar + no]
        arefs = refs[nt + npar + no:]
        for r_, val in zip(orefs, res[:no]):
            r_[...] = val.astype(r_.dtype)
        if na:
            @pl.when(i == 0)
            def _():
                for r_ in arefs:
                    r_[...] = jnp.zeros_like(r_)

            for r_, val in zip(arefs, res[no:]):
                r_[...] += jnp.broadcast_to(val, r_.shape)

    in_specs = [pl.BlockSpec((ts, a.shape[1]), lambda i: (i, 0)) for a in tiled]
    in_specs += [pl.BlockSpec(p.shape, lambda i: (0, 0)) for p in params]
    out_specs = [pl.BlockSpec((ts, w), lambda i: (i, 0)) for (w, _) in outs]
    out_specs += [pl.BlockSpec(shape, lambda i: (0, 0)) for shape in accs]
    out_shape = [jax.ShapeDtypeStruct((s, w), dt) for (w, dt) in outs]
    out_shape += [jax.ShapeDtypeStruct(shape, F32) for shape in accs]
    res = pl.pallas_call(
        body, name=name, grid=(s // ts,),
        in_specs=in_specs, out_specs=out_specs, out_shape=out_shape,
        compiler_params=_cparams(("arbitrary",)),
    )(*tiled, *params)
    return res


def _sb_scores(q, kb, k0, q0, scale):
    tq, bk = q.shape[0], kb.shape[0]
    z = _dot(q, kb, NT) * scale
    causal = (k0 + _iota((tq, bk), 1)) < (q0 + _iota((tq, bk), 0))
    lb = jnp.minimum(z, 0.0) - jnp.log(1.0 + jnp.exp(-jnp.abs(z)))
    lk = jnp.where(causal, lb - z, 0.0)
    return z, causal, lb, lk


def _sb_fwd(q, k, v):
    h, s, dh = q.shape
    tq, bk = min(SB_TQ, s), SB_BK
    scale = dh ** -0.5

    def body(q_ref, k_ref, v_ref, o_ref, lt_ref):
        i = pl.program_id(1)
        q0 = i * tq
        qv = q_ref[...]
        tri = (_iota((bk, bk), 0) > _iota((bk, bk), 1)).astype(CDT)
        nkb = (q0 + tq) // bk

        def step(jj, carry):
            cf, acc = carry
            k0 = pl.multiple_of((nkb - 1 - jj) * bk, bk)
            kb = k_ref[pl.ds(k0, bk), :]
            vb = v_ref[pl.ds(k0, bk), :]
            _, causal, lb, lk = _sb_scores(qv, kb, k0, q0, scale)
            f = cf + _split_dot(lk, tri, False, 2)
            w = jnp.where(causal, jnp.exp(lb + f), 0.0)
            acc = acc + _dot(w, vb)
            cf = cf + jnp.sum(lk, axis=1, keepdims=True)
            return cf, acc

        cf, acc = lax.fori_loop(0, nkb, step, (jnp.zeros((tq, 1), F32), jnp.zeros((tq, dh), F32)))
        o_ref[...] = acc
        lt_ref[...] = cf

    return pl.pallas_call(
        body, name="sb_fwd", grid=(h, s // tq),
        in_specs=[pl.BlockSpec((None, tq, dh), lambda a, i: (a, i, 0)),
                  pl.BlockSpec((None, s, dh), lambda a, i: (a, 0, 0)),
                  pl.BlockSpec((None, s, dh), lambda a, i: (a, 0, 0))],
        out_specs=[pl.BlockSpec((None, tq, dh), lambda a, i: (a, i, 0)),
                   pl.BlockSpec((None, tq, 1), lambda a, i: (a, i, 0))],
        out_shape=[jax.ShapeDtypeStruct((h, s, dh), F32), jax.ShapeDtypeStruct((h, s, 1), F32)],
        compiler_params=_cparams(("parallel", "arbitrary")),
    )(q, k, v)


def _sb_bwd(q, k, v, ltot, do):
    h, s, dh = q.shape
    tq, bk = min(SB_TQ, s), SB_BK
    scale = dh ** -0.5

    def body(q_ref, k_ref, v_ref, lt_ref, do_ref, dq_ref, dk_ref, dv_ref):
        i = pl.program_id(1)

        @pl.when(i == 0)
        def _():
            dk_ref[...] = jnp.zeros_like(dk_ref)
            dv_ref[...] = jnp.zeros_like(dv_ref)

        q0 = i * tq
        qv = q_ref[...]
        dov = do_ref[...]
        ltot = lt_ref[...]
        tri_le = (_iota((bk, bk), 0) <= _iota((bk, bk), 1)).astype(CDT)
        tri_lt = (_iota((bk, bk), 0) < _iota((bk, bk), 1)).astype(CDT)
        nkb = (q0 + tq) // bk

        def step(jj, carry):
            cf, cg, dq = carry
            k0 = pl.multiple_of(jj * bk, bk)
            kb = k_ref[pl.ds(k0, bk), :]
            vb = v_ref[pl.ds(k0, bk), :]
            _, causal, lb, lk = _sb_scores(qv, kb, k0, q0, scale)
            f = ltot - (cf + _split_dot(lk, tri_le, False, 2))
            w = jnp.where(causal, jnp.exp(lb + f), 0.0)
            g = w * _dot(dov, vb, NT)
            gsum = cg + _split_dot(g, tri_lt, False, 2)
            beta = jnp.exp(lb)
            dz = jnp.where(causal, g * (1.0 - beta) - gsum * beta, 0.0) * scale
            dq = dq + _dot(dz, kb)
            dk_ref[pl.ds(k0, bk), :] += _dot(dz, qv, TN)
            dv_ref[pl.ds(k0, bk), :] += _dot(w, dov, TN)
            cf = cf + jnp.sum(lk, axis=1, keepdims=True)
            cg = cg + jnp.sum(g, axis=1, keepdims=True)
            return cf, cg, dq

        init = (jnp.zeros((tq, 1), F32), jnp.zeros((tq, 1), F32), jnp.zeros((tq, dh), F32))
        _, _, dq = lax.fori_loop(0, nkb, step, init)
        dq_ref[...] = dq

    tile = pl.BlockSpec((None, tq, dh), lambda a, i: (a, i, 0))
    full = pl.BlockSpec((None, s, dh), lambda a, i: (a, 0, 0))
    shp = jax.ShapeDtypeStruct((h, s, dh), F32)
    return pl.pallas_call(
        body, name="sb_bwd", grid=(h, s // tq),
        in_specs=[tile, full, full, pl.BlockSpec((None, tq, 1), lambda a, i: (a, i, 0)), tile],
        out_specs=[tile, full, full],
        out_shape=[shp, shp, shp],
        compiler_params=_cparams(("parallel", "arbitrary")),
    )(q, k, v, ltot, do)


def _pick_lane(tile, r):
    return jnp.sum(jnp.where(_iota(tile.shape, 1) == r, tile, 0.0), axis=1, keepdims=True)


def _pick_row(tile, r):
    return jnp.sum(jnp.where(_iota(tile.shape, 0) == r, tile, 0.0), axis=0, keepdims=True)


def _ssd_chunk_setup(c_ref, b_ref, dac_ref, dar_ref, cb_ref, acsc_ref, acsr_ref):
    l = SSD_L
    tdt = F32 if CDT == F32 else BF16
    lower = (_iota((l, l), 1) <= _iota((l, l), 0)).astype(tdt)
    upper = (_iota((l, l), 0) <= _iota((l, l), 1)).astype(tdt)
    cb_ref[...] = _dot(c_ref[...], b_ref[...], NT)
    acsc_ref[...] = _split_dot(dac_ref[...], lower, True, 3)
    acsr_ref[...] = _split_dot(dar_ref[...], upper, False, 3)


def _ssd_fwd(xh, dtc, dac, dar, dsk, xbc):
    hh, s, p = xh.shape
    l, n, g_, r_ = SSD_L, SSD_N, SSD_G, SSD_R
    nc = s // l
    boff = SSD_INNER // n
    coff = boff + g_

    def body(x_ref, dtc_ref, dac_ref, dar_ref, dsk_ref, b_ref, c_ref, y_ref, st_ref,
             state_ref, cb_ref, acsc_ref, acsr_ref):
        c = pl.program_id(1)
        r = pl.program_id(2)

        @pl.when(r == 0)
        def _():
            _ssd_chunk_setup(c_ref, b_ref, dac_ref, dar_ref, cb_ref, acsc_ref, acsr_ref)

        @pl.when(c == 0)
        def _():
            state_ref[r] = jnp.zeros((n, p), F32)

        a_col = _pick_lane(acsc_ref[...], r)
        a_row = _pick_row(acsr_ref[...], r)
        dt_col = _pick_lane(dtc_ref[...], r)
        dsk_h = _pick_lane(dsk_ref[...], r)
        xv = x_ref[...]
        xd = xv * dt_col
        mask = _iota((l, l), 1) <= _iota((l, l), 0)
        decay = jnp.where(mask, jnp.exp(jnp.minimum(a_col - a_row, 0.0)), 0.0)
        w = cb_ref[...] * decay
        hprev = state_ref[r]
        cv = c_ref[...]
        y = _dot(w, xd) + jnp.exp(a_col) * _dot(cv, hprev)
        y_ref[...] = y + dsk_h * xv
        a_end = a_col[l - 1:l, :]
        dte = jnp.exp(a_end - a_col)
        st_ref[...] = hprev
        state_ref[r] = hprev * jnp.exp(a_end) + _dot(b_ref[...], xd * dte, TN)

    return pl.pallas_call(
        body, name="ssd_fwd", grid=(g_, nc, r_),
        in_specs=[pl.BlockSpec((None, l, p), lambda g, c, r: (g * r_ + r, c, 0)),
                  pl.BlockSpec((None, l, r_), lambda g, c, r: (g, c, 0)),
                  pl.BlockSpec((None, l, r_), lambda g, c, r: (g, c, 0)),
                  pl.BlockSpec((None, r_, l), lambda g, c, r: (g, 0, c)),
                  pl.BlockSpec((None, 1, r_), lambda g, c, r: (g, 0, 0)),
                  pl.BlockSpec((l, n), lambda g, c, r: (c, boff + g)),
                  pl.BlockSpec((l, n), lambda g, c, r: (c, coff + g))],
        out_specs=[pl.BlockSpec((None, l, p), lambda g, c, r: (g * r_ + r, c, 0)),
                   pl.BlockSpec((None, None, n, p), lambda g, c, r: (g * r_ + r, c, 0, 0))],
        out_shape=[jax.ShapeDtypeStruct((hh, s, p), F32),
                   jax.ShapeDtypeStruct((hh, nc, n, p), F32)],
        scratch_shapes=[pltpu.VMEM((r_, n, p), F32), pltpu.VMEM((l, l), F32),
                        pltpu.VMEM((l, r_), F32), pltpu.VMEM((r_, l), F32)],
        compiler_params=_cparams(("parallel", "arbitrary", "arbitrary")),
    )(xh, dtc, dac, dar, dsk, xbc, xbc)


def _ssd_bwd(xh, dtc, dac, dar, dsk, xbc, st, dy):
    hh, s, p = xh.shape
    l, n, g_, r_ = SSD_L, SSD_N, SSD_G, SSD_R
    nc = s // l
    boff = SSD_INNER // n
    coff = boff + g_

    def body(x_ref, dtc_ref, dac_ref, dar_ref, dsk_ref, b_ref, c_ref, st_ref, dy_ref,
             dx_ref, dda_ref, dxx_ref, db_ref, dc_ref, dah_ref, ddsk_ref,
             dstate_ref, cb_ref, acsc_ref, acsr_ref):
        c = pl.program_id(1)
        r = pl.program_id(2)

        @pl.when(r == 0)
        def _():
            _ssd_chunk_setup(c_ref, b_ref, dac_ref, dar_ref, cb_ref, acsc_ref, acsr_ref)
            db_ref[...] = jnp.zeros_like(db_ref)
            dc_ref[...] = jnp.zeros_like(dc_ref)
            dda_ref[...] = jnp.zeros_like(dda_ref)
            dxx_ref[...] = jnp.zeros_like(dxx_ref)

        @pl.when(jnp.logical_and(c == 0, r == 0))
        def _():
            dah_ref[...] = jnp.zeros_like(dah_ref)
            ddsk_ref[...] = jnp.zeros_like(ddsk_ref)

        @pl.when(c == 0)
        def _():
            dstate_ref[r] = jnp.zeros((n, p), F32)

        a_col = _pick_lane(acsc_ref[...], r)
        a_row = _pick_row(acsr_ref[...], r)
        dt_col = _pick_lane(dtc_ref[...], r)
        dsk_h = _pick_lane(dsk_ref[...], r)
        xv = x_ref[...]
        dyv = dy_ref[...]
        xd = xv * dt_col
        il = _iota((l, l), 0)
        isx = _iota((l, l), 1)
        decay = jnp.where(isx <= il, jnp.exp(jnp.minimum(a_col - a_row, 0.0)), 0.0)
        cb = cb_ref[...]
        w = cb * decay
        dhn = dstate_ref[r]
        hc = st_ref[...]
        bv = b_ref[...]
        cv = c_ref[...]
        a_end = a_col[l - 1:l, :]
        ea = jnp.exp(a_col)
        dte = jnp.exp(a_end - a_col)

        dx_state = dte * _dot(bv, dhn)
        dxd = _dot(w, dyv, TN) + dx_state
        md = decay * _dot(dyv, xd, NT)
        m = md * cb
        dc_ref[...] += _dot(md, bv) + ea * _dot(dyv, hc, NT)
        db_ref[...] += _dot(md, cv, TN) + dte * _dot(xd, dhn, NT)
        dstate_ref[r] = jnp.exp(a_end) * dhn + _dot(cv, dyv * ea, TN)

        tdt = F32 if CDT == F32 else BF16
        t1 = (isx >= il).astype(tdt)
        yoff = ea * _dot(cv, hc)
        xdx = jnp.sum(xd * dx_state, axis=1, keepdims=True)
        vec = jnp.sum(dyv * yoff, axis=1, keepdims=True) - xdx
        end_term = jnp.sum(xdx, axis=0, keepdims=True) + \
            jnp.exp(a_end) * jnp.sum(jnp.sum(hc * dhn, axis=1, keepdims=True), axis=0, keepdims=True)
        zmat = _split_dot(m, t1, True, 2)
        span = jnp.sum(jnp.where(isx < il, zmat, 0.0), axis=1, keepdims=True)
        rc = _split_dot(jnp.broadcast_to(vec, (l, 128)), t1, True, 2)[:, :1]
        dda = span + rc + end_term
        dxx = jnp.sum(dxd * xv, axis=1, keepdims=True)

        lane = _iota((l, r_), 1) == r
        dda_ref[...] += jnp.where(lane, dda, 0.0)
        dxx_ref[...] += jnp.where(lane, dxx, 0.0)
        dx_ref[...] = dxd * dt_col + dsk_h * dyv
        lane1 = _iota((1, r_), 1) == r
        dah_ref[...] += jnp.where(lane1, jnp.sum(dda * dt_col, axis=0, keepdims=True), 0.0)
        ddsk_ref[...] += jnp.where(
            lane1, jnp.sum(jnp.sum(dyv * xv, axis=1, keepdims=True), axis=0, keepdims=True), 0.0)

    rev = lambda c: nc - 1 - c
    xspec = pl.BlockSpec((None, l, p), lambda g, c, r: (g * r_ + r, rev(c), 0))
    cspec = pl.BlockSpec((None, l, r_), lambda g, c, r: (g, rev(c), 0))
    hspec = pl.BlockSpec((None, 1, r_), lambda g, c, r: (g, 0, 0))
    return pl.pallas_call(
        body, name="ssd_bwd", grid=(g_, nc, r_),
        in_specs=[xspec, cspec, cspec,
                  pl.BlockSpec((None, r_, l), lambda g, c, r: (g, 0, rev(c))),
                  hspec,
                  pl.BlockSpec((l, n), lambda g, c, r: (rev(c), boff + g)),
                  pl.BlockSpec((l, n), lambda g, c, r: (rev(c), coff + g)),
                  pl.BlockSpec((None, None, n, p), lambda g, c, r: (g * r_ + r, rev(c), 0, 0)),
                  xspec],
        out_specs=[xspec, cspec, cspec,
                   pl.BlockSpec((l, n), lambda g, c, r: (rev(c), g)),
                   pl.BlockSpec((l, n), lambda g, c, r: (rev(c), g)),
                   hspec, hspec],
        out_shape=[jax.ShapeDtypeStruct((hh, s, p), F32),
                   jax.ShapeDtypeStruct((g_, s, r_), F32),
                   jax.ShapeDtypeStruct((g_, s, r_), F32),
                   jax.ShapeDtypeStruct((s, g_ * n), F32),
                   jax.ShapeDtypeStruct((s, g_ * n), F32),
                   jax.ShapeDtypeStruct((g_, 1, r_), F32),
                   jax.ShapeDtypeStruct((g_, 1, r_), F32)],
        scratch_shapes=[pltpu.VMEM((r_, n, p), F32), pltpu.VMEM((l, l), F32),
                        pltpu.VMEM((l, r_), F32), pltpu.VMEM((r_, l), F32)],
        compiler_params=_cparams(("parallel", "arbitrary", "arbitrary")),
    )(xh, dtc, dac, dar, dsk, xbc, xbc, st, dy)


CONV_TC = 256
CONV_RC = 512


def _conv_fwd(xp, w, b):
    sp, ch = xp.shape
    s = sp - CONV_PAD
    rc = min(CONV_RC, s)
    base = CONV_PAD - (CONV_K - 1)

    def body(x_ref, w_ref, b_ref, pre_ref, act_ref):
        wv = w_ref[...]
        for t0 in range(0, s, rc):
            acc = jnp.broadcast_to(b_ref[...], (rc, CONV_TC))
            for kk in range(CONV_K):
                acc = acc + wv[kk:kk + 1, :] * x_ref[pl.ds(t0 + base + kk, rc), :]
            pre_ref[pl.ds(t0, rc), :] = acc
            act_ref[pl.ds(t0, rc), :] = _silu(acc)

    shp = jax.ShapeDtypeStruct((s, ch), F32)
    return pl.pallas_call(
        body, name="conv_fwd", grid=(ch // CONV_TC,),
        in_specs=[pl.BlockSpec((sp, CONV_TC), lambda j: (0, j)),
                  pl.BlockSpec((CONV_K, CONV_TC), lambda j: (0, j)),
                  pl.BlockSpec((1, CONV_TC), lambda j: (0, j))],
        out_specs=[pl.BlockSpec((s, CONV_TC), lambda j: (0, j))] * 2,
        out_shape=[shp, shp],
        compiler_params=_cparams(("parallel",)),
    )(xp, w, b)


def _conv_bwd(xp, prep, dactp, w):
    sp, ch = xp.shape
    s = sp - CONV_PAD
    rc = min(CONV_RC, s)
    base = CONV_PAD - (CONV_K - 1)

    def body(x_ref, pre_ref, da_ref, w_ref, dx_ref, dw_ref, db_ref, dpre_ref):
        wv = w_ref[...]
        for t0 in range(0, s, rc):
            dpre_ref[pl.ds(t0, rc), :] = da_ref[pl.ds(t0, rc), :] * _dsilu(pre_ref[pl.ds(t0, rc), :])
        dpre_ref[pl.ds(s, CONV_PAD), :] = jnp.zeros((CONV_PAD, CONV_TC), F32)
        dws = [jnp.zeros((1, CONV_TC), F32) for _ in range(CONV_K)]
        dbs = jnp.zeros((1, CONV_TC), F32)
        for t0 in range(0, s, rc):
            acc = jnp.zeros((rc, CONV_TC), F32)
            dp = dpre_ref[pl.ds(t0, rc), :]
            for kk in range(CONV_K):
                acc = acc + wv[kk:kk + 1, :] * dpre_ref[pl.ds(t0 + CONV_K - 1 - kk, rc), :]
                dws[kk] = dws[kk] + jnp.sum(dp * x_ref[pl.ds(t0 + base + kk, rc), :], axis=0, keepdims=True)
            dbs = dbs + jnp.sum(dp, axis=0, keepdims=True)
            dx_ref[pl.ds(t0, rc), :] = acc.astype(dx_ref.dtype)
        for kk in range(CONV_K):
            dw_ref[kk:kk + 1, :] = dws[kk]
        db_ref[...] = dbs

    col = pl.BlockSpec((sp, CONV_TC), lambda j: (0, j))
    return pl.pallas_call(
        body, name="conv_bwd", grid=(ch // CONV_TC,),
        in_specs=[col, col, col, pl.BlockSpec((CONV_K, CONV_TC), lambda j: (0, j))],
        out_specs=[pl.BlockSpec((s, CONV_TC), lambda j: (0, j)),
                   pl.BlockSpec((CONV_K, CONV_TC), lambda j: (0, j)),
                   pl.BlockSpec((1, CONV_TC), lambda j: (0, j))],
        out_shape=[jax.ShapeDtypeStruct((s, ch), CDT),
                   jax.ShapeDtypeStruct((CONV_K, ch), F32),
                   jax.ShapeDtypeStruct((1, ch), F32)],
        scratch_shapes=[pltpu.VMEM((sp, CONV_TC), F32)],
        compiler_params=_cparams(("parallel",)),
    )(xp, prep, dactp, w)


MEM_TS = 512


def _mem_fwd(mq, kv):
    s = mq.shape[0]
    m = kv.shape[0]
    ts = min(MEM_TS, s)
    scale = MEM_DH ** -0.5

    def body(q_ref, k_ref, v_ref, o_ref):
        sc = _dot(q_ref[...], k_ref[...], NT) * scale
        e = jnp.exp(sc - jnp.max(sc, axis=1, keepdims=True))
        pr = e / jnp.sum(e, axis=1, keepdims=True)
        o_ref[...] = _dot(pr, v_ref[...]).astype(o_ref.dtype)

    return pl.pallas_call(
        body, name="mem_fwd", grid=(MEM_H, s // ts),
        in_specs=[pl.BlockSpec((ts, MEM_DH), lambda a, i: (i, a)),
                  pl.BlockSpec((m, MEM_DH), lambda a, i: (0, a)),
                  pl.BlockSpec((m, MEM_DH), lambda a, i: (0, MEM_H + a))],
        out_specs=pl.BlockSpec((ts, MEM_DH), lambda a, i: (i, a)),
        out_shape=jax.ShapeDtypeStruct((s, MEM_H * MEM_DH), CDT),
        compiler_params=_cparams(("parallel", "arbitrary")),
    )(mq, kv, kv)


def _mem_bwd(mq, kv, do):
    s = mq.shape[0]
    m = kv.shape[0]
    ts = min(MEM_TS, s)
    scale = MEM_DH ** -0.5

    def body(q_ref, k_ref, v_ref, do_ref, dq_ref, dk_ref, dv_ref):
        i = pl.program_id(1)

        @pl.when(i == 0)
        def _():
            dk_ref[...] = jnp.zeros_like(dk_ref)
            dv_ref[...] = jnp.zeros_like(dv_ref)

        qv, kb, vb, dov = q_ref[...], k_ref[...], v_ref[...], do_ref[...]
        sc = _dot(qv, kb, NT) * scale
        e = jnp.exp(sc - jnp.max(sc, axis=1, keepdims=True))
        pr = e / jnp.sum(e, axis=1, keepdims=True)
        dp = _dot(dov, vb, NT)
        ds = pr * (dp - jnp.sum(dp * pr, axis=1, keepdims=True)) * scale
        dq_ref[...] = _dot(ds, kb).astype(dq_ref.dtype)
        dk_ref[...] += _dot(ds, qv, TN)
        dv_ref[...] += _dot(pr, dov, TN)

    tile = pl.BlockSpec((ts, MEM_DH), lambda a, i: (i, a))
    kvo = pl.BlockSpec((m, MEM_DH), lambda a, i: (0, a))
    return pl.pallas_call(
        body, name="mem_bwd", grid=(MEM_H, s // ts),
        in_specs=[tile, kvo, pl.BlockSpec((m, MEM_DH), lambda a, i: (0, MEM_H + a)), tile],
        out_specs=[tile, kvo, kvo],
        out_shape=[jax.ShapeDtypeStruct((s, MEM_H * MEM_DH), CDT),
                   jax.ShapeDtypeStruct((m, MEM_H * MEM_DH), F32),
                   jax.ShapeDtypeStruct((m, MEM_H * MEM_DH), F32)],
        compiler_params=_cparams(("parallel", "arbitrary")),
    )(mq, kv, kv, do)


def _heads(t, nh, dh):
    return t.reshape(t.shape[0], nh, dh).transpose(1, 0, 2)


def _unheads(t):
    return t.transpose(1, 0, 2).reshape(t.shape[1], t.shape[0] * t.shape[2])


def _group_cols(t):
    return t.reshape(t.shape[0], SSD_G, SSD_R).transpose(1, 0, 2)


def _pad_cols(t, width):
    return jnp.pad(t, ((0, 0), (0, width - t.shape[1])))


def _local_step(x, mem, tgt, p, wt):
    s, d = x.shape
    c1, c2, c3, c4, c5 = 3 * d, 3 * d + SSD_INNER, 3 * d + SSD_INNER + CONV_DIM, \
        3 * d + SSD_INNER + CONV_DIM + SSD_H, 3 * d + SSD_INNER + CONV_DIM + SSD_H + d
    w_in = wt["w_in"]
    w_seg = [w_in[:, :c1], w_in[:, c1:c2], w_in[:, c2:c3], _pad_cols(w_in[:, c3:c4], DT_PAD),
             w_in[:, c4:c5], w_in[:, c5:]]
    seg_dtype = [CDT, F32, F32, F32, CDT, F32]
    seg_name = ["qkv", "z", "xbc", "dt", "mq", "gl"]

    u = _rows(lambda xv, g: _rms(xv, g), [x], [p["norm_mix_pre"]], [(d, CDT)], ts=512, name="f_norm_pre")[0]
    qkv, z, xbc_raw, dt_raw, mq, gl = [
        _mm(u, w_seg[i], out_dtype=seg_dtype[i], name="f_in_" + seg_name[i]) for i in range(6)]

    bias128 = _pad_cols(p["dt_bias"], DT_PAD)
    alog128 = _pad_cols(p["a_log"], DT_PAD)

    def dt_fn(dtr, bias, alog):
        dt = _softplus(dtr + bias)
        return dt, dt * (-jnp.exp(alog))

    dt128, da128 = _rows(dt_fn, [dt_raw], [bias128, alog128], [(DT_PAD, F32), (DT_PAD, F32)],
                         ts=512, name="f_dt")
    dtc = _group_cols(dt128[:, :SSD_H])
    dac = _group_cols(da128[:, :SSD_H])
    dar = dac.transpose(0, 2, 1)
    dsk = p["d_skip"].reshape(SSD_G, 1, SSD_R)

    xp = jnp.pad(xbc_raw, ((CONV_PAD, 0), (0, 0)))
    conv_w, conv_b = p["conv_w"], p["conv_b"]
    pre, xbc = _conv_fwd(xp, conv_w, conv_b)
    xh = _heads(xbc[:, :SSD_INNER], SSD_H, SSD_P)
    y_h, st = _ssd_fwd(xh, dtc, dac, dar, dsk, xbc)
    y_core = _unheads(y_h)

    def group_norm_fwd(yv, zv, wn):
        y2 = yv * _silu(zv)
        gw = SSD_INNER // SSD_G
        outs = []
        for gi in range(SSD_G):
            seg = y2[:, gi * gw:(gi + 1) * gw]
            outs.append(_rms(seg, wn[:, gi * gw:(gi + 1) * gw]))
        return jnp.concatenate(outs, axis=1)

    y_ssd = _rows(group_norm_fwd, [y_core, z], [p["ssd_norm"]], [(SSD_INNER, CDT)], ts=256, name="f_ssd_post")[0]

    q_h = _heads(qkv[:, :d], SB_H, SB_DH)
    k_h = _heads(qkv[:, d:2 * d], SB_H, SB_DH)
    v_h = _heads(qkv[:, 2 * d:], SB_H, SB_DH)
    o_h, lt_h = _sb_fwd(q_h, k_h, v_h)
    y_sb = _unheads(o_h).astype(CDT)

    mu = _rows(lambda mv, g: _rms(mv, g), [mem], [p["norm_mem"]], [(d, CDT)], ts=256, name="f_norm_mem")[0]
    kv = _mm(mu, wt["w_mem_kv"], out_dtype=CDT, name="f_mem_kv")
    y_mem = _mem_fwd(mq, kv)

    p_sb = _mm(y_sb, wt["w_sb_out"], name="f_sb_out")
    p_ssd = _mm(y_ssd, wt["w_ssd_out"], name="f_ssd_out")
    p_mem = _mm(y_mem, wt["w_mem_out"], name="f_mem_out")

    def merge_fn(glv, a, b, c):
        return (_sigmoid(glv[:, :d]) * a + _sigmoid(glv[:, d:2 * d]) * b + _sigmoid(glv[:, 2 * d:]) * c)

    merged = _rows(merge_fn, [gl, p_sb, p_ssd, p_mem], [], [(d, CDT)], ts=256, name="f_merge")[0]
    mix = _mm(merged, wt["w_o"], name="f_w_o")

    def mid_fn(xv, mixv, g_post, g_pre):
        h1 = xv + _rms(mixv, g_post)
        return h1, _rms(h1, g_pre)

    h1, u2 = _rows(mid_fn, [x, mix], [p["norm_mix_post"], p["norm_mlp_pre"]], [(d, F32), (d, CDT)],
                   ts=512, name="f_mid")
    a1 = _mm(u2, wt["w_up"], name="f_up")
    act = _rows(lambda a: jnp.square(jnp.maximum(a, 0.0)), [a1], [], [(a1.shape[1], CDT)], ts=256, name="f_act")[0]
    ff = _mm(act, wt["w_down"], name="f_down")

    def loss_fn(h1v, ffv, tv, g):
        diff = h1v + _rms(ffv, g) - tv
        tot = jnp.sum(jnp.sum(diff * diff, axis=1, keepdims=True), axis=0, keepdims=True)
        return diff * (1.0 / d), tot

    dh2, loss_acc = _rows(loss_fn, [h1, ff, tgt], [p["norm_mlp_post"]], [(d, F32)], [(1, 128)],
                          ts=512, name="f_loss")
    loss = loss_acc[:, :1] * (0.5 / d)

    sg = {}

    def b_post(ffv, dyv, g):
        dx, dg = _rms_bwd(ffv, g, dyv)
        return dx, dg

    d_ff, sg["norm_mlp_post"] = _rows(b_post, [ff, dh2], [p["norm_mlp_post"]], [(d, CDT)], [(1, d)],
                                      ts=512, name="b_norm_mlp_post")
    dact = _mm(d_ff, wt["w_down"], tb=True, name="b_down_x")
    gw = {"w_down": _mm(act, d_ff, ta=True, name="b_down_w")}
    da1 = _rows(lambda dv, a: dv * 2.0 * jnp.maximum(a, 0.0), [dact, a1], [], [(a1.shape[1], CDT)],
                ts=256, name="b_act")[0]
    du2 = _mm(da1, wt["w_up"], tb=True, name="b_up_x")
    gw["w_up"] = _mm(u2, da1, ta=True, name="b_up_w")

    def b_mid(h1v, du2v, dh2v, mixv, g_pre, g_post):
        dxa, dga = _rms_bwd(h1v, g_pre, du2v)
        dh1 = dh2v + dxa
        dmix, dgb = _rms_bwd(mixv, g_post, dh1)
        return dh1, dmix, dga, dgb

    dh1, dmix, sg["norm_mlp_pre"], sg["norm_mix_post"] = _rows(
        b_mid, [h1, du2, dh2, mix], [p["norm_mlp_pre"], p["norm_mix_post"]],
        [(d, F32), (d, CDT)], [(1, d), (1, d)], ts=256, name="b_mid")
    dmerged = _mm(dmix, wt["w_o"], tb=True, name="b_w_o_x")
    gw["w_o"] = _mm(merged, dmix, ta=True, name="b_w_o_w")

    def b_merge(dm, glv, a, b, c):
        outs, dgl = [], []
        for i, br in enumerate((a, b, c)):
            gt = _sigmoid(glv[:, i * d:(i + 1) * d])
            outs.append(gt * dm)
            dgl.append(dm * br * gt * (1.0 - gt))
        return outs[0], outs[1], outs[2], jnp.concatenate(dgl, axis=1)

    dp_sb, dp_ssd, dp_mem, dgl = _rows(b_merge, [dmerged, gl, p_sb, p_ssd, p_mem], [],
                                       [(d, CDT), (d, CDT), (d, CDT), (3 * d, CDT)], ts=256, name="b_merge")
    dy_sb = _mm(dp_sb, wt["w_sb_out"], tb=True, name="b_sb_out_x")
    gw["w_sb_out"] = _mm(y_sb, dp_sb, ta=True, name="b_sb_out_w")
    dy_ssd = _mm(dp_ssd, wt["w_ssd_out"], tb=True, name="b_ssd_out_x")
    gw["w_ssd_out"] = _mm(y_ssd, dp_ssd, ta=True, name="b_ssd_out_w")
    dy_mem = _mm(dp_mem, wt["w_mem_out"], tb=True, out_dtype=CDT, name="b_mem_out_x")
    gw["w_mem_out"] = _mm(y_mem, dp_mem, ta=True, name="b_mem_out_w")

    dmq, dk_m, dv_m = _mem_bwd(mq, kv, dy_mem)
    dkv = jnp.concatenate([dk_m, dv_m], axis=1)
    gw["w_mem_kv"] = _mm(mu, dkv, ta=True, name="b_mem_kv_w")
    dmu = _mm(dkv, wt["w_mem_kv"], tb=True, name="b_mem_kv_x")
    sg["norm_mem"] = _rows(lambda mv, dv, g: _rms_bwd(mv, g, dv)[1], [mem, dmu], [p["norm_mem"]], [], [(1, d)],
                           ts=256, name="b_norm_mem")[0]

    dq_h, dk_h, dv_h = _sb_bwd(q_h, k_h, v_h, lt_h, _heads(dy_sb, SB_H, SB_DH))
    dqkv = jnp.concatenate([_unheads(dq_h), _unheads(dk_h), _unheads(dv_h)], axis=1).astype(CDT)

    def group_norm_bwd(dyo, yv, zv, wn):
        sz = _silu(zv)
        y2 = yv * sz
        gw_ = SSD_INNER // SSD_G
        dy2, dwn = [], []
        for gi in range(SSD_G):
            sl = slice(gi * gw_, (gi + 1) * gw_)
            dxs, dgs = _rms_bwd(y2[:, sl], wn[:, sl], dyo[:, sl])
            dy2.append(dxs)
            dwn.append(dgs)
        dy2 = jnp.concatenate(dy2, axis=1)
        return dy2 * sz, dy2 * yv * _dsilu(zv), jnp.concatenate(dwn, axis=1)

    dy_core, dz, sg["ssd_norm"] = _rows(group_norm_bwd, [dy_ssd, y_core, z], [p["ssd_norm"]],
                                        [(SSD_INNER, F32), (SSD_INNER, CDT)], [(1, SSD_INNER)],
                                        ts=256, name="b_ssd_post")
    dxh, dda, dxx, d_b, d_c, dah, ddsk = _ssd_bwd(xh, dtc, dac, dar, dsk, xbc, st, _heads(dy_core, SSD_H, SSD_P))
    sg["d_skip"] = ddsk.reshape(1, SSD_H)
    sg["a_log"] = dah.reshape(1, SSD_H) * (-jnp.exp(p["a_log"]))

    def b_dt(ddav, dxxv, dtr, bias, alog):
        ddt = ddav * (-jnp.exp(alog)) + dxxv
        draw = ddt * _sigmoid(dtr + bias)
        return draw, jnp.sum(draw, axis=0, keepdims=True)

    ungroup = lambda t: _pad_cols(t.transpose(1, 0, 2).reshape(s, SSD_H), DT_PAD)
    ddt_raw, dbias128 = _rows(b_dt, [ungroup(dda), ungroup(dxx), dt_raw], [bias128, alog128],
                              [(DT_PAD, CDT)], [(1, DT_PAD)], ts=512, name="b_dt")
    sg["dt_bias"] = dbias128[:, :SSD_H]

    dxbc = jnp.concatenate([_unheads(dxh), d_b, d_c], axis=1)
    back = ((0, CONV_PAD), (0, 0))
    dxbc_raw, sg["conv_w"], sg["conv_b"] = _conv_bwd(xp, jnp.pad(pre, back), jnp.pad(dxbc, back), conv_w)

    dseg = [dqkv, dz, dxbc_raw, ddt_raw, dmq, dgl]
    dus = [_mm(dseg[i], w_seg[i], tb=True, name="b_in_x_" + seg_name[i]) for i in range(6)]
    dws = [_mm(u, dseg[i], ta=True, name="b_in_w_" + seg_name[i]) for i in range(6)]
    dws[3] = dws[3][:, :SSD_H]
    gw["w_in"] = jnp.concatenate(dws, axis=1)

    def b_pre(xv, dh1v, d0, d1, d2, d3, d4, d5, g):
        dx, dg = _rms_bwd(xv, g, d0 + d1 + d2 + d3 + d4 + d5)
        return dh1v + dx, dg

    grad_x, sg["norm_mix_pre"] = _rows(b_pre, [x, dh1] + dus, [p["norm_mix_pre"]], [(d, F32)], [(1, d)],
                                       ts=256, name="b_norm_pre")
    return loss, grad_x, gw, sg


HBM = pl.BlockSpec(memory_space=pltpu.HBM)
MESH = pl.DeviceIdType.MESH


def _me_and_peers():
    x, y, c = lax.axis_index("x"), lax.axis_index("y"), lax.axis_index("c")
    me = 4 * x + 2 * y + c
    peers = [(x, y, 1 - c), (1 - x, y, c), (x, 1 - y, c), (1 - x, 1 - y, c),
             (1 - x, y, 1 - c), (x, 1 - y, 1 - c), (1 - x, 1 - y, 1 - c)]
    return me, peers


def _peer_index(peer):
    return 4 * peer[0] + 2 * peer[1] + peer[2]


def _all_gather(shards, name):
    n = len(shards)

    def body(*refs):
        ins, outs = refs[:n], refs[n:2 * n]
        send_sems, recv_sems, local_sems = refs[2 * n:]
        me, peers = _me_and_peers()
        copies = []
        for a in range(n):
            loc = pltpu.make_async_copy(ins[a], outs[a].at[me], local_sems.at[a])
            loc.start()
            copies.append(loc)
            for kk, peer in enumerate(peers):
                cp = pltpu.make_async_remote_copy(
                    src_ref=ins[a], dst_ref=outs[a].at[me],
                    send_sem=send_sems.at[a, kk], recv_sem=recv_sems.at[a, kk],
                    device_id=peer, device_id_type=MESH)
                cp.start()
                copies.append(cp)
        for cp in copies:
            cp.wait()

    return pl.pallas_call(
        body, name=name,
        in_specs=[HBM] * n, out_specs=[HBM] * n,
        out_shape=[jax.ShapeDtypeStruct((N_DEV,) + t.shape, t.dtype) for t in shards],
        scratch_shapes=[pltpu.SemaphoreType.DMA((n, N_DEV - 1)), pltpu.SemaphoreType.DMA((n, N_DEV - 1)),
                        pltpu.SemaphoreType.DMA((n,))],
        compiler_params=pltpu.CompilerParams(has_side_effects=True),
    )(*shards)


def _scatter_exchange(payloads, name):
    n = len(payloads)

    def body(*refs):
        ins, outs = refs[:n], refs[n:2 * n]
        send_sems, recv_sems, local_sems = refs[2 * n:]
        me, peers = _me_and_peers()
        copies = []
        for a in range(n):
            loc = pltpu.make_async_copy(ins[a].at[me], outs[a].at[me], local_sems.at[a])
            loc.start()
            copies.append(loc)
            for kk, peer in enumerate(peers):
                cp = pltpu.make_async_remote_copy(
                    src_ref=ins[a].at[_peer_index(peer)], dst_ref=outs[a].at[me],
                    send_sem=send_sems.at[a, kk], recv_sem=recv_sems.at[a, kk],
                    device_id=peer, device_id_type=MESH)
                cp.start()
                copies.append(cp)
        for cp in copies:
            cp.wait()

    return pl.pallas_call(
        body, name=name,
        in_specs=[HBM] * n, out_specs=[HBM] * n,
        out_shape=[jax.ShapeDtypeStruct(t.shape, t.dtype) for t in payloads],
        scratch_shapes=[pltpu.SemaphoreType.DMA((n, N_DEV - 1)), pltpu.SemaphoreType.DMA((n, N_DEV - 1)),
                        pltpu.SemaphoreType.DMA((n,))],
        compiler_params=pltpu.CompilerParams(has_side_effects=True),
    )(*payloads)


def _all_reduce_small(v, name):
    r, c = v.shape

    def body(v_ref, o_ref, buf, send_sems, recv_sems):
        me, peers = _me_and_peers()
        buf[me] = v_ref[...]
        copies = []
        for kk, peer in enumerate(peers):
            cp = pltpu.make_async_remote_copy(
                src_ref=v_ref, dst_ref=buf.at[me],
                send_sem=send_sems.at[kk], recv_sem=recv_sems.at[kk],
                device_id=peer, device_id_type=MESH)
            cp.start()
            copies.append(cp)
        for cp in copies:
            cp.wait()
        acc = buf[0]
        for i in range(1, N_DEV):
            acc = acc + buf[i]
        o_ref[...] = acc

    return pl.pallas_call(
        body, name=name,
        in_specs=[pl.BlockSpec(memory_space=pltpu.VMEM)],
        out_specs=pl.BlockSpec(memory_space=pltpu.VMEM),
        out_shape=jax.ShapeDtypeStruct((r, c), F32),
        scratch_shapes=[pltpu.VMEM((N_DEV, r, c), F32),
                        pltpu.SemaphoreType.DMA((N_DEV - 1,)), pltpu.SemaphoreType.DMA((N_DEV - 1,))],
        compiler_params=pltpu.CompilerParams(has_side_effects=True),
    )(v)


def _adamw_math(g, w, m, v):
    m2 = ADAM_B1 * m + (1.0 - ADAM_B1) * g
    v2 = ADAM_B2 * v + (1.0 - ADAM_B2) * jnp.square(g)
    m_hat = m2 / (1.0 - ADAM_B1 ** ADAM_STEP)
    v_hat = v2 / (1.0 - ADAM_B2 ** ADAM_STEP)
    delta = -ADAM_LR * (m_hat / (jnp.sqrt(v_hat) + ADAM_EPS) + ADAM_WD * w)
    return delta, m2, v2


def _adamw_reduce(parts, w, m, v, name):
    r, c = w.shape
    tr = _pick(r, (128, 64, 32, 16, 8))

    def body(p_ref, w_ref, m_ref, v_ref, g_ref, d_ref, m2_ref, v2_ref):
        g = p_ref[0].astype(F32)
        for i in range(1, N_DEV):
            g = g + p_ref[i].astype(F32)
        delta, m2, v2 = _adamw_math(g, w_ref[...], m_ref[...], v_ref[...])
        g_ref[...] = g
        d_ref[...] = delta
        m2_ref[...] = m2
        v2_ref[...] = v2

    tile = pl.BlockSpec((tr, c), lambda i: (i, 0))
    shp = jax.ShapeDtypeStruct((r, c), F32)
    return pl.pallas_call(
        body, name=name, grid=(r // tr,),
        in_specs=[pl.BlockSpec((N_DEV, tr, c), lambda i: (0, i, 0)), tile, tile, tile],
        out_specs=[tile] * 4, out_shape=[shp] * 4,
        compiler_params=_cparams(("parallel",)),
    )(parts, w, m, v)


def _adamw_plain(g, w, m, v, name):
    def body(g_ref, w_ref, m_ref, v_ref, d_ref, m2_ref, v2_ref):
        delta, m2, v2 = _adamw_math(g_ref[...], w_ref[...], m_ref[...], v_ref[...])
        d_ref[...] = delta
        m2_ref[...] = m2
        v2_ref[...] = v2

    spec = pl.BlockSpec(memory_space=pltpu.VMEM)
    shp = jax.ShapeDtypeStruct(g.shape, F32)
    return pl.pallas_call(
        body, name=name, in_specs=[spec] * 4, out_specs=[spec] * 3, out_shape=[shp] * 3,
    )(g, w, m, v)


def _cast_shard(w, name):
    r = w.shape[0]
    return _rows(lambda t: t, [w], [], [(w.shape[1], CDT)], ts=_pick(r, (256, 128)), name=name)[0]


BIG = ["w_in", "w_mem_kv", "w_up", "w_sb_out", "w_ssd_out", "w_mem_out", "w_o", "w_down"]
COL_SHARDED = ("w_in", "w_mem_kv", "w_up")
SMALL = ["norm_mix_pre", "conv_b", "dt_bias", "a_log", "d_skip", "ssd_norm", "norm_mem",
         "norm_mix_post", "norm_mlp_pre", "norm_mlp_post"]
ALL_W = ["norm_mix_pre", "w_in", "conv_w", "conv_b", "dt_bias", "a_log", "d_skip", "ssd_norm", "norm_mem",
         "w_mem_kv", "w_sb_out", "w_ssd_out", "w_mem_out", "w_o", "norm_mix_post", "norm_mlp_pre", "w_up",
         "w_down", "norm_mlp_post"]
LANES = 128


def _pack_rows(vecs):
    parts, offs, off = [], [], 0
    for t in vecs:
        flat = t.reshape(-1)
        n = flat.shape[0]
        rows = -(-n // (8 * LANES)) * 8
        parts.append(jnp.pad(flat, (0, rows * LANES - n)).reshape(rows, LANES))
        offs.append((off, n))
        off += rows
    return jnp.concatenate(parts, axis=0), offs


def _unpack_rows(packed, offs, shapes):
    out = []
    for (off, n), shape in zip(offs, shapes):
        rows = -(-n // (8 * LANES)) * 8
        out.append(packed[off:off + rows].reshape(-1)[:n].reshape(shape))
    return out


def kernel(x, mem, norm_mix_pre, w_in, conv_w, conv_b, dt_bias, a_log, d_skip, ssd_norm, norm_mem, w_mem_kv, w_sb_out, w_ssd_out, w_mem_out, w_o, norm_mix_post, norm_mlp_pre, w_up, w_down, norm_mlp_post, loss_target, m_norm_mix_pre, m_w_in, m_conv_w, m_conv_b, m_dt_bias, m_a_log, m_d_skip, m_ssd_norm, m_norm_mem, m_w_mem_kv, m_w_sb_out, m_w_ssd_out, m_w_mem_out, m_w_o, m_norm_mix_post, m_norm_mlp_pre, m_w_up, m_w_down, m_norm_mlp_post, v_norm_mix_pre, v_w_in, v_conv_w, v_conv_b, v_dt_bias, v_a_log, v_d_skip, v_ssd_norm, v_norm_mem, v_w_mem_kv, v_w_sb_out, v_w_ssd_out, v_w_mem_out, v_w_o, v_norm_mix_post, v_norm_mlp_pre, v_w_up, v_w_down, v_norm_mlp_post):
    wd = dict(norm_mix_pre=norm_mix_pre, w_in=w_in, conv_w=conv_w, conv_b=conv_b, dt_bias=dt_bias, a_log=a_log,
              d_skip=d_skip, ssd_norm=ssd_norm, norm_mem=norm_mem, w_mem_kv=w_mem_kv, w_sb_out=w_sb_out,
              w_ssd_out=w_ssd_out, w_mem_out=w_mem_out, w_o=w_o, norm_mix_post=norm_mix_post,
              norm_mlp_pre=norm_mlp_pre, w_up=w_up, w_down=w_down, norm_mlp_post=norm_mlp_post)
    md = dict(norm_mix_pre=m_norm_mix_pre, w_in=m_w_in, conv_w=m_conv_w, conv_b=m_conv_b, dt_bias=m_dt_bias,
              a_log=m_a_log, d_skip=m_d_skip, ssd_norm=m_ssd_norm, norm_mem=m_norm_mem, w_mem_kv=m_w_mem_kv,
              w_sb_out=m_w_sb_out, w_ssd_out=m_w_ssd_out, w_mem_out=m_w_mem_out, w_o=m_w_o,
              norm_mix_post=m_norm_mix_post, norm_mlp_pre=m_norm_mlp_pre, w_up=m_w_up, w_down=m_w_down,
              norm_mlp_post=m_norm_mlp_post)
    vd = dict(norm_mix_pre=v_norm_mix_pre, w_in=v_w_in, conv_w=v_conv_w, conv_b=v_conv_b, dt_bias=v_dt_bias,
              a_log=v_a_log, d_skip=v_d_skip, ssd_norm=v_ssd_norm, norm_mem=v_norm_mem, w_mem_kv=v_w_mem_kv,
              w_sb_out=v_w_sb_out, w_ssd_out=v_w_ssd_out, w_mem_out=v_w_mem_out, w_o=v_w_o,
              norm_mix_post=v_norm_mix_post, norm_mlp_pre=v_norm_mlp_pre, w_up=v_w_up, w_down=v_w_down,
              norm_mlp_post=v_norm_mlp_post)
    me = 4 * lax.axis_index("x") + 2 * lax.axis_index("y") + lax.axis_index("c")

    shards = [_cast_shard(wd[n][0], "cast_" + n) for n in BIG]
    gathered = _all_gather(shards, "gather_weights")
    wt = {}
    for n, gth in zip(BIG, gathered):
        if n in COL_SHARDED:
            wt[n] = gth.transpose(1, 0, 2).reshape(gth.shape[1], N_DEV * gth.shape[2])
        else:
            wt[n] = gth.reshape(N_DEV * gth.shape[1], gth.shape[2])

    cw = wd["conv_w"][0]
    ch = cw.shape[1]
    cw_slot = lax.dynamic_update_slice(jnp.zeros((CONV_K, N_DEV * ch), F32), cw, (0, me * ch))
    cw_packed, cw_offs = _pack_rows([cw_slot])
    conv_w_full = _unpack_rows(_all_reduce_small(cw_packed, "gather_conv_w"), cw_offs, [(CONV_K, N_DEV * ch)])[0]

    p = {n: wd[n] for n in SMALL}
    p["conv_w"] = conv_w_full
    loss, grad_x, gw, sg = _local_step(x[0], mem[0], loss_target[0], p, wt)

    payloads = []
    for n in BIG:
        g = gw[n]
        if n in COL_SHARDED:
            payloads.append(g.reshape(g.shape[0], N_DEV, g.shape[1] // N_DEV).transpose(1, 0, 2))
        else:
            payloads.append(g.reshape(N_DEV, g.shape[0] // N_DEV, g.shape[1]))
    received = _scatter_exchange(payloads, "scatter_grads")

    small_names = SMALL + ["conv_w"]
    packed, offs = _pack_rows([sg[n] for n in small_names] + [loss])
    reduced = _all_reduce_small(packed, "reduce_small")
    small_red = _unpack_rows(reduced, offs, [sg[n].shape for n in small_names] + [(1, 1)])
    loss_out = small_red[-1].reshape(())
    sgr = dict(zip(small_names, small_red[:-1]))
    sgr["conv_w"] = lax.dynamic_slice(sgr["conv_w"], (0, me * ch), (CONV_K, ch))

    grads, deltas, new_m, new_v = {}, {}, {}, {}
    for n, rec in zip(BIG, received):
        g, dl, m2, v2 = _adamw_reduce(rec, wd[n][0], md[n][0], vd[n][0], "adamw_" + n)
        grads[n], deltas[n], new_m[n], new_v[n] = g[None], dl[None], m2[None], v2[None]
    g_p, g_offs = _pack_rows([sgr[n] for n in small_names])
    w_p, _ = _pack_rows([wd[n] for n in small_names])
    m_p, _ = _pack_rows([md[n] for n in small_names])
    v_p, _ = _pack_rows([vd[n] for n in small_names])
    d_p, m2_p, v2_p = _adamw_plain(g_p, w_p, m_p, v_p, "adamw_small")
    shapes = [wd[n].shape for n in small_names]
    for n, g, dl, m2, v2 in zip(small_names, [sgr[n].reshape(wd[n].shape) for n in small_names],
                                _unpack_rows(d_p, g_offs, shapes), _unpack_rows(m2_p, g_offs, shapes),
                                _unpack_rows(v2_p, g_offs, shapes)):
        grads[n], deltas[n], new_m[n], new_v[n] = g, dl, m2, v2

    return (loss_out, grad_x[None], *[grads[n] for n in ALL_W], *[deltas[n] for n in ALL_W],
            *[new_m[n] for n in ALL_W], *[new_v[n] for n in ALL_W])
```

```python
import functools

import jax
import jax.numpy as jnp
from jax import lax
from jax.experimental import pallas as pl
from jax.experimental.pallas import tpu as pltpu

F32 = jnp.float32
BF16 = jnp.bfloat16
CDT = jnp.bfloat16
EPS = 1e-6
VMEM_LIMIT = 56 * 1024 * 1024

N_DEV = 8
D_MODEL = 1024
SB_H, SB_DH = 16, 64
SSD_G, SSD_R, SSD_P, SSD_N, SSD_L = 4, 8, 64, 128, 128
SSD_H = SSD_G * SSD_R
SSD_INNER = SSD_H * SSD_P
CONV_K = 4
CONV_DIM = SSD_INNER + 2 * SSD_G * SSD_N
MEM_H, MEM_DH = 4, 256
DT_PAD = 128
SB_TQ, SB_BK = 512, 256
CONV_PAD = 8
MM_TILE, MM_TILE_K = 1024, 2048

ADAM_LR, ADAM_B1, ADAM_B2, ADAM_EPS, ADAM_WD, ADAM_STEP = 0.001, 0.9, 0.999, 1e-08, 0.01, 10

NT = (((1,), (1,)), ((), ()))
TN = (((0,), (0,)), ((), ()))
NN = (((1,), (0,)), ((), ()))


def _cparams(sem=None):
    return pltpu.CompilerParams(dimension_semantics=sem, vmem_limit_bytes=VMEM_LIMIT)


def _pick(n, cands):
    for c in cands:
        if n % c == 0:
            return c
    return n


def _dot(a, b, dims=NN):
    return lax.dot_general(a.astype(CDT), b.astype(CDT), dims, preferred_element_type=F32)


def _split_dot(x, t, left, pieces):
    if CDT == F32:
        return lax.dot_general(t, x, NN, preferred_element_type=F32) if left else \
            lax.dot_general(x, t, NN, preferred_element_type=F32)
    acc = None
    rem = x
    for _ in range(pieces):
        hi = rem.astype(BF16)
        rem = rem - hi.astype(F32)
        d = lax.dot_general(t, hi, NN, preferred_element_type=F32) if left else \
            lax.dot_general(hi, t, NN, preferred_element_type=F32)
        acc = d if acc is None else acc + d
    return acc


def _iota(shape, dim):
    return lax.broadcasted_iota(jnp.int32, shape, dim)


def _sigmoid(x):
    return 1.0 / (1.0 + jnp.exp(-x))


def _silu(x):
    return x * _sigmoid(x)


def _dsilu(x):
    s = _sigmoid(x)
    return s * (1.0 + x * (1.0 - s))


def _softplus(x):
    return jnp.maximum(x, 0.0) + jnp.log(1.0 + jnp.exp(-jnp.abs(x)))


def _rms(x, g):
    r = lax.rsqrt(jnp.mean(x * x, axis=-1, keepdims=True) + EPS)
    return x * r * g


def _rms_bwd(x, g, dy):
    r = lax.rsqrt(jnp.mean(x * x, axis=-1, keepdims=True) + EPS)
    n = x * r
    dn = dy * g
    dx = r * (dn - n * jnp.mean(dn * n, axis=-1, keepdims=True))
    dg = jnp.sum(dy * n, axis=0, keepdims=True)
    return dx, dg


def _mm(a, b, *, ta=False, tb=False, out_dtype=F32, name):
    m = a.shape[1] if ta else a.shape[0]
    k = a.shape[0] if ta else a.shape[1]
    n = b.shape[0] if tb else b.shape[1]
    assert k == (b.shape[1] if tb else b.shape[0])
    bm = _pick(m, (MM_TILE, 512, 256, 128))
    bn = _pick(n, (MM_TILE, 512, 256, 128))
    bk = _pick(k, (MM_TILE_K, 1024, 512, 256, 128))
    nk = k // bk
    dims = (((0 if ta else 1,), (1 if tb else 0,)), ((), ()))

    def body(a_ref, b_ref, o_ref, acc_ref):
        part = _dot(a_ref[...], b_ref[...], dims)
        if nk == 1:
            o_ref[...] = part.astype(o_ref.dtype)
            return
        kk = pl.program_id(2)

        @pl.when(kk == 0)
        def _():
            acc_ref[...] = part

        @pl.when(jnp.logical_and(kk > 0, kk < nk - 1))
        def _():
            acc_ref[...] += part

        @pl.when(kk == nk - 1)
        def _():
            o_ref[...] = (acc_ref[...] + part).astype(o_ref.dtype)

    a_spec = pl.BlockSpec((bk, bm), lambda i, j, kk: (kk, i)) if ta else \
        pl.BlockSpec((bm, bk), lambda i, j, kk: (i, kk))
    b_spec = pl.BlockSpec((bn, bk), lambda i, j, kk: (j, kk)) if tb else \
        pl.BlockSpec((bk, bn), lambda i, j, kk: (kk, j))
    return pl.pallas_call(
        body, name=name,
        grid=(m // bm, n // bn, nk),
        in_specs=[a_spec, b_spec],
        out_specs=pl.BlockSpec((bm, bn), lambda i, j, kk: (i, j)),
        out_shape=jax.ShapeDtypeStruct((m, n), out_dtype),
        scratch_shapes=[pltpu.VMEM((bm, bn) if nk > 1 else (8, 128), F32)],
        compiler_params=_cparams(("parallel", "parallel", "arbitrary")),
    )(a, b)


def _rows(fn, tiled, params, outs, accs=(), *, ts, name):
    s = tiled[0].shape[0]
    ts = min(ts, s)
    assert s % ts == 0
    nt, npar, no, na = len(tiled), len(params), len(outs), len(accs)

    def body(*refs):
        i = pl.program_id(0)
        vals = [r[...] for r in refs[:nt + npar]]
        res = fn(*vals)
        if not isinstance(res, (tuple, list)):
            res = (res,)
        orefs = refs[nt + npar:nt + npar + no]
        arefs = refs[nt + npar + no:]
        for r_, val in zip(orefs, res[:no]):
            r_[...] = val.astype(r_.dtype)
        if na:
            @pl.when(i == 0)
            def _():
                for r_ in arefs:
                    r_[...] = jnp.zeros_like(r_)

            for r_, val in zip(arefs, res[no:]):
                r_[...] += jnp.broadcast_to(val, r_.shape)

    in_specs = [pl.BlockSpec((ts, a.shape[1]), lambda i: (i, 0)) for a in tiled]
    in_specs += [pl.BlockSpec(p.shape, lambda i: (0, 0)) for p in params]
    out_specs = [pl.BlockSpec((ts, w), lambda i: (i, 0)) for (w, _) in outs]
    out_specs += [pl.BlockSpec(shape, lambda i: (0, 0)) for shape in accs]
    out_shape = [jax.ShapeDtypeStruct((s, w), dt) for (w, dt) in outs]
    out_shape += [jax.ShapeDtypeStruct(shape, F32) for shape in accs]
    res = pl.pallas_call(
        body, name=name, grid=(s // ts,),
        in_specs=in_specs, out_specs=out_specs, out_shape=out_shape,
        compiler_params=_cparams(("arbitrary",)),
    )(*tiled, *params)
    return res


def _sb_block(qs, kb, diag):
    tq, bk = qs.shape[0], kb.shape[0]
    z = _dot(qs, kb, NT)
    lb = jnp.minimum(z, 0.0) - jnp.log(1.0 + jnp.exp(-jnp.abs(z)))
    lk = lb - z
    if diag is None:
        return None, lb, lk
    causal = (diag + _iota((tq, bk), 1)) < _iota((tq, bk), 0)
    return causal, lb, jnp.where(causal, lk, 0.0)


def _fused_exchange(scatter, ncols, nsteps):
    def hooks(ins, outs, sems):
        step = pl.program_id(0) * ncols + pl.program_id(1)

        @pl.when(step == 0)
        def _():
            for cp in _exchange_copies(ins, outs, *sems, scatter):
                cp.start()

        def finish():
            @pl.when(step == nsteps - 1)
            def _():
                for cp in _exchange_copies(ins, outs, *sems, scatter):
                    cp.wait()
        return finish
    return hooks


def _sb_fwd(q, k, v, comm=()):
    h, s, dh = q.shape
    tq, bk = min(SB_TQ, s), min(SB_BK, s)
    scale = dh ** -0.5
    nc = len(comm)
    hooks = _fused_exchange(False, s // tq, h * (s // tq))

    def body(q_ref, k_ref, v_ref, *rest):
        o_ref, lt_ref = rest[nc:nc + 2]
        if nc:
            finish = hooks(rest[:nc], rest[nc + 2:2 * nc + 2], rest[2 * nc + 2:])
        i = pl.program_id(1)
        q0 = i * tq
        qs = q_ref[...] * scale
        tri = (_iota((bk, bk), 0) > _iota((bk, bk), 1)).astype(CDT)

        def step(k0, carry, diag):
            cf, acc = carry
            kb = k_ref[pl.ds(k0, bk), :]
            vb = v_ref[pl.ds(k0, bk), :]
            causal, lb, lk = _sb_block(qs, kb, diag)
            w = jnp.exp(lb + cf + _split_dot(lk, tri, False, 2))
            if causal is not None:
                w = jnp.where(causal, w, 0.0)
            return cf + jnp.sum(lk, axis=1, keepdims=True), acc + _dot(w, vb)

        carry = (jnp.zeros((tq, 1), F32), jnp.zeros((tq, dh), F32))
        for d in reversed(range(tq // bk)):
            carry = step(pl.multiple_of(q0 + d * bk, bk), carry, d * bk)
        nfull = q0 // bk
        cf, acc = lax.fori_loop(
            0, nfull, lambda jj, c: step(pl.multiple_of((nfull - 1 - jj) * bk, bk), c, None), carry)
        o_ref[...] = acc
        lt_ref[...] = cf
        if nc:
            finish()

    return pl.pallas_call(
        body, name="sb_fwd", grid=(h, s // tq),
        in_specs=[pl.BlockSpec((None, tq, dh), lambda a, i: (a, i, 0)),
                  pl.BlockSpec((None, s, dh), lambda a, i: (a, 0, 0)),
                  pl.BlockSpec((None, s, dh), lambda a, i: (a, 0, 0))] + [HBM] * nc,
        out_specs=[pl.BlockSpec((None, tq, dh), lambda a, i: (a, i, 0)),
                   pl.BlockSpec((None, tq, 1), lambda a, i: (a, i, 0))] + [HBM] * nc,
        out_shape=[jax.ShapeDtypeStruct((h, s, dh), F32), jax.ShapeDtypeStruct((h, s, 1), F32)]
        + _exchange_shapes(comm, False),
        scratch_shapes=_exchange_sems(nc) if nc else [],
        compiler_params=_cparams(("arbitrary", "arbitrary")),
    )(q, k, v, *comm)


def _sb_bwd(q, k, v, ltot, do, comm=()):
    h, s, dh = q.shape
    tq, bk = min(SB_TQ, s), min(SB_BK, s)
    scale = dh ** -0.5
    nc = len(comm)
    hooks = _fused_exchange(True, s // tq, h * (s // tq))

    def body(q_ref, k_ref, v_ref, lt_ref, do_ref, *rest):
        dq_ref, dk_ref, dv_ref = rest[nc:nc + 3]
        if nc:
            finish = hooks(rest[:nc], rest[nc + 3:2 * nc + 3], rest[2 * nc + 3:])
        i = pl.program_id(1)

        @pl.when(i == 0)
        def _():
            dk_ref[...] = jnp.zeros_like(dk_ref)
            dv_ref[...] = jnp.zeros_like(dv_ref)

        q0 = i * tq
        qs = q_ref[...] * scale
        dov = do_ref[...].astype(CDT)
        ltot = lt_ref[...]
        tri_le = (_iota((bk, bk), 0) <= _iota((bk, bk), 1)).astype(CDT)
        tri_lt = (_iota((bk, bk), 0) < _iota((bk, bk), 1)).astype(CDT)

        def step(k0, carry, diag):
            cf, cg, dq = carry
            kb = k_ref[pl.ds(k0, bk), :]
            vb = v_ref[pl.ds(k0, bk), :]
            causal, lb, lk = _sb_block(qs, kb, diag)
            w = jnp.exp(lb + ltot - (cf + _split_dot(lk, tri_le, False, 2)))
            if causal is not None:
                w = jnp.where(causal, w, 0.0)
            g = w * _dot(dov, vb, NT)
            gsum = cg + _split_dot(g, tri_lt, False, 2)
            dz = g - (g + gsum) * jnp.exp(lb)
            if causal is not None:
                dz = jnp.where(causal, dz, 0.0)
            dzc = dz.astype(CDT)
            dk_ref[pl.ds(k0, bk), :] += _dot(dzc, qs, TN)
            dv_ref[pl.ds(k0, bk), :] += _dot(w, dov, TN)
            return (cf + jnp.sum(lk, axis=1, keepdims=True), cg + jnp.sum(g, axis=1, keepdims=True),
                    dq + _dot(dzc, kb))

        carry = (jnp.zeros((tq, 1), F32), jnp.zeros((tq, 1), F32), jnp.zeros((tq, dh), F32))
        carry = lax.fori_loop(0, q0 // bk, lambda jj, c: step(pl.multiple_of(jj * bk, bk), c, None), carry)
        for d in range(tq // bk):
            carry = step(pl.multiple_of(q0 + d * bk, bk), carry, d * bk)
        dq_ref[...] = carry[2] * scale
        if nc:
            finish()

    tile = pl.BlockSpec((None, tq, dh), lambda a, i: (a, i, 0))
    full = pl.BlockSpec((None, s, dh), lambda a, i: (a, 0, 0))
    shp = jax.ShapeDtypeStruct((h, s, dh), F32)
    return pl.pallas_call(
        body, name="sb_bwd", grid=(h, s // tq),
        in_specs=[tile, full, full, pl.BlockSpec((None, tq, 1), lambda a, i: (a, i, 0)), tile] + [HBM] * nc,
        out_specs=[tile, full, full] + [HBM] * nc,
        out_shape=[shp, shp, shp] + _exchange_shapes(comm, True),
        scratch_shapes=_exchange_sems(nc) if nc else [],
        compiler_params=_cparams(("arbitrary", "arbitrary")),
    )(q, k, v, ltot, do, *comm)


def _pick_lane(tile, r):
    return jnp.sum(jnp.where(_iota(tile.shape, 1) == r, tile, 0.0), axis=1, keepdims=True)


def _pick_row(tile, r):
    return jnp.sum(jnp.where(_iota(tile.shape, 0) == r, tile, 0.0), axis=0, keepdims=True)


def _ssd_chunk_setup(c_ref, b_ref, dac_ref, dar_ref, cb_ref, acsc_ref, acsr_ref):
    l = SSD_L
    tdt = F32 if CDT == F32 else BF16
    lower = (_iota((l, l), 1) <= _iota((l, l), 0)).astype(tdt)
    upper = (_iota((l, l), 0) <= _iota((l, l), 1)).astype(tdt)
    cb_ref[...] = _dot(c_ref[...], b_ref[...], NT)
    acsc_ref[...] = _split_dot(dac_ref[...], lower, True, 3)
    acsr_ref[...] = _split_dot(dar_ref[...], upper, False, 3)


def _ssd_fwd(xh, dtc, dac, dar, dsk, xbc):
    hh, s, p = xh.shape
    l, n, g_, r_ = SSD_L, SSD_N, SSD_G, SSD_R
    nc = s // l
    boff = SSD_INNER // n
    coff = boff + g_

    def body(x_ref, dtc_ref, dac_ref, dar_ref, dsk_ref, b_ref, c_ref, y_ref, st_ref,
             state_ref, cb_ref, acsc_ref, acsr_ref):
        c = pl.program_id(1)
        r = pl.program_id(2)

        @pl.when(r == 0)
        def _():
            _ssd_chunk_setup(c_ref, b_ref, dac_ref, dar_ref, cb_ref, acsc_ref, acsr_ref)

        @pl.when(c == 0)
        def _():
            state_ref[r] = jnp.zeros((n, p), F32)

        a_col = _pick_lane(acsc_ref[...], r)
        a_row = _pick_row(acsr_ref[...], r)
        dt_col = _pick_lane(dtc_ref[...], r)
        dsk_h = _pick_lane(dsk_ref[...], r)
        xv = x_ref[...]
        xd = xv * dt_col
        mask = _iota((l, l), 1) <= _iota((l, l), 0)
        decay = jnp.where(mask, jnp.exp(jnp.minimum(a_col - a_row, 0.0)), 0.0)
        w = cb_ref[...] * decay
        hprev = state_ref[r]
        cv = c_ref[...]
        y = _dot(w, xd) + jnp.exp(a_col) * _dot(cv, hprev)
        y_ref[...] = y + dsk_h * xv
        a_end = a_col[l - 1:l, :]
        dte = jnp.exp(a_end - a_col)
        st_ref[...] = hprev
        state_ref[r] = hprev * jnp.exp(a_end) + _dot(b_ref[...], xd * dte, TN)

    return pl.pallas_call(
        body, name="ssd_fwd", grid=(g_, nc, r_),
        in_specs=[pl.BlockSpec((None, l, p), lambda g, c, r: (g * r_ + r, c, 0)),
                  pl.BlockSpec((None, l, r_), lambda g, c, r: (g, c, 0)),
                  pl.BlockSpec((None, l, r_), lambda g, c, r: (g, c, 0)),
                  pl.BlockSpec((None, r_, l), lambda g, c, r: (g, 0, c)),
                  pl.BlockSpec((None, 1, r_), lambda g, c, r: (g, 0, 0)),
                  pl.BlockSpec((l, n), lambda g, c, r: (c, boff + g)),
                  pl.BlockSpec((l, n), lambda g, c, r: (c, coff + g))],
        out_specs=[pl.BlockSpec((None, l, p), lambda g, c, r: (g * r_ + r, c, 0)),
                   pl.BlockSpec((None, None, n, p), lambda g, c, r: (g * r_ + r, c, 0, 0))],
        out_shape=[jax.ShapeDtypeStruct((hh, s, p), F32),
                   jax.ShapeDtypeStruct((hh, nc, n, p), F32)],
        scratch_shapes=[pltpu.VMEM((r_, n, p), F32), pltpu.VMEM((l, l), F32),
                        pltpu.VMEM((l, r_), F32), pltpu.VMEM((r_, l), F32)],
        compiler_params=_cparams(("parallel", "arbitrary", "arbitrary")),
    )(xh, dtc, dac, dar, dsk, xbc, xbc)


def _ssd_bwd(xh, dtc, dac, dar, dsk, xbc, st, dy):
    hh, s, p = xh.shape
    l, n, g_, r_ = SSD_L, SSD_N, SSD_G, SSD_R
    nc = s // l
    boff = SSD_INNER // n
    coff = boff + g_

    def body(x_ref, dtc_ref, dac_ref, dar_ref, dsk_ref, b_ref, c_ref, st_ref, dy_ref,
             dx_ref, dda_ref, dxx_ref, db_ref, dc_ref, dah_ref, ddsk_ref,
             dstate_ref, cb_ref, acsc_ref, acsr_ref):
        c = pl.program_id(1)
        r = pl.program_id(2)

        @pl.when(r == 0)
        def _():
            _ssd_chunk_setup(c_ref, b_ref, dac_ref, dar_ref, cb_ref, acsc_ref, acsr_ref)
            db_ref[...] = jnp.zeros_like(db_ref)
            dc_ref[...] = jnp.zeros_like(dc_ref)
            dda_ref[...] = jnp.zeros_like(dda_ref)
            dxx_ref[...] = jnp.zeros_like(dxx_ref)

        @pl.when(jnp.logical_and(c == 0, r == 0))
        def _():
            dah_ref[...] = jnp.zeros_like(dah_ref)
            ddsk_ref[...] = jnp.zeros_like(ddsk_ref)

        @pl.when(c == 0)
        def _():
            dstate_ref[r] = jnp.zeros((n, p), F32)

        a_col = _pick_lane(acsc_ref[...], r)
        a_row = _pick_row(acsr_ref[...], r)
        dt_col = _pick_lane(dtc_ref[...], r)
        dsk_h = _pick_lane(dsk_ref[...], r)
        xv = x_ref[...]
        dyv = dy_ref[...]
        xd = xv * dt_col
        il = _iota((l, l), 0)
        isx = _iota((l, l), 1)
        decay = jnp.where(isx <= il, jnp.exp(jnp.minimum(a_col - a_row, 0.0)), 0.0)
        cb = cb_ref[...]
        w = cb * decay
        dhn = dstate_ref[r]
        hc = st_ref[...]
        bv = b_ref[...]
        cv = c_ref[...]
        a_end = a_col[l - 1:l, :]
        ea = jnp.exp(a_col)
        dte = jnp.exp(a_end - a_col)

        dx_state = dte * _dot(bv, dhn)
        dxd = _dot(w, dyv, TN) + dx_state
        md = decay * _dot(dyv, xd, NT)
        m = md * cb
        dc_ref[...] += _dot(md, bv) + ea * _dot(dyv, hc, NT)
        db_ref[...] += _dot(md, cv, TN) + dte * _dot(xd, dhn, NT)
        dstate_ref[r] = jnp.exp(a_end) * dhn + _dot(cv, dyv * ea, TN)

        tdt = F32 if CDT == F32 else BF16
        t1 = (isx >= il).astype(tdt)
        yoff = ea * _dot(cv, hc)
        xdx = jnp.sum(xd * dx_state, axis=1, keepdims=True)
        vec = jnp.sum(dyv * yoff, axis=1, keepdims=True) - xdx
        end_term = jnp.sum(xdx, axis=0, keepdims=True) + \
            jnp.exp(a_end) * jnp.sum(jnp.sum(hc * dhn, axis=1, keepdims=True), axis=0, keepdims=True)
        zmat = _split_dot(m, t1, True, 2)
        span = jnp.sum(jnp.where(isx < il, zmat, 0.0), axis=1, keepdims=True)
        rc = _split_dot(jnp.broadcast_to(vec, (l, 128)), t1, True, 2)[:, :1]
        dda = span + rc + end_term
        dxx = jnp.sum(dxd * xv, axis=1, keepdims=True)

        lane = _iota((l, r_), 1) == r
        dda_ref[...] += jnp.where(lane, dda, 0.0)
        dxx_ref[...] += jnp.where(lane, dxx, 0.0)
        dx_ref[...] = dxd * dt_col + dsk_h * dyv
        lane1 = _iota((1, r_), 1) == r
        dah_ref[...] += jnp.where(lane1, jnp.sum(dda * dt_col, axis=0, keepdims=True), 0.0)
        ddsk_ref[...] += jnp.where(
            lane1, jnp.sum(jnp.sum(dyv * xv, axis=1, keepdims=True), axis=0, keepdims=True), 0.0)

    rev = lambda c: nc - 1 - c
    xspec = pl.BlockSpec((None, l, p), lambda g, c, r: (g * r_ + r, rev(c), 0))
    cspec = pl.BlockSpec((None, l, r_), lambda g, c, r: (g, rev(c), 0))
    hspec = pl.BlockSpec((None, 1, r_), lambda g, c, r: (g, 0, 0))
    return pl.pallas_call(
        body, name="ssd_bwd", grid=(g_, nc, r_),
        in_specs=[xspec, cspec, cspec,
                  pl.BlockSpec((None, r_, l), lambda g, c, r: (g, 0, rev(c))),
                  hspec,
                  pl.BlockSpec((l, n), lambda g, c, r: (rev(c), boff + g)),
                  pl.BlockSpec((l, n), lambda g, c, r: (rev(c), coff + g)),
                  pl.BlockSpec((None, None, n, p), lambda g, c, r: (g * r_ + r, rev(c), 0, 0)),
                  xspec],
        out_specs=[xspec, cspec, cspec,
                   pl.BlockSpec((l, n), lambda g, c, r: (rev(c), g)),
                   pl.BlockSpec((l, n), lambda g, c, r: (rev(c), g)),
                   hspec, hspec],
        out_shape=[jax.ShapeDtypeStruct((hh, s, p), F32),
                   jax.ShapeDtypeStruct((g_, s, r_), F32),
                   jax.ShapeDtypeStruct((g_, s, r_), F32),
                   jax.ShapeDtypeStruct((s, g_ * n), F32),
                   jax.ShapeDtypeStruct((s, g_ * n), F32),
                   jax.ShapeDtypeStruct((g_, 1, r_), F32),
                   jax.ShapeDtypeStruct((g_, 1, r_), F32)],
        scratch_shapes=[pltpu.VMEM((r_, n, p), F32), pltpu.VMEM((l, l), F32),
                        pltpu.VMEM((l, r_), F32), pltpu.VMEM((r_, l), F32)],
        compiler_params=_cparams(("parallel", "arbitrary", "arbitrary")),
    )(xh, dtc, dac, dar, dsk, xbc, xbc, st, dy)


CONV_TC = 256
CONV_RC = 512


def _conv_fwd(xp, w, b):
    sp, ch = xp.shape
    s = sp - CONV_PAD
    rc = min(CONV_RC, s)
    base = CONV_PAD - (CONV_K - 1)

    def body(x_ref, w_ref, b_ref, pre_ref, act_ref):
        wv = w_ref[...]
        for t0 in range(0, s, rc):
            acc = jnp.broadcast_to(b_ref[...], (rc, CONV_TC))
            for kk in range(CONV_K):
                acc = acc + wv[kk:kk + 1, :] * x_ref[pl.ds(t0 + base + kk, rc), :]
            pre_ref[pl.ds(t0, rc), :] = acc
            act_ref[pl.ds(t0, rc), :] = _silu(acc)

    shp = jax.ShapeDtypeStruct((s, ch), F32)
    return pl.pallas_call(
        body, name="conv_fwd", grid=(ch // CONV_TC,),
        in_specs=[pl.BlockSpec((sp, CONV_TC), lambda j: (0, j)),
                  pl.BlockSpec((CONV_K, CONV_TC), lambda j: (0, j)),
                  pl.BlockSpec((1, CONV_TC), lambda j: (0, j))],
        out_specs=[pl.BlockSpec((s, CONV_TC), lambda j: (0, j))] * 2,
        out_shape=[shp, shp],
        compiler_params=_cparams(("parallel",)),
    )(xp, w, b)


def _conv_bwd(xp, prep, dactp, w):
    sp, ch = xp.shape
    s = sp - CONV_PAD
    rc = min(CONV_RC, s)
    base = CONV_PAD - (CONV_K - 1)

    def body(x_ref, pre_ref, da_ref, w_ref, dx_ref, dw_ref, db_ref, dpre_ref):
        wv = w_ref[...]
        for t0 in range(0, s, rc):
            dpre_ref[pl.ds(t0, rc), :] = da_ref[pl.ds(t0, rc), :] * _dsilu(pre_ref[pl.ds(t0, rc), :])
        dpre_ref[pl.ds(s, CONV_PAD), :] = jnp.zeros((CONV_PAD, CONV_TC), F32)
        dws = [jnp.zeros((1, CONV_TC), F32) for _ in range(CONV_K)]
        dbs = jnp.zeros((1, CONV_TC), F32)
        for t0 in range(0, s, rc):
            acc = jnp.zeros((rc, CONV_TC), F32)
            dp = dpre_ref[pl.ds(t0, rc), :]
            for kk in range(CONV_K):
                acc = acc + wv[kk:kk + 1, :] * dpre_ref[pl.ds(t0 + CONV_K - 1 - kk, rc), :]
                dws[kk] = dws[kk] + jnp.sum(dp * x_ref[pl.ds(t0 + base + kk, rc), :], axis=0, keepdims=True)
            dbs = dbs + jnp.sum(dp, axis=0, keepdims=True)
            dx_ref[pl.ds(t0, rc), :] = acc.astype(dx_ref.dtype)
        for kk in range(CONV_K):
            dw_ref[kk:kk + 1, :] = dws[kk]
        db_ref[...] = dbs

    col = pl.BlockSpec((sp, CONV_TC), lambda j: (0, j))
    return pl.pallas_call(
        body, name="conv_bwd", grid=(ch // CONV_TC,),
        in_specs=[col, col, col, pl.BlockSpec((CONV_K, CONV_TC), lambda j: (0, j))],
        out_specs=[pl.BlockSpec((s, CONV_TC), lambda j: (0, j)),
                   pl.BlockSpec((CONV_K, CONV_TC), lambda j: (0, j)),
                   pl.BlockSpec((1, CONV_TC), lambda j: (0, j))],
        out_shape=[jax.ShapeDtypeStruct((s, ch), CDT),
                   jax.ShapeDtypeStruct((CONV_K, ch), F32),
                   jax.ShapeDtypeStruct((1, ch), F32)],
        scratch_shapes=[pltpu.VMEM((sp, CONV_TC), F32)],
        compiler_params=_cparams(("parallel",)),
    )(xp, prep, dactp, w)


MEM_TS = 512


def _mem_fwd(mq, kv):
    s = mq.shape[0]
    m = kv.shape[0]
    ts = min(MEM_TS, s)
    scale = MEM_DH ** -0.5

    def body(q_ref, k_ref, v_ref, o_ref):
        sc = _dot(q_ref[...], k_ref[...], NT) * scale
        e = jnp.exp(sc - jnp.max(sc, axis=1, keepdims=True))
        pr = e / jnp.sum(e, axis=1, keepdims=True)
        o_ref[...] = _dot(pr, v_ref[...]).astype(o_ref.dtype)

    return pl.pallas_call(
        body, name="mem_fwd", grid=(MEM_H, s // ts),
        in_specs=[pl.BlockSpec((ts, MEM_DH), lambda a, i: (i, a)),
                  pl.BlockSpec((m, MEM_DH), lambda a, i: (0, a)),
                  pl.BlockSpec((m, MEM_DH), lambda a, i: (0, MEM_H + a))],
        out_specs=pl.BlockSpec((ts, MEM_DH), lambda a, i: (i, a)),
        out_shape=jax.ShapeDtypeStruct((s, MEM_H * MEM_DH), CDT),
        compiler_params=_cparams(("parallel", "arbitrary")),
    )(mq, kv, kv)


def _mem_bwd(mq, kv, do):
    s = mq.shape[0]
    m = kv.shape[0]
    ts = min(MEM_TS, s)
    scale = MEM_DH ** -0.5

    def body(q_ref, k_ref, v_ref, do_ref, dq_ref, dk_ref, dv_ref):
        i = pl.program_id(1)

        @pl.when(i == 0)
        def _():
            dk_ref[...] = jnp.zeros_like(dk_ref)
            dv_ref[...] = jnp.zeros_like(dv_ref)

        qv, kb, vb, dov = q_ref[...], k_ref[...], v_ref[...], do_ref[...]
        sc = _dot(qv, kb, NT) * scale
        e = jnp.exp(sc - jnp.max(sc, axis=1, keepdims=True))
        pr = e / jnp.sum(e, axis=1, keepdims=True)
        dp = _dot(dov, vb, NT)
        ds = pr * (dp - jnp.sum(dp * pr, axis=1, keepdims=True)) * scale
        dq_ref[...] = _dot(ds, kb).astype(dq_ref.dtype)
        dk_ref[...] += _dot(ds, qv, TN)
        dv_ref[...] += _dot(pr, dov, TN)

    tile = pl.BlockSpec((ts, MEM_DH), lambda a, i: (i, a))
    kvo = pl.BlockSpec((m, MEM_DH), lambda a, i: (0, a))
    return pl.pallas_call(
        body, name="mem_bwd", grid=(MEM_H, s // ts),
        in_specs=[tile, kvo, pl.BlockSpec((m, MEM_DH), lambda a, i: (0, MEM_H + a)), tile],
        out_specs=[tile, kvo, kvo],
        out_shape=[jax.ShapeDtypeStruct((s, MEM_H * MEM_DH), CDT),
                   jax.ShapeDtypeStruct((m, MEM_H * MEM_DH), F32),
                   jax.ShapeDtypeStruct((m, MEM_H * MEM_DH), F32)],
        compiler_params=_cparams(("parallel", "arbitrary")),
    )(mq, kv, kv, do)


def _heads(t, nh, dh):
    return t.reshape(t.shape[0], nh, dh).transpose(1, 0, 2)


def _unheads(t):
    return t.transpose(1, 0, 2).reshape(t.shape[1], t.shape[0] * t.shape[2])


def _group_cols(t):
    return t.reshape(t.shape[0], SSD_G, SSD_R).transpose(1, 0, 2)


def _pad_cols(t, width):
    return jnp.pad(t, ((0, 0), (0, width - t.shape[1])))


def _full_weight(name, gathered):
    if name in COL_SHARDED:
        return gathered.transpose(1, 0, 2).reshape(gathered.shape[1], N_DEV * gathered.shape[2])
    return gathered.reshape(N_DEV * gathered.shape[1], gathered.shape[2])


def _grad_payload(name, g):
    if name in COL_SHARDED:
        return g.reshape(g.shape[0], N_DEV, g.shape[1] // N_DEV).transpose(1, 0, 2)
    return g.reshape(N_DEV, g.shape[0] // N_DEV, g.shape[1])


def _local_step(x, mem, tgt, p, wt, shards=None):
    s, d = x.shape
    wt = dict(wt)
    c1, c2, c3, c4, c5 = 3 * d, 3 * d + SSD_INNER, 3 * d + SSD_INNER + CONV_DIM, \
        3 * d + SSD_INNER + CONV_DIM + SSD_H, 3 * d + SSD_INNER + CONV_DIM + SSD_H + d
    w_in = wt["w_in"]
    w_seg = [w_in[:, :c1], w_in[:, c1:c2], w_in[:, c2:c3], _pad_cols(w_in[:, c3:c4], DT_PAD),
             w_in[:, c4:c5], w_in[:, c5:]]
    seg_dtype = [CDT, F32, F32, F32, CDT, F32]
    seg_name = ["qkv", "z", "xbc", "dt", "mq", "gl"]

    u = _rows(lambda xv, g: _rms(xv, g), [x], [p["norm_mix_pre"]], [(d, CDT)], ts=512, name="f_norm_pre")[0]
    qkv, z, xbc_raw, dt_raw, mq, gl = [
        _mm(u, w_seg[i], out_dtype=seg_dtype[i], name="f_in_" + seg_name[i]) for i in range(6)]

    bias128 = _pad_cols(p["dt_bias"], DT_PAD)
    alog128 = _pad_cols(p["a_log"], DT_PAD)

    def dt_fn(dtr, bias, alog):
        dt = _softplus(dtr + bias)
        return dt, dt * (-jnp.exp(alog))

    dt128, da128 = _rows(dt_fn, [dt_raw], [bias128, alog128], [(DT_PAD, F32), (DT_PAD, F32)],
                         ts=512, name="f_dt")
    dtc = _group_cols(dt128[:, :SSD_H])
    dac = _group_cols(da128[:, :SSD_H])
    dar = dac.transpose(0, 2, 1)
    dsk = p["d_skip"].reshape(SSD_G, 1, SSD_R)

    xp = jnp.pad(xbc_raw, ((CONV_PAD, 0), (0, 0)))
    conv_w, conv_b = p["conv_w"], p["conv_b"]
    pre, xbc = _conv_fwd(xp, conv_w, conv_b)
    xh = _heads(xbc[:, :SSD_INNER], SSD_H, SSD_P)
    y_h, st = _ssd_fwd(xh, dtc, dac, dar, dsk, xbc)
    y_core = _unheads(y_h)

    def group_norm_fwd(yv, zv, wn):
        y2 = yv * _silu(zv)
        gw = SSD_INNER // SSD_G
        outs = []
        for gi in range(SSD_G):
            seg = y2[:, gi * gw:(gi + 1) * gw]
            outs.append(_rms(seg, wn[:, gi * gw:(gi + 1) * gw]))
        return jnp.concatenate(outs, axis=1)

    y_ssd = _rows(group_norm_fwd, [y_core, z], [p["ssd_norm"]], [(SSD_INNER, CDT)], ts=256, name="f_ssd_post")[0]

    q_h = _heads(qkv[:, :d], SB_H, SB_DH)
    k_h = _heads(qkv[:, d:2 * d], SB_H, SB_DH)
    v_h = _heads(qkv[:, 2 * d:], SB_H, SB_DH)
    o_h, lt_h, *late = _sb_fwd(q_h, k_h, v_h, tuple(shards) if shards is not None else ())
    for n, gth in zip(LATE_W, late):
        wt[n] = _full_weight(n, gth)
    y_sb = _unheads(o_h).astype(CDT)

    mu = _rows(lambda mv, g: _rms(mv, g), [mem], [p["norm_mem"]], [(d, CDT)], ts=256, name="f_norm_mem")[0]
    kv = _mm(mu, wt["w_mem_kv"], out_dtype=CDT, name="f_mem_kv")
    y_mem = _mem_fwd(mq, kv)

    p_sb = _mm(y_sb, wt["w_sb_out"], name="f_sb_out")
    p_ssd = _mm(y_ssd, wt["w_ssd_out"], name="f_ssd_out")
    p_mem = _mm(y_mem, wt["w_mem_out"], name="f_mem_out")

    def merge_fn(glv, a, b, c):
        return (_sigmoid(glv[:, :d]) * a + _sigmoid(glv[:, d:2 * d]) * b + _sigmoid(glv[:, 2 * d:]) * c)

    merged = _rows(merge_fn, [gl, p_sb, p_ssd, p_mem], [], [(d, CDT)], ts=256, name="f_merge")[0]
    mix = _mm(merged, wt["w_o"], name="f_w_o")

    def mid_fn(xv, mixv, g_post, g_pre):
        h1 = xv + _rms(mixv, g_post)
        return h1, _rms(h1, g_pre)

    h1, u2 = _rows(mid_fn, [x, mix], [p["norm_mix_post"], p["norm_mlp_pre"]], [(d, F32), (d, CDT)],
                   ts=512, name="f_mid")
    a1 = _mm(u2, wt["w_up"], name="f_up")
    act = _rows(lambda a: jnp.square(jnp.maximum(a, 0.0)), [a1], [], [(a1.shape[1], CDT)], ts=256, name="f_act")[0]
    ff = _mm(act, wt["w_down"], name="f_down")

    def loss_fn(h1v, ffv, tv, g):
        diff = h1v + _rms(ffv, g) - tv
        tot = jnp.sum(jnp.sum(diff * diff, axis=1, keepdims=True), axis=0, keepdims=True)
        return diff * (1.0 / d), tot

    dh2, loss_acc = _rows(loss_fn, [h1, ff, tgt], [p["norm_mlp_post"]], [(d, F32)], [(1, 128)],
                          ts=512, name="f_loss")
    loss = loss_acc[:, :1] * (0.5 / d)

    sg = {}

    def b_post(ffv, dyv, g):
        dx, dg = _rms_bwd(ffv, g, dyv)
        return dx, dg

    d_ff, sg["norm_mlp_post"] = _rows(b_post, [ff, dh2], [p["norm_mlp_post"]], [(d, CDT)], [(1, d)],
                                      ts=512, name="b_norm_mlp_post")
    dact = _mm(d_ff, wt["w_down"], tb=True, name="b_down_x")
    gw = {"w_down": _mm(act, d_ff, ta=True, name="b_down_w")}
    da1 = _rows(lambda dv, a: dv * 2.0 * jnp.maximum(a, 0.0), [dact, a1], [], [(a1.shape[1], CDT)],
                ts=256, name="b_act")[0]
    du2 = _mm(da1, wt["w_up"], tb=True, name="b_up_x")
    gw["w_up"] = _mm(u2, da1, ta=True, name="b_up_w")

    def b_mid(h1v, du2v, dh2v, mixv, g_pre, g_post):
        dxa, dga = _rms_bwd(h1v, g_pre, du2v)
        dh1 = dh2v + dxa
        dmix, dgb = _rms_bwd(mixv, g_post, dh1)
        return dh1, dmix, dga, dgb

    dh1, dmix, sg["norm_mlp_pre"], sg["norm_mix_post"] = _rows(
        b_mid, [h1, du2, dh2, mix], [p["norm_mlp_pre"], p["norm_mix_post"]],
        [(d, F32), (d, CDT)], [(1, d), (1, d)], ts=256, name="b_mid")
    dmerged = _mm(dmix, wt["w_o"], tb=True, name="b_w_o_x")
    gw["w_o"] = _mm(merged, dmix, ta=True, name="b_w_o_w")

    def b_merge(dm, glv, a, b, c):
        outs, dgl = [], []
        for i, br in enumerate((a, b, c)):
            gt = _sigmoid(glv[:, i * d:(i + 1) * d])
            outs.append(gt * dm)
            dgl.append(dm * br * gt * (1.0 - gt))
        return outs[0], outs[1], outs[2], jnp.concatenate(dgl, axis=1)

    dp_sb, dp_ssd, dp_mem, dgl = _rows(b_merge, [dmerged, gl, p_sb, p_ssd, p_mem], [],
                                       [(d, CDT), (d, CDT), (d, CDT), (3 * d, CDT)], ts=256, name="b_merge")
    dy_sb = _mm(dp_sb, wt["w_sb_out"], tb=True, name="b_sb_out_x")
    gw["w_sb_out"] = _mm(y_sb, dp_sb, ta=True, name="b_sb_out_w")
    dy_ssd = _mm(dp_ssd, wt["w_ssd_out"], tb=True, name="b_ssd_out_x")
    gw["w_ssd_out"] = _mm(y_ssd, dp_ssd, ta=True, name="b_ssd_out_w")
    dy_mem = _mm(dp_mem, wt["w_mem_out"], tb=True, out_dtype=CDT, name="b_mem_out_x")
    gw["w_mem_out"] = _mm(y_mem, dp_mem, ta=True, name="b_mem_out_w")

    dmq, dk_m, dv_m = _mem_bwd(mq, kv, dy_mem)
    dkv = jnp.concatenate([dk_m, dv_m], axis=1).astype(CDT)
    gw["w_mem_kv"] = _mm(mu, dkv, ta=True, name="b_mem_kv_w")
    dmu = _mm(dkv, wt["w_mem_kv"], tb=True, name="b_mem_kv_x")
    sg["norm_mem"] = _rows(lambda mv, dv, g: _rms_bwd(mv, g, dv)[1], [mem, dmu], [p["norm_mem"]], [], [(1, d)],
                           ts=256, name="b_norm_mem")[0]

    payloads = tuple(_grad_payload(n, gw[n]) for n in LATE_W) if shards is not None else ()
    dq_h, dk_h, dv_h, *received = _sb_bwd(q_h, k_h, v_h, lt_h, _heads(dy_sb, SB_H, SB_DH), payloads)
    dqkv = jnp.concatenate([_unheads(dq_h), _unheads(dk_h), _unheads(dv_h)], axis=1).astype(CDT)

    def group_norm_bwd(dyo, yv, zv, wn):
        sz = _silu(zv)
        y2 = yv * sz
        gw_ = SSD_INNER // SSD_G
        dy2, dwn = [], []
        for gi in range(SSD_G):
            sl = slice(gi * gw_, (gi + 1) * gw_)
            dxs, dgs = _rms_bwd(y2[:, sl], wn[:, sl], dyo[:, sl])
            dy2.append(dxs)
            dwn.append(dgs)
        dy2 = jnp.concatenate(dy2, axis=1)
        return dy2 * sz, dy2 * yv * _dsilu(zv), jnp.concatenate(dwn, axis=1)

    dy_core, dz, sg["ssd_norm"] = _rows(group_norm_bwd, [dy_ssd, y_core, z], [p["ssd_norm"]],
                                        [(SSD_INNER, F32), (SSD_INNER, CDT)], [(1, SSD_INNER)],
                                        ts=256, name="b_ssd_post")
    dxh, dda, dxx, d_b, d_c, dah, ddsk = _ssd_bwd(xh, dtc, dac, dar, dsk, xbc, st, _heads(dy_core, SSD_H, SSD_P))
    sg["d_skip"] = ddsk.reshape(1, SSD_H)
    sg["a_log"] = dah.reshape(1, SSD_H) * (-jnp.exp(p["a_log"]))

    def b_dt(ddav, dxxv, dtr, bias, alog):
        ddt = ddav * (-jnp.exp(alog)) + dxxv
        draw = ddt * _sigmoid(dtr + bias)
        return draw, jnp.sum(draw, axis=0, keepdims=True)

    ungroup = lambda t: _pad_cols(t.transpose(1, 0, 2).reshape(s, SSD_H), DT_PAD)
    ddt_raw, dbias128 = _rows(b_dt, [ungroup(dda), ungroup(dxx), dt_raw], [bias128, alog128],
                              [(DT_PAD, CDT)], [(1, DT_PAD)], ts=512, name="b_dt")
    sg["dt_bias"] = dbias128[:, :SSD_H]

    dxbc = jnp.concatenate([_unheads(dxh), d_b, d_c], axis=1)
    back = ((0, CONV_PAD), (0, 0))
    dxbc_raw, sg["conv_w"], sg["conv_b"] = _conv_bwd(xp, jnp.pad(pre, back), jnp.pad(dxbc, back), conv_w)

    dseg = [dqkv, dz, dxbc_raw, ddt_raw, dmq, dgl]
    dus = [_mm(dseg[i], w_seg[i], tb=True, name="b_in_x_" + seg_name[i]) for i in range(6)]
    dws = [_mm(u, dseg[i], ta=True, name="b_in_w_" + seg_name[i]) for i in range(6)]
    dws[3] = dws[3][:, :SSD_H]
    gw["w_in"] = jnp.concatenate(dws, axis=1)

    def b_pre(xv, dh1v, d0, d1, d2, d3, d4, d5, g):
        dx, dg = _rms_bwd(xv, g, d0 + d1 + d2 + d3 + d4 + d5)
        return dh1v + dx, dg

    grad_x, sg["norm_mix_pre"] = _rows(b_pre, [x, dh1] + dus, [p["norm_mix_pre"]], [(d, F32)], [(1, d)],
                                       ts=256, name="b_norm_pre")
    return loss, grad_x, gw, sg, (received if shards is not None else None)


HBM = pl.BlockSpec(memory_space=pltpu.HBM)
MESH = pl.DeviceIdType.MESH


def _me_and_peers():
    x, y, c = lax.axis_index("x"), lax.axis_index("y"), lax.axis_index("c")
    me = 4 * x + 2 * y + c
    peers = [(x, y, 1 - c), (1 - x, y, c), (x, 1 - y, c), (1 - x, 1 - y, c),
             (1 - x, y, 1 - c), (x, 1 - y, 1 - c), (1 - x, 1 - y, 1 - c)]
    return me, peers


def _peer_index(peer):
    return 4 * peer[0] + 2 * peer[1] + peer[2]


def _exchange_copies(ins, outs, send_sems, recv_sems, local_sems, scatter):
    me, peers = _me_and_peers()
    copies = []
    for a in range(len(ins)):
        own = ins[a].at[me] if scatter else ins[a]
        copies.append(pltpu.make_async_copy(own, outs[a].at[me], local_sems.at[a]))
        for kk, peer in enumerate(peers):
            src = ins[a].at[_peer_index(peer)] if scatter else ins[a]
            copies.append(pltpu.make_async_remote_copy(
                src_ref=src, dst_ref=outs[a].at[me],
                send_sem=send_sems.at[a, kk], recv_sem=recv_sems.at[a, kk],
                device_id=peer, device_id_type=MESH))
    return copies


def _exchange_shapes(ins, scatter):
    return [jax.ShapeDtypeStruct(t.shape if scatter else (N_DEV,) + t.shape, t.dtype) for t in ins]


def _exchange_sems(n):
    return [pltpu.SemaphoreType.DMA((n, N_DEV - 1)), pltpu.SemaphoreType.DMA((n, N_DEV - 1)),
            pltpu.SemaphoreType.DMA((n,))]


def _exchange(ins, scatter, name):
    n = len(ins)

    def body(*refs):
        copies = _exchange_copies(refs[:n], refs[n:2 * n], *refs[2 * n:], scatter)
        for cp in copies:
            cp.start()
        for cp in copies:
            cp.wait()

    return pl.pallas_call(
        body, name=name,
        in_specs=[HBM] * n, out_specs=[HBM] * n,
        out_shape=_exchange_shapes(ins, scatter),
        scratch_shapes=_exchange_sems(n),
        compiler_params=pltpu.CompilerParams(has_side_effects=True),
    )(*ins)


def _all_reduce_small(v, name):
    r, c = v.shape

    def body(v_ref, o_ref, buf, send_sems, recv_sems):
        me, peers = _me_and_peers()
        buf[me] = v_ref[...]
        copies = []
        for kk, peer in enumerate(peers):
            cp = pltpu.make_async_remote_copy(
                src_ref=v_ref, dst_ref=buf.at[me],
                send_sem=send_sems.at[kk], recv_sem=recv_sems.at[kk],
                device_id=peer, device_id_type=MESH)
            cp.start()
            copies.append(cp)
        for cp in copies:
            cp.wait()
        acc = buf[0]
        for i in range(1, N_DEV):
            acc = acc + buf[i]
        o_ref[...] = acc

    return pl.pallas_call(
        body, name=name,
        in_specs=[pl.BlockSpec(memory_space=pltpu.VMEM)],
        out_specs=pl.BlockSpec(memory_space=pltpu.VMEM),
        out_shape=jax.ShapeDtypeStruct((r, c), F32),
        scratch_shapes=[pltpu.VMEM((N_DEV, r, c), F32),
                        pltpu.SemaphoreType.DMA((N_DEV - 1,)), pltpu.SemaphoreType.DMA((N_DEV - 1,))],
        compiler_params=pltpu.CompilerParams(has_side_effects=True),
    )(v)


def _adamw_math(g, w, m, v):
    m2 = ADAM_B1 * m + (1.0 - ADAM_B1) * g
    v2 = ADAM_B2 * v + (1.0 - ADAM_B2) * jnp.square(g)
    m_hat = m2 / (1.0 - ADAM_B1 ** ADAM_STEP)
    v_hat = v2 / (1.0 - ADAM_B2 ** ADAM_STEP)
    delta = -ADAM_LR * (m_hat / (jnp.sqrt(v_hat) + ADAM_EPS) + ADAM_WD * w)
    return delta, m2, v2


def _adamw_reduce(parts, w, m, v, name):
    r, c = w.shape
    tr = _pick(r, (128, 64, 32, 16, 8))

    def body(p_ref, w_ref, m_ref, v_ref, g_ref, d_ref, m2_ref, v2_ref):
        g = p_ref[0].astype(F32)
        for i in range(1, N_DEV):
            g = g + p_ref[i].astype(F32)
        delta, m2, v2 = _adamw_math(g, w_ref[...], m_ref[...], v_ref[...])
        g_ref[...] = g
        d_ref[...] = delta
        m2_ref[...] = m2
        v2_ref[...] = v2

    tile = pl.BlockSpec((tr, c), lambda i: (i, 0))
    shp = jax.ShapeDtypeStruct((r, c), F32)
    return pl.pallas_call(
        body, name=name, grid=(r // tr,),
        in_specs=[pl.BlockSpec((N_DEV, tr, c), lambda i: (0, i, 0)), tile, tile, tile],
        out_specs=[tile] * 4, out_shape=[shp] * 4,
        compiler_params=_cparams(("parallel",)),
    )(parts, w, m, v)


def _adamw_plain(g, w, m, v, name):
    def body(g_ref, w_ref, m_ref, v_ref, d_ref, m2_ref, v2_ref):
        delta, m2, v2 = _adamw_math(g_ref[...], w_ref[...], m_ref[...], v_ref[...])
        d_ref[...] = delta
        m2_ref[...] = m2
        v2_ref[...] = v2

    spec = pl.BlockSpec(memory_space=pltpu.VMEM)
    shp = jax.ShapeDtypeStruct(g.shape, F32)
    return pl.pallas_call(
        body, name=name, in_specs=[spec] * 4, out_specs=[spec] * 3, out_shape=[shp] * 3,
    )(g, w, m, v)


def _cast_shard(w, name):
    r = w.shape[0]
    return _rows(lambda t: t, [w], [], [(w.shape[1], CDT)], ts=_pick(r, (256, 128)), name=name)[0]


BIG = ["w_in", "w_mem_kv", "w_up", "w_sb_out", "w_ssd_out", "w_mem_out", "w_o", "w_down"]
LATE_W = BIG[1:]
COL_SHARDED = ("w_in", "w_mem_kv", "w_up")
SMALL = ["norm_mix_pre", "conv_b", "dt_bias", "a_log", "d_skip", "ssd_norm", "norm_mem",
         "norm_mix_post", "norm_mlp_pre", "norm_mlp_post"]
ALL_W = ["norm_mix_pre", "w_in", "conv_w", "conv_b", "dt_bias", "a_log", "d_skip", "ssd_norm", "norm_mem",
         "w_mem_kv", "w_sb_out", "w_ssd_out", "w_mem_out", "w_o", "norm_mix_post", "norm_mlp_pre", "w_up",
         "w_down", "norm_mlp_post"]
LANES = 128


def _pack_rows(vecs):
    parts, offs, off = [], [], 0
    for t in vecs:
        flat = t.reshape(-1)
        n = flat.shape[0]
        rows = -(-n // (8 * LANES)) * 8
        parts.append(jnp.pad(flat, (0, rows * LANES - n)).reshape(rows, LANES))
        offs.append((off, n))
        off += rows
    return jnp.concatenate(parts, axis=0), offs


def _unpack_rows(packed, offs, shapes):
    out = []
    for (off, n), shape in zip(offs, shapes):
        rows = -(-n // (8 * LANES)) * 8
        out.append(packed[off:off + rows].reshape(-1)[:n].reshape(shape))
    return out


def kernel(x, mem, norm_mix_pre, w_in, conv_w, conv_b, dt_bias, a_log, d_skip, ssd_norm, norm_mem, w_mem_kv, w_sb_out, w_ssd_out, w_mem_out, w_o, norm_mix_post, norm_mlp_pre, w_up, w_down, norm_mlp_post, loss_target, m_norm_mix_pre, m_w_in, m_conv_w, m_conv_b, m_dt_bias, m_a_log, m_d_skip, m_ssd_norm, m_norm_mem, m_w_mem_kv, m_w_sb_out, m_w_ssd_out, m_w_mem_out, m_w_o, m_norm_mix_post, m_norm_mlp_pre, m_w_up, m_w_down, m_norm_mlp_post, v_norm_mix_pre, v_w_in, v_conv_w, v_conv_b, v_dt_bias, v_a_log, v_d_skip, v_ssd_norm, v_norm_mem, v_w_mem_kv, v_w_sb_out, v_w_ssd_out, v_w_mem_out, v_w_o, v_norm_mix_post, v_norm_mlp_pre, v_w_up, v_w_down, v_norm_mlp_post):
    wd = dict(norm_mix_pre=norm_mix_pre, w_in=w_in, conv_w=conv_w, conv_b=conv_b, dt_bias=dt_bias, a_log=a_log,
              d_skip=d_skip, ssd_norm=ssd_norm, norm_mem=norm_mem, w_mem_kv=w_mem_kv, w_sb_out=w_sb_out,
              w_ssd_out=w_ssd_out, w_mem_out=w_mem_out, w_o=w_o, norm_mix_post=norm_mix_post,
              norm_mlp_pre=norm_mlp_pre, w_up=w_up, w_down=w_down, norm_mlp_post=norm_mlp_post)
    md = dict(norm_mix_pre=m_norm_mix_pre, w_in=m_w_in, conv_w=m_conv_w, conv_b=m_conv_b, dt_bias=m_dt_bias,
              a_log=m_a_log, d_skip=m_d_skip, ssd_norm=m_ssd_norm, norm_mem=m_norm_mem, w_mem_kv=m_w_mem_kv,
              w_sb_out=m_w_sb_out, w_ssd_out=m_w_ssd_out, w_mem_out=m_w_mem_out, w_o=m_w_o,
              norm_mix_post=m_norm_mix_post, norm_mlp_pre=m_norm_mlp_pre, w_up=m_w_up, w_down=m_w_down,
              norm_mlp_post=m_norm_mlp_post)
    vd = dict(norm_mix_pre=v_norm_mix_pre, w_in=v_w_in, conv_w=v_conv_w, conv_b=v_conv_b, dt_bias=v_dt_bias,
              a_log=v_a_log, d_skip=v_d_skip, ssd_norm=v_ssd_norm, norm_mem=v_norm_mem, w_mem_kv=v_w_mem_kv,
              w_sb_out=v_w_sb_out, w_ssd_out=v_w_ssd_out, w_mem_out=v_w_mem_out, w_o=v_w_o,
              norm_mix_post=v_norm_mix_post, norm_mlp_pre=v_norm_mlp_pre, w_up=v_w_up, w_down=v_w_down,
              norm_mlp_post=v_norm_mlp_post)
    me = 4 * lax.axis_index("x") + 2 * lax.axis_index("y") + lax.axis_index("c")

    shards = {n: _cast_shard(wd[n][0], "cast_" + n) for n in BIG}
    wt = {"w_in": _full_weight("w_in", _exchange([shards["w_in"]], False, "gather_w_in")[0])}

    cw = wd["conv_w"][0]
    ch = cw.shape[1]
    cw_slot = lax.dynamic_update_slice(jnp.zeros((CONV_K, N_DEV * ch), F32), cw, (0, me * ch))
    cw_packed, cw_offs = _pack_rows([cw_slot])
    conv_w_full = _unpack_rows(_all_reduce_small(cw_packed, "gather_conv_w"), cw_offs, [(CONV_K, N_DEV * ch)])[0]

    p = {n: wd[n] for n in SMALL}
    p["conv_w"] = conv_w_full
    loss, grad_x, gw, sg, late_received = _local_step(x[0], mem[0], loss_target[0], p, wt,
                                                      [shards[n] for n in LATE_W])

    received = dict(zip(LATE_W, late_received))
    received["w_in"] = _exchange([_grad_payload("w_in", gw["w_in"])], True, "scatter_w_in")[0]

    small_names = SMALL + ["conv_w"]
    packed, offs = _pack_rows([sg[n] for n in small_names] + [loss])
    reduced = _all_reduce_small(packed, "reduce_small")
    small_red = _unpack_rows(reduced, offs, [sg[n].shape for n in small_names] + [(1, 1)])
    loss_out = small_red[-1].reshape(())
    sgr = dict(zip(small_names, small_red[:-1]))
    sgr["conv_w"] = lax.dynamic_slice(sgr["conv_w"], (0, me * ch), (CONV_K, ch))

    grads, deltas, new_m, new_v = {}, {}, {}, {}
    for n in BIG:
        g, dl, m2, v2 = _adamw_reduce(received[n], wd[n][0], md[n][0], vd[n][0], "adamw_" + n)
        grads[n], deltas[n], new_m[n], new_v[n] = g[None], dl[None], m2[None], v2[None]
    g_p, g_offs = _pack_rows([sgr[n] for n in small_names])
    w_p, _ = _pack_rows([wd[n] for n in small_names])
    m_p, _ = _pack_rows([md[n] for n in small_names])
    v_p, _ = _pack_rows([vd[n] for n in small_names])
    d_p, m2_p, v2_p = _adamw_plain(g_p, w_p, m_p, v_p, "adamw_small")
    shapes = [wd[n].shape for n in small_names]
    for n, g, dl, m2, v2 in zip(small_names, [sgr[n].reshape(wd[n].shape) for n in small_names],
                                _unpack_rows(d_p, g_offs, shapes), _unpack_rows(m2_p, g_offs, shapes),
                                _unpack_rows(v2_p, g_offs, shapes)):
        grads[n], deltas[n], new_m[n], new_v[n] = g, dl, m2, v2

    return (loss_out, grad_x[None], *[grads[n] for n in ALL_W], *[deltas[n] for n in ALL_W],
            *[new_m[n] for n in ALL_W], *[new_v[n] for n in ALL_W])
```

```python
import functools

import jax
import jax.numpy as jnp
from jax import lax
from jax.experimental import pallas as pl
from jax.experimental.pallas import tpu as pltpu

F32 = jnp.float32
BF16 = jnp.bfloat16
CDT = jnp.bfloat16
EPS = 1e-6
VMEM_LIMIT = 56 * 1024 * 1024

N_DEV = 8
D_MODEL = 1024
SB_H, SB_DH = 16, 64
SSD_G, SSD_R, SSD_P, SSD_N, SSD_L = 4, 8, 64, 128, 128
SSD_H = SSD_G * SSD_R
SSD_INNER = SSD_H * SSD_P
CONV_K = 4
CONV_DIM = SSD_INNER + 2 * SSD_G * SSD_N
MEM_H, MEM_DH = 4, 256
DT_PAD = 128
SB_TQ, SB_BK = 512, 256
CONV_PAD = 8
MM_TILE, MM_TILE_K = 1024, 2048

ADAM_LR, ADAM_B1, ADAM_B2, ADAM_EPS, ADAM_WD, ADAM_STEP = 0.001, 0.9, 0.999, 1e-08, 0.01, 10

NT = (((1,), (1,)), ((), ()))
TN = (((0,), (0,)), ((), ()))
NN = (((1,), (0,)), ((), ()))


def _cparams(sem=None):
    return pltpu.CompilerParams(dimension_semantics=sem, vmem_limit_bytes=VMEM_LIMIT)


def _pick(n, cands):
    for c in cands:
        if n % c == 0:
            return c
    return n


def _dot(a, b, dims=NN):
    return lax.dot_general(a.astype(CDT), b.astype(CDT), dims, preferred_element_type=F32)


def _split_dot(x, t, left, pieces):
    if CDT == F32:
        return lax.dot_general(t, x, NN, preferred_element_type=F32) if left else \
            lax.dot_general(x, t, NN, preferred_element_type=F32)
    acc = None
    rem = x
    for _ in range(pieces):
        hi = rem.astype(BF16)
        rem = rem - hi.astype(F32)
        d = lax.dot_general(t, hi, NN, preferred_element_type=F32) if left else \
            lax.dot_general(hi, t, NN, preferred_element_type=F32)
        acc = d if acc is None else acc + d
    return acc


def _iota(shape, dim):
    return lax.broadcasted_iota(jnp.int32, shape, dim)


def _sigmoid(x):
    return 1.0 / (1.0 + jnp.exp(-x))


def _silu(x):
    return x * _sigmoid(x)


def _dsilu(x):
    s = _sigmoid(x)
    return s * (1.0 + x * (1.0 - s))


def _softplus(x):
    return jnp.maximum(x, 0.0) + jnp.log(1.0 + jnp.exp(-jnp.abs(x)))


def _rms(x, g):
    r = lax.rsqrt(jnp.mean(x * x, axis=-1, keepdims=True) + EPS)
    return x * r * g


def _rms_bwd(x, g, dy):
    r = lax.rsqrt(jnp.mean(x * x, axis=-1, keepdims=True) + EPS)
    n = x * r
    dn = dy * g
    dx = r * (dn - n * jnp.mean(dn * n, axis=-1, keepdims=True))
    dg = jnp.sum(dy * n, axis=0, keepdims=True)
    return dx, dg


def _mm(a, b, *, ta=False, tb=False, out_dtype=F32, name):
    m = a.shape[1] if ta else a.shape[0]
    k = a.shape[0] if ta else a.shape[1]
    n = b.shape[0] if tb else b.shape[1]
    assert k == (b.shape[1] if tb else b.shape[0])
    bm = _pick(m, (MM_TILE, 512, 256, 128))
    bn = _pick(n, (MM_TILE, 512, 256, 128))
    bk = _pick(k, (MM_TILE_K, 1024, 512, 256, 128))
    nk = k // bk
    dims = (((0 if ta else 1,), (1 if tb else 0,)), ((), ()))

    def body(a_ref, b_ref, o_ref, acc_ref):
        part = _dot(a_ref[...], b_ref[...], dims)
        if nk == 1:
            o_ref[...] = part.astype(o_ref.dtype)
            return
        kk = pl.program_id(2)

        @pl.when(kk == 0)
        def _():
            acc_ref[...] = part

        @pl.when(jnp.logical_and(kk > 0, kk < nk - 1))
        def _():
            acc_ref[...] += part

        @pl.when(kk == nk - 1)
        def _():
            o_ref[...] = (acc_ref[...] + part).astype(o_ref.dtype)

    a_spec = pl.BlockSpec((bk, bm), lambda i, j, kk: (kk, i)) if ta else \
        pl.BlockSpec((bm, bk), lambda i, j, kk: (i, kk))
    b_spec = pl.BlockSpec((bn, bk), lambda i, j, kk: (j, kk)) if tb else \
        pl.BlockSpec((bk, bn), lambda i, j, kk: (kk, j))
    return pl.pallas_call(
        body, name=name,
        grid=(m // bm, n // bn, nk),
        in_specs=[a_spec, b_spec],
        out_specs=pl.BlockSpec((bm, bn), lambda i, j, kk: (i, j)),
        out_shape=jax.ShapeDtypeStruct((m, n), out_dtype),
        scratch_shapes=[pltpu.VMEM((bm, bn) if nk > 1 else (8, 128), F32)],
        compiler_params=_cparams(("parallel", "parallel", "arbitrary")),
    )(a, b)


def _rows(fn, tiled, params, outs, accs=(), *, ts, name):
    s = tiled[0].shape[0]
    ts = min(ts, s)
    assert s % ts == 0
    nt, npar, no, na = len(tiled), len(params), len(outs), len(accs)

    def body(*refs):
        i = pl.program_id(0)
        vals = [r[...] for r in refs[:nt + npar]]
        res = fn(*vals)
        if not isinstance(res, (tuple, list)):
            res = (res,)
        orefs = refs[nt + npar:nt + npar + no]
        arefs = refs[nt + npar + no:]
        for r_, val in zip(orefs, res[:no]):
            r_[...] = val.astype(r_.dtype)
        if na:
            @pl.when(i == 0)
            def _():
                for r_ in arefs:
                    r_[...] = jnp.zeros_like(r_)

            for r_, val in zip(arefs, res[no:]):
                r_[...] += jnp.broadcast_to(val, r_.shape)

    in_specs = [pl.BlockSpec((ts, a.shape[1]), lambda i: (i, 0)) for a in tiled]
    in_specs += [pl.BlockSpec(p.shape, lambda i: (0, 0)) for p in params]
    out_specs = [pl.BlockSpec((ts, w), lambda i: (i, 0)) for (w, _) in outs]
    out_specs += [pl.BlockSpec(shape, lambda i: (0, 0)) for shape in accs]
    out_shape = [jax.ShapeDtypeStruct((s, w), dt) for (w, dt) in outs]
    out_shape += [jax.ShapeDtypeStruct(shape, F32) for shape in accs]
    res = pl.pallas_call(
        body, name=name, grid=(s // ts,),
        in_specs=in_specs, out_specs=out_specs, out_shape=out_shape,
        compiler_params=_cparams(("arbitrary",)),
    )(*tiled, *params)
    return res


def _sb_block(qs, kb, diag):
    tq, bk = qs.shape[0], kb.shape[0]
    z = _dot(qs, kb, NT)
    lb = jnp.minimum(z, 0.0) - jnp.log(1.0 + jnp.exp(-jnp.abs(z)))
    lk = lb - z
    if diag is None:
        return None, lb, lk
    causal = (diag + _iota((tq, bk), 1)) < _iota((tq, bk), 0)
    return causal, lb, jnp.where(causal, lk, 0.0)


def _fused_exchange(scatter, ncols, nsteps):
    def hooks(ins, outs, sems):
        step = pl.program_id(0) * ncols + pl.program_id(1)

        @pl.when(step == 0)
        def _():
            for cp in _exchange_copies(ins, outs, *sems, scatter):
                cp.start()

        def finish():
            @pl.when(step == nsteps - 1)
            def _():
                for cp in _exchange_copies(ins, outs, *sems, scatter):
                    cp.wait()
        return finish
    return hooks


def _sb_fwd(q, k, v, comm=()):
    h, s, dh = q.shape
    tq, bk = min(SB_TQ, s), min(SB_BK, s)
    scale = dh ** -0.5
    nc = len(comm)
    hooks = _fused_exchange(False, s // tq, h * (s // tq))

    def body(q_ref, k_ref, v_ref, *rest):
        o_ref, lt_ref = rest[nc:nc + 2]
        if nc:
            finish = hooks(rest[:nc], rest[nc + 2:2 * nc + 2], rest[2 * nc + 2:])
        i = pl.program_id(1)
        q0 = i * tq
        qs = q_ref[...] * scale
        tri = (_iota((bk, bk), 0) > _iota((bk, bk), 1)).astype(CDT)

        def step(k0, carry, diag):
            cf, acc = carry
            kb = k_ref[pl.ds(k0, bk), :]
            vb = v_ref[pl.ds(k0, bk), :]
            causal, lb, lk = _sb_block(qs, kb, diag)
            w = jnp.exp(lb + cf + _split_dot(lk, tri, False, 2))
            if causal is not None:
                w = jnp.where(causal, w, 0.0)
            return cf + jnp.sum(lk, axis=1, keepdims=True), acc + _dot(w, vb)

        carry = (jnp.zeros((tq, 1), F32), jnp.zeros((tq, dh), F32))
        for d in reversed(range(tq // bk)):
            carry = step(pl.multiple_of(q0 + d * bk, bk), carry, d * bk)
        nfull = q0 // bk
        cf, acc = lax.fori_loop(
            0, nfull, lambda jj, c: step(pl.multiple_of((nfull - 1 - jj) * bk, bk), c, None), carry)
        o_ref[...] = acc
        lt_ref[...] = cf
        if nc:
            finish()

    return pl.pallas_call(
        body, name="sb_fwd", grid=(h, s // tq),
        in_specs=[pl.BlockSpec((None, tq, dh), lambda a, i: (a, i, 0)),
                  pl.BlockSpec((None, s, dh), lambda a, i: (a, 0, 0)),
                  pl.BlockSpec((None, s, dh), lambda a, i: (a, 0, 0))] + [HBM] * nc,
        out_specs=[pl.BlockSpec((None, tq, dh), lambda a, i: (a, i, 0)),
                   pl.BlockSpec((None, tq, 1), lambda a, i: (a, i, 0))] + [HBM] * nc,
        out_shape=[jax.ShapeDtypeStruct((h, s, dh), F32), jax.ShapeDtypeStruct((h, s, 1), F32)]
        + _exchange_shapes(comm, False),
        scratch_shapes=_exchange_sems(nc) if nc else [],
        compiler_params=_cparams(("arbitrary", "arbitrary")),
    )(q, k, v, *comm)


def _sb_bwd(q, k, v, ltot, do, comm=()):
    h, s, dh = q.shape
    tq, bk = min(SB_TQ, s), min(SB_BK, s)
    scale = dh ** -0.5
    nc = len(comm)
    hooks = _fused_exchange(True, s // tq, h * (s // tq))

    def body(q_ref, k_ref, v_ref, lt_ref, do_ref, *rest):
        dq_ref, dk_ref, dv_ref = rest[nc:nc + 3]
        if nc:
            finish = hooks(rest[:nc], rest[nc + 3:2 * nc + 3], rest[2 * nc + 3:])
        i = pl.program_id(1)

        @pl.when(i == 0)
        def _():
            dk_ref[...] = jnp.zeros_like(dk_ref)
            dv_ref[...] = jnp.zeros_like(dv_ref)

        q0 = i * tq
        qs = q_ref[...] * scale
        dov = do_ref[...].astype(CDT)
        ltot = lt_ref[...]
        tri_le = (_iota((bk, bk), 0) <= _iota((bk, bk), 1)).astype(CDT)
        tri_lt = (_iota((bk, bk), 0) < _iota((bk, bk), 1)).astype(CDT)

        def step(k0, carry, diag):
            cf, cg, dq = carry
            kb = k_ref[pl.ds(k0, bk), :]
            vb = v_ref[pl.ds(k0, bk), :]
            causal, lb, lk = _sb_block(qs, kb, diag)
            w = jnp.exp(lb + ltot - (cf + _split_dot(lk, tri_le, False, 2)))
            if causal is not None:
                w = jnp.where(causal, w, 0.0)
            g = w * _dot(dov, vb, NT)
            gsum = cg + _split_dot(g, tri_lt, False, 2)
            dz = g - (g + gsum) * jnp.exp(lb)
            if causal is not None:
                dz = jnp.where(causal, dz, 0.0)
            dzc = dz.astype(CDT)
            dk_ref[pl.ds(k0, bk), :] += _dot(dzc, qs, TN)
            dv_ref[pl.ds(k0, bk), :] += _dot(w, dov, TN)
            return (cf + jnp.sum(lk, axis=1, keepdims=True), cg + jnp.sum(g, axis=1, keepdims=True),
                    dq + _dot(dzc, kb))

        carry = (jnp.zeros((tq, 1), F32), jnp.zeros((tq, 1), F32), jnp.zeros((tq, dh), F32))
        carry = lax.fori_loop(0, q0 // bk, lambda jj, c: step(pl.multiple_of(jj * bk, bk), c, None), carry)
        for d in range(tq // bk):
            carry = step(pl.multiple_of(q0 + d * bk, bk), carry, d * bk)
        dq_ref[...] = carry[2] * scale
        if nc:
            finish()

    tile = pl.BlockSpec((None, tq, dh), lambda a, i: (a, i, 0))
    full = pl.BlockSpec((None, s, dh), lambda a, i: (a, 0, 0))
    shp = jax.ShapeDtypeStruct((h, s, dh), F32)
    return pl.pallas_call(
        body, name="sb_bwd", grid=(h, s // tq),
        in_specs=[tile, full, full, pl.BlockSpec((None, tq, 1), lambda a, i: (a, i, 0)), tile] + [HBM] * nc,
        out_specs=[tile, full, full] + [HBM] * nc,
        out_shape=[shp, shp, shp] + _exchange_shapes(comm, True),
        scratch_shapes=_exchange_sems(nc) if nc else [],
        compiler_params=_cparams(("arbitrary", "arbitrary")),
    )(q, k, v, ltot, do, *comm)


def _pick_lane(tile, r):
    return jnp.sum(jnp.where(_iota(tile.shape, 1) == r, tile, 0.0), axis=1, keepdims=True)


def _pick_row(tile, r):
    return jnp.sum(jnp.where(_iota(tile.shape, 0) == r, tile, 0.0), axis=0, keepdims=True)


def _ssd_chunk_setup(c_ref, b_ref, dac_ref, dar_ref, cb_ref, acsc_ref, acsr_ref):
    l = SSD_L
    tdt = F32 if CDT == F32 else BF16
    lower = (_iota((l, l), 1) <= _iota((l, l), 0)).astype(tdt)
    upper = (_iota((l, l), 0) <= _iota((l, l), 1)).astype(tdt)
    cb_ref[...] = _dot(c_ref[...], b_ref[...], NT)
    acsc_ref[...] = _split_dot(dac_ref[...], lower, True, 3)
    acsr_ref[...] = _split_dot(dar_ref[...], upper, False, 3)


def _ssd_fwd(xh, dtc, dac, dar, dsk, xbc):
    hh, s, p = xh.shape
    l, n, g_, r_ = SSD_L, SSD_N, SSD_G, SSD_R
    nc = s // l
    boff = SSD_INNER // n
    coff = boff + g_

    def body(x_ref, dtc_ref, dac_ref, dar_ref, dsk_ref, b_ref, c_ref, y_ref, st_ref,
             state_ref, cb_ref, acsc_ref, acsr_ref):
        c = pl.program_id(1)
        r = pl.program_id(2)

        @pl.when(r == 0)
        def _():
            _ssd_chunk_setup(c_ref, b_ref, dac_ref, dar_ref, cb_ref, acsc_ref, acsr_ref)

        @pl.when(c == 0)
        def _():
            state_ref[r] = jnp.zeros((n, p), F32)

        a_col = _pick_lane(acsc_ref[...], r)
        a_row = _pick_row(acsr_ref[...], r)
        dt_col = _pick_lane(dtc_ref[...], r)
        dsk_h = _pick_lane(dsk_ref[...], r)
        xv = x_ref[...]
        xd = xv * dt_col
        mask = _iota((l, l), 1) <= _iota((l, l), 0)
        decay = jnp.where(mask, jnp.exp(jnp.minimum(a_col - a_row, 0.0)), 0.0)
        w = cb_ref[...] * decay
        hprev = state_ref[r]
        cv = c_ref[...]
        y = _dot(w, xd) + jnp.exp(a_col) * _dot(cv, hprev)
        y_ref[...] = y + dsk_h * xv
        a_end = a_col[l - 1:l, :]
        dte = jnp.exp(a_end - a_col)
        st_ref[...] = hprev
        state_ref[r] = hprev * jnp.exp(a_end) + _dot(b_ref[...], xd * dte, TN)

    return pl.pallas_call(
        body, name="ssd_fwd", grid=(g_, nc, r_),
        in_specs=[pl.BlockSpec((None, l, p), lambda g, c, r: (g * r_ + r, c, 0)),
                  pl.BlockSpec((None, l, r_), lambda g, c, r: (g, c, 0)),
                  pl.BlockSpec((None, l, r_), lambda g, c, r: (g, c, 0)),
                  pl.BlockSpec((None, r_, l), lambda g, c, r: (g, 0, c)),
                  pl.BlockSpec((None, 1, r_), lambda g, c, r: (g, 0, 0)),
                  pl.BlockSpec((l, n), lambda g, c, r: (c, boff + g)),
                  pl.BlockSpec((l, n), lambda g, c, r: (c, coff + g))],
        out_specs=[pl.BlockSpec((None, l, p), lambda g, c, r: (g * r_ + r, c, 0)),
                   pl.BlockSpec((None, None, n, p), lambda g, c, r: (g * r_ + r, c, 0, 0))],
        out_shape=[jax.ShapeDtypeStruct((hh, s, p), F32),
                   jax.ShapeDtypeStruct((hh, nc, n, p), F32)],
        scratch_shapes=[pltpu.VMEM((r_, n, p), F32), pltpu.VMEM((l, l), F32),
                        pltpu.VMEM((l, r_), F32), pltpu.VMEM((r_, l), F32)],
        compiler_params=_cparams(("parallel", "arbitrary", "arbitrary")),
    )(xh, dtc, dac, dar, dsk, xbc, xbc)


def _ssd_bwd(xh, dtc, dac, dar, dsk, xbc, st, dy):
    hh, s, p = xh.shape
    l, n, g_, r_ = SSD_L, SSD_N, SSD_G, SSD_R
    nc = s // l
    boff = SSD_INNER // n
    coff = boff + g_

    def body(x_ref, dtc_ref, dac_ref, dar_ref, dsk_ref, b_ref, c_ref, st_ref, dy_ref,
             dx_ref, dda_ref, dxx_ref, db_ref, dc_ref, dah_ref, ddsk_ref,
             dstate_ref, cb_ref, acsc_ref, acsr_ref):
        c = pl.program_id(1)
        r = pl.program_id(2)

        @pl.when(r == 0)
        def _():
            _ssd_chunk_setup(c_ref, b_ref, dac_ref, dar_ref, cb_ref, acsc_ref, acsr_ref)
            db_ref[...] = jnp.zeros_like(db_ref)
            dc_ref[...] = jnp.zeros_like(dc_ref)
            dda_ref[...] = jnp.zeros_like(dda_ref)
            dxx_ref[...] = jnp.zeros_like(dxx_ref)

        @pl.when(jnp.logical_and(c == 0, r == 0))
        def _():
            dah_ref[...] = jnp.zeros_like(dah_ref)
            ddsk_ref[...] = jnp.zeros_like(ddsk_ref)

        @pl.when(c == 0)
        def _():
            dstate_ref[r] = jnp.zeros((n, p), F32)

        a_col = _pick_lane(acsc_ref[...], r)
        a_row = _pick_row(acsr_ref[...], r)
        dt_col = _pick_lane(dtc_ref[...], r)
        dsk_h = _pick_lane(dsk_ref[...], r)
        xv = x_ref[...]
        dyv = dy_ref[...]
        xd = xv * dt_col
        il = _iota((l, l), 0)
        isx = _iota((l, l), 1)
        decay = jnp.where(isx <= il, jnp.exp(jnp.minimum(a_col - a_row, 0.0)), 0.0)
        cb = cb_ref[...]
        w = cb * decay
        dhn = dstate_ref[r]
        hc = st_ref[...]
        bv = b_ref[...]
        cv = c_ref[...]
        a_end = a_col[l - 1:l, :]
        ea = jnp.exp(a_col)
        dte = jnp.exp(a_end - a_col)

        dx_state = dte * _dot(bv, dhn)
        dxd = _dot(w, dyv, TN) + dx_state
        md = decay * _dot(dyv, xd, NT)
        m = md * cb
        dc_ref[...] += _dot(md, bv) + ea * _dot(dyv, hc, NT)
        db_ref[...] += _dot(md, cv, TN) + dte * _dot(xd, dhn, NT)
        dstate_ref[r] = jnp.exp(a_end) * dhn + _dot(cv, dyv * ea, TN)

        tdt = F32 if CDT == F32 else BF16
        t1 = (isx >= il).astype(tdt)
        yoff = ea * _dot(cv, hc)
        xdx = jnp.sum(xd * dx_state, axis=1, keepdims=True)
        vec = jnp.sum(dyv * yoff, axis=1, keepdims=True) - xdx
        end_term = jnp.sum(xdx, axis=0, keepdims=True) + \
            jnp.exp(a_end) * jnp.sum(jnp.sum(hc * dhn, axis=1, keepdims=True), axis=0, keepdims=True)
        zmat = _split_dot(m, t1, True, 2)
        span = jnp.sum(jnp.where(isx < il, zmat, 0.0), axis=1, keepdims=True)
        rc = _split_dot(jnp.broadcast_to(vec, (l, 128)), t1, True, 2)[:, :1]
        dda = span + rc + end_term
        dxx = jnp.sum(dxd * xv, axis=1, keepdims=True)

        lane = _iota((l, r_), 1) == r
        dda_ref[...] += jnp.where(lane, dda, 0.0)
        dxx_ref[...] += jnp.where(lane, dxx, 0.0)
        dx_ref[...] = dxd * dt_col + dsk_h * dyv
        lane1 = _iota((1, r_), 1) == r
        dah_ref[...] += jnp.where(lane1, jnp.sum(dda * dt_col, axis=0, keepdims=True), 0.0)
        ddsk_ref[...] += jnp.where(
            lane1, jnp.sum(jnp.sum(dyv * xv, axis=1, keepdims=True), axis=0, keepdims=True), 0.0)

    rev = lambda c: nc - 1 - c
    xspec = pl.BlockSpec((None, l, p), lambda g, c, r: (g * r_ + r, rev(c), 0))
    cspec = pl.BlockSpec((None, l, r_), lambda g, c, r: (g, rev(c), 0))
    hspec = pl.BlockSpec((None, 1, r_), lambda g, c, r: (g, 0, 0))
    return pl.pallas_call(
        body, name="ssd_bwd", grid=(g_, nc, r_),
        in_specs=[xspec, cspec, cspec,
                  pl.BlockSpec((None, r_, l), lambda g, c, r: (g, 0, rev(c))),
                  hspec,
                  pl.BlockSpec((l, n), lambda g, c, r: (rev(c), boff + g)),
                  pl.BlockSpec((l, n), lambda g, c, r: (rev(c), coff + g)),
                  pl.BlockSpec((None, None, n, p), lambda g, c, r: (g * r_ + r, rev(c), 0, 0)),
                  xspec],
        out_specs=[xspec, cspec, cspec,
                   pl.BlockSpec((l, n), lambda g, c, r: (rev(c), g)),
                   pl.BlockSpec((l, n), lambda g, c, r: (rev(c), g)),
                   hspec, hspec],
        out_shape=[jax.ShapeDtypeStruct((hh, s, p), F32),
                   jax.ShapeDtypeStruct((g_, s, r_), F32),
                   jax.ShapeDtypeStruct((g_, s, r_), F32),
                   jax.ShapeDtypeStruct((s, g_ * n), F32),
                   jax.ShapeDtypeStruct((s, g_ * n), F32),
                   jax.ShapeDtypeStruct((g_, 1, r_), F32),
                   jax.ShapeDtypeStruct((g_, 1, r_), F32)],
        scratch_shapes=[pltpu.VMEM((r_, n, p), F32), pltpu.VMEM((l, l), F32),
                        pltpu.VMEM((l, r_), F32), pltpu.VMEM((r_, l), F32)],
        compiler_params=_cparams(("parallel", "arbitrary", "arbitrary")),
    )(xh, dtc, dac, dar, dsk, xbc, xbc, st, dy)


def _ssd_chunk_common(c_ref, b_ref, dac_ref, dar_ref):
    l = SSD_L
    tdt = F32 if CDT == F32 else BF16
    lower = (_iota((l, l), 1) <= _iota((l, l), 0)).astype(tdt)
    upper = (_iota((l, l), 0) <= _iota((l, l), 1)).astype(tdt)
    cb = _dot(c_ref[...], b_ref[...], NT)
    return cb, _split_dot(dac_ref[...], lower, True, 3), _split_dot(dar_ref[...], upper, False, 3)


def _ssd_fwd_g(xh, dtc, dac, dar, dsk, xbc):
    hh, s, p = xh.shape
    l, n, g_, r_ = SSD_L, SSD_N, SSD_G, SSD_R
    nc = s // l
    boff = SSD_INNER // n
    coff = boff + g_

    def body(x_ref, dtc_ref, dac_ref, dar_ref, dsk_ref, b_ref, c_ref, y_ref, st_ref, state_ref):
        c = pl.program_id(1)

        @pl.when(c == 0)
        def _():
            state_ref[...] = jnp.zeros_like(state_ref)

        cb, acs_c, acs_r = _ssd_chunk_common(c_ref, b_ref, dac_ref, dar_ref)
        mask = _iota((l, l), 1) <= _iota((l, l), 0)
        cv, bv = c_ref[...], b_ref[...]
        dtcv, dskv = dtc_ref[...], dsk_ref[...]
        for r in range(r_):
            a_col = _pick_lane(acs_c, r)
            a_row = _pick_row(acs_r, r)
            dt_col = _pick_lane(dtcv, r)
            dsk_h = _pick_lane(dskv, r)
            xv = x_ref[r]
            xd = xv * dt_col
            decay = jnp.where(mask, jnp.exp(jnp.minimum(a_col - a_row, 0.0)), 0.0)
            hprev = state_ref[r]
            y = _dot(cb * decay, xd) + jnp.exp(a_col) * _dot(cv, hprev)
            y_ref[r] = y + dsk_h * xv
            a_end = a_col[l - 1:l, :]
            st_ref[r] = hprev
            state_ref[r] = hprev * jnp.exp(a_end) + _dot(bv, xd * jnp.exp(a_end - a_col), TN)

    return pl.pallas_call(
        body, name="ssd_fwd", grid=(g_, nc),
        in_specs=[pl.BlockSpec((r_, l, p), lambda g, c: (g, c, 0)),
                  pl.BlockSpec((None, l, r_), lambda g, c: (g, c, 0)),
                  pl.BlockSpec((None, l, r_), lambda g, c: (g, c, 0)),
                  pl.BlockSpec((None, r_, l), lambda g, c: (g, 0, c)),
                  pl.BlockSpec((None, 1, r_), lambda g, c: (g, 0, 0)),
                  pl.BlockSpec((l, n), lambda g, c: (c, boff + g)),
                  pl.BlockSpec((l, n), lambda g, c: (c, coff + g))],
        out_specs=[pl.BlockSpec((r_, l, p), lambda g, c: (g, c, 0)),
                   pl.BlockSpec((r_, None, n, p), lambda g, c: (g, c, 0, 0))],
        out_shape=[jax.ShapeDtypeStruct((hh, s, p), F32),
                   jax.ShapeDtypeStruct((hh, nc, n, p), F32)],
        scratch_shapes=[pltpu.VMEM((r_, n, p), F32)],
        compiler_params=_cparams(("parallel", "arbitrary")),
    )(xh, dtc, dac, dar, dsk, xbc, xbc)


def _ssd_bwd_g(xh, dtc, dac, dar, dsk, xbc, st, dy):
    hh, s, p = xh.shape
    l, n, g_, r_ = SSD_L, SSD_N, SSD_G, SSD_R
    nc = s // l
    boff = SSD_INNER // n
    coff = boff + g_

    def body(x_ref, dtc_ref, dac_ref, dar_ref, dsk_ref, b_ref, c_ref, st_ref, dy_ref,
             dx_ref, dda_ref, dxx_ref, db_ref, dc_ref, dah_ref, ddsk_ref, dstate_ref):
        c = pl.program_id(1)

        @pl.when(c == 0)
        def _():
            dstate_ref[...] = jnp.zeros_like(dstate_ref)
            dah_ref[...] = jnp.zeros_like(dah_ref)
            ddsk_ref[...] = jnp.zeros_like(ddsk_ref)

        cb, acs_c, acs_r = _ssd_chunk_common(c_ref, b_ref, dac_ref, dar_ref)
        il = _iota((l, l), 0)
        isx = _iota((l, l), 1)
        tdt = F32 if CDT == F32 else BF16
        t1 = (isx >= il).astype(tdt)
        cv, bv = c_ref[...], b_ref[...]
        dtcv, dskv = dtc_ref[...], dsk_ref[...]
        lane = _iota((l, r_), 1)
        lane1 = _iota((1, r_), 1)
        dda_all = jnp.zeros((l, r_), F32)
        dxx_all = jnp.zeros((l, r_), F32)
        dah_all = jnp.zeros((1, r_), F32)
        ddsk_all = jnp.zeros((1, r_), F32)
        db_acc = jnp.zeros((l, n), F32)
        dc_acc = jnp.zeros((l, n), F32)
        for r in range(r_):
            a_col = _pick_lane(acs_c, r)
            a_row = _pick_row(acs_r, r)
            dt_col = _pick_lane(dtcv, r)
            dsk_h = _pick_lane(dskv, r)
            xv = x_ref[r]
            dyv = dy_ref[r]
            xd = xv * dt_col
            decay = jnp.where(isx <= il, jnp.exp(jnp.minimum(a_col - a_row, 0.0)), 0.0)
            dhn = dstate_ref[r]
            hc = st_ref[r]
            a_end = a_col[l - 1:l, :]
            ea = jnp.exp(a_col)
            dte = jnp.exp(a_end - a_col)

            dx_state = dte * _dot(bv, dhn)
            dxd = _dot(cb * decay, dyv, TN) + dx_state
            md = decay * _dot(dyv, xd, NT)
            dc_acc = dc_acc + _dot(md, bv) + ea * _dot(dyv, hc, NT)
            db_acc = db_acc + _dot(md, cv, TN) + dte * _dot(xd, dhn, NT)
            dstate_ref[r] = jnp.exp(a_end) * dhn + _dot(cv, dyv * ea, TN)

            yoff = ea * _dot(cv, hc)
            xdx = jnp.sum(xd * dx_state, axis=1, keepdims=True)
            vec = jnp.sum(dyv * yoff, axis=1, keepdims=True) - xdx
            end_term = jnp.sum(xdx, axis=0, keepdims=True) + \
                jnp.exp(a_end) * jnp.sum(jnp.sum(hc * dhn, axis=1, keepdims=True), axis=0, keepdims=True)
            zmat = _split_dot(md * cb, t1, True, 2)
            span = jnp.sum(jnp.where(isx < il, zmat, 0.0), axis=1, keepdims=True)
            rc = _split_dot(jnp.broadcast_to(vec, (l, 128)), t1, True, 2)[:, :1]
            dda = span + rc + end_term
            dda_all = jnp.where(lane == r, dda, dda_all)
            dxx_all = jnp.where(lane == r, jnp.sum(dxd * xv, axis=1, keepdims=True), dxx_all)
            dx_ref[r] = dxd * dt_col + dsk_h * dyv
            dah_all = jnp.where(lane1 == r, jnp.sum(dda * dt_col, axis=0, keepdims=True), dah_all)
            ddsk_all = jnp.where(
                lane1 == r, jnp.sum(jnp.sum(dyv * xv, axis=1, keepdims=True), axis=0, keepdims=True), ddsk_all)
        dda_ref[...] = dda_all
        dxx_ref[...] = dxx_all
        db_ref[...] = db_acc
        dc_ref[...] = dc_acc
        dah_ref[...] += dah_all
        ddsk_ref[...] += ddsk_all

    rev = lambda c: nc - 1 - c
    xspec = pl.BlockSpec((r_, l, p), lambda g, c: (g, rev(c), 0))
    cspec = pl.BlockSpec((None, l, r_), lambda g, c: (g, rev(c), 0))
    hspec = pl.BlockSpec((None, 1, r_), lambda g, c: (g, 0, 0))
    return pl.pallas_call(
        body, name="ssd_bwd", grid=(g_, nc),
        in_specs=[xspec, cspec, cspec,
                  pl.BlockSpec((None, r_, l), lambda g, c: (g, 0, rev(c))),
                  hspec,
                  pl.BlockSpec((l, n), lambda g, c: (rev(c), boff + g)),
                  pl.BlockSpec((l, n), lambda g, c: (rev(c), coff + g)),
                  pl.BlockSpec((r_, None, n, p), lambda g, c: (g, rev(c), 0, 0)),
                  xspec],
        out_specs=[xspec, cspec, cspec,
                   pl.BlockSpec((l, n), lambda g, c: (rev(c), g)),
                   pl.BlockSpec((l, n), lambda g, c: (rev(c), g)),
                   hspec, hspec],
        out_shape=[jax.ShapeDtypeStruct((hh, s, p), F32),
                   jax.ShapeDtypeStruct((g_, s, r_), F32),
                   jax.ShapeDtypeStruct((g_, s, r_), F32),
                   jax.ShapeDtypeStruct((s, g_ * n), F32),
                   jax.ShapeDtypeStruct((s, g_ * n), F32),
                   jax.ShapeDtypeStruct((g_, 1, r_), F32),
                   jax.ShapeDtypeStruct((g_, 1, r_), F32)],
        scratch_shapes=[pltpu.VMEM((r_, n, p), F32)],
        compiler_params=_cparams(("parallel", "arbitrary")),
    )(xh, dtc, dac, dar, dsk, xbc, xbc, st, dy)


CONV_TC = 256
CONV_RC = 512


def _conv_fwd(xp, w, b):
    sp, ch = xp.shape
    s = sp - CONV_PAD
    rc = min(CONV_RC, s)
    base = CONV_PAD - (CONV_K - 1)

    def body(x_ref, w_ref, b_ref, pre_ref, act_ref):
        wv = w_ref[...]
        for t0 in range(0, s, rc):
            acc = jnp.broadcast_to(b_ref[...], (rc, CONV_TC))
            for kk in range(CONV_K):
                acc = acc + wv[kk:kk + 1, :] * x_ref[pl.ds(t0 + base + kk, rc), :]
            pre_ref[pl.ds(t0, rc), :] = acc
            act_ref[pl.ds(t0, rc), :] = _silu(acc)

    shp = jax.ShapeDtypeStruct((s, ch), F32)
    return pl.pallas_call(
        body, name="conv_fwd", grid=(ch // CONV_TC,),
        in_specs=[pl.BlockSpec((sp, CONV_TC), lambda j: (0, j)),
                  pl.BlockSpec((CONV_K, CONV_TC), lambda j: (0, j)),
                  pl.BlockSpec((1, CONV_TC), lambda j: (0, j))],
        out_specs=[pl.BlockSpec((s, CONV_TC), lambda j: (0, j))] * 2,
        out_shape=[shp, shp],
        compiler_params=_cparams(("parallel",)),
    )(xp, w, b)


def _conv_bwd(xp, prep, dactp, w):
    sp, ch = xp.shape
    s = sp - CONV_PAD
    rc = min(CONV_RC, s)
    base = CONV_PAD - (CONV_K - 1)

    def body(x_ref, pre_ref, da_ref, w_ref, dx_ref, dw_ref, db_ref, dpre_ref):
        wv = w_ref[...]
        for t0 in range(0, s, rc):
            dpre_ref[pl.ds(t0, rc), :] = da_ref[pl.ds(t0, rc), :] * _dsilu(pre_ref[pl.ds(t0, rc), :])
        dpre_ref[pl.ds(s, CONV_PAD), :] = jnp.zeros((CONV_PAD, CONV_TC), F32)
        dws = [jnp.zeros((1, CONV_TC), F32) for _ in range(CONV_K)]
        dbs = jnp.zeros((1, CONV_TC), F32)
        for t0 in range(0, s, rc):
            acc = jnp.zeros((rc, CONV_TC), F32)
            dp = dpre_ref[pl.ds(t0, rc), :]
            for kk in range(CONV_K):
                acc = acc + wv[kk:kk + 1, :] * dpre_ref[pl.ds(t0 + CONV_K - 1 - kk, rc), :]
                dws[kk] = dws[kk] + jnp.sum(dp * x_ref[pl.ds(t0 + base + kk, rc), :], axis=0, keepdims=True)
            dbs = dbs + jnp.sum(dp, axis=0, keepdims=True)
            dx_ref[pl.ds(t0, rc), :] = acc.astype(dx_ref.dtype)
        for kk in range(CONV_K):
            dw_ref[kk:kk + 1, :] = dws[kk]
        db_ref[...] = dbs

    col = pl.BlockSpec((sp, CONV_TC), lambda j: (0, j))
    return pl.pallas_call(
        body, name="conv_bwd", grid=(ch // CONV_TC,),
        in_specs=[col, col, col, pl.BlockSpec((CONV_K, CONV_TC), lambda j: (0, j))],
        out_specs=[pl.BlockSpec((s, CONV_TC), lambda j: (0, j)),
                   pl.BlockSpec((CONV_K, CONV_TC), lambda j: (0, j)),
                   pl.BlockSpec((1, CONV_TC), lambda j: (0, j))],
        out_shape=[jax.ShapeDtypeStruct((s, ch), CDT),
                   jax.ShapeDtypeStruct((CONV_K, ch), F32),
                   jax.ShapeDtypeStruct((1, ch), F32)],
        scratch_shapes=[pltpu.VMEM((sp, CONV_TC), F32)],
        compiler_params=_cparams(("parallel",)),
    )(xp, prep, dactp, w)


MEM_TS = 512


def _mem_fwd(mq, kv):
    s = mq.shape[0]
    m = kv.shape[0]
    ts = min(MEM_TS, s)
    scale = MEM_DH ** -0.5

    def body(q_ref, k_ref, v_ref, o_ref):
        sc = _dot(q_ref[...], k_ref[...], NT) * scale
        e = jnp.exp(sc - jnp.max(sc, axis=1, keepdims=True))
        pr = e / jnp.sum(e, axis=1, keepdims=True)
        o_ref[...] = _dot(pr, v_ref[...]).astype(o_ref.dtype)

    return pl.pallas_call(
        body, name="mem_fwd", grid=(MEM_H, s // ts),
        in_specs=[pl.BlockSpec((ts, MEM_DH), lambda a, i: (i, a)),
                  pl.BlockSpec((m, MEM_DH), lambda a, i: (0, a)),
                  pl.BlockSpec((m, MEM_DH), lambda a, i: (0, MEM_H + a))],
        out_specs=pl.BlockSpec((ts, MEM_DH), lambda a, i: (i, a)),
        out_shape=jax.ShapeDtypeStruct((s, MEM_H * MEM_DH), CDT),
        compiler_params=_cparams(("parallel", "arbitrary")),
    )(mq, kv, kv)


def _mem_bwd(mq, kv, do):
    s = mq.shape[0]
    m = kv.shape[0]
    ts = min(MEM_TS, s)
    scale = MEM_DH ** -0.5

    def body(q_ref, k_ref, v_ref, do_ref, dq_ref, dk_ref, dv_ref):
        i = pl.program_id(1)

        @pl.when(i == 0)
        def _():
            dk_ref[...] = jnp.zeros_like(dk_ref)
            dv_ref[...] = jnp.zeros_like(dv_ref)

        qv, kb, vb, dov = q_ref[...], k_ref[...], v_ref[...], do_ref[...]
        sc = _dot(qv, kb, NT) * scale
        e = jnp.exp(sc - jnp.max(sc, axis=1, keepdims=True))
        pr = e / jnp.sum(e, axis=1, keepdims=True)
        dp = _dot(dov, vb, NT)
        ds = pr * (dp - jnp.sum(dp * pr, axis=1, keepdims=True)) * scale
        dq_ref[...] = _dot(ds, kb).astype(dq_ref.dtype)
        dk_ref[...] += _dot(ds, qv, TN)
        dv_ref[...] += _dot(pr, dov, TN)

    tile = pl.BlockSpec((ts, MEM_DH), lambda a, i: (i, a))
    kvo = pl.BlockSpec((m, MEM_DH), lambda a, i: (0, a))
    return pl.pallas_call(
        body, name="mem_bwd", grid=(MEM_H, s // ts),
        in_specs=[tile, kvo, pl.BlockSpec((m, MEM_DH), lambda a, i: (0, MEM_H + a)), tile],
        out_specs=[tile, kvo, kvo],
        out_shape=[jax.ShapeDtypeStruct((s, MEM_H * MEM_DH), CDT),
                   jax.ShapeDtypeStruct((m, MEM_H * MEM_DH), F32),
                   jax.ShapeDtypeStruct((m, MEM_H * MEM_DH), F32)],
        compiler_params=_cparams(("parallel", "arbitrary")),
    )(mq, kv, kv, do)


def _heads(t, nh, dh):
    return t.reshape(t.shape[0], nh, dh).transpose(1, 0, 2)


def _unheads(t):
    return t.transpose(1, 0, 2).reshape(t.shape[1], t.shape[0] * t.shape[2])


def _group_cols(t):
    return t.reshape(t.shape[0], SSD_G, SSD_R).transpose(1, 0, 2)


def _pad_cols(t, width):
    return jnp.pad(t, ((0, 0), (0, width - t.shape[1])))


def _full_weight(name, gathered):
    if name in COL_SHARDED:
        return gathered.transpose(1, 0, 2).reshape(gathered.shape[1], N_DEV * gathered.shape[2])
    return gathered.reshape(N_DEV * gathered.shape[1], gathered.shape[2])


def _grad_payload(name, g):
    if name in COL_SHARDED:
        return g.reshape(g.shape[0], N_DEV, g.shape[1] // N_DEV).transpose(1, 0, 2)
    return g.reshape(N_DEV, g.shape[0] // N_DEV, g.shape[1])


def _local_step(x, mem, tgt, p, wt, shards=None):
    s, d = x.shape
    wt = dict(wt)
    c1, c2, c3, c4, c5 = 3 * d, 3 * d + SSD_INNER, 3 * d + SSD_INNER + CONV_DIM, \
        3 * d + SSD_INNER + CONV_DIM + SSD_H, 3 * d + SSD_INNER + CONV_DIM + SSD_H + d
    w_in = wt["w_in"]
    w_seg = [w_in[:, :c1], w_in[:, c1:c2], w_in[:, c2:c3], _pad_cols(w_in[:, c3:c4], DT_PAD),
             w_in[:, c4:c5], w_in[:, c5:]]
    seg_dtype = [CDT, F32, F32, F32, CDT, F32]
    seg_name = ["qkv", "z", "xbc", "dt", "mq", "gl"]

    u = _rows(lambda xv, g: _rms(xv, g), [x], [p["norm_mix_pre"]], [(d, CDT)], ts=512, name="f_norm_pre")[0]
    qkv, z, xbc_raw, dt_raw, mq, gl = [
        _mm(u, w_seg[i], out_dtype=seg_dtype[i], name="f_in_" + seg_name[i]) for i in range(6)]

    bias128 = _pad_cols(p["dt_bias"], DT_PAD)
    alog128 = _pad_cols(p["a_log"], DT_PAD)

    def dt_fn(dtr, bias, alog):
        dt = _softplus(dtr + bias)
        return dt, dt * (-jnp.exp(alog))

    dt128, da128 = _rows(dt_fn, [dt_raw], [bias128, alog128], [(DT_PAD, F32), (DT_PAD, F32)],
                         ts=512, name="f_dt")
    dtc = _group_cols(dt128[:, :SSD_H])
    dac = _group_cols(da128[:, :SSD_H])
    dar = dac.transpose(0, 2, 1)
    dsk = p["d_skip"].reshape(SSD_G, 1, SSD_R)

    xp = jnp.pad(xbc_raw, ((CONV_PAD, 0), (0, 0)))
    conv_w, conv_b = p["conv_w"], p["conv_b"]
    pre, xbc = _conv_fwd(xp, conv_w, conv_b)
    xh = _heads(xbc[:, :SSD_INNER], SSD_H, SSD_P)
    y_h, st = _ssd_fwd_g(xh, dtc, dac, dar, dsk, xbc)
    y_core = _unheads(y_h)

    def group_norm_fwd(yv, zv, wn):
        y2 = yv * _silu(zv)
        gw = SSD_INNER // SSD_G
        outs = []
        for gi in range(SSD_G):
            seg = y2[:, gi * gw:(gi + 1) * gw]
            outs.append(_rms(seg, wn[:, gi * gw:(gi + 1) * gw]))
        return jnp.concatenate(outs, axis=1)

    y_ssd = _rows(group_norm_fwd, [y_core, z], [p["ssd_norm"]], [(SSD_INNER, CDT)], ts=256, name="f_ssd_post")[0]

    q_h = _heads(qkv[:, :d], SB_H, SB_DH)
    k_h = _heads(qkv[:, d:2 * d], SB_H, SB_DH)
    v_h = _heads(qkv[:, 2 * d:], SB_H, SB_DH)
    o_h, lt_h, *late = _sb_fwd(q_h, k_h, v_h, tuple(shards) if shards is not None else ())
    for n, gth in zip(LATE_W, late):
        wt[n] = _full_weight(n, gth)
    y_sb = _unheads(o_h).astype(CDT)

    mu = _rows(lambda mv, g: _rms(mv, g), [mem], [p["norm_mem"]], [(d, CDT)], ts=256, name="f_norm_mem")[0]
    kv = _mm(mu, wt["w_mem_kv"], out_dtype=CDT, name="f_mem_kv")
    y_mem = _mem_fwd(mq, kv)

    p_sb = _mm(y_sb, wt["w_sb_out"], name="f_sb_out")
    p_ssd = _mm(y_ssd, wt["w_ssd_out"], name="f_ssd_out")
    p_mem = _mm(y_mem, wt["w_mem_out"], name="f_mem_out")

    def merge_fn(glv, a, b, c):
        return (_sigmoid(glv[:, :d]) * a + _sigmoid(glv[:, d:2 * d]) * b + _sigmoid(glv[:, 2 * d:]) * c)

    merged = _rows(merge_fn, [gl, p_sb, p_ssd, p_mem], [], [(d, CDT)], ts=256, name="f_merge")[0]
    mix = _mm(merged, wt["w_o"], name="f_w_o")

    def mid_fn(xv, mixv, g_post, g_pre):
        h1 = xv + _rms(mixv, g_post)
        return h1, _rms(h1, g_pre)

    h1, u2 = _rows(mid_fn, [x, mix], [p["norm_mix_post"], p["norm_mlp_pre"]], [(d, F32), (d, CDT)],
                   ts=512, name="f_mid")
    a1 = _mm(u2, wt["w_up"], name="f_up")
    act = _rows(lambda a: jnp.square(jnp.maximum(a, 0.0)), [a1], [], [(a1.shape[1], CDT)], ts=256, name="f_act")[0]
    ff = _mm(act, wt["w_down"], name="f_down")

    def loss_fn(h1v, ffv, tv, g):
        diff = h1v + _rms(ffv, g) - tv
        tot = jnp.sum(jnp.sum(diff * diff, axis=1, keepdims=True), axis=0, keepdims=True)
        return diff * (1.0 / d), tot

    dh2, loss_acc = _rows(loss_fn, [h1, ff, tgt], [p["norm_mlp_post"]], [(d, F32)], [(1, 128)],
                          ts=512, name="f_loss")
    loss = loss_acc[:, :1] * (0.5 / d)

    sg = {}

    def b_post(ffv, dyv, g):
        dx, dg = _rms_bwd(ffv, g, dyv)
        return dx, dg

    d_ff, sg["norm_mlp_post"] = _rows(b_post, [ff, dh2], [p["norm_mlp_post"]], [(d, CDT)], [(1, d)],
                                      ts=512, name="b_norm_mlp_post")
    dact = _mm(d_ff, wt["w_down"], tb=True, name="b_down_x")
    gw = {"w_down": _mm(act, d_ff, ta=True, name="b_down_w")}
    da1 = _rows(lambda dv, a: dv * 2.0 * jnp.maximum(a, 0.0), [dact, a1], [], [(a1.shape[1], CDT)],
                ts=256, name="b_act")[0]
    du2 = _mm(da1, wt["w_up"], tb=True, name="b_up_x")
    gw["w_up"] = _mm(u2, da1, ta=True, name="b_up_w")

    def b_mid(h1v, du2v, dh2v, mixv, g_pre, g_post):
        dxa, dga = _rms_bwd(h1v, g_pre, du2v)
        dh1 = dh2v + dxa
        dmix, dgb = _rms_bwd(mixv, g_post, dh1)
        return dh1, dmix, dga, dgb

    dh1, dmix, sg["norm_mlp_pre"], sg["norm_mix_post"] = _rows(
        b_mid, [h1, du2, dh2, mix], [p["norm_mlp_pre"], p["norm_mix_post"]],
        [(d, F32), (d, CDT)], [(1, d), (1, d)], ts=256, name="b_mid")
    dmerged = _mm(dmix, wt["w_o"], tb=True, name="b_w_o_x")
    gw["w_o"] = _mm(merged, dmix, ta=True, name="b_w_o_w")

    def b_merge(dm, glv, a, b, c):
        outs, dgl = [], []
        for i, br in enumerate((a, b, c)):
            gt = _sigmoid(glv[:, i * d:(i + 1) * d])
            outs.append(gt * dm)
            dgl.append(dm * br * gt * (1.0 - gt))
        return outs[0], outs[1], outs[2], jnp.concatenate(dgl, axis=1)

    dp_sb, dp_ssd, dp_mem, dgl = _rows(b_merge, [dmerged, gl, p_sb, p_ssd, p_mem], [],
                                       [(d, CDT), (d, CDT), (d, CDT), (3 * d, CDT)], ts=256, name="b_merge")
    dy_sb = _mm(dp_sb, wt["w_sb_out"], tb=True, name="b_sb_out_x")
    gw["w_sb_out"] = _mm(y_sb, dp_sb, ta=True, name="b_sb_out_w")
    dy_ssd = _mm(dp_ssd, wt["w_ssd_out"], tb=True, name="b_ssd_out_x")
    gw["w_ssd_out"] = _mm(y_ssd, dp_ssd, ta=True, name="b_ssd_out_w")
    dy_mem = _mm(dp_mem, wt["w_mem_out"], tb=True, out_dtype=CDT, name="b_mem_out_x")
    gw["w_mem_out"] = _mm(y_mem, dp_mem, ta=True, name="b_mem_out_w")

    dmq, dk_m, dv_m = _mem_bwd(mq, kv, dy_mem)
    dkv = jnp.concatenate([dk_m, dv_m], axis=1).astype(CDT)
    gw["w_mem_kv"] = _mm(mu, dkv, ta=True, name="b_mem_kv_w")
    dmu = _mm(dkv, wt["w_mem_kv"], tb=True, name="b_mem_kv_x")
    sg["norm_mem"] = _rows(lambda mv, dv, g: _rms_bwd(mv, g, dv)[1], [mem, dmu], [p["norm_mem"]], [], [(1, d)],
                           ts=256, name="b_norm_mem")[0]

    payloads = tuple(_grad_payload(n, gw[n]) for n in LATE_W) if shards is not None else ()
    dq_h, dk_h, dv_h, *received = _sb_bwd(q_h, k_h, v_h, lt_h, _heads(dy_sb, SB_H, SB_DH), payloads)
    dqkv = jnp.concatenate([_unheads(dq_h), _unheads(dk_h), _unheads(dv_h)], axis=1).astype(CDT)

    def group_norm_bwd(dyo, yv, zv, wn):
        sz = _silu(zv)
        y2 = yv * sz
        gw_ = SSD_INNER // SSD_G
        dy2, dwn = [], []
        for gi in range(SSD_G):
            sl = slice(gi * gw_, (gi + 1) * gw_)
            dxs, dgs = _rms_bwd(y2[:, sl], wn[:, sl], dyo[:, sl])
            dy2.append(dxs)
            dwn.append(dgs)
        dy2 = jnp.concatenate(dy2, axis=1)
        return dy2 * sz, dy2 * yv * _dsilu(zv), jnp.concatenate(dwn, axis=1)

    dy_core, dz, sg["ssd_norm"] = _rows(group_norm_bwd, [dy_ssd, y_core, z], [p["ssd_norm"]],
                                        [(SSD_INNER, F32), (SSD_INNER, CDT)], [(1, SSD_INNER)],
                                        ts=256, name="b_ssd_post")
    dxh, dda, dxx, d_b, d_c, dah, ddsk = _ssd_bwd_g(xh, dtc, dac, dar, dsk, xbc, st, _heads(dy_core, SSD_H, SSD_P))
    sg["d_skip"] = ddsk.reshape(1, SSD_H)
    sg["a_log"] = dah.reshape(1, SSD_H) * (-jnp.exp(p["a_log"]))

    def b_dt(ddav, dxxv, dtr, bias, alog):
        ddt = ddav * (-jnp.exp(alog)) + dxxv
        draw = ddt * _sigmoid(dtr + bias)
        return draw, jnp.sum(draw, axis=0, keepdims=True)

    ungroup = lambda t: _pad_cols(t.transpose(1, 0, 2).reshape(s, SSD_H), DT_PAD)
    ddt_raw, dbias128 = _rows(b_dt, [ungroup(dda), ungroup(dxx), dt_raw], [bias128, alog128],
                              [(DT_PAD, CDT)], [(1, DT_PAD)], ts=512, name="b_dt")
    sg["dt_bias"] = dbias128[:, :SSD_H]

    dxbc = jnp.concatenate([_unheads(dxh), d_b, d_c], axis=1)
    back = ((0, CONV_PAD), (0, 0))
    dxbc_raw, sg["conv_w"], sg["conv_b"] = _conv_bwd(xp, jnp.pad(pre, back), jnp.pad(dxbc, back), conv_w)

    dseg = [dqkv, dz, dxbc_raw, ddt_raw, dmq, dgl]
    dus = [_mm(dseg[i], w_seg[i], tb=True, name="b_in_x_" + seg_name[i]) for i in range(6)]
    dws = [_mm(u, dseg[i], ta=True, name="b_in_w_" + seg_name[i]) for i in range(6)]
    dws[3] = dws[3][:, :SSD_H]
    gw["w_in"] = jnp.concatenate(dws, axis=1)

    def b_pre(xv, dh1v, d0, d1, d2, d3, d4, d5, g):
        dx, dg = _rms_bwd(xv, g, d0 + d1 + d2 + d3 + d4 + d5)
        return dh1v + dx, dg

    grad_x, sg["norm_mix_pre"] = _rows(b_pre, [x, dh1] + dus, [p["norm_mix_pre"]], [(d, F32)], [(1, d)],
                                       ts=256, name="b_norm_pre")
    return loss, grad_x, gw, sg, (received if shards is not None else None)


HBM = pl.BlockSpec(memory_space=pltpu.HBM)
MESH = pl.DeviceIdType.MESH


def _me_and_peers():
    x, y, c = lax.axis_index("x"), lax.axis_index("y"), lax.axis_index("c")
    me = 4 * x + 2 * y + c
    peers = [(x, y, 1 - c), (1 - x, y, c), (x, 1 - y, c), (1 - x, 1 - y, c),
             (1 - x, y, 1 - c), (x, 1 - y, 1 - c), (1 - x, 1 - y, 1 - c)]
    return me, peers


def _peer_index(peer):
    return 4 * peer[0] + 2 * peer[1] + peer[2]


def _exchange_copies(ins, outs, send_sems, recv_sems, local_sems, scatter):
    me, peers = _me_and_peers()
    copies = []
    for a in range(len(ins)):
        own = ins[a].at[me] if scatter else ins[a]
        copies.append(pltpu.make_async_copy(own, outs[a].at[me], local_sems.at[a]))
        for kk, peer in enumerate(peers):
            src = ins[a].at[_peer_index(peer)] if scatter else ins[a]
            copies.append(pltpu.make_async_remote_copy(
                src_ref=src, dst_ref=outs[a].at[me],
                send_sem=send_sems.at[a, kk], recv_sem=recv_sems.at[a, kk],
                device_id=peer, device_id_type=MESH))
    return copies


def _exchange_shapes(ins, scatter):
    return [jax.ShapeDtypeStruct(t.shape if scatter else (N_DEV,) + t.shape, t.dtype) for t in ins]


def _exchange_sems(n):
    return [pltpu.SemaphoreType.DMA((n, N_DEV - 1)), pltpu.SemaphoreType.DMA((n, N_DEV - 1)),
            pltpu.SemaphoreType.DMA((n,))]


def _exchange(ins, scatter, name):
    n = len(ins)

    def body(*refs):
        copies = _exchange_copies(refs[:n], refs[n:2 * n], *refs[2 * n:], scatter)
        for cp in copies:
            cp.start()
        for cp in copies:
            cp.wait()

    return pl.pallas_call(
        body, name=name,
        in_specs=[HBM] * n, out_specs=[HBM] * n,
        out_shape=_exchange_shapes(ins, scatter),
        scratch_shapes=_exchange_sems(n),
        compiler_params=pltpu.CompilerParams(has_side_effects=True),
    )(*ins)


def _gather_two_level(shards, name):
    n = len(shards)

    def body(*refs):
        ins, outs = refs[:n], refs[n:2 * n]
        send_sems, recv_sems, local_sems = refs[2 * n:]
        x, y, c = lax.axis_index("x"), lax.axis_index("y"), lax.axis_index("c")
        me, sib = (x, y, c), (x, y, 1 - c)
        chips = [(1 - x, y), (x, 1 - y), (1 - x, 1 - y)]

        def copy(a, k, block, to, src=None):
            slot = outs[a].at[_peer_index(block)]
            return pltpu.make_async_remote_copy(
                src_ref=slot if src is None else src, dst_ref=slot,
                send_sem=send_sems.at[a, k], recv_sem=recv_sems.at[a, k], device_id=to, device_id_type=MESH)

        own = [pltpu.make_async_copy(ins[a], outs[a].at[_peer_index(me)], local_sems.at[a]) for a in range(n)]
        first = []
        for a in range(n):
            first.append(copy(a, 0, me, sib, src=ins[a]))
            first += [copy(a, 1 + j, me, (*chip, c), src=ins[a]) for j, chip in enumerate(chips)]
        for cp in own + first:
            cp.start()
        passed = []
        for j, chip in enumerate(chips):
            for a in range(n):
                copy(a, 1 + j, (*chip, c), me).wait_recv()
                fwd = copy(a, 4 + j, (*chip, c), sib)
                fwd.start()
                passed.append(fwd)
        for a in range(n):
            copy(a, 0, sib, me).wait_recv()
            for j, chip in enumerate(chips):
                copy(a, 4 + j, (*chip, 1 - c), me).wait_recv()
        for cp in first + passed:
            cp.wait_send()
        for cp in own:
            cp.wait()

    return pl.pallas_call(
        body, name=name,
        in_specs=[HBM] * n, out_specs=[HBM] * n,
        out_shape=_exchange_shapes(shards, False),
        scratch_shapes=_exchange_sems(n),
        compiler_params=pltpu.CompilerParams(has_side_effects=True),
    )(*shards)


def _pair_exchange(t, name):
    def body(t_ref, o_ref, send_sem, recv_sem):
        sib = (lax.axis_index("x"), lax.axis_index("y"), 1 - lax.axis_index("c"))
        cp = pltpu.make_async_remote_copy(src_ref=t_ref, dst_ref=o_ref, send_sem=send_sem, recv_sem=recv_sem,
                                          device_id=sib, device_id_type=MESH)
        cp.start()
        cp.wait()

    return pl.pallas_call(
        body, name=name, in_specs=[HBM], out_specs=HBM,
        out_shape=jax.ShapeDtypeStruct(t.shape, t.dtype),
        scratch_shapes=[pltpu.SemaphoreType.DMA, pltpu.SemaphoreType.DMA],
        compiler_params=pltpu.CompilerParams(has_side_effects=True),
    )(t)


N_CHIP = 4


def _chip_scatter(t, name):
    def body(t_ref, o_ref, send_sems, recv_sems, local_sem):
        x, y, c = lax.axis_index("x"), lax.axis_index("y"), lax.axis_index("c")
        mine = 2 * x + y
        copies = [pltpu.make_async_copy(t_ref.at[mine], o_ref.at[mine], local_sem)]
        for j, (px, py) in enumerate([(1 - x, y), (x, 1 - y), (1 - x, 1 - y)]):
            copies.append(pltpu.make_async_remote_copy(
                src_ref=t_ref.at[2 * px + py], dst_ref=o_ref.at[mine],
                send_sem=send_sems.at[j], recv_sem=recv_sems.at[j], device_id=(px, py, c), device_id_type=MESH))
        for cp in copies:
            cp.start()
        for cp in copies:
            cp.wait()

    return pl.pallas_call(
        body, name=name, in_specs=[HBM], out_specs=HBM,
        out_shape=jax.ShapeDtypeStruct(t.shape, t.dtype),
        scratch_shapes=[pltpu.SemaphoreType.DMA((N_CHIP - 1,)), pltpu.SemaphoreType.DMA((N_CHIP - 1,)),
                        pltpu.SemaphoreType.DMA],
        compiler_params=pltpu.CompilerParams(has_side_effects=True),
    )(t)


def _pair_sum(a, b, name):
    q, r, c = a.shape
    tr = _pick(r, (256, 128))
    spec = pl.BlockSpec((None, tr, c), lambda i, j: (i, j, 0))

    def body(a_ref, b_ref, o_ref):
        o_ref[...] = (a_ref[...] + b_ref[...]).astype(o_ref.dtype)

    return pl.pallas_call(
        body, name=name, grid=(q, r // tr), in_specs=[spec, spec], out_specs=spec,
        out_shape=jax.ShapeDtypeStruct(a.shape, CDT),
        compiler_params=_cparams(("parallel", "parallel")),
    )(a, b)


def _all_reduce_small(v, name):
    r, c = v.shape

    def body(v_ref, o_ref, buf, send_sems, recv_sems):
        me, peers = _me_and_peers()
        buf[me] = v_ref[...]
        copies = []
        for kk, peer in enumerate(peers):
            cp = pltpu.make_async_remote_copy(
                src_ref=v_ref, dst_ref=buf.at[me],
                send_sem=send_sems.at[kk], recv_sem=recv_sems.at[kk],
                device_id=peer, device_id_type=MESH)
            cp.start()
            copies.append(cp)
        for cp in copies:
            cp.wait()
        acc = buf[0]
        for i in range(1, N_DEV):
            acc = acc + buf[i]
        o_ref[...] = acc

    return pl.pallas_call(
        body, name=name,
        in_specs=[pl.BlockSpec(memory_space=pltpu.VMEM)],
        out_specs=pl.BlockSpec(memory_space=pltpu.VMEM),
        out_shape=jax.ShapeDtypeStruct((r, c), F32),
        scratch_shapes=[pltpu.VMEM((N_DEV, r, c), F32),
                        pltpu.SemaphoreType.DMA((N_DEV - 1,)), pltpu.SemaphoreType.DMA((N_DEV - 1,))],
        compiler_params=pltpu.CompilerParams(has_side_effects=True),
    )(v)


def _adamw_math(g, w, m, v):
    m2 = ADAM_B1 * m + (1.0 - ADAM_B1) * g
    v2 = ADAM_B2 * v + (1.0 - ADAM_B2) * jnp.square(g)
    m_hat = m2 / (1.0 - ADAM_B1 ** ADAM_STEP)
    v_hat = v2 / (1.0 - ADAM_B2 ** ADAM_STEP)
    delta = -ADAM_LR * (m_hat / (jnp.sqrt(v_hat) + ADAM_EPS) + ADAM_WD * w)
    return delta, m2, v2


def _adamw_reduce(parts, w, m, v, name):
    r, c = w.shape
    nparts = parts.shape[0]
    tr = _pick(r, (128, 64, 32, 16, 8))

    def body(p_ref, w_ref, m_ref, v_ref, g_ref, d_ref, m2_ref, v2_ref):
        g = p_ref[0].astype(F32)
        for i in range(1, nparts):
            g = g + p_ref[i].astype(F32)
        delta, m2, v2 = _adamw_math(g, w_ref[...], m_ref[...], v_ref[...])
        g_ref[...] = g
        d_ref[...] = delta
        m2_ref[...] = m2
        v2_ref[...] = v2

    tile = pl.BlockSpec((tr, c), lambda i: (i, 0))
    shp = jax.ShapeDtypeStruct((r, c), F32)
    return pl.pallas_call(
        body, name=name, grid=(r // tr,),
        in_specs=[pl.BlockSpec((nparts, tr, c), lambda i: (0, i, 0)), tile, tile, tile],
        out_specs=[tile] * 4, out_shape=[shp] * 4,
        compiler_params=_cparams(("parallel",)),
    )(parts, w, m, v)


def _adamw_plain(g, w, m, v, name):
    def body(g_ref, w_ref, m_ref, v_ref, d_ref, m2_ref, v2_ref):
        delta, m2, v2 = _adamw_math(g_ref[...], w_ref[...], m_ref[...], v_ref[...])
        d_ref[...] = delta
        m2_ref[...] = m2
        v2_ref[...] = v2

    spec = pl.BlockSpec(memory_space=pltpu.VMEM)
    shp = jax.ShapeDtypeStruct(g.shape, F32)
    return pl.pallas_call(
        body, name=name, in_specs=[spec] * 4, out_specs=[spec] * 3, out_shape=[shp] * 3,
    )(g, w, m, v)


SMALL_ROWS, SMALL_COLS = 16, 3072


def _small_step(sg, gcw, loss, ws, ms, vs):
    ns = len(sg)
    widths = [t.shape[1] for t in sg]
    kk_, ch = gcw.shape[1], gcw.shape[2]
    assert ns < SMALL_ROWS and max(widths) <= SMALL_COLS

    def body(*refs):
        g_refs = refs[:ns]
        gcw_ref, loss_ref = refs[ns], refs[ns + 1]
        w_refs = refs[ns + 2:2 * ns + 3]
        m_refs = refs[2 * ns + 3:3 * ns + 4]
        v_refs = refs[3 * ns + 4:4 * ns + 5]
        outs = refs[4 * ns + 5:8 * ns + 10]
        loss_out = outs[0]
        go, do_, mo, vo = (outs[1 + i * (ns + 1):1 + (i + 1) * (ns + 1)] for i in range(4))
        mine, buf, minec, bufc, send_sems, recv_sems = refs[8 * ns + 10:]
        me, peers = _me_and_peers()

        mine[...] = jnp.zeros_like(mine)
        for i in range(ns):
            mine[i:i + 1, 0:widths[i]] = g_refs[i][...]
        mine[ns:ns + 1, 0:LANES] = jnp.broadcast_to(loss_ref[...], (1, LANES))
        minec[...] = gcw_ref[...]
        buf[me] = mine[...]
        bufc[me] = minec[...]
        copies = []
        for j, peer in enumerate(peers):
            copies.append(pltpu.make_async_remote_copy(
                src_ref=mine, dst_ref=buf.at[me], send_sem=send_sems.at[0, j], recv_sem=recv_sems.at[0, j],
                device_id=peer, device_id_type=MESH))
            copies.append(pltpu.make_async_remote_copy(
                src_ref=minec, dst_ref=bufc.at[me], send_sem=send_sems.at[1, j], recv_sem=recv_sems.at[1, j],
                device_id=peer, device_id_type=MESH))
        for cp in copies:
            cp.start()
        for cp in copies:
            cp.wait()
        tot = buf[0]
        totc = bufc[0]
        for i in range(1, N_DEV):
            tot = tot + buf[i]
            totc = totc + bufc[i]
        mine[...] = tot
        minec[...] = totc
        loss_out[...] = mine[ns:ns + 1, 0:1]
        for i in range(ns + 1):
            g = mine[i:i + 1, 0:widths[i]] if i < ns else minec[me]
            delta, m2, v2 = _adamw_math(g, w_refs[i][...], m_refs[i][...], v_refs[i][...])
            go[i][...] = g
            do_[i][...] = delta
            mo[i][...] = m2
            vo[i][...] = v2

    vm = pl.BlockSpec(memory_space=pltpu.VMEM)
    shapes = [jax.ShapeDtypeStruct(t.shape, F32) for t in ws]
    res = pl.pallas_call(
        body, name="small_step",
        in_specs=[vm] * (4 * ns + 5), out_specs=[vm] * (4 * ns + 5),
        out_shape=[jax.ShapeDtypeStruct((1, 1), F32)] + shapes * 4,
        scratch_shapes=[pltpu.VMEM((SMALL_ROWS, SMALL_COLS), F32), pltpu.VMEM((N_DEV, SMALL_ROWS, SMALL_COLS), F32),
                        pltpu.VMEM((N_DEV, kk_, ch), F32), pltpu.VMEM((N_DEV, N_DEV, kk_, ch), F32),
                        pltpu.SemaphoreType.DMA((2, N_DEV - 1)), pltpu.SemaphoreType.DMA((2, N_DEV - 1))],
        compiler_params=pltpu.CompilerParams(has_side_effects=True),
    )(*sg, gcw, loss, *ws, *ms, *vs)
    n1 = ns + 1
    return res[0], res[1:1 + n1], res[1 + n1:1 + 2 * n1], res[1 + 2 * n1:1 + 3 * n1], res[1 + 3 * n1:]


def _cast_shard(w, name):
    r = w.shape[0]
    return _rows(lambda t: t, [w], [], [(w.shape[1], CDT)], ts=_pick(r, (256, 128)), name=name)[0]


BIG = ["w_in", "w_mem_kv", "w_up", "w_sb_out", "w_ssd_out", "w_mem_out", "w_o", "w_down"]
LATE_W = BIG[1:]
COL_SHARDED = ("w_in", "w_mem_kv", "w_up")
SMALL = ["norm_mix_pre", "conv_b", "dt_bias", "a_log", "d_skip", "ssd_norm", "norm_mem",
         "norm_mix_post", "norm_mlp_pre", "norm_mlp_post"]
ALL_W = ["norm_mix_pre", "w_in", "conv_w", "conv_b", "dt_bias", "a_log", "d_skip", "ssd_norm", "norm_mem",
         "w_mem_kv", "w_sb_out", "w_ssd_out", "w_mem_out", "w_o", "norm_mix_post", "norm_mlp_pre", "w_up",
         "w_down", "norm_mlp_post"]
LANES = 128


def _pack_rows(vecs):
    parts, offs, off = [], [], 0
    for t in vecs:
        flat = t.reshape(-1)
        n = flat.shape[0]
        rows = -(-n // (8 * LANES)) * 8
        parts.append(jnp.pad(flat, (0, rows * LANES - n)).reshape(rows, LANES))
        offs.append((off, n))
        off += rows
    return jnp.concatenate(parts, axis=0), offs


def _unpack_rows(packed, offs, shapes):
    out = []
    for (off, n), shape in zip(offs, shapes):
        rows = -(-n // (8 * LANES)) * 8
        out.append(packed[off:off + rows].reshape(-1)[:n].reshape(shape))
    return out


def kernel(x, mem, norm_mix_pre, w_in, conv_w, conv_b, dt_bias, a_log, d_skip, ssd_norm, norm_mem, w_mem_kv, w_sb_out, w_ssd_out, w_mem_out, w_o, norm_mix_post, norm_mlp_pre, w_up, w_down, norm_mlp_post, loss_target, m_norm_mix_pre, m_w_in, m_conv_w, m_conv_b, m_dt_bias, m_a_log, m_d_skip, m_ssd_norm, m_norm_mem, m_w_mem_kv, m_w_sb_out, m_w_ssd_out, m_w_mem_out, m_w_o, m_norm_mix_post, m_norm_mlp_pre, m_w_up, m_w_down, m_norm_mlp_post, v_norm_mix_pre, v_w_in, v_conv_w, v_conv_b, v_dt_bias, v_a_log, v_d_skip, v_ssd_norm, v_norm_mem, v_w_mem_kv, v_w_sb_out, v_w_ssd_out, v_w_mem_out, v_w_o, v_norm_mix_post, v_norm_mlp_pre, v_w_up, v_w_down, v_norm_mlp_post):
    wd = dict(norm_mix_pre=norm_mix_pre, w_in=w_in, conv_w=conv_w, conv_b=conv_b, dt_bias=dt_bias, a_log=a_log,
              d_skip=d_skip, ssd_norm=ssd_norm, norm_mem=norm_mem, w_mem_kv=w_mem_kv, w_sb_out=w_sb_out,
              w_ssd_out=w_ssd_out, w_mem_out=w_mem_out, w_o=w_o, norm_mix_post=norm_mix_post,
              norm_mlp_pre=norm_mlp_pre, w_up=w_up, w_down=w_down, norm_mlp_post=norm_mlp_post)
    md = dict(norm_mix_pre=m_norm_mix_pre, w_in=m_w_in, conv_w=m_conv_w, conv_b=m_conv_b, dt_bias=m_dt_bias,
              a_log=m_a_log, d_skip=m_d_skip, ssd_norm=m_ssd_norm, norm_mem=m_norm_mem, w_mem_kv=m_w_mem_kv,
              w_sb_out=m_w_sb_out, w_ssd_out=m_w_ssd_out, w_mem_out=m_w_mem_out, w_o=m_w_o,
              norm_mix_post=m_norm_mix_post, norm_mlp_pre=m_norm_mlp_pre, w_up=m_w_up, w_down=m_w_down,
              norm_mlp_post=m_norm_mlp_post)
    vd = dict(norm_mix_pre=v_norm_mix_pre, w_in=v_w_in, conv_w=v_conv_w, conv_b=v_conv_b, dt_bias=v_dt_bias,
              a_log=v_a_log, d_skip=v_d_skip, ssd_norm=v_ssd_norm, norm_mem=v_norm_mem, w_mem_kv=v_w_mem_kv,
              w_sb_out=v_w_sb_out, w_ssd_out=v_w_ssd_out, w_mem_out=v_w_mem_out, w_o=v_w_o,
              norm_mix_post=v_norm_mix_post, norm_mlp_pre=v_norm_mlp_pre, w_up=v_w_up, w_down=v_w_down,
              norm_mlp_post=v_norm_mlp_post)
    shards = {n: _cast_shard(wd[n][0], "cast_" + n) for n in BIG}
    w_in_g, conv_w_g = _gather_two_level([shards["w_in"], wd["conv_w"][0]], "gather_w_in")
    wt = {"w_in": _full_weight("w_in", w_in_g)}
    ch = conv_w_g.shape[2]

    p = {n: wd[n] for n in SMALL}
    p["conv_w"] = conv_w_g.transpose(1, 0, 2).reshape(CONV_K, N_DEV * ch)
    loss, grad_x, gw, sg, late_received = _local_step(x[0], mem[0], loss_target[0], p, wt,
                                                      [shards[n] for n in LATE_W])

    received = dict(zip(LATE_W, late_received))
    g_in = gw["w_in"]
    by_core = g_in.reshape(g_in.shape[0], N_CHIP, 2, g_in.shape[1] // N_DEV).transpose(2, 1, 0, 3)
    core = lax.axis_index("c")
    mine = lax.dynamic_index_in_dim(by_core, core, 0, keepdims=False)
    for_sibling = lax.dynamic_index_in_dim(by_core, 1 - core, 0, keepdims=False)
    pair = _pair_sum(mine, _pair_exchange(for_sibling, "scatter_w_in_pair"), "sum_w_in_pair")
    received["w_in"] = _chip_scatter(pair, "scatter_w_in_chips")

    grads, deltas, new_m, new_v = {}, {}, {}, {}
    for n in BIG:
        g, dl, m2, v2 = _adamw_reduce(received[n], wd[n][0], md[n][0], vd[n][0], "adamw_" + n)
        grads[n], deltas[n], new_m[n], new_v[n] = g[None], dl[None], m2[None], v2[None]
    small_names = SMALL + ["conv_w"]
    gcw = sg["conv_w"].reshape(CONV_K, N_DEV, ch).transpose(1, 0, 2)
    small_of = lambda dct: [dct[n] for n in SMALL] + [dct["conv_w"][0]]
    loss_red, g_s, d_s, m_s, v_s = _small_step([sg[n] for n in SMALL], gcw, loss, small_of(wd), small_of(md),
                                               small_of(vd))
    for i, n in enumerate(small_names):
        shape = wd[n].shape
        grads[n], deltas[n], new_m[n], new_v[n] = (t.reshape(shape) for t in (g_s[i], d_s[i], m_s[i], v_s[i]))
    loss_out = loss_red.reshape(())

    return (loss_out, grad_x[None], *[grads[n] for n in ALL_W], *[deltas[n] for n in ALL_W],
            *[new_m[n] for n in ALL_W], *[new_v[n] for n in ALL_W])
```

```python
import functools

import jax
import jax.numpy as jnp
from jax import lax
from jax.experimental import pallas as pl
from jax.experimental.pallas import tpu as pltpu

F32 = jnp.float32
BF16 = jnp.bfloat16
CDT = jnp.bfloat16
EPS = 1e-6
VMEM_LIMIT = 56 * 1024 * 1024

N_DEV = 8
D_MODEL = 1024
SB_H, SB_DH = 16, 64
SSD_G, SSD_R, SSD_P, SSD_N, SSD_L = 4, 8, 64, 128, 128
SSD_H = SSD_G * SSD_R
SSD_INNER = SSD_H * SSD_P
CONV_K = 4
CONV_DIM = SSD_INNER + 2 * SSD_G * SSD_N
MEM_H, MEM_DH = 4, 256
DT_PAD = 128
SB_TQ, SB_BK = 1024, 256
CONV_PAD = 8
MM_TILE, MM_TILE_K = 1024, 2048

ADAM_LR, ADAM_B1, ADAM_B2, ADAM_EPS, ADAM_WD, ADAM_STEP = 0.001, 0.9, 0.999, 1e-08, 0.01, 10

NT = (((1,), (1,)), ((), ()))
TN = (((0,), (0,)), ((), ()))
NN = (((1,), (0,)), ((), ()))


def _cparams(sem=None):
    return pltpu.CompilerParams(dimension_semantics=sem, vmem_limit_bytes=VMEM_LIMIT)


def _pick(n, cands):
    for c in cands:
        if n % c == 0:
            return c
    return n


def _dot(a, b, dims=NN):
    return lax.dot_general(a.astype(CDT), b.astype(CDT), dims, preferred_element_type=F32)


def _split_dot(x, t, left, pieces):
    if CDT == F32:
        return lax.dot_general(t, x, NN, preferred_element_type=F32) if left else \
            lax.dot_general(x, t, NN, preferred_element_type=F32)
    acc = None
    rem = x
    for _ in range(pieces):
        hi = rem.astype(BF16)
        rem = rem - hi.astype(F32)
        d = lax.dot_general(t, hi, NN, preferred_element_type=F32) if left else \
            lax.dot_general(hi, t, NN, preferred_element_type=F32)
        acc = d if acc is None else acc + d
    return acc


def _iota(shape, dim):
    return lax.broadcasted_iota(jnp.int32, shape, dim)


def _sigmoid(x):
    return 1.0 / (1.0 + jnp.exp(-x))


def _silu(x):
    return x * _sigmoid(x)


def _dsilu(x):
    s = _sigmoid(x)
    return s * (1.0 + x * (1.0 - s))


def _softplus(x):
    return jnp.maximum(x, 0.0) + jnp.log(1.0 + jnp.exp(-jnp.abs(x)))


def _rms(x, g):
    r = lax.rsqrt(jnp.mean(x * x, axis=-1, keepdims=True) + EPS)
    return x * r * g


def _rms_bwd(x, g, dy):
    r = lax.rsqrt(jnp.mean(x * x, axis=-1, keepdims=True) + EPS)
    n = x * r
    dn = dy * g
    dx = r * (dn - n * jnp.mean(dn * n, axis=-1, keepdims=True))
    dg = jnp.sum(dy * n, axis=0, keepdims=True)
    return dx, dg


def _mm(a, b, *, ta=False, tb=False, out_dtype=F32, name, b_off=0, n=None, b_koff=0, into=None, into_off=0):
    m = a.shape[1] if ta else a.shape[0]
    k = a.shape[0] if ta else a.shape[1]
    if n is None:
        n = b.shape[0] if tb else b.shape[1]
    assert b_koff + k <= (b.shape[1] if tb else b.shape[0])
    bm = _pick(m, (MM_TILE, 512, 256, 128))
    bn = _pick(n, (MM_TILE, 512, 256, 128))
    bk = next(c for c in (MM_TILE_K, 1024, 512, 256, 128, k) if k % c == 0 and b_koff % c == 0)
    nk = k // bk
    assert b_off % bn == 0 and into_off % bn == 0
    jb, jo, kb = b_off // bn, into_off // bn, b_koff // bk
    dims = (((0 if ta else 1,), (1 if tb else 0,)), ((), ()))

    def body(a_ref, b_ref, *rest):
        o_ref, acc_ref = rest[-2:]
        part = _dot(a_ref[...], b_ref[...], dims)
        if nk == 1:
            o_ref[...] = part.astype(o_ref.dtype)
            return
        kk = pl.program_id(2)

        @pl.when(kk == 0)
        def _():
            acc_ref[...] = part

        @pl.when(jnp.logical_and(kk > 0, kk < nk - 1))
        def _():
            acc_ref[...] += part

        @pl.when(kk == nk - 1)
        def _():
            o_ref[...] = (acc_ref[...] + part).astype(o_ref.dtype)

    a_spec = pl.BlockSpec((bk, bm), lambda i, j, kk: (kk, i)) if ta else \
        pl.BlockSpec((bm, bk), lambda i, j, kk: (i, kk))
    b_spec = pl.BlockSpec((bn, bk), lambda i, j, kk: (j + jb, kk + kb)) if tb else \
        pl.BlockSpec((bk, bn), lambda i, j, kk: (kk + kb, j + jb))
    extra = {} if into is None else {"input_output_aliases": {2: 0}}
    return pl.pallas_call(
        body, name=name,
        grid=(m // bm, n // bn, nk),
        in_specs=[a_spec, b_spec] + ([] if into is None else [pl.BlockSpec(memory_space=pl.ANY)]),
        out_specs=pl.BlockSpec((bm, bn), lambda i, j, kk: (i, j + jo)),
        out_shape=jax.ShapeDtypeStruct((m, n), out_dtype) if into is None else
        jax.ShapeDtypeStruct(into.shape, into.dtype),
        scratch_shapes=[pltpu.VMEM((bm, bn) if nk > 1 else (8, 128), F32)],
        compiler_params=_cparams(("parallel", "parallel", "arbitrary")),
        **extra,
    )(*((a, b) if into is None else (a, b, into)))


def _rows(fn, tiled, params, outs, accs=(), *, ts, name):
    s = tiled[0].shape[0]
    ts = min(ts, s)
    assert s % ts == 0
    nt, npar, no, na = len(tiled), len(params), len(outs), len(accs)

    def body(*refs):
        i = pl.program_id(0)
        vals = [r[...] for r in refs[:nt + npar]]
        res = fn(*vals)
        if not isinstance(res, (tuple, list)):
            res = (res,)
        orefs = refs[nt + npar:nt + npar + no]
        arefs = refs[nt + npar + no:]
        for r_, val in zip(orefs, res[:no]):
            r_[...] = val.astype(r_.dtype)
        if na:
            @pl.when(i == 0)
            def _():
                for r_ in arefs:
                    r_[...] = jnp.zeros_like(r_)

            for r_, val in zip(arefs, res[no:]):
                r_[...] += jnp.broadcast_to(val, r_.shape)

    in_specs = [pl.BlockSpec((ts, a.shape[1]), lambda i: (i, 0)) for a in tiled]
    in_specs += [pl.BlockSpec(p.shape, lambda i: (0, 0)) for p in params]
    out_specs = [pl.BlockSpec((ts, w), lambda i: (i, 0)) for (w, _) in outs]
    out_specs += [pl.BlockSpec(shape, lambda i: (0, 0)) for shape in accs]
    out_shape = [jax.ShapeDtypeStruct((s, w), dt) for (w, dt) in outs]
    out_shape += [jax.ShapeDtypeStruct(shape, F32) for shape in accs]
    res = pl.pallas_call(
        body, name=name, grid=(s // ts,),
        in_specs=in_specs, out_specs=out_specs, out_shape=out_shape,
        compiler_params=_cparams(("arbitrary",)),
    )(*tiled, *params)
    return res


def _sb_block(qs, kb, diag):
    tq, bk = qs.shape[0], kb.shape[0]
    z = _dot(qs, kb, NT)
    lb = jnp.minimum(z, 0.0) - jnp.log(1.0 + jnp.exp(-jnp.abs(z)))
    lk = lb - z
    if diag is None:
        return None, lb, lk
    causal = (diag + _iota((tq, bk), 1)) < _iota((tq, bk), 0)
    return causal, lb, jnp.where(causal, lk, 0.0)


def _fused_exchange(scatter, ncols, nsteps):
    def hooks(ins, outs, sems):
        step = pl.program_id(0) * ncols + pl.program_id(1)

        @pl.when(step == 0)
        def _():
            for cp in _exchange_copies(ins, outs, *sems, scatter):
                cp.start()

        def finish():
            @pl.when(step == nsteps - 1)
            def _():
                for cp in _exchange_copies(ins, outs, *sems, scatter):
                    cp.wait()
        return finish
    return hooks


def _sb_fwd(q, k, v, comm=()):
    h, s, dh = q.shape
    tq, bk = min(SB_TQ, s), min(SB_BK, s)
    scale = dh ** -0.5
    nc = len(comm)
    hooks = _fused_exchange(False, s // tq, h * (s // tq))

    def body(q_ref, k_ref, v_ref, *rest):
        o_ref, lt_ref = rest[nc:nc + 2]
        if nc:
            finish = hooks(rest[:nc], rest[nc + 2:2 * nc + 2], rest[2 * nc + 2:])
        i = pl.program_id(1)
        q0 = i * tq
        qs = q_ref[...] * scale
        tri = (_iota((bk, bk), 0) > _iota((bk, bk), 1)).astype(CDT)

        def step(k0, carry, diag, r0=0):
            cf, acc = carry
            kb = k_ref[pl.ds(k0, bk), :]
            vb = v_ref[pl.ds(k0, bk), :]
            causal, lb, lk = _sb_block(qs[r0:], kb, diag)
            w = jnp.exp(lb + cf + _split_dot(lk, tri, False, 2))
            if causal is not None:
                w = jnp.where(causal, w, 0.0)
            return cf + jnp.sum(lk, axis=1, keepdims=True), acc + _dot(w, vb)

        carry = (jnp.zeros((tq, 1), F32), jnp.zeros((tq, dh), F32))
        for d in reversed(range(tq // bk)):
            r0 = d * bk
            sub = step(pl.multiple_of(q0 + r0, bk), tuple(t[r0:] for t in carry), 0, r0)
            carry = tuple(jnp.concatenate([t[:r0], u], axis=0) if r0 else u for t, u in zip(carry, sub))
        nfull = q0 // bk
        cf, acc = lax.fori_loop(
            0, nfull, lambda jj, c: step(pl.multiple_of((nfull - 1 - jj) * bk, bk), c, None), carry)
        o_ref[...] = acc
        lt_ref[...] = cf
        if nc:
            finish()

    return pl.pallas_call(
        body, name="sb_fwd", grid=(h, s // tq),
        in_specs=[pl.BlockSpec((None, tq, dh), lambda a, i: (a, i, 0)),
                  pl.BlockSpec((None, s, dh), lambda a, i: (a, 0, 0)),
                  pl.BlockSpec((None, s, dh), lambda a, i: (a, 0, 0))] + [HBM] * nc,
        out_specs=[pl.BlockSpec((None, tq, dh), lambda a, i: (a, i, 0)),
                   pl.BlockSpec((None, tq, 1), lambda a, i: (a, i, 0))] + [HBM] * nc,
        out_shape=[jax.ShapeDtypeStruct((h, s, dh), F32), jax.ShapeDtypeStruct((h, s, 1), F32)]
        + _exchange_shapes(comm, False),
        scratch_shapes=_exchange_sems(nc) if nc else [],
        compiler_params=_cparams(("arbitrary", "arbitrary")),
    )(q, k, v, *comm)


def _sb_bwd(q, k, v, ltot, do, comm=()):
    h, s, dh = q.shape
    tq, bk = min(SB_TQ, s), min(SB_BK, s)
    scale = dh ** -0.5
    nc = len(comm)
    hooks = _fused_exchange(True, s // tq, h * (s // tq))

    def body(q_ref, k_ref, v_ref, lt_ref, do_ref, *rest):
        dq_ref, dk_ref, dv_ref = rest[nc:nc + 3]
        if nc:
            finish = hooks(rest[:nc], rest[nc + 3:2 * nc + 3], rest[2 * nc + 3:])
        i = pl.program_id(1)

        @pl.when(i == 0)
        def _():
            dk_ref[...] = jnp.zeros_like(dk_ref)
            dv_ref[...] = jnp.zeros_like(dv_ref)

        q0 = i * tq
        qs = q_ref[...] * scale
        dov = do_ref[...].astype(CDT)
        ltot = lt_ref[...]
        tri_le = (_iota((bk, bk), 0) <= _iota((bk, bk), 1)).astype(CDT)
        tri_lt = (_iota((bk, bk), 0) < _iota((bk, bk), 1)).astype(CDT)

        def step(k0, carry, diag, r0=0):
            cf, cg, dq = carry
            kb = k_ref[pl.ds(k0, bk), :]
            vb = v_ref[pl.ds(k0, bk), :]
            causal, lb, lk = _sb_block(qs[r0:], kb, diag)
            w = jnp.exp(lb + ltot[r0:] - (cf + _split_dot(lk, tri_le, False, 2)))
            if causal is not None:
                w = jnp.where(causal, w, 0.0)
            g = w * _dot(dov[r0:], vb, NT)
            gsum = cg + _split_dot(g, tri_lt, False, 2)
            dz = g - (g + gsum) * jnp.exp(lb)
            if causal is not None:
                dz = jnp.where(causal, dz, 0.0)
            dzc = dz.astype(CDT)
            dk_ref[pl.ds(k0, bk), :] += _dot(dzc, qs[r0:], TN)
            dv_ref[pl.ds(k0, bk), :] += _dot(w, dov[r0:], TN)
            return (cf + jnp.sum(lk, axis=1, keepdims=True), cg + jnp.sum(g, axis=1, keepdims=True),
                    dq + _dot(dzc, kb))

        carry = (jnp.zeros((tq, 1), F32), jnp.zeros((tq, 1), F32), jnp.zeros((tq, dh), F32))
        carry = lax.fori_loop(0, q0 // bk, lambda jj, c: step(pl.multiple_of(jj * bk, bk), c, None), carry)
        for d in range(tq // bk):
            r0 = d * bk
            sub = step(pl.multiple_of(q0 + r0, bk), tuple(t[r0:] for t in carry), 0, r0)
            carry = tuple(jnp.concatenate([t[:r0], u], axis=0) if r0 else u for t, u in zip(carry, sub))
        dq_ref[...] = carry[2] * scale
        if nc:
            finish()

    tile = pl.BlockSpec((None, tq, dh), lambda a, i: (a, i, 0))
    full = pl.BlockSpec((None, s, dh), lambda a, i: (a, 0, 0))
    shp = jax.ShapeDtypeStruct((h, s, dh), F32)
    return pl.pallas_call(
        body, name="sb_bwd", grid=(h, s // tq),
        in_specs=[tile, full, full, pl.BlockSpec((None, tq, 1), lambda a, i: (a, i, 0)), tile] + [HBM] * nc,
        out_specs=[tile, full, full] + [HBM] * nc,
        out_shape=[shp, shp, shp] + _exchange_shapes(comm, True),
        scratch_shapes=_exchange_sems(nc) if nc else [],
        compiler_params=_cparams(("arbitrary", "arbitrary")),
    )(q, k, v, ltot, do, *comm)


def _pick_lane(tile, r):
    return jnp.sum(jnp.where(_iota(tile.shape, 1) == r, tile, 0.0), axis=1, keepdims=True)


def _pick_row(tile, r):
    return jnp.sum(jnp.where(_iota(tile.shape, 0) == r, tile, 0.0), axis=0, keepdims=True)


def _ssd_chunk_setup(c_ref, b_ref, dac_ref, dar_ref, cb_ref, acsc_ref, acsr_ref):
    l = SSD_L
    tdt = F32 if CDT == F32 else BF16
    lower = (_iota((l, l), 1) <= _iota((l, l), 0)).astype(tdt)
    upper = (_iota((l, l), 0) <= _iota((l, l), 1)).astype(tdt)
    cb_ref[...] = _dot(c_ref[...], b_ref[...], NT)
    acsc_ref[...] = _split_dot(dac_ref[...], lower, True, 3)
    acsr_ref[...] = _split_dot(dar_ref[...], upper, False, 3)


def _ssd_fwd(xh, dtc, dac, dar, dsk, xbc):
    hh, s, p = xh.shape
    l, n, g_, r_ = SSD_L, SSD_N, SSD_G, SSD_R
    nc = s // l
    boff = SSD_INNER // n
    coff = boff + g_

    def body(x_ref, dtc_ref, dac_ref, dar_ref, dsk_ref, b_ref, c_ref, y_ref, st_ref,
             state_ref, cb_ref, acsc_ref, acsr_ref):
        c = pl.program_id(1)
        r = pl.program_id(2)

        @pl.when(r == 0)
        def _():
            _ssd_chunk_setup(c_ref, b_ref, dac_ref, dar_ref, cb_ref, acsc_ref, acsr_ref)

        @pl.when(c == 0)
        def _():
            state_ref[r] = jnp.zeros((n, p), F32)

        a_col = _pick_lane(acsc_ref[...], r)
        a_row = _pick_row(acsr_ref[...], r)
        dt_col = _pick_lane(dtc_ref[...], r)
        dsk_h = _pick_lane(dsk_ref[...], r)
        xv = x_ref[...]
        xd = xv * dt_col
        mask = _iota((l, l), 1) <= _iota((l, l), 0)
        decay = jnp.where(mask, jnp.exp(jnp.minimum(a_col - a_row, 0.0)), 0.0)
        w = cb_ref[...] * decay
        hprev = state_ref[r]
        cv = c_ref[...]
        y = _dot(w, xd) + jnp.exp(a_col) * _dot(cv, hprev)
        y_ref[...] = y + dsk_h * xv
        a_end = a_col[l - 1:l, :]
        dte = jnp.exp(a_end - a_col)
        st_ref[...] = hprev
        state_ref[r] = hprev * jnp.exp(a_end) + _dot(b_ref[...], xd * dte, TN)

    return pl.pallas_call(
        body, name="ssd_fwd", grid=(g_, nc, r_),
        in_specs=[pl.BlockSpec((None, l, p), lambda g, c, r: (g * r_ + r, c, 0)),
                  pl.BlockSpec((None, l, r_), lambda g, c, r: (g, c, 0)),
                  pl.BlockSpec((None, l, r_), lambda g, c, r: (g, c, 0)),
                  pl.BlockSpec((None, r_, l), lambda g, c, r: (g, 0, c)),
                  pl.BlockSpec((None, 1, r_), lambda g, c, r: (g, 0, 0)),
                  pl.BlockSpec((l, n), lambda g, c, r: (c, boff + g)),
                  pl.BlockSpec((l, n), lambda g, c, r: (c, coff + g))],
        out_specs=[pl.BlockSpec((None, l, p), lambda g, c, r: (g * r_ + r, c, 0)),
                   pl.BlockSpec((None, None, n, p), lambda g, c, r: (g * r_ + r, c, 0, 0))],
        out_shape=[jax.ShapeDtypeStruct((hh, s, p), F32),
                   jax.ShapeDtypeStruct((hh, nc, n, p), F32)],
        scratch_shapes=[pltpu.VMEM((r_, n, p), F32), pltpu.VMEM((l, l), F32),
                        pltpu.VMEM((l, r_), F32), pltpu.VMEM((r_, l), F32)],
        compiler_params=_cparams(("parallel", "arbitrary", "arbitrary")),
    )(xh, dtc, dac, dar, dsk, xbc, xbc)


def _ssd_bwd(xh, dtc, dac, dar, dsk, xbc, st, dy):
    hh, s, p = xh.shape
    l, n, g_, r_ = SSD_L, SSD_N, SSD_G, SSD_R
    nc = s // l
    boff = SSD_INNER // n
    coff = boff + g_

    def body(x_ref, dtc_ref, dac_ref, dar_ref, dsk_ref, b_ref, c_ref, st_ref, dy_ref,
             dx_ref, dda_ref, dxx_ref, db_ref, dc_ref, dah_ref, ddsk_ref,
             dstate_ref, cb_ref, acsc_ref, acsr_ref):
        c = pl.program_id(1)
        r = pl.program_id(2)

        @pl.when(r == 0)
        def _():
            _ssd_chunk_setup(c_ref, b_ref, dac_ref, dar_ref, cb_ref, acsc_ref, acsr_ref)
            db_ref[...] = jnp.zeros_like(db_ref)
            dc_ref[...] = jnp.zeros_like(dc_ref)
            dda_ref[...] = jnp.zeros_like(dda_ref)
            dxx_ref[...] = jnp.zeros_like(dxx_ref)

        @pl.when(jnp.logical_and(c == 0, r == 0))
        def _():
            dah_ref[...] = jnp.zeros_like(dah_ref)
            ddsk_ref[...] = jnp.zeros_like(ddsk_ref)

        @pl.when(c == 0)
        def _():
            dstate_ref[r] = jnp.zeros((n, p), F32)

        a_col = _pick_lane(acsc_ref[...], r)
        a_row = _pick_row(acsr_ref[...], r)
        dt_col = _pick_lane(dtc_ref[...], r)
        dsk_h = _pick_lane(dsk_ref[...], r)
        xv = x_ref[...]
        dyv = dy_ref[...]
        xd = xv * dt_col
        il = _iota((l, l), 0)
        isx = _iota((l, l), 1)
        decay = jnp.where(isx <= il, jnp.exp(jnp.minimum(a_col - a_row, 0.0)), 0.0)
        cb = cb_ref[...]
        w = cb * decay
        dhn = dstate_ref[r]
        hc = st_ref[...]
        bv = b_ref[...]
        cv = c_ref[...]
        a_end = a_col[l - 1:l, :]
        ea = jnp.exp(a_col)
        dte = jnp.exp(a_end - a_col)

        dx_state = dte * _dot(bv, dhn)
        dxd = _dot(w, dyv, TN) + dx_state
        md = decay * _dot(dyv, xd, NT)
        m = md * cb
        dc_ref[...] += _dot(md, bv) + ea * _dot(dyv, hc, NT)
        db_ref[...] += _dot(md, cv, TN) + dte * _dot(xd, dhn, NT)
        dstate_ref[r] = jnp.exp(a_end) * dhn + _dot(cv, dyv * ea, TN)

        tdt = F32 if CDT == F32 else BF16
        t1 = (isx >= il).astype(tdt)
        yoff = ea * _dot(cv, hc)
        xdx = jnp.sum(xd * dx_state, axis=1, keepdims=True)
        vec = jnp.sum(dyv * yoff, axis=1, keepdims=True) - xdx
        end_term = jnp.sum(xdx, axis=0, keepdims=True) + \
            jnp.exp(a_end) * jnp.sum(jnp.sum(hc * dhn, axis=1, keepdims=True), axis=0, keepdims=True)
        zmat = _split_dot(m, t1, True, 2)
        span = jnp.sum(jnp.where(isx < il, zmat, 0.0), axis=1, keepdims=True)
        rc = _split_dot(jnp.broadcast_to(vec, (l, 128)), t1, True, 2)[:, :1]
        dda = span + rc + end_term
        dxx = jnp.sum(dxd * xv, axis=1, keepdims=True)

        lane = _iota((l, r_), 1) == r
        dda_ref[...] += jnp.where(lane, dda, 0.0)
        dxx_ref[...] += jnp.where(lane, dxx, 0.0)
        dx_ref[...] = dxd * dt_col + dsk_h * dyv
        lane1 = _iota((1, r_), 1) == r
        dah_ref[...] += jnp.where(lane1, jnp.sum(dda * dt_col, axis=0, keepdims=True), 0.0)
        ddsk_ref[...] += jnp.where(
            lane1, jnp.sum(jnp.sum(dyv * xv, axis=1, keepdims=True), axis=0, keepdims=True), 0.0)

    rev = lambda c: nc - 1 - c
    xspec = pl.BlockSpec((None, l, p), lambda g, c, r: (g * r_ + r, rev(c), 0))
    cspec = pl.BlockSpec((None, l, r_), lambda g, c, r: (g, rev(c), 0))
    hspec = pl.BlockSpec((None, 1, r_), lambda g, c, r: (g, 0, 0))
    return pl.pallas_call(
        body, name="ssd_bwd", grid=(g_, nc, r_),
        in_specs=[xspec, cspec, cspec,
                  pl.BlockSpec((None, r_, l), lambda g, c, r: (g, 0, rev(c))),
                  hspec,
                  pl.BlockSpec((l, n), lambda g, c, r: (rev(c), boff + g)),
                  pl.BlockSpec((l, n), lambda g, c, r: (rev(c), coff + g)),
                  pl.BlockSpec((None, None, n, p), lambda g, c, r: (g * r_ + r, rev(c), 0, 0)),
                  xspec],
        out_specs=[xspec, cspec, cspec,
                   pl.BlockSpec((l, n), lambda g, c, r: (rev(c), g)),
                   pl.BlockSpec((l, n), lambda g, c, r: (rev(c), g)),
                   hspec, hspec],
        out_shape=[jax.ShapeDtypeStruct((hh, s, p), F32),
                   jax.ShapeDtypeStruct((g_, s, r_), F32),
                   jax.ShapeDtypeStruct((g_, s, r_), F32),
                   jax.ShapeDtypeStruct((s, g_ * n), F32),
                   jax.ShapeDtypeStruct((s, g_ * n), F32),
                   jax.ShapeDtypeStruct((g_, 1, r_), F32),
                   jax.ShapeDtypeStruct((g_, 1, r_), F32)],
        scratch_shapes=[pltpu.VMEM((r_, n, p), F32), pltpu.VMEM((l, l), F32),
                        pltpu.VMEM((l, r_), F32), pltpu.VMEM((r_, l), F32)],
        compiler_params=_cparams(("parallel", "arbitrary", "arbitrary")),
    )(xh, dtc, dac, dar, dsk, xbc, xbc, st, dy)


def _ssd_chunk_common(c_ref, b_ref, dac_ref, dar_ref):
    l = SSD_L
    tdt = F32 if CDT == F32 else BF16
    lower = (_iota((l, l), 1) <= _iota((l, l), 0)).astype(tdt)
    upper = (_iota((l, l), 0) <= _iota((l, l), 1)).astype(tdt)
    cb = _dot(c_ref[...], b_ref[...], NT)
    return cb, _split_dot(dac_ref[...], lower, True, 3), _split_dot(dar_ref[...], upper, False, 3)


def _ssd_fwd_g(xh, dtc, dac, dar, dsk, xbc):
    hh, s, p = xh.shape
    l, n, g_, r_ = SSD_L, SSD_N, SSD_G, SSD_R
    nc = s // l
    boff = SSD_INNER // n
    coff = boff + g_

    def body(x_ref, dtc_ref, dac_ref, dar_ref, dsk_ref, b_ref, c_ref, y_ref, st_ref, state_ref):
        c = pl.program_id(1)

        @pl.when(c == 0)
        def _():
            state_ref[...] = jnp.zeros_like(state_ref)

        cb, acs_c, acs_r = _ssd_chunk_common(c_ref, b_ref, dac_ref, dar_ref)
        mask = _iota((l, l), 1) <= _iota((l, l), 0)
        cv, bv = c_ref[...], b_ref[...]
        dtcv, dskv = dtc_ref[...], dsk_ref[...]
        for r in range(r_):
            a_col = _pick_lane(acs_c, r)
            a_row = _pick_row(acs_r, r)
            dt_col = _pick_lane(dtcv, r)
            dsk_h = _pick_lane(dskv, r)
            xv = x_ref[r]
            xd = xv * dt_col
            decay = jnp.where(mask, jnp.exp(jnp.minimum(a_col - a_row, 0.0)), 0.0)
            hprev = state_ref[r]
            y = _dot(cb * decay, xd) + jnp.exp(a_col) * _dot(cv, hprev)
            y_ref[r] = y + dsk_h * xv
            a_end = a_col[l - 1:l, :]
            st_ref[r] = hprev
            state_ref[r] = hprev * jnp.exp(a_end) + _dot(bv, xd * jnp.exp(a_end - a_col), TN)

    return pl.pallas_call(
        body, name="ssd_fwd", grid=(g_, nc),
        in_specs=[pl.BlockSpec((r_, l, p), lambda g, c: (g, c, 0)),
                  pl.BlockSpec((None, l, r_), lambda g, c: (g, c, 0)),
                  pl.BlockSpec((None, l, r_), lambda g, c: (g, c, 0)),
                  pl.BlockSpec((None, r_, l), lambda g, c: (g, 0, c)),
                  pl.BlockSpec((None, 1, r_), lambda g, c: (g, 0, 0)),
                  pl.BlockSpec((l, n), lambda g, c: (c, boff + g)),
                  pl.BlockSpec((l, n), lambda g, c: (c, coff + g))],
        out_specs=[pl.BlockSpec((r_, l, p), lambda g, c: (g, c, 0)),
                   pl.BlockSpec((r_, None, n, p), lambda g, c: (g, c, 0, 0))],
        out_shape=[jax.ShapeDtypeStruct((hh, s, p), F32),
                   jax.ShapeDtypeStruct((hh, nc, n, p), F32)],
        scratch_shapes=[pltpu.VMEM((r_, n, p), F32)],
        compiler_params=_cparams(("parallel", "arbitrary")),
    )(xh, dtc, dac, dar, dsk, xbc, xbc)


def _ssd_bwd_g(xh, dtc, dac, dar, dsk, xbc, st, dy):
    hh, s, p = xh.shape
    l, n, g_, r_ = SSD_L, SSD_N, SSD_G, SSD_R
    nc = s // l
    boff = SSD_INNER // n
    coff = boff + g_

    def body(x_ref, dtc_ref, dac_ref, dar_ref, dsk_ref, b_ref, c_ref, st_ref, dy_ref,
             dx_ref, dda_ref, dxx_ref, db_ref, dc_ref, dah_ref, ddsk_ref, dstate_ref):
        c = pl.program_id(1)

        @pl.when(c == 0)
        def _():
            dstate_ref[...] = jnp.zeros_like(dstate_ref)
            dah_ref[...] = jnp.zeros_like(dah_ref)
            ddsk_ref[...] = jnp.zeros_like(ddsk_ref)

        cb, acs_c, acs_r = _ssd_chunk_common(c_ref, b_ref, dac_ref, dar_ref)
        il = _iota((l, l), 0)
        isx = _iota((l, l), 1)
        tdt = F32 if CDT == F32 else BF16
        t1 = (isx >= il).astype(tdt)
        cv, bv = c_ref[...], b_ref[...]
        dtcv, dskv = dtc_ref[...], dsk_ref[...]
        lane = _iota((l, r_), 1)
        lane1 = _iota((1, r_), 1)
        dda_all = jnp.zeros((l, r_), F32)
        dxx_all = jnp.zeros((l, r_), F32)
        dah_all = jnp.zeros((1, r_), F32)
        ddsk_all = jnp.zeros((1, r_), F32)
        db_acc = jnp.zeros((l, n), F32)
        dc_acc = jnp.zeros((l, n), F32)
        for r in range(r_):
            a_col = _pick_lane(acs_c, r)
            a_row = _pick_row(acs_r, r)
            dt_col = _pick_lane(dtcv, r)
            dsk_h = _pick_lane(dskv, r)
            xv = x_ref[r]
            dyv = dy_ref[r]
            xd = xv * dt_col
            decay = jnp.where(isx <= il, jnp.exp(jnp.minimum(a_col - a_row, 0.0)), 0.0)
            dhn = dstate_ref[r]
            hc = st_ref[r]
            a_end = a_col[l - 1:l, :]
            ea = jnp.exp(a_col)
            dte = jnp.exp(a_end - a_col)

            dx_state = dte * _dot(bv, dhn)
            dxd = _dot(cb * decay, dyv, TN) + dx_state
            md = decay * _dot(dyv, xd, NT)
            dc_acc = dc_acc + _dot(md, bv) + ea * _dot(dyv, hc, NT)
            db_acc = db_acc + _dot(md, cv, TN) + dte * _dot(xd, dhn, NT)
            dstate_ref[r] = jnp.exp(a_end) * dhn + _dot(cv, dyv * ea, TN)

            yoff = ea * _dot(cv, hc)
            xdx = jnp.sum(xd * dx_state, axis=1, keepdims=True)
            vec = jnp.sum(dyv * yoff, axis=1, keepdims=True) - xdx
            end_term = jnp.sum(xdx, axis=0, keepdims=True) + \
                jnp.exp(a_end) * jnp.sum(jnp.sum(hc * dhn, axis=1, keepdims=True), axis=0, keepdims=True)
            zmat = _split_dot(md * cb, t1, True, 2)
            span = jnp.sum(jnp.where(isx < il, zmat, 0.0), axis=1, keepdims=True)
            rc = _split_dot(jnp.broadcast_to(vec, (l, 128)), t1, True, 2)[:, :1]
            dda = span + rc + end_term
            dda_all = jnp.where(lane == r, dda, dda_all)
            dxx_all = jnp.where(lane == r, jnp.sum(dxd * xv, axis=1, keepdims=True), dxx_all)
            dx_ref[r] = dxd * dt_col + dsk_h * dyv
            dah_all = jnp.where(lane1 == r, jnp.sum(dda * dt_col, axis=0, keepdims=True), dah_all)
            ddsk_all = jnp.where(
                lane1 == r, jnp.sum(jnp.sum(dyv * xv, axis=1, keepdims=True), axis=0, keepdims=True), ddsk_all)
        dda_ref[...] = dda_all
        dxx_ref[...] = dxx_all
        db_ref[...] = db_acc
        dc_ref[...] = dc_acc
        dah_ref[...] += dah_all
        ddsk_ref[...] += ddsk_all

    rev = lambda c: nc - 1 - c
    xspec = pl.BlockSpec((r_, l, p), lambda g, c: (g, rev(c), 0))
    cspec = pl.BlockSpec((None, l, r_), lambda g, c: (g, rev(c), 0))
    hspec = pl.BlockSpec((None, 1, r_), lambda g, c: (g, 0, 0))
    return pl.pallas_call(
        body, name="ssd_bwd", grid=(g_, nc),
        in_specs=[xspec, cspec, cspec,
                  pl.BlockSpec((None, r_, l), lambda g, c: (g, 0, rev(c))),
                  hspec,
                  pl.BlockSpec((l, n), lambda g, c: (rev(c), boff + g)),
                  pl.BlockSpec((l, n), lambda g, c: (rev(c), coff + g)),
                  pl.BlockSpec((r_, None, n, p), lambda g, c: (g, rev(c), 0, 0)),
                  xspec],
        out_specs=[xspec, cspec, cspec,
                   pl.BlockSpec((l, n), lambda g, c: (rev(c), g)),
                   pl.BlockSpec((l, n), lambda g, c: (rev(c), g)),
                   hspec, hspec],
        out_shape=[jax.ShapeDtypeStruct((hh, s, p), F32),
                   jax.ShapeDtypeStruct((g_, s, r_), F32),
                   jax.ShapeDtypeStruct((g_, s, r_), F32),
                   jax.ShapeDtypeStruct((s, g_ * n), F32),
                   jax.ShapeDtypeStruct((s, g_ * n), F32),
                   jax.ShapeDtypeStruct((g_, 1, r_), F32),
                   jax.ShapeDtypeStruct((g_, 1, r_), F32)],
        scratch_shapes=[pltpu.VMEM((r_, n, p), F32)],
        compiler_params=_cparams(("parallel", "arbitrary")),
    )(xh, dtc, dac, dar, dsk, xbc, xbc, st, dy)


CONV_TC = 256
CONV_RC = 512


def _conv_taps(x_ref, head_ref, rc):
    base = CONV_PAD - (CONV_K - 1)
    head_ref[pl.ds(0, CONV_PAD), :] = jnp.zeros((CONV_PAD, head_ref.shape[1]), F32)
    head_ref[pl.ds(CONV_PAD, rc), :] = x_ref[pl.ds(0, rc), :]

    def tap(t0, kk):
        if t0 == 0:
            return head_ref[pl.ds(base + kk, rc), :]
        return x_ref[pl.ds(t0 - (CONV_K - 1) + kk, rc), :]
    return tap


def _conv_fwd(x, w, b):
    s, ch = x.shape
    rc = min(CONV_RC, s)

    def body(x_ref, w_ref, b_ref, pre_ref, act_ref, head_ref):
        wv = w_ref[...]
        tap = _conv_taps(x_ref, head_ref, rc)
        for t0 in range(0, s, rc):
            acc = jnp.broadcast_to(b_ref[...], (rc, CONV_TC))
            for kk in range(CONV_K):
                acc = acc + wv[kk:kk + 1, :] * tap(t0, kk)
            pre_ref[pl.ds(t0, rc), :] = acc
            act_ref[pl.ds(t0, rc), :] = _silu(acc)

    col = pl.BlockSpec((s, CONV_TC), lambda j: (0, j))
    shp = jax.ShapeDtypeStruct((s, ch), F32)
    return pl.pallas_call(
        body, name="conv_fwd", grid=(ch // CONV_TC,),
        in_specs=[col, pl.BlockSpec((CONV_K, CONV_TC), lambda j: (0, j)),
                  pl.BlockSpec((1, CONV_TC), lambda j: (0, j))],
        out_specs=[col, col],
        out_shape=[shp, shp],
        scratch_shapes=[pltpu.VMEM((CONV_PAD + rc, CONV_TC), F32)],
        compiler_params=_cparams(("parallel",)),
    )(x, w, b)


def _conv_bwd(x, pre, dact, w):
    s, ch = x.shape
    rc = min(CONV_RC, s)

    def body(x_ref, pre_ref, da_ref, w_ref, dx_ref, dw_ref, db_ref, dpre_ref, head_ref):
        wv = w_ref[...]
        tap = _conv_taps(x_ref, head_ref, rc)
        for t0 in range(0, s, rc):
            dpre_ref[pl.ds(t0, rc), :] = da_ref[pl.ds(t0, rc), :] * _dsilu(pre_ref[pl.ds(t0, rc), :])
        dpre_ref[pl.ds(s, CONV_PAD), :] = jnp.zeros((CONV_PAD, CONV_TC), F32)
        dws = [jnp.zeros((1, CONV_TC), F32) for _ in range(CONV_K)]
        dbs = jnp.zeros((1, CONV_TC), F32)
        for t0 in range(0, s, rc):
            acc = jnp.zeros((rc, CONV_TC), F32)
            dp = dpre_ref[pl.ds(t0, rc), :]
            for kk in range(CONV_K):
                acc = acc + wv[kk:kk + 1, :] * dpre_ref[pl.ds(t0 + CONV_K - 1 - kk, rc), :]
                dws[kk] = dws[kk] + jnp.sum(dp * tap(t0, kk), axis=0, keepdims=True)
            dbs = dbs + jnp.sum(dp, axis=0, keepdims=True)
            dx_ref[pl.ds(t0, rc), :] = acc.astype(dx_ref.dtype)
        for kk in range(CONV_K):
            dw_ref[kk:kk + 1, :] = dws[kk]
        db_ref[...] = dbs

    col = pl.BlockSpec((s, CONV_TC), lambda j: (0, j))
    return pl.pallas_call(
        body, name="conv_bwd", grid=(ch // CONV_TC,),
        in_specs=[col, col, col, pl.BlockSpec((CONV_K, CONV_TC), lambda j: (0, j))],
        out_specs=[col, pl.BlockSpec((CONV_K, CONV_TC), lambda j: (0, j)),
                   pl.BlockSpec((1, CONV_TC), lambda j: (0, j))],
        out_shape=[jax.ShapeDtypeStruct((s, ch), CDT),
                   jax.ShapeDtypeStruct((CONV_K, ch), F32),
                   jax.ShapeDtypeStruct((1, ch), F32)],
        scratch_shapes=[pltpu.VMEM((s + CONV_PAD, CONV_TC), F32), pltpu.VMEM((CONV_PAD + rc, CONV_TC), F32)],
        compiler_params=_cparams(("parallel",)),
    )(x, pre, dact, w)


MEM_TS = 512


def _mem_fwd(mq, kv):
    s = mq.shape[0]
    m = kv.shape[0]
    ts = min(MEM_TS, s)
    scale = MEM_DH ** -0.5

    def body(q_ref, k_ref, v_ref, o_ref):
        sc = _dot(q_ref[...], k_ref[...], NT) * scale
        e = jnp.exp(sc - jnp.max(sc, axis=1, keepdims=True))
        pr = e / jnp.sum(e, axis=1, keepdims=True)
        o_ref[...] = _dot(pr, v_ref[...]).astype(o_ref.dtype)

    return pl.pallas_call(
        body, name="mem_fwd", grid=(MEM_H, s // ts),
        in_specs=[pl.BlockSpec((ts, MEM_DH), lambda a, i: (i, a)),
                  pl.BlockSpec((m, MEM_DH), lambda a, i: (0, a)),
                  pl.BlockSpec((m, MEM_DH), lambda a, i: (0, MEM_H + a))],
        out_specs=pl.BlockSpec((ts, MEM_DH), lambda a, i: (i, a)),
        out_shape=jax.ShapeDtypeStruct((s, MEM_H * MEM_DH), CDT),
        compiler_params=_cparams(("parallel", "arbitrary")),
    )(mq, kv, kv)


def _mem_bwd(mq, kv, do):
    s = mq.shape[0]
    m = kv.shape[0]
    ts = min(MEM_TS, s)
    scale = MEM_DH ** -0.5

    def body(q_ref, k_ref, v_ref, do_ref, dq_ref, dk_ref, dv_ref):
        i = pl.program_id(1)

        @pl.when(i == 0)
        def _():
            dk_ref[...] = jnp.zeros_like(dk_ref)
            dv_ref[...] = jnp.zeros_like(dv_ref)

        qv, kb, vb, dov = q_ref[...], k_ref[...], v_ref[...], do_ref[...]
        sc = _dot(qv, kb, NT) * scale
        e = jnp.exp(sc - jnp.max(sc, axis=1, keepdims=True))
        pr = e / jnp.sum(e, axis=1, keepdims=True)
        dp = _dot(dov, vb, NT)
        ds = pr * (dp - jnp.sum(dp * pr, axis=1, keepdims=True)) * scale
        dq_ref[...] = _dot(ds, kb).astype(dq_ref.dtype)
        dk_ref[...] += _dot(ds, qv, TN)
        dv_ref[...] += _dot(pr, dov, TN)

    tile = pl.BlockSpec((ts, MEM_DH), lambda a, i: (i, a))
    kvo = pl.BlockSpec((m, MEM_DH), lambda a, i: (0, a))
    return pl.pallas_call(
        body, name="mem_bwd", grid=(MEM_H, s // ts),
        in_specs=[tile, kvo, pl.BlockSpec((m, MEM_DH), lambda a, i: (0, MEM_H + a)), tile],
        out_specs=[tile, kvo, kvo],
        out_shape=[jax.ShapeDtypeStruct((s, MEM_H * MEM_DH), CDT),
                   jax.ShapeDtypeStruct((m, MEM_H * MEM_DH), F32),
                   jax.ShapeDtypeStruct((m, MEM_H * MEM_DH), F32)],
        compiler_params=_cparams(("parallel", "arbitrary")),
    )(mq, kv, kv, do)


def _heads(t, nh, dh):
    return t.reshape(t.shape[0], nh, dh).transpose(1, 0, 2)


def _unheads(t):
    return t.transpose(1, 0, 2).reshape(t.shape[1], t.shape[0] * t.shape[2])


def _group_cols(t):
    return t.reshape(t.shape[0], SSD_G, SSD_R).transpose(1, 0, 2)


def _pad_cols(t, width):
    return jnp.pad(t, ((0, 0), (0, width - t.shape[1])))


def _full_weight(name, gathered):
    if name in COL_SHARDED:
        return gathered.transpose(1, 0, 2).reshape(gathered.shape[1], N_DEV * gathered.shape[2])
    return gathered.reshape(N_DEV * gathered.shape[1], gathered.shape[2])


def _grad_payload(name, g):
    if name in COL_SHARDED:
        return g.reshape(g.shape[0], N_DEV, g.shape[1] // N_DEV).transpose(1, 0, 2)
    return g.reshape(N_DEV, g.shape[0] // N_DEV, g.shape[1])


def _local_step(x, mem, tgt, p, wt, shards=None):
    s, d = x.shape
    wt = dict(wt)
    c1, c2, c3, c4, c5 = 3 * d, 3 * d + SSD_INNER, 3 * d + SSD_INNER + CONV_DIM, \
        3 * d + SSD_INNER + CONV_DIM + SSD_H, 3 * d + SSD_INNER + CONV_DIM + SSD_H + d
    w_in = wt["w_in"]
    w_main = jnp.concatenate([w_in[:, :c3], w_in[:, c4:]], axis=1)
    w_dt = _pad_cols(w_in[:, c3:c4], DT_PAD)
    seg_name = ["qkv", "z", "xbc", "mq", "gl"]
    seg_dtype = [CDT, F32, F32, CDT, F32]
    seg_off = [0, c1, c2, c3, c3 + d]
    seg_n = [c1, c2 - c1, c3 - c2, d, 3 * d]

    u = _rows(lambda xv, g: _rms(xv, g), [x], [p["norm_mix_pre"]], [(d, CDT)], ts=512, name="f_norm_pre")[0]
    qkv, z, xbc_raw, mq, gl = [
        _mm(u, w_main, b_off=seg_off[i], n=seg_n[i], out_dtype=seg_dtype[i], name="f_in_" + seg_name[i])
        for i in range(5)]
    dt_raw = _mm(u, w_dt, name="f_in_dt")

    bias128 = _pad_cols(p["dt_bias"], DT_PAD)
    alog128 = _pad_cols(p["a_log"], DT_PAD)

    def dt_fn(dtr, bias, alog):
        dt = _softplus(dtr + bias)
        return dt, dt * (-jnp.exp(alog))

    dt128, da128 = _rows(dt_fn, [dt_raw], [bias128, alog128], [(DT_PAD, F32), (DT_PAD, F32)],
                         ts=512, name="f_dt")
    dtc = _group_cols(dt128[:, :SSD_H])
    dac = _group_cols(da128[:, :SSD_H])
    dar = dac.transpose(0, 2, 1)
    dsk = p["d_skip"].reshape(SSD_G, 1, SSD_R)

    conv_w, conv_b = p["conv_w"], p["conv_b"]
    pre, xbc = _conv_fwd(xbc_raw, conv_w, conv_b)
    xh = _heads(xbc[:, :SSD_INNER], SSD_H, SSD_P)
    y_h, st = _ssd_fwd_g(xh, dtc, dac, dar, dsk, xbc)
    y_core = _unheads(y_h)

    def group_norm_fwd(yv, zv, wn):
        y2 = yv * _silu(zv)
        gw = SSD_INNER // SSD_G
        outs = []
        for gi in range(SSD_G):
            seg = y2[:, gi * gw:(gi + 1) * gw]
            outs.append(_rms(seg, wn[:, gi * gw:(gi + 1) * gw]))
        return jnp.concatenate(outs, axis=1)

    y_ssd = _rows(group_norm_fwd, [y_core, z], [p["ssd_norm"]], [(SSD_INNER, CDT)], ts=256, name="f_ssd_post")[0]

    q_h = _heads(qkv[:, :d], SB_H, SB_DH)
    k_h = _heads(qkv[:, d:2 * d], SB_H, SB_DH)
    v_h = _heads(qkv[:, 2 * d:], SB_H, SB_DH)
    o_h, lt_h, *late = _sb_fwd(q_h, k_h, v_h, tuple(shards) if shards is not None else ())
    for n, gth in zip(LATE_W, late):
        wt[n] = _full_weight(n, gth)
    y_sb = _unheads(o_h).astype(CDT)

    mu = _rows(lambda mv, g: _rms(mv, g), [mem], [p["norm_mem"]], [(d, CDT)], ts=256, name="f_norm_mem")[0]
    kv = _mm(mu, wt["w_mem_kv"], out_dtype=CDT, name="f_mem_kv")
    y_mem = _mem_fwd(mq, kv)

    p_sb = _mm(y_sb, wt["w_sb_out"], name="f_sb_out")
    p_ssd = _mm(y_ssd, wt["w_ssd_out"], name="f_ssd_out")
    p_mem = _mm(y_mem, wt["w_mem_out"], name="f_mem_out")

    def merge_fn(glv, a, b, c):
        return (_sigmoid(glv[:, :d]) * a + _sigmoid(glv[:, d:2 * d]) * b + _sigmoid(glv[:, 2 * d:]) * c)

    merged = _rows(merge_fn, [gl, p_sb, p_ssd, p_mem], [], [(d, CDT)], ts=256, name="f_merge")[0]
    mix = _mm(merged, wt["w_o"], name="f_w_o")

    def mid_fn(xv, mixv, g_post, g_pre):
        h1 = xv + _rms(mixv, g_post)
        return h1, _rms(h1, g_pre)

    h1, u2 = _rows(mid_fn, [x, mix], [p["norm_mix_post"], p["norm_mlp_pre"]], [(d, F32), (d, CDT)],
                   ts=512, name="f_mid")
    a1 = _mm(u2, wt["w_up"], name="f_up")
    act = _rows(lambda a: jnp.square(jnp.maximum(a, 0.0)), [a1], [], [(a1.shape[1], CDT)], ts=256, name="f_act")[0]
    ff = _mm(act, wt["w_down"], name="f_down")

    def loss_fn(h1v, ffv, tv, g):
        diff = h1v + _rms(ffv, g) - tv
        tot = jnp.sum(jnp.sum(diff * diff, axis=1, keepdims=True), axis=0, keepdims=True)
        return diff * (1.0 / d), tot

    dh2, loss_acc = _rows(loss_fn, [h1, ff, tgt], [p["norm_mlp_post"]], [(d, F32)], [(1, 128)],
                          ts=512, name="f_loss")
    loss = loss_acc[:, :1] * (0.5 / d)

    sg = {}

    def b_post(ffv, dyv, g):
        dx, dg = _rms_bwd(ffv, g, dyv)
        return dx, dg

    d_ff, sg["norm_mlp_post"] = _rows(b_post, [ff, dh2], [p["norm_mlp_post"]], [(d, CDT)], [(1, d)],
                                      ts=512, name="b_norm_mlp_post")
    dact = _mm(d_ff, wt["w_down"], tb=True, name="b_down_x")
    gw = {"w_down": _mm(act, d_ff, ta=True, name="b_down_w")}
    da1 = _rows(lambda dv, a: dv * 2.0 * jnp.maximum(a, 0.0), [dact, a1], [], [(a1.shape[1], CDT)],
                ts=256, name="b_act")[0]
    du2 = _mm(da1, wt["w_up"], tb=True, name="b_up_x")
    gw["w_up"] = _mm(u2, da1, ta=True, name="b_up_w")

    def b_mid(h1v, du2v, dh2v, mixv, g_pre, g_post):
        dxa, dga = _rms_bwd(h1v, g_pre, du2v)
        dh1 = dh2v + dxa
        dmix, dgb = _rms_bwd(mixv, g_post, dh1)
        return dh1, dmix, dga, dgb

    dh1, dmix, sg["norm_mlp_pre"], sg["norm_mix_post"] = _rows(
        b_mid, [h1, du2, dh2, mix], [p["norm_mlp_pre"], p["norm_mix_post"]],
        [(d, F32), (d, CDT)], [(1, d), (1, d)], ts=256, name="b_mid")
    dmerged = _mm(dmix, wt["w_o"], tb=True, name="b_w_o_x")
    gw["w_o"] = _mm(merged, dmix, ta=True, name="b_w_o_w")

    def b_merge(dm, glv, a, b, c):
        outs, dgl = [], []
        for i, br in enumerate((a, b, c)):
            gt = _sigmoid(glv[:, i * d:(i + 1) * d])
            outs.append(gt * dm)
            dgl.append(dm * br * gt * (1.0 - gt))
        return outs[0], outs[1], outs[2], jnp.concatenate(dgl, axis=1)

    dp_sb, dp_ssd, dp_mem, dgl = _rows(b_merge, [dmerged, gl, p_sb, p_ssd, p_mem], [],
                                       [(d, CDT), (d, CDT), (d, CDT), (3 * d, CDT)], ts=256, name="b_merge")
    dy_sb = _mm(dp_sb, wt["w_sb_out"], tb=True, name="b_sb_out_x")
    gw["w_sb_out"] = _mm(y_sb, dp_sb, ta=True, name="b_sb_out_w")
    dy_ssd = _mm(dp_ssd, wt["w_ssd_out"], tb=True, name="b_ssd_out_x")
    gw["w_ssd_out"] = _mm(y_ssd, dp_ssd, ta=True, name="b_ssd_out_w")
    dy_mem = _mm(dp_mem, wt["w_mem_out"], tb=True, out_dtype=CDT, name="b_mem_out_x")
    gw["w_mem_out"] = _mm(y_mem, dp_mem, ta=True, name="b_mem_out_w")

    dmq, dk_m, dv_m = _mem_bwd(mq, kv, dy_mem)
    dkv = jnp.concatenate([dk_m, dv_m], axis=1).astype(CDT)
    gw["w_mem_kv"] = _mm(mu, dkv, ta=True, name="b_mem_kv_w")
    dmu = _mm(dkv, wt["w_mem_kv"], tb=True, name="b_mem_kv_x")
    sg["norm_mem"] = _rows(lambda mv, dv, g: _rms_bwd(mv, g, dv)[1], [mem, dmu], [p["norm_mem"]], [], [(1, d)],
                           ts=256, name="b_norm_mem")[0]

    payloads = tuple(_grad_payload(n, gw[n]) for n in LATE_W) if shards is not None else ()
    dq_h, dk_h, dv_h, *received = _sb_bwd(q_h, k_h, v_h, lt_h, _heads(dy_sb, SB_H, SB_DH), payloads)
    dqkv = jnp.concatenate([_unheads(dq_h), _unheads(dk_h), _unheads(dv_h)], axis=1).astype(CDT)

    def group_norm_bwd(dyo, yv, zv, wn):
        sz = _silu(zv)
        y2 = yv * sz
        gw_ = SSD_INNER // SSD_G
        dy2, dwn = [], []
        for gi in range(SSD_G):
            sl = slice(gi * gw_, (gi + 1) * gw_)
            dxs, dgs = _rms_bwd(y2[:, sl], wn[:, sl], dyo[:, sl])
            dy2.append(dxs)
            dwn.append(dgs)
        dy2 = jnp.concatenate(dy2, axis=1)
        return dy2 * sz, dy2 * yv * _dsilu(zv), jnp.concatenate(dwn, axis=1)

    dy_core, dz, sg["ssd_norm"] = _rows(group_norm_bwd, [dy_ssd, y_core, z], [p["ssd_norm"]],
                                        [(SSD_INNER, F32), (SSD_INNER, CDT)], [(1, SSD_INNER)],
                                        ts=256, name="b_ssd_post")
    dxh, dda, dxx, d_b, d_c, dah, ddsk = _ssd_bwd_g(xh, dtc, dac, dar, dsk, xbc, st, _heads(dy_core, SSD_H, SSD_P))
    sg["d_skip"] = ddsk.reshape(1, SSD_H)
    sg["a_log"] = dah.reshape(1, SSD_H) * (-jnp.exp(p["a_log"]))

    def b_dt(ddav, dxxv, dtr, bias, alog):
        ddt = ddav * (-jnp.exp(alog)) + dxxv
        draw = ddt * _sigmoid(dtr + bias)
        return draw, jnp.sum(draw, axis=0, keepdims=True)

    ungroup = lambda t: _pad_cols(t.transpose(1, 0, 2).reshape(s, SSD_H), DT_PAD)
    ddt_raw, dbias128 = _rows(b_dt, [ungroup(dda), ungroup(dxx), dt_raw], [bias128, alog128],
                              [(DT_PAD, CDT)], [(1, DT_PAD)], ts=512, name="b_dt")
    sg["dt_bias"] = dbias128[:, :SSD_H]

    dxbc = jnp.concatenate([_unheads(dxh), d_b, d_c], axis=1)
    dxbc_raw, sg["conv_w"], sg["conv_b"] = _conv_bwd(xbc_raw, pre, dxbc, conv_w)

    dseg = [dqkv, dz, dxbc_raw, dmq, dgl]
    dus = [_mm(dseg[i], w_main, tb=True, b_koff=seg_off[i], name="b_in_x_" + seg_name[i])
           for i in range(5)]
    dus.append(_mm(ddt_raw, w_dt, tb=True, name="b_in_x_dt"))
    dw_main = lax.empty((d, w_main.shape[1]), F32)
    for i in range(5):
        dw_main = _mm(u, dseg[i], ta=True, into=dw_main, into_off=seg_off[i], name="b_in_w_" + seg_name[i])
    dw_dt = _mm(u, ddt_raw, ta=True, name="b_in_w_dt")
    gw["w_in"] = jnp.concatenate([dw_main[:, :c3], dw_dt[:, :SSD_H], dw_main[:, c3:]], axis=1)

    def b_pre(xv, dh1v, d0, d1, d2, d3, d4, d5, g):
        dx, dg = _rms_bwd(xv, g, d0 + d1 + d2 + d3 + d4 + d5)
        return dh1v + dx, dg

    grad_x, sg["norm_mix_pre"] = _rows(b_pre, [x, dh1] + dus, [p["norm_mix_pre"]], [(d, F32)], [(1, d)],
                                       ts=256, name="b_norm_pre")
    return loss, grad_x, gw, sg, (received if shards is not None else None)


HBM = pl.BlockSpec(memory_space=pltpu.HBM)
MESH = pl.DeviceIdType.MESH


def _me_and_peers():
    x, y, c = lax.axis_index("x"), lax.axis_index("y"), lax.axis_index("c")
    me = 4 * x + 2 * y + c
    peers = [(x, y, 1 - c), (1 - x, y, c), (x, 1 - y, c), (1 - x, 1 - y, c),
             (1 - x, y, 1 - c), (x, 1 - y, 1 - c), (1 - x, 1 - y, 1 - c)]
    return me, peers


def _peer_index(peer):
    return 4 * peer[0] + 2 * peer[1] + peer[2]


def _exchange_copies(ins, outs, send_sems, recv_sems, local_sems, scatter):
    me, peers = _me_and_peers()
    copies = []
    for a in range(len(ins)):
        own = ins[a].at[me] if scatter else ins[a]
        copies.append(pltpu.make_async_copy(own, outs[a].at[me], local_sems.at[a]))
        for kk, peer in enumerate(peers):
            src = ins[a].at[_peer_index(peer)] if scatter else ins[a]
            copies.append(pltpu.make_async_remote_copy(
                src_ref=src, dst_ref=outs[a].at[me],
                send_sem=send_sems.at[a, kk], recv_sem=recv_sems.at[a, kk],
                device_id=peer, device_id_type=MESH))
    return copies


def _exchange_shapes(ins, scatter):
    return [jax.ShapeDtypeStruct(t.shape if scatter else (N_DEV,) + t.shape, t.dtype) for t in ins]


def _exchange_sems(n):
    return [pltpu.SemaphoreType.DMA((n, N_DEV - 1)), pltpu.SemaphoreType.DMA((n, N_DEV - 1)),
            pltpu.SemaphoreType.DMA((n,))]


def _exchange(ins, scatter, name):
    n = len(ins)

    def body(*refs):
        copies = _exchange_copies(refs[:n], refs[n:2 * n], *refs[2 * n:], scatter)
        for cp in copies:
            cp.start()
        for cp in copies:
            cp.wait()

    return pl.pallas_call(
        body, name=name,
        in_specs=[HBM] * n, out_specs=[HBM] * n,
        out_shape=_exchange_shapes(ins, scatter),
        scratch_shapes=_exchange_sems(n),
        compiler_params=pltpu.CompilerParams(has_side_effects=True),
    )(*ins)


def _gather_two_level(shards, name):
    n = len(shards)

    def body(*refs):
        ins, outs = refs[:n], refs[n:2 * n]
        send_sems, recv_sems, local_sems = refs[2 * n:]
        x, y, c = lax.axis_index("x"), lax.axis_index("y"), lax.axis_index("c")
        me, sib = (x, y, c), (x, y, 1 - c)
        chips = [(1 - x, y), (x, 1 - y), (1 - x, 1 - y)]

        def copy(a, k, block, to, src=None):
            slot = outs[a].at[_peer_index(block)]
            return pltpu.make_async_remote_copy(
                src_ref=slot if src is None else src, dst_ref=slot,
                send_sem=send_sems.at[a, k], recv_sem=recv_sems.at[a, k], device_id=to, device_id_type=MESH)

        own = [pltpu.make_async_copy(ins[a], outs[a].at[_peer_index(me)], local_sems.at[a]) for a in range(n)]
        first = []
        for a in range(n):
            first.append(copy(a, 0, me, sib, src=ins[a]))
            first += [copy(a, 1 + j, me, (*chip, c), src=ins[a]) for j, chip in enumerate(chips)]
        for cp in own + first:
            cp.start()
        passed = []
        for j, chip in enumerate(chips):
            for a in range(n):
                copy(a, 1 + j, (*chip, c), me).wait_recv()
                fwd = copy(a, 4 + j, (*chip, c), sib)
                fwd.start()
                passed.append(fwd)
        for a in range(n):
            copy(a, 0, sib, me).wait_recv()
            for j, chip in enumerate(chips):
                copy(a, 4 + j, (*chip, 1 - c), me).wait_recv()
        for cp in first + passed:
            cp.wait_send()
        for cp in own:
            cp.wait()

    return pl.pallas_call(
        body, name=name,
        in_specs=[HBM] * n, out_specs=[HBM] * n,
        out_shape=_exchange_shapes(shards, False),
        scratch_shapes=_exchange_sems(n),
        compiler_params=pltpu.CompilerParams(has_side_effects=True),
    )(*shards)


def _pair_exchange(t, name):
    def body(t_ref, o_ref, send_sem, recv_sem):
        c = lax.axis_index("c")
        sib = (lax.axis_index("x"), lax.axis_index("y"), 1 - c)
        cp = pltpu.make_async_remote_copy(src_ref=t_ref.at[1 - c], dst_ref=o_ref, send_sem=send_sem,
                                          recv_sem=recv_sem, device_id=sib, device_id_type=MESH)
        cp.start()
        cp.wait()

    return pl.pallas_call(
        body, name=name, in_specs=[HBM], out_specs=HBM,
        out_shape=jax.ShapeDtypeStruct(t.shape[1:], t.dtype),
        scratch_shapes=[pltpu.SemaphoreType.DMA, pltpu.SemaphoreType.DMA],
        compiler_params=pltpu.CompilerParams(has_side_effects=True),
    )(t)


N_CHIP = 4


def _chip_scatter(t, name):
    def body(t_ref, o_ref, send_sems, recv_sems, local_sem):
        x, y, c = lax.axis_index("x"), lax.axis_index("y"), lax.axis_index("c")
        mine = 2 * x + y
        copies = [pltpu.make_async_copy(t_ref.at[mine], o_ref.at[mine], local_sem)]
        for j, (px, py) in enumerate([(1 - x, y), (x, 1 - y), (1 - x, 1 - y)]):
            copies.append(pltpu.make_async_remote_copy(
                src_ref=t_ref.at[2 * px + py], dst_ref=o_ref.at[mine],
                send_sem=send_sems.at[j], recv_sem=recv_sems.at[j], device_id=(px, py, c), device_id_type=MESH))
        for cp in copies:
            cp.start()
        for cp in copies:
            cp.wait()

    return pl.pallas_call(
        body, name=name, in_specs=[HBM], out_specs=HBM,
        out_shape=jax.ShapeDtypeStruct(t.shape, t.dtype),
        scratch_shapes=[pltpu.SemaphoreType.DMA((N_CHIP - 1,)), pltpu.SemaphoreType.DMA((N_CHIP - 1,)),
                        pltpu.SemaphoreType.DMA],
        compiler_params=pltpu.CompilerParams(has_side_effects=True),
    )(t)


def _pair_sum(a, b, name):
    _, q, r, c = a.shape
    tr = _pick(r, (256, 128))
    spec = pl.BlockSpec((None, tr, c), lambda i, j, core: (i, j, 0))

    def body(core_ref, a_ref, b_ref, o_ref):
        o_ref[...] = (a_ref[...] + b_ref[...]).astype(o_ref.dtype)

    return pl.pallas_call(
        body, name=name,
        grid_spec=pltpu.PrefetchScalarGridSpec(
            num_scalar_prefetch=1, grid=(q, r // tr),
            in_specs=[pl.BlockSpec((None, None, tr, c), lambda i, j, core: (core[0], i, j, 0)), spec],
            out_specs=spec),
        out_shape=jax.ShapeDtypeStruct((q, r, c), CDT),
        compiler_params=_cparams(("parallel", "parallel")),
    )(lax.axis_index("c").astype(jnp.int32).reshape(1), a, b)


def _all_reduce_small(v, name):
    r, c = v.shape

    def body(v_ref, o_ref, buf, send_sems, recv_sems):
        me, peers = _me_and_peers()
        buf[me] = v_ref[...]
        copies = []
        for kk, peer in enumerate(peers):
            cp = pltpu.make_async_remote_copy(
                src_ref=v_ref, dst_ref=buf.at[me],
                send_sem=send_sems.at[kk], recv_sem=recv_sems.at[kk],
                device_id=peer, device_id_type=MESH)
            cp.start()
            copies.append(cp)
        for cp in copies:
            cp.wait()
        acc = buf[0]
        for i in range(1, N_DEV):
            acc = acc + buf[i]
        o_ref[...] = acc

    return pl.pallas_call(
        body, name=name,
        in_specs=[pl.BlockSpec(memory_space=pltpu.VMEM)],
        out_specs=pl.BlockSpec(memory_space=pltpu.VMEM),
        out_shape=jax.ShapeDtypeStruct((r, c), F32),
        scratch_shapes=[pltpu.VMEM((N_DEV, r, c), F32),
                        pltpu.SemaphoreType.DMA((N_DEV - 1,)), pltpu.SemaphoreType.DMA((N_DEV - 1,))],
        compiler_params=pltpu.CompilerParams(has_side_effects=True),
    )(v)


def _adamw_math(g, w, m, v):
    m2 = ADAM_B1 * m + (1.0 - ADAM_B1) * g
    v2 = ADAM_B2 * v + (1.0 - ADAM_B2) * jnp.square(g)
    m_hat = m2 / (1.0 - ADAM_B1 ** ADAM_STEP)
    v_hat = v2 / (1.0 - ADAM_B2 ** ADAM_STEP)
    delta = -ADAM_LR * (m_hat / (jnp.sqrt(v_hat) + ADAM_EPS) + ADAM_WD * w)
    return delta, m2, v2


def _adamw_reduce(parts, w, m, v, name):
    r, c = w.shape
    nparts = parts.shape[0]
    tr = _pick(r, (128, 64, 32, 16, 8))

    def body(p_ref, w_ref, m_ref, v_ref, g_ref, d_ref, m2_ref, v2_ref):
        g = p_ref[0].astype(F32)
        for i in range(1, nparts):
            g = g + p_ref[i].astype(F32)
        delta, m2, v2 = _adamw_math(g, w_ref[...], m_ref[...], v_ref[...])
        g_ref[...] = g
        d_ref[...] = delta
        m2_ref[...] = m2
        v2_ref[...] = v2

    tile = pl.BlockSpec((tr, c), lambda i: (i, 0))
    shp = jax.ShapeDtypeStruct((r, c), F32)
    return pl.pallas_call(
        body, name=name, grid=(r // tr,),
        in_specs=[pl.BlockSpec((nparts, tr, c), lambda i: (0, i, 0)), tile, tile, tile],
        out_specs=[tile] * 4, out_shape=[shp] * 4,
        compiler_params=_cparams(("parallel",)),
    )(parts, w, m, v)


def _adamw_plain(g, w, m, v, name):
    def body(g_ref, w_ref, m_ref, v_ref, d_ref, m2_ref, v2_ref):
        delta, m2, v2 = _adamw_math(g_ref[...], w_ref[...], m_ref[...], v_ref[...])
        d_ref[...] = delta
        m2_ref[...] = m2
        v2_ref[...] = v2

    spec = pl.BlockSpec(memory_space=pltpu.VMEM)
    shp = jax.ShapeDtypeStruct(g.shape, F32)
    return pl.pallas_call(
        body, name=name, in_specs=[spec] * 4, out_specs=[spec] * 3, out_shape=[shp] * 3,
    )(g, w, m, v)


SMALL_ROWS, SMALL_COLS = 16, 3072


def _small_step(sg, gcw, loss, ws, ms, vs):
    ns = len(sg)
    widths = [t.shape[1] for t in sg]
    kk_, ch = gcw.shape[1], gcw.shape[2]
    assert ns < SMALL_ROWS and max(widths) <= SMALL_COLS

    def reduce_body(*refs):
        g_refs = refs[:ns]
        gcw_ref, loss_ref, tot_ref, totc_ref = refs[ns:ns + 4]
        mine, buf, minec, bufc, send_sems, recv_sems = refs[ns + 4:]
        me, peers = _me_and_peers()

        mine[...] = jnp.zeros_like(mine)
        for i in range(ns):
            mine[i:i + 1, 0:widths[i]] = g_refs[i][...]
        mine[ns:ns + 1, 0:LANES] = jnp.broadcast_to(loss_ref[...], (1, LANES))
        minec[...] = gcw_ref[...]
        buf[me] = mine[...]
        bufc[me] = minec[...]
        copies = []
        for j, peer in enumerate(peers):
            copies.append(pltpu.make_async_remote_copy(
                src_ref=mine, dst_ref=buf.at[me], send_sem=send_sems.at[0, j], recv_sem=recv_sems.at[0, j],
                device_id=peer, device_id_type=MESH))
            copies.append(pltpu.make_async_remote_copy(
                src_ref=minec, dst_ref=bufc.at[me], send_sem=send_sems.at[1, j], recv_sem=recv_sems.at[1, j],
                device_id=peer, device_id_type=MESH))
        for cp in copies:
            cp.start()
        for cp in copies:
            cp.wait()
        tot = buf[0]
        totc = bufc[0]
        for i in range(1, N_DEV):
            tot = tot + buf[i]
            totc = totc + bufc[i]
        tot_ref[...] = tot
        totc_ref[...] = totc

    vm = pl.BlockSpec(memory_space=pltpu.VMEM)
    tot, totc = pl.pallas_call(
        reduce_body, name="small_reduce",
        in_specs=[vm] * (ns + 2), out_specs=[vm, vm],
        out_shape=[jax.ShapeDtypeStruct((SMALL_ROWS, SMALL_COLS), F32), jax.ShapeDtypeStruct((N_DEV, kk_, ch), F32)],
        scratch_shapes=[pltpu.VMEM((SMALL_ROWS, SMALL_COLS), F32), pltpu.VMEM((N_DEV, SMALL_ROWS, SMALL_COLS), F32),
                        pltpu.VMEM((N_DEV, kk_, ch), F32), pltpu.VMEM((N_DEV, N_DEV, kk_, ch), F32),
                        pltpu.SemaphoreType.DMA((2, N_DEV - 1)), pltpu.SemaphoreType.DMA((2, N_DEV - 1))],
        compiler_params=pltpu.CompilerParams(has_side_effects=True),
    )(*sg, gcw, loss)

    def adamw_body(*refs):
        tot_ref, totc_ref = refs[:2]
        w_refs, m_refs, v_refs = (refs[2 + i * (ns + 1):2 + (i + 1) * (ns + 1)] for i in range(3))
        outs = refs[3 * ns + 5:]
        loss_out = outs[0]
        go, do_, mo, vo = (outs[1 + i * (ns + 1):1 + (i + 1) * (ns + 1)] for i in range(4))
        me, _ = _me_and_peers()
        loss_out[...] = tot_ref[ns:ns + 1, 0:1]
        for i in range(ns + 1):
            g = tot_ref[i:i + 1, 0:widths[i]] if i < ns else totc_ref[me]
            delta, m2, v2 = _adamw_math(g, w_refs[i][...], m_refs[i][...], v_refs[i][...])
            go[i][...] = g
            do_[i][...] = delta
            mo[i][...] = m2
            vo[i][...] = v2

    shapes = [jax.ShapeDtypeStruct(t.shape, F32) for t in ws]
    res = pl.pallas_call(
        adamw_body, name="small_adamw",
        in_specs=[vm] * (3 * ns + 5), out_specs=[vm] * (4 * ns + 5),
        out_shape=[jax.ShapeDtypeStruct((1, 1), F32)] + shapes * 4,
    )(tot, totc, *ws, *ms, *vs)
    n1 = ns + 1
    return res[0], res[1:1 + n1], res[1 + n1:1 + 2 * n1], res[1 + 2 * n1:1 + 3 * n1], res[1 + 3 * n1:]


def _cast_shard(w, name):
    r = w.shape[0]
    return _rows(lambda t: t, [w], [], [(w.shape[1], CDT)], ts=_pick(r, (256, 128)), name=name)[0]


BIG = ["w_in", "w_mem_kv", "w_up", "w_sb_out", "w_ssd_out", "w_mem_out", "w_o", "w_down"]
LATE_W = BIG[1:]
COL_SHARDED = ("w_in", "w_mem_kv", "w_up")
SMALL = ["norm_mix_pre", "conv_b", "dt_bias", "a_log", "d_skip", "ssd_norm", "norm_mem",
         "norm_mix_post", "norm_mlp_pre", "norm_mlp_post"]
ALL_W = ["norm_mix_pre", "w_in", "conv_w", "conv_b", "dt_bias", "a_log", "d_skip", "ssd_norm", "norm_mem",
         "w_mem_kv", "w_sb_out", "w_ssd_out", "w_mem_out", "w_o", "norm_mix_post", "norm_mlp_pre", "w_up",
         "w_down", "norm_mlp_post"]
LANES = 128


def _pack_rows(vecs):
    parts, offs, off = [], [], 0
    for t in vecs:
        flat = t.reshape(-1)
        n = flat.shape[0]
        rows = -(-n // (8 * LANES)) * 8
        parts.append(jnp.pad(flat, (0, rows * LANES - n)).reshape(rows, LANES))
        offs.append((off, n))
        off += rows
    return jnp.concatenate(parts, axis=0), offs


def _unpack_rows(packed, offs, shapes):
    out = []
    for (off, n), shape in zip(offs, shapes):
        rows = -(-n // (8 * LANES)) * 8
        out.append(packed[off:off + rows].reshape(-1)[:n].reshape(shape))
    return out


def kernel(x, mem, norm_mix_pre, w_in, conv_w, conv_b, dt_bias, a_log, d_skip, ssd_norm, norm_mem, w_mem_kv, w_sb_out, w_ssd_out, w_mem_out, w_o, norm_mix_post, norm_mlp_pre, w_up, w_down, norm_mlp_post, loss_target, m_norm_mix_pre, m_w_in, m_conv_w, m_conv_b, m_dt_bias, m_a_log, m_d_skip, m_ssd_norm, m_norm_mem, m_w_mem_kv, m_w_sb_out, m_w_ssd_out, m_w_mem_out, m_w_o, m_norm_mix_post, m_norm_mlp_pre, m_w_up, m_w_down, m_norm_mlp_post, v_norm_mix_pre, v_w_in, v_conv_w, v_conv_b, v_dt_bias, v_a_log, v_d_skip, v_ssd_norm, v_norm_mem, v_w_mem_kv, v_w_sb_out, v_w_ssd_out, v_w_mem_out, v_w_o, v_norm_mix_post, v_norm_mlp_pre, v_w_up, v_w_down, v_norm_mlp_post):
    wd = dict(norm_mix_pre=norm_mix_pre, w_in=w_in, conv_w=conv_w, conv_b=conv_b, dt_bias=dt_bias, a_log=a_log,
              d_skip=d_skip, ssd_norm=ssd_norm, norm_mem=norm_mem, w_mem_kv=w_mem_kv, w_sb_out=w_sb_out,
              w_ssd_out=w_ssd_out, w_mem_out=w_mem_out, w_o=w_o, norm_mix_post=norm_mix_post,
              norm_mlp_pre=norm_mlp_pre, w_up=w_up, w_down=w_down, norm_mlp_post=norm_mlp_post)
    md = dict(norm_mix_pre=m_norm_mix_pre, w_in=m_w_in, conv_w=m_conv_w, conv_b=m_conv_b, dt_bias=m_dt_bias,
              a_log=m_a_log, d_skip=m_d_skip, ssd_norm=m_ssd_norm, norm_mem=m_norm_mem, w_mem_kv=m_w_mem_kv,
              w_sb_out=m_w_sb_out, w_ssd_out=m_w_ssd_out, w_mem_out=m_w_mem_out, w_o=m_w_o,
              norm_mix_post=m_norm_mix_post, norm_mlp_pre=m_norm_mlp_pre, w_up=m_w_up, w_down=m_w_down,
              norm_mlp_post=m_norm_mlp_post)
    vd = dict(norm_mix_pre=v_norm_mix_pre, w_in=v_w_in, conv_w=v_conv_w, conv_b=v_conv_b, dt_bias=v_dt_bias,
              a_log=v_a_log, d_skip=v_d_skip, ssd_norm=v_ssd_norm, norm_mem=v_norm_mem, w_mem_kv=v_w_mem_kv,
              w_sb_out=v_w_sb_out, w_ssd_out=v_w_ssd_out, w_mem_out=v_w_mem_out, w_o=v_w_o,
              norm_mix_post=v_norm_mix_post, norm_mlp_pre=v_norm_mlp_pre, w_up=v_w_up, w_down=v_w_down,
              norm_mlp_post=v_norm_mlp_post)
    shards = {n: _cast_shard(wd[n][0], "cast_" + n) for n in BIG}
    w_in_g, conv_w_g = _gather_two_level([shards["w_in"], wd["conv_w"][0]], "gather_w_in")
    wt = {"w_in": _full_weight("w_in", w_in_g)}
    ch = conv_w_g.shape[2]

    p = {n: wd[n] for n in SMALL}
    p["conv_w"] = conv_w_g.transpose(1, 0, 2).reshape(CONV_K, N_DEV * ch)
    loss, grad_x, gw, sg, late_received = _local_step(x[0], mem[0], loss_target[0], p, wt,
                                                      [shards[n] for n in LATE_W])

    received = dict(zip(LATE_W, late_received))
    g_in = gw["w_in"]
    by_core = g_in.reshape(g_in.shape[0], N_CHIP, 2, g_in.shape[1] // N_DEV).transpose(2, 1, 0, 3)
    pair = _pair_sum(by_core, _pair_exchange(by_core, "scatter_w_in_pair"), "sum_w_in_pair")
    received["w_in"] = _chip_scatter(pair, "scatter_w_in_chips")

    grads, deltas, new_m, new_v = {}, {}, {}, {}
    for n in BIG:
        g, dl, m2, v2 = _adamw_reduce(received[n], wd[n][0], md[n][0], vd[n][0], "adamw_" + n)
        grads[n], deltas[n], new_m[n], new_v[n] = g[None], dl[None], m2[None], v2[None]
    small_names = SMALL + ["conv_w"]
    gcw = sg["conv_w"].reshape(CONV_K, N_DEV, ch).transpose(1, 0, 2)
    small_of = lambda dct: [dct[n] for n in SMALL] + [dct["conv_w"][0]]
    loss_red, g_s, d_s, m_s, v_s = _small_step([sg[n] for n in SMALL], gcw, loss, small_of(wd), small_of(md),
                                               small_of(vd))
    for i, n in enumerate(small_names):
        shape = wd[n].shape
        grads[n], deltas[n], new_m[n], new_v[n] = (t.reshape(shape) for t in (g_s[i], d_s[i], m_s[i], v_s[i]))
    loss_out = loss_red.reshape(())

    return (loss_out, grad_x[None], *[grads[n] for n in ALL_W], *[deltas[n] for n in ALL_W],
            *[new_m[n] for n in ALL_W], *[new_v[n] for n in ALL_W])
```

```python
import functools

import jax
import jax.numpy as jnp
from jax import lax
from jax.experimental import pallas as pl
from jax.experimental.pallas import tpu as pltpu

F32 = jnp.float32
BF16 = jnp.bfloat16
CDT = jnp.bfloat16
EPS = 1e-6
VMEM_LIMIT = 56 * 1024 * 1024

N_DEV = 8
D_MODEL = 1024
SB_H, SB_DH = 16, 64
SSD_G, SSD_R, SSD_P, SSD_N, SSD_L = 4, 8, 64, 128, 128
SSD_H = SSD_G * SSD_R
SSD_INNER = SSD_H * SSD_P
CONV_K = 4
CONV_DIM = SSD_INNER + 2 * SSD_G * SSD_N
MEM_H, MEM_DH = 4, 256
DT_PAD = 128
SB_TQ, SB_BK = 1024, 256
CONV_PAD = 8
MM_TILE, MM_TILE_K = 1024, 2048

ADAM_LR, ADAM_B1, ADAM_B2, ADAM_EPS, ADAM_WD, ADAM_STEP = 0.001, 0.9, 0.999, 1e-08, 0.01, 10

NT = (((1,), (1,)), ((), ()))
TN = (((0,), (0,)), ((), ()))
NN = (((1,), (0,)), ((), ()))


def _cparams(sem=None):
    return pltpu.CompilerParams(dimension_semantics=sem, vmem_limit_bytes=VMEM_LIMIT)


def _pick(n, cands):
    for c in cands:
        if n % c == 0:
            return c
    return n


def _dot(a, b, dims=NN):
    return lax.dot_general(a.astype(CDT), b.astype(CDT), dims, preferred_element_type=F32)


def _split_dot(x, t, left, pieces):
    if CDT == F32:
        return lax.dot_general(t, x, NN, preferred_element_type=F32) if left else \
            lax.dot_general(x, t, NN, preferred_element_type=F32)
    acc = None
    rem = x
    for _ in range(pieces):
        hi = rem.astype(BF16)
        rem = rem - hi.astype(F32)
        d = lax.dot_general(t, hi, NN, preferred_element_type=F32) if left else \
            lax.dot_general(hi, t, NN, preferred_element_type=F32)
        acc = d if acc is None else acc + d
    return acc


def _iota(shape, dim):
    return lax.broadcasted_iota(jnp.int32, shape, dim)


def _sigmoid(x):
    return 1.0 / (1.0 + jnp.exp(-x))


def _silu(x):
    return x * _sigmoid(x)


def _dsilu(x):
    s = _sigmoid(x)
    return s * (1.0 + x * (1.0 - s))


def _softplus(x):
    return jnp.maximum(x, 0.0) + jnp.log(1.0 + jnp.exp(-jnp.abs(x)))


def _rms(x, g):
    r = lax.rsqrt(jnp.mean(x * x, axis=-1, keepdims=True) + EPS)
    return x * r * g


def _rms_bwd(x, g, dy):
    r = lax.rsqrt(jnp.mean(x * x, axis=-1, keepdims=True) + EPS)
    n = x * r
    dn = dy * g
    dx = r * (dn - n * jnp.mean(dn * n, axis=-1, keepdims=True))
    dg = jnp.sum(dy * n, axis=0, keepdims=True)
    return dx, dg


def _mm(a, b, *, ta=False, tb=False, out_dtype=F32, name, b_off=0, n=None, b_koff=0, into=None, into_off=0,
        hosted=None):
    m = a.shape[1] if ta else a.shape[0]
    k = a.shape[0] if ta else a.shape[1]
    if n is None:
        n = b.shape[0] if tb else b.shape[1]
    assert b_koff + k <= (b.shape[1] if tb else b.shape[0])
    bm = _pick(m, (MM_TILE, 512, 256, 128))
    bn = _pick(n, (MM_TILE, 512, 256, 128))
    bk = next(c for c in (MM_TILE_K, 1024, 512, 256, 128, k) if k % c == 0 and b_koff % c == 0)
    nk = k // bk
    assert b_off % bn == 0 and into_off % bn == 0
    jb, jo, kb = b_off // bn, into_off // bn, b_koff // bk
    dims = (((0 if ta else 1,), (1 if tb else 0,)), ((), ()))
    grid = (m // bm, n // bn, nk)
    off = 1 if into is not None else 0
    nh_in = len(hosted.ins) if hosted else 0
    nh_out = len(hosted.shapes) if hosted else 0

    def body(a_ref, b_ref, *rest):
        o_ref = rest[off + nh_in]
        acc_ref = rest[off + nh_in + 1 + nh_out]
        if hosted:
            h_refs = (rest[off:off + nh_in], rest[off + nh_in + 1:off + nh_in + 1 + nh_out],
                      rest[off + nh_in + nh_out + 2:])
            step = (pl.program_id(0) * grid[1] + pl.program_id(1)) * grid[2] + pl.program_id(2)

            @pl.when(step == 0)
            def _():
                for cp in hosted.copies(*h_refs):
                    cp.start()

        part = _dot(a_ref[...], b_ref[...], dims)
        if nk == 1:
            o_ref[...] = part.astype(o_ref.dtype)
        else:
            kk = pl.program_id(2)

            @pl.when(kk == 0)
            def _():
                acc_ref[...] = part

            @pl.when(jnp.logical_and(kk > 0, kk < nk - 1))
            def _():
                acc_ref[...] += part

            @pl.when(kk == nk - 1)
            def _():
                o_ref[...] = (acc_ref[...] + part).astype(o_ref.dtype)

        if hosted:
            @pl.when(step == grid[0] * grid[1] * grid[2] - 1)
            def _():
                for cp in hosted.copies(*h_refs):
                    cp.wait()

    a_spec = pl.BlockSpec((bk, bm), lambda i, j, kk: (kk, i)) if ta else \
        pl.BlockSpec((bm, bk), lambda i, j, kk: (i, kk))
    b_spec = pl.BlockSpec((bn, bk), lambda i, j, kk: (j + jb, kk + kb)) if tb else \
        pl.BlockSpec((bk, bn), lambda i, j, kk: (kk + kb, j + jb))
    extra = {} if into is None else {"input_output_aliases": {2: 0}}
    out_shape = jax.ShapeDtypeStruct((m, n), out_dtype) if into is None else \
        jax.ShapeDtypeStruct(into.shape, into.dtype)
    res = pl.pallas_call(
        body, name=name, grid=grid,
        in_specs=[a_spec, b_spec] + ([] if into is None else [pl.BlockSpec(memory_space=pl.ANY)]) + [HBM] * nh_in,
        out_specs=[pl.BlockSpec((bm, bn), lambda i, j, kk: (i, j + jo))] + [HBM] * nh_out,
        out_shape=[out_shape] + (list(hosted.shapes) if hosted else []),
        scratch_shapes=[pltpu.VMEM((bm, bn) if nk > 1 else (8, 128), F32)] + (list(hosted.sems) if hosted else []),
        compiler_params=_cparams(("arbitrary",) * 3 if hosted else ("parallel", "parallel", "arbitrary")),
        **extra,
    )(*((a, b) if into is None else (a, b, into)), *(hosted.ins if hosted else ()))
    return res if hosted else res[0]


def _rows(fn, tiled, params, outs, accs=(), *, ts, name):
    s = tiled[0].shape[0]
    ts = min(ts, s)
    assert s % ts == 0
    nt, npar, no, na = len(tiled), len(params), len(outs), len(accs)

    def body(*refs):
        i = pl.program_id(0)
        vals = [r[...] for r in refs[:nt + npar]]
        res = fn(*vals)
        if not isinstance(res, (tuple, list)):
            res = (res,)
        orefs = refs[nt + npar:nt + npar + no]
        arefs = refs[nt + npar + no:]
        for r_, val in zip(orefs, res[:no]):
            r_[...] = val.astype(r_.dtype)
        if na:
            @pl.when(i == 0)
            def _():
                for r_ in arefs:
                    r_[...] = jnp.zeros_like(r_)

            for r_, val in zip(arefs, res[no:]):
                r_[...] += jnp.broadcast_to(val, r_.shape)

    in_specs = [pl.BlockSpec((ts, a.shape[1]), lambda i: (i, 0)) for a in tiled]
    in_specs += [pl.BlockSpec(p.shape, lambda i: (0, 0)) for p in params]
    out_specs = [pl.BlockSpec((ts, w), lambda i: (i, 0)) for (w, _) in outs]
    out_specs += [pl.BlockSpec(shape, lambda i: (0, 0)) for shape in accs]
    out_shape = [jax.ShapeDtypeStruct((s, w), dt) for (w, dt) in outs]
    out_shape += [jax.ShapeDtypeStruct(shape, F32) for shape in accs]
    res = pl.pallas_call(
        body, name=name, grid=(s // ts,),
        in_specs=in_specs, out_specs=out_specs, out_shape=out_shape,
        compiler_params=_cparams(("arbitrary",)),
    )(*tiled, *params)
    return res


def _sb_block(qs, kb, diag):
    tq, bk = qs.shape[0], kb.shape[0]
    z = _dot(qs, kb, NT)
    lb = jnp.minimum(z, 0.0) - jnp.log(1.0 + jnp.exp(-jnp.abs(z)))
    lk = lb - z
    if diag is None:
        return None, lb, lk
    causal = (diag + _iota((tq, bk), 1)) < _iota((tq, bk), 0)
    return causal, lb, jnp.where(causal, lk, 0.0)


def _fused_exchange(scatter, ncols, nsteps):
    def hooks(ins, outs, sems):
        step = pl.program_id(0) * ncols + pl.program_id(1)

        @pl.when(step == 0)
        def _():
            for cp in _exchange_copies(ins, outs, *sems, scatter):
                cp.start()

        def finish():
            @pl.when(step == nsteps - 1)
            def _():
                for cp in _exchange_copies(ins, outs, *sems, scatter):
                    cp.wait()
        return finish
    return hooks


def _sb_fwd(q, k, v, comm=()):
    h, s, dh = q.shape
    tq, bk = min(SB_TQ, s), min(SB_BK, s)
    scale = dh ** -0.5
    nc = len(comm)
    hooks = _fused_exchange(False, s // tq, h * (s // tq))

    def body(q_ref, k_ref, v_ref, *rest):
        o_ref, lt_ref = rest[nc:nc + 2]
        if nc:
            finish = hooks(rest[:nc], rest[nc + 2:2 * nc + 2], rest[2 * nc + 2:])
        i = pl.program_id(1)
        q0 = i * tq
        qs = q_ref[...] * scale
        tri = (_iota((bk, bk), 0) > _iota((bk, bk), 1)).astype(CDT)

        def step(k0, carry, diag, r0=0):
            cf, acc = carry
            kb = k_ref[pl.ds(k0, bk), :]
            vb = v_ref[pl.ds(k0, bk), :]
            causal, lb, lk = _sb_block(qs[r0:], kb, diag)
            w = jnp.exp(lb + cf + _split_dot(lk, tri, False, 2))
            if causal is not None:
                w = jnp.where(causal, w, 0.0)
            return cf + jnp.sum(lk, axis=1, keepdims=True), acc + _dot(w, vb)

        carry = (jnp.zeros((tq, 1), F32), jnp.zeros((tq, dh), F32))
        for d in reversed(range(tq // bk)):
            r0 = d * bk
            sub = step(pl.multiple_of(q0 + r0, bk), tuple(t[r0:] for t in carry), 0, r0)
            carry = tuple(jnp.concatenate([t[:r0], u], axis=0) if r0 else u for t, u in zip(carry, sub))
        nfull = q0 // bk
        cf, acc = lax.fori_loop(
            0, nfull, lambda jj, c: step(pl.multiple_of((nfull - 1 - jj) * bk, bk), c, None), carry)
        o_ref[...] = acc
        lt_ref[...] = cf
        if nc:
            finish()

    return pl.pallas_call(
        body, name="sb_fwd", grid=(h, s // tq),
        in_specs=[pl.BlockSpec((None, tq, dh), lambda a, i: (a, i, 0)),
                  pl.BlockSpec((None, s, dh), lambda a, i: (a, 0, 0)),
                  pl.BlockSpec((None, s, dh), lambda a, i: (a, 0, 0))] + [HBM] * nc,
        out_specs=[pl.BlockSpec((None, tq, dh), lambda a, i: (a, i, 0)),
                   pl.BlockSpec((None, tq, 1), lambda a, i: (a, i, 0))] + [HBM] * nc,
        out_shape=[jax.ShapeDtypeStruct((h, s, dh), F32), jax.ShapeDtypeStruct((h, s, 1), F32)]
        + _exchange_shapes(comm, False),
        scratch_shapes=_exchange_sems(nc) if nc else [],
        compiler_params=_cparams(("arbitrary", "arbitrary")),
    )(q, k, v, *comm)


def _sb_bwd(q, k, v, ltot, do, comm=()):
    h, s, dh = q.shape
    tq, bk = min(SB_TQ, s), min(SB_BK, s)
    scale = dh ** -0.5
    nc = len(comm)
    hooks = _fused_exchange(True, s // tq, h * (s // tq))

    def body(q_ref, k_ref, v_ref, lt_ref, do_ref, *rest):
        dq_ref, dk_ref, dv_ref = rest[nc:nc + 3]
        if nc:
            finish = hooks(rest[:nc], rest[nc + 3:2 * nc + 3], rest[2 * nc + 3:])
        i = pl.program_id(1)

        @pl.when(i == 0)
        def _():
            dk_ref[...] = jnp.zeros_like(dk_ref)
            dv_ref[...] = jnp.zeros_like(dv_ref)

        q0 = i * tq
        qs = q_ref[...] * scale
        dov = do_ref[...].astype(CDT)
        ltot = lt_ref[...]
        tri_le = (_iota((bk, bk), 0) <= _iota((bk, bk), 1)).astype(CDT)
        tri_lt = (_iota((bk, bk), 0) < _iota((bk, bk), 1)).astype(CDT)

        def step(k0, carry, diag, r0=0):
            cf, cg, dq = carry
            kb = k_ref[pl.ds(k0, bk), :]
            vb = v_ref[pl.ds(k0, bk), :]
            causal, lb, lk = _sb_block(qs[r0:], kb, diag)
            w = jnp.exp(lb + ltot[r0:] - (cf + _split_dot(lk, tri_le, False, 2)))
            if causal is not None:
                w = jnp.where(causal, w, 0.0)
            g = w * _dot(dov[r0:], vb, NT)
            gsum = cg + _split_dot(g, tri_lt, False, 2)
            dz = g - (g + gsum) * jnp.exp(lb)
            if causal is not None:
                dz = jnp.where(causal, dz, 0.0)
            dzc = dz.astype(CDT)
            dk_ref[pl.ds(k0, bk), :] += _dot(dzc, qs[r0:], TN)
            dv_ref[pl.ds(k0, bk), :] += _dot(w, dov[r0:], TN)
            return (cf + jnp.sum(lk, axis=1, keepdims=True), cg + jnp.sum(g, axis=1, keepdims=True),
                    dq + _dot(dzc, kb))

        carry = (jnp.zeros((tq, 1), F32), jnp.zeros((tq, 1), F32), jnp.zeros((tq, dh), F32))
        carry = lax.fori_loop(0, q0 // bk, lambda jj, c: step(pl.multiple_of(jj * bk, bk), c, None), carry)
        for d in range(tq // bk):
            r0 = d * bk
            sub = step(pl.multiple_of(q0 + r0, bk), tuple(t[r0:] for t in carry), 0, r0)
            carry = tuple(jnp.concatenate([t[:r0], u], axis=0) if r0 else u for t, u in zip(carry, sub))
        dq_ref[...] = carry[2] * scale
        if nc:
            finish()

    tile = pl.BlockSpec((None, tq, dh), lambda a, i: (a, i, 0))
    full = pl.BlockSpec((None, s, dh), lambda a, i: (a, 0, 0))
    shp = jax.ShapeDtypeStruct((h, s, dh), F32)
    return pl.pallas_call(
        body, name="sb_bwd", grid=(h, s // tq),
        in_specs=[tile, full, full, pl.BlockSpec((None, tq, 1), lambda a, i: (a, i, 0)), tile] + [HBM] * nc,
        out_specs=[tile, full, full] + [HBM] * nc,
        out_shape=[shp, shp, shp] + _exchange_shapes(comm, True),
        scratch_shapes=_exchange_sems(nc) if nc else [],
        compiler_params=_cparams(("arbitrary", "arbitrary")),
    )(q, k, v, ltot, do, *comm)


def _pick_lane(tile, r):
    return jnp.sum(jnp.where(_iota(tile.shape, 1) == r, tile, 0.0), axis=1, keepdims=True)


def _pick_row(tile, r):
    return jnp.sum(jnp.where(_iota(tile.shape, 0) == r, tile, 0.0), axis=0, keepdims=True)


def _ssd_chunk_setup(c_ref, b_ref, dac_ref, dar_ref, cb_ref, acsc_ref, acsr_ref):
    l = SSD_L
    tdt = F32 if CDT == F32 else BF16
    lower = (_iota((l, l), 1) <= _iota((l, l), 0)).astype(tdt)
    upper = (_iota((l, l), 0) <= _iota((l, l), 1)).astype(tdt)
    cb_ref[...] = _dot(c_ref[...], b_ref[...], NT)
    acsc_ref[...] = _split_dot(dac_ref[...], lower, True, 3)
    acsr_ref[...] = _split_dot(dar_ref[...], upper, False, 3)


def _ssd_fwd(xh, dtc, dac, dar, dsk, xbc):
    hh, s, p = xh.shape
    l, n, g_, r_ = SSD_L, SSD_N, SSD_G, SSD_R
    nc = s // l
    boff = SSD_INNER // n
    coff = boff + g_

    def body(x_ref, dtc_ref, dac_ref, dar_ref, dsk_ref, b_ref, c_ref, y_ref, st_ref,
             state_ref, cb_ref, acsc_ref, acsr_ref):
        c = pl.program_id(1)
        r = pl.program_id(2)

        @pl.when(r == 0)
        def _():
            _ssd_chunk_setup(c_ref, b_ref, dac_ref, dar_ref, cb_ref, acsc_ref, acsr_ref)

        @pl.when(c == 0)
        def _():
            state_ref[r] = jnp.zeros((n, p), F32)

        a_col = _pick_lane(acsc_ref[...], r)
        a_row = _pick_row(acsr_ref[...], r)
        dt_col = _pick_lane(dtc_ref[...], r)
        dsk_h = _pick_lane(dsk_ref[...], r)
        xv = x_ref[...]
        xd = xv * dt_col
        mask = _iota((l, l), 1) <= _iota((l, l), 0)
        decay = jnp.where(mask, jnp.exp(jnp.minimum(a_col - a_row, 0.0)), 0.0)
        w = cb_ref[...] * decay
        hprev = state_ref[r]
        cv = c_ref[...]
        y = _dot(w, xd) + jnp.exp(a_col) * _dot(cv, hprev)
        y_ref[...] = y + dsk_h * xv
        a_end = a_col[l - 1:l, :]
        dte = jnp.exp(a_end - a_col)
        st_ref[...] = hprev
        state_ref[r] = hprev * jnp.exp(a_end) + _dot(b_ref[...], xd * dte, TN)

    return pl.pallas_call(
        body, name="ssd_fwd", grid=(g_, nc, r_),
        in_specs=[pl.BlockSpec((None, l, p), lambda g, c, r: (g * r_ + r, c, 0)),
                  pl.BlockSpec((None, l, r_), lambda g, c, r: (g, c, 0)),
                  pl.BlockSpec((None, l, r_), lambda g, c, r: (g, c, 0)),
                  pl.BlockSpec((None, r_, l), lambda g, c, r: (g, 0, c)),
                  pl.BlockSpec((None, 1, r_), lambda g, c, r: (g, 0, 0)),
                  pl.BlockSpec((l, n), lambda g, c, r: (c, boff + g)),
                  pl.BlockSpec((l, n), lambda g, c, r: (c, coff + g))],
        out_specs=[pl.BlockSpec((None, l, p), lambda g, c, r: (g * r_ + r, c, 0)),
                   pl.BlockSpec((None, None, n, p), lambda g, c, r: (g * r_ + r, c, 0, 0))],
        out_shape=[jax.ShapeDtypeStruct((hh, s, p), F32),
                   jax.ShapeDtypeStruct((hh, nc, n, p), F32)],
        scratch_shapes=[pltpu.VMEM((r_, n, p), F32), pltpu.VMEM((l, l), F32),
                        pltpu.VMEM((l, r_), F32), pltpu.VMEM((r_, l), F32)],
        compiler_params=_cparams(("parallel", "arbitrary", "arbitrary")),
    )(xh, dtc, dac, dar, dsk, xbc, xbc)


def _ssd_bwd(xh, dtc, dac, dar, dsk, xbc, st, dy):
    hh, s, p = xh.shape
    l, n, g_, r_ = SSD_L, SSD_N, SSD_G, SSD_R
    nc = s // l
    boff = SSD_INNER // n
    coff = boff + g_

    def body(x_ref, dtc_ref, dac_ref, dar_ref, dsk_ref, b_ref, c_ref, st_ref, dy_ref,
             dx_ref, dda_ref, dxx_ref, db_ref, dc_ref, dah_ref, ddsk_ref,
             dstate_ref, cb_ref, acsc_ref, acsr_ref):
        c = pl.program_id(1)
        r = pl.program_id(2)

        @pl.when(r == 0)
        def _():
            _ssd_chunk_setup(c_ref, b_ref, dac_ref, dar_ref, cb_ref, acsc_ref, acsr_ref)
            db_ref[...] = jnp.zeros_like(db_ref)
            dc_ref[...] = jnp.zeros_like(dc_ref)
            dda_ref[...] = jnp.zeros_like(dda_ref)
            dxx_ref[...] = jnp.zeros_like(dxx_ref)

        @pl.when(jnp.logical_and(c == 0, r == 0))
        def _():
            dah_ref[...] = jnp.zeros_like(dah_ref)
            ddsk_ref[...] = jnp.zeros_like(ddsk_ref)

        @pl.when(c == 0)
        def _():
            dstate_ref[r] = jnp.zeros((n, p), F32)

        a_col = _pick_lane(acsc_ref[...], r)
        a_row = _pick_row(acsr_ref[...], r)
        dt_col = _pick_lane(dtc_ref[...], r)
        dsk_h = _pick_lane(dsk_ref[...], r)
        xv = x_ref[...]
        dyv = dy_ref[...]
        xd = xv * dt_col
        il = _iota((l, l), 0)
        isx = _iota((l, l), 1)
        decay = jnp.where(isx <= il, jnp.exp(jnp.minimum(a_col - a_row, 0.0)), 0.0)
        cb = cb_ref[...]
        w = cb * decay
        dhn = dstate_ref[r]
        hc = st_ref[...]
        bv = b_ref[...]
        cv = c_ref[...]
        a_end = a_col[l - 1:l, :]
        ea = jnp.exp(a_col)
        dte = jnp.exp(a_end - a_col)

        dx_state = dte * _dot(bv, dhn)
        dxd = _dot(w, dyv, TN) + dx_state
        md = decay * _dot(dyv, xd, NT)
        m = md * cb
        dc_ref[...] += _dot(md, bv) + ea * _dot(dyv, hc, NT)
        db_ref[...] += _dot(md, cv, TN) + dte * _dot(xd, dhn, NT)
        dstate_ref[r] = jnp.exp(a_end) * dhn + _dot(cv, dyv * ea, TN)

        tdt = F32 if CDT == F32 else BF16
        t1 = (isx >= il).astype(tdt)
        yoff = ea * _dot(cv, hc)
        xdx = jnp.sum(xd * dx_state, axis=1, keepdims=True)
        vec = jnp.sum(dyv * yoff, axis=1, keepdims=True) - xdx
        end_term = jnp.sum(xdx, axis=0, keepdims=True) + \
            jnp.exp(a_end) * jnp.sum(jnp.sum(hc * dhn, axis=1, keepdims=True), axis=0, keepdims=True)
        zmat = _split_dot(m, t1, True, 2)
        span = jnp.sum(jnp.where(isx < il, zmat, 0.0), axis=1, keepdims=True)
        rc = _split_dot(jnp.broadcast_to(vec, (l, 128)), t1, True, 2)[:, :1]
        dda = span + rc + end_term
        dxx = jnp.sum(dxd * xv, axis=1, keepdims=True)

        lane = _iota((l, r_), 1) == r
        dda_ref[...] += jnp.where(lane, dda, 0.0)
        dxx_ref[...] += jnp.where(lane, dxx, 0.0)
        dx_ref[...] = dxd * dt_col + dsk_h * dyv
        lane1 = _iota((1, r_), 1) == r
        dah_ref[...] += jnp.where(lane1, jnp.sum(dda * dt_col, axis=0, keepdims=True), 0.0)
        ddsk_ref[...] += jnp.where(
            lane1, jnp.sum(jnp.sum(dyv * xv, axis=1, keepdims=True), axis=0, keepdims=True), 0.0)

    rev = lambda c: nc - 1 - c
    xspec = pl.BlockSpec((None, l, p), lambda g, c, r: (g * r_ + r, rev(c), 0))
    cspec = pl.BlockSpec((None, l, r_), lambda g, c, r: (g, rev(c), 0))
    hspec = pl.BlockSpec((None, 1, r_), lambda g, c, r: (g, 0, 0))
    return pl.pallas_call(
        body, name="ssd_bwd", grid=(g_, nc, r_),
        in_specs=[xspec, cspec, cspec,
                  pl.BlockSpec((None, r_, l), lambda g, c, r: (g, 0, rev(c))),
                  hspec,
                  pl.BlockSpec((l, n), lambda g, c, r: (rev(c), boff + g)),
                  pl.BlockSpec((l, n), lambda g, c, r: (rev(c), coff + g)),
                  pl.BlockSpec((None, None, n, p), lambda g, c, r: (g * r_ + r, rev(c), 0, 0)),
                  xspec],
        out_specs=[xspec, cspec, cspec,
                   pl.BlockSpec((l, n), lambda g, c, r: (rev(c), g)),
                   pl.BlockSpec((l, n), lambda g, c, r: (rev(c), g)),
                   hspec, hspec],
        out_shape=[jax.ShapeDtypeStruct((hh, s, p), F32),
                   jax.ShapeDtypeStruct((g_, s, r_), F32),
                   jax.ShapeDtypeStruct((g_, s, r_), F32),
                   jax.ShapeDtypeStruct((s, g_ * n), F32),
                   jax.ShapeDtypeStruct((s, g_ * n), F32),
                   jax.ShapeDtypeStruct((g_, 1, r_), F32),
                   jax.ShapeDtypeStruct((g_, 1, r_), F32)],
        scratch_shapes=[pltpu.VMEM((r_, n, p), F32), pltpu.VMEM((l, l), F32),
                        pltpu.VMEM((l, r_), F32), pltpu.VMEM((r_, l), F32)],
        compiler_params=_cparams(("parallel", "arbitrary", "arbitrary")),
    )(xh, dtc, dac, dar, dsk, xbc, xbc, st, dy)


def _ssd_chunk_common(c_ref, b_ref, dac_ref, dar_ref):
    l = SSD_L
    tdt = F32 if CDT == F32 else BF16
    lower = (_iota((l, l), 1) <= _iota((l, l), 0)).astype(tdt)
    upper = (_iota((l, l), 0) <= _iota((l, l), 1)).astype(tdt)
    cb = _dot(c_ref[...], b_ref[...], NT)
    return cb, _split_dot(dac_ref[...], lower, True, 3), _split_dot(dar_ref[...], upper, False, 3)


def _ssd_fwd_g(xh, dtc, dac, dar, dsk, xbc):
    hh, s, p = xh.shape
    l, n, g_, r_ = SSD_L, SSD_N, SSD_G, SSD_R
    nc = s // l
    boff = SSD_INNER // n
    coff = boff + g_

    def body(x_ref, dtc_ref, dac_ref, dar_ref, dsk_ref, b_ref, c_ref, y_ref, st_ref, state_ref):
        c = pl.program_id(1)

        @pl.when(c == 0)
        def _():
            state_ref[...] = jnp.zeros_like(state_ref)

        cb, acs_c, acs_r = _ssd_chunk_common(c_ref, b_ref, dac_ref, dar_ref)
        mask = _iota((l, l), 1) <= _iota((l, l), 0)
        cv, bv = c_ref[...], b_ref[...]
        dtcv, dskv = dtc_ref[...], dsk_ref[...]
        for r in range(r_):
            a_col = _pick_lane(acs_c, r)
            a_row = _pick_row(acs_r, r)
            dt_col = _pick_lane(dtcv, r)
            dsk_h = _pick_lane(dskv, r)
            xv = x_ref[r]
            xd = xv * dt_col
            decay = jnp.where(mask, jnp.exp(jnp.minimum(a_col - a_row, 0.0)), 0.0)
            hprev = state_ref[r]
            y = _dot(cb * decay, xd) + jnp.exp(a_col) * _dot(cv, hprev)
            y_ref[r] = y + dsk_h * xv
            a_end = a_col[l - 1:l, :]
            st_ref[r] = hprev
            state_ref[r] = hprev * jnp.exp(a_end) + _dot(bv, xd * jnp.exp(a_end - a_col), TN)

    return pl.pallas_call(
        body, name="ssd_fwd", grid=(g_, nc),
        in_specs=[pl.BlockSpec((r_, l, p), lambda g, c: (g, c, 0)),
                  pl.BlockSpec((None, l, r_), lambda g, c: (g, c, 0)),
                  pl.BlockSpec((None, l, r_), lambda g, c: (g, c, 0)),
                  pl.BlockSpec((None, r_, l), lambda g, c: (g, 0, c)),
                  pl.BlockSpec((None, 1, r_), lambda g, c: (g, 0, 0)),
                  pl.BlockSpec((l, n), lambda g, c: (c, boff + g)),
                  pl.BlockSpec((l, n), lambda g, c: (c, coff + g))],
        out_specs=[pl.BlockSpec((r_, l, p), lambda g, c: (g, c, 0)),
                   pl.BlockSpec((r_, None, n, p), lambda g, c: (g, c, 0, 0))],
        out_shape=[jax.ShapeDtypeStruct((hh, s, p), F32),
                   jax.ShapeDtypeStruct((hh, nc, n, p), F32)],
        scratch_shapes=[pltpu.VMEM((r_, n, p), F32)],
        compiler_params=_cparams(("parallel", "arbitrary")),
    )(xh, dtc, dac, dar, dsk, xbc, xbc)


def _ssd_bwd_g(xh, dtc, dac, dar, dsk, xbc, st, dy):
    hh, s, p = xh.shape
    l, n, g_, r_ = SSD_L, SSD_N, SSD_G, SSD_R
    nc = s // l
    boff = SSD_INNER // n
    coff = boff + g_

    def body(x_ref, dtc_ref, dac_ref, dar_ref, dsk_ref, b_ref, c_ref, st_ref, dy_ref,
             dx_ref, dda_ref, dxx_ref, db_ref, dc_ref, dah_ref, ddsk_ref, dstate_ref):
        c = pl.program_id(1)

        @pl.when(c == 0)
        def _():
            dstate_ref[...] = jnp.zeros_like(dstate_ref)
            dah_ref[...] = jnp.zeros_like(dah_ref)
            ddsk_ref[...] = jnp.zeros_like(ddsk_ref)

        cb, acs_c, acs_r = _ssd_chunk_common(c_ref, b_ref, dac_ref, dar_ref)
        il = _iota((l, l), 0)
        isx = _iota((l, l), 1)
        tdt = F32 if CDT == F32 else BF16
        t1 = (isx >= il).astype(tdt)
        cv, bv = c_ref[...], b_ref[...]
        dtcv, dskv = dtc_ref[...], dsk_ref[...]
        lane = _iota((l, r_), 1)
        lane1 = _iota((1, r_), 1)
        dda_all = jnp.zeros((l, r_), F32)
        dxx_all = jnp.zeros((l, r_), F32)
        dah_all = jnp.zeros((1, r_), F32)
        ddsk_all = jnp.zeros((1, r_), F32)
        db_acc = jnp.zeros((l, n), F32)
        dc_acc = jnp.zeros((l, n), F32)
        for r in range(r_):
            a_col = _pick_lane(acs_c, r)
            a_row = _pick_row(acs_r, r)
            dt_col = _pick_lane(dtcv, r)
            dsk_h = _pick_lane(dskv, r)
            xv = x_ref[r]
            dyv = dy_ref[r]
            xd = xv * dt_col
            decay = jnp.where(isx <= il, jnp.exp(jnp.minimum(a_col - a_row, 0.0)), 0.0)
            dhn = dstate_ref[r]
            hc = st_ref[r]
            a_end = a_col[l - 1:l, :]
            ea = jnp.exp(a_col)
            dte = jnp.exp(a_end - a_col)

            dx_state = dte * _dot(bv, dhn)
            dxd = _dot(cb * decay, dyv, TN) + dx_state
            md = decay * _dot(dyv, xd, NT)
            dc_acc = dc_acc + _dot(md, bv) + ea * _dot(dyv, hc, NT)
            db_acc = db_acc + _dot(md, cv, TN) + dte * _dot(xd, dhn, NT)
            dstate_ref[r] = jnp.exp(a_end) * dhn + _dot(cv, dyv * ea, TN)

            yoff = ea * _dot(cv, hc)
            xdx = jnp.sum(xd * dx_state, axis=1, keepdims=True)
            vec = jnp.sum(dyv * yoff, axis=1, keepdims=True) - xdx
            end_term = jnp.sum(xdx, axis=0, keepdims=True) + \
                jnp.exp(a_end) * jnp.sum(jnp.sum(hc * dhn, axis=1, keepdims=True), axis=0, keepdims=True)
            zmat = _split_dot(md * cb, t1, True, 2)
            span = jnp.sum(jnp.where(isx < il, zmat, 0.0), axis=1, keepdims=True)
            rc = _split_dot(jnp.broadcast_to(vec, (l, 128)), t1, True, 2)[:, :1]
            dda = span + rc + end_term
            dda_all = jnp.where(lane == r, dda, dda_all)
            dxx_all = jnp.where(lane == r, jnp.sum(dxd * xv, axis=1, keepdims=True), dxx_all)
            dx_ref[r] = dxd * dt_col + dsk_h * dyv
            dah_all = jnp.where(lane1 == r, jnp.sum(dda * dt_col, axis=0, keepdims=True), dah_all)
            ddsk_all = jnp.where(
                lane1 == r, jnp.sum(jnp.sum(dyv * xv, axis=1, keepdims=True), axis=0, keepdims=True), ddsk_all)
        dda_ref[...] = dda_all
        dxx_ref[...] = dxx_all
        db_ref[...] = db_acc
        dc_ref[...] = dc_acc
        dah_ref[...] += dah_all
        ddsk_ref[...] += ddsk_all

    rev = lambda c: nc - 1 - c
    xspec = pl.BlockSpec((r_, l, p), lambda g, c: (g, rev(c), 0))
    cspec = pl.BlockSpec((None, l, r_), lambda g, c: (g, rev(c), 0))
    hspec = pl.BlockSpec((None, 1, r_), lambda g, c: (g, 0, 0))
    return pl.pallas_call(
        body, name="ssd_bwd", grid=(g_, nc),
        in_specs=[xspec, cspec, cspec,
                  pl.BlockSpec((None, r_, l), lambda g, c: (g, 0, rev(c))),
                  hspec,
                  pl.BlockSpec((l, n), lambda g, c: (rev(c), boff + g)),
                  pl.BlockSpec((l, n), lambda g, c: (rev(c), coff + g)),
                  pl.BlockSpec((r_, None, n, p), lambda g, c: (g, rev(c), 0, 0)),
                  xspec],
        out_specs=[xspec, cspec, cspec,
                   pl.BlockSpec((l, n), lambda g, c: (rev(c), g)),
                   pl.BlockSpec((l, n), lambda g, c: (rev(c), g)),
                   hspec, hspec],
        out_shape=[jax.ShapeDtypeStruct((hh, s, p), F32),
                   jax.ShapeDtypeStruct((g_, s, r_), F32),
                   jax.ShapeDtypeStruct((g_, s, r_), F32),
                   jax.ShapeDtypeStruct((s, g_ * n), F32),
                   jax.ShapeDtypeStruct((s, g_ * n), F32),
                   jax.ShapeDtypeStruct((g_, 1, r_), F32),
                   jax.ShapeDtypeStruct((g_, 1, r_), F32)],
        scratch_shapes=[pltpu.VMEM((r_, n, p), F32)],
        compiler_params=_cparams(("parallel", "arbitrary")),
    )(xh, dtc, dac, dar, dsk, xbc, xbc, st, dy)


CONV_TC = 256
CONV_RC = 512


def _conv_taps(x_ref, head_ref, rc):
    base = CONV_PAD - (CONV_K - 1)
    head_ref[pl.ds(0, CONV_PAD), :] = jnp.zeros((CONV_PAD, head_ref.shape[1]), F32)
    head_ref[pl.ds(CONV_PAD, rc), :] = x_ref[pl.ds(0, rc), :]

    def tap(t0, kk):
        if t0 == 0:
            return head_ref[pl.ds(base + kk, rc), :]
        return x_ref[pl.ds(t0 - (CONV_K - 1) + kk, rc), :]
    return tap


def _conv_fwd(x, w, b):
    s, ch = x.shape
    rc = min(CONV_RC, s)

    def body(x_ref, w_ref, b_ref, pre_ref, act_ref, head_ref):
        wv = w_ref[...]
        tap = _conv_taps(x_ref, head_ref, rc)
        for t0 in range(0, s, rc):
            acc = jnp.broadcast_to(b_ref[...], (rc, CONV_TC))
            for kk in range(CONV_K):
                acc = acc + wv[kk:kk + 1, :] * tap(t0, kk)
            pre_ref[pl.ds(t0, rc), :] = acc
            act_ref[pl.ds(t0, rc), :] = _silu(acc)

    col = pl.BlockSpec((s, CONV_TC), lambda j: (0, j))
    shp = jax.ShapeDtypeStruct((s, ch), F32)
    return pl.pallas_call(
        body, name="conv_fwd", grid=(ch // CONV_TC,),
        in_specs=[col, pl.BlockSpec((CONV_K, CONV_TC), lambda j: (0, j)),
                  pl.BlockSpec((1, CONV_TC), lambda j: (0, j))],
        out_specs=[col, col],
        out_shape=[shp, shp],
        scratch_shapes=[pltpu.VMEM((CONV_PAD + rc, CONV_TC), F32)],
        compiler_params=_cparams(("parallel",)),
    )(x, w, b)


def _conv_bwd(x, pre, dact, w):
    s, ch = x.shape
    rc = min(CONV_RC, s)

    def body(x_ref, pre_ref, da_ref, w_ref, dx_ref, dw_ref, db_ref, dpre_ref, head_ref):
        wv = w_ref[...]
        tap = _conv_taps(x_ref, head_ref, rc)
        for t0 in range(0, s, rc):
            dpre_ref[pl.ds(t0, rc), :] = da_ref[pl.ds(t0, rc), :] * _dsilu(pre_ref[pl.ds(t0, rc), :])
        dpre_ref[pl.ds(s, CONV_PAD), :] = jnp.zeros((CONV_PAD, CONV_TC), F32)
        dws = [jnp.zeros((1, CONV_TC), F32) for _ in range(CONV_K)]
        dbs = jnp.zeros((1, CONV_TC), F32)
        for t0 in range(0, s, rc):
            acc = jnp.zeros((rc, CONV_TC), F32)
            dp = dpre_ref[pl.ds(t0, rc), :]
            for kk in range(CONV_K):
                acc = acc + wv[kk:kk + 1, :] * dpre_ref[pl.ds(t0 + CONV_K - 1 - kk, rc), :]
                dws[kk] = dws[kk] + jnp.sum(dp * tap(t0, kk), axis=0, keepdims=True)
            dbs = dbs + jnp.sum(dp, axis=0, keepdims=True)
            dx_ref[pl.ds(t0, rc), :] = acc.astype(dx_ref.dtype)
        for kk in range(CONV_K):
            dw_ref[kk:kk + 1, :] = dws[kk]
        db_ref[...] = dbs

    col = pl.BlockSpec((s, CONV_TC), lambda j: (0, j))
    return pl.pallas_call(
        body, name="conv_bwd", grid=(ch // CONV_TC,),
        in_specs=[col, col, col, pl.BlockSpec((CONV_K, CONV_TC), lambda j: (0, j))],
        out_specs=[col, pl.BlockSpec((CONV_K, CONV_TC), lambda j: (0, j)),
                   pl.BlockSpec((1, CONV_TC), lambda j: (0, j))],
        out_shape=[jax.ShapeDtypeStruct((s, ch), CDT),
                   jax.ShapeDtypeStruct((CONV_K, ch), F32),
                   jax.ShapeDtypeStruct((1, ch), F32)],
        scratch_shapes=[pltpu.VMEM((s + CONV_PAD, CONV_TC), F32), pltpu.VMEM((CONV_PAD + rc, CONV_TC), F32)],
        compiler_params=_cparams(("parallel",)),
    )(x, pre, dact, w)


MEM_TS = 512


def _mem_fwd(mq, kv):
    s = mq.shape[0]
    m = kv.shape[0]
    ts = min(MEM_TS, s)
    scale = MEM_DH ** -0.5

    def body(q_ref, k_ref, v_ref, o_ref):
        sc = _dot(q_ref[...], k_ref[...], NT) * scale
        e = jnp.exp(sc - jnp.max(sc, axis=1, keepdims=True))
        pr = e / jnp.sum(e, axis=1, keepdims=True)
        o_ref[...] = _dot(pr, v_ref[...]).astype(o_ref.dtype)

    return pl.pallas_call(
        body, name="mem_fwd", grid=(MEM_H, s // ts),
        in_specs=[pl.BlockSpec((ts, MEM_DH), lambda a, i: (i, a)),
                  pl.BlockSpec((m, MEM_DH), lambda a, i: (0, a)),
                  pl.BlockSpec((m, MEM_DH), lambda a, i: (0, MEM_H + a))],
        out_specs=pl.BlockSpec((ts, MEM_DH), lambda a, i: (i, a)),
        out_shape=jax.ShapeDtypeStruct((s, MEM_H * MEM_DH), CDT),
        compiler_params=_cparams(("parallel", "arbitrary")),
    )(mq, kv, kv)


def _mem_bwd(mq, kv, do):
    s = mq.shape[0]
    m = kv.shape[0]
    ts = min(MEM_TS, s)
    scale = MEM_DH ** -0.5

    def body(q_ref, k_ref, v_ref, do_ref, dq_ref, dk_ref, dv_ref):
        i = pl.program_id(1)

        @pl.when(i == 0)
        def _():
            dk_ref[...] = jnp.zeros_like(dk_ref)
            dv_ref[...] = jnp.zeros_like(dv_ref)

        qv, kb, vb, dov = q_ref[...], k_ref[...], v_ref[...], do_ref[...]
        sc = _dot(qv, kb, NT) * scale
        e = jnp.exp(sc - jnp.max(sc, axis=1, keepdims=True))
        pr = e / jnp.sum(e, axis=1, keepdims=True)
        dp = _dot(dov, vb, NT)
        ds = pr * (dp - jnp.sum(dp * pr, axis=1, keepdims=True)) * scale
        dq_ref[...] = _dot(ds, kb).astype(dq_ref.dtype)
        dk_ref[...] += _dot(ds, qv, TN)
        dv_ref[...] += _dot(pr, dov, TN)

    tile = pl.BlockSpec((ts, MEM_DH), lambda a, i: (i, a))
    kvo = pl.BlockSpec((m, MEM_DH), lambda a, i: (0, a))
    return pl.pallas_call(
        body, name="mem_bwd", grid=(MEM_H, s // ts),
        in_specs=[tile, kvo, pl.BlockSpec((m, MEM_DH), lambda a, i: (0, MEM_H + a)), tile],
        out_specs=[tile, kvo, kvo],
        out_shape=[jax.ShapeDtypeStruct((s, MEM_H * MEM_DH), CDT),
                   jax.ShapeDtypeStruct((m, MEM_H * MEM_DH), F32),
                   jax.ShapeDtypeStruct((m, MEM_H * MEM_DH), F32)],
        compiler_params=_cparams(("parallel", "arbitrary")),
    )(mq, kv, kv, do)


def _heads(t, nh, dh):
    return t.reshape(t.shape[0], nh, dh).transpose(1, 0, 2)


def _unheads(t):
    return t.transpose(1, 0, 2).reshape(t.shape[1], t.shape[0] * t.shape[2])


def _group_cols(t):
    return t.reshape(t.shape[0], SSD_G, SSD_R).transpose(1, 0, 2)


def _pad_cols(t, width):
    return jnp.pad(t, ((0, 0), (0, width - t.shape[1])))


def _full_weight(name, gathered):
    if name in COL_SHARDED:
        return gathered.transpose(1, 0, 2).reshape(gathered.shape[1], N_DEV * gathered.shape[2])
    return gathered.reshape(N_DEV * gathered.shape[1], gathered.shape[2])


def _grad_payload(name, g):
    if name in COL_SHARDED:
        return g.reshape(g.shape[0], N_DEV, g.shape[1] // N_DEV).transpose(1, 0, 2)
    return g.reshape(N_DEV, g.shape[0] // N_DEV, g.shape[1])


def _local_step(x, mem, tgt, p, wt, shards=None):
    s, d = x.shape
    wt = dict(wt)
    c1, c2, c3, c4, c5 = 3 * d, 3 * d + SSD_INNER, 3 * d + SSD_INNER + CONV_DIM, \
        3 * d + SSD_INNER + CONV_DIM + SSD_H, 3 * d + SSD_INNER + CONV_DIM + SSD_H + d
    w_in = wt["w_in"]
    w_main = jnp.concatenate([w_in[:, :c3], w_in[:, c4:]], axis=1)
    w_dt = _pad_cols(w_in[:, c3:c4], DT_PAD)
    seg_name = ["qkv", "z", "xbc", "mq", "gl"]
    seg_dtype = [CDT, F32, F32, CDT, F32]
    seg_off = [0, c1, c2, c3, c3 + d]
    seg_n = [c1, c2 - c1, c3 - c2, d, 3 * d]

    u = _rows(lambda xv, g: _rms(xv, g), [x], [p["norm_mix_pre"]], [(d, CDT)], ts=512, name="f_norm_pre")[0]
    qkv, z, xbc_raw, mq, gl = [
        _mm(u, w_main, b_off=seg_off[i], n=seg_n[i], out_dtype=seg_dtype[i], name="f_in_" + seg_name[i])
        for i in range(5)]
    dt_raw = _mm(u, w_dt, name="f_in_dt")

    bias128 = _pad_cols(p["dt_bias"], DT_PAD)
    alog128 = _pad_cols(p["a_log"], DT_PAD)

    def dt_fn(dtr, bias, alog):
        dt = _softplus(dtr + bias)
        return dt, dt * (-jnp.exp(alog))

    dt128, da128 = _rows(dt_fn, [dt_raw], [bias128, alog128], [(DT_PAD, F32), (DT_PAD, F32)],
                         ts=512, name="f_dt")
    dtc = _group_cols(dt128[:, :SSD_H])
    dac = _group_cols(da128[:, :SSD_H])
    dar = dac.transpose(0, 2, 1)
    dsk = p["d_skip"].reshape(SSD_G, 1, SSD_R)

    conv_w, conv_b = p["conv_w"], p["conv_b"]
    pre, xbc = _conv_fwd(xbc_raw, conv_w, conv_b)
    xh = _heads(xbc[:, :SSD_INNER], SSD_H, SSD_P)
    y_h, st = _ssd_fwd_g(xh, dtc, dac, dar, dsk, xbc)
    y_core = _unheads(y_h)

    def group_norm_fwd(yv, zv, wn):
        y2 = yv * _silu(zv)
        gw = SSD_INNER // SSD_G
        outs = []
        for gi in range(SSD_G):
            seg = y2[:, gi * gw:(gi + 1) * gw]
            outs.append(_rms(seg, wn[:, gi * gw:(gi + 1) * gw]))
        return jnp.concatenate(outs, axis=1)

    y_ssd = _rows(group_norm_fwd, [y_core, z], [p["ssd_norm"]], [(SSD_INNER, CDT)], ts=256, name="f_ssd_post")[0]

    q_h = _heads(qkv[:, :d], SB_H, SB_DH)
    k_h = _heads(qkv[:, d:2 * d], SB_H, SB_DH)
    v_h = _heads(qkv[:, 2 * d:], SB_H, SB_DH)
    o_h, lt_h, *late = _sb_fwd(q_h, k_h, v_h, tuple(shards) if shards is not None else ())
    for n, gth in zip(LATE_W, late):
        wt[n] = _full_weight(n, gth)
    y_sb = _unheads(o_h).astype(CDT)

    mu = _rows(lambda mv, g: _rms(mv, g), [mem], [p["norm_mem"]], [(d, CDT)], ts=256, name="f_norm_mem")[0]
    kv = _mm(mu, wt["w_mem_kv"], out_dtype=CDT, name="f_mem_kv")
    y_mem = _mem_fwd(mq, kv)

    p_sb = _mm(y_sb, wt["w_sb_out"], name="f_sb_out")
    p_ssd = _mm(y_ssd, wt["w_ssd_out"], name="f_ssd_out")
    p_mem = _mm(y_mem, wt["w_mem_out"], name="f_mem_out")

    def merge_fn(glv, a, b, c):
        return (_sigmoid(glv[:, :d]) * a + _sigmoid(glv[:, d:2 * d]) * b + _sigmoid(glv[:, 2 * d:]) * c)

    merged = _rows(merge_fn, [gl, p_sb, p_ssd, p_mem], [], [(d, CDT)], ts=256, name="f_merge")[0]
    mix = _mm(merged, wt["w_o"], name="f_w_o")

    def mid_fn(xv, mixv, g_post, g_pre):
        h1 = xv + _rms(mixv, g_post)
        return h1, _rms(h1, g_pre)

    h1, u2 = _rows(mid_fn, [x, mix], [p["norm_mix_post"], p["norm_mlp_pre"]], [(d, F32), (d, CDT)],
                   ts=512, name="f_mid")
    a1 = _mm(u2, wt["w_up"], name="f_up")
    act = _rows(lambda a: jnp.square(jnp.maximum(a, 0.0)), [a1], [], [(a1.shape[1], CDT)], ts=256, name="f_act")[0]
    ff = _mm(act, wt["w_down"], name="f_down")

    def loss_fn(h1v, ffv, tv, g):
        diff = h1v + _rms(ffv, g) - tv
        tot = jnp.sum(jnp.sum(diff * diff, axis=1, keepdims=True), axis=0, keepdims=True)
        return diff * (1.0 / d), tot

    dh2, loss_acc = _rows(loss_fn, [h1, ff, tgt], [p["norm_mlp_post"]], [(d, F32)], [(1, 128)],
                          ts=512, name="f_loss")
    loss = loss_acc[:, :1] * (0.5 / d)

    sg = {}

    def b_post(ffv, dyv, g):
        dx, dg = _rms_bwd(ffv, g, dyv)
        return dx, dg

    d_ff, sg["norm_mlp_post"] = _rows(b_post, [ff, dh2], [p["norm_mlp_post"]], [(d, CDT)], [(1, d)],
                                      ts=512, name="b_norm_mlp_post")
    dact = _mm(d_ff, wt["w_down"], tb=True, name="b_down_x")
    gw = {"w_down": _mm(act, d_ff, ta=True, name="b_down_w")}
    da1 = _rows(lambda dv, a: dv * 2.0 * jnp.maximum(a, 0.0), [dact, a1], [], [(a1.shape[1], CDT)],
                ts=256, name="b_act")[0]
    du2 = _mm(da1, wt["w_up"], tb=True, name="b_up_x")
    gw["w_up"] = _mm(u2, da1, ta=True, name="b_up_w")

    def b_mid(h1v, du2v, dh2v, mixv, g_pre, g_post):
        dxa, dga = _rms_bwd(h1v, g_pre, du2v)
        dh1 = dh2v + dxa
        dmix, dgb = _rms_bwd(mixv, g_post, dh1)
        return dh1, dmix, dga, dgb

    dh1, dmix, sg["norm_mlp_pre"], sg["norm_mix_post"] = _rows(
        b_mid, [h1, du2, dh2, mix], [p["norm_mlp_pre"], p["norm_mix_post"]],
        [(d, F32), (d, CDT)], [(1, d), (1, d)], ts=256, name="b_mid")
    dmerged = _mm(dmix, wt["w_o"], tb=True, name="b_w_o_x")
    gw["w_o"] = _mm(merged, dmix, ta=True, name="b_w_o_w")

    def b_merge(dm, glv, a, b, c):
        outs, dgl = [], []
        for i, br in enumerate((a, b, c)):
            gt = _sigmoid(glv[:, i * d:(i + 1) * d])
            outs.append(gt * dm)
            dgl.append(dm * br * gt * (1.0 - gt))
        return outs[0], outs[1], outs[2], jnp.concatenate(dgl, axis=1)

    dp_sb, dp_ssd, dp_mem, dgl = _rows(b_merge, [dmerged, gl, p_sb, p_ssd, p_mem], [],
                                       [(d, CDT), (d, CDT), (d, CDT), (3 * d, CDT)], ts=256, name="b_merge")
    dy_sb = _mm(dp_sb, wt["w_sb_out"], tb=True, name="b_sb_out_x")
    gw["w_sb_out"] = _mm(y_sb, dp_sb, ta=True, name="b_sb_out_w")
    dy_ssd = _mm(dp_ssd, wt["w_ssd_out"], tb=True, name="b_ssd_out_x")
    gw["w_ssd_out"] = _mm(y_ssd, dp_ssd, ta=True, name="b_ssd_out_w")
    dy_mem = _mm(dp_mem, wt["w_mem_out"], tb=True, out_dtype=CDT, name="b_mem_out_x")
    gw["w_mem_out"] = _mm(y_mem, dp_mem, ta=True, name="b_mem_out_w")

    dmq, dk_m, dv_m = _mem_bwd(mq, kv, dy_mem)
    dkv = jnp.concatenate([dk_m, dv_m], axis=1).astype(CDT)
    gw["w_mem_kv"] = _mm(mu, dkv, ta=True, name="b_mem_kv_w")
    dmu = _mm(dkv, wt["w_mem_kv"], tb=True, name="b_mem_kv_x")
    sg["norm_mem"] = _rows(lambda mv, dv, g: _rms_bwd(mv, g, dv)[1], [mem, dmu], [p["norm_mem"]], [], [(1, d)],
                           ts=256, name="b_norm_mem")[0]

    payloads = tuple(_grad_payload(n, gw[n]) for n in LATE_W) if shards is not None else ()
    dq_h, dk_h, dv_h, *received = _sb_bwd(q_h, k_h, v_h, lt_h, _heads(dy_sb, SB_H, SB_DH), payloads)
    dqkv = jnp.concatenate([_unheads(dq_h), _unheads(dk_h), _unheads(dv_h)], axis=1).astype(CDT)

    def group_norm_bwd(dyo, yv, zv, wn):
        sz = _silu(zv)
        y2 = yv * sz
        gw_ = SSD_INNER // SSD_G
        dy2, dwn = [], []
        for gi in range(SSD_G):
            sl = slice(gi * gw_, (gi + 1) * gw_)
            dxs, dgs = _rms_bwd(y2[:, sl], wn[:, sl], dyo[:, sl])
            dy2.append(dxs)
            dwn.append(dgs)
        dy2 = jnp.concatenate(dy2, axis=1)
        return dy2 * sz, dy2 * yv * _dsilu(zv), jnp.concatenate(dwn, axis=1)

    dy_core, dz, sg["ssd_norm"] = _rows(group_norm_bwd, [dy_ssd, y_core, z], [p["ssd_norm"]],
                                        [(SSD_INNER, F32), (SSD_INNER, CDT)], [(1, SSD_INNER)],
                                        ts=256, name="b_ssd_post")
    dxh, dda, dxx, d_b, d_c, dah, ddsk = _ssd_bwd_g(xh, dtc, dac, dar, dsk, xbc, st, _heads(dy_core, SSD_H, SSD_P))
    sg["d_skip"] = ddsk.reshape(1, SSD_H)
    sg["a_log"] = dah.reshape(1, SSD_H) * (-jnp.exp(p["a_log"]))

    def b_dt(ddav, dxxv, dtr, bias, alog):
        ddt = ddav * (-jnp.exp(alog)) + dxxv
        draw = ddt * _sigmoid(dtr + bias)
        return draw, jnp.sum(draw, axis=0, keepdims=True)

    ungroup = lambda t: _pad_cols(t.transpose(1, 0, 2).reshape(s, SSD_H), DT_PAD)
    ddt_raw, dbias128 = _rows(b_dt, [ungroup(dda), ungroup(dxx), dt_raw], [bias128, alog128],
                              [(DT_PAD, CDT)], [(1, DT_PAD)], ts=512, name="b_dt")
    sg["dt_bias"] = dbias128[:, :SSD_H]

    dxbc = jnp.concatenate([_unheads(dxh), d_b, d_c], axis=1)
    dxbc_raw, sg["conv_w"], sg["conv_b"] = _conv_bwd(xbc_raw, pre, dxbc, conv_w)

    dseg = [dqkv, dz, dxbc_raw, dmq, dgl]
    dw_main = lax.empty((d, w_main.shape[1]), F32)
    for i in range(5):
        dw_main = _mm(u, dseg[i], ta=True, into=dw_main, into_off=seg_off[i], name="b_in_w_" + seg_name[i])
    dw_dt = _mm(u, ddt_raw, ta=True, name="b_in_w_dt")
    half = w_main.shape[1] // 2
    hosts = {0: _pair_cols([(dw_main, 0, half), (dw_dt, 0, DT_PAD)]),
             2: _pair_cols([(dw_main, half, half)])} if shards is not None else {}
    dus, from_sibling = [], []
    for i in range(5):
        res = _mm(dseg[i], w_main, tb=True, b_koff=seg_off[i], name="b_in_x_" + seg_name[i], hosted=hosts.get(i))
        if i in hosts:
            dus.append(res[0])
            from_sibling += res[1:]
        else:
            dus.append(res)
    dus.append(_mm(ddt_raw, w_dt, tb=True, name="b_in_x_dt"))
    if shards is None:
        gw["w_in"] = jnp.concatenate([dw_main[:, :c3], dw_dt[:, :SSD_H], dw_main[:, c3:]], axis=1)
    else:
        sum_a = _add_cols(dw_main, 0, from_sibling[0], "sum_w_in_a")
        sum_dt = _add_cols(dw_dt, 0, from_sibling[1], "sum_w_in_dt")
        sum_b = _add_cols(dw_main, half, from_sibling[2], "sum_w_in_b")
        gw["w_in"] = jnp.concatenate([sum_a, sum_b[:, :c3 - half], sum_dt[:, :SSD_H], sum_b[:, c3 - half:]], axis=1)

    def b_pre(xv, dh1v, d0, d1, d2, d3, d4, d5, g):
        dx, dg = _rms_bwd(xv, g, d0 + d1 + d2 + d3 + d4 + d5)
        return dh1v + dx, dg

    grad_x, sg["norm_mix_pre"] = _rows(b_pre, [x, dh1] + dus, [p["norm_mix_pre"]], [(d, F32)], [(1, d)],
                                       ts=256, name="b_norm_pre")
    return loss, grad_x, gw, sg, (received if shards is not None else None)


HBM = pl.BlockSpec(memory_space=pltpu.HBM)
MESH = pl.DeviceIdType.MESH


def _me_and_peers():
    x, y, c = lax.axis_index("x"), lax.axis_index("y"), lax.axis_index("c")
    me = 4 * x + 2 * y + c
    peers = [(x, y, 1 - c), (1 - x, y, c), (x, 1 - y, c), (1 - x, 1 - y, c),
             (1 - x, y, 1 - c), (x, 1 - y, 1 - c), (1 - x, 1 - y, 1 - c)]
    return me, peers


def _peer_index(peer):
    return 4 * peer[0] + 2 * peer[1] + peer[2]


def _exchange_copies(ins, outs, send_sems, recv_sems, local_sems, scatter):
    me, peers = _me_and_peers()
    copies = []
    for a in range(len(ins)):
        own = ins[a].at[me] if scatter else ins[a]
        copies.append(pltpu.make_async_copy(own, outs[a].at[me], local_sems.at[a]))
        for kk, peer in enumerate(peers):
            src = ins[a].at[_peer_index(peer)] if scatter else ins[a]
            copies.append(pltpu.make_async_remote_copy(
                src_ref=src, dst_ref=outs[a].at[me],
                send_sem=send_sems.at[a, kk], recv_sem=recv_sems.at[a, kk],
                device_id=peer, device_id_type=MESH))
    return copies


def _exchange_shapes(ins, scatter):
    return [jax.ShapeDtypeStruct(t.shape if scatter else (N_DEV,) + t.shape, t.dtype) for t in ins]


def _exchange_sems(n):
    return [pltpu.SemaphoreType.DMA((n, N_DEV - 1)), pltpu.SemaphoreType.DMA((n, N_DEV - 1)),
            pltpu.SemaphoreType.DMA((n,))]


def _exchange(ins, scatter, name):
    n = len(ins)

    def body(*refs):
        copies = _exchange_copies(refs[:n], refs[n:2 * n], *refs[2 * n:], scatter)
        for cp in copies:
            cp.start()
        for cp in copies:
            cp.wait()

    return pl.pallas_call(
        body, name=name,
        in_specs=[HBM] * n, out_specs=[HBM] * n,
        out_shape=_exchange_shapes(ins, scatter),
        scratch_shapes=_exchange_sems(n),
        compiler_params=pltpu.CompilerParams(has_side_effects=True),
    )(*ins)


def _gather_two_level(shards, name):
    n = len(shards)

    def body(*refs):
        ins, outs = refs[:n], refs[n:2 * n]
        send_sems, recv_sems, local_sems = refs[2 * n:]
        x, y, c = lax.axis_index("x"), lax.axis_index("y"), lax.axis_index("c")
        me, sib = (x, y, c), (x, y, 1 - c)
        chips = [(1 - x, y), (x, 1 - y), (1 - x, 1 - y)]

        def copy(a, k, block, to, src=None):
            slot = outs[a].at[_peer_index(block)]
            return pltpu.make_async_remote_copy(
                src_ref=slot if src is None else src, dst_ref=slot,
                send_sem=send_sems.at[a, k], recv_sem=recv_sems.at[a, k], device_id=to, device_id_type=MESH)

        own = [pltpu.make_async_copy(ins[a], outs[a].at[_peer_index(me)], local_sems.at[a]) for a in range(n)]
        first = []
        for a in range(n):
            first.append(copy(a, 0, me, sib, src=ins[a]))
            first += [copy(a, 1 + j, me, (*chip, c), src=ins[a]) for j, chip in enumerate(chips)]
        for cp in own + first:
            cp.start()
        passed = []
        for j, chip in enumerate(chips):
            for a in range(n):
                copy(a, 1 + j, (*chip, c), me).wait_recv()
                fwd = copy(a, 4 + j, (*chip, c), sib)
                fwd.start()
                passed.append(fwd)
        for a in range(n):
            copy(a, 0, sib, me).wait_recv()
            for j, chip in enumerate(chips):
                copy(a, 4 + j, (*chip, 1 - c), me).wait_recv()
        for cp in first + passed:
            cp.wait_send()
        for cp in own:
            cp.wait()

    return pl.pallas_call(
        body, name=name,
        in_specs=[HBM] * n, out_specs=[HBM] * n,
        out_shape=_exchange_shapes(shards, False),
        scratch_shapes=_exchange_sems(n),
        compiler_params=pltpu.CompilerParams(has_side_effects=True),
    )(*shards)


class _Hosted:
    def __init__(self, ins, shapes, sems, copies):
        self.ins, self.shapes, self.sems, self.copies = ins, shapes, sems, copies


def _pair_cols(pieces):
    n = len(pieces)

    def copies(in_refs, out_refs, sems):
        sib = (lax.axis_index("x"), lax.axis_index("y"), 1 - lax.axis_index("c"))
        return [pltpu.make_async_remote_copy(
            src_ref=in_refs[i].at[:, pl.ds(c0, w)], dst_ref=out_refs[i], send_sem=sems[0].at[i],
            recv_sem=sems[1].at[i], device_id=sib, device_id_type=MESH) for i, (_, c0, w) in enumerate(pieces)]

    return _Hosted([t for t, _, _ in pieces],
                   [jax.ShapeDtypeStruct((t.shape[0], w), t.dtype) for t, _, w in pieces],
                   [pltpu.SemaphoreType.DMA((n,)), pltpu.SemaphoreType.DMA((n,))], copies)


def _add_cols(a, c0, b, name):
    r, w = b.shape
    assert c0 % w == 0
    tr = _pick(r, (256, 128))

    def body(a_ref, b_ref, o_ref):
        o_ref[...] = (a_ref[...] + b_ref[...]).astype(o_ref.dtype)

    return pl.pallas_call(
        body, name=name, grid=(r // tr,),
        in_specs=[pl.BlockSpec((tr, w), lambda i: (i, c0 // w)), pl.BlockSpec((tr, w), lambda i: (i, 0))],
        out_specs=pl.BlockSpec((tr, w), lambda i: (i, 0)),
        out_shape=jax.ShapeDtypeStruct((r, w), CDT),
        compiler_params=_cparams(("parallel",)),
    )(a, b)


N_CHIP = 4


def _chip_scatter(t, name):
    def body(t_ref, o_ref, send_sems, recv_sems, local_sem):
        x, y, c = lax.axis_index("x"), lax.axis_index("y"), lax.axis_index("c")
        mine = 2 * x + y
        copies = [pltpu.make_async_copy(t_ref.at[mine], o_ref.at[mine], local_sem)]
        for j, (px, py) in enumerate([(1 - x, y), (x, 1 - y), (1 - x, 1 - y)]):
            copies.append(pltpu.make_async_remote_copy(
                src_ref=t_ref.at[2 * px + py], dst_ref=o_ref.at[mine],
                send_sem=send_sems.at[j], recv_sem=recv_sems.at[j], device_id=(px, py, c), device_id_type=MESH))
        for cp in copies:
            cp.start()
        for cp in copies:
            cp.wait()

    return pl.pallas_call(
        body, name=name, in_specs=[HBM], out_specs=HBM,
        out_shape=jax.ShapeDtypeStruct(t.shape, t.dtype),
        scratch_shapes=[pltpu.SemaphoreType.DMA((N_CHIP - 1,)), pltpu.SemaphoreType.DMA((N_CHIP - 1,)),
                        pltpu.SemaphoreType.DMA],
        compiler_params=pltpu.CompilerParams(has_side_effects=True),
    )(t)


def _all_reduce_small(v, name):
    r, c = v.shape

    def body(v_ref, o_ref, buf, send_sems, recv_sems):
        me, peers = _me_and_peers()
        buf[me] = v_ref[...]
        copies = []
        for kk, peer in enumerate(peers):
            cp = pltpu.make_async_remote_copy(
                src_ref=v_ref, dst_ref=buf.at[me],
                send_sem=send_sems.at[kk], recv_sem=recv_sems.at[kk],
                device_id=peer, device_id_type=MESH)
            cp.start()
            copies.append(cp)
        for cp in copies:
            cp.wait()
        acc = buf[0]
        for i in range(1, N_DEV):
            acc = acc + buf[i]
        o_ref[...] = acc

    return pl.pallas_call(
        body, name=name,
        in_specs=[pl.BlockSpec(memory_space=pltpu.VMEM)],
        out_specs=pl.BlockSpec(memory_space=pltpu.VMEM),
        out_shape=jax.ShapeDtypeStruct((r, c), F32),
        scratch_shapes=[pltpu.VMEM((N_DEV, r, c), F32),
                        pltpu.SemaphoreType.DMA((N_DEV - 1,)), pltpu.SemaphoreType.DMA((N_DEV - 1,))],
        compiler_params=pltpu.CompilerParams(has_side_effects=True),
    )(v)


def _adamw_math(g, w, m, v):
    m2 = ADAM_B1 * m + (1.0 - ADAM_B1) * g
    v2 = ADAM_B2 * v + (1.0 - ADAM_B2) * jnp.square(g)
    m_hat = m2 / (1.0 - ADAM_B1 ** ADAM_STEP)
    v_hat = v2 / (1.0 - ADAM_B2 ** ADAM_STEP)
    delta = -ADAM_LR * (m_hat / (jnp.sqrt(v_hat) + ADAM_EPS) + ADAM_WD * w)
    return delta, m2, v2


def _adamw_reduce(parts, w, m, v, name):
    r, c = w.shape
    nparts = parts.shape[0]
    tr = _pick(r, (128, 64, 32, 16, 8))

    def body(p_ref, w_ref, m_ref, v_ref, g_ref, d_ref, m2_ref, v2_ref):
        g = p_ref[0].astype(F32)
        for i in range(1, nparts):
            g = g + p_ref[i].astype(F32)
        delta, m2, v2 = _adamw_math(g, w_ref[...], m_ref[...], v_ref[...])
        g_ref[...] = g
        d_ref[...] = delta
        m2_ref[...] = m2
        v2_ref[...] = v2

    tile = pl.BlockSpec((tr, c), lambda i: (i, 0))
    shp = jax.ShapeDtypeStruct((r, c), F32)
    return pl.pallas_call(
        body, name=name, grid=(r // tr,),
        in_specs=[pl.BlockSpec((nparts, tr, c), lambda i: (0, i, 0)), tile, tile, tile],
        out_specs=[tile] * 4, out_shape=[shp] * 4,
        compiler_params=_cparams(("parallel",)),
    )(parts, w, m, v)


def _adamw_plain(g, w, m, v, name):
    def body(g_ref, w_ref, m_ref, v_ref, d_ref, m2_ref, v2_ref):
        delta, m2, v2 = _adamw_math(g_ref[...], w_ref[...], m_ref[...], v_ref[...])
        d_ref[...] = delta
        m2_ref[...] = m2
        v2_ref[...] = v2

    spec = pl.BlockSpec(memory_space=pltpu.VMEM)
    shp = jax.ShapeDtypeStruct(g.shape, F32)
    return pl.pallas_call(
        body, name=name, in_specs=[spec] * 4, out_specs=[spec] * 3, out_shape=[shp] * 3,
    )(g, w, m, v)


SMALL_ROWS, SMALL_COLS = 16, 3072


def _small_step(sg, gcw, loss, ws, ms, vs):
    ns = len(sg)
    widths = [t.shape[1] for t in sg]
    kk_, ch = gcw.shape[1], gcw.shape[2]
    assert ns < SMALL_ROWS and max(widths) <= SMALL_COLS

    def reduce_body(*refs):
        g_refs = refs[:ns]
        gcw_ref, loss_ref, tot_ref, totc_ref = refs[ns:ns + 4]
        mine, buf, minec, bufc, send_sems, recv_sems = refs[ns + 4:]
        me, peers = _me_and_peers()

        mine[...] = jnp.zeros_like(mine)
        for i in range(ns):
            mine[i:i + 1, 0:widths[i]] = g_refs[i][...]
        mine[ns:ns + 1, 0:LANES] = jnp.broadcast_to(loss_ref[...], (1, LANES))
        minec[...] = gcw_ref[...]
        buf[me] = mine[...]
        bufc[me] = minec[...]
        copies = []
        for j, peer in enumerate(peers):
            copies.append(pltpu.make_async_remote_copy(
                src_ref=mine, dst_ref=buf.at[me], send_sem=send_sems.at[0, j], recv_sem=recv_sems.at[0, j],
                device_id=peer, device_id_type=MESH))
            copies.append(pltpu.make_async_remote_copy(
                src_ref=minec, dst_ref=bufc.at[me], send_sem=send_sems.at[1, j], recv_sem=recv_sems.at[1, j],
                device_id=peer, device_id_type=MESH))
        for cp in copies:
            cp.start()
        for cp in copies:
            cp.wait()
        tot = buf[0]
        totc = bufc[0]
        for i in range(1, N_DEV):
            tot = tot + buf[i]
            totc = totc + bufc[i]
        tot_ref[...] = tot
        totc_ref[...] = totc

    vm = pl.BlockSpec(memory_space=pltpu.VMEM)
    tot, totc = pl.pallas_call(
        reduce_body, name="small_reduce",
        in_specs=[vm] * (ns + 2), out_specs=[vm, vm],
        out_shape=[jax.ShapeDtypeStruct((SMALL_ROWS, SMALL_COLS), F32), jax.ShapeDtypeStruct((N_DEV, kk_, ch), F32)],
        scratch_shapes=[pltpu.VMEM((SMALL_ROWS, SMALL_COLS), F32), pltpu.VMEM((N_DEV, SMALL_ROWS, SMALL_COLS), F32),
                        pltpu.VMEM((N_DEV, kk_, ch), F32), pltpu.VMEM((N_DEV, N_DEV, kk_, ch), F32),
                        pltpu.SemaphoreType.DMA((2, N_DEV - 1)), pltpu.SemaphoreType.DMA((2, N_DEV - 1))],
        compiler_params=pltpu.CompilerParams(has_side_effects=True),
    )(*sg, gcw, loss)

    def adamw_body(*refs):
        tot_ref, totc_ref = refs[:2]
        w_refs, m_refs, v_refs = (refs[2 + i * (ns + 1):2 + (i + 1) * (ns + 1)] for i in range(3))
        outs = refs[3 * ns + 5:]
        loss_out = outs[0]
        go, do_, mo, vo = (outs[1 + i * (ns + 1):1 + (i + 1) * (ns + 1)] for i in range(4))
        me, _ = _me_and_peers()
        loss_out[...] = tot_ref[ns:ns + 1, 0:1]
        for i in range(ns + 1):
            g = tot_ref[i:i + 1, 0:widths[i]] if i < ns else totc_ref[me]
            delta, m2, v2 = _adamw_math(g, w_refs[i][...], m_refs[i][...], v_refs[i][...])
            go[i][...] = g
            do_[i][...] = delta
            mo[i][...] = m2
            vo[i][...] = v2

    shapes = [jax.ShapeDtypeStruct(t.shape, F32) for t in ws]
    res = pl.pallas_call(
        adamw_body, name="small_adamw",
        in_specs=[vm] * (3 * ns + 5), out_specs=[vm] * (4 * ns + 5),
        out_shape=[jax.ShapeDtypeStruct((1, 1), F32)] + shapes * 4,
    )(tot, totc, *ws, *ms, *vs)
    n1 = ns + 1
    return res[0], res[1:1 + n1], res[1 + n1:1 + 2 * n1], res[1 + 2 * n1:1 + 3 * n1], res[1 + 3 * n1:]


def _cast_shard(w, name):
    r = w.shape[0]
    return _rows(lambda t: t, [w], [], [(w.shape[1], CDT)], ts=_pick(r, (256, 128)), name=name)[0]


BIG = ["w_in", "w_mem_kv", "w_up", "w_sb_out", "w_ssd_out", "w_mem_out", "w_o", "w_down"]
LATE_W = BIG[1:]
COL_SHARDED = ("w_in", "w_mem_kv", "w_up")
SMALL = ["norm_mix_pre", "conv_b", "dt_bias", "a_log", "d_skip", "ssd_norm", "norm_mem",
         "norm_mix_post", "norm_mlp_pre", "norm_mlp_post"]
ALL_W = ["norm_mix_pre", "w_in", "conv_w", "conv_b", "dt_bias", "a_log", "d_skip", "ssd_norm", "norm_mem",
         "w_mem_kv", "w_sb_out", "w_ssd_out", "w_mem_out", "w_o", "norm_mix_post", "norm_mlp_pre", "w_up",
         "w_down", "norm_mlp_post"]
LANES = 128


def _pack_rows(vecs):
    parts, offs, off = [], [], 0
    for t in vecs:
        flat = t.reshape(-1)
        n = flat.shape[0]
        rows = -(-n // (8 * LANES)) * 8
        parts.append(jnp.pad(flat, (0, rows * LANES - n)).reshape(rows, LANES))
        offs.append((off, n))
        off += rows
    return jnp.concatenate(parts, axis=0), offs


def _unpack_rows(packed, offs, shapes):
    out = []
    for (off, n), shape in zip(offs, shapes):
        rows = -(-n // (8 * LANES)) * 8
        out.append(packed[off:off + rows].reshape(-1)[:n].reshape(shape))
    return out


def kernel(x, mem, norm_mix_pre, w_in, conv_w, conv_b, dt_bias, a_log, d_skip, ssd_norm, norm_mem, w_mem_kv, w_sb_out, w_ssd_out, w_mem_out, w_o, norm_mix_post, norm_mlp_pre, w_up, w_down, norm_mlp_post, loss_target, m_norm_mix_pre, m_w_in, m_conv_w, m_conv_b, m_dt_bias, m_a_log, m_d_skip, m_ssd_norm, m_norm_mem, m_w_mem_kv, m_w_sb_out, m_w_ssd_out, m_w_mem_out, m_w_o, m_norm_mix_post, m_norm_mlp_pre, m_w_up, m_w_down, m_norm_mlp_post, v_norm_mix_pre, v_w_in, v_conv_w, v_conv_b, v_dt_bias, v_a_log, v_d_skip, v_ssd_norm, v_norm_mem, v_w_mem_kv, v_w_sb_out, v_w_ssd_out, v_w_mem_out, v_w_o, v_norm_mix_post, v_norm_mlp_pre, v_w_up, v_w_down, v_norm_mlp_post):
    wd = dict(norm_mix_pre=norm_mix_pre, w_in=w_in, conv_w=conv_w, conv_b=conv_b, dt_bias=dt_bias, a_log=a_log,
              d_skip=d_skip, ssd_norm=ssd_norm, norm_mem=norm_mem, w_mem_kv=w_mem_kv, w_sb_out=w_sb_out,
              w_ssd_out=w_ssd_out, w_mem_out=w_mem_out, w_o=w_o, norm_mix_post=norm_mix_post,
              norm_mlp_pre=norm_mlp_pre, w_up=w_up, w_down=w_down, norm_mlp_post=norm_mlp_post)
    md = dict(norm_mix_pre=m_norm_mix_pre, w_in=m_w_in, conv_w=m_conv_w, conv_b=m_conv_b, dt_bias=m_dt_bias,
              a_log=m_a_log, d_skip=m_d_skip, ssd_norm=m_ssd_norm, norm_mem=m_norm_mem, w_mem_kv=m_w_mem_kv,
              w_sb_out=m_w_sb_out, w_ssd_out=m_w_ssd_out, w_mem_out=m_w_mem_out, w_o=m_w_o,
              norm_mix_post=m_norm_mix_post, norm_mlp_pre=m_norm_mlp_pre, w_up=m_w_up, w_down=m_w_down,
              norm_mlp_post=m_norm_mlp_post)
    vd = dict(norm_mix_pre=v_norm_mix_pre, w_in=v_w_in, conv_w=v_conv_w, conv_b=v_conv_b, dt_bias=v_dt_bias,
              a_log=v_a_log, d_skip=v_d_skip, ssd_norm=v_ssd_norm, norm_mem=v_norm_mem, w_mem_kv=v_w_mem_kv,
              w_sb_out=v_w_sb_out, w_ssd_out=v_w_ssd_out, w_mem_out=v_w_mem_out, w_o=v_w_o,
              norm_mix_post=v_norm_mix_post, norm_mlp_pre=v_norm_mlp_pre, w_up=v_w_up, w_down=v_w_down,
              norm_mlp_post=v_norm_mlp_post)
    shards = {n: _cast_shard(wd[n][0], "cast_" + n) for n in BIG}
    w_in_g, conv_w_g = _gather_two_level([shards["w_in"], wd["conv_w"][0]], "gather_w_in")
    wt = {"w_in": _full_weight("w_in", w_in_g)}
    ch = conv_w_g.shape[2]

    p = {n: wd[n] for n in SMALL}
    p["conv_w"] = conv_w_g.transpose(1, 0, 2).reshape(CONV_K, N_DEV * ch)
    loss, grad_x, gw, sg, late_received = _local_step(x[0], mem[0], loss_target[0], p, wt,
                                                      [shards[n] for n in LATE_W])

    received = dict(zip(LATE_W, late_received))
    g_in = gw["w_in"]
    by_core = g_in.reshape(g_in.shape[0], N_CHIP, 2, g_in.shape[1] // N_DEV)
    mine = lax.dynamic_index_in_dim(by_core, lax.axis_index("c"), 2, keepdims=False).transpose(1, 0, 2)
    received["w_in"] = _chip_scatter(mine, "scatter_w_in_chips")

    grads, deltas, new_m, new_v = {}, {}, {}, {}
    for n in BIG:
        g, dl, m2, v2 = _adamw_reduce(received[n], wd[n][0], md[n][0], vd[n][0], "adamw_" + n)
        grads[n], deltas[n], new_m[n], new_v[n] = g[None], dl[None], m2[None], v2[None]
    small_names = SMALL + ["conv_w"]
    gcw = sg["conv_w"].reshape(CONV_K, N_DEV, ch).transpose(1, 0, 2)
    small_of = lambda dct: [dct[n] for n in SMALL] + [dct["conv_w"][0]]
    loss_red, g_s, d_s, m_s, v_s = _small_step([sg[n] for n in SMALL], gcw, loss, small_of(wd), small_of(md),
                                               small_of(vd))
    for i, n in enumerate(small_names):
        shape = wd[n].shape
        grads[n], deltas[n], new_m[n], new_v[n] = (t.reshape(shape) for t in (g_s[i], d_s[i], m_s[i], v_s[i]))
    loss_out = loss_red.reshape(())

    return (loss_out, grad_x[None], *[grads[n] for n in ALL_W], *[deltas[n] for n in ALL_W],
            *[new_m[n] for n in ALL_W], *[new_v[n] for n in ALL_W])
```

```python
import functools

import jax
import jax.numpy as jnp
from jax import lax
from jax.experimental import pallas as pl
from jax.experimental.pallas import tpu as pltpu

F32 = jnp.float32
BF16 = jnp.bfloat16
CDT = jnp.bfloat16
EPS = 1e-6
VMEM_LIMIT = 56 * 1024 * 1024

N_DEV = 8
D_MODEL = 1024
SB_H, SB_DH = 16, 64
SSD_G, SSD_R, SSD_P, SSD_N, SSD_L = 4, 8, 64, 128, 128
SSD_H = SSD_G * SSD_R
SSD_INNER = SSD_H * SSD_P
CONV_K = 4
CONV_DIM = SSD_INNER + 2 * SSD_G * SSD_N
MEM_H, MEM_DH = 4, 256
DT_PAD = 128
SB_TQ, SB_BK = 1024, 256
CONV_PAD = 8
MM_TILE, MM_TILE_K = 1024, 2048

ADAM_LR, ADAM_B1, ADAM_B2, ADAM_EPS, ADAM_WD, ADAM_STEP = 0.001, 0.9, 0.999, 1e-08, 0.01, 10

NT = (((1,), (1,)), ((), ()))
TN = (((0,), (0,)), ((), ()))
NN = (((1,), (0,)), ((), ()))


def _cparams(sem=None):
    return pltpu.CompilerParams(dimension_semantics=sem, vmem_limit_bytes=VMEM_LIMIT)


def _pick(n, cands):
    for c in cands:
        if n % c == 0:
            return c
    return n


def _dot(a, b, dims=NN):
    return lax.dot_general(a.astype(CDT), b.astype(CDT), dims, preferred_element_type=F32)


def _split_dot(x, t, left, pieces):
    if CDT == F32:
        return lax.dot_general(t, x, NN, preferred_element_type=F32) if left else \
            lax.dot_general(x, t, NN, preferred_element_type=F32)
    acc = None
    rem = x
    for _ in range(pieces):
        hi = rem.astype(BF16)
        rem = rem - hi.astype(F32)
        d = lax.dot_general(t, hi, NN, preferred_element_type=F32) if left else \
            lax.dot_general(hi, t, NN, preferred_element_type=F32)
        acc = d if acc is None else acc + d
    return acc


def _iota(shape, dim):
    return lax.broadcasted_iota(jnp.int32, shape, dim)


def _sigmoid(x):
    return 1.0 / (1.0 + jnp.exp(-x))


def _silu(x):
    return x * _sigmoid(x)


def _dsilu(x):
    s = _sigmoid(x)
    return s * (1.0 + x * (1.0 - s))


def _softplus(x):
    return jnp.maximum(x, 0.0) + jnp.log(1.0 + jnp.exp(-jnp.abs(x)))


def _rms(x, g):
    r = lax.rsqrt(jnp.mean(x * x, axis=-1, keepdims=True) + EPS)
    return x * r * g


def _rms_bwd(x, g, dy):
    r = lax.rsqrt(jnp.mean(x * x, axis=-1, keepdims=True) + EPS)
    n = x * r
    dn = dy * g
    dx = r * (dn - n * jnp.mean(dn * n, axis=-1, keepdims=True))
    dg = jnp.sum(dy * n, axis=0, keepdims=True)
    return dx, dg


def _mm(a, b, *, ta=False, tb=False, out_dtype=F32, name, b_off=0, n=None, b_koff=0, into=None, into_off=0,
        hosted=None):
    m = a.shape[1] if ta else a.shape[0]
    k = a.shape[0] if ta else a.shape[1]
    if n is None:
        n = b.shape[0] if tb else b.shape[1]
    assert b_koff + k <= (b.shape[1] if tb else b.shape[0])
    bm = _pick(m, (MM_TILE, 512, 256, 128))
    bn = _pick(n, (MM_TILE, 512, 256, 128))
    bk = next(c for c in (MM_TILE_K, 1024, 512, 256, 128, k) if k % c == 0 and b_koff % c == 0)
    nk = k // bk
    assert b_off % bn == 0 and into_off % bn == 0
    jb, jo, kb = b_off // bn, into_off // bn, b_koff // bk
    dims = (((0 if ta else 1,), (1 if tb else 0,)), ((), ()))
    grid = (m // bm, n // bn, nk)
    off = 1 if into is not None else 0
    nh_in = len(hosted.ins) if hosted else 0
    nh_out = len(hosted.shapes) if hosted else 0

    def body(a_ref, b_ref, *rest):
        o_ref = rest[off + nh_in]
        acc_ref = rest[off + nh_in + 1 + nh_out]
        if hosted:
            h_refs = (rest[off:off + nh_in], rest[off + nh_in + 1:off + nh_in + 1 + nh_out],
                      rest[off + nh_in + nh_out + 2:])
            step = (pl.program_id(0) * grid[1] + pl.program_id(1)) * grid[2] + pl.program_id(2)

            @pl.when(step == 0)
            def _():
                for cp in hosted.copies(*h_refs):
                    cp.start()

        part = _dot(a_ref[...], b_ref[...], dims)
        if nk == 1:
            o_ref[...] = part.astype(o_ref.dtype)
        else:
            kk = pl.program_id(2)

            @pl.when(kk == 0)
            def _():
                acc_ref[...] = part

            @pl.when(jnp.logical_and(kk > 0, kk < nk - 1))
            def _():
                acc_ref[...] += part

            @pl.when(kk == nk - 1)
            def _():
                o_ref[...] = (acc_ref[...] + part).astype(o_ref.dtype)

        if hosted:
            @pl.when(step == grid[0] * grid[1] * grid[2] - 1)
            def _():
                for cp in hosted.copies(*h_refs):
                    cp.wait()

    a_spec = pl.BlockSpec((bk, bm), lambda i, j, kk: (kk, i)) if ta else \
        pl.BlockSpec((bm, bk), lambda i, j, kk: (i, kk))
    b_spec = pl.BlockSpec((bn, bk), lambda i, j, kk: (j + jb, kk + kb)) if tb else \
        pl.BlockSpec((bk, bn), lambda i, j, kk: (kk + kb, j + jb))
    extra = {} if into is None else {"input_output_aliases": {2: 0}}
    out_shape = jax.ShapeDtypeStruct((m, n), out_dtype) if into is None else \
        jax.ShapeDtypeStruct(into.shape, into.dtype)
    res = pl.pallas_call(
        body, name=name, grid=grid,
        in_specs=[a_spec, b_spec] + ([] if into is None else [pl.BlockSpec(memory_space=pl.ANY)]) + [HBM] * nh_in,
        out_specs=[pl.BlockSpec((bm, bn), lambda i, j, kk: (i, j + jo))] + [HBM] * nh_out,
        out_shape=[out_shape] + (list(hosted.shapes) if hosted else []),
        scratch_shapes=[pltpu.VMEM((bm, bn) if nk > 1 else (8, 128), F32)] + (list(hosted.sems) if hosted else []),
        compiler_params=_cparams(("arbitrary",) * 3 if hosted else ("parallel", "parallel", "arbitrary")),
        **extra,
    )(*((a, b) if into is None else (a, b, into)), *(hosted.ins if hosted else ()))
    return res if hosted else res[0]


def _rows(fn, tiled, params, outs, accs=(), *, ts, name):
    s = tiled[0].shape[0]
    ts = min(ts, s)
    assert s % ts == 0
    nt, npar, no, na = len(tiled), len(params), len(outs), len(accs)

    def body(*refs):
        i = pl.program_id(0)
        vals = [r[...] for r in refs[:nt + npar]]
        res = fn(*vals)
        if not isinstance(res, (tuple, list)):
            res = (res,)
        orefs = refs[nt + npar:nt + npar + no]
        arefs = refs[nt + npar + no:]
        for r_, val in zip(orefs, res[:no]):
            r_[...] = val.astype(r_.dtype)
        if na:
            @pl.when(i == 0)
            def _():
                for r_ in arefs:
                    r_[...] = jnp.zeros_like(r_)

            for r_, val in zip(arefs, res[no:]):
                r_[...] += jnp.broadcast_to(val, r_.shape)

    in_specs = [pl.BlockSpec((ts, a.shape[1]), lambda i: (i, 0)) for a in tiled]
    in_specs += [pl.BlockSpec(p.shape, lambda i: (0, 0)) for p in params]
    out_specs = [pl.BlockSpec((ts, w), lambda i: (i, 0)) for (w, _) in outs]
    out_specs += [pl.BlockSpec(shape, lambda i: (0, 0)) for shape in accs]
    out_shape = [jax.ShapeDtypeStruct((s, w), dt) for (w, dt) in outs]
    out_shape += [jax.ShapeDtypeStruct(shape, F32) for shape in accs]
    res = pl.pallas_call(
        body, name=name, grid=(s // ts,),
        in_specs=in_specs, out_specs=out_specs, out_shape=out_shape,
        compiler_params=_cparams(("arbitrary",)),
    )(*tiled, *params)
    return res


def _sb_block(qs, kb, diag):
    tq, bk = qs.shape[0], kb.shape[0]
    z = _dot(qs, kb, NT)
    lb = jnp.minimum(z, 0.0) - jnp.log(1.0 + jnp.exp(-jnp.abs(z)))
    lk = lb - z
    if diag is None:
        return None, lb, lk
    causal = (diag + _iota((tq, bk), 1)) < _iota((tq, bk), 0)
    return causal, lb, jnp.where(causal, lk, 0.0)


def _fused_exchange(scatter, ncols, nsteps):
    def hooks(ins, outs, sems):
        step = pl.program_id(0) * ncols + pl.program_id(1)

        @pl.when(step == 0)
        def _():
            for cp in _exchange_copies(ins, outs, *sems, scatter):
                cp.start()

        def finish():
            @pl.when(step == nsteps - 1)
            def _():
                for cp in _exchange_copies(ins, outs, *sems, scatter):
                    cp.wait()
        return finish
    return hooks


def _sb_fwd(q, k, v, comm=()):
    h, s, dh = q.shape
    tq, bk = min(SB_TQ, s), min(SB_BK, s)
    scale = dh ** -0.5
    nc = len(comm)
    hooks = _fused_exchange(False, s // tq, h * (s // tq))

    def body(q_ref, k_ref, v_ref, *rest):
        o_ref, lt_ref = rest[nc:nc + 2]
        if nc:
            finish = hooks(rest[:nc], rest[nc + 2:2 * nc + 2], rest[2 * nc + 2:])
        i = pl.program_id(1)
        q0 = i * tq
        qs = q_ref[...] * scale
        tri = (_iota((bk, bk), 0) > _iota((bk, bk), 1)).astype(CDT)

        def step(k0, carry, diag, r0=0):
            cf, acc = carry
            kb = k_ref[pl.ds(k0, bk), :]
            vb = v_ref[pl.ds(k0, bk), :]
            causal, lb, lk = _sb_block(qs[r0:], kb, diag)
            w = jnp.exp(lb + cf + _split_dot(lk, tri, False, 2))
            if causal is not None:
                w = jnp.where(causal, w, 0.0)
            return cf + jnp.sum(lk, axis=1, keepdims=True), acc + _dot(w, vb)

        carry = (jnp.zeros((tq, 1), F32), jnp.zeros((tq, dh), F32))
        for d in reversed(range(tq // bk)):
            r0 = d * bk
            sub = step(pl.multiple_of(q0 + r0, bk), tuple(t[r0:] for t in carry), 0, r0)
            carry = tuple(jnp.concatenate([t[:r0], u], axis=0) if r0 else u for t, u in zip(carry, sub))
        nfull = q0 // bk
        cf, acc = lax.fori_loop(
            0, nfull, lambda jj, c: step(pl.multiple_of((nfull - 1 - jj) * bk, bk), c, None), carry)
        o_ref[...] = acc
        lt_ref[...] = cf
        if nc:
            finish()

    return pl.pallas_call(
        body, name="sb_fwd", grid=(h, s // tq),
        in_specs=[pl.BlockSpec((None, tq, dh), lambda a, i: (a, i, 0)),
                  pl.BlockSpec((None, s, dh), lambda a, i: (a, 0, 0)),
                  pl.BlockSpec((None, s, dh), lambda a, i: (a, 0, 0))] + [HBM] * nc,
        out_specs=[pl.BlockSpec((None, tq, dh), lambda a, i: (a, i, 0)),
                   pl.BlockSpec((None, tq, 1), lambda a, i: (a, i, 0))] + [HBM] * nc,
        out_shape=[jax.ShapeDtypeStruct((h, s, dh), F32), jax.ShapeDtypeStruct((h, s, 1), F32)]
        + _exchange_shapes(comm, False),
        scratch_shapes=_exchange_sems(nc) if nc else [],
        compiler_params=_cparams(("arbitrary", "arbitrary")),
    )(q, k, v, *comm)


def _sb_bwd(q, k, v, ltot, do, comm=()):
    h, s, dh = q.shape
    tq, bk = min(SB_TQ, s), min(SB_BK, s)
    scale = dh ** -0.5
    nc = len(comm)
    hooks = _fused_exchange(True, s // tq, h * (s // tq))

    def body(q_ref, k_ref, v_ref, lt_ref, do_ref, *rest):
        dq_ref, dk_ref, dv_ref = rest[nc:nc + 3]
        if nc:
            finish = hooks(rest[:nc], rest[nc + 3:2 * nc + 3], rest[2 * nc + 3:])
        i = pl.program_id(1)

        @pl.when(i == 0)
        def _():
            dk_ref[...] = jnp.zeros_like(dk_ref)
            dv_ref[...] = jnp.zeros_like(dv_ref)

        q0 = i * tq
        qs = q_ref[...] * scale
        dov = do_ref[...].astype(CDT)
        ltot = lt_ref[...]
        tri_le = (_iota((bk, bk), 0) <= _iota((bk, bk), 1)).astype(CDT)
        tri_lt = (_iota((bk, bk), 0) < _iota((bk, bk), 1)).astype(CDT)

        def step(k0, carry, diag, r0=0):
            cf, cg, dq = carry
            kb = k_ref[pl.ds(k0, bk), :]
            vb = v_ref[pl.ds(k0, bk), :]
            causal, lb, lk = _sb_block(qs[r0:], kb, diag)
            w = jnp.exp(lb + ltot[r0:] - (cf + _split_dot(lk, tri_le, False, 2)))
            if causal is not None:
                w = jnp.where(causal, w, 0.0)
            g = w * _dot(dov[r0:], vb, NT)
            gsum = cg + _split_dot(g, tri_lt, False, 2)
            dz = g - (g + gsum) * jnp.exp(lb)
            if causal is not None:
                dz = jnp.where(causal, dz, 0.0)
            dzc = dz.astype(CDT)
            dk_ref[pl.ds(k0, bk), :] += _dot(dzc, qs[r0:], TN)
            dv_ref[pl.ds(k0, bk), :] += _dot(w, dov[r0:], TN)
            return (cf + jnp.sum(lk, axis=1, keepdims=True), cg + jnp.sum(g, axis=1, keepdims=True),
                    dq + _dot(dzc, kb))

        carry = (jnp.zeros((tq, 1), F32), jnp.zeros((tq, 1), F32), jnp.zeros((tq, dh), F32))
        carry = lax.fori_loop(0, q0 // bk, lambda jj, c: step(pl.multiple_of(jj * bk, bk), c, None), carry)
        for d in range(tq // bk):
            r0 = d * bk
            sub = step(pl.multiple_of(q0 + r0, bk), tuple(t[r0:] for t in carry), 0, r0)
            carry = tuple(jnp.concatenate([t[:r0], u], axis=0) if r0 else u for t, u in zip(carry, sub))
        dq_ref[...] = carry[2] * scale
        if nc:
            finish()

    tile = pl.BlockSpec((None, tq, dh), lambda a, i: (a, i, 0))
    full = pl.BlockSpec((None, s, dh), lambda a, i: (a, 0, 0))
    shp = jax.ShapeDtypeStruct((h, s, dh), F32)
    return pl.pallas_call(
        body, name="sb_bwd", grid=(h, s // tq),
        in_specs=[tile, full, full, pl.BlockSpec((None, tq, 1), lambda a, i: (a, i, 0)), tile] + [HBM] * nc,
        out_specs=[tile, full, full] + [HBM] * nc,
        out_shape=[shp, shp, shp] + _exchange_shapes(comm, True),
        scratch_shapes=_exchange_sems(nc) if nc else [],
        compiler_params=_cparams(("arbitrary", "arbitrary")),
    )(q, k, v, ltot, do, *comm)


SB_PAIR = 128


def _sb_fwd_pairs(qkv, comm=()):
    s, d3 = qkv.shape
    d = d3 // 3
    npair = d // SB_PAIR
    tq, bk = min(SB_TQ, s), min(SB_BK, s)
    scale = SB_DH ** -0.5
    nc = len(comm)
    hooks = _fused_exchange(False, s // tq, npair * (s // tq))

    def body(q_ref, k_ref, v_ref, *rest):
        y_ref, lt_ref = rest[nc:nc + 2]
        if nc:
            finish = hooks(rest[:nc], rest[nc + 2:2 * nc + 2], rest[2 * nc + 2:])
        i = pl.program_id(1)
        q0 = i * tq
        q2 = q_ref[...] * scale
        lane_head = (_iota((1, SB_PAIR), 1) >= SB_DH).astype(jnp.int32)
        tri = (_iota((bk, bk), 0) > _iota((bk, bk), 1)).astype(CDT)

        def head(hh, y):
            mine = lane_head == hh
            qs = jnp.where(mine, q2, jnp.zeros_like(q2))

            def step(k0, carry, diag, r0=0):
                cf, acc = carry
                kb = k_ref[pl.ds(k0, bk), :]
                vb = v_ref[pl.ds(k0, bk), :]
                causal, lb, lk = _sb_block(qs[r0:], kb, diag)
                w = jnp.exp(lb + cf + _split_dot(lk, tri, False, 2))
                if causal is not None:
                    w = jnp.where(causal, w, 0.0)
                return cf + jnp.sum(lk, axis=1, keepdims=True), acc + _dot(w, vb)

            carry = (jnp.zeros((tq, 1), F32), jnp.zeros((tq, SB_PAIR), F32))
            for dd in reversed(range(tq // bk)):
                r0 = dd * bk
                sub = step(pl.multiple_of(q0 + r0, bk), tuple(t[r0:] for t in carry), 0, r0)
                carry = tuple(jnp.concatenate([t[:r0], u], axis=0) if r0 else u for t, u in zip(carry, sub))
            nfull = q0 // bk
            cf, acc = lax.fori_loop(
                0, nfull, lambda jj, c: step(pl.multiple_of((nfull - 1 - jj) * bk, bk), c, None), carry)
            lt_ref[hh] = cf
            return jnp.where(mine, acc, y)

        y_ref[...] = lax.fori_loop(0, 2, head, jnp.zeros((tq, SB_PAIR), F32)).astype(y_ref.dtype)
        if nc:
            finish()

    return pl.pallas_call(
        body, name="sb_fwd", grid=(npair, s // tq),
        in_specs=[pl.BlockSpec((tq, SB_PAIR), lambda a, i: (i, a)),
                  pl.BlockSpec((s, SB_PAIR), lambda a, i: (0, npair + a)),
                  pl.BlockSpec((s, SB_PAIR), lambda a, i: (0, 2 * npair + a))] + [HBM] * nc,
        out_specs=[pl.BlockSpec((tq, SB_PAIR), lambda a, i: (i, a)),
                   pl.BlockSpec((2, tq, 1), lambda a, i: (a, i, 0))] + [HBM] * nc,
        out_shape=[jax.ShapeDtypeStruct((s, d), CDT), jax.ShapeDtypeStruct((2 * npair, s, 1), F32)]
        + _exchange_shapes(comm, False),
        scratch_shapes=_exchange_sems(nc) if nc else [],
        compiler_params=_cparams(("arbitrary", "arbitrary")),
    )(qkv, qkv, qkv, *comm)


def _sb_bwd_pairs(qkv, ltot, dy, comm=()):
    s, d3 = qkv.shape
    d = d3 // 3
    npair = d // SB_PAIR
    tq, bk = min(SB_TQ, s), min(SB_BK, s)
    scale = SB_DH ** -0.5
    nc = len(comm)
    hooks = _fused_exchange(True, s // tq, npair * (s // tq))

    def body(q_ref, k_ref, v_ref, lt_ref, dy_ref, *rest):
        dq_ref, dk_ref, dv_ref = rest[nc:nc + 3]
        if nc:
            finish = hooks(rest[:nc], rest[nc + 3:2 * nc + 3], rest[2 * nc + 3:])
        i = pl.program_id(1)

        @pl.when(i == 0)
        def _():
            dk_ref[...] = jnp.zeros_like(dk_ref)
            dv_ref[...] = jnp.zeros_like(dv_ref)

        q0 = i * tq
        q2 = q_ref[...] * scale
        do2 = dy_ref[...].astype(CDT)
        lane_head = (_iota((1, SB_PAIR), 1) >= SB_DH).astype(jnp.int32)
        tri_le = (_iota((bk, bk), 0) <= _iota((bk, bk), 1)).astype(CDT)
        tri_lt = (_iota((bk, bk), 0) < _iota((bk, bk), 1)).astype(CDT)

        def head(hh, dq_all):
            mine = lane_head == hh
            qs = jnp.where(mine, q2, jnp.zeros_like(q2))
            dov = jnp.where(mine, do2, jnp.zeros_like(do2))
            ltot_h = lt_ref[hh]

            def step(k0, carry, diag, r0=0):
                cf, cg, dq = carry
                kb = k_ref[pl.ds(k0, bk), :]
                vb = v_ref[pl.ds(k0, bk), :]
                causal, lb, lk = _sb_block(qs[r0:], kb, diag)
                w = jnp.exp(lb + ltot_h[r0:] - (cf + _split_dot(lk, tri_le, False, 2)))
                if causal is not None:
                    w = jnp.where(causal, w, 0.0)
                g = w * _dot(dov[r0:], vb, NT)
                gsum = cg + _split_dot(g, tri_lt, False, 2)
                dz = g - (g + gsum) * jnp.exp(lb)
                if causal is not None:
                    dz = jnp.where(causal, dz, 0.0)
                dzc = dz.astype(CDT)
                dk_ref[pl.ds(k0, bk), :] += _dot(dzc, qs[r0:], TN)
                dv_ref[pl.ds(k0, bk), :] += _dot(w, dov[r0:], TN)
                kbm = jnp.where(mine, kb, jnp.zeros_like(kb))
                return (cf + jnp.sum(lk, axis=1, keepdims=True), cg + jnp.sum(g, axis=1, keepdims=True),
                        dq + _dot(dzc, kbm))

            carry = (jnp.zeros((tq, 1), F32), jnp.zeros((tq, 1), F32), jnp.zeros((tq, SB_PAIR), F32))
            carry = lax.fori_loop(0, q0 // bk, lambda jj, c: step(pl.multiple_of(jj * bk, bk), c, None), carry)
            for dd in range(tq // bk):
                r0 = dd * bk
                sub = step(pl.multiple_of(q0 + r0, bk), tuple(t[r0:] for t in carry), 0, r0)
                carry = tuple(jnp.concatenate([t[:r0], u], axis=0) if r0 else u for t, u in zip(carry, sub))
            return dq_all + carry[2]

        dq_ref[...] = lax.fori_loop(0, 2, head, jnp.zeros((tq, SB_PAIR), F32)) * scale
        if nc:
            finish()

    tile = pl.BlockSpec((tq, SB_PAIR), lambda a, i: (i, a))
    acc = pl.BlockSpec((s, SB_PAIR), lambda a, i: (0, a))
    shp = jax.ShapeDtypeStruct((s, d), F32)
    return pl.pallas_call(
        body, name="sb_bwd", grid=(npair, s // tq),
        in_specs=[tile, pl.BlockSpec((s, SB_PAIR), lambda a, i: (0, npair + a)),
                  pl.BlockSpec((s, SB_PAIR), lambda a, i: (0, 2 * npair + a)),
                  pl.BlockSpec((2, tq, 1), lambda a, i: (a, i, 0)), tile] + [HBM] * nc,
        out_specs=[tile, acc, acc] + [HBM] * nc,
        out_shape=[shp, shp, shp] + _exchange_shapes(comm, True),
        scratch_shapes=_exchange_sems(nc) if nc else [],
        compiler_params=_cparams(("arbitrary", "arbitrary")),
    )(qkv, qkv, qkv, ltot, dy, *comm)


def _pick_lane(tile, r):
    return jnp.sum(jnp.where(_iota(tile.shape, 1) == r, tile, 0.0), axis=1, keepdims=True)


def _pick_row(tile, r):
    return jnp.sum(jnp.where(_iota(tile.shape, 0) == r, tile, 0.0), axis=0, keepdims=True)


def _ssd_chunk_setup(c_ref, b_ref, dac_ref, dar_ref, cb_ref, acsc_ref, acsr_ref):
    l = SSD_L
    tdt = F32 if CDT == F32 else BF16
    lower = (_iota((l, l), 1) <= _iota((l, l), 0)).astype(tdt)
    upper = (_iota((l, l), 0) <= _iota((l, l), 1)).astype(tdt)
    cb_ref[...] = _dot(c_ref[...], b_ref[...], NT)
    acsc_ref[...] = _split_dot(dac_ref[...], lower, True, 3)
    acsr_ref[...] = _split_dot(dar_ref[...], upper, False, 3)


def _ssd_fwd(xh, dtc, dac, dar, dsk, xbc):
    hh, s, p = xh.shape
    l, n, g_, r_ = SSD_L, SSD_N, SSD_G, SSD_R
    nc = s // l
    boff = SSD_INNER // n
    coff = boff + g_

    def body(x_ref, dtc_ref, dac_ref, dar_ref, dsk_ref, b_ref, c_ref, y_ref, st_ref,
             state_ref, cb_ref, acsc_ref, acsr_ref):
        c = pl.program_id(1)
        r = pl.program_id(2)

        @pl.when(r == 0)
        def _():
            _ssd_chunk_setup(c_ref, b_ref, dac_ref, dar_ref, cb_ref, acsc_ref, acsr_ref)

        @pl.when(c == 0)
        def _():
            state_ref[r] = jnp.zeros((n, p), F32)

        a_col = _pick_lane(acsc_ref[...], r)
        a_row = _pick_row(acsr_ref[...], r)
        dt_col = _pick_lane(dtc_ref[...], r)
        dsk_h = _pick_lane(dsk_ref[...], r)
        xv = x_ref[...]
        xd = xv * dt_col
        mask = _iota((l, l), 1) <= _iota((l, l), 0)
        decay = jnp.where(mask, jnp.exp(jnp.minimum(a_col - a_row, 0.0)), 0.0)
        w = cb_ref[...] * decay
        hprev = state_ref[r]
        cv = c_ref[...]
        y = _dot(w, xd) + jnp.exp(a_col) * _dot(cv, hprev)
        y_ref[...] = y + dsk_h * xv
        a_end = a_col[l - 1:l, :]
        dte = jnp.exp(a_end - a_col)
        st_ref[...] = hprev
        state_ref[r] = hprev * jnp.exp(a_end) + _dot(b_ref[...], xd * dte, TN)

    return pl.pallas_call(
        body, name="ssd_fwd", grid=(g_, nc, r_),
        in_specs=[pl.BlockSpec((None, l, p), lambda g, c, r: (g * r_ + r, c, 0)),
                  pl.BlockSpec((None, l, r_), lambda g, c, r: (g, c, 0)),
                  pl.BlockSpec((None, l, r_), lambda g, c, r: (g, c, 0)),
                  pl.BlockSpec((None, r_, l), lambda g, c, r: (g, 0, c)),
                  pl.BlockSpec((None, 1, r_), lambda g, c, r: (g, 0, 0)),
                  pl.BlockSpec((l, n), lambda g, c, r: (c, boff + g)),
                  pl.BlockSpec((l, n), lambda g, c, r: (c, coff + g))],
        out_specs=[pl.BlockSpec((None, l, p), lambda g, c, r: (g * r_ + r, c, 0)),
                   pl.BlockSpec((None, None, n, p), lambda g, c, r: (g * r_ + r, c, 0, 0))],
        out_shape=[jax.ShapeDtypeStruct((hh, s, p), F32),
                   jax.ShapeDtypeStruct((hh, nc, n, p), F32)],
        scratch_shapes=[pltpu.VMEM((r_, n, p), F32), pltpu.VMEM((l, l), F32),
                        pltpu.VMEM((l, r_), F32), pltpu.VMEM((r_, l), F32)],
        compiler_params=_cparams(("parallel", "arbitrary", "arbitrary")),
    )(xh, dtc, dac, dar, dsk, xbc, xbc)


def _ssd_bwd(xh, dtc, dac, dar, dsk, xbc, st, dy):
    hh, s, p = xh.shape
    l, n, g_, r_ = SSD_L, SSD_N, SSD_G, SSD_R
    nc = s // l
    boff = SSD_INNER // n
    coff = boff + g_

    def body(x_ref, dtc_ref, dac_ref, dar_ref, dsk_ref, b_ref, c_ref, st_ref, dy_ref,
             dx_ref, dda_ref, dxx_ref, db_ref, dc_ref, dah_ref, ddsk_ref,
             dstate_ref, cb_ref, acsc_ref, acsr_ref):
        c = pl.program_id(1)
        r = pl.program_id(2)

        @pl.when(r == 0)
        def _():
            _ssd_chunk_setup(c_ref, b_ref, dac_ref, dar_ref, cb_ref, acsc_ref, acsr_ref)
            db_ref[...] = jnp.zeros_like(db_ref)
            dc_ref[...] = jnp.zeros_like(dc_ref)
            dda_ref[...] = jnp.zeros_like(dda_ref)
            dxx_ref[...] = jnp.zeros_like(dxx_ref)

        @pl.when(jnp.logical_and(c == 0, r == 0))
        def _():
            dah_ref[...] = jnp.zeros_like(dah_ref)
            ddsk_ref[...] = jnp.zeros_like(ddsk_ref)

        @pl.when(c == 0)
        def _():
            dstate_ref[r] = jnp.zeros((n, p), F32)

        a_col = _pick_lane(acsc_ref[...], r)
        a_row = _pick_row(acsr_ref[...], r)
        dt_col = _pick_lane(dtc_ref[...], r)
        dsk_h = _pick_lane(dsk_ref[...], r)
        xv = x_ref[...]
        dyv = dy_ref[...]
        xd = xv * dt_col
        il = _iota((l, l), 0)
        isx = _iota((l, l), 1)
        decay = jnp.where(isx <= il, jnp.exp(jnp.minimum(a_col - a_row, 0.0)), 0.0)
        cb = cb_ref[...]
        w = cb * decay
        dhn = dstate_ref[r]
        hc = st_ref[...]
        bv = b_ref[...]
        cv = c_ref[...]
        a_end = a_col[l - 1:l, :]
        ea = jnp.exp(a_col)
        dte = jnp.exp(a_end - a_col)

        dx_state = dte * _dot(bv, dhn)
        dxd = _dot(w, dyv, TN) + dx_state
        md = decay * _dot(dyv, xd, NT)
        m = md * cb
        dc_ref[...] += _dot(md, bv) + ea * _dot(dyv, hc, NT)
        db_ref[...] += _dot(md, cv, TN) + dte * _dot(xd, dhn, NT)
        dstate_ref[r] = jnp.exp(a_end) * dhn + _dot(cv, dyv * ea, TN)

        tdt = F32 if CDT == F32 else BF16
        t1 = (isx >= il).astype(tdt)
        yoff = ea * _dot(cv, hc)
        xdx = jnp.sum(xd * dx_state, axis=1, keepdims=True)
        vec = jnp.sum(dyv * yoff, axis=1, keepdims=True) - xdx
        end_term = jnp.sum(xdx, axis=0, keepdims=True) + \
            jnp.exp(a_end) * jnp.sum(jnp.sum(hc * dhn, axis=1, keepdims=True), axis=0, keepdims=True)
        zmat = _split_dot(m, t1, True, 2)
        span = jnp.sum(jnp.where(isx < il, zmat, 0.0), axis=1, keepdims=True)
        rc = _split_dot(jnp.broadcast_to(vec, (l, 128)), t1, True, 2)[:, :1]
        dda = span + rc + end_term
        dxx = jnp.sum(dxd * xv, axis=1, keepdims=True)

        lane = _iota((l, r_), 1) == r
        dda_ref[...] += jnp.where(lane, dda, 0.0)
        dxx_ref[...] += jnp.where(lane, dxx, 0.0)
        dx_ref[...] = dxd * dt_col + dsk_h * dyv
        lane1 = _iota((1, r_), 1) == r
        dah_ref[...] += jnp.where(lane1, jnp.sum(dda * dt_col, axis=0, keepdims=True), 0.0)
        ddsk_ref[...] += jnp.where(
            lane1, jnp.sum(jnp.sum(dyv * xv, axis=1, keepdims=True), axis=0, keepdims=True), 0.0)

    rev = lambda c: nc - 1 - c
    xspec = pl.BlockSpec((None, l, p), lambda g, c, r: (g * r_ + r, rev(c), 0))
    cspec = pl.BlockSpec((None, l, r_), lambda g, c, r: (g, rev(c), 0))
    hspec = pl.BlockSpec((None, 1, r_), lambda g, c, r: (g, 0, 0))
    return pl.pallas_call(
        body, name="ssd_bwd", grid=(g_, nc, r_),
        in_specs=[xspec, cspec, cspec,
                  pl.BlockSpec((None, r_, l), lambda g, c, r: (g, 0, rev(c))),
                  hspec,
                  pl.BlockSpec((l, n), lambda g, c, r: (rev(c), boff + g)),
                  pl.BlockSpec((l, n), lambda g, c, r: (rev(c), coff + g)),
                  pl.BlockSpec((None, None, n, p), lambda g, c, r: (g * r_ + r, rev(c), 0, 0)),
                  xspec],
        out_specs=[xspec, cspec, cspec,
                   pl.BlockSpec((l, n), lambda g, c, r: (rev(c), g)),
                   pl.BlockSpec((l, n), lambda g, c, r: (rev(c), g)),
                   hspec, hspec],
        out_shape=[jax.ShapeDtypeStruct((hh, s, p), F32),
                   jax.ShapeDtypeStruct((g_, s, r_), F32),
                   jax.ShapeDtypeStruct((g_, s, r_), F32),
                   jax.ShapeDtypeStruct((s, g_ * n), F32),
                   jax.ShapeDtypeStruct((s, g_ * n), F32),
                   jax.ShapeDtypeStruct((g_, 1, r_), F32),
                   jax.ShapeDtypeStruct((g_, 1, r_), F32)],
        scratch_shapes=[pltpu.VMEM((r_, n, p), F32), pltpu.VMEM((l, l), F32),
                        pltpu.VMEM((l, r_), F32), pltpu.VMEM((r_, l), F32)],
        compiler_params=_cparams(("parallel", "arbitrary", "arbitrary")),
    )(xh, dtc, dac, dar, dsk, xbc, xbc, st, dy)


def _ssd_chunk_common(c_ref, b_ref, dac_ref, dar_ref):
    l = SSD_L
    tdt = F32 if CDT == F32 else BF16
    lower = (_iota((l, l), 1) <= _iota((l, l), 0)).astype(tdt)
    upper = (_iota((l, l), 0) <= _iota((l, l), 1)).astype(tdt)
    cb = _dot(c_ref[...], b_ref[...], NT)
    return cb, _split_dot(dac_ref[...], lower, True, 3), _split_dot(dar_ref[...], upper, False, 3)


def _ssd_fwd_g(xh, dtc, dac, dar, dsk, xbc):
    hh, s, p = xh.shape
    l, n, g_, r_ = SSD_L, SSD_N, SSD_G, SSD_R
    nc = s // l
    boff = SSD_INNER // n
    coff = boff + g_

    def body(x_ref, dtc_ref, dac_ref, dar_ref, dsk_ref, b_ref, c_ref, y_ref, st_ref, state_ref):
        c = pl.program_id(1)

        @pl.when(c == 0)
        def _():
            state_ref[...] = jnp.zeros_like(state_ref)

        cb, acs_c, acs_r = _ssd_chunk_common(c_ref, b_ref, dac_ref, dar_ref)
        mask = _iota((l, l), 1) <= _iota((l, l), 0)
        cv, bv = c_ref[...], b_ref[...]
        dtcv, dskv = dtc_ref[...], dsk_ref[...]
        for r in range(r_):
            a_col = _pick_lane(acs_c, r)
            a_row = _pick_row(acs_r, r)
            dt_col = _pick_lane(dtcv, r)
            dsk_h = _pick_lane(dskv, r)
            xv = x_ref[r]
            xd = xv * dt_col
            decay = jnp.where(mask, jnp.exp(jnp.minimum(a_col - a_row, 0.0)), 0.0)
            hprev = state_ref[r]
            y = _dot(cb * decay, xd) + jnp.exp(a_col) * _dot(cv, hprev)
            y_ref[r] = y + dsk_h * xv
            a_end = a_col[l - 1:l, :]
            st_ref[r] = hprev
            state_ref[r] = hprev * jnp.exp(a_end) + _dot(bv, xd * jnp.exp(a_end - a_col), TN)

    return pl.pallas_call(
        body, name="ssd_fwd", grid=(g_, nc),
        in_specs=[pl.BlockSpec((r_, l, p), lambda g, c: (g, c, 0)),
                  pl.BlockSpec((None, l, r_), lambda g, c: (g, c, 0)),
                  pl.BlockSpec((None, l, r_), lambda g, c: (g, c, 0)),
                  pl.BlockSpec((None, r_, l), lambda g, c: (g, 0, c)),
                  pl.BlockSpec((None, 1, r_), lambda g, c: (g, 0, 0)),
                  pl.BlockSpec((l, n), lambda g, c: (c, boff + g)),
                  pl.BlockSpec((l, n), lambda g, c: (c, coff + g))],
        out_specs=[pl.BlockSpec((r_, l, p), lambda g, c: (g, c, 0)),
                   pl.BlockSpec((r_, None, n, p), lambda g, c: (g, c, 0, 0))],
        out_shape=[jax.ShapeDtypeStruct((hh, s, p), F32),
                   jax.ShapeDtypeStruct((hh, nc, n, p), F32)],
        scratch_shapes=[pltpu.VMEM((r_, n, p), F32)],
        compiler_params=_cparams(("parallel", "arbitrary")),
    )(xh, dtc, dac, dar, dsk, xbc, xbc)


def _ssd_bwd_g(xh, dtc, dac, dar, dsk, xbc, st, dy):
    hh, s, p = xh.shape
    l, n, g_, r_ = SSD_L, SSD_N, SSD_G, SSD_R
    nc = s // l
    boff = SSD_INNER // n
    coff = boff + g_

    def body(x_ref, dtc_ref, dac_ref, dar_ref, dsk_ref, b_ref, c_ref, st_ref, dy_ref,
             dx_ref, dda_ref, dxx_ref, db_ref, dc_ref, dah_ref, ddsk_ref, dstate_ref):
        c = pl.program_id(1)

        @pl.when(c == 0)
        def _():
            dstate_ref[...] = jnp.zeros_like(dstate_ref)
            dah_ref[...] = jnp.zeros_like(dah_ref)
            ddsk_ref[...] = jnp.zeros_like(ddsk_ref)

        cb, acs_c, acs_r = _ssd_chunk_common(c_ref, b_ref, dac_ref, dar_ref)
        il = _iota((l, l), 0)
        isx = _iota((l, l), 1)
        tdt = F32 if CDT == F32 else BF16
        t1 = (isx >= il).astype(tdt)
        cv, bv = c_ref[...], b_ref[...]
        dtcv, dskv = dtc_ref[...], dsk_ref[...]
        lane = _iota((l, r_), 1)
        lane1 = _iota((1, r_), 1)
        dda_all = jnp.zeros((l, r_), F32)
        dxx_all = jnp.zeros((l, r_), F32)
        dah_all = jnp.zeros((1, r_), F32)
        ddsk_all = jnp.zeros((1, r_), F32)
        db_acc = jnp.zeros((l, n), F32)
        dc_acc = jnp.zeros((l, n), F32)
        for r in range(r_):
            a_col = _pick_lane(acs_c, r)
            a_row = _pick_row(acs_r, r)
            dt_col = _pick_lane(dtcv, r)
            dsk_h = _pick_lane(dskv, r)
            xv = x_ref[r]
            dyv = dy_ref[r]
            xd = xv * dt_col
            decay = jnp.where(isx <= il, jnp.exp(jnp.minimum(a_col - a_row, 0.0)), 0.0)
            dhn = dstate_ref[r]
            hc = st_ref[r]
            a_end = a_col[l - 1:l, :]
            ea = jnp.exp(a_col)
            dte = jnp.exp(a_end - a_col)

            dx_state = dte * _dot(bv, dhn)
            dxd = _dot(cb * decay, dyv, TN) + dx_state
            md = decay * _dot(dyv, xd, NT)
            dc_acc = dc_acc + _dot(md, bv) + ea * _dot(dyv, hc, NT)
            db_acc = db_acc + _dot(md, cv, TN) + dte * _dot(xd, dhn, NT)
            dstate_ref[r] = jnp.exp(a_end) * dhn + _dot(cv, dyv * ea, TN)

            yoff = ea * _dot(cv, hc)
            xdx = jnp.sum(xd * dx_state, axis=1, keepdims=True)
            vec = jnp.sum(dyv * yoff, axis=1, keepdims=True) - xdx
            end_term = jnp.sum(xdx, axis=0, keepdims=True) + \
                jnp.exp(a_end) * jnp.sum(jnp.sum(hc * dhn, axis=1, keepdims=True), axis=0, keepdims=True)
            zmat = _split_dot(md * cb, t1, True, 2)
            span = jnp.sum(jnp.where(isx < il, zmat, 0.0), axis=1, keepdims=True)
            rc = _split_dot(jnp.broadcast_to(vec, (l, 128)), t1, True, 2)[:, :1]
            dda = span + rc + end_term
            dda_all = jnp.where(lane == r, dda, dda_all)
            dxx_all = jnp.where(lane == r, jnp.sum(dxd * xv, axis=1, keepdims=True), dxx_all)
            dx_ref[r] = dxd * dt_col + dsk_h * dyv
            dah_all = jnp.where(lane1 == r, jnp.sum(dda * dt_col, axis=0, keepdims=True), dah_all)
            ddsk_all = jnp.where(
                lane1 == r, jnp.sum(jnp.sum(dyv * xv, axis=1, keepdims=True), axis=0, keepdims=True), ddsk_all)
        dda_ref[...] = dda_all
        dxx_ref[...] = dxx_all
        db_ref[...] = db_acc
        dc_ref[...] = dc_acc
        dah_ref[...] += dah_all
        ddsk_ref[...] += ddsk_all

    rev = lambda c: nc - 1 - c
    xspec = pl.BlockSpec((r_, l, p), lambda g, c: (g, rev(c), 0))
    cspec = pl.BlockSpec((None, l, r_), lambda g, c: (g, rev(c), 0))
    hspec = pl.BlockSpec((None, 1, r_), lambda g, c: (g, 0, 0))
    return pl.pallas_call(
        body, name="ssd_bwd", grid=(g_, nc),
        in_specs=[xspec, cspec, cspec,
                  pl.BlockSpec((None, r_, l), lambda g, c: (g, 0, rev(c))),
                  hspec,
                  pl.BlockSpec((l, n), lambda g, c: (rev(c), boff + g)),
                  pl.BlockSpec((l, n), lambda g, c: (rev(c), coff + g)),
                  pl.BlockSpec((r_, None, n, p), lambda g, c: (g, rev(c), 0, 0)),
                  xspec],
        out_specs=[xspec, cspec, cspec,
                   pl.BlockSpec((l, n), lambda g, c: (rev(c), g)),
                   pl.BlockSpec((l, n), lambda g, c: (rev(c), g)),
                   hspec, hspec],
        out_shape=[jax.ShapeDtypeStruct((hh, s, p), F32),
                   jax.ShapeDtypeStruct((g_, s, r_), F32),
                   jax.ShapeDtypeStruct((g_, s, r_), F32),
                   jax.ShapeDtypeStruct((s, g_ * n), F32),
                   jax.ShapeDtypeStruct((s, g_ * n), F32),
                   jax.ShapeDtypeStruct((g_, 1, r_), F32),
                   jax.ShapeDtypeStruct((g_, 1, r_), F32)],
        scratch_shapes=[pltpu.VMEM((r_, n, p), F32)],
        compiler_params=_cparams(("parallel", "arbitrary")),
    )(xh, dtc, dac, dar, dsk, xbc, xbc, st, dy)


CONV_TC = 256
CONV_RC = 512


def _conv_taps(x_ref, head_ref, rc):
    base = CONV_PAD - (CONV_K - 1)
    head_ref[pl.ds(0, CONV_PAD), :] = jnp.zeros((CONV_PAD, head_ref.shape[1]), F32)
    head_ref[pl.ds(CONV_PAD, rc), :] = x_ref[pl.ds(0, rc), :]

    def tap(t0, kk):
        if t0 == 0:
            return head_ref[pl.ds(base + kk, rc), :]
        return x_ref[pl.ds(t0 - (CONV_K - 1) + kk, rc), :]
    return tap


def _conv_fwd(x, w, b):
    s, ch = x.shape
    rc = min(CONV_RC, s)

    def body(x_ref, w_ref, b_ref, pre_ref, act_ref, head_ref):
        wv = w_ref[...]
        tap = _conv_taps(x_ref, head_ref, rc)
        for t0 in range(0, s, rc):
            acc = jnp.broadcast_to(b_ref[...], (rc, CONV_TC))
            for kk in range(CONV_K):
                acc = acc + wv[kk:kk + 1, :] * tap(t0, kk)
            pre_ref[pl.ds(t0, rc), :] = acc
            act_ref[pl.ds(t0, rc), :] = _silu(acc)

    col = pl.BlockSpec((s, CONV_TC), lambda j: (0, j))
    shp = jax.ShapeDtypeStruct((s, ch), F32)
    return pl.pallas_call(
        body, name="conv_fwd", grid=(ch // CONV_TC,),
        in_specs=[col, pl.BlockSpec((CONV_K, CONV_TC), lambda j: (0, j)),
                  pl.BlockSpec((1, CONV_TC), lambda j: (0, j))],
        out_specs=[col, col],
        out_shape=[shp, shp],
        scratch_shapes=[pltpu.VMEM((CONV_PAD + rc, CONV_TC), F32)],
        compiler_params=_cparams(("parallel",)),
    )(x, w, b)


def _conv_bwd(x, pre, dact, w):
    s, ch = x.shape
    rc = min(CONV_RC, s)

    def body(x_ref, pre_ref, da_ref, w_ref, dx_ref, dw_ref, db_ref, dpre_ref, head_ref):
        wv = w_ref[...]
        tap = _conv_taps(x_ref, head_ref, rc)
        for t0 in range(0, s, rc):
            dpre_ref[pl.ds(t0, rc), :] = da_ref[pl.ds(t0, rc), :] * _dsilu(pre_ref[pl.ds(t0, rc), :])
        dpre_ref[pl.ds(s, CONV_PAD), :] = jnp.zeros((CONV_PAD, CONV_TC), F32)
        dws = [jnp.zeros((1, CONV_TC), F32) for _ in range(CONV_K)]
        dbs = jnp.zeros((1, CONV_TC), F32)
        for t0 in range(0, s, rc):
            acc = jnp.zeros((rc, CONV_TC), F32)
            dp = dpre_ref[pl.ds(t0, rc), :]
            for kk in range(CONV_K):
                acc = acc + wv[kk:kk + 1, :] * dpre_ref[pl.ds(t0 + CONV_K - 1 - kk, rc), :]
                dws[kk] = dws[kk] + jnp.sum(dp * tap(t0, kk), axis=0, keepdims=True)
            dbs = dbs + jnp.sum(dp, axis=0, keepdims=True)
            dx_ref[pl.ds(t0, rc), :] = acc.astype(dx_ref.dtype)
        for kk in range(CONV_K):
            dw_ref[kk:kk + 1, :] = dws[kk]
        db_ref[...] = dbs

    col = pl.BlockSpec((s, CONV_TC), lambda j: (0, j))
    return pl.pallas_call(
        body, name="conv_bwd", grid=(ch // CONV_TC,),
        in_specs=[col, col, col, pl.BlockSpec((CONV_K, CONV_TC), lambda j: (0, j))],
        out_specs=[col, pl.BlockSpec((CONV_K, CONV_TC), lambda j: (0, j)),
                   pl.BlockSpec((1, CONV_TC), lambda j: (0, j))],
        out_shape=[jax.ShapeDtypeStruct((s, ch), CDT),
                   jax.ShapeDtypeStruct((CONV_K, ch), F32),
                   jax.ShapeDtypeStruct((1, ch), F32)],
        scratch_shapes=[pltpu.VMEM((s + CONV_PAD, CONV_TC), F32), pltpu.VMEM((CONV_PAD + rc, CONV_TC), F32)],
        compiler_params=_cparams(("parallel",)),
    )(x, pre, dact, w)


MEM_TS = 512


def _mem_fwd(mq, kv):
    s = mq.shape[0]
    m = kv.shape[0]
    ts = min(MEM_TS, s)
    scale = MEM_DH ** -0.5

    def body(q_ref, k_ref, v_ref, o_ref):
        sc = _dot(q_ref[...], k_ref[...], NT) * scale
        e = jnp.exp(sc - jnp.max(sc, axis=1, keepdims=True))
        pr = e / jnp.sum(e, axis=1, keepdims=True)
        o_ref[...] = _dot(pr, v_ref[...]).astype(o_ref.dtype)

    return pl.pallas_call(
        body, name="mem_fwd", grid=(MEM_H, s // ts),
        in_specs=[pl.BlockSpec((ts, MEM_DH), lambda a, i: (i, a)),
                  pl.BlockSpec((m, MEM_DH), lambda a, i: (0, a)),
                  pl.BlockSpec((m, MEM_DH), lambda a, i: (0, MEM_H + a))],
        out_specs=pl.BlockSpec((ts, MEM_DH), lambda a, i: (i, a)),
        out_shape=jax.ShapeDtypeStruct((s, MEM_H * MEM_DH), CDT),
        compiler_params=_cparams(("parallel", "arbitrary")),
    )(mq, kv, kv)


def _mem_bwd(mq, kv, do):
    s = mq.shape[0]
    m = kv.shape[0]
    ts = min(MEM_TS, s)
    scale = MEM_DH ** -0.5

    def body(q_ref, k_ref, v_ref, do_ref, dq_ref, dk_ref, dv_ref):
        i = pl.program_id(1)

        @pl.when(i == 0)
        def _():
            dk_ref[...] = jnp.zeros_like(dk_ref)
            dv_ref[...] = jnp.zeros_like(dv_ref)

        qv, kb, vb, dov = q_ref[...], k_ref[...], v_ref[...], do_ref[...]
        sc = _dot(qv, kb, NT) * scale
        e = jnp.exp(sc - jnp.max(sc, axis=1, keepdims=True))
        pr = e / jnp.sum(e, axis=1, keepdims=True)
        dp = _dot(dov, vb, NT)
        ds = pr * (dp - jnp.sum(dp * pr, axis=1, keepdims=True)) * scale
        dq_ref[...] = _dot(ds, kb).astype(dq_ref.dtype)
        dk_ref[...] += _dot(ds, qv, TN)
        dv_ref[...] += _dot(pr, dov, TN)

    tile = pl.BlockSpec((ts, MEM_DH), lambda a, i: (i, a))
    kvo = pl.BlockSpec((m, MEM_DH), lambda a, i: (0, a))
    return pl.pallas_call(
        body, name="mem_bwd", grid=(MEM_H, s // ts),
        in_specs=[tile, kvo, pl.BlockSpec((m, MEM_DH), lambda a, i: (0, MEM_H + a)), tile],
        out_specs=[tile, kvo, kvo],
        out_shape=[jax.ShapeDtypeStruct((s, MEM_H * MEM_DH), CDT),
                   jax.ShapeDtypeStruct((m, MEM_H * MEM_DH), F32),
                   jax.ShapeDtypeStruct((m, MEM_H * MEM_DH), F32)],
        compiler_params=_cparams(("parallel", "arbitrary")),
    )(mq, kv, kv, do)


def _heads(t, nh, dh):
    return t.reshape(t.shape[0], nh, dh).transpose(1, 0, 2)


def _unheads(t):
    return t.transpose(1, 0, 2).reshape(t.shape[1], t.shape[0] * t.shape[2])


def _group_cols(t):
    return t.reshape(t.shape[0], SSD_G, SSD_R).transpose(1, 0, 2)


def _pad_cols(t, width):
    return jnp.pad(t, ((0, 0), (0, width - t.shape[1])))


def _full_weight(name, gathered):
    if name in COL_SHARDED:
        return gathered.transpose(1, 0, 2).reshape(gathered.shape[1], N_DEV * gathered.shape[2])
    return gathered.reshape(N_DEV * gathered.shape[1], gathered.shape[2])


def _grad_payload(name, g):
    if name in COL_SHARDED:
        return g.reshape(g.shape[0], N_DEV, g.shape[1] // N_DEV).transpose(1, 0, 2)
    return g.reshape(N_DEV, g.shape[0] // N_DEV, g.shape[1])


def _local_step(x, mem, tgt, p, wt, shards=None):
    s, d = x.shape
    wt = dict(wt)
    c1, c2, c3, c4, c5 = 3 * d, 3 * d + SSD_INNER, 3 * d + SSD_INNER + CONV_DIM, \
        3 * d + SSD_INNER + CONV_DIM + SSD_H, 3 * d + SSD_INNER + CONV_DIM + SSD_H + d
    w_in = wt["w_in"]
    w_main = jnp.concatenate([w_in[:, :c3], w_in[:, c4:]], axis=1)
    w_dt = _pad_cols(w_in[:, c3:c4], DT_PAD)
    seg_name = ["qkv", "z", "xbc", "mq", "gl"]
    seg_dtype = [CDT, F32, F32, CDT, F32]
    seg_off = [0, c1, c2, c3, c3 + d]
    seg_n = [c1, c2 - c1, c3 - c2, d, 3 * d]

    u = _rows(lambda xv, g: _rms(xv, g), [x], [p["norm_mix_pre"]], [(d, CDT)], ts=512, name="f_norm_pre")[0]
    qkv, z, xbc_raw, mq, gl = [
        _mm(u, w_main, b_off=seg_off[i], n=seg_n[i], out_dtype=seg_dtype[i], name="f_in_" + seg_name[i])
        for i in range(5)]
    dt_raw = _mm(u, w_dt, name="f_in_dt")

    bias128 = _pad_cols(p["dt_bias"], DT_PAD)
    alog128 = _pad_cols(p["a_log"], DT_PAD)

    def dt_fn(dtr, bias, alog):
        dt = _softplus(dtr + bias)
        return dt, dt * (-jnp.exp(alog))

    dt128, da128 = _rows(dt_fn, [dt_raw], [bias128, alog128], [(DT_PAD, F32), (DT_PAD, F32)],
                         ts=512, name="f_dt")
    dtc = _group_cols(dt128[:, :SSD_H])
    dac = _group_cols(da128[:, :SSD_H])
    dar = dac.transpose(0, 2, 1)
    dsk = p["d_skip"].reshape(SSD_G, 1, SSD_R)

    conv_w, conv_b = p["conv_w"], p["conv_b"]
    pre, xbc = _conv_fwd(xbc_raw, conv_w, conv_b)
    xh = _heads(xbc[:, :SSD_INNER], SSD_H, SSD_P)
    y_h, st = _ssd_fwd_g(xh, dtc, dac, dar, dsk, xbc)
    y_core = _unheads(y_h)

    def group_norm_fwd(yv, zv, wn):
        y2 = yv * _silu(zv)
        gw = SSD_INNER // SSD_G
        outs = []
        for gi in range(SSD_G):
            seg = y2[:, gi * gw:(gi + 1) * gw]
            outs.append(_rms(seg, wn[:, gi * gw:(gi + 1) * gw]))
        return jnp.concatenate(outs, axis=1)

    y_ssd = _rows(group_norm_fwd, [y_core, z], [p["ssd_norm"]], [(SSD_INNER, CDT)], ts=256, name="f_ssd_post")[0]

    y_sb, lt_h, *late = _sb_fwd_pairs(qkv, tuple(shards) if shards is not None else ())
    for n, gth in zip(LATE_W, late):
        wt[n] = _full_weight(n, gth)

    mu = _rows(lambda mv, g: _rms(mv, g), [mem], [p["norm_mem"]], [(d, CDT)], ts=256, name="f_norm_mem")[0]
    kv = _mm(mu, wt["w_mem_kv"], out_dtype=CDT, name="f_mem_kv")
    y_mem = _mem_fwd(mq, kv)

    p_sb = _mm(y_sb, wt["w_sb_out"], name="f_sb_out")
    p_ssd = _mm(y_ssd, wt["w_ssd_out"], name="f_ssd_out")
    p_mem = _mm(y_mem, wt["w_mem_out"], name="f_mem_out")

    def merge_fn(glv, a, b, c):
        return (_sigmoid(glv[:, :d]) * a + _sigmoid(glv[:, d:2 * d]) * b + _sigmoid(glv[:, 2 * d:]) * c)

    merged = _rows(merge_fn, [gl, p_sb, p_ssd, p_mem], [], [(d, CDT)], ts=256, name="f_merge")[0]
    mix = _mm(merged, wt["w_o"], name="f_w_o")

    def mid_fn(xv, mixv, g_post, g_pre):
        h1 = xv + _rms(mixv, g_post)
        return h1, _rms(h1, g_pre)

    h1, u2 = _rows(mid_fn, [x, mix], [p["norm_mix_post"], p["norm_mlp_pre"]], [(d, F32), (d, CDT)],
                   ts=512, name="f_mid")
    a1 = _mm(u2, wt["w_up"], name="f_up")
    act = _rows(lambda a: jnp.square(jnp.maximum(a, 0.0)), [a1], [], [(a1.shape[1], CDT)], ts=256, name="f_act")[0]
    ff = _mm(act, wt["w_down"], name="f_down")

    def loss_fn(h1v, ffv, tv, g):
        diff = h1v + _rms(ffv, g) - tv
        tot = jnp.sum(jnp.sum(diff * diff, axis=1, keepdims=True), axis=0, keepdims=True)
        return diff * (1.0 / d), tot

    dh2, loss_acc = _rows(loss_fn, [h1, ff, tgt], [p["norm_mlp_post"]], [(d, F32)], [(1, 128)],
                          ts=512, name="f_loss")
    loss = loss_acc[:, :1] * (0.5 / d)

    sg = {}

    def b_post(ffv, dyv, g):
        dx, dg = _rms_bwd(ffv, g, dyv)
        return dx, dg

    d_ff, sg["norm_mlp_post"] = _rows(b_post, [ff, dh2], [p["norm_mlp_post"]], [(d, CDT)], [(1, d)],
                                      ts=512, name="b_norm_mlp_post")
    dact = _mm(d_ff, wt["w_down"], tb=True, name="b_down_x")
    gw = {"w_down": _mm(act, d_ff, ta=True, name="b_down_w")}
    da1 = _rows(lambda dv, a: dv * 2.0 * jnp.maximum(a, 0.0), [dact, a1], [], [(a1.shape[1], CDT)],
                ts=256, name="b_act")[0]
    du2 = _mm(da1, wt["w_up"], tb=True, name="b_up_x")
    gw["w_up"] = _mm(u2, da1, ta=True, name="b_up_w")

    def b_mid(h1v, du2v, dh2v, mixv, g_pre, g_post):
        dxa, dga = _rms_bwd(h1v, g_pre, du2v)
        dh1 = dh2v + dxa
        dmix, dgb = _rms_bwd(mixv, g_post, dh1)
        return dh1, dmix, dga, dgb

    dh1, dmix, sg["norm_mlp_pre"], sg["norm_mix_post"] = _rows(
        b_mid, [h1, du2, dh2, mix], [p["norm_mlp_pre"], p["norm_mix_post"]],
        [(d, F32), (d, CDT)], [(1, d), (1, d)], ts=256, name="b_mid")
    dmerged = _mm(dmix, wt["w_o"], tb=True, name="b_w_o_x")
    gw["w_o"] = _mm(merged, dmix, ta=True, name="b_w_o_w")

    def b_merge(dm, glv, a, b, c):
        outs, dgl = [], []
        for i, br in enumerate((a, b, c)):
            gt = _sigmoid(glv[:, i * d:(i + 1) * d])
            outs.append(gt * dm)
            dgl.append(dm * br * gt * (1.0 - gt))
        return outs[0], outs[1], outs[2], jnp.concatenate(dgl, axis=1)

    dp_sb, dp_ssd, dp_mem, dgl = _rows(b_merge, [dmerged, gl, p_sb, p_ssd, p_mem], [],
                                       [(d, CDT), (d, CDT), (d, CDT), (3 * d, CDT)], ts=256, name="b_merge")
    dy_sb = _mm(dp_sb, wt["w_sb_out"], tb=True, name="b_sb_out_x")
    gw["w_sb_out"] = _mm(y_sb, dp_sb, ta=True, name="b_sb_out_w")
    dy_ssd = _mm(dp_ssd, wt["w_ssd_out"], tb=True, name="b_ssd_out_x")
    gw["w_ssd_out"] = _mm(y_ssd, dp_ssd, ta=True, name="b_ssd_out_w")
    dy_mem = _mm(dp_mem, wt["w_mem_out"], tb=True, out_dtype=CDT, name="b_mem_out_x")
    gw["w_mem_out"] = _mm(y_mem, dp_mem, ta=True, name="b_mem_out_w")

    dmq, dk_m, dv_m = _mem_bwd(mq, kv, dy_mem)
    dkv = jnp.concatenate([dk_m, dv_m], axis=1).astype(CDT)
    gw["w_mem_kv"] = _mm(mu, dkv, ta=True, name="b_mem_kv_w")
    dmu = _mm(dkv, wt["w_mem_kv"], tb=True, name="b_mem_kv_x")
    sg["norm_mem"] = _rows(lambda mv, dv, g: _rms_bwd(mv, g, dv)[1], [mem, dmu], [p["norm_mem"]], [], [(1, d)],
                           ts=256, name="b_norm_mem")[0]

    payloads = tuple(_grad_payload(n, gw[n]) for n in LATE_W) if shards is not None else ()
    dq, dk, dv, *received = _sb_bwd_pairs(qkv, lt_h, dy_sb, payloads)
    dqkv = jnp.concatenate([dq, dk, dv], axis=1).astype(CDT)

    def group_norm_bwd(dyo, yv, zv, wn):
        sz = _silu(zv)
        y2 = yv * sz
        gw_ = SSD_INNER // SSD_G
        dy2, dwn = [], []
        for gi in range(SSD_G):
            sl = slice(gi * gw_, (gi + 1) * gw_)
            dxs, dgs = _rms_bwd(y2[:, sl], wn[:, sl], dyo[:, sl])
            dy2.append(dxs)
            dwn.append(dgs)
        dy2 = jnp.concatenate(dy2, axis=1)
        return dy2 * sz, dy2 * yv * _dsilu(zv), jnp.concatenate(dwn, axis=1)

    dy_core, dz, sg["ssd_norm"] = _rows(group_norm_bwd, [dy_ssd, y_core, z], [p["ssd_norm"]],
                                        [(SSD_INNER, F32), (SSD_INNER, CDT)], [(1, SSD_INNER)],
                                        ts=256, name="b_ssd_post")
    dxh, dda, dxx, d_b, d_c, dah, ddsk = _ssd_bwd_g(xh, dtc, dac, dar, dsk, xbc, st, _heads(dy_core, SSD_H, SSD_P))
    sg["d_skip"] = ddsk.reshape(1, SSD_H)
    sg["a_log"] = dah.reshape(1, SSD_H) * (-jnp.exp(p["a_log"]))

    def b_dt(ddav, dxxv, dtr, bias, alog):
        ddt = ddav * (-jnp.exp(alog)) + dxxv
        draw = ddt * _sigmoid(dtr + bias)
        return draw, jnp.sum(draw, axis=0, keepdims=True)

    ungroup = lambda t: _pad_cols(t.transpose(1, 0, 2).reshape(s, SSD_H), DT_PAD)
    ddt_raw, dbias128 = _rows(b_dt, [ungroup(dda), ungroup(dxx), dt_raw], [bias128, alog128],
                              [(DT_PAD, CDT)], [(1, DT_PAD)], ts=512, name="b_dt")
    sg["dt_bias"] = dbias128[:, :SSD_H]

    dxbc = jnp.concatenate([_unheads(dxh), d_b, d_c], axis=1)
    dxbc_raw, sg["conv_w"], sg["conv_b"] = _conv_bwd(xbc_raw, pre, dxbc, conv_w)

    dseg = [dqkv, dz, dxbc_raw, dmq, dgl]
    dw_main = lax.empty((d, w_main.shape[1]), F32)
    for i in range(5):
        dw_main = _mm(u, dseg[i], ta=True, into=dw_main, into_off=seg_off[i], name="b_in_w_" + seg_name[i])
    dw_dt = _mm(u, ddt_raw, ta=True, name="b_in_w_dt")
    half = w_main.shape[1] // 2
    hosts = {0: _pair_cols([(dw_main, 0, half), (dw_dt, 0, DT_PAD)]),
             2: _pair_cols([(dw_main, half, half)])} if shards is not None else {}
    dus, from_sibling = [], []
    for i in range(5):
        res = _mm(dseg[i], w_main, tb=True, b_koff=seg_off[i], name="b_in_x_" + seg_name[i], hosted=hosts.get(i))
        if i in hosts:
            dus.append(res[0])
            from_sibling += res[1:]
        else:
            dus.append(res)
    dus.append(_mm(ddt_raw, w_dt, tb=True, name="b_in_x_dt"))
    if shards is None:
        gw["w_in"] = jnp.concatenate([dw_main[:, :c3], dw_dt[:, :SSD_H], dw_main[:, c3:]], axis=1)
    else:
        sum_a = _add_cols(dw_main, 0, from_sibling[0], "sum_w_in_a")
        sum_dt = _add_cols(dw_dt, 0, from_sibling[1], "sum_w_in_dt")
        sum_b = _add_cols(dw_main, half, from_sibling[2], "sum_w_in_b")
        gw["w_in"] = jnp.concatenate([sum_a, sum_b[:, :c3 - half], sum_dt[:, :SSD_H], sum_b[:, c3 - half:]], axis=1)

    def b_pre(xv, dh1v, d0, d1, d2, d3, d4, d5, g):
        dx, dg = _rms_bwd(xv, g, d0 + d1 + d2 + d3 + d4 + d5)
        return dh1v + dx, dg

    grad_x, sg["norm_mix_pre"] = _rows(b_pre, [x, dh1] + dus, [p["norm_mix_pre"]], [(d, F32)], [(1, d)],
                                       ts=256, name="b_norm_pre")
    return loss, grad_x, gw, sg, (received if shards is not None else None)


HBM = pl.BlockSpec(memory_space=pltpu.HBM)
MESH = pl.DeviceIdType.MESH


def _me_and_peers():
    x, y, c = lax.axis_index("x"), lax.axis_index("y"), lax.axis_index("c")
    me = 4 * x + 2 * y + c
    peers = [(x, y, 1 - c), (1 - x, y, c), (x, 1 - y, c), (1 - x, 1 - y, c),
             (1 - x, y, 1 - c), (x, 1 - y, 1 - c), (1 - x, 1 - y, 1 - c)]
    return me, peers


def _peer_index(peer):
    return 4 * peer[0] + 2 * peer[1] + peer[2]


def _exchange_copies(ins, outs, send_sems, recv_sems, local_sems, scatter):
    me, peers = _me_and_peers()
    copies = []
    for a in range(len(ins)):
        own = ins[a].at[me] if scatter else ins[a]
        copies.append(pltpu.make_async_copy(own, outs[a].at[me], local_sems.at[a]))
        for kk, peer in enumerate(peers):
            src = ins[a].at[_peer_index(peer)] if scatter else ins[a]
            copies.append(pltpu.make_async_remote_copy(
                src_ref=src, dst_ref=outs[a].at[me],
                send_sem=send_sems.at[a, kk], recv_sem=recv_sems.at[a, kk],
                device_id=peer, device_id_type=MESH))
    return copies


def _exchange_shapes(ins, scatter):
    return [jax.ShapeDtypeStruct(t.shape if scatter else (N_DEV,) + t.shape, t.dtype) for t in ins]


def _exchange_sems(n):
    return [pltpu.SemaphoreType.DMA((n, N_DEV - 1)), pltpu.SemaphoreType.DMA((n, N_DEV - 1)),
            pltpu.SemaphoreType.DMA((n,))]


def _exchange(ins, scatter, name):
    n = len(ins)

    def body(*refs):
        copies = _exchange_copies(refs[:n], refs[n:2 * n], *refs[2 * n:], scatter)
        for cp in copies:
            cp.start()
        for cp in copies:
            cp.wait()

    return pl.pallas_call(
        body, name=name,
        in_specs=[HBM] * n, out_specs=[HBM] * n,
        out_shape=_exchange_shapes(ins, scatter),
        scratch_shapes=_exchange_sems(n),
        compiler_params=pltpu.CompilerParams(has_side_effects=True),
    )(*ins)


def _gather_two_level(shards, name):
    n = len(shards)

    def body(*refs):
        ins, outs = refs[:n], refs[n:2 * n]
        send_sems, recv_sems, local_sems = refs[2 * n:]
        x, y, c = lax.axis_index("x"), lax.axis_index("y"), lax.axis_index("c")
        me, sib = (x, y, c), (x, y, 1 - c)
        chips = [(1 - x, y), (x, 1 - y), (1 - x, 1 - y)]

        def copy(a, k, block, to, src=None):
            slot = outs[a].at[_peer_index(block)]
            return pltpu.make_async_remote_copy(
                src_ref=slot if src is None else src, dst_ref=slot,
                send_sem=send_sems.at[a, k], recv_sem=recv_sems.at[a, k], device_id=to, device_id_type=MESH)

        own = [pltpu.make_async_copy(ins[a], outs[a].at[_peer_index(me)], local_sems.at[a]) for a in range(n)]
        first = []
        for a in range(n):
            first.append(copy(a, 0, me, sib, src=ins[a]))
            first += [copy(a, 1 + j, me, (*chip, c), src=ins[a]) for j, chip in enumerate(chips)]
        for cp in own + first:
            cp.start()
        passed = []
        for j, chip in enumerate(chips):
            for a in range(n):
                copy(a, 1 + j, (*chip, c), me).wait_recv()
                fwd = copy(a, 4 + j, (*chip, c), sib)
                fwd.start()
                passed.append(fwd)
        for a in range(n):
            copy(a, 0, sib, me).wait_recv()
            for j, chip in enumerate(chips):
                copy(a, 4 + j, (*chip, 1 - c), me).wait_recv()
        for cp in first + passed:
            cp.wait_send()
        for cp in own:
            cp.wait()

    return pl.pallas_call(
        body, name=name,
        in_specs=[HBM] * n, out_specs=[HBM] * n,
        out_shape=_exchange_shapes(shards, False),
        scratch_shapes=_exchange_sems(n),
        compiler_params=pltpu.CompilerParams(has_side_effects=True),
    )(*shards)


class _Hosted:
    def __init__(self, ins, shapes, sems, copies):
        self.ins, self.shapes, self.sems, self.copies = ins, shapes, sems, copies


def _pair_cols(pieces):
    n = len(pieces)

    def copies(in_refs, out_refs, sems):
        sib = (lax.axis_index("x"), lax.axis_index("y"), 1 - lax.axis_index("c"))
        return [pltpu.make_async_remote_copy(
            src_ref=in_refs[i].at[:, pl.ds(c0, w)], dst_ref=out_refs[i], send_sem=sems[0].at[i],
            recv_sem=sems[1].at[i], device_id=sib, device_id_type=MESH) for i, (_, c0, w) in enumerate(pieces)]

    return _Hosted([t for t, _, _ in pieces],
                   [jax.ShapeDtypeStruct((t.shape[0], w), t.dtype) for t, _, w in pieces],
                   [pltpu.SemaphoreType.DMA((n,)), pltpu.SemaphoreType.DMA((n,))], copies)


def _add_cols(a, c0, b, name):
    r, w = b.shape
    assert c0 % w == 0
    tr = _pick(r, (256, 128))

    def body(a_ref, b_ref, o_ref):
        o_ref[...] = (a_ref[...] + b_ref[...]).astype(o_ref.dtype)

    return pl.pallas_call(
        body, name=name, grid=(r // tr,),
        in_specs=[pl.BlockSpec((tr, w), lambda i: (i, c0 // w)), pl.BlockSpec((tr, w), lambda i: (i, 0))],
        out_specs=pl.BlockSpec((tr, w), lambda i: (i, 0)),
        out_shape=jax.ShapeDtypeStruct((r, w), CDT),
        compiler_params=_cparams(("parallel",)),
    )(a, b)


N_CHIP = 4


def _chip_scatter(t, name):
    def body(t_ref, o_ref, send_sems, recv_sems, local_sem):
        x, y, c = lax.axis_index("x"), lax.axis_index("y"), lax.axis_index("c")
        mine = 2 * x + y
        copies = [pltpu.make_async_copy(t_ref.at[mine], o_ref.at[mine], local_sem)]
        for j, (px, py) in enumerate([(1 - x, y), (x, 1 - y), (1 - x, 1 - y)]):
            copies.append(pltpu.make_async_remote_copy(
                src_ref=t_ref.at[2 * px + py], dst_ref=o_ref.at[mine],
                send_sem=send_sems.at[j], recv_sem=recv_sems.at[j], device_id=(px, py, c), device_id_type=MESH))
        for cp in copies:
            cp.start()
        for cp in copies:
            cp.wait()

    return pl.pallas_call(
        body, name=name, in_specs=[HBM], out_specs=HBM,
        out_shape=jax.ShapeDtypeStruct(t.shape, t.dtype),
        scratch_shapes=[pltpu.SemaphoreType.DMA((N_CHIP - 1,)), pltpu.SemaphoreType.DMA((N_CHIP - 1,)),
                        pltpu.SemaphoreType.DMA],
        compiler_params=pltpu.CompilerParams(has_side_effects=True),
    )(t)


def _all_reduce_small(v, name):
    r, c = v.shape

    def body(v_ref, o_ref, buf, send_sems, recv_sems):
        me, peers = _me_and_peers()
        buf[me] = v_ref[...]
        copies = []
        for kk, peer in enumerate(peers):
            cp = pltpu.make_async_remote_copy(
                src_ref=v_ref, dst_ref=buf.at[me],
                send_sem=send_sems.at[kk], recv_sem=recv_sems.at[kk],
                device_id=peer, device_id_type=MESH)
            cp.start()
            copies.append(cp)
        for cp in copies:
            cp.wait()
        acc = buf[0]
        for i in range(1, N_DEV):
            acc = acc + buf[i]
        o_ref[...] = acc

    return pl.pallas_call(
        body, name=name,
        in_specs=[pl.BlockSpec(memory_space=pltpu.VMEM)],
        out_specs=pl.BlockSpec(memory_space=pltpu.VMEM),
        out_shape=jax.ShapeDtypeStruct((r, c), F32),
        scratch_shapes=[pltpu.VMEM((N_DEV, r, c), F32),
                        pltpu.SemaphoreType.DMA((N_DEV - 1,)), pltpu.SemaphoreType.DMA((N_DEV - 1,))],
        compiler_params=pltpu.CompilerParams(has_side_effects=True),
    )(v)


def _adamw_math(g, w, m, v):
    m2 = ADAM_B1 * m + (1.0 - ADAM_B1) * g
    v2 = ADAM_B2 * v + (1.0 - ADAM_B2) * jnp.square(g)
    m_hat = m2 / (1.0 - ADAM_B1 ** ADAM_STEP)
    v_hat = v2 / (1.0 - ADAM_B2 ** ADAM_STEP)
    delta = -ADAM_LR * (m_hat / (jnp.sqrt(v_hat) + ADAM_EPS) + ADAM_WD * w)
    return delta, m2, v2


def _adamw_reduce(parts, w, m, v, name):
    r, c = w.shape
    nparts = parts.shape[0]
    tr = _pick(r, (128, 64, 32, 16, 8))

    def body(p_ref, w_ref, m_ref, v_ref, g_ref, d_ref, m2_ref, v2_ref):
        g = p_ref[0].astype(F32)
        for i in range(1, nparts):
            g = g + p_ref[i].astype(F32)
        delta, m2, v2 = _adamw_math(g, w_ref[...], m_ref[...], v_ref[...])
        g_ref[...] = g
        d_ref[...] = delta
        m2_ref[...] = m2
        v2_ref[...] = v2

    tile = pl.BlockSpec((tr, c), lambda i: (i, 0))
    shp = jax.ShapeDtypeStruct((r, c), F32)
    return pl.pallas_call(
        body, name=name, grid=(r // tr,),
        in_specs=[pl.BlockSpec((nparts, tr, c), lambda i: (0, i, 0)), tile, tile, tile],
        out_specs=[tile] * 4, out_shape=[shp] * 4,
        compiler_params=_cparams(("parallel",)),
    )(parts, w, m, v)


def _adamw_plain(g, w, m, v, name):
    def body(g_ref, w_ref, m_ref, v_ref, d_ref, m2_ref, v2_ref):
        delta, m2, v2 = _adamw_math(g_ref[...], w_ref[...], m_ref[...], v_ref[...])
        d_ref[...] = delta
        m2_ref[...] = m2
        v2_ref[...] = v2

    spec = pl.BlockSpec(memory_space=pltpu.VMEM)
    shp = jax.ShapeDtypeStruct(g.shape, F32)
    return pl.pallas_call(
        body, name=name, in_specs=[spec] * 4, out_specs=[spec] * 3, out_shape=[shp] * 3,
    )(g, w, m, v)


SMALL_ROWS, SMALL_COLS = 16, 3072


def _small_step(sg, gcw, loss, ws, ms, vs):
    ns = len(sg)
    widths = [t.shape[1] for t in sg]
    kk_, ch = gcw.shape[1], gcw.shape[2]
    assert ns < SMALL_ROWS and max(widths) <= SMALL_COLS

    def reduce_body(*refs):
        g_refs = refs[:ns]
        gcw_ref, loss_ref, tot_ref, totc_ref = refs[ns:ns + 4]
        mine, buf, minec, bufc, send_sems, recv_sems = refs[ns + 4:]
        me, peers = _me_and_peers()

        mine[...] = jnp.zeros_like(mine)
        for i in range(ns):
            mine[i:i + 1, 0:widths[i]] = g_refs[i][...]
        mine[ns:ns + 1, 0:LANES] = jnp.broadcast_to(loss_ref[...], (1, LANES))
        minec[...] = gcw_ref[...]
        buf[me] = mine[...]
        bufc[me] = minec[...]
        copies = []
        for j, peer in enumerate(peers):
            copies.append(pltpu.make_async_remote_copy(
                src_ref=mine, dst_ref=buf.at[me], send_sem=send_sems.at[0, j], recv_sem=recv_sems.at[0, j],
                device_id=peer, device_id_type=MESH))
            copies.append(pltpu.make_async_remote_copy(
                src_ref=minec, dst_ref=bufc.at[me], send_sem=send_sems.at[1, j], recv_sem=recv_sems.at[1, j],
                device_id=peer, device_id_type=MESH))
        for cp in copies:
            cp.start()
        for cp in copies:
            cp.wait()
        tot = buf[0]
        totc = bufc[0]
        for i in range(1, N_DEV):
            tot = tot + buf[i]
            totc = totc + bufc[i]
        tot_ref[...] = tot
        totc_ref[...] = totc

    vm = pl.BlockSpec(memory_space=pltpu.VMEM)
    tot, totc = pl.pallas_call(
        reduce_body, name="small_reduce",
        in_specs=[vm] * (ns + 2), out_specs=[vm, vm],
        out_shape=[jax.ShapeDtypeStruct((SMALL_ROWS, SMALL_COLS), F32), jax.ShapeDtypeStruct((N_DEV, kk_, ch), F32)],
        scratch_shapes=[pltpu.VMEM((SMALL_ROWS, SMALL_COLS), F32), pltpu.VMEM((N_DEV, SMALL_ROWS, SMALL_COLS), F32),
                        pltpu.VMEM((N_DEV, kk_, ch), F32), pltpu.VMEM((N_DEV, N_DEV, kk_, ch), F32),
                        pltpu.SemaphoreType.DMA((2, N_DEV - 1)), pltpu.SemaphoreType.DMA((2, N_DEV - 1))],
        compiler_params=pltpu.CompilerParams(has_side_effects=True),
    )(*sg, gcw, loss)

    def adamw_body(*refs):
        tot_ref, totc_ref = refs[:2]
        w_refs, m_refs, v_refs = (refs[2 + i * (ns + 1):2 + (i + 1) * (ns + 1)] for i in range(3))
        outs = refs[3 * ns + 5:]
        loss_out = outs[0]
        go, do_, mo, vo = (outs[1 + i * (ns + 1):1 + (i + 1) * (ns + 1)] for i in range(4))
        me, _ = _me_and_peers()
        loss_out[...] = tot_ref[ns:ns + 1, 0:1]
        for i in range(ns + 1):
            g = tot_ref[i:i + 1, 0:widths[i]] if i < ns else totc_ref[me]
            delta, m2, v2 = _adamw_math(g, w_refs[i][...], m_refs[i][...], v_refs[i][...])
            go[i][...] = g
            do_[i][...] = delta
            mo[i][...] = m2
            vo[i][...] = v2

    shapes = [jax.ShapeDtypeStruct(t.shape, F32) for t in ws]
    res = pl.pallas_call(
        adamw_body, name="small_adamw",
        in_specs=[vm] * (3 * ns + 5), out_specs=[vm] * (4 * ns + 5),
        out_shape=[jax.ShapeDtypeStruct((1, 1), F32)] + shapes * 4,
    )(tot, totc, *ws, *ms, *vs)
    n1 = ns + 1
    return res[0], res[1:1 + n1], res[1 + n1:1 + 2 * n1], res[1 + 2 * n1:1 + 3 * n1], res[1 + 3 * n1:]


def _cast_shard(w, name):
    r = w.shape[0]
    return _rows(lambda t: t, [w], [], [(w.shape[1], CDT)], ts=_pick(r, (256, 128)), name=name)[0]


BIG = ["w_in", "w_mem_kv", "w_up", "w_sb_out", "w_ssd_out", "w_mem_out", "w_o", "w_down"]
LATE_W = BIG[1:]
COL_SHARDED = ("w_in", "w_mem_kv", "w_up")
SMALL = ["norm_mix_pre", "conv_b", "dt_bias", "a_log", "d_skip", "ssd_norm", "norm_mem",
         "norm_mix_post", "norm_mlp_pre", "norm_mlp_post"]
ALL_W = ["norm_mix_pre", "w_in", "conv_w", "conv_b", "dt_bias", "a_log", "d_skip", "ssd_norm", "norm_mem",
         "w_mem_kv", "w_sb_out", "w_ssd_out", "w_mem_out", "w_o", "norm_mix_post", "norm_mlp_pre", "w_up",
         "w_down", "norm_mlp_post"]
LANES = 128


def _pack_rows(vecs):
    parts, offs, off = [], [], 0
    for t in vecs:
        flat = t.reshape(-1)
        n = flat.shape[0]
        rows = -(-n // (8 * LANES)) * 8
        parts.append(jnp.pad(flat, (0, rows * LANES - n)).reshape(rows, LANES))
        offs.append((off, n))
        off += rows
    return jnp.concatenate(parts, axis=0), offs


def _unpack_rows(packed, offs, shapes):
    out = []
    for (off, n), shape in zip(offs, shapes):
        rows = -(-n // (8 * LANES)) * 8
        out.append(packed[off:off + rows].reshape(-1)[:n].reshape(shape))
    return out


def kernel(x, mem, norm_mix_pre, w_in, conv_w, conv_b, dt_bias, a_log, d_skip, ssd_norm, norm_mem, w_mem_kv, w_sb_out, w_ssd_out, w_mem_out, w_o, norm_mix_post, norm_mlp_pre, w_up, w_down, norm_mlp_post, loss_target, m_norm_mix_pre, m_w_in, m_conv_w, m_conv_b, m_dt_bias, m_a_log, m_d_skip, m_ssd_norm, m_norm_mem, m_w_mem_kv, m_w_sb_out, m_w_ssd_out, m_w_mem_out, m_w_o, m_norm_mix_post, m_norm_mlp_pre, m_w_up, m_w_down, m_norm_mlp_post, v_norm_mix_pre, v_w_in, v_conv_w, v_conv_b, v_dt_bias, v_a_log, v_d_skip, v_ssd_norm, v_norm_mem, v_w_mem_kv, v_w_sb_out, v_w_ssd_out, v_w_mem_out, v_w_o, v_norm_mix_post, v_norm_mlp_pre, v_w_up, v_w_down, v_norm_mlp_post):
    wd = dict(norm_mix_pre=norm_mix_pre, w_in=w_in, conv_w=conv_w, conv_b=conv_b, dt_bias=dt_bias, a_log=a_log,
              d_skip=d_skip, ssd_norm=ssd_norm, norm_mem=norm_mem, w_mem_kv=w_mem_kv, w_sb_out=w_sb_out,
              w_ssd_out=w_ssd_out, w_mem_out=w_mem_out, w_o=w_o, norm_mix_post=norm_mix_post,
              norm_mlp_pre=norm_mlp_pre, w_up=w_up, w_down=w_down, norm_mlp_post=norm_mlp_post)
    md = dict(norm_mix_pre=m_norm_mix_pre, w_in=m_w_in, conv_w=m_conv_w, conv_b=m_conv_b, dt_bias=m_dt_bias,
              a_log=m_a_log, d_skip=m_d_skip, ssd_norm=m_ssd_norm, norm_mem=m_norm_mem, w_mem_kv=m_w_mem_kv,
              w_sb_out=m_w_sb_out, w_ssd_out=m_w_ssd_out, w_mem_out=m_w_mem_out, w_o=m_w_o,
              norm_mix_post=m_norm_mix_post, norm_mlp_pre=m_norm_mlp_pre, w_up=m_w_up, w_down=m_w_down,
              norm_mlp_post=m_norm_mlp_post)
    vd = dict(norm_mix_pre=v_norm_mix_pre, w_in=v_w_in, conv_w=v_conv_w, conv_b=v_conv_b, dt_bias=v_dt_bias,
              a_log=v_a_log, d_skip=v_d_skip, ssd_norm=v_ssd_norm, norm_mem=v_norm_mem, w_mem_kv=v_w_mem_kv,
              w_sb_out=v_w_sb_out, w_ssd_out=v_w_ssd_out, w_mem_out=v_w_mem_out, w_o=v_w_o,
              norm_mix_post=v_norm_mix_post, norm_mlp_pre=v_norm_mlp_pre, w_up=v_w_up, w_down=v_w_down,
              norm_mlp_post=v_norm_mlp_post)
    shards = {n: _cast_shard(wd[n][0], "cast_" + n) for n in BIG}
    w_in_g, conv_w_g = _gather_two_level([shards["w_in"], wd["conv_w"][0]], "gather_w_in")
    wt = {"w_in": _full_weight("w_in", w_in_g)}
    ch = conv_w_g.shape[2]

    p = {n: wd[n] for n in SMALL}
    p["conv_w"] = conv_w_g.transpose(1, 0, 2).reshape(CONV_K, N_DEV * ch)
    loss, grad_x, gw, sg, late_received = _local_step(x[0], mem[0], loss_target[0], p, wt,
                                                      [shards[n] for n in LATE_W])

    received = dict(zip(LATE_W, late_received))
    g_in = gw["w_in"]
    by_core = g_in.reshape(g_in.shape[0], N_CHIP, 2, g_in.shape[1] // N_DEV)
    mine = lax.dynamic_index_in_dim(by_core, lax.axis_index("c"), 2, keepdims=False).transpose(1, 0, 2)
    received["w_in"] = _chip_scatter(mine, "scatter_w_in_chips")

    grads, deltas, new_m, new_v = {}, {}, {}, {}
    for n in BIG:
        g, dl, m2, v2 = _adamw_reduce(received[n], wd[n][0], md[n][0], vd[n][0], "adamw_" + n)
        grads[n], deltas[n], new_m[n], new_v[n] = g[None], dl[None], m2[None], v2[None]
    small_names = SMALL + ["conv_w"]
    gcw = sg["conv_w"].reshape(CONV_K, N_DEV, ch).transpose(1, 0, 2)
    small_of = lambda dct: [dct[n] for n in SMALL] + [dct["conv_w"][0]]
    loss_red, g_s, d_s, m_s, v_s = _small_step([sg[n] for n in SMALL], gcw, loss, small_of(wd), small_of(md),
                                               small_of(vd))
    for i, n in enumerate(small_names):
        shape = wd[n].shape
        grads[n], deltas[n], new_m[n], new_v[n] = (t.reshape(shape) for t in (g_s[i], d_s[i], m_s[i], v_s[i]))
    loss_out = loss_red.reshape(())

    return (loss_out, grad_x[None], *[grads[n] for n in ALL_W], *[deltas[n] for n in ALL_W],
            *[new_m[n] for n in ALL_W], *[new_v[n] for n in ALL_W])
```

```python
import functools

import jax
import jax.numpy as jnp
from jax import lax
from jax.experimental import pallas as pl
from jax.experimental.pallas import tpu as pltpu

F32 = jnp.float32
BF16 = jnp.bfloat16
CDT = jnp.bfloat16
EPS = 1e-6
VMEM_LIMIT = 56 * 1024 * 1024

N_DEV = 8
D_MODEL = 1024
SB_H, SB_DH = 16, 64
SSD_G, SSD_R, SSD_P, SSD_N, SSD_L = 4, 8, 64, 128, 128
SSD_H = SSD_G * SSD_R
SSD_INNER = SSD_H * SSD_P
CONV_K = 4
CONV_DIM = SSD_INNER + 2 * SSD_G * SSD_N
MEM_H, MEM_DH = 4, 256
DT_PAD = 128
SB_TQ, SB_BK = 1024, 256
CONV_PAD = 8
MM_TILE, MM_TILE_K = 1024, 2048

ADAM_LR, ADAM_B1, ADAM_B2, ADAM_EPS, ADAM_WD, ADAM_STEP = 0.001, 0.9, 0.999, 1e-08, 0.01, 10

NT = (((1,), (1,)), ((), ()))
TN = (((0,), (0,)), ((), ()))
NN = (((1,), (0,)), ((), ()))


def _cparams(sem=None):
    return pltpu.CompilerParams(dimension_semantics=sem, vmem_limit_bytes=VMEM_LIMIT)


def _pick(n, cands):
    for c in cands:
        if n % c == 0:
            return c
    return n


def _dot(a, b, dims=NN):
    return lax.dot_general(a.astype(CDT), b.astype(CDT), dims, preferred_element_type=F32)


def _split_dot(x, t, left, pieces):
    if CDT == F32:
        return lax.dot_general(t, x, NN, preferred_element_type=F32) if left else \
            lax.dot_general(x, t, NN, preferred_element_type=F32)
    acc = None
    rem = x
    for _ in range(pieces):
        hi = rem.astype(BF16)
        rem = rem - hi.astype(F32)
        d = lax.dot_general(t, hi, NN, preferred_element_type=F32) if left else \
            lax.dot_general(hi, t, NN, preferred_element_type=F32)
        acc = d if acc is None else acc + d
    return acc


def _iota(shape, dim):
    return lax.broadcasted_iota(jnp.int32, shape, dim)


def _sigmoid(x):
    return 1.0 / (1.0 + jnp.exp(-x))


def _silu(x):
    return x * _sigmoid(x)


def _dsilu(x):
    s = _sigmoid(x)
    return s * (1.0 + x * (1.0 - s))


def _softplus(x):
    return jnp.maximum(x, 0.0) + jnp.log(1.0 + jnp.exp(-jnp.abs(x)))


def _rms(x, g):
    r = lax.rsqrt(jnp.mean(x * x, axis=-1, keepdims=True) + EPS)
    return x * r * g


def _rms_bwd(x, g, dy):
    r = lax.rsqrt(jnp.mean(x * x, axis=-1, keepdims=True) + EPS)
    n = x * r
    dn = dy * g
    dx = r * (dn - n * jnp.mean(dn * n, axis=-1, keepdims=True))
    dg = jnp.sum(dy * n, axis=0, keepdims=True)
    return dx, dg


def _mm(a, b, *, ta=False, tb=False, out_dtype=F32, name, b_off=0, n=None, b_koff=0, into=None, into_off=0,
        hosted=None):
    m = a.shape[1] if ta else a.shape[0]
    k = a.shape[0] if ta else a.shape[1]
    if n is None:
        n = b.shape[0] if tb else b.shape[1]
    assert b_koff + k <= (b.shape[1] if tb else b.shape[0])
    bm = _pick(m, (MM_TILE, 512, 256, 128))
    bn = _pick(n, (MM_TILE, 512, 256, 128))
    bk = next(c for c in (MM_TILE_K, 1024, 512, 256, 128, k) if k % c == 0 and b_koff % c == 0)
    nk = k // bk
    assert b_off % bn == 0 and into_off % bn == 0
    jb, jo, kb = b_off // bn, into_off // bn, b_koff // bk
    dims = (((0 if ta else 1,), (1 if tb else 0,)), ((), ()))
    grid = (m // bm, n // bn, nk)
    off = 1 if into is not None else 0
    nh_in = len(hosted.ins) if hosted else 0
    nh_out = len(hosted.shapes) if hosted else 0

    def body(a_ref, b_ref, *rest):
        o_ref = rest[off + nh_in]
        acc_ref = rest[off + nh_in + 1 + nh_out]
        if hosted:
            h_refs = (rest[off:off + nh_in], rest[off + nh_in + 1:off + nh_in + 1 + nh_out],
                      rest[off + nh_in + nh_out + 2:])
            step = (pl.program_id(0) * grid[1] + pl.program_id(1)) * grid[2] + pl.program_id(2)

            @pl.when(step == 0)
            def _():
                for cp in hosted.copies(*h_refs):
                    cp.start()

        part = _dot(a_ref[...], b_ref[...], dims)
        if nk == 1:
            o_ref[...] = part.astype(o_ref.dtype)
        else:
            kk = pl.program_id(2)

            @pl.when(kk == 0)
            def _():
                acc_ref[...] = part

            @pl.when(jnp.logical_and(kk > 0, kk < nk - 1))
            def _():
                acc_ref[...] += part

            @pl.when(kk == nk - 1)
            def _():
                o_ref[...] = (acc_ref[...] + part).astype(o_ref.dtype)

        if hosted:
            @pl.when(step == grid[0] * grid[1] * grid[2] - 1)
            def _():
                for cp in hosted.copies(*h_refs):
                    cp.wait()

    a_spec = pl.BlockSpec((bk, bm), lambda i, j, kk: (kk, i)) if ta else \
        pl.BlockSpec((bm, bk), lambda i, j, kk: (i, kk))
    b_spec = pl.BlockSpec((bn, bk), lambda i, j, kk: (j + jb, kk + kb)) if tb else \
        pl.BlockSpec((bk, bn), lambda i, j, kk: (kk + kb, j + jb))
    extra = {} if into is None else {"input_output_aliases": {2: 0}}
    out_shape = jax.ShapeDtypeStruct((m, n), out_dtype) if into is None else \
        jax.ShapeDtypeStruct(into.shape, into.dtype)
    res = pl.pallas_call(
        body, name=name, grid=grid,
        in_specs=[a_spec, b_spec] + ([] if into is None else [pl.BlockSpec(memory_space=pl.ANY)]) + [HBM] * nh_in,
        out_specs=[pl.BlockSpec((bm, bn), lambda i, j, kk: (i, j + jo))] + [HBM] * nh_out,
        out_shape=[out_shape] + (list(hosted.shapes) if hosted else []),
        scratch_shapes=[pltpu.VMEM((bm, bn) if nk > 1 else (8, 128), F32)] + (list(hosted.sems) if hosted else []),
        compiler_params=_cparams(("arbitrary",) * 3 if hosted else ("parallel", "parallel", "arbitrary")),
        **extra,
    )(*((a, b) if into is None else (a, b, into)), *(hosted.ins if hosted else ()))
    return res if hosted else res[0]


def _rows(fn, tiled, params, outs, accs=(), *, ts, name):
    s = tiled[0].shape[0]
    ts = min(ts, s)
    assert s % ts == 0
    nt, npar, no, na = len(tiled), len(params), len(outs), len(accs)

    def body(*refs):
        i = pl.program_id(0)
        vals = [r[...] for r in refs[:nt + npar]]
        res = fn(*vals)
        if not isinstance(res, (tuple, list)):
            res = (res,)
        orefs = refs[nt + npar:nt + npar + no]
        arefs = refs[nt + npar + no:]
        for r_, val in zip(orefs, res[:no]):
            r_[...] = val.astype(r_.dtype)
        if na:
            @pl.when(i == 0)
            def _():
                for r_ in arefs:
                    r_[...] = jnp.zeros_like(r_)

            for r_, val in zip(arefs, res[no:]):
                r_[...] += jnp.broadcast_to(val, r_.shape)

    in_specs = [pl.BlockSpec((ts, a.shape[1]), lambda i: (i, 0)) for a in tiled]
    in_specs += [pl.BlockSpec(p.shape, lambda i: (0, 0)) for p in params]
    out_specs = [pl.BlockSpec((ts, w), lambda i: (i, 0)) for (w, _) in outs]
    out_specs += [pl.BlockSpec(shape, lambda i: (0, 0)) for shape in accs]
    out_shape = [jax.ShapeDtypeStruct((s, w), dt) for (w, dt) in outs]
    out_shape += [jax.ShapeDtypeStruct(shape, F32) for shape in accs]
    res = pl.pallas_call(
        body, name=name, grid=(s // ts,),
        in_specs=in_specs, out_specs=out_specs, out_shape=out_shape,
        compiler_params=_cparams(("arbitrary",)),
    )(*tiled, *params)
    return res


def _sb_block(qs, kb, diag):
    tq, bk = qs.shape[0], kb.shape[0]
    z = _dot(qs, kb, NT)
    lb = jnp.minimum(z, 0.0) - jnp.log(1.0 + jnp.exp(-jnp.abs(z)))
    lk = lb - z
    if diag is None:
        return None, lb, lk
    causal = (diag + _iota((tq, bk), 1)) < _iota((tq, bk), 0)
    return causal, lb, jnp.where(causal, lk, 0.0)


def _fused_exchange(scatter, ncols, nsteps):
    def hooks(ins, outs, sems):
        step = pl.program_id(0) * ncols + pl.program_id(1)

        @pl.when(step == 0)
        def _():
            for cp in _exchange_copies(ins, outs, *sems, scatter):
                cp.start()

        def finish():
            @pl.when(step == nsteps - 1)
            def _():
                for cp in _exchange_copies(ins, outs, *sems, scatter):
                    cp.wait()
        return finish
    return hooks


def _sb_fwd(q, k, v, comm=()):
    h, s, dh = q.shape
    tq, bk = min(SB_TQ, s), min(SB_BK, s)
    scale = dh ** -0.5
    nc = len(comm)
    hooks = _fused_exchange(False, s // tq, h * (s // tq))

    def body(q_ref, k_ref, v_ref, *rest):
        o_ref, lt_ref = rest[nc:nc + 2]
        if nc:
            finish = hooks(rest[:nc], rest[nc + 2:2 * nc + 2], rest[2 * nc + 2:])
        i = pl.program_id(1)
        q0 = i * tq
        qs = q_ref[...] * scale
        tri = (_iota((bk, bk), 0) > _iota((bk, bk), 1)).astype(CDT)

        def step(k0, carry, diag, r0=0):
            cf, acc = carry
            kb = k_ref[pl.ds(k0, bk), :]
            vb = v_ref[pl.ds(k0, bk), :]
            causal, lb, lk = _sb_block(qs[r0:], kb, diag)
            w = jnp.exp(lb + cf + _split_dot(lk, tri, False, 2))
            if causal is not None:
                w = jnp.where(causal, w, 0.0)
            return cf + jnp.sum(lk, axis=1, keepdims=True), acc + _dot(w, vb)

        carry = (jnp.zeros((tq, 1), F32), jnp.zeros((tq, dh), F32))
        for d in reversed(range(tq // bk)):
            r0 = d * bk
            sub = step(pl.multiple_of(q0 + r0, bk), tuple(t[r0:] for t in carry), 0, r0)
            carry = tuple(jnp.concatenate([t[:r0], u], axis=0) if r0 else u for t, u in zip(carry, sub))
        nfull = q0 // bk
        cf, acc = lax.fori_loop(
            0, nfull, lambda jj, c: step(pl.multiple_of((nfull - 1 - jj) * bk, bk), c, None), carry)
        o_ref[...] = acc
        lt_ref[...] = cf
        if nc:
            finish()

    return pl.pallas_call(
        body, name="sb_fwd", grid=(h, s // tq),
        in_specs=[pl.BlockSpec((None, tq, dh), lambda a, i: (a, i, 0)),
                  pl.BlockSpec((None, s, dh), lambda a, i: (a, 0, 0)),
                  pl.BlockSpec((None, s, dh), lambda a, i: (a, 0, 0))] + [HBM] * nc,
        out_specs=[pl.BlockSpec((None, tq, dh), lambda a, i: (a, i, 0)),
                   pl.BlockSpec((None, tq, 1), lambda a, i: (a, i, 0))] + [HBM] * nc,
        out_shape=[jax.ShapeDtypeStruct((h, s, dh), F32), jax.ShapeDtypeStruct((h, s, 1), F32)]
        + _exchange_shapes(comm, False),
        scratch_shapes=_exchange_sems(nc) if nc else [],
        compiler_params=_cparams(("arbitrary", "arbitrary")),
    )(q, k, v, *comm)


def _sb_bwd(q, k, v, ltot, do, comm=()):
    h, s, dh = q.shape
    tq, bk = min(SB_TQ, s), min(SB_BK, s)
    scale = dh ** -0.5
    nc = len(comm)
    hooks = _fused_exchange(True, s // tq, h * (s // tq))

    def body(q_ref, k_ref, v_ref, lt_ref, do_ref, *rest):
        dq_ref, dk_ref, dv_ref = rest[nc:nc + 3]
        if nc:
            finish = hooks(rest[:nc], rest[nc + 3:2 * nc + 3], rest[2 * nc + 3:])
        i = pl.program_id(1)

        @pl.when(i == 0)
        def _():
            dk_ref[...] = jnp.zeros_like(dk_ref)
            dv_ref[...] = jnp.zeros_like(dv_ref)

        q0 = i * tq
        qs = q_ref[...] * scale
        dov = do_ref[...].astype(CDT)
        ltot = lt_ref[...]
        tri_le = (_iota((bk, bk), 0) <= _iota((bk, bk), 1)).astype(CDT)
        tri_lt = (_iota((bk, bk), 0) < _iota((bk, bk), 1)).astype(CDT)

        def step(k0, carry, diag, r0=0):
            cf, cg, dq = carry
            kb = k_ref[pl.ds(k0, bk), :]
            vb = v_ref[pl.ds(k0, bk), :]
            causal, lb, lk = _sb_block(qs[r0:], kb, diag)
            w = jnp.exp(lb + ltot[r0:] - (cf + _split_dot(lk, tri_le, False, 2)))
            if causal is not None:
                w = jnp.where(causal, w, 0.0)
            g = w * _dot(dov[r0:], vb, NT)
            gsum = cg + _split_dot(g, tri_lt, False, 2)
            dz = g - (g + gsum) * jnp.exp(lb)
            if causal is not None:
                dz = jnp.where(causal, dz, 0.0)
            dzc = dz.astype(CDT)
            dk_ref[pl.ds(k0, bk), :] += _dot(dzc, qs[r0:], TN)
            dv_ref[pl.ds(k0, bk), :] += _dot(w, dov[r0:], TN)
            return (cf + jnp.sum(lk, axis=1, keepdims=True), cg + jnp.sum(g, axis=1, keepdims=True),
                    dq + _dot(dzc, kb))

        carry = (jnp.zeros((tq, 1), F32), jnp.zeros((tq, 1), F32), jnp.zeros((tq, dh), F32))
        carry = lax.fori_loop(0, q0 // bk, lambda jj, c: step(pl.multiple_of(jj * bk, bk), c, None), carry)
        for d in range(tq // bk):
            r0 = d * bk
            sub = step(pl.multiple_of(q0 + r0, bk), tuple(t[r0:] for t in carry), 0, r0)
            carry = tuple(jnp.concatenate([t[:r0], u], axis=0) if r0 else u for t, u in zip(carry, sub))
        dq_ref[...] = carry[2] * scale
        if nc:
            finish()

    tile = pl.BlockSpec((None, tq, dh), lambda a, i: (a, i, 0))
    full = pl.BlockSpec((None, s, dh), lambda a, i: (a, 0, 0))
    shp = jax.ShapeDtypeStruct((h, s, dh), F32)
    return pl.pallas_call(
        body, name="sb_bwd", grid=(h, s // tq),
        in_specs=[tile, full, full, pl.BlockSpec((None, tq, 1), lambda a, i: (a, i, 0)), tile] + [HBM] * nc,
        out_specs=[tile, full, full] + [HBM] * nc,
        out_shape=[shp, shp, shp] + _exchange_shapes(comm, True),
        scratch_shapes=_exchange_sems(nc) if nc else [],
        compiler_params=_cparams(("arbitrary", "arbitrary")),
    )(q, k, v, ltot, do, *comm)


SB_PAIR = 128


def _sb_fwd_pairs(qkv, comm=()):
    s, d3 = qkv.shape
    d = d3 // 3
    npair = d // SB_PAIR
    tq, bk = min(SB_TQ, s), min(SB_BK, s)
    scale = SB_DH ** -0.5
    nc = len(comm)
    hooks = _fused_exchange(False, s // tq, npair * (s // tq))

    def body(q_ref, k_ref, v_ref, *rest):
        y_ref, lt_ref = rest[nc:nc + 2]
        if nc:
            finish = hooks(rest[:nc], rest[nc + 2:2 * nc + 2], rest[2 * nc + 2:])
        i = pl.program_id(1)
        q0 = i * tq
        q2 = q_ref[...] * scale
        lane_head = (_iota((1, SB_PAIR), 1) >= SB_DH).astype(jnp.int32)
        tri = (_iota((bk, bk), 0) > _iota((bk, bk), 1)).astype(CDT)

        def head(hh, y):
            mine = lane_head == hh
            qs = jnp.where(mine, q2, jnp.zeros_like(q2))

            def step(k0, carry, diag, r0=0):
                cf, acc = carry
                kb = k_ref[pl.ds(k0, bk), :]
                vb = v_ref[pl.ds(k0, bk), :]
                causal, lb, lk = _sb_block(qs[r0:], kb, diag)
                w = jnp.exp(lb + cf + _split_dot(lk, tri, False, 2))
                if causal is not None:
                    w = jnp.where(causal, w, 0.0)
                return cf + jnp.sum(lk, axis=1, keepdims=True), acc + _dot(w, vb)

            carry = (jnp.zeros((tq, 1), F32), jnp.zeros((tq, SB_PAIR), F32))
            for dd in reversed(range(tq // bk)):
                r0 = dd * bk
                sub = step(pl.multiple_of(q0 + r0, bk), tuple(t[r0:] for t in carry), 0, r0)
                carry = tuple(jnp.concatenate([t[:r0], u], axis=0) if r0 else u for t, u in zip(carry, sub))
            nfull = q0 // bk
            cf, acc = lax.fori_loop(
                0, nfull, lambda jj, c: step(pl.multiple_of((nfull - 1 - jj) * bk, bk), c, None), carry)
            lt_ref[hh] = cf
            return jnp.where(mine, acc, y)

        y_ref[...] = lax.fori_loop(0, 2, head, jnp.zeros((tq, SB_PAIR), F32)).astype(y_ref.dtype)
        if nc:
            finish()

    return pl.pallas_call(
        body, name="sb_fwd", grid=(npair, s // tq),
        in_specs=[pl.BlockSpec((tq, SB_PAIR), lambda a, i: (i, a)),
                  pl.BlockSpec((s, SB_PAIR), lambda a, i: (0, npair + a)),
                  pl.BlockSpec((s, SB_PAIR), lambda a, i: (0, 2 * npair + a))] + [HBM] * nc,
        out_specs=[pl.BlockSpec((tq, SB_PAIR), lambda a, i: (i, a)),
                   pl.BlockSpec((2, tq, 1), lambda a, i: (a, i, 0))] + [HBM] * nc,
        out_shape=[jax.ShapeDtypeStruct((s, d), CDT), jax.ShapeDtypeStruct((2 * npair, s, 1), F32)]
        + _exchange_shapes(comm, False),
        scratch_shapes=_exchange_sems(nc) if nc else [],
        compiler_params=_cparams(("arbitrary", "arbitrary")),
    )(qkv, qkv, qkv, *comm)


def _sb_bwd_pairs(qkv, ltot, dy, comm=()):
    s, d3 = qkv.shape
    d = d3 // 3
    npair = d // SB_PAIR
    tq, bk = min(SB_TQ, s), min(SB_BK, s)
    scale = SB_DH ** -0.5
    nc = len(comm)
    hooks = _fused_exchange(True, s // tq, npair * (s // tq))

    def body(q_ref, k_ref, v_ref, lt_ref, dy_ref, *rest):
        dq_ref, dk_ref, dv_ref = rest[nc:nc + 3]
        if nc:
            finish = hooks(rest[:nc], rest[nc + 3:2 * nc + 3], rest[2 * nc + 3:])
        i = pl.program_id(1)

        @pl.when(i == 0)
        def _():
            dk_ref[...] = jnp.zeros_like(dk_ref)
            dv_ref[...] = jnp.zeros_like(dv_ref)

        q0 = i * tq
        q2 = q_ref[...] * scale
        do2 = dy_ref[...].astype(CDT)
        lane_head = (_iota((1, SB_PAIR), 1) >= SB_DH).astype(jnp.int32)
        tri_le = (_iota((bk, bk), 0) <= _iota((bk, bk), 1)).astype(CDT)
        tri_lt = (_iota((bk, bk), 0) < _iota((bk, bk), 1)).astype(CDT)

        def head(hh, dq_all):
            mine = lane_head == hh
            qs = jnp.where(mine, q2, jnp.zeros_like(q2))
            dov = jnp.where(mine, do2, jnp.zeros_like(do2))
            ltot_h = lt_ref[hh]

            def step(k0, carry, diag, r0=0):
                cf, cg, dq = carry
                kb = k_ref[pl.ds(k0, bk), :]
                vb = v_ref[pl.ds(k0, bk), :]
                causal, lb, lk = _sb_block(qs[r0:], kb, diag)
                w = jnp.exp(lb + ltot_h[r0:] - (cf + _split_dot(lk, tri_le, False, 2)))
                if causal is not None:
                    w = jnp.where(causal, w, 0.0)
                g = w * _dot(dov[r0:], vb, NT)
                gsum = cg + _split_dot(g, tri_lt, False, 2)
                dz = g - (g + gsum) * jnp.exp(lb)
                if causal is not None:
                    dz = jnp.where(causal, dz, 0.0)
                dzc = dz.astype(CDT)
                dk_ref[pl.ds(k0, bk), :] += _dot(dzc, qs[r0:], TN)
                dv_ref[pl.ds(k0, bk), :] += _dot(w, dov[r0:], TN)
                kbm = jnp.where(mine, kb, jnp.zeros_like(kb))
                return (cf + jnp.sum(lk, axis=1, keepdims=True), cg + jnp.sum(g, axis=1, keepdims=True),
                        dq + _dot(dzc, kbm))

            carry = (jnp.zeros((tq, 1), F32), jnp.zeros((tq, 1), F32), jnp.zeros((tq, SB_PAIR), F32))
            carry = lax.fori_loop(0, q0 // bk, lambda jj, c: step(pl.multiple_of(jj * bk, bk), c, None), carry)
            for dd in range(tq // bk):
                r0 = dd * bk
                sub = step(pl.multiple_of(q0 + r0, bk), tuple(t[r0:] for t in carry), 0, r0)
                carry = tuple(jnp.concatenate([t[:r0], u], axis=0) if r0 else u for t, u in zip(carry, sub))
            return dq_all + carry[2]

        dq_ref[...] = lax.fori_loop(0, 2, head, jnp.zeros((tq, SB_PAIR), F32)) * scale
        if nc:
            finish()

    tile = pl.BlockSpec((tq, SB_PAIR), lambda a, i: (i, a))
    acc = pl.BlockSpec((s, SB_PAIR), lambda a, i: (0, a))
    shp = jax.ShapeDtypeStruct((s, d), F32)
    return pl.pallas_call(
        body, name="sb_bwd", grid=(npair, s // tq),
        in_specs=[tile, pl.BlockSpec((s, SB_PAIR), lambda a, i: (0, npair + a)),
                  pl.BlockSpec((s, SB_PAIR), lambda a, i: (0, 2 * npair + a)),
                  pl.BlockSpec((2, tq, 1), lambda a, i: (a, i, 0)), tile] + [HBM] * nc,
        out_specs=[tile, acc, acc] + [HBM] * nc,
        out_shape=[shp, shp, shp] + _exchange_shapes(comm, True),
        scratch_shapes=_exchange_sems(nc) if nc else [],
        compiler_params=_cparams(("arbitrary", "arbitrary")),
    )(qkv, qkv, qkv, ltot, dy, *comm)


def _pick_lane(tile, r):
    return jnp.sum(jnp.where(_iota(tile.shape, 1) == r, tile, 0.0), axis=1, keepdims=True)


def _pick_row(tile, r):
    return jnp.sum(jnp.where(_iota(tile.shape, 0) == r, tile, 0.0), axis=0, keepdims=True)


def _ssd_chunk_setup(c_ref, b_ref, dac_ref, dar_ref, cb_ref, acsc_ref, acsr_ref):
    l = SSD_L
    tdt = F32 if CDT == F32 else BF16
    lower = (_iota((l, l), 1) <= _iota((l, l), 0)).astype(tdt)
    upper = (_iota((l, l), 0) <= _iota((l, l), 1)).astype(tdt)
    cb_ref[...] = _dot(c_ref[...], b_ref[...], NT)
    acsc_ref[...] = _split_dot(dac_ref[...], lower, True, 3)
    acsr_ref[...] = _split_dot(dar_ref[...], upper, False, 3)


def _ssd_fwd(xh, dtc, dac, dar, dsk, xbc):
    hh, s, p = xh.shape
    l, n, g_, r_ = SSD_L, SSD_N, SSD_G, SSD_R
    nc = s // l
    boff = SSD_INNER // n
    coff = boff + g_

    def body(x_ref, dtc_ref, dac_ref, dar_ref, dsk_ref, b_ref, c_ref, y_ref, st_ref,
             state_ref, cb_ref, acsc_ref, acsr_ref):
        c = pl.program_id(1)
        r = pl.program_id(2)

        @pl.when(r == 0)
        def _():
            _ssd_chunk_setup(c_ref, b_ref, dac_ref, dar_ref, cb_ref, acsc_ref, acsr_ref)

        @pl.when(c == 0)
        def _():
            state_ref[r] = jnp.zeros((n, p), F32)

        a_col = _pick_lane(acsc_ref[...], r)
        a_row = _pick_row(acsr_ref[...], r)
        dt_col = _pick_lane(dtc_ref[...], r)
        dsk_h = _pick_lane(dsk_ref[...], r)
        xv = x_ref[...]
        xd = xv * dt_col
        mask = _iota((l, l), 1) <= _iota((l, l), 0)
        decay = jnp.where(mask, jnp.exp(jnp.minimum(a_col - a_row, 0.0)), 0.0)
        w = cb_ref[...] * decay
        hprev = state_ref[r]
        cv = c_ref[...]
        y = _dot(w, xd) + jnp.exp(a_col) * _dot(cv, hprev)
        y_ref[...] = y + dsk_h * xv
        a_end = a_col[l - 1:l, :]
        dte = jnp.exp(a_end - a_col)
        st_ref[...] = hprev
        state_ref[r] = hprev * jnp.exp(a_end) + _dot(b_ref[...], xd * dte, TN)

    return pl.pallas_call(
        body, name="ssd_fwd", grid=(g_, nc, r_),
        in_specs=[pl.BlockSpec((None, l, p), lambda g, c, r: (g * r_ + r, c, 0)),
                  pl.BlockSpec((None, l, r_), lambda g, c, r: (g, c, 0)),
                  pl.BlockSpec((None, l, r_), lambda g, c, r: (g, c, 0)),
                  pl.BlockSpec((None, r_, l), lambda g, c, r: (g, 0, c)),
                  pl.BlockSpec((None, 1, r_), lambda g, c, r: (g, 0, 0)),
                  pl.BlockSpec((l, n), lambda g, c, r: (c, boff + g)),
                  pl.BlockSpec((l, n), lambda g, c, r: (c, coff + g))],
        out_specs=[pl.BlockSpec((None, l, p), lambda g, c, r: (g * r_ + r, c, 0)),
                   pl.BlockSpec((None, None, n, p), lambda g, c, r: (g * r_ + r, c, 0, 0))],
        out_shape=[jax.ShapeDtypeStruct((hh, s, p), F32),
                   jax.ShapeDtypeStruct((hh, nc, n, p), F32)],
        scratch_shapes=[pltpu.VMEM((r_, n, p), F32), pltpu.VMEM((l, l), F32),
                        pltpu.VMEM((l, r_), F32), pltpu.VMEM((r_, l), F32)],
        compiler_params=_cparams(("parallel", "arbitrary", "arbitrary")),
    )(xh, dtc, dac, dar, dsk, xbc, xbc)


def _ssd_bwd(xh, dtc, dac, dar, dsk, xbc, st, dy):
    hh, s, p = xh.shape
    l, n, g_, r_ = SSD_L, SSD_N, SSD_G, SSD_R
    nc = s // l
    boff = SSD_INNER // n
    coff = boff + g_

    def body(x_ref, dtc_ref, dac_ref, dar_ref, dsk_ref, b_ref, c_ref, st_ref, dy_ref,
             dx_ref, dda_ref, dxx_ref, db_ref, dc_ref, dah_ref, ddsk_ref,
             dstate_ref, cb_ref, acsc_ref, acsr_ref):
        c = pl.program_id(1)
        r = pl.program_id(2)

        @pl.when(r == 0)
        def _():
            _ssd_chunk_setup(c_ref, b_ref, dac_ref, dar_ref, cb_ref, acsc_ref, acsr_ref)
            db_ref[...] = jnp.zeros_like(db_ref)
            dc_ref[...] = jnp.zeros_like(dc_ref)
            dda_ref[...] = jnp.zeros_like(dda_ref)
            dxx_ref[...] = jnp.zeros_like(dxx_ref)

        @pl.when(jnp.logical_and(c == 0, r == 0))
        def _():
            dah_ref[...] = jnp.zeros_like(dah_ref)
            ddsk_ref[...] = jnp.zeros_like(ddsk_ref)

        @pl.when(c == 0)
        def _():
            dstate_ref[r] = jnp.zeros((n, p), F32)

        a_col = _pick_lane(acsc_ref[...], r)
        a_row = _pick_row(acsr_ref[...], r)
        dt_col = _pick_lane(dtc_ref[...], r)
        dsk_h = _pick_lane(dsk_ref[...], r)
        xv = x_ref[...]
        dyv = dy_ref[...]
        xd = xv * dt_col
        il = _iota((l, l), 0)
        isx = _iota((l, l), 1)
        decay = jnp.where(isx <= il, jnp.exp(jnp.minimum(a_col - a_row, 0.0)), 0.0)
        cb = cb_ref[...]
        w = cb * decay
        dhn = dstate_ref[r]
        hc = st_ref[...]
        bv = b_ref[...]
        cv = c_ref[...]
        a_end = a_col[l - 1:l, :]
        ea = jnp.exp(a_col)
        dte = jnp.exp(a_end - a_col)

        dx_state = dte * _dot(bv, dhn)
        dxd = _dot(w, dyv, TN) + dx_state
        md = decay * _dot(dyv, xd, NT)
        m = md * cb
        dc_ref[...] += _dot(md, bv) + ea * _dot(dyv, hc, NT)
        db_ref[...] += _dot(md, cv, TN) + dte * _dot(xd, dhn, NT)
        dstate_ref[r] = jnp.exp(a_end) * dhn + _dot(cv, dyv * ea, TN)

        tdt = F32 if CDT == F32 else BF16
        t1 = (isx >= il).astype(tdt)
        yoff = ea * _dot(cv, hc)
        xdx = jnp.sum(xd * dx_state, axis=1, keepdims=True)
        vec = jnp.sum(dyv * yoff, axis=1, keepdims=True) - xdx
        end_term = jnp.sum(xdx, axis=0, keepdims=True) + \
            jnp.exp(a_end) * jnp.sum(jnp.sum(hc * dhn, axis=1, keepdims=True), axis=0, keepdims=True)
        zmat = _split_dot(m, t1, True, 2)
        span = jnp.sum(jnp.where(isx < il, zmat, 0.0), axis=1, keepdims=True)
        rc = _split_dot(jnp.broadcast_to(vec, (l, 128)), t1, True, 2)[:, :1]
        dda = span + rc + end_term
        dxx = jnp.sum(dxd * xv, axis=1, keepdims=True)

        lane = _iota((l, r_), 1) == r
        dda_ref[...] += jnp.where(lane, dda, 0.0)
        dxx_ref[...] += jnp.where(lane, dxx, 0.0)
        dx_ref[...] = dxd * dt_col + dsk_h * dyv
        lane1 = _iota((1, r_), 1) == r
        dah_ref[...] += jnp.where(lane1, jnp.sum(dda * dt_col, axis=0, keepdims=True), 0.0)
        ddsk_ref[...] += jnp.where(
            lane1, jnp.sum(jnp.sum(dyv * xv, axis=1, keepdims=True), axis=0, keepdims=True), 0.0)

    rev = lambda c: nc - 1 - c
    xspec = pl.BlockSpec((None, l, p), lambda g, c, r: (g * r_ + r, rev(c), 0))
    cspec = pl.BlockSpec((None, l, r_), lambda g, c, r: (g, rev(c), 0))
    hspec = pl.BlockSpec((None, 1, r_), lambda g, c, r: (g, 0, 0))
    return pl.pallas_call(
        body, name="ssd_bwd", grid=(g_, nc, r_),
        in_specs=[xspec, cspec, cspec,
                  pl.BlockSpec((None, r_, l), lambda g, c, r: (g, 0, rev(c))),
                  hspec,
                  pl.BlockSpec((l, n), lambda g, c, r: (rev(c), boff + g)),
                  pl.BlockSpec((l, n), lambda g, c, r: (rev(c), coff + g)),
                  pl.BlockSpec((None, None, n, p), lambda g, c, r: (g * r_ + r, rev(c), 0, 0)),
                  xspec],
        out_specs=[xspec, cspec, cspec,
                   pl.BlockSpec((l, n), lambda g, c, r: (rev(c), g)),
                   pl.BlockSpec((l, n), lambda g, c, r: (rev(c), g)),
                   hspec, hspec],
        out_shape=[jax.ShapeDtypeStruct((hh, s, p), F32),
                   jax.ShapeDtypeStruct((g_, s, r_), F32),
                   jax.ShapeDtypeStruct((g_, s, r_), F32),
                   jax.ShapeDtypeStruct((s, g_ * n), F32),
                   jax.ShapeDtypeStruct((s, g_ * n), F32),
                   jax.ShapeDtypeStruct((g_, 1, r_), F32),
                   jax.ShapeDtypeStruct((g_, 1, r_), F32)],
        scratch_shapes=[pltpu.VMEM((r_, n, p), F32), pltpu.VMEM((l, l), F32),
                        pltpu.VMEM((l, r_), F32), pltpu.VMEM((r_, l), F32)],
        compiler_params=_cparams(("parallel", "arbitrary", "arbitrary")),
    )(xh, dtc, dac, dar, dsk, xbc, xbc, st, dy)


def _ssd_chunk_common(c_ref, b_ref, dac_ref, dar_ref):
    l = SSD_L
    tdt = F32 if CDT == F32 else BF16
    lower = (_iota((l, l), 1) <= _iota((l, l), 0)).astype(tdt)
    upper = (_iota((l, l), 0) <= _iota((l, l), 1)).astype(tdt)
    cb = _dot(c_ref[...], b_ref[...], NT)
    return cb, _split_dot(dac_ref[...], lower, True, 3), _split_dot(dar_ref[...], upper, False, 3)


def _ssd_fwd_g(xh, dtc, dac, dar, dsk, xbc):
    hh, s, p = xh.shape
    l, n, g_, r_ = SSD_L, SSD_N, SSD_G, SSD_R
    nc = s // l
    boff = SSD_INNER // n
    coff = boff + g_

    def body(x_ref, dtc_ref, dac_ref, dar_ref, dsk_ref, b_ref, c_ref, y_ref, st_ref, state_ref):
        c = pl.program_id(1)

        @pl.when(c == 0)
        def _():
            state_ref[...] = jnp.zeros_like(state_ref)

        cb, acs_c, acs_r = _ssd_chunk_common(c_ref, b_ref, dac_ref, dar_ref)
        mask = _iota((l, l), 1) <= _iota((l, l), 0)
        cv, bv = c_ref[...], b_ref[...]
        dtcv, dskv = dtc_ref[...], dsk_ref[...]
        for r in range(r_):
            a_col = _pick_lane(acs_c, r)
            a_row = _pick_row(acs_r, r)
            dt_col = _pick_lane(dtcv, r)
            dsk_h = _pick_lane(dskv, r)
            xv = x_ref[r]
            xd = xv * dt_col
            decay = jnp.where(mask, jnp.exp(jnp.minimum(a_col - a_row, 0.0)), 0.0)
            hprev = state_ref[r]
            y = _dot(cb * decay, xd) + jnp.exp(a_col) * _dot(cv, hprev)
            y_ref[r] = y + dsk_h * xv
            a_end = a_col[l - 1:l, :]
            st_ref[r] = hprev
            state_ref[r] = hprev * jnp.exp(a_end) + _dot(bv, xd * jnp.exp(a_end - a_col), TN)

    return pl.pallas_call(
        body, name="ssd_fwd", grid=(g_, nc),
        in_specs=[pl.BlockSpec((r_, l, p), lambda g, c: (g, c, 0)),
                  pl.BlockSpec((None, l, r_), lambda g, c: (g, c, 0)),
                  pl.BlockSpec((None, l, r_), lambda g, c: (g, c, 0)),
                  pl.BlockSpec((None, r_, l), lambda g, c: (g, 0, c)),
                  pl.BlockSpec((None, 1, r_), lambda g, c: (g, 0, 0)),
                  pl.BlockSpec((l, n), lambda g, c: (c, boff + g)),
                  pl.BlockSpec((l, n), lambda g, c: (c, coff + g))],
        out_specs=[pl.BlockSpec((r_, l, p), lambda g, c: (g, c, 0)),
                   pl.BlockSpec((r_, None, n, p), lambda g, c: (g, c, 0, 0))],
        out_shape=[jax.ShapeDtypeStruct((hh, s, p), F32),
                   jax.ShapeDtypeStruct((hh, nc, n, p), F32)],
        scratch_shapes=[pltpu.VMEM((r_, n, p), F32)],
        compiler_params=_cparams(("parallel", "arbitrary")),
    )(xh, dtc, dac, dar, dsk, xbc, xbc)


def _ssd_bwd_g(xh, dtc, dac, dar, dsk, xbc, st, dy):
    hh, s, p = xh.shape
    l, n, g_, r_ = SSD_L, SSD_N, SSD_G, SSD_R
    nc = s // l
    boff = SSD_INNER // n
    coff = boff + g_

    def body(x_ref, dtc_ref, dac_ref, dar_ref, dsk_ref, b_ref, c_ref, st_ref, dy_ref,
             dx_ref, dda_ref, dxx_ref, db_ref, dc_ref, dah_ref, ddsk_ref, dstate_ref):
        c = pl.program_id(1)

        @pl.when(c == 0)
        def _():
            dstate_ref[...] = jnp.zeros_like(dstate_ref)
            dah_ref[...] = jnp.zeros_like(dah_ref)
            ddsk_ref[...] = jnp.zeros_like(ddsk_ref)

        cb, acs_c, acs_r = _ssd_chunk_common(c_ref, b_ref, dac_ref, dar_ref)
        il = _iota((l, l), 0)
        isx = _iota((l, l), 1)
        tdt = F32 if CDT == F32 else BF16
        t1 = (isx >= il).astype(tdt)
        cv, bv = c_ref[...], b_ref[...]
        dtcv, dskv = dtc_ref[...], dsk_ref[...]
        lane = _iota((l, r_), 1)
        lane1 = _iota((1, r_), 1)
        dda_all = jnp.zeros((l, r_), F32)
        dxx_all = jnp.zeros((l, r_), F32)
        dah_all = jnp.zeros((1, r_), F32)
        ddsk_all = jnp.zeros((1, r_), F32)
        db_acc = jnp.zeros((l, n), F32)
        dc_acc = jnp.zeros((l, n), F32)
        md_sum = jnp.zeros((l, l), F32)
        cb_t = _dot(bv, cv, NT)
        cv_t = cv.T
        for r in range(r_):
            a_col = _pick_lane(acs_c, r)
            a_row = _pick_row(acs_r, r)
            dt_col = _pick_lane(dtcv, r)
            dsk_h = _pick_lane(dskv, r)
            xv = x_ref[r]
            dyv = dy_ref[r]
            xd = xv * dt_col
            decay = jnp.where(isx <= il, jnp.exp(jnp.minimum(a_col - a_row, 0.0)), 0.0)
            decay_t = jnp.where(isx >= il, jnp.exp(jnp.minimum(a_row - a_col, 0.0)), 0.0)
            dhn = dstate_ref[r]
            hc = st_ref[r]
            a_end = a_col[l - 1:l, :]
            ea = jnp.exp(a_col)
            dte = jnp.exp(a_end - a_col)

            dx_state = dte * _dot(bv, dhn)
            dxd = _dot(cb_t * decay_t, dyv) + dx_state
            md = decay * _dot(dyv, xd, NT)
            md_sum = md_sum + md
            dc_acc = dc_acc + ea * _dot(dyv, hc, NT)
            db_acc = db_acc + dte * _dot(xd, dhn, NT)
            dstate_ref[r] = jnp.exp(a_end) * dhn + _dot(cv_t, dyv * ea)

            yoff = ea * _dot(cv, hc)
            xdx = jnp.sum(xd * dx_state, axis=1, keepdims=True)
            vec = jnp.sum(dyv * yoff, axis=1, keepdims=True) - xdx
            end_term = jnp.sum(xdx, axis=0, keepdims=True) + \
                jnp.exp(a_end) * jnp.sum(jnp.sum(hc * dhn, axis=1, keepdims=True), axis=0, keepdims=True)
            zmat = _split_dot(md * cb, t1, True, 2)
            span = jnp.sum(jnp.where(isx < il, zmat, 0.0), axis=1, keepdims=True)
            rc = _split_dot(jnp.broadcast_to(vec, (l, 128)), t1, True, 2)[:, :1]
            dda = span + rc + end_term
            dda_all = jnp.where(lane == r, dda, dda_all)
            dxx_all = jnp.where(lane == r, jnp.sum(dxd * xv, axis=1, keepdims=True), dxx_all)
            dx_ref[r] = dxd * dt_col + dsk_h * dyv
            dah_all = jnp.where(lane1 == r, jnp.sum(dda * dt_col, axis=0, keepdims=True), dah_all)
            ddsk_all = jnp.where(
                lane1 == r, jnp.sum(jnp.sum(dyv * xv, axis=1, keepdims=True), axis=0, keepdims=True), ddsk_all)
        dda_ref[...] = dda_all
        dxx_ref[...] = dxx_all
        db_ref[...] = db_acc + _dot(md_sum, cv, TN)
        dc_ref[...] = dc_acc + _dot(md_sum, bv)
        dah_ref[...] += dah_all
        ddsk_ref[...] += ddsk_all

    rev = lambda c: nc - 1 - c
    xspec = pl.BlockSpec((r_, l, p), lambda g, c: (g, rev(c), 0))
    cspec = pl.BlockSpec((None, l, r_), lambda g, c: (g, rev(c), 0))
    hspec = pl.BlockSpec((None, 1, r_), lambda g, c: (g, 0, 0))
    return pl.pallas_call(
        body, name="ssd_bwd", grid=(g_, nc),
        in_specs=[xspec, cspec, cspec,
                  pl.BlockSpec((None, r_, l), lambda g, c: (g, 0, rev(c))),
                  hspec,
                  pl.BlockSpec((l, n), lambda g, c: (rev(c), boff + g)),
                  pl.BlockSpec((l, n), lambda g, c: (rev(c), coff + g)),
                  pl.BlockSpec((r_, None, n, p), lambda g, c: (g, rev(c), 0, 0)),
                  xspec],
        out_specs=[xspec, cspec, cspec,
                   pl.BlockSpec((l, n), lambda g, c: (rev(c), g)),
                   pl.BlockSpec((l, n), lambda g, c: (rev(c), g)),
                   hspec, hspec],
        out_shape=[jax.ShapeDtypeStruct((hh, s, p), F32),
                   jax.ShapeDtypeStruct((g_, s, r_), F32),
                   jax.ShapeDtypeStruct((g_, s, r_), F32),
                   jax.ShapeDtypeStruct((s, g_ * n), F32),
                   jax.ShapeDtypeStruct((s, g_ * n), F32),
                   jax.ShapeDtypeStruct((g_, 1, r_), F32),
                   jax.ShapeDtypeStruct((g_, 1, r_), F32)],
        scratch_shapes=[pltpu.VMEM((r_, n, p), F32)],
        compiler_params=_cparams(("parallel", "arbitrary")),
    )(xh, dtc, dac, dar, dsk, xbc, xbc, st, dy)


CONV_TC = 256
CONV_RC = 512


def _conv_taps(x_ref, head_ref, rc):
    base = CONV_PAD - (CONV_K - 1)
    head_ref[pl.ds(0, CONV_PAD), :] = jnp.zeros((CONV_PAD, head_ref.shape[1]), F32)
    head_ref[pl.ds(CONV_PAD, rc), :] = x_ref[pl.ds(0, rc), :]

    def tap(t0, kk):
        if t0 == 0:
            return head_ref[pl.ds(base + kk, rc), :]
        return x_ref[pl.ds(t0 - (CONV_K - 1) + kk, rc), :]
    return tap


def _conv_fwd(x, w, b):
    s, ch = x.shape
    rc = min(CONV_RC, s)

    def body(x_ref, w_ref, b_ref, pre_ref, act_ref, head_ref):
        wv = w_ref[...]
        tap = _conv_taps(x_ref, head_ref, rc)
        for t0 in range(0, s, rc):
            acc = jnp.broadcast_to(b_ref[...], (rc, CONV_TC))
            for kk in range(CONV_K):
                acc = acc + wv[kk:kk + 1, :] * tap(t0, kk)
            pre_ref[pl.ds(t0, rc), :] = acc
            act_ref[pl.ds(t0, rc), :] = _silu(acc)

    col = pl.BlockSpec((s, CONV_TC), lambda j: (0, j))
    shp = jax.ShapeDtypeStruct((s, ch), F32)
    return pl.pallas_call(
        body, name="conv_fwd", grid=(ch // CONV_TC,),
        in_specs=[col, pl.BlockSpec((CONV_K, CONV_TC), lambda j: (0, j)),
                  pl.BlockSpec((1, CONV_TC), lambda j: (0, j))],
        out_specs=[col, col],
        out_shape=[shp, shp],
        scratch_shapes=[pltpu.VMEM((CONV_PAD + rc, CONV_TC), F32)],
        compiler_params=_cparams(("parallel",)),
    )(x, w, b)


def _conv_bwd(x, pre, dact, w):
    s, ch = x.shape
    rc = min(CONV_RC, s)

    def body(x_ref, pre_ref, da_ref, w_ref, dx_ref, dw_ref, db_ref, dpre_ref, head_ref):
        wv = w_ref[...]
        tap = _conv_taps(x_ref, head_ref, rc)
        for t0 in range(0, s, rc):
            dpre_ref[pl.ds(t0, rc), :] = da_ref[pl.ds(t0, rc), :] * _dsilu(pre_ref[pl.ds(t0, rc), :])
        dpre_ref[pl.ds(s, CONV_PAD), :] = jnp.zeros((CONV_PAD, CONV_TC), F32)
        dws = [jnp.zeros((1, CONV_TC), F32) for _ in range(CONV_K)]
        dbs = jnp.zeros((1, CONV_TC), F32)
        for t0 in range(0, s, rc):
            acc = jnp.zeros((rc, CONV_TC), F32)
            dp = dpre_ref[pl.ds(t0, rc), :]
            for kk in range(CONV_K):
                acc = acc + wv[kk:kk + 1, :] * dpre_ref[pl.ds(t0 + CONV_K - 1 - kk, rc), :]
                dws[kk] = dws[kk] + jnp.sum(dp * tap(t0, kk), axis=0, keepdims=True)
            dbs = dbs + jnp.sum(dp, axis=0, keepdims=True)
            dx_ref[pl.ds(t0, rc), :] = acc.astype(dx_ref.dtype)
        for kk in range(CONV_K):
            dw_ref[kk:kk + 1, :] = dws[kk]
        db_ref[...] = dbs

    col = pl.BlockSpec((s, CONV_TC), lambda j: (0, j))
    return pl.pallas_call(
        body, name="conv_bwd", grid=(ch // CONV_TC,),
        in_specs=[col, col, col, pl.BlockSpec((CONV_K, CONV_TC), lambda j: (0, j))],
        out_specs=[col, pl.BlockSpec((CONV_K, CONV_TC), lambda j: (0, j)),
                   pl.BlockSpec((1, CONV_TC), lambda j: (0, j))],
        out_shape=[jax.ShapeDtypeStruct((s, ch), CDT),
                   jax.ShapeDtypeStruct((CONV_K, ch), F32),
                   jax.ShapeDtypeStruct((1, ch), F32)],
        scratch_shapes=[pltpu.VMEM((s + CONV_PAD, CONV_TC), F32), pltpu.VMEM((CONV_PAD + rc, CONV_TC), F32)],
        compiler_params=_cparams(("parallel",)),
    )(x, pre, dact, w)


MEM_TS = 512


def _mem_fwd(mq, kv):
    s = mq.shape[0]
    m = kv.shape[0]
    ts = min(MEM_TS, s)
    scale = MEM_DH ** -0.5

    def body(q_ref, k_ref, v_ref, o_ref):
        sc = _dot(q_ref[...], k_ref[...], NT) * scale
        e = jnp.exp(sc - jnp.max(sc, axis=1, keepdims=True))
        pr = e / jnp.sum(e, axis=1, keepdims=True)
        o_ref[...] = _dot(pr, v_ref[...]).astype(o_ref.dtype)

    return pl.pallas_call(
        body, name="mem_fwd", grid=(MEM_H, s // ts),
        in_specs=[pl.BlockSpec((ts, MEM_DH), lambda a, i: (i, a)),
                  pl.BlockSpec((m, MEM_DH), lambda a, i: (0, a)),
                  pl.BlockSpec((m, MEM_DH), lambda a, i: (0, MEM_H + a))],
        out_specs=pl.BlockSpec((ts, MEM_DH), lambda a, i: (i, a)),
        out_shape=jax.ShapeDtypeStruct((s, MEM_H * MEM_DH), CDT),
        compiler_params=_cparams(("parallel", "arbitrary")),
    )(mq, kv, kv)


def _mem_bwd(mq, kv, do):
    s = mq.shape[0]
    m = kv.shape[0]
    ts = min(MEM_TS, s)
    scale = MEM_DH ** -0.5

    def body(q_ref, k_ref, v_ref, do_ref, dq_ref, dk_ref, dv_ref):
        i = pl.program_id(1)

        @pl.when(i == 0)
        def _():
            dk_ref[...] = jnp.zeros_like(dk_ref)
            dv_ref[...] = jnp.zeros_like(dv_ref)

        qv, kb, vb, dov = q_ref[...], k_ref[...], v_ref[...], do_ref[...]
        sc = _dot(qv, kb, NT) * scale
        e = jnp.exp(sc - jnp.max(sc, axis=1, keepdims=True))
        pr = e / jnp.sum(e, axis=1, keepdims=True)
        dp = _dot(dov, vb, NT)
        ds = pr * (dp - jnp.sum(dp * pr, axis=1, keepdims=True)) * scale
        dq_ref[...] = _dot(ds, kb).astype(dq_ref.dtype)
        dk_ref[...] += _dot(ds, qv, TN)
        dv_ref[...] += _dot(pr, dov, TN)

    tile = pl.BlockSpec((ts, MEM_DH), lambda a, i: (i, a))
    kvo = pl.BlockSpec((m, MEM_DH), lambda a, i: (0, a))
    return pl.pallas_call(
        body, name="mem_bwd", grid=(MEM_H, s // ts),
        in_specs=[tile, kvo, pl.BlockSpec((m, MEM_DH), lambda a, i: (0, MEM_H + a)), tile],
        out_specs=[tile, kvo, kvo],
        out_shape=[jax.ShapeDtypeStruct((s, MEM_H * MEM_DH), CDT),
                   jax.ShapeDtypeStruct((m, MEM_H * MEM_DH), F32),
                   jax.ShapeDtypeStruct((m, MEM_H * MEM_DH), F32)],
        compiler_params=_cparams(("parallel", "arbitrary")),
    )(mq, kv, kv, do)


def _heads(t, nh, dh):
    return t.reshape(t.shape[0], nh, dh).transpose(1, 0, 2)


def _unheads(t):
    return t.transpose(1, 0, 2).reshape(t.shape[1], t.shape[0] * t.shape[2])


def _group_cols(t):
    return t.reshape(t.shape[0], SSD_G, SSD_R).transpose(1, 0, 2)


def _pad_cols(t, width):
    return jnp.pad(t, ((0, 0), (0, width - t.shape[1])))


def _full_weight(name, gathered):
    if name in COL_SHARDED:
        return gathered.transpose(1, 0, 2).reshape(gathered.shape[1], N_DEV * gathered.shape[2])
    return gathered.reshape(N_DEV * gathered.shape[1], gathered.shape[2])


def _grad_payload(name, g):
    if name in COL_SHARDED:
        return g.reshape(g.shape[0], N_DEV, g.shape[1] // N_DEV).transpose(1, 0, 2)
    return g.reshape(N_DEV, g.shape[0] // N_DEV, g.shape[1])


def _local_step(x, mem, tgt, p, wt, shards=None):
    s, d = x.shape
    wt = dict(wt)
    c1, c2, c3, c4, c5 = 3 * d, 3 * d + SSD_INNER, 3 * d + SSD_INNER + CONV_DIM, \
        3 * d + SSD_INNER + CONV_DIM + SSD_H, 3 * d + SSD_INNER + CONV_DIM + SSD_H + d
    w_in = wt["w_in"]
    w_main = jnp.concatenate([w_in[:, :c3], w_in[:, c4:]], axis=1)
    w_dt = _pad_cols(w_in[:, c3:c4], DT_PAD)
    seg_name = ["qkv", "z", "xbc", "mq", "gl"]
    seg_dtype = [CDT, F32, F32, CDT, F32]
    seg_off = [0, c1, c2, c3, c3 + d]
    seg_n = [c1, c2 - c1, c3 - c2, d, 3 * d]

    u = _rows(lambda xv, g: _rms(xv, g), [x], [p["norm_mix_pre"]], [(d, CDT)], ts=512, name="f_norm_pre")[0]
    qkv, z, xbc_raw, mq, gl = [
        _mm(u, w_main, b_off=seg_off[i], n=seg_n[i], out_dtype=seg_dtype[i], name="f_in_" + seg_name[i])
        for i in range(5)]
    dt_raw = _mm(u, w_dt, name="f_in_dt")

    bias128 = _pad_cols(p["dt_bias"], DT_PAD)
    alog128 = _pad_cols(p["a_log"], DT_PAD)

    def dt_fn(dtr, bias, alog):
        dt = _softplus(dtr + bias)
        return dt, dt * (-jnp.exp(alog))

    dt128, da128 = _rows(dt_fn, [dt_raw], [bias128, alog128], [(DT_PAD, F32), (DT_PAD, F32)],
                         ts=512, name="f_dt")
    dtc = _group_cols(dt128[:, :SSD_H])
    dac = _group_cols(da128[:, :SSD_H])
    dar = dac.transpose(0, 2, 1)
    dsk = p["d_skip"].reshape(SSD_G, 1, SSD_R)

    conv_w, conv_b = p["conv_w"], p["conv_b"]
    pre, xbc = _conv_fwd(xbc_raw, conv_w, conv_b)
    xh = _heads(xbc[:, :SSD_INNER], SSD_H, SSD_P)
    y_h, st = _ssd_fwd_g(xh, dtc, dac, dar, dsk, xbc)
    y_core = _unheads(y_h)

    def group_norm_fwd(yv, zv, wn):
        y2 = yv * _silu(zv)
        gw = SSD_INNER // SSD_G
        outs = []
        for gi in range(SSD_G):
            seg = y2[:, gi * gw:(gi + 1) * gw]
            outs.append(_rms(seg, wn[:, gi * gw:(gi + 1) * gw]))
        return jnp.concatenate(outs, axis=1)

    y_ssd = _rows(group_norm_fwd, [y_core, z], [p["ssd_norm"]], [(SSD_INNER, CDT)], ts=256, name="f_ssd_post")[0]

    y_sb, lt_h, *late = _sb_fwd_pairs(qkv, tuple(shards) if shards is not None else ())
    for n, gth in zip(LATE_W, late):
        wt[n] = _full_weight(n, gth)

    mu = _rows(lambda mv, g: _rms(mv, g), [mem], [p["norm_mem"]], [(d, CDT)], ts=256, name="f_norm_mem")[0]
    kv = _mm(mu, wt["w_mem_kv"], out_dtype=CDT, name="f_mem_kv")
    y_mem = _mem_fwd(mq, kv)

    p_sb = _mm(y_sb, wt["w_sb_out"], name="f_sb_out")
    p_ssd = _mm(y_ssd, wt["w_ssd_out"], name="f_ssd_out")
    p_mem = _mm(y_mem, wt["w_mem_out"], name="f_mem_out")

    def merge_fn(glv, a, b, c):
        return (_sigmoid(glv[:, :d]) * a + _sigmoid(glv[:, d:2 * d]) * b + _sigmoid(glv[:, 2 * d:]) * c)

    merged = _rows(merge_fn, [gl, p_sb, p_ssd, p_mem], [], [(d, CDT)], ts=256, name="f_merge")[0]
    mix = _mm(merged, wt["w_o"], name="f_w_o")

    def mid_fn(xv, mixv, g_post, g_pre):
        h1 = xv + _rms(mixv, g_post)
        return h1, _rms(h1, g_pre)

    h1, u2 = _rows(mid_fn, [x, mix], [p["norm_mix_post"], p["norm_mlp_pre"]], [(d, F32), (d, CDT)],
                   ts=512, name="f_mid")
    a1 = _mm(u2, wt["w_up"], name="f_up")
    act = _rows(lambda a: jnp.square(jnp.maximum(a, 0.0)), [a1], [], [(a1.shape[1], CDT)], ts=256, name="f_act")[0]
    ff = _mm(act, wt["w_down"], name="f_down")

    def loss_fn(h1v, ffv, tv, g):
        diff = h1v + _rms(ffv, g) - tv
        tot = jnp.sum(jnp.sum(diff * diff, axis=1, keepdims=True), axis=0, keepdims=True)
        return diff * (1.0 / d), tot

    dh2, loss_acc = _rows(loss_fn, [h1, ff, tgt], [p["norm_mlp_post"]], [(d, F32)], [(1, 128)],
                          ts=512, name="f_loss")
    loss = loss_acc[:, :1] * (0.5 / d)

    sg = {}

    def b_post(ffv, dyv, g):
        dx, dg = _rms_bwd(ffv, g, dyv)
        return dx, dg

    d_ff, sg["norm_mlp_post"] = _rows(b_post, [ff, dh2], [p["norm_mlp_post"]], [(d, CDT)], [(1, d)],
                                      ts=512, name="b_norm_mlp_post")
    dact = _mm(d_ff, wt["w_down"], tb=True, name="b_down_x")
    gw = {"w_down": _mm(act, d_ff, ta=True, name="b_down_w")}
    da1 = _rows(lambda dv, a: dv * 2.0 * jnp.maximum(a, 0.0), [dact, a1], [], [(a1.shape[1], CDT)],
                ts=256, name="b_act")[0]
    du2 = _mm(da1, wt["w_up"], tb=True, name="b_up_x")
    gw["w_up"] = _mm(u2, da1, ta=True, name="b_up_w")

    def b_mid(h1v, du2v, dh2v, mixv, g_pre, g_post):
        dxa, dga = _rms_bwd(h1v, g_pre, du2v)
        dh1 = dh2v + dxa
        dmix, dgb = _rms_bwd(mixv, g_post, dh1)
        return dh1, dmix, dga, dgb

    dh1, dmix, sg["norm_mlp_pre"], sg["norm_mix_post"] = _rows(
        b_mid, [h1, du2, dh2, mix], [p["norm_mlp_pre"], p["norm_mix_post"]],
        [(d, F32), (d, CDT)], [(1, d), (1, d)], ts=256, name="b_mid")
    dmerged = _mm(dmix, wt["w_o"], tb=True, name="b_w_o_x")
    gw["w_o"] = _mm(merged, dmix, ta=True, name="b_w_o_w")

    def b_merge(dm, glv, a, b, c):
        outs, dgl = [], []
        for i, br in enumerate((a, b, c)):
            gt = _sigmoid(glv[:, i * d:(i + 1) * d])
            outs.append(gt * dm)
            dgl.append(dm * br * gt * (1.0 - gt))
        return outs[0], outs[1], outs[2], jnp.concatenate(dgl, axis=1)

    dp_sb, dp_ssd, dp_mem, dgl = _rows(b_merge, [dmerged, gl, p_sb, p_ssd, p_mem], [],
                                       [(d, CDT), (d, CDT), (d, CDT), (3 * d, CDT)], ts=256, name="b_merge")
    dy_sb = _mm(dp_sb, wt["w_sb_out"], tb=True, name="b_sb_out_x")
    gw["w_sb_out"] = _mm(y_sb, dp_sb, ta=True, name="b_sb_out_w")
    dy_ssd = _mm(dp_ssd, wt["w_ssd_out"], tb=True, name="b_ssd_out_x")
    gw["w_ssd_out"] = _mm(y_ssd, dp_ssd, ta=True, name="b_ssd_out_w")
    dy_mem = _mm(dp_mem, wt["w_mem_out"], tb=True, out_dtype=CDT, name="b_mem_out_x")
    gw["w_mem_out"] = _mm(y_mem, dp_mem, ta=True, name="b_mem_out_w")

    dmq, dk_m, dv_m = _mem_bwd(mq, kv, dy_mem)
    dkv = jnp.concatenate([dk_m, dv_m], axis=1).astype(CDT)
    gw["w_mem_kv"] = _mm(mu, dkv, ta=True, name="b_mem_kv_w")
    dmu = _mm(dkv, wt["w_mem_kv"], tb=True, name="b_mem_kv_x")
    sg["norm_mem"] = _rows(lambda mv, dv, g: _rms_bwd(mv, g, dv)[1], [mem, dmu], [p["norm_mem"]], [], [(1, d)],
                           ts=256, name="b_norm_mem")[0]

    payloads = tuple(_grad_payload(n, gw[n]) for n in LATE_W) if shards is not None else ()
    dq, dk, dv, *received = _sb_bwd_pairs(qkv, lt_h, dy_sb, payloads)
    dqkv = jnp.concatenate([dq, dk, dv], axis=1).astype(CDT)

    def group_norm_bwd(dyo, yv, zv, wn):
        sz = _silu(zv)
        y2 = yv * sz
        gw_ = SSD_INNER // SSD_G
        dy2, dwn = [], []
        for gi in range(SSD_G):
            sl = slice(gi * gw_, (gi + 1) * gw_)
            dxs, dgs = _rms_bwd(y2[:, sl], wn[:, sl], dyo[:, sl])
            dy2.append(dxs)
            dwn.append(dgs)
        dy2 = jnp.concatenate(dy2, axis=1)
        return dy2 * sz, dy2 * yv * _dsilu(zv), jnp.concatenate(dwn, axis=1)

    dy_core, dz, sg["ssd_norm"] = _rows(group_norm_bwd, [dy_ssd, y_core, z], [p["ssd_norm"]],
                                        [(SSD_INNER, F32), (SSD_INNER, CDT)], [(1, SSD_INNER)],
                                        ts=256, name="b_ssd_post")
    dxh, dda, dxx, d_b, d_c, dah, ddsk = _ssd_bwd_g(xh, dtc, dac, dar, dsk, xbc, st, _heads(dy_core, SSD_H, SSD_P))
    sg["d_skip"] = ddsk.reshape(1, SSD_H)
    sg["a_log"] = dah.reshape(1, SSD_H) * (-jnp.exp(p["a_log"]))

    def b_dt(ddav, dxxv, dtr, bias, alog):
        ddt = ddav * (-jnp.exp(alog)) + dxxv
        draw = ddt * _sigmoid(dtr + bias)
        return draw, jnp.sum(draw, axis=0, keepdims=True)

    ungroup = lambda t: _pad_cols(t.transpose(1, 0, 2).reshape(s, SSD_H), DT_PAD)
    ddt_raw, dbias128 = _rows(b_dt, [ungroup(dda), ungroup(dxx), dt_raw], [bias128, alog128],
                              [(DT_PAD, CDT)], [(1, DT_PAD)], ts=512, name="b_dt")
    sg["dt_bias"] = dbias128[:, :SSD_H]

    dxbc = jnp.concatenate([_unheads(dxh), d_b, d_c], axis=1)
    dxbc_raw, sg["conv_w"], sg["conv_b"] = _conv_bwd(xbc_raw, pre, dxbc, conv_w)

    dseg = [dqkv, dz, dxbc_raw, dmq, dgl]
    dw_main = lax.empty((d, w_main.shape[1]), F32)
    for i in range(5):
        dw_main = _mm(u, dseg[i], ta=True, into=dw_main, into_off=seg_off[i], name="b_in_w_" + seg_name[i])
    dw_dt = _mm(u, ddt_raw, ta=True, name="b_in_w_dt")
    half = w_main.shape[1] // 2
    hosts = {0: _pair_cols([(dw_main, 0, half), (dw_dt, 0, DT_PAD)]),
             2: _pair_cols([(dw_main, half, half)])} if shards is not None else {}
    dus, from_sibling = [], []
    for i in range(5):
        res = _mm(dseg[i], w_main, tb=True, b_koff=seg_off[i], name="b_in_x_" + seg_name[i], hosted=hosts.get(i))
        if i in hosts:
            dus.append(res[0])
            from_sibling += res[1:]
        else:
            dus.append(res)
    dus.append(_mm(ddt_raw, w_dt, tb=True, name="b_in_x_dt"))
    if shards is None:
        gw["w_in"] = jnp.concatenate([dw_main[:, :c3], dw_dt[:, :SSD_H], dw_main[:, c3:]], axis=1)
    else:
        sum_a = _add_cols(dw_main, 0, from_sibling[0], "sum_w_in_a")
        sum_dt = _add_cols(dw_dt, 0, from_sibling[1], "sum_w_in_dt")
        sum_b = _add_cols(dw_main, half, from_sibling[2], "sum_w_in_b")
        natural = jnp.concatenate([sum_a, sum_b[:, :c3 - half], sum_dt[:, :SSD_H], sum_b[:, c3 - half:]], axis=1)
        gw["w_in"] = _pad_cols(natural, _window_cols(natural.shape[1]))

    def b_pre(xv, dh1v, d0, d1, d2, d3, d4, d5, g):
        dx, dg = _rms_bwd(xv, g, d0 + d1 + d2 + d3 + d4 + d5)
        return dh1v + dx, dg

    grad_x, sg["norm_mix_pre"] = _rows(b_pre, [x, dh1] + dus, [p["norm_mix_pre"]], [(d, F32)], [(1, d)],
                                       ts=256, name="b_norm_pre")
    return loss, grad_x, gw, sg, (received if shards is not None else None)


HBM = pl.BlockSpec(memory_space=pltpu.HBM)
MESH = pl.DeviceIdType.MESH


def _me_and_peers():
    x, y, c = lax.axis_index("x"), lax.axis_index("y"), lax.axis_index("c")
    me = 4 * x + 2 * y + c
    peers = [(x, y, 1 - c), (1 - x, y, c), (x, 1 - y, c), (1 - x, 1 - y, c),
             (1 - x, y, 1 - c), (x, 1 - y, 1 - c), (1 - x, 1 - y, 1 - c)]
    return me, peers


def _peer_index(peer):
    return 4 * peer[0] + 2 * peer[1] + peer[2]


def _exchange_copies(ins, outs, send_sems, recv_sems, local_sems, scatter):
    me, peers = _me_and_peers()
    copies = []
    for a in range(len(ins)):
        own = ins[a].at[me] if scatter else ins[a]
        copies.append(pltpu.make_async_copy(own, outs[a].at[me], local_sems.at[a]))
        for kk, peer in enumerate(peers):
            src = ins[a].at[_peer_index(peer)] if scatter else ins[a]
            copies.append(pltpu.make_async_remote_copy(
                src_ref=src, dst_ref=outs[a].at[me],
                send_sem=send_sems.at[a, kk], recv_sem=recv_sems.at[a, kk],
                device_id=peer, device_id_type=MESH))
    return copies


def _exchange_shapes(ins, scatter):
    return [jax.ShapeDtypeStruct(t.shape if scatter else (N_DEV,) + t.shape, t.dtype) for t in ins]


def _exchange_sems(n):
    return [pltpu.SemaphoreType.DMA((n, N_DEV - 1)), pltpu.SemaphoreType.DMA((n, N_DEV - 1)),
            pltpu.SemaphoreType.DMA((n,))]


def _exchange(ins, scatter, name):
    n = len(ins)

    def body(*refs):
        copies = _exchange_copies(refs[:n], refs[n:2 * n], *refs[2 * n:], scatter)
        for cp in copies:
            cp.start()
        for cp in copies:
            cp.wait()

    return pl.pallas_call(
        body, name=name,
        in_specs=[HBM] * n, out_specs=[HBM] * n,
        out_shape=_exchange_shapes(ins, scatter),
        scratch_shapes=_exchange_sems(n),
        compiler_params=pltpu.CompilerParams(has_side_effects=True),
    )(*ins)


def _gather_two_level(shards, name):
    n = len(shards)

    def body(*refs):
        ins, outs = refs[:n], refs[n:2 * n]
        send_sems, recv_sems, local_sems = refs[2 * n:]
        x, y, c = lax.axis_index("x"), lax.axis_index("y"), lax.axis_index("c")
        me, sib = (x, y, c), (x, y, 1 - c)
        chips = [(1 - x, y), (x, 1 - y), (1 - x, 1 - y)]

        def copy(a, k, block, to, src=None):
            slot = outs[a].at[_peer_index(block)]
            return pltpu.make_async_remote_copy(
                src_ref=slot if src is None else src, dst_ref=slot,
                send_sem=send_sems.at[a, k], recv_sem=recv_sems.at[a, k], device_id=to, device_id_type=MESH)

        own = [pltpu.make_async_copy(ins[a], outs[a].at[_peer_index(me)], local_sems.at[a]) for a in range(n)]
        first = []
        for a in range(n):
            first.append(copy(a, 0, me, sib, src=ins[a]))
            first += [copy(a, 1 + j, me, (*chip, c), src=ins[a]) for j, chip in enumerate(chips)]
        for cp in own + first:
            cp.start()
        passed = []
        for j, chip in enumerate(chips):
            for a in range(n):
                copy(a, 1 + j, (*chip, c), me).wait_recv()
                fwd = copy(a, 4 + j, (*chip, c), sib)
                fwd.start()
                passed.append(fwd)
        for a in range(n):
            copy(a, 0, sib, me).wait_recv()
            for j, chip in enumerate(chips):
                copy(a, 4 + j, (*chip, 1 - c), me).wait_recv()
        for cp in first + passed:
            cp.wait_send()
        for cp in own:
            cp.wait()

    return pl.pallas_call(
        body, name=name,
        in_specs=[HBM] * n, out_specs=[HBM] * n,
        out_shape=_exchange_shapes(shards, False),
        scratch_shapes=_exchange_sems(n),
        compiler_params=pltpu.CompilerParams(has_side_effects=True),
    )(*shards)


class _Hosted:
    def __init__(self, ins, shapes, sems, copies):
        self.ins, self.shapes, self.sems, self.copies = ins, shapes, sems, copies


def _pair_cols(pieces):
    n = len(pieces)

    def copies(in_refs, out_refs, sems):
        sib = (lax.axis_index("x"), lax.axis_index("y"), 1 - lax.axis_index("c"))
        return [pltpu.make_async_remote_copy(
            src_ref=in_refs[i].at[:, pl.ds(c0, w)], dst_ref=out_refs[i], send_sem=sems[0].at[i],
            recv_sem=sems[1].at[i], device_id=sib, device_id_type=MESH) for i, (_, c0, w) in enumerate(pieces)]

    return _Hosted([t for t, _, _ in pieces],
                   [jax.ShapeDtypeStruct((t.shape[0], w), t.dtype) for t, _, w in pieces],
                   [pltpu.SemaphoreType.DMA((n,)), pltpu.SemaphoreType.DMA((n,))], copies)


def _add_cols(a, c0, b, name):
    r, w = b.shape
    assert c0 % w == 0
    tr = _pick(r, (256, 128))

    def body(a_ref, b_ref, o_ref):
        o_ref[...] = (a_ref[...] + b_ref[...]).astype(o_ref.dtype)

    return pl.pallas_call(
        body, name=name, grid=(r // tr,),
        in_specs=[pl.BlockSpec((tr, w), lambda i: (i, c0 // w)), pl.BlockSpec((tr, w), lambda i: (i, 0))],
        out_specs=pl.BlockSpec((tr, w), lambda i: (i, 0)),
        out_shape=jax.ShapeDtypeStruct((r, w), CDT),
        compiler_params=_cparams(("parallel",)),
    )(a, b)


N_CHIP = 4


def _chip_scatter(t, name):
    def body(t_ref, o_ref, send_sems, recv_sems, local_sem):
        x, y, c = lax.axis_index("x"), lax.axis_index("y"), lax.axis_index("c")
        mine = 2 * x + y
        copies = [pltpu.make_async_copy(t_ref.at[mine], o_ref.at[mine], local_sem)]
        for j, (px, py) in enumerate([(1 - x, y), (x, 1 - y), (1 - x, 1 - y)]):
            copies.append(pltpu.make_async_remote_copy(
                src_ref=t_ref.at[2 * px + py], dst_ref=o_ref.at[mine],
                send_sem=send_sems.at[j], recv_sem=recv_sems.at[j], device_id=(px, py, c), device_id_type=MESH))
        for cp in copies:
            cp.start()
        for cp in copies:
            cp.wait()

    return pl.pallas_call(
        body, name=name, in_specs=[HBM], out_specs=HBM,
        out_shape=jax.ShapeDtypeStruct(t.shape, t.dtype),
        scratch_shapes=[pltpu.SemaphoreType.DMA((N_CHIP - 1,)), pltpu.SemaphoreType.DMA((N_CHIP - 1,)),
                        pltpu.SemaphoreType.DMA],
        compiler_params=pltpu.CompilerParams(has_side_effects=True),
    )(t)


def _all_reduce_small(v, name):
    r, c = v.shape

    def body(v_ref, o_ref, buf, send_sems, recv_sems):
        me, peers = _me_and_peers()
        buf[me] = v_ref[...]
        copies = []
        for kk, peer in enumerate(peers):
            cp = pltpu.make_async_remote_copy(
                src_ref=v_ref, dst_ref=buf.at[me],
                send_sem=send_sems.at[kk], recv_sem=recv_sems.at[kk],
                device_id=peer, device_id_type=MESH)
            cp.start()
            copies.append(cp)
        for cp in copies:
            cp.wait()
        acc = buf[0]
        for i in range(1, N_DEV):
            acc = acc + buf[i]
        o_ref[...] = acc

    return pl.pallas_call(
        body, name=name,
        in_specs=[pl.BlockSpec(memory_space=pltpu.VMEM)],
        out_specs=pl.BlockSpec(memory_space=pltpu.VMEM),
        out_shape=jax.ShapeDtypeStruct((r, c), F32),
        scratch_shapes=[pltpu.VMEM((N_DEV, r, c), F32),
                        pltpu.SemaphoreType.DMA((N_DEV - 1,)), pltpu.SemaphoreType.DMA((N_DEV - 1,))],
        compiler_params=pltpu.CompilerParams(has_side_effects=True),
    )(v)


def _adamw_math(g, w, m, v):
    m2 = ADAM_B1 * m + (1.0 - ADAM_B1) * g
    v2 = ADAM_B2 * v + (1.0 - ADAM_B2) * jnp.square(g)
    m_hat = m2 / (1.0 - ADAM_B1 ** ADAM_STEP)
    v_hat = v2 / (1.0 - ADAM_B2 ** ADAM_STEP)
    delta = -ADAM_LR * (m_hat / (jnp.sqrt(v_hat) + ADAM_EPS) + ADAM_WD * w)
    return delta, m2, v2


def _adamw_reduce(parts, w, m, v, name):
    r, c = w.shape
    nparts = parts.shape[0]
    tr = _pick(r, (128, 64, 32, 16, 8))

    def body(p_ref, w_ref, m_ref, v_ref, g_ref, d_ref, m2_ref, v2_ref):
        g = p_ref[0].astype(F32)
        for i in range(1, nparts):
            g = g + p_ref[i].astype(F32)
        delta, m2, v2 = _adamw_math(g, w_ref[...], m_ref[...], v_ref[...])
        g_ref[...] = g
        d_ref[...] = delta
        m2_ref[...] = m2
        v2_ref[...] = v2

    tile = pl.BlockSpec((tr, c), lambda i: (i, 0))
    shp = jax.ShapeDtypeStruct((r, c), F32)
    return pl.pallas_call(
        body, name=name, grid=(r // tr,),
        in_specs=[pl.BlockSpec((nparts, tr, c), lambda i: (0, i, 0)), tile, tile, tile],
        out_specs=[tile] * 4, out_shape=[shp] * 4,
        compiler_params=_cparams(("parallel",)),
    )(parts, w, m, v)


def _window(shard):
    assert (N_DEV - 1) * (shard % LANES) < LANES
    return -(-((N_DEV - 1) * (shard % LANES) + shard) // LANES) * LANES


def _window_cols(total):
    shard = total // N_DEV
    return ((N_DEV - 1) * shard // LANES) * LANES + _window(shard)


def _chip_scatter_windows(t, shard, name):
    rows = t.shape[0]
    win = _window(shard)

    def body(t_ref, o_ref, send_sems, recv_sems, local_sem):
        x, y, c = lax.axis_index("x"), lax.axis_index("y"), lax.axis_index("c")
        mine = 2 * x + y

        def window(q):
            a0 = pl.multiple_of(((2 * q + c) * shard // LANES) * LANES, LANES)
            return t_ref.at[:, pl.ds(a0, win)]

        copies = [pltpu.make_async_copy(window(mine), o_ref.at[mine], local_sem)]
        for j, (px, py) in enumerate([(1 - x, y), (x, 1 - y), (1 - x, 1 - y)]):
            copies.append(pltpu.make_async_remote_copy(
                src_ref=window(2 * px + py), dst_ref=o_ref.at[mine],
                send_sem=send_sems.at[j], recv_sem=recv_sems.at[j], device_id=(px, py, c), device_id_type=MESH))
        for cp in copies:
            cp.start()
        for cp in copies:
            cp.wait()

    return pl.pallas_call(
        body, name=name, in_specs=[HBM], out_specs=HBM,
        out_shape=jax.ShapeDtypeStruct((N_CHIP, rows, win), t.dtype),
        scratch_shapes=[pltpu.SemaphoreType.DMA((N_CHIP - 1,)), pltpu.SemaphoreType.DMA((N_CHIP - 1,)),
                        pltpu.SemaphoreType.DMA],
        compiler_params=pltpu.CompilerParams(has_side_effects=True),
    )(t)


def _adamw_windows(parts, w, m, v, name):
    r, c = w.shape
    nparts, _, win = parts.shape
    tr = _pick(r, (128, 64, 32, 16, 8))

    def body(p_ref, w_ref, m_ref, v_ref, g_ref, d_ref, m2_ref, v2_ref):
        me, _ = _me_and_peers()
        gw_ = p_ref[0].astype(F32)
        for i in range(1, nparts):
            gw_ = gw_ + p_ref[i].astype(F32)
        for j in range(N_DEV):
            @pl.when(me == j)
            def _():
                lead = (j * c) % LANES
                g = (pltpu.roll(gw_, win - lead, 1) if lead else gw_)[:, :c]
                delta, m2, v2 = _adamw_math(g, w_ref[...], m_ref[...], v_ref[...])
                g_ref[...] = g
                d_ref[...] = delta
                m2_ref[...] = m2
                v2_ref[...] = v2

    tile = pl.BlockSpec((tr, c), lambda i: (i, 0))
    shp = jax.ShapeDtypeStruct((r, c), F32)
    return pl.pallas_call(
        body, name=name, grid=(r // tr,),
        in_specs=[pl.BlockSpec((nparts, tr, win), lambda i: (0, i, 0)), tile, tile, tile],
        out_specs=[tile] * 4, out_shape=[shp] * 4,
        compiler_params=_cparams(("parallel",)),
    )(parts, w, m, v)


def _adamw_plain(g, w, m, v, name):
    def body(g_ref, w_ref, m_ref, v_ref, d_ref, m2_ref, v2_ref):
        delta, m2, v2 = _adamw_math(g_ref[...], w_ref[...], m_ref[...], v_ref[...])
        d_ref[...] = delta
        m2_ref[...] = m2
        v2_ref[...] = v2

    spec = pl.BlockSpec(memory_space=pltpu.VMEM)
    shp = jax.ShapeDtypeStruct(g.shape, F32)
    return pl.pallas_call(
        body, name=name, in_specs=[spec] * 4, out_specs=[spec] * 3, out_shape=[shp] * 3,
    )(g, w, m, v)


SMALL_ROWS, SMALL_COLS = 16, 3072


def _small_step(sg, gcw, loss, ws, ms, vs):
    ns = len(sg)
    widths = [t.shape[1] for t in sg]
    kk_, ch = gcw.shape[1], gcw.shape[2]
    assert ns < SMALL_ROWS and max(widths) <= SMALL_COLS

    def reduce_body(*refs):
        g_refs = refs[:ns]
        gcw_ref, loss_ref, tot_ref, totc_ref = refs[ns:ns + 4]
        mine, buf, minec, bufc, send_sems, recv_sems = refs[ns + 4:]
        me, peers = _me_and_peers()

        mine[...] = jnp.zeros_like(mine)
        for i in range(ns):
            mine[i:i + 1, 0:widths[i]] = g_refs[i][...]
        mine[ns:ns + 1, 0:LANES] = jnp.broadcast_to(loss_ref[...], (1, LANES))
        minec[...] = gcw_ref[...]
        buf[me] = mine[...]
        bufc[me] = minec[...]
        copies = []
        for j, peer in enumerate(peers):
            copies.append(pltpu.make_async_remote_copy(
                src_ref=mine, dst_ref=buf.at[me], send_sem=send_sems.at[0, j], recv_sem=recv_sems.at[0, j],
                device_id=peer, device_id_type=MESH))
            copies.append(pltpu.make_async_remote_copy(
                src_ref=minec, dst_ref=bufc.at[me], send_sem=send_sems.at[1, j], recv_sem=recv_sems.at[1, j],
                device_id=peer, device_id_type=MESH))
        for cp in copies:
            cp.start()
        for cp in copies:
            cp.wait()
        tot = buf[0]
        totc = bufc[0]
        for i in range(1, N_DEV):
            tot = tot + buf[i]
            totc = totc + bufc[i]
        tot_ref[...] = tot
        totc_ref[...] = totc

    vm = pl.BlockSpec(memory_space=pltpu.VMEM)
    tot, totc = pl.pallas_call(
        reduce_body, name="small_reduce",
        in_specs=[vm] * (ns + 2), out_specs=[vm, vm],
        out_shape=[jax.ShapeDtypeStruct((SMALL_ROWS, SMALL_COLS), F32), jax.ShapeDtypeStruct((N_DEV, kk_, ch), F32)],
        scratch_shapes=[pltpu.VMEM((SMALL_ROWS, SMALL_COLS), F32), pltpu.VMEM((N_DEV, SMALL_ROWS, SMALL_COLS), F32),
                        pltpu.VMEM((N_DEV, kk_, ch), F32), pltpu.VMEM((N_DEV, N_DEV, kk_, ch), F32),
                        pltpu.SemaphoreType.DMA((2, N_DEV - 1)), pltpu.SemaphoreType.DMA((2, N_DEV - 1))],
        compiler_params=pltpu.CompilerParams(has_side_effects=True),
    )(*sg, gcw, loss)

    def adamw_body(*refs):
        tot_ref, totc_ref = refs[:2]
        w_refs, m_refs, v_refs = (refs[2 + i * (ns + 1):2 + (i + 1) * (ns + 1)] for i in range(3))
        outs = refs[3 * ns + 5:]
        loss_out = outs[0]
        go, do_, mo, vo = (outs[1 + i * (ns + 1):1 + (i + 1) * (ns + 1)] for i in range(4))
        me, _ = _me_and_peers()
        loss_out[...] = tot_ref[ns:ns + 1, 0:1]
        for i in range(ns + 1):
            g = tot_ref[i:i + 1, 0:widths[i]] if i < ns else totc_ref[me]
            delta, m2, v2 = _adamw_math(g, w_refs[i][...], m_refs[i][...], v_refs[i][...])
            go[i][...] = g
            do_[i][...] = delta
            mo[i][...] = m2
            vo[i][...] = v2

    shapes = [jax.ShapeDtypeStruct(t.shape, F32) for t in ws]
    res = pl.pallas_call(
        adamw_body, name="small_adamw",
        in_specs=[vm] * (3 * ns + 5), out_specs=[vm] * (4 * ns + 5),
        out_shape=[jax.ShapeDtypeStruct((1, 1), F32)] + shapes * 4,
    )(tot, totc, *ws, *ms, *vs)
    n1 = ns + 1
    return res[0], res[1:1 + n1], res[1 + n1:1 + 2 * n1], res[1 + 2 * n1:1 + 3 * n1], res[1 + 3 * n1:]


def _cast_shard(w, name):
    r = w.shape[0]
    return _rows(lambda t: t, [w], [], [(w.shape[1], CDT)], ts=_pick(r, (256, 128)), name=name)[0]


BIG = ["w_in", "w_mem_kv", "w_up", "w_sb_out", "w_ssd_out", "w_mem_out", "w_o", "w_down"]
LATE_W = BIG[1:]
COL_SHARDED = ("w_in", "w_mem_kv", "w_up")
SMALL = ["norm_mix_pre", "conv_b", "dt_bias", "a_log", "d_skip", "ssd_norm", "norm_mem",
         "norm_mix_post", "norm_mlp_pre", "norm_mlp_post"]
ALL_W = ["norm_mix_pre", "w_in", "conv_w", "conv_b", "dt_bias", "a_log", "d_skip", "ssd_norm", "norm_mem",
         "w_mem_kv", "w_sb_out", "w_ssd_out", "w_mem_out", "w_o", "norm_mix_post", "norm_mlp_pre", "w_up",
         "w_down", "norm_mlp_post"]
LANES = 128


def _pack_rows(vecs):
    parts, offs, off = [], [], 0
    for t in vecs:
        flat = t.reshape(-1)
        n = flat.shape[0]
        rows = -(-n // (8 * LANES)) * 8
        parts.append(jnp.pad(flat, (0, rows * LANES - n)).reshape(rows, LANES))
        offs.append((off, n))
        off += rows
    return jnp.concatenate(parts, axis=0), offs


def _unpack_rows(packed, offs, shapes):
    out = []
    for (off, n), shape in zip(offs, shapes):
        rows = -(-n // (8 * LANES)) * 8
        out.append(packed[off:off + rows].reshape(-1)[:n].reshape(shape))
    return out


def kernel(x, mem, norm_mix_pre, w_in, conv_w, conv_b, dt_bias, a_log, d_skip, ssd_norm, norm_mem, w_mem_kv, w_sb_out, w_ssd_out, w_mem_out, w_o, norm_mix_post, norm_mlp_pre, w_up, w_down, norm_mlp_post, loss_target, m_norm_mix_pre, m_w_in, m_conv_w, m_conv_b, m_dt_bias, m_a_log, m_d_skip, m_ssd_norm, m_norm_mem, m_w_mem_kv, m_w_sb_out, m_w_ssd_out, m_w_mem_out, m_w_o, m_norm_mix_post, m_norm_mlp_pre, m_w_up, m_w_down, m_norm_mlp_post, v_norm_mix_pre, v_w_in, v_conv_w, v_conv_b, v_dt_bias, v_a_log, v_d_skip, v_ssd_norm, v_norm_mem, v_w_mem_kv, v_w_sb_out, v_w_ssd_out, v_w_mem_out, v_w_o, v_norm_mix_post, v_norm_mlp_pre, v_w_up, v_w_down, v_norm_mlp_post):
    wd = dict(norm_mix_pre=norm_mix_pre, w_in=w_in, conv_w=conv_w, conv_b=conv_b, dt_bias=dt_bias, a_log=a_log,
              d_skip=d_skip, ssd_norm=ssd_norm, norm_mem=norm_mem, w_mem_kv=w_mem_kv, w_sb_out=w_sb_out,
              w_ssd_out=w_ssd_out, w_mem_out=w_mem_out, w_o=w_o, norm_mix_post=norm_mix_post,
              norm_mlp_pre=norm_mlp_pre, w_up=w_up, w_down=w_down, norm_mlp_post=norm_mlp_post)
    md = dict(norm_mix_pre=m_norm_mix_pre, w_in=m_w_in, conv_w=m_conv_w, conv_b=m_conv_b, dt_bias=m_dt_bias,
              a_log=m_a_log, d_skip=m_d_skip, ssd_norm=m_ssd_norm, norm_mem=m_norm_mem, w_mem_kv=m_w_mem_kv,
              w_sb_out=m_w_sb_out, w_ssd_out=m_w_ssd_out, w_mem_out=m_w_mem_out, w_o=m_w_o,
              norm_mix_post=m_norm_mix_post, norm_mlp_pre=m_norm_mlp_pre, w_up=m_w_up, w_down=m_w_down,
              norm_mlp_post=m_norm_mlp_post)
    vd = dict(norm_mix_pre=v_norm_mix_pre, w_in=v_w_in, conv_w=v_conv_w, conv_b=v_conv_b, dt_bias=v_dt_bias,
              a_log=v_a_log, d_skip=v_d_skip, ssd_norm=v_ssd_norm, norm_mem=v_norm_mem, w_mem_kv=v_w_mem_kv,
              w_sb_out=v_w_sb_out, w_ssd_out=v_w_ssd_out, w_mem_out=v_w_mem_out, w_o=v_w_o,
              norm_mix_post=v_norm_mix_post, norm_mlp_pre=v_norm_mlp_pre, w_up=v_w_up, w_down=v_w_down,
              norm_mlp_post=v_norm_mlp_post)
    shards = {n: _cast_shard(wd[n][0], "cast_" + n) for n in BIG}
    w_in_g, conv_w_g = _gather_two_level([shards["w_in"], wd["conv_w"][0]], "gather_w_in")
    wt = {"w_in": _full_weight("w_in", w_in_g)}
    ch = conv_w_g.shape[2]

    p = {n: wd[n] for n in SMALL}
    p["conv_w"] = conv_w_g.transpose(1, 0, 2).reshape(CONV_K, N_DEV * ch)
    loss, grad_x, gw, sg, late_received = _local_step(x[0], mem[0], loss_target[0], p, wt,
                                                      [shards[n] for n in LATE_W])

    received = dict(zip(LATE_W, late_received))
    w_in_windows = _chip_scatter_windows(gw["w_in"], wd["w_in"].shape[2], "scatter_w_in_chips")

    grads, deltas, new_m, new_v = {}, {}, {}, {}
    for n in BIG:
        if n == "w_in":
            g, dl, m2, v2 = _adamw_windows(w_in_windows, wd[n][0], md[n][0], vd[n][0], "adamw_" + n)
        else:
            g, dl, m2, v2 = _adamw_reduce(received[n], wd[n][0], md[n][0], vd[n][0], "adamw_" + n)
        grads[n], deltas[n], new_m[n], new_v[n] = g[None], dl[None], m2[None], v2[None]
    small_names = SMALL + ["conv_w"]
    gcw = sg["conv_w"].reshape(CONV_K, N_DEV, ch).transpose(1, 0, 2)
    small_of = lambda dct: [dct[n] for n in SMALL] + [dct["conv_w"][0]]
    loss_red, g_s, d_s, m_s, v_s = _small_step([sg[n] for n in SMALL], gcw, loss, small_of(wd), small_of(md),
                                               small_of(vd))
    for i, n in enumerate(small_names):
        shape = wd[n].shape
        grads[n], deltas[n], new_m[n], new_v[n] = (t.reshape(shape) for t in (g_s[i], d_s[i], m_s[i], v_s[i]))
    loss_out = loss_red.reshape(())

    return (loss_out, grad_x[None], *[grads[n] for n in ALL_W], *[deltas[n] for n in ALL_W],
            *[new_m[n] for n in ALL_W], *[new_v[n] for n in ALL_W])
```

```python
import functools

import jax
import jax.numpy as jnp
from jax import lax
from jax.experimental import pallas as pl
from jax.experimental.pallas import tpu as pltpu

F32 = jnp.float32
BF16 = jnp.bfloat16
CDT = jnp.bfloat16
EPS = 1e-6
VMEM_LIMIT = 56 * 1024 * 1024

N_DEV = 8
D_MODEL = 1024
SB_H, SB_DH = 16, 64
SSD_G, SSD_R, SSD_P, SSD_N, SSD_L = 4, 8, 64, 128, 128
SSD_H = SSD_G * SSD_R
SSD_INNER = SSD_H * SSD_P
CONV_K = 4
CONV_DIM = SSD_INNER + 2 * SSD_G * SSD_N
MEM_H, MEM_DH = 4, 256
DT_PAD = 128
SB_TQ, SB_BK = 1024, 256
CONV_PAD = 8
MM_TILE, MM_TILE_K = 1024, 2048

ADAM_LR, ADAM_B1, ADAM_B2, ADAM_EPS, ADAM_WD, ADAM_STEP = 0.001, 0.9, 0.999, 1e-08, 0.01, 10

NT = (((1,), (1,)), ((), ()))
TN = (((0,), (0,)), ((), ()))
NN = (((1,), (0,)), ((), ()))


def _cparams(sem=None):
    return pltpu.CompilerParams(dimension_semantics=sem, vmem_limit_bytes=VMEM_LIMIT)


def _pick(n, cands):
    for c in cands:
        if n % c == 0:
            return c
    return n


def _dot(a, b, dims=NN):
    return lax.dot_general(a.astype(CDT), b.astype(CDT), dims, preferred_element_type=F32)


def _split_dot(x, t, left, pieces):
    if CDT == F32:
        return lax.dot_general(t, x, NN, preferred_element_type=F32) if left else \
            lax.dot_general(x, t, NN, preferred_element_type=F32)
    acc = None
    rem = x
    for _ in range(pieces):
        hi = rem.astype(BF16)
        rem = rem - hi.astype(F32)
        d = lax.dot_general(t, hi, NN, preferred_element_type=F32) if left else \
            lax.dot_general(hi, t, NN, preferred_element_type=F32)
        acc = d if acc is None else acc + d
    return acc


def _iota(shape, dim):
    return lax.broadcasted_iota(jnp.int32, shape, dim)


def _sigmoid(x):
    return 1.0 / (1.0 + jnp.exp(-x))


def _silu(x):
    return x * _sigmoid(x)


def _dsilu(x):
    s = _sigmoid(x)
    return s * (1.0 + x * (1.0 - s))


def _softplus(x):
    return jnp.maximum(x, 0.0) + jnp.log(1.0 + jnp.exp(-jnp.abs(x)))


def _rms(x, g):
    r = lax.rsqrt(jnp.mean(x * x, axis=-1, keepdims=True) + EPS)
    return x * r * g


def _rms_bwd(x, g, dy):
    r = lax.rsqrt(jnp.mean(x * x, axis=-1, keepdims=True) + EPS)
    n = x * r
    dn = dy * g
    dx = r * (dn - n * jnp.mean(dn * n, axis=-1, keepdims=True))
    dg = jnp.sum(dy * n, axis=0, keepdims=True)
    return dx, dg


def _mm(a, b, *, ta=False, tb=False, out_dtype=F32, name, b_off=0, n=None, b_koff=0, into=None, into_off=0,
        into_rows=False, hosted=None, epilogue=None):
    m = a.shape[1] if ta else a.shape[0]
    k = a.shape[0] if ta else a.shape[1]
    if n is None:
        n = b.shape[0] if tb else b.shape[1]
    assert b_koff + k <= (b.shape[1] if tb else b.shape[0])
    bm = _pick(m, (MM_TILE, 512, 256, 128))
    bn = _pick(n, (MM_TILE, 512, 256, 128))
    bk = next(c for c in (MM_TILE_K, 1024, 512, 256, 128, k) if k % c == 0 and b_koff % c == 0)
    nk = k // bk
    assert b_off % bn == 0 and into_off % (bm if into_rows else bn) == 0
    jb, kb = b_off // bn, b_koff // bk
    io, jo = (into_off // bm, 0) if into_rows else (0, into_off // bn)
    dims = (((0 if ta else 1,), (1 if tb else 0,)), ((), ()))
    grid = (m // bm, n // bn, nk)
    off = 1 if into is not None else 0
    nh_in = len(hosted.ins) if hosted else 0
    nh_out = len(hosted.shapes) if hosted else 0
    epi_fn, extras, out_dtypes = epilogue if epilogue else (lambda p: (p,), [], [out_dtype])
    ne, no = len(extras), len(out_dtypes)
    assert not (epilogue and into is not None)

    def body(a_ref, b_ref, *rest):
        e_refs = rest[off:off + ne]
        o_refs = rest[off + ne + nh_in:off + ne + nh_in + no]
        acc_ref = rest[off + ne + nh_in + no + nh_out]

        def emit(total):
            res = epi_fn(total, *[e[...] for e in e_refs])
            for o_ref, val in zip(o_refs, res):
                o_ref[...] = val.astype(o_ref.dtype)

        if hosted:
            h_refs = (rest[off + ne:off + ne + nh_in],
                      rest[off + ne + nh_in + no:off + ne + nh_in + no + nh_out],
                      rest[off + ne + nh_in + no + nh_out + 1:])
            step = (pl.program_id(0) * grid[1] + pl.program_id(1)) * grid[2] + pl.program_id(2)

            @pl.when(step == 0)
            def _():
                for cp in hosted.copies(*h_refs):
                    cp.start()

        part = _dot(a_ref[...], b_ref[...], dims)
        if nk == 1:
            emit(part)
        else:
            kk = pl.program_id(2)

            @pl.when(kk == 0)
            def _():
                acc_ref[...] = part

            @pl.when(jnp.logical_and(kk > 0, kk < nk - 1))
            def _():
                acc_ref[...] += part

            @pl.when(kk == nk - 1)
            def _():
                emit(acc_ref[...] + part)

        if hosted:
            @pl.when(step == grid[0] * grid[1] * grid[2] - 1)
            def _():
                for cp in hosted.copies(*h_refs):
                    cp.wait()

    a_spec = pl.BlockSpec((bk, bm), lambda i, j, kk: (kk, i)) if ta else \
        pl.BlockSpec((bm, bk), lambda i, j, kk: (i, kk))
    b_spec = pl.BlockSpec((bn, bk), lambda i, j, kk: (j + jb, kk + kb)) if tb else \
        pl.BlockSpec((bk, bn), lambda i, j, kk: (kk + kb, j + jb))
    extra = {} if into is None else {"input_output_aliases": {2: 0}}
    out_shapes = [jax.ShapeDtypeStruct((m, n), dt) for dt in out_dtypes] if into is None else \
        [jax.ShapeDtypeStruct(into.shape, into.dtype)]
    block = pl.BlockSpec((bm, bn), lambda i, j, kk: (i, j))
    res = pl.pallas_call(
        body, name=name, grid=grid,
        in_specs=[a_spec, b_spec] + ([] if into is None else [pl.BlockSpec(memory_space=pl.ANY)])
        + [block] * ne + [HBM] * nh_in,
        out_specs=[pl.BlockSpec((bm, bn), lambda i, j, kk: (i + io, j + jo))] * no + [HBM] * nh_out,
        out_shape=out_shapes + (list(hosted.shapes) if hosted else []),
        scratch_shapes=[pltpu.VMEM((bm, bn) if nk > 1 else (8, 128), F32)] + (list(hosted.sems) if hosted else []),
        compiler_params=_cparams(("arbitrary",) * 3 if hosted else ("parallel", "parallel", "arbitrary")),
        **extra,
    )(*((a, b) if into is None else (a, b, into)), *extras, *(hosted.ins if hosted else ()))
    return res if (hosted or epilogue) else res[0]


def _rows(fn, tiled, params, outs, accs=(), *, ts, name):
    s = tiled[0].shape[0]
    ts = min(ts, s)
    assert s % ts == 0
    nt, npar, no, na = len(tiled), len(params), len(outs), len(accs)

    def body(*refs):
        i = pl.program_id(0)
        vals = [r[...] for r in refs[:nt + npar]]
        res = fn(*vals)
        if not isinstance(res, (tuple, list)):
            res = (res,)
        orefs = refs[nt + npar:nt + npar + no]
        arefs = refs[nt + npar + no:]
        for r_, val in zip(orefs, res[:no]):
            r_[...] = val.astype(r_.dtype)
        if na:
            @pl.when(i == 0)
            def _():
                for r_ in arefs:
                    r_[...] = jnp.zeros_like(r_)

            for r_, val in zip(arefs, res[no:]):
                r_[...] += jnp.broadcast_to(val, r_.shape)

    in_specs = [pl.BlockSpec((ts, a.shape[1]), lambda i: (i, 0)) for a in tiled]
    in_specs += [pl.BlockSpec(p.shape, lambda i: (0, 0)) for p in params]
    out_specs = [pl.BlockSpec((ts, w), lambda i: (i, 0)) for (w, _) in outs]
    out_specs += [pl.BlockSpec(shape, lambda i: (0, 0)) for shape in accs]
    out_shape = [jax.ShapeDtypeStruct((s, w), dt) for (w, dt) in outs]
    out_shape += [jax.ShapeDtypeStruct(shape, F32) for shape in accs]
    res = pl.pallas_call(
        body, name=name, grid=(s // ts,),
        in_specs=in_specs, out_specs=out_specs, out_shape=out_shape,
        compiler_params=_cparams(("arbitrary",)),
    )(*tiled, *params)
    return res


def _sb_block(qs, kb, diag):
    tq, bk = qs.shape[0], kb.shape[0]
    z = _dot(qs, kb, NT)
    lb = jnp.minimum(z, 0.0) - jnp.log(1.0 + jnp.exp(-jnp.abs(z)))
    lk = lb - z
    if diag is None:
        return None, lb, lk
    causal = (diag + _iota((tq, bk), 1)) < _iota((tq, bk), 0)
    return causal, lb, jnp.where(causal, lk, 0.0)


def _fused_exchange(scatter, ncols, nsteps):
    def hooks(ins, outs, sems):
        step = pl.program_id(0) * ncols + pl.program_id(1)

        @pl.when(step == 0)
        def _():
            for cp in _exchange_copies(ins, outs, *sems, scatter):
                cp.start()

        def finish():
            @pl.when(step == nsteps - 1)
            def _():
                for cp in _exchange_copies(ins, outs, *sems, scatter):
                    cp.wait()
        return finish
    return hooks


def _sb_fwd(q, k, v, comm=()):
    h, s, dh = q.shape
    tq, bk = min(SB_TQ, s), min(SB_BK, s)
    scale = dh ** -0.5
    nc = len(comm)
    hooks = _fused_exchange(False, s // tq, h * (s // tq))

    def body(q_ref, k_ref, v_ref, *rest):
        o_ref, lt_ref = rest[nc:nc + 2]
        if nc:
            finish = hooks(rest[:nc], rest[nc + 2:2 * nc + 2], rest[2 * nc + 2:])
        i = pl.program_id(1)
        q0 = i * tq
        qs = q_ref[...] * scale
        tri = (_iota((bk, bk), 0) > _iota((bk, bk), 1)).astype(CDT)

        def step(k0, carry, diag, r0=0):
            cf, acc = carry
            kb = k_ref[pl.ds(k0, bk), :]
            vb = v_ref[pl.ds(k0, bk), :]
            causal, lb, lk = _sb_block(qs[r0:], kb, diag)
            w = jnp.exp(lb + cf + _split_dot(lk, tri, False, 2))
            if causal is not None:
                w = jnp.where(causal, w, 0.0)
            return cf + jnp.sum(lk, axis=1, keepdims=True), acc + _dot(w, vb)

        carry = (jnp.zeros((tq, 1), F32), jnp.zeros((tq, dh), F32))
        for d in reversed(range(tq // bk)):
            r0 = d * bk
            sub = step(pl.multiple_of(q0 + r0, bk), tuple(t[r0:] for t in carry), 0, r0)
            carry = tuple(jnp.concatenate([t[:r0], u], axis=0) if r0 else u for t, u in zip(carry, sub))
        nfull = q0 // bk
        cf, acc = lax.fori_loop(
            0, nfull, lambda jj, c: step(pl.multiple_of((nfull - 1 - jj) * bk, bk), c, None), carry)
        o_ref[...] = acc
        lt_ref[...] = cf
        if nc:
            finish()

    return pl.pallas_call(
        body, name="sb_fwd", grid=(h, s // tq),
        in_specs=[pl.BlockSpec((None, tq, dh), lambda a, i: (a, i, 0)),
                  pl.BlockSpec((None, s, dh), lambda a, i: (a, 0, 0)),
                  pl.BlockSpec((None, s, dh), lambda a, i: (a, 0, 0))] + [HBM] * nc,
        out_specs=[pl.BlockSpec((None, tq, dh), lambda a, i: (a, i, 0)),
                   pl.BlockSpec((None, tq, 1), lambda a, i: (a, i, 0))] + [HBM] * nc,
        out_shape=[jax.ShapeDtypeStruct((h, s, dh), F32), jax.ShapeDtypeStruct((h, s, 1), F32)]
        + _exchange_shapes(comm, False),
        scratch_shapes=_exchange_sems(nc) if nc else [],
        compiler_params=_cparams(("arbitrary", "arbitrary")),
    )(q, k, v, *comm)


def _sb_bwd(q, k, v, ltot, do, comm=()):
    h, s, dh = q.shape
    tq, bk = min(SB_TQ, s), min(SB_BK, s)
    scale = dh ** -0.5
    nc = len(comm)
    hooks = _fused_exchange(True, s // tq, h * (s // tq))

    def body(q_ref, k_ref, v_ref, lt_ref, do_ref, *rest):
        dq_ref, dk_ref, dv_ref = rest[nc:nc + 3]
        if nc:
            finish = hooks(rest[:nc], rest[nc + 3:2 * nc + 3], rest[2 * nc + 3:])
        i = pl.program_id(1)

        @pl.when(i == 0)
        def _():
            dk_ref[...] = jnp.zeros_like(dk_ref)
            dv_ref[...] = jnp.zeros_like(dv_ref)

        q0 = i * tq
        qs = q_ref[...] * scale
        dov = do_ref[...].astype(CDT)
        ltot = lt_ref[...]
        tri_le = (_iota((bk, bk), 0) <= _iota((bk, bk), 1)).astype(CDT)
        tri_lt = (_iota((bk, bk), 0) < _iota((bk, bk), 1)).astype(CDT)

        def step(k0, carry, diag, r0=0):
            cf, cg, dq = carry
            kb = k_ref[pl.ds(k0, bk), :]
            vb = v_ref[pl.ds(k0, bk), :]
            causal, lb, lk = _sb_block(qs[r0:], kb, diag)
            w = jnp.exp(lb + ltot[r0:] - (cf + _split_dot(lk, tri_le, False, 2)))
            if causal is not None:
                w = jnp.where(causal, w, 0.0)
            g = w * _dot(dov[r0:], vb, NT)
            gsum = cg + _split_dot(g, tri_lt, False, 2)
            dz = g - (g + gsum) * jnp.exp(lb)
            if causal is not None:
                dz = jnp.where(causal, dz, 0.0)
            dzc = dz.astype(CDT)
            dk_ref[pl.ds(k0, bk), :] += _dot(dzc, qs[r0:], TN)
            dv_ref[pl.ds(k0, bk), :] += _dot(w, dov[r0:], TN)
            return (cf + jnp.sum(lk, axis=1, keepdims=True), cg + jnp.sum(g, axis=1, keepdims=True),
                    dq + _dot(dzc, kb))

        carry = (jnp.zeros((tq, 1), F32), jnp.zeros((tq, 1), F32), jnp.zeros((tq, dh), F32))
        carry = lax.fori_loop(0, q0 // bk, lambda jj, c: step(pl.multiple_of(jj * bk, bk), c, None), carry)
        for d in range(tq // bk):
            r0 = d * bk
            sub = step(pl.multiple_of(q0 + r0, bk), tuple(t[r0:] for t in carry), 0, r0)
            carry = tuple(jnp.concatenate([t[:r0], u], axis=0) if r0 else u for t, u in zip(carry, sub))
        dq_ref[...] = carry[2] * scale
        if nc:
            finish()

    tile = pl.BlockSpec((None, tq, dh), lambda a, i: (a, i, 0))
    full = pl.BlockSpec((None, s, dh), lambda a, i: (a, 0, 0))
    shp = jax.ShapeDtypeStruct((h, s, dh), F32)
    return pl.pallas_call(
        body, name="sb_bwd", grid=(h, s // tq),
        in_specs=[tile, full, full, pl.BlockSpec((None, tq, 1), lambda a, i: (a, i, 0)), tile] + [HBM] * nc,
        out_specs=[tile, full, full] + [HBM] * nc,
        out_shape=[shp, shp, shp] + _exchange_shapes(comm, True),
        scratch_shapes=_exchange_sems(nc) if nc else [],
        compiler_params=_cparams(("arbitrary", "arbitrary")),
    )(q, k, v, ltot, do, *comm)


SB_PAIR = 128


def _sb_fwd_pairs(qkv, comm=()):
    s, d3 = qkv.shape
    d = d3 // 3
    npair = d // SB_PAIR
    tq, bk = min(SB_TQ, s), min(SB_BK, s)
    scale = SB_DH ** -0.5
    nc = len(comm)
    hooks = _fused_exchange(False, s // tq, npair * (s // tq))

    def body(q_ref, k_ref, v_ref, *rest):
        y_ref, lt_ref = rest[nc:nc + 2]
        if nc:
            finish = hooks(rest[:nc], rest[nc + 2:2 * nc + 2], rest[2 * nc + 2:])
        i = pl.program_id(1)
        q0 = i * tq
        q2 = q_ref[...] * scale
        lane_head = (_iota((1, SB_PAIR), 1) >= SB_DH).astype(jnp.int32)
        tri = (_iota((bk, bk), 0) > _iota((bk, bk), 1)).astype(CDT)

        def head(hh, y):
            mine = lane_head == hh
            qs = jnp.where(mine, q2, jnp.zeros_like(q2))

            def step(k0, carry, diag, r0=0):
                cf, acc = carry
                kb = k_ref[pl.ds(k0, bk), :]
                vb = v_ref[pl.ds(k0, bk), :]
                causal, lb, lk = _sb_block(qs[r0:], kb, diag)
                w = jnp.exp(lb + cf + _split_dot(lk, tri, False, 2))
                if causal is not None:
                    w = jnp.where(causal, w, 0.0)
                return cf + jnp.sum(lk, axis=1, keepdims=True), acc + _dot(w, vb)

            carry = (jnp.zeros((tq, 1), F32), jnp.zeros((tq, SB_PAIR), F32))
            for dd in reversed(range(tq // bk)):
                r0 = dd * bk
                sub = step(pl.multiple_of(q0 + r0, bk), tuple(t[r0:] for t in carry), 0, r0)
                carry = tuple(jnp.concatenate([t[:r0], u], axis=0) if r0 else u for t, u in zip(carry, sub))
            nfull = q0 // bk
            cf, acc = lax.fori_loop(
                0, nfull, lambda jj, c: step(pl.multiple_of((nfull - 1 - jj) * bk, bk), c, None), carry)
            lt_ref[hh] = cf
            return jnp.where(mine, acc, y)

        y_ref[...] = lax.fori_loop(0, 2, head, jnp.zeros((tq, SB_PAIR), F32)).astype(y_ref.dtype)
        if nc:
            finish()

    return pl.pallas_call(
        body, name="sb_fwd", grid=(npair, s // tq),
        in_specs=[pl.BlockSpec((tq, SB_PAIR), lambda a, i: (i, a)),
                  pl.BlockSpec((s, SB_PAIR), lambda a, i: (0, npair + a)),
                  pl.BlockSpec((s, SB_PAIR), lambda a, i: (0, 2 * npair + a))] + [HBM] * nc,
        out_specs=[pl.BlockSpec((tq, SB_PAIR), lambda a, i: (i, a)),
                   pl.BlockSpec((2, tq, 1), lambda a, i: (a, i, 0))] + [HBM] * nc,
        out_shape=[jax.ShapeDtypeStruct((s, d), CDT), jax.ShapeDtypeStruct((2 * npair, s, 1), F32)]
        + _exchange_shapes(comm, False),
        scratch_shapes=_exchange_sems(nc) if nc else [],
        compiler_params=_cparams(("arbitrary", "arbitrary")),
    )(qkv, qkv, qkv, *comm)


def _sb_bwd_pairs(qkv, ltot, dy, comm=()):
    s, d3 = qkv.shape
    d = d3 // 3
    npair = d // SB_PAIR
    tq, bk = min(SB_TQ, s), min(SB_BK, s)
    scale = SB_DH ** -0.5
    nc = len(comm)
    hooks = _fused_exchange(True, s // tq, npair * (s // tq))

    def body(q_ref, k_ref, v_ref, lt_ref, dy_ref, *rest):
        dq_ref, dk_ref, dv_ref = rest[nc:nc + 3]
        if nc:
            finish = hooks(rest[:nc], rest[nc + 3:2 * nc + 3], rest[2 * nc + 3:])
        i = pl.program_id(1)

        @pl.when(i == 0)
        def _():
            dk_ref[...] = jnp.zeros_like(dk_ref)
            dv_ref[...] = jnp.zeros_like(dv_ref)

        q0 = i * tq
        q2 = q_ref[...] * scale
        do2 = dy_ref[...].astype(CDT)
        lane_head = (_iota((1, SB_PAIR), 1) >= SB_DH).astype(jnp.int32)
        tri_le = (_iota((bk, bk), 0) <= _iota((bk, bk), 1)).astype(CDT)
        tri_lt = (_iota((bk, bk), 0) < _iota((bk, bk), 1)).astype(CDT)

        def head(hh, dq_all):
            mine = lane_head == hh
            qs = jnp.where(mine, q2, jnp.zeros_like(q2))
            dov = jnp.where(mine, do2, jnp.zeros_like(do2))
            ltot_h = lt_ref[hh]

            def step(k0, carry, diag, r0=0):
                cf, cg, dq = carry
                kb = k_ref[pl.ds(k0, bk), :]
                vb = v_ref[pl.ds(k0, bk), :]
                causal, lb, lk = _sb_block(qs[r0:], kb, diag)
                w = jnp.exp(lb + ltot_h[r0:] - (cf + _split_dot(lk, tri_le, False, 2)))
                if causal is not None:
                    w = jnp.where(causal, w, 0.0)
                g = w * _dot(dov[r0:], vb, NT)
                gsum = cg + _split_dot(g, tri_lt, False, 2)
                dz = g - (g + gsum) * jnp.exp(lb)
                if causal is not None:
                    dz = jnp.where(causal, dz, 0.0)
                dzc = dz.astype(CDT)
                dk_ref[pl.ds(k0, bk), :] += _dot(dzc, qs[r0:], TN)
                dv_ref[pl.ds(k0, bk), :] += _dot(w, dov[r0:], TN)
                kbm = jnp.where(mine, kb, jnp.zeros_like(kb))
                return (cf + jnp.sum(lk, axis=1, keepdims=True), cg + jnp.sum(g, axis=1, keepdims=True),
                        dq + _dot(dzc, kbm))

            carry = (jnp.zeros((tq, 1), F32), jnp.zeros((tq, 1), F32), jnp.zeros((tq, SB_PAIR), F32))
            carry = lax.fori_loop(0, q0 // bk, lambda jj, c: step(pl.multiple_of(jj * bk, bk), c, None), carry)
            for dd in range(tq // bk):
                r0 = dd * bk
                sub = step(pl.multiple_of(q0 + r0, bk), tuple(t[r0:] for t in carry), 0, r0)
                carry = tuple(jnp.concatenate([t[:r0], u], axis=0) if r0 else u for t, u in zip(carry, sub))
            return dq_all + carry[2]

        dq_ref[...] = lax.fori_loop(0, 2, head, jnp.zeros((tq, SB_PAIR), F32)) * scale
        if nc:
            finish()

    tile = pl.BlockSpec((tq, SB_PAIR), lambda a, i: (i, a))
    acc = pl.BlockSpec((s, SB_PAIR), lambda a, i: (0, a))
    shp = jax.ShapeDtypeStruct((s, d), F32)
    return pl.pallas_call(
        body, name="sb_bwd", grid=(npair, s // tq),
        in_specs=[tile, pl.BlockSpec((s, SB_PAIR), lambda a, i: (0, npair + a)),
                  pl.BlockSpec((s, SB_PAIR), lambda a, i: (0, 2 * npair + a)),
                  pl.BlockSpec((2, tq, 1), lambda a, i: (a, i, 0)), tile] + [HBM] * nc,
        out_specs=[tile, acc, acc] + [HBM] * nc,
        out_shape=[shp, shp, shp] + _exchange_shapes(comm, True),
        scratch_shapes=_exchange_sems(nc) if nc else [],
        compiler_params=_cparams(("arbitrary", "arbitrary")),
    )(qkv, qkv, qkv, ltot, dy, *comm)


def _pick_lane(tile, r):
    return jnp.sum(jnp.where(_iota(tile.shape, 1) == r, tile, 0.0), axis=1, keepdims=True)


def _pick_row(tile, r):
    return jnp.sum(jnp.where(_iota(tile.shape, 0) == r, tile, 0.0), axis=0, keepdims=True)


def _ssd_chunk_setup(c_ref, b_ref, dac_ref, dar_ref, cb_ref, acsc_ref, acsr_ref):
    l = SSD_L
    tdt = F32 if CDT == F32 else BF16
    lower = (_iota((l, l), 1) <= _iota((l, l), 0)).astype(tdt)
    upper = (_iota((l, l), 0) <= _iota((l, l), 1)).astype(tdt)
    cb_ref[...] = _dot(c_ref[...], b_ref[...], NT)
    acsc_ref[...] = _split_dot(dac_ref[...], lower, True, 3)
    acsr_ref[...] = _split_dot(dar_ref[...], upper, False, 3)


def _ssd_fwd(xh, dtc, dac, dar, dsk, xbc):
    hh, s, p = xh.shape
    l, n, g_, r_ = SSD_L, SSD_N, SSD_G, SSD_R
    nc = s // l
    boff = SSD_INNER // n
    coff = boff + g_

    def body(x_ref, dtc_ref, dac_ref, dar_ref, dsk_ref, b_ref, c_ref, y_ref, st_ref,
             state_ref, cb_ref, acsc_ref, acsr_ref):
        c = pl.program_id(1)
        r = pl.program_id(2)

        @pl.when(r == 0)
        def _():
            _ssd_chunk_setup(c_ref, b_ref, dac_ref, dar_ref, cb_ref, acsc_ref, acsr_ref)

        @pl.when(c == 0)
        def _():
            state_ref[r] = jnp.zeros((n, p), F32)

        a_col = _pick_lane(acsc_ref[...], r)
        a_row = _pick_row(acsr_ref[...], r)
        dt_col = _pick_lane(dtc_ref[...], r)
        dsk_h = _pick_lane(dsk_ref[...], r)
        xv = x_ref[...]
        xd = xv * dt_col
        mask = _iota((l, l), 1) <= _iota((l, l), 0)
        decay = jnp.where(mask, jnp.exp(jnp.minimum(a_col - a_row, 0.0)), 0.0)
        w = cb_ref[...] * decay
        hprev = state_ref[r]
        cv = c_ref[...]
        y = _dot(w, xd) + jnp.exp(a_col) * _dot(cv, hprev)
        y_ref[...] = y + dsk_h * xv
        a_end = a_col[l - 1:l, :]
        dte = jnp.exp(a_end - a_col)
        st_ref[...] = hprev
        state_ref[r] = hprev * jnp.exp(a_end) + _dot(b_ref[...], xd * dte, TN)

    return pl.pallas_call(
        body, name="ssd_fwd", grid=(g_, nc, r_),
        in_specs=[pl.BlockSpec((None, l, p), lambda g, c, r: (g * r_ + r, c, 0)),
                  pl.BlockSpec((None, l, r_), lambda g, c, r: (g, c, 0)),
                  pl.BlockSpec((None, l, r_), lambda g, c, r: (g, c, 0)),
                  pl.BlockSpec((None, r_, l), lambda g, c, r: (g, 0, c)),
                  pl.BlockSpec((None, 1, r_), lambda g, c, r: (g, 0, 0)),
                  pl.BlockSpec((l, n), lambda g, c, r: (c, boff + g)),
                  pl.BlockSpec((l, n), lambda g, c, r: (c, coff + g))],
        out_specs=[pl.BlockSpec((None, l, p), lambda g, c, r: (g * r_ + r, c, 0)),
                   pl.BlockSpec((None, None, n, p), lambda g, c, r: (g * r_ + r, c, 0, 0))],
        out_shape=[jax.ShapeDtypeStruct((hh, s, p), F32),
                   jax.ShapeDtypeStruct((hh, nc, n, p), F32)],
        scratch_shapes=[pltpu.VMEM((r_, n, p), F32), pltpu.VMEM((l, l), F32),
                        pltpu.VMEM((l, r_), F32), pltpu.VMEM((r_, l), F32)],
        compiler_params=_cparams(("parallel", "arbitrary", "arbitrary")),
    )(xh, dtc, dac, dar, dsk, xbc, xbc)


def _ssd_bwd(xh, dtc, dac, dar, dsk, xbc, st, dy):
    hh, s, p = xh.shape
    l, n, g_, r_ = SSD_L, SSD_N, SSD_G, SSD_R
    nc = s // l
    boff = SSD_INNER // n
    coff = boff + g_

    def body(x_ref, dtc_ref, dac_ref, dar_ref, dsk_ref, b_ref, c_ref, st_ref, dy_ref,
             dx_ref, dda_ref, dxx_ref, db_ref, dc_ref, dah_ref, ddsk_ref,
             dstate_ref, cb_ref, acsc_ref, acsr_ref):
        c = pl.program_id(1)
        r = pl.program_id(2)

        @pl.when(r == 0)
        def _():
            _ssd_chunk_setup(c_ref, b_ref, dac_ref, dar_ref, cb_ref, acsc_ref, acsr_ref)
            db_ref[...] = jnp.zeros_like(db_ref)
            dc_ref[...] = jnp.zeros_like(dc_ref)
            dda_ref[...] = jnp.zeros_like(dda_ref)
            dxx_ref[...] = jnp.zeros_like(dxx_ref)

        @pl.when(jnp.logical_and(c == 0, r == 0))
        def _():
            dah_ref[...] = jnp.zeros_like(dah_ref)
            ddsk_ref[...] = jnp.zeros_like(ddsk_ref)

        @pl.when(c == 0)
        def _():
            dstate_ref[r] = jnp.zeros((n, p), F32)

        a_col = _pick_lane(acsc_ref[...], r)
        a_row = _pick_row(acsr_ref[...], r)
        dt_col = _pick_lane(dtc_ref[...], r)
        dsk_h = _pick_lane(dsk_ref[...], r)
        xv = x_ref[...]
        dyv = dy_ref[...]
        xd = xv * dt_col
        il = _iota((l, l), 0)
        isx = _iota((l, l), 1)
        decay = jnp.where(isx <= il, jnp.exp(jnp.minimum(a_col - a_row, 0.0)), 0.0)
        cb = cb_ref[...]
        w = cb * decay
        dhn = dstate_ref[r]
        hc = st_ref[...]
        bv = b_ref[...]
        cv = c_ref[...]
        a_end = a_col[l - 1:l, :]
        ea = jnp.exp(a_col)
        dte = jnp.exp(a_end - a_col)

        dx_state = dte * _dot(bv, dhn)
        dxd = _dot(w, dyv, TN) + dx_state
        md = decay * _dot(dyv, xd, NT)
        m = md * cb
        dc_ref[...] += _dot(md, bv) + ea * _dot(dyv, hc, NT)
        db_ref[...] += _dot(md, cv, TN) + dte * _dot(xd, dhn, NT)
        dstate_ref[r] = jnp.exp(a_end) * dhn + _dot(cv, dyv * ea, TN)

        tdt = F32 if CDT == F32 else BF16
        t1 = (isx >= il).astype(tdt)
        yoff = ea * _dot(cv, hc)
        xdx = jnp.sum(xd * dx_state, axis=1, keepdims=True)
        vec = jnp.sum(dyv * yoff, axis=1, keepdims=True) - xdx
        end_term = jnp.sum(xdx, axis=0, keepdims=True) + \
            jnp.exp(a_end) * jnp.sum(jnp.sum(hc * dhn, axis=1, keepdims=True), axis=0, keepdims=True)
        zmat = _split_dot(m, t1, True, 2)
        span = jnp.sum(jnp.where(isx < il, zmat, 0.0), axis=1, keepdims=True)
        rc = _split_dot(jnp.broadcast_to(vec, (l, 128)), t1, True, 2)[:, :1]
        dda = span + rc + end_term
        dxx = jnp.sum(dxd * xv, axis=1, keepdims=True)

        lane = _iota((l, r_), 1) == r
        dda_ref[...] += jnp.where(lane, dda, 0.0)
        dxx_ref[...] += jnp.where(lane, dxx, 0.0)
        dx_ref[...] = dxd * dt_col + dsk_h * dyv
        lane1 = _iota((1, r_), 1) == r
        dah_ref[...] += jnp.where(lane1, jnp.sum(dda * dt_col, axis=0, keepdims=True), 0.0)
        ddsk_ref[...] += jnp.where(
            lane1, jnp.sum(jnp.sum(dyv * xv, axis=1, keepdims=True), axis=0, keepdims=True), 0.0)

    rev = lambda c: nc - 1 - c
    xspec = pl.BlockSpec((None, l, p), lambda g, c, r: (g * r_ + r, rev(c), 0))
    cspec = pl.BlockSpec((None, l, r_), lambda g, c, r: (g, rev(c), 0))
    hspec = pl.BlockSpec((None, 1, r_), lambda g, c, r: (g, 0, 0))
    return pl.pallas_call(
        body, name="ssd_bwd", grid=(g_, nc, r_),
        in_specs=[xspec, cspec, cspec,
                  pl.BlockSpec((None, r_, l), lambda g, c, r: (g, 0, rev(c))),
                  hspec,
                  pl.BlockSpec((l, n), lambda g, c, r: (rev(c), boff + g)),
                  pl.BlockSpec((l, n), lambda g, c, r: (rev(c), coff + g)),
                  pl.BlockSpec((None, None, n, p), lambda g, c, r: (g * r_ + r, rev(c), 0, 0)),
                  xspec],
        out_specs=[xspec, cspec, cspec,
                   pl.BlockSpec((l, n), lambda g, c, r: (rev(c), g)),
                   pl.BlockSpec((l, n), lambda g, c, r: (rev(c), g)),
                   hspec, hspec],
        out_shape=[jax.ShapeDtypeStruct((hh, s, p), F32),
                   jax.ShapeDtypeStruct((g_, s, r_), F32),
                   jax.ShapeDtypeStruct((g_, s, r_), F32),
                   jax.ShapeDtypeStruct((s, g_ * n), F32),
                   jax.ShapeDtypeStruct((s, g_ * n), F32),
                   jax.ShapeDtypeStruct((g_, 1, r_), F32),
                   jax.ShapeDtypeStruct((g_, 1, r_), F32)],
        scratch_shapes=[pltpu.VMEM((r_, n, p), F32), pltpu.VMEM((l, l), F32),
                        pltpu.VMEM((l, r_), F32), pltpu.VMEM((r_, l), F32)],
        compiler_params=_cparams(("parallel", "arbitrary", "arbitrary")),
    )(xh, dtc, dac, dar, dsk, xbc, xbc, st, dy)


def _ssd_chunk_common(c_ref, b_ref, dac_ref, dar_ref):
    l = SSD_L
    tdt = F32 if CDT == F32 else BF16
    lower = (_iota((l, l), 1) <= _iota((l, l), 0)).astype(tdt)
    upper = (_iota((l, l), 0) <= _iota((l, l), 1)).astype(tdt)
    cb = _dot(c_ref[...], b_ref[...], NT)
    return cb, _split_dot(dac_ref[...], lower, True, 3), _split_dot(dar_ref[...], upper, False, 3)


def _ssd_fwd_g(xh, dtc, dac, dar, dsk, xbc):
    hh, s, p = xh.shape
    l, n, g_, r_ = SSD_L, SSD_N, SSD_G, SSD_R
    nc = s // l
    boff = SSD_INNER // n
    coff = boff + g_

    def body(x_ref, dtc_ref, dac_ref, dar_ref, dsk_ref, b_ref, c_ref, y_ref, st_ref, state_ref):
        c = pl.program_id(1)

        @pl.when(c == 0)
        def _():
            state_ref[...] = jnp.zeros_like(state_ref)

        cb, acs_c, acs_r = _ssd_chunk_common(c_ref, b_ref, dac_ref, dar_ref)
        mask = _iota((l, l), 1) <= _iota((l, l), 0)
        cv, bv = c_ref[...], b_ref[...]
        dtcv, dskv = dtc_ref[...], dsk_ref[...]
        for r in range(r_):
            a_col = _pick_lane(acs_c, r)
            a_row = _pick_row(acs_r, r)
            dt_col = _pick_lane(dtcv, r)
            dsk_h = _pick_lane(dskv, r)
            xv = x_ref[r]
            xd = xv * dt_col
            decay = jnp.where(mask, jnp.exp(jnp.minimum(a_col - a_row, 0.0)), 0.0)
            hprev = state_ref[r]
            y = _dot(cb * decay, xd) + jnp.exp(a_col) * _dot(cv, hprev)
            y_ref[r] = y + dsk_h * xv
            a_end = a_col[l - 1:l, :]
            st_ref[r] = hprev
            state_ref[r] = hprev * jnp.exp(a_end) + _dot(bv, xd * jnp.exp(a_end - a_col), TN)

    return pl.pallas_call(
        body, name="ssd_fwd", grid=(g_, nc),
        in_specs=[pl.BlockSpec((r_, l, p), lambda g, c: (g, c, 0)),
                  pl.BlockSpec((None, l, r_), lambda g, c: (g, c, 0)),
                  pl.BlockSpec((None, l, r_), lambda g, c: (g, c, 0)),
                  pl.BlockSpec((None, r_, l), lambda g, c: (g, 0, c)),
                  pl.BlockSpec((None, 1, r_), lambda g, c: (g, 0, 0)),
                  pl.BlockSpec((l, n), lambda g, c: (c, boff + g)),
                  pl.BlockSpec((l, n), lambda g, c: (c, coff + g))],
        out_specs=[pl.BlockSpec((r_, l, p), lambda g, c: (g, c, 0)),
                   pl.BlockSpec((r_, None, n, p), lambda g, c: (g, c, 0, 0))],
        out_shape=[jax.ShapeDtypeStruct((hh, s, p), F32),
                   jax.ShapeDtypeStruct((hh, nc, n, p), F32)],
        scratch_shapes=[pltpu.VMEM((r_, n, p), F32)],
        compiler_params=_cparams(("parallel", "arbitrary")),
    )(xh, dtc, dac, dar, dsk, xbc, xbc)


def _ssd_bwd_g(xh, dtc, dac, dar, dsk, xbc, st, dy):
    hh, s, p = xh.shape
    l, n, g_, r_ = SSD_L, SSD_N, SSD_G, SSD_R
    nc = s // l
    boff = SSD_INNER // n
    coff = boff + g_

    def body(x_ref, dtc_ref, dac_ref, dar_ref, dsk_ref, b_ref, c_ref, st_ref, dy_ref,
             dx_ref, dda_ref, dxx_ref, db_ref, dc_ref, dah_ref, ddsk_ref, dstate_ref):
        c = pl.program_id(1)

        @pl.when(c == 0)
        def _():
            dstate_ref[...] = jnp.zeros_like(dstate_ref)
            dah_ref[...] = jnp.zeros_like(dah_ref)
            ddsk_ref[...] = jnp.zeros_like(ddsk_ref)

        cb, acs_c, acs_r = _ssd_chunk_common(c_ref, b_ref, dac_ref, dar_ref)
        il = _iota((l, l), 0)
        isx = _iota((l, l), 1)
        tdt = F32 if CDT == F32 else BF16
        t1 = (isx >= il).astype(tdt)
        cv, bv = c_ref[...], b_ref[...]
        dtcv, dskv = dtc_ref[...], dsk_ref[...]
        lane = _iota((l, r_), 1)
        lane1 = _iota((1, r_), 1)
        dda_all = jnp.zeros((l, r_), F32)
        dxx_all = jnp.zeros((l, r_), F32)
        dah_all = jnp.zeros((1, r_), F32)
        ddsk_all = jnp.zeros((1, r_), F32)
        db_acc = jnp.zeros((l, n), F32)
        dc_acc = jnp.zeros((l, n), F32)
        md_sum = jnp.zeros((l, l), F32)
        cb_t = _dot(bv, cv, NT)
        cv_t = cv.T
        for r in range(r_):
            a_col = _pick_lane(acs_c, r)
            a_row = _pick_row(acs_r, r)
            dt_col = _pick_lane(dtcv, r)
            dsk_h = _pick_lane(dskv, r)
            xv = x_ref[r]
            dyv = dy_ref[r]
            xd = xv * dt_col
            decay = jnp.where(isx <= il, jnp.exp(jnp.minimum(a_col - a_row, 0.0)), 0.0)
            decay_t = jnp.where(isx >= il, jnp.exp(jnp.minimum(a_row - a_col, 0.0)), 0.0)
            dhn = dstate_ref[r]
            hc = st_ref[r]
            a_end = a_col[l - 1:l, :]
            ea = jnp.exp(a_col)
            dte = jnp.exp(a_end - a_col)

            dx_state = dte * _dot(bv, dhn)
            dxd = _dot(cb_t * decay_t, dyv) + dx_state
            md = decay * _dot(dyv, xd, NT)
            md_sum = md_sum + md
            dc_acc = dc_acc + ea * _dot(dyv, hc, NT)
            db_acc = db_acc + dte * _dot(xd, dhn, NT)
            dstate_ref[r] = jnp.exp(a_end) * dhn + _dot(cv_t, dyv * ea)

            yoff = ea * _dot(cv, hc)
            xdx = jnp.sum(xd * dx_state, axis=1, keepdims=True)
            vec = jnp.sum(dyv * yoff, axis=1, keepdims=True) - xdx
            end_term = jnp.sum(xdx, axis=0, keepdims=True) + \
                jnp.exp(a_end) * jnp.sum(jnp.sum(hc * dhn, axis=1, keepdims=True), axis=0, keepdims=True)
            zmat = _split_dot(md * cb, t1, True, 2)
            span = jnp.sum(jnp.where(isx < il, zmat, 0.0), axis=1, keepdims=True)
            rc = _split_dot(jnp.broadcast_to(vec, (l, 128)), t1, True, 2)[:, :1]
            dda = span + rc + end_term
            dda_all = jnp.where(lane == r, dda, dda_all)
            dxx_all = jnp.where(lane == r, jnp.sum(dxd * xv, axis=1, keepdims=True), dxx_all)
            dx_ref[r] = dxd * dt_col + dsk_h * dyv
            dah_all = jnp.where(lane1 == r, jnp.sum(dda * dt_col, axis=0, keepdims=True), dah_all)
            ddsk_all = jnp.where(
                lane1 == r, jnp.sum(jnp.sum(dyv * xv, axis=1, keepdims=True), axis=0, keepdims=True), ddsk_all)
        dda_ref[...] = dda_all
        dxx_ref[...] = dxx_all
        db_ref[...] = db_acc + _dot(md_sum, cv, TN)
        dc_ref[...] = dc_acc + _dot(md_sum, bv)
        dah_ref[...] += dah_all
        ddsk_ref[...] += ddsk_all

    rev = lambda c: nc - 1 - c
    xspec = pl.BlockSpec((r_, l, p), lambda g, c: (g, rev(c), 0))
    cspec = pl.BlockSpec((None, l, r_), lambda g, c: (g, rev(c), 0))
    hspec = pl.BlockSpec((None, 1, r_), lambda g, c: (g, 0, 0))
    return pl.pallas_call(
        body, name="ssd_bwd", grid=(g_, nc),
        in_specs=[xspec, cspec, cspec,
                  pl.BlockSpec((None, r_, l), lambda g, c: (g, 0, rev(c))),
                  hspec,
                  pl.BlockSpec((l, n), lambda g, c: (rev(c), boff + g)),
                  pl.BlockSpec((l, n), lambda g, c: (rev(c), coff + g)),
                  pl.BlockSpec((r_, None, n, p), lambda g, c: (g, rev(c), 0, 0)),
                  xspec],
        out_specs=[xspec, cspec, cspec,
                   pl.BlockSpec((l, n), lambda g, c: (rev(c), g)),
                   pl.BlockSpec((l, n), lambda g, c: (rev(c), g)),
                   hspec, hspec],
        out_shape=[jax.ShapeDtypeStruct((hh, s, p), F32),
                   jax.ShapeDtypeStruct((g_, s, r_), F32),
                   jax.ShapeDtypeStruct((g_, s, r_), F32),
                   jax.ShapeDtypeStruct((s, g_ * n), F32),
                   jax.ShapeDtypeStruct((s, g_ * n), F32),
                   jax.ShapeDtypeStruct((g_, 1, r_), F32),
                   jax.ShapeDtypeStruct((g_, 1, r_), F32)],
        scratch_shapes=[pltpu.VMEM((r_, n, p), F32)],
        compiler_params=_cparams(("parallel", "arbitrary")),
    )(xh, dtc, dac, dar, dsk, xbc, xbc, st, dy)


CONV_TC = 256
CONV_RC = 512


def _conv_taps(x_ref, head_ref, rc):
    base = CONV_PAD - (CONV_K - 1)
    head_ref[pl.ds(0, CONV_PAD), :] = jnp.zeros((CONV_PAD, head_ref.shape[1]), F32)
    head_ref[pl.ds(CONV_PAD, rc), :] = x_ref[pl.ds(0, rc), :]

    def tap(t0, kk):
        if t0 == 0:
            return head_ref[pl.ds(base + kk, rc), :]
        return x_ref[pl.ds(t0 - (CONV_K - 1) + kk, rc), :]
    return tap


def _conv_fwd(x, w, b):
    s, ch = x.shape
    rc = min(CONV_RC, s)

    def body(x_ref, w_ref, b_ref, pre_ref, act_ref, head_ref):
        wv = w_ref[...]
        tap = _conv_taps(x_ref, head_ref, rc)
        for t0 in range(0, s, rc):
            acc = jnp.broadcast_to(b_ref[...], (rc, CONV_TC))
            for kk in range(CONV_K):
                acc = acc + wv[kk:kk + 1, :] * tap(t0, kk)
            pre_ref[pl.ds(t0, rc), :] = acc
            act_ref[pl.ds(t0, rc), :] = _silu(acc)

    col = pl.BlockSpec((s, CONV_TC), lambda j: (0, j))
    shp = jax.ShapeDtypeStruct((s, ch), F32)
    return pl.pallas_call(
        body, name="conv_fwd", grid=(ch // CONV_TC,),
        in_specs=[col, pl.BlockSpec((CONV_K, CONV_TC), lambda j: (0, j)),
                  pl.BlockSpec((1, CONV_TC), lambda j: (0, j))],
        out_specs=[col, col],
        out_shape=[shp, shp],
        scratch_shapes=[pltpu.VMEM((CONV_PAD + rc, CONV_TC), F32)],
        compiler_params=_cparams(("parallel",)),
    )(x, w, b)


def _conv_bwd(x, pre, dact, w):
    s, ch = x.shape
    rc = min(CONV_RC, s)

    def body(x_ref, pre_ref, da_ref, w_ref, dx_ref, dw_ref, db_ref, dpre_ref, head_ref):
        wv = w_ref[...]
        tap = _conv_taps(x_ref, head_ref, rc)
        for t0 in range(0, s, rc):
            dpre_ref[pl.ds(t0, rc), :] = da_ref[pl.ds(t0, rc), :] * _dsilu(pre_ref[pl.ds(t0, rc), :])
        dpre_ref[pl.ds(s, CONV_PAD), :] = jnp.zeros((CONV_PAD, CONV_TC), F32)
        dws = [jnp.zeros((1, CONV_TC), F32) for _ in range(CONV_K)]
        dbs = jnp.zeros((1, CONV_TC), F32)
        for t0 in range(0, s, rc):
            acc = jnp.zeros((rc, CONV_TC), F32)
            dp = dpre_ref[pl.ds(t0, rc), :]
            for kk in range(CONV_K):
                acc = acc + wv[kk:kk + 1, :] * dpre_ref[pl.ds(t0 + CONV_K - 1 - kk, rc), :]
                dws[kk] = dws[kk] + jnp.sum(dp * tap(t0, kk), axis=0, keepdims=True)
            dbs = dbs + jnp.sum(dp, axis=0, keepdims=True)
            dx_ref[pl.ds(t0, rc), :] = acc.astype(dx_ref.dtype)
        for kk in range(CONV_K):
            dw_ref[kk:kk + 1, :] = dws[kk]
        db_ref[...] = dbs

    col = pl.BlockSpec((s, CONV_TC), lambda j: (0, j))
    return pl.pallas_call(
        body, name="conv_bwd", grid=(ch // CONV_TC,),
        in_specs=[col, col, col, pl.BlockSpec((CONV_K, CONV_TC), lambda j: (0, j))],
        out_specs=[col, pl.BlockSpec((CONV_K, CONV_TC), lambda j: (0, j)),
                   pl.BlockSpec((1, CONV_TC), lambda j: (0, j))],
        out_shape=[jax.ShapeDtypeStruct((s, ch), CDT),
                   jax.ShapeDtypeStruct((CONV_K, ch), F32),
                   jax.ShapeDtypeStruct((1, ch), F32)],
        scratch_shapes=[pltpu.VMEM((s + CONV_PAD, CONV_TC), F32), pltpu.VMEM((CONV_PAD + rc, CONV_TC), F32)],
        compiler_params=_cparams(("parallel",)),
    )(x, pre, dact, w)


MEM_TS = 512


def _mem_fwd(mq, kv):
    s = mq.shape[0]
    m = kv.shape[0]
    ts = min(MEM_TS, s)
    scale = MEM_DH ** -0.5

    def body(q_ref, k_ref, v_ref, o_ref):
        sc = _dot(q_ref[...], k_ref[...], NT) * scale
        e = jnp.exp(sc - jnp.max(sc, axis=1, keepdims=True))
        pr = e / jnp.sum(e, axis=1, keepdims=True)
        o_ref[...] = _dot(pr, v_ref[...]).astype(o_ref.dtype)

    return pl.pallas_call(
        body, name="mem_fwd", grid=(MEM_H, s // ts),
        in_specs=[pl.BlockSpec((ts, MEM_DH), lambda a, i: (i, a)),
                  pl.BlockSpec((m, MEM_DH), lambda a, i: (0, a)),
                  pl.BlockSpec((m, MEM_DH), lambda a, i: (0, MEM_H + a))],
        out_specs=pl.BlockSpec((ts, MEM_DH), lambda a, i: (i, a)),
        out_shape=jax.ShapeDtypeStruct((s, MEM_H * MEM_DH), CDT),
        compiler_params=_cparams(("parallel", "arbitrary")),
    )(mq, kv, kv)


def _mem_bwd(mq, kv, do):
    s = mq.shape[0]
    m = kv.shape[0]
    ts = min(MEM_TS, s)
    scale = MEM_DH ** -0.5

    def body(q_ref, k_ref, v_ref, do_ref, dq_ref, dk_ref, dv_ref):
        i = pl.program_id(1)

        @pl.when(i == 0)
        def _():
            dk_ref[...] = jnp.zeros_like(dk_ref)
            dv_ref[...] = jnp.zeros_like(dv_ref)

        qv, kb, vb, dov = q_ref[...], k_ref[...], v_ref[...], do_ref[...]
        sc = _dot(qv, kb, NT) * scale
        e = jnp.exp(sc - jnp.max(sc, axis=1, keepdims=True))
        pr = e / jnp.sum(e, axis=1, keepdims=True)
        dp = _dot(dov, vb, NT)
        ds = pr * (dp - jnp.sum(dp * pr, axis=1, keepdims=True)) * scale
        dq_ref[...] = _dot(ds, kb).astype(dq_ref.dtype)
        dk_ref[...] += _dot(ds, qv, TN)
        dv_ref[...] += _dot(pr, dov, TN)

    tile = pl.BlockSpec((ts, MEM_DH), lambda a, i: (i, a))
    kvo = pl.BlockSpec((m, MEM_DH), lambda a, i: (0, a))
    return pl.pallas_call(
        body, name="mem_bwd", grid=(MEM_H, s // ts),
        in_specs=[tile, kvo, pl.BlockSpec((m, MEM_DH), lambda a, i: (0, MEM_H + a)), tile],
        out_specs=[tile, kvo, kvo],
        out_shape=[jax.ShapeDtypeStruct((s, MEM_H * MEM_DH), CDT),
                   jax.ShapeDtypeStruct((m, MEM_H * MEM_DH), F32),
                   jax.ShapeDtypeStruct((m, MEM_H * MEM_DH), F32)],
        compiler_params=_cparams(("parallel", "arbitrary")),
    )(mq, kv, kv, do)


def _heads(t, nh, dh):
    return t.reshape(t.shape[0], nh, dh).transpose(1, 0, 2)


def _unheads(t):
    return t.transpose(1, 0, 2).reshape(t.shape[1], t.shape[0] * t.shape[2])


def _group_cols(t):
    return t.reshape(t.shape[0], SSD_G, SSD_R).transpose(1, 0, 2)


def _pad_cols(t, width):
    return jnp.pad(t, ((0, 0), (0, width - t.shape[1])))


def _full_weight(name, gathered):
    if name in COL_SHARDED:
        return gathered.transpose(1, 0, 2).reshape(gathered.shape[1], N_DEV * gathered.shape[2])
    return gathered.reshape(N_DEV * gathered.shape[1], gathered.shape[2])


def _grad_payload(name, g):
    if name in COL_SHARDED:
        return g.reshape(g.shape[0], N_DEV, g.shape[1] // N_DEV).transpose(1, 0, 2)
    return g.reshape(N_DEV, g.shape[0] // N_DEV, g.shape[1])


def _local_step(x, mem, tgt, p, wt, shards=None):
    s, d = x.shape
    wt = dict(wt)
    c1, c2, c3, c4, c5 = 3 * d, 3 * d + SSD_INNER, 3 * d + SSD_INNER + CONV_DIM, \
        3 * d + SSD_INNER + CONV_DIM + SSD_H, 3 * d + SSD_INNER + CONV_DIM + SSD_H + d
    w_t = wt["w_in"]
    w_tail = w_t[c4:]
    w_dt = jnp.pad(w_t[c3:c4], ((0, DT_PAD - SSD_H), (0, 0)))
    seg_name = ["qkv", "z", "xbc", "mq", "gl"]
    seg_dtype = [CDT, F32, F32, CDT, F32]
    seg_src = [w_t, w_t, w_t, w_tail, w_tail]
    seg_off = [0, c1, c2, 0, d]
    seg_n = [c1, c2 - c1, c3 - c2, d, 3 * d]

    u = _rows(lambda xv, g: _rms(xv, g), [x], [p["norm_mix_pre"]], [(d, CDT)], ts=512, name="f_norm_pre")[0]
    qkv, z, xbc_raw, mq, gl = [
        _mm(u, seg_src[i], tb=True, b_off=seg_off[i], n=seg_n[i], out_dtype=seg_dtype[i],
            name="f_in_" + seg_name[i]) for i in range(5)]
    dt_raw = _mm(u, w_dt, tb=True, name="f_in_dt")

    bias128 = _pad_cols(p["dt_bias"], DT_PAD)
    alog128 = _pad_cols(p["a_log"], DT_PAD)

    def dt_fn(dtr, bias, alog):
        dt = _softplus(dtr + bias)
        return dt, dt * (-jnp.exp(alog))

    dt128, da128 = _rows(dt_fn, [dt_raw], [bias128, alog128], [(DT_PAD, F32), (DT_PAD, F32)],
                         ts=512, name="f_dt")
    dtc = _group_cols(dt128[:, :SSD_H])
    dac = _group_cols(da128[:, :SSD_H])
    dar = dac.transpose(0, 2, 1)
    dsk = p["d_skip"].reshape(SSD_G, 1, SSD_R)

    conv_w, conv_b = p["conv_w"], p["conv_b"]
    pre, xbc = _conv_fwd(xbc_raw, conv_w, conv_b)
    xh = _heads(xbc[:, :SSD_INNER], SSD_H, SSD_P)
    y_h, st = _ssd_fwd_g(xh, dtc, dac, dar, dsk, xbc)
    y_core = _unheads(y_h)

    def group_norm_fwd(yv, zv, wn):
        y2 = yv * _silu(zv)
        gw = SSD_INNER // SSD_G
        outs = []
        for gi in range(SSD_G):
            seg = y2[:, gi * gw:(gi + 1) * gw]
            outs.append(_rms(seg, wn[:, gi * gw:(gi + 1) * gw]))
        return jnp.concatenate(outs, axis=1)

    y_ssd = _rows(group_norm_fwd, [y_core, z], [p["ssd_norm"]], [(SSD_INNER, CDT)], ts=256, name="f_ssd_post")[0]

    y_sb, lt_h, *late = _sb_fwd_pairs(qkv, tuple(shards) if shards is not None else ())
    for n, gth in zip(LATE_W, late):
        wt[n] = _full_weight(n, gth)

    mu = _rows(lambda mv, g: _rms(mv, g), [mem], [p["norm_mem"]], [(d, CDT)], ts=256, name="f_norm_mem")[0]
    kv = _mm(mu, wt["w_mem_kv"], out_dtype=CDT, name="f_mem_kv")
    y_mem = _mem_fwd(mq, kv)

    p_sb = _mm(y_sb, wt["w_sb_out"], name="f_sb_out")
    p_ssd = _mm(y_ssd, wt["w_ssd_out"], name="f_ssd_out")
    p_mem = _mm(y_mem, wt["w_mem_out"], name="f_mem_out")

    def merge_fn(glv, a, b, c):
        return (_sigmoid(glv[:, :d]) * a + _sigmoid(glv[:, d:2 * d]) * b + _sigmoid(glv[:, 2 * d:]) * c)

    merged = _rows(merge_fn, [gl, p_sb, p_ssd, p_mem], [], [(d, CDT)], ts=256, name="f_merge")[0]
    mix = _mm(merged, wt["w_o"], name="f_w_o")

    def mid_fn(xv, mixv, g_post, g_pre):
        h1 = xv + _rms(mixv, g_post)
        return h1, _rms(h1, g_pre)

    h1, u2 = _rows(mid_fn, [x, mix], [p["norm_mix_post"], p["norm_mlp_pre"]], [(d, F32), (d, CDT)],
                   ts=512, name="f_mid")
    a1, act = _mm(u2, wt["w_up"], name="f_up",
                  epilogue=(lambda pv: (pv, jnp.square(jnp.maximum(pv, 0.0))), [], [F32, CDT]))
    ff = _mm(act, wt["w_down"], name="f_down")

    def loss_fn(h1v, ffv, tv, g):
        diff = h1v + _rms(ffv, g) - tv
        tot = jnp.sum(jnp.sum(diff * diff, axis=1, keepdims=True), axis=0, keepdims=True)
        return diff * (1.0 / d), tot

    dh2, loss_acc = _rows(loss_fn, [h1, ff, tgt], [p["norm_mlp_post"]], [(d, F32)], [(1, 128)],
                          ts=512, name="f_loss")
    loss = loss_acc[:, :1] * (0.5 / d)

    sg = {}

    def b_post(ffv, dyv, g):
        dx, dg = _rms_bwd(ffv, g, dyv)
        return dx, dg

    d_ff, sg["norm_mlp_post"] = _rows(b_post, [ff, dh2], [p["norm_mlp_post"]], [(d, CDT)], [(1, d)],
                                      ts=512, name="b_norm_mlp_post")
    da1 = _mm(d_ff, wt["w_down"], tb=True, name="b_down_x",
              epilogue=(lambda pv, a: (pv * 2.0 * jnp.maximum(a, 0.0),), [a1], [CDT]))[0]
    gw = {"w_down": _mm(act, d_ff, ta=True, name="b_down_w")}
    du2 = _mm(da1, wt["w_up"], tb=True, name="b_up_x")
    gw["w_up"] = _mm(u2, da1, ta=True, name="b_up_w")

    def b_mid(h1v, du2v, dh2v, mixv, g_pre, g_post):
        dxa, dga = _rms_bwd(h1v, g_pre, du2v)
        dh1 = dh2v + dxa
        dmix, dgb = _rms_bwd(mixv, g_post, dh1)
        return dh1, dmix, dga, dgb

    dh1, dmix, sg["norm_mlp_pre"], sg["norm_mix_post"] = _rows(
        b_mid, [h1, du2, dh2, mix], [p["norm_mlp_pre"], p["norm_mix_post"]],
        [(d, F32), (d, CDT)], [(1, d), (1, d)], ts=256, name="b_mid")
    dmerged = _mm(dmix, wt["w_o"], tb=True, name="b_w_o_x")
    gw["w_o"] = _mm(merged, dmix, ta=True, name="b_w_o_w")

    def b_merge(dm, glv, a, b, c):
        outs, dgl = [], []
        for i, br in enumerate((a, b, c)):
            gt = _sigmoid(glv[:, i * d:(i + 1) * d])
            outs.append(gt * dm)
            dgl.append(dm * br * gt * (1.0 - gt))
        return outs[0], outs[1], outs[2], jnp.concatenate(dgl, axis=1)

    dp_sb, dp_ssd, dp_mem, dgl = _rows(b_merge, [dmerged, gl, p_sb, p_ssd, p_mem], [],
                                       [(d, CDT), (d, CDT), (d, CDT), (3 * d, CDT)], ts=256, name="b_merge")
    dy_sb = _mm(dp_sb, wt["w_sb_out"], tb=True, name="b_sb_out_x")
    gw["w_sb_out"] = _mm(y_sb, dp_sb, ta=True, name="b_sb_out_w")
    dy_ssd = _mm(dp_ssd, wt["w_ssd_out"], tb=True, name="b_ssd_out_x")
    gw["w_ssd_out"] = _mm(y_ssd, dp_ssd, ta=True, name="b_ssd_out_w")
    dy_mem = _mm(dp_mem, wt["w_mem_out"], tb=True, out_dtype=CDT, name="b_mem_out_x")
    gw["w_mem_out"] = _mm(y_mem, dp_mem, ta=True, name="b_mem_out_w")

    dmq, dk_m, dv_m = _mem_bwd(mq, kv, dy_mem)
    dkv = jnp.concatenate([dk_m, dv_m], axis=1).astype(CDT)
    gw["w_mem_kv"] = _mm(mu, dkv, ta=True, name="b_mem_kv_w")
    dmu = _mm(dkv, wt["w_mem_kv"], tb=True, name="b_mem_kv_x")
    sg["norm_mem"] = _rows(lambda mv, dv, g: _rms_bwd(mv, g, dv)[1], [mem, dmu], [p["norm_mem"]], [], [(1, d)],
                           ts=256, name="b_norm_mem")[0]

    payloads = tuple(_grad_payload(n, gw[n]) for n in LATE_W) if shards is not None else ()
    dq, dk, dv, *received = _sb_bwd_pairs(qkv, lt_h, dy_sb, payloads)
    dqkv = jnp.concatenate([dq, dk, dv], axis=1).astype(CDT)

    def group_norm_bwd(dyo, yv, zv, wn):
        sz = _silu(zv)
        y2 = yv * sz
        gw_ = SSD_INNER // SSD_G
        dy2, dwn = [], []
        for gi in range(SSD_G):
            sl = slice(gi * gw_, (gi + 1) * gw_)
            dxs, dgs = _rms_bwd(y2[:, sl], wn[:, sl], dyo[:, sl])
            dy2.append(dxs)
            dwn.append(dgs)
        dy2 = jnp.concatenate(dy2, axis=1)
        return dy2 * sz, dy2 * yv * _dsilu(zv), jnp.concatenate(dwn, axis=1)

    dy_core, dz, sg["ssd_norm"] = _rows(group_norm_bwd, [dy_ssd, y_core, z], [p["ssd_norm"]],
                                        [(SSD_INNER, F32), (SSD_INNER, CDT)], [(1, SSD_INNER)],
                                        ts=256, name="b_ssd_post")
    dxh, dda, dxx, d_b, d_c, dah, ddsk = _ssd_bwd_g(xh, dtc, dac, dar, dsk, xbc, st, _heads(dy_core, SSD_H, SSD_P))
    sg["d_skip"] = ddsk.reshape(1, SSD_H)
    sg["a_log"] = dah.reshape(1, SSD_H) * (-jnp.exp(p["a_log"]))

    def b_dt(ddav, dxxv, dtr, bias, alog):
        ddt = ddav * (-jnp.exp(alog)) + dxxv
        draw = ddt * _sigmoid(dtr + bias)
        return draw, jnp.sum(draw, axis=0, keepdims=True)

    ungroup = lambda t: _pad_cols(t.transpose(1, 0, 2).reshape(s, SSD_H), DT_PAD)
    ddt_raw, dbias128 = _rows(b_dt, [ungroup(dda), ungroup(dxx), dt_raw], [bias128, alog128],
                              [(DT_PAD, CDT)], [(1, DT_PAD)], ts=512, name="b_dt")
    sg["dt_bias"] = dbias128[:, :SSD_H]

    dxbc = jnp.concatenate([_unheads(dxh), d_b, d_c], axis=1)
    dxbc_raw, sg["conv_w"], sg["conv_b"] = _conv_bwd(xbc_raw, pre, dxbc, conv_w)

    dseg = [dqkv, dz, dxbc_raw, dmq, dgl]
    dw_bufs = [lax.empty((c3, d), F32), lax.empty((c5 - c4 + 3 * d, d), F32)]
    for i in range(5):
        bi = 0 if i < 3 else 1
        dw_bufs[bi] = _mm(dseg[i], u, ta=True, into=dw_bufs[bi], into_off=seg_off[i], into_rows=True,
                          name="b_in_w_" + seg_name[i])
    dw_dt = _mm(ddt_raw, u, ta=True, name="b_in_w_dt")
    hosts = {0: _pair_swap([dw_bufs[0]]), 2: _pair_swap([dw_bufs[1], dw_dt])} if shards is not None else {}
    dus, from_sibling = [], []
    for i in range(5):
        res = _mm(dseg[i], seg_src[i], b_koff=seg_off[i], name="b_in_x_" + seg_name[i], hosted=hosts.get(i))
        if i in hosts:
            dus.append(res[0])
            from_sibling += res[1:]
        else:
            dus.append(res)
    dus.append(_mm(ddt_raw, w_dt, name="b_in_x_dt"))
    if shards is None:
        gw["w_in"] = jnp.concatenate([dw_bufs[0], dw_dt[:SSD_H], dw_bufs[1]], axis=0).T
    else:
        sums = [_add_cast(t, r, "sum_w_in_" + nm)
                for t, r, nm in zip([dw_bufs[0], dw_bufs[1], dw_dt], from_sibling, ["head", "tail", "dt"])]
        gw["w_in"] = jnp.concatenate([sums[0], sums[2][:SSD_H], sums[1]], axis=0)

    def b_pre(xv, dh1v, d0, d1, d2, d3, d4, d5, g):
        dx, dg = _rms_bwd(xv, g, d0 + d1 + d2 + d3 + d4 + d5)
        return dh1v + dx, dg

    grad_x, sg["norm_mix_pre"] = _rows(b_pre, [x, dh1] + dus, [p["norm_mix_pre"]], [(d, F32)], [(1, d)],
                                       ts=256, name="b_norm_pre")
    return loss, grad_x, gw, sg, (received if shards is not None else None)


HBM = pl.BlockSpec(memory_space=pltpu.HBM)
MESH = pl.DeviceIdType.MESH


def _me_and_peers():
    x, y, c = lax.axis_index("x"), lax.axis_index("y"), lax.axis_index("c")
    me = 4 * x + 2 * y + c
    peers = [(x, y, 1 - c), (1 - x, y, c), (x, 1 - y, c), (1 - x, 1 - y, c),
             (1 - x, y, 1 - c), (x, 1 - y, 1 - c), (1 - x, 1 - y, 1 - c)]
    return me, peers


def _peer_index(peer):
    return 4 * peer[0] + 2 * peer[1] + peer[2]


def _exchange_copies(ins, outs, send_sems, recv_sems, local_sems, scatter):
    me, peers = _me_and_peers()
    copies = []
    for a in range(len(ins)):
        own = ins[a].at[me] if scatter else ins[a]
        copies.append(pltpu.make_async_copy(own, outs[a].at[me], local_sems.at[a]))
        for kk, peer in enumerate(peers):
            src = ins[a].at[_peer_index(peer)] if scatter else ins[a]
            copies.append(pltpu.make_async_remote_copy(
                src_ref=src, dst_ref=outs[a].at[me],
                send_sem=send_sems.at[a, kk], recv_sem=recv_sems.at[a, kk],
                device_id=peer, device_id_type=MESH))
    return copies


def _exchange_shapes(ins, scatter):
    return [jax.ShapeDtypeStruct(t.shape if scatter else (N_DEV,) + t.shape, t.dtype) for t in ins]


def _exchange_sems(n):
    return [pltpu.SemaphoreType.DMA((n, N_DEV - 1)), pltpu.SemaphoreType.DMA((n, N_DEV - 1)),
            pltpu.SemaphoreType.DMA((n,))]


def _exchange(ins, scatter, name):
    n = len(ins)

    def body(*refs):
        copies = _exchange_copies(refs[:n], refs[n:2 * n], *refs[2 * n:], scatter)
        for cp in copies:
            cp.start()
        for cp in copies:
            cp.wait()

    return pl.pallas_call(
        body, name=name,
        in_specs=[HBM] * n, out_specs=[HBM] * n,
        out_shape=_exchange_shapes(ins, scatter),
        scratch_shapes=_exchange_sems(n),
        compiler_params=pltpu.CompilerParams(has_side_effects=True),
    )(*ins)


def _gather_two_level(shards, name):
    n = len(shards)

    def body(*refs):
        ins, outs = refs[:n], refs[n:2 * n]
        send_sems, recv_sems, local_sems = refs[2 * n:]
        x, y, c = lax.axis_index("x"), lax.axis_index("y"), lax.axis_index("c")
        me, sib = (x, y, c), (x, y, 1 - c)
        chips = [(1 - x, y), (x, 1 - y), (1 - x, 1 - y)]

        def copy(a, k, block, to, src=None):
            slot = outs[a].at[_peer_index(block)]
            return pltpu.make_async_remote_copy(
                src_ref=slot if src is None else src, dst_ref=slot,
                send_sem=send_sems.at[a, k], recv_sem=recv_sems.at[a, k], device_id=to, device_id_type=MESH)

        own = [pltpu.make_async_copy(ins[a], outs[a].at[_peer_index(me)], local_sems.at[a]) for a in range(n)]
        first = []
        for a in range(n):
            first.append(copy(a, 0, me, sib, src=ins[a]))
            first += [copy(a, 1 + j, me, (*chip, c), src=ins[a]) for j, chip in enumerate(chips)]
        for cp in own + first:
            cp.start()
        passed = []
        for j, chip in enumerate(chips):
            for a in range(n):
                copy(a, 1 + j, (*chip, c), me).wait_recv()
                fwd = copy(a, 4 + j, (*chip, c), sib)
                fwd.start()
                passed.append(fwd)
        for a in range(n):
            copy(a, 0, sib, me).wait_recv()
            for j, chip in enumerate(chips):
                copy(a, 4 + j, (*chip, 1 - c), me).wait_recv()
        for cp in first + passed:
            cp.wait_send()
        for cp in own:
            cp.wait()

    return pl.pallas_call(
        body, name=name,
        in_specs=[HBM] * n, out_specs=[HBM] * n,
        out_shape=_exchange_shapes(shards, False),
        scratch_shapes=_exchange_sems(n),
        compiler_params=pltpu.CompilerParams(has_side_effects=True),
    )(*shards)


class _Hosted:
    def __init__(self, ins, shapes, sems, copies):
        self.ins, self.shapes, self.sems, self.copies = ins, shapes, sems, copies


def _pair_swap(arrays):
    n = len(arrays)

    def copies(in_refs, out_refs, sems):
        sib = (lax.axis_index("x"), lax.axis_index("y"), 1 - lax.axis_index("c"))
        return [pltpu.make_async_remote_copy(
            src_ref=in_refs[i], dst_ref=out_refs[i], send_sem=sems[0].at[i], recv_sem=sems[1].at[i],
            device_id=sib, device_id_type=MESH) for i in range(n)]

    return _Hosted(list(arrays), [jax.ShapeDtypeStruct(t.shape, t.dtype) for t in arrays],
                   [pltpu.SemaphoreType.DMA((n,)), pltpu.SemaphoreType.DMA((n,))], copies)


def _add_cast(a, b, name):
    r, w = a.shape
    tr = _pick(r, (256, 128))
    spec = pl.BlockSpec((tr, w), lambda i: (i, 0))

    def body(a_ref, b_ref, o_ref):
        o_ref[...] = (a_ref[...] + b_ref[...]).astype(o_ref.dtype)

    return pl.pallas_call(
        body, name=name, grid=(r // tr,), in_specs=[spec, spec], out_specs=spec,
        out_shape=jax.ShapeDtypeStruct((r, w), CDT),
        compiler_params=_cparams(("parallel",)),
    )(a, b)


N_CHIP = 4


def _chip_scatter(t, name):
    def body(t_ref, o_ref, send_sems, recv_sems, local_sem):
        x, y, c = lax.axis_index("x"), lax.axis_index("y"), lax.axis_index("c")
        mine = 2 * x + y
        copies = [pltpu.make_async_copy(t_ref.at[mine], o_ref.at[mine], local_sem)]
        for j, (px, py) in enumerate([(1 - x, y), (x, 1 - y), (1 - x, 1 - y)]):
            copies.append(pltpu.make_async_remote_copy(
                src_ref=t_ref.at[2 * px + py], dst_ref=o_ref.at[mine],
                send_sem=send_sems.at[j], recv_sem=recv_sems.at[j], device_id=(px, py, c), device_id_type=MESH))
        for cp in copies:
            cp.start()
        for cp in copies:
            cp.wait()

    return pl.pallas_call(
        body, name=name, in_specs=[HBM], out_specs=HBM,
        out_shape=jax.ShapeDtypeStruct(t.shape, t.dtype),
        scratch_shapes=[pltpu.SemaphoreType.DMA((N_CHIP - 1,)), pltpu.SemaphoreType.DMA((N_CHIP - 1,)),
                        pltpu.SemaphoreType.DMA],
        compiler_params=pltpu.CompilerParams(has_side_effects=True),
    )(t)


def _all_reduce_small(v, name):
    r, c = v.shape

    def body(v_ref, o_ref, buf, send_sems, recv_sems):
        me, peers = _me_and_peers()
        buf[me] = v_ref[...]
        copies = []
        for kk, peer in enumerate(peers):
            cp = pltpu.make_async_remote_copy(
                src_ref=v_ref, dst_ref=buf.at[me],
                send_sem=send_sems.at[kk], recv_sem=recv_sems.at[kk],
                device_id=peer, device_id_type=MESH)
            cp.start()
            copies.append(cp)
        for cp in copies:
            cp.wait()
        acc = buf[0]
        for i in range(1, N_DEV):
            acc = acc + buf[i]
        o_ref[...] = acc

    return pl.pallas_call(
        body, name=name,
        in_specs=[pl.BlockSpec(memory_space=pltpu.VMEM)],
        out_specs=pl.BlockSpec(memory_space=pltpu.VMEM),
        out_shape=jax.ShapeDtypeStruct((r, c), F32),
        scratch_shapes=[pltpu.VMEM((N_DEV, r, c), F32),
                        pltpu.SemaphoreType.DMA((N_DEV - 1,)), pltpu.SemaphoreType.DMA((N_DEV - 1,))],
        compiler_params=pltpu.CompilerParams(has_side_effects=True),
    )(v)


def _adamw_math(g, w, m, v):
    m2 = ADAM_B1 * m + (1.0 - ADAM_B1) * g
    v2 = ADAM_B2 * v + (1.0 - ADAM_B2) * jnp.square(g)
    m_hat = m2 / (1.0 - ADAM_B1 ** ADAM_STEP)
    v_hat = v2 / (1.0 - ADAM_B2 ** ADAM_STEP)
    delta = -ADAM_LR * (m_hat / (jnp.sqrt(v_hat) + ADAM_EPS) + ADAM_WD * w)
    return delta, m2, v2


def _adamw_reduce(parts, w, m, v, name):
    r, c = w.shape
    nparts = parts.shape[0]
    tr = _pick(r, (128, 64, 32, 16, 8))

    def body(p_ref, w_ref, m_ref, v_ref, g_ref, d_ref, m2_ref, v2_ref):
        g = p_ref[0].astype(F32)
        for i in range(1, nparts):
            g = g + p_ref[i].astype(F32)
        delta, m2, v2 = _adamw_math(g, w_ref[...], m_ref[...], v_ref[...])
        g_ref[...] = g
        d_ref[...] = delta
        m2_ref[...] = m2
        v2_ref[...] = v2

    tile = pl.BlockSpec((tr, c), lambda i: (i, 0))
    shp = jax.ShapeDtypeStruct((r, c), F32)
    return pl.pallas_call(
        body, name=name, grid=(r // tr,),
        in_specs=[pl.BlockSpec((nparts, tr, c), lambda i: (0, i, 0)), tile, tile, tile],
        out_specs=[tile] * 4, out_shape=[shp] * 4,
        compiler_params=_cparams(("parallel",)),
    )(parts, w, m, v)


ROW_ALIGN = 16


def _window(shard):
    lead = max((j * shard) % ROW_ALIGN for j in range(N_DEV))
    return -(-(lead + shard) // ROW_ALIGN) * ROW_ALIGN


def _chip_scatter_windows(t, shard, name):
    cols = t.shape[1]
    win = _window(shard)
    assert all((j * shard // ROW_ALIGN) * ROW_ALIGN + win <= t.shape[0] for j in range(N_DEV))

    def body(t_ref, o_ref, send_sems, recv_sems, local_sem):
        x, y, c = lax.axis_index("x"), lax.axis_index("y"), lax.axis_index("c")
        mine = 2 * x + y

        def window(q):
            a0 = pl.multiple_of(((2 * q + c) * shard // ROW_ALIGN) * ROW_ALIGN, ROW_ALIGN)
            return t_ref.at[pl.ds(a0, win)]

        copies = [pltpu.make_async_copy(window(mine), o_ref.at[mine], local_sem)]
        for j, (px, py) in enumerate([(1 - x, y), (x, 1 - y), (1 - x, 1 - y)]):
            copies.append(pltpu.make_async_remote_copy(
                src_ref=window(2 * px + py), dst_ref=o_ref.at[mine],
                send_sem=send_sems.at[j], recv_sem=recv_sems.at[j], device_id=(px, py, c), device_id_type=MESH))
        for cp in copies:
            cp.start()
        for cp in copies:
            cp.wait()

    return pl.pallas_call(
        body, name=name, in_specs=[HBM], out_specs=HBM,
        out_shape=jax.ShapeDtypeStruct((N_CHIP, win, cols), t.dtype),
        scratch_shapes=[pltpu.SemaphoreType.DMA((N_CHIP - 1,)), pltpu.SemaphoreType.DMA((N_CHIP - 1,)),
                        pltpu.SemaphoreType.DMA],
        compiler_params=pltpu.CompilerParams(has_side_effects=True),
    )(t)


ADAMW_TC = 256


def _adamw_windows(parts, w, m, v, name):
    r, c = w.shape
    nparts, win, _ = parts.shape
    tc = min(ADAMW_TC, c)

    def body(p_ref, w_ref, m_ref, v_ref, g_ref, d_ref, m2_ref, v2_ref):
        me, _ = _me_and_peers()
        gw_ = p_ref[0].astype(F32)
        for i in range(1, nparts):
            gw_ = gw_ + p_ref[i].astype(F32)
        for j in range(N_DEV):
            @pl.when(me == j)
            def _():
                lead = (j * r) % ROW_ALIGN
                g = (pltpu.roll(gw_, win - lead, 0) if lead else gw_)[:r]
                delta, m2, v2 = _adamw_math(g, w_ref[...], m_ref[...], v_ref[...])
                g_ref[...] = g
                d_ref[...] = delta
                m2_ref[...] = m2
                v2_ref[...] = v2

    tile = pl.BlockSpec((r, tc), lambda i: (0, i))
    shp = jax.ShapeDtypeStruct((r, c), F32)
    return pl.pallas_call(
        body, name=name, grid=(c // tc,),
        in_specs=[pl.BlockSpec((nparts, win, tc), lambda i: (0, 0, i)), tile, tile, tile],
        out_specs=[tile] * 4, out_shape=[shp] * 4,
        compiler_params=_cparams(("parallel",)),
    )(parts, w, m, v)


def _adamw_plain(g, w, m, v, name):
    def body(g_ref, w_ref, m_ref, v_ref, d_ref, m2_ref, v2_ref):
        delta, m2, v2 = _adamw_math(g_ref[...], w_ref[...], m_ref[...], v_ref[...])
        d_ref[...] = delta
        m2_ref[...] = m2
        v2_ref[...] = v2

    spec = pl.BlockSpec(memory_space=pltpu.VMEM)
    shp = jax.ShapeDtypeStruct(g.shape, F32)
    return pl.pallas_call(
        body, name=name, in_specs=[spec] * 4, out_specs=[spec] * 3, out_shape=[shp] * 3,
    )(g, w, m, v)


SMALL_ROWS, SMALL_COLS = 16, 3072


def _small_step(sg, gcw, loss, ws, ms, vs):
    ns = len(sg)
    widths = [t.shape[1] for t in sg]
    kk_, ch = gcw.shape[1], gcw.shape[2]
    assert ns < SMALL_ROWS and max(widths) <= SMALL_COLS

    def reduce_body(*refs):
        g_refs = refs[:ns]
        gcw_ref, loss_ref, tot_ref, totc_ref = refs[ns:ns + 4]
        mine, buf, minec, bufc, send_sems, recv_sems = refs[ns + 4:]
        me, peers = _me_and_peers()

        mine[...] = jnp.zeros_like(mine)
        for i in range(ns):
            mine[i:i + 1, 0:widths[i]] = g_refs[i][...]
        mine[ns:ns + 1, 0:LANES] = jnp.broadcast_to(loss_ref[...], (1, LANES))
        minec[...] = gcw_ref[...]
        buf[me] = mine[...]
        bufc[me] = minec[...]
        copies = []
        for j, peer in enumerate(peers):
            copies.append(pltpu.make_async_remote_copy(
                src_ref=mine, dst_ref=buf.at[me], send_sem=send_sems.at[0, j], recv_sem=recv_sems.at[0, j],
                device_id=peer, device_id_type=MESH))
            copies.append(pltpu.make_async_remote_copy(
                src_ref=minec, dst_ref=bufc.at[me], send_sem=send_sems.at[1, j], recv_sem=recv_sems.at[1, j],
                device_id=peer, device_id_type=MESH))
        for cp in copies:
            cp.start()
        for cp in copies:
            cp.wait()
        tot = buf[0]
        totc = bufc[0]
        for i in range(1, N_DEV):
            tot = tot + buf[i]
            totc = totc + bufc[i]
        tot_ref[...] = tot
        totc_ref[...] = totc

    vm = pl.BlockSpec(memory_space=pltpu.VMEM)
    tot, totc = pl.pallas_call(
        reduce_body, name="small_reduce",
        in_specs=[vm] * (ns + 2), out_specs=[vm, vm],
        out_shape=[jax.ShapeDtypeStruct((SMALL_ROWS, SMALL_COLS), F32), jax.ShapeDtypeStruct((N_DEV, kk_, ch), F32)],
        scratch_shapes=[pltpu.VMEM((SMALL_ROWS, SMALL_COLS), F32), pltpu.VMEM((N_DEV, SMALL_ROWS, SMALL_COLS), F32),
                        pltpu.VMEM((N_DEV, kk_, ch), F32), pltpu.VMEM((N_DEV, N_DEV, kk_, ch), F32),
                        pltpu.SemaphoreType.DMA((2, N_DEV - 1)), pltpu.SemaphoreType.DMA((2, N_DEV - 1))],
        compiler_params=pltpu.CompilerParams(has_side_effects=True),
    )(*sg, gcw, loss)

    def adamw_body(*refs):
        tot_ref, totc_ref = refs[:2]
        w_refs, m_refs, v_refs = (refs[2 + i * (ns + 1):2 + (i + 1) * (ns + 1)] for i in range(3))
        outs = refs[3 * ns + 5:]
        loss_out = outs[0]
        go, do_, mo, vo = (outs[1 + i * (ns + 1):1 + (i + 1) * (ns + 1)] for i in range(4))
        me, _ = _me_and_peers()
        loss_out[...] = tot_ref[ns:ns + 1, 0:1]
        for i in range(ns + 1):
            g = tot_ref[i:i + 1, 0:widths[i]] if i < ns else totc_ref[me]
            delta, m2, v2 = _adamw_math(g, w_refs[i][...], m_refs[i][...], v_refs[i][...])
            go[i][...] = g
            do_[i][...] = delta
            mo[i][...] = m2
            vo[i][...] = v2

    shapes = [jax.ShapeDtypeStruct(t.shape, F32) for t in ws]
    res = pl.pallas_call(
        adamw_body, name="small_adamw",
        in_specs=[vm] * (3 * ns + 5), out_specs=[vm] * (4 * ns + 5),
        out_shape=[jax.ShapeDtypeStruct((1, 1), F32)] + shapes * 4,
    )(tot, totc, *ws, *ms, *vs)
    n1 = ns + 1
    return res[0], res[1:1 + n1], res[1 + n1:1 + 2 * n1], res[1 + 2 * n1:1 + 3 * n1], res[1 + 3 * n1:]


def _cast_shard(w, name):
    r = w.shape[0]
    return _rows(lambda t: t, [w], [], [(w.shape[1], CDT)], ts=_pick(r, (256, 128)), name=name)[0]


BIG = ["w_in", "w_mem_kv", "w_up", "w_sb_out", "w_ssd_out", "w_mem_out", "w_o", "w_down"]
LATE_W = BIG[1:]
COL_SHARDED = ("w_in", "w_mem_kv", "w_up")
SMALL = ["norm_mix_pre", "conv_b", "dt_bias", "a_log", "d_skip", "ssd_norm", "norm_mem",
         "norm_mix_post", "norm_mlp_pre", "norm_mlp_post"]
ALL_W = ["norm_mix_pre", "w_in", "conv_w", "conv_b", "dt_bias", "a_log", "d_skip", "ssd_norm", "norm_mem",
         "w_mem_kv", "w_sb_out", "w_ssd_out", "w_mem_out", "w_o", "norm_mix_post", "norm_mlp_pre", "w_up",
         "w_down", "norm_mlp_post"]
LANES = 128


def _pack_rows(vecs):
    parts, offs, off = [], [], 0
    for t in vecs:
        flat = t.reshape(-1)
        n = flat.shape[0]
        rows = -(-n // (8 * LANES)) * 8
        parts.append(jnp.pad(flat, (0, rows * LANES - n)).reshape(rows, LANES))
        offs.append((off, n))
        off += rows
    return jnp.concatenate(parts, axis=0), offs


def _unpack_rows(packed, offs, shapes):
    out = []
    for (off, n), shape in zip(offs, shapes):
        rows = -(-n // (8 * LANES)) * 8
        out.append(packed[off:off + rows].reshape(-1)[:n].reshape(shape))
    return out


def kernel(x, mem, norm_mix_pre, w_in, conv_w, conv_b, dt_bias, a_log, d_skip, ssd_norm, norm_mem, w_mem_kv, w_sb_out, w_ssd_out, w_mem_out, w_o, norm_mix_post, norm_mlp_pre, w_up, w_down, norm_mlp_post, loss_target, m_norm_mix_pre, m_w_in, m_conv_w, m_conv_b, m_dt_bias, m_a_log, m_d_skip, m_ssd_norm, m_norm_mem, m_w_mem_kv, m_w_sb_out, m_w_ssd_out, m_w_mem_out, m_w_o, m_norm_mix_post, m_norm_mlp_pre, m_w_up, m_w_down, m_norm_mlp_post, v_norm_mix_pre, v_w_in, v_conv_w, v_conv_b, v_dt_bias, v_a_log, v_d_skip, v_ssd_norm, v_norm_mem, v_w_mem_kv, v_w_sb_out, v_w_ssd_out, v_w_mem_out, v_w_o, v_norm_mix_post, v_norm_mlp_pre, v_w_up, v_w_down, v_norm_mlp_post):
    wd = dict(norm_mix_pre=norm_mix_pre, w_in=w_in, conv_w=conv_w, conv_b=conv_b, dt_bias=dt_bias, a_log=a_log,
              d_skip=d_skip, ssd_norm=ssd_norm, norm_mem=norm_mem, w_mem_kv=w_mem_kv, w_sb_out=w_sb_out,
              w_ssd_out=w_ssd_out, w_mem_out=w_mem_out, w_o=w_o, norm_mix_post=norm_mix_post,
              norm_mlp_pre=norm_mlp_pre, w_up=w_up, w_down=w_down, norm_mlp_post=norm_mlp_post)
    md = dict(norm_mix_pre=m_norm_mix_pre, w_in=m_w_in, conv_w=m_conv_w, conv_b=m_conv_b, dt_bias=m_dt_bias,
              a_log=m_a_log, d_skip=m_d_skip, ssd_norm=m_ssd_norm, norm_mem=m_norm_mem, w_mem_kv=m_w_mem_kv,
              w_sb_out=m_w_sb_out, w_ssd_out=m_w_ssd_out, w_mem_out=m_w_mem_out, w_o=m_w_o,
              norm_mix_post=m_norm_mix_post, norm_mlp_pre=m_norm_mlp_pre, w_up=m_w_up, w_down=m_w_down,
              norm_mlp_post=m_norm_mlp_post)
    vd = dict(norm_mix_pre=v_norm_mix_pre, w_in=v_w_in, conv_w=v_conv_w, conv_b=v_conv_b, dt_bias=v_dt_bias,
              a_log=v_a_log, d_skip=v_d_skip, ssd_norm=v_ssd_norm, norm_mem=v_norm_mem, w_mem_kv=v_w_mem_kv,
              w_sb_out=v_w_sb_out, w_ssd_out=v_w_ssd_out, w_mem_out=v_w_mem_out, w_o=v_w_o,
              norm_mix_post=v_norm_mix_post, norm_mlp_pre=v_norm_mlp_pre, w_up=v_w_up, w_down=v_w_down,
              norm_mlp_post=v_norm_mlp_post)
    w_in_t, m_in_t, v_in_t = (t["w_in"][0].T for t in (wd, md, vd))
    shards = {n: _cast_shard(w_in_t if n == "w_in" else wd[n][0], "cast_" + n) for n in BIG}
    w_in_g, conv_w_g = _gather_two_level([shards["w_in"], wd["conv_w"][0]], "gather_w_in")
    wt = {"w_in": w_in_g.reshape(N_DEV * w_in_g.shape[1], w_in_g.shape[2])}
    ch = conv_w_g.shape[2]

    p = {n: wd[n] for n in SMALL}
    p["conv_w"] = conv_w_g.transpose(1, 0, 2).reshape(CONV_K, N_DEV * ch)
    loss, grad_x, gw, sg, late_received = _local_step(x[0], mem[0], loss_target[0], p, wt,
                                                      [shards[n] for n in LATE_W])

    received = dict(zip(LATE_W, late_received))
    w_in_windows = _chip_scatter_windows(gw["w_in"], w_in_t.shape[0], "scatter_w_in_chips")

    grads, deltas, new_m, new_v = {}, {}, {}, {}
    for n in BIG:
        if n == "w_in":
            res = [t.T for t in _adamw_windows(w_in_windows, w_in_t, m_in_t, v_in_t, "adamw_" + n)]
        else:
            res = _adamw_reduce(received[n], wd[n][0], md[n][0], vd[n][0], "adamw_" + n)
        grads[n], deltas[n], new_m[n], new_v[n] = (t[None] for t in res)
    small_names = SMALL + ["conv_w"]
    gcw = sg["conv_w"].reshape(CONV_K, N_DEV, ch).transpose(1, 0, 2)
    small_of = lambda dct: [dct[n] for n in SMALL] + [dct["conv_w"][0]]
    loss_red, g_s, d_s, m_s, v_s = _small_step([sg[n] for n in SMALL], gcw, loss, small_of(wd), small_of(md),
                                               small_of(vd))
    for i, n in enumerate(small_names):
        shape = wd[n].shape
        grads[n], deltas[n], new_m[n], new_v[n] = (t.reshape(shape) for t in (g_s[i], d_s[i], m_s[i], v_s[i]))
    loss_out = loss_red.reshape(())

    return (loss_out, grad_x[None], *[grads[n] for n in ALL_W], *[deltas[n] for n in ALL_W],
            *[new_m[n] for n in ALL_W], *[new_v[n] for n in ALL_W])
```

```python
import functools

import jax
import jax.numpy as jnp
from jax import lax
from jax.experimental import pallas as pl
from jax.experimental.pallas import tpu as pltpu

F32 = jnp.float32
BF16 = jnp.bfloat16
CDT = jnp.bfloat16
EPS = 1e-6
VMEM_LIMIT = 56 * 1024 * 1024

N_DEV = 8
D_MODEL = 1024
SB_H, SB_DH = 16, 64
SSD_G, SSD_R, SSD_P, SSD_N, SSD_L = 4, 8, 64, 128, 128
SSD_H = SSD_G * SSD_R
SSD_INNER = SSD_H * SSD_P
CONV_K = 4
CONV_DIM = SSD_INNER + 2 * SSD_G * SSD_N
MEM_H, MEM_DH = 4, 256
DT_PAD = 128
SB_TQ, SB_BK = 2048, 256
CONV_PAD = 8
MM_TILE, MM_TILE_K = 1024, 2048

ADAM_LR, ADAM_B1, ADAM_B2, ADAM_EPS, ADAM_WD, ADAM_STEP = 0.001, 0.9, 0.999, 1e-08, 0.01, 10

NT = (((1,), (1,)), ((), ()))
TN = (((0,), (0,)), ((), ()))
NN = (((1,), (0,)), ((), ()))


def _cparams(sem=None):
    return pltpu.CompilerParams(dimension_semantics=sem, vmem_limit_bytes=VMEM_LIMIT)


def _pick(n, cands):
    for c in cands:
        if n % c == 0:
            return c
    return n


def _dot(a, b, dims=NN):
    return lax.dot_general(a.astype(CDT), b.astype(CDT), dims, preferred_element_type=F32)


def _split_dot(x, t, left, pieces):
    if CDT == F32:
        return lax.dot_general(t, x, NN, preferred_element_type=F32) if left else \
            lax.dot_general(x, t, NN, preferred_element_type=F32)
    acc = None
    rem = x
    for _ in range(pieces):
        hi = rem.astype(BF16)
        rem = rem - hi.astype(F32)
        d = lax.dot_general(t, hi, NN, preferred_element_type=F32) if left else \
            lax.dot_general(hi, t, NN, preferred_element_type=F32)
        acc = d if acc is None else acc + d
    return acc


def _iota(shape, dim):
    return lax.broadcasted_iota(jnp.int32, shape, dim)


def _sigmoid(x):
    return 1.0 / (1.0 + jnp.exp(-x))


def _silu(x):
    return x * _sigmoid(x)


def _dsilu(x):
    s = _sigmoid(x)
    return s * (1.0 + x * (1.0 - s))


def _softplus(x):
    return jnp.maximum(x, 0.0) + jnp.log(1.0 + jnp.exp(-jnp.abs(x)))


def _rms(x, g):
    r = lax.rsqrt(jnp.mean(x * x, axis=-1, keepdims=True) + EPS)
    return x * r * g


def _rms_bwd(x, g, dy):
    r = lax.rsqrt(jnp.mean(x * x, axis=-1, keepdims=True) + EPS)
    n = x * r
    dn = dy * g
    dx = r * (dn - n * jnp.mean(dn * n, axis=-1, keepdims=True))
    dg = jnp.sum(dy * n, axis=0, keepdims=True)
    return dx, dg


def _mm(a, b, *, ta=False, tb=False, out_dtype=F32, name, b_off=0, n=None, b_koff=0, into=None, into_off=0,
        into_rows=False, hosted=None, epilogue=None):
    m = a.shape[1] if ta else a.shape[0]
    k = a.shape[0] if ta else a.shape[1]
    if n is None:
        n = b.shape[0] if tb else b.shape[1]
    assert b_koff + k <= (b.shape[1] if tb else b.shape[0])
    bm = _pick(m, (MM_TILE, 512, 256, 128))
    bn = _pick(n, (MM_TILE, 512, 256, 128))
    bk = next(c for c in (MM_TILE_K, 1024, 512, 256, 128, k) if k % c == 0 and b_koff % c == 0)
    nk = k // bk
    assert b_off % bn == 0 and into_off % (bm if into_rows else bn) == 0
    jb, kb = b_off // bn, b_koff // bk
    io, jo = (into_off // bm, 0) if into_rows else (0, into_off // bn)
    dims = (((0 if ta else 1,), (1 if tb else 0,)), ((), ()))
    grid = (m // bm, n // bn, nk)
    off = 1 if into is not None else 0
    nh_in = len(hosted.ins) if hosted else 0
    nh_out = len(hosted.shapes) if hosted else 0
    epi_fn, extras, out_dtypes = epilogue if epilogue else (lambda p: (p,), [], [out_dtype])
    ne, no = len(extras), len(out_dtypes)
    assert not (epilogue and into is not None)

    def body(a_ref, b_ref, *rest):
        e_refs = rest[off:off + ne]
        o_refs = rest[off + ne + nh_in:off + ne + nh_in + no]
        acc_ref = rest[off + ne + nh_in + no + nh_out]

        def emit(total):
            res = epi_fn(total, *[e[...] for e in e_refs])
            for o_ref, val in zip(o_refs, res):
                o_ref[...] = val.astype(o_ref.dtype)

        if hosted:
            h_refs = (rest[off + ne:off + ne + nh_in],
                      rest[off + ne + nh_in + no:off + ne + nh_in + no + nh_out],
                      rest[off + ne + nh_in + no + nh_out + 1:])
            step = (pl.program_id(0) * grid[1] + pl.program_id(1)) * grid[2] + pl.program_id(2)

            @pl.when(step == 0)
            def _():
                for cp in hosted.copies(*h_refs):
                    cp.start()

        part = _dot(a_ref[...], b_ref[...], dims)
        if nk == 1:
            emit(part)
        else:
            kk = pl.program_id(2)

            @pl.when(kk == 0)
            def _():
                acc_ref[...] = part

            @pl.when(jnp.logical_and(kk > 0, kk < nk - 1))
            def _():
                acc_ref[...] += part

            @pl.when(kk == nk - 1)
            def _():
                emit(acc_ref[...] + part)

        if hosted:
            @pl.when(step == grid[0] * grid[1] * grid[2] - 1)
            def _():
                for cp in hosted.copies(*h_refs):
                    cp.wait()

    a_spec = pl.BlockSpec((bk, bm), lambda i, j, kk: (kk, i)) if ta else \
        pl.BlockSpec((bm, bk), lambda i, j, kk: (i, kk))
    b_spec = pl.BlockSpec((bn, bk), lambda i, j, kk: (j + jb, kk + kb)) if tb else \
        pl.BlockSpec((bk, bn), lambda i, j, kk: (kk + kb, j + jb))
    extra = {} if into is None else {"input_output_aliases": {2: 0}}
    out_shapes = [jax.ShapeDtypeStruct((m, n), dt) for dt in out_dtypes] if into is None else \
        [jax.ShapeDtypeStruct(into.shape, into.dtype)]
    block = pl.BlockSpec((bm, bn), lambda i, j, kk: (i, j))
    res = pl.pallas_call(
        body, name=name, grid=grid,
        in_specs=[a_spec, b_spec] + ([] if into is None else [pl.BlockSpec(memory_space=pl.ANY)])
        + [block] * ne + [HBM] * nh_in,
        out_specs=[pl.BlockSpec((bm, bn), lambda i, j, kk: (i + io, j + jo))] * no + [HBM] * nh_out,
        out_shape=out_shapes + (list(hosted.shapes) if hosted else []),
        scratch_shapes=[pltpu.VMEM((bm, bn) if nk > 1 else (8, 128), F32)] + (list(hosted.sems) if hosted else []),
        compiler_params=_cparams(("arbitrary",) * 3 if hosted else ("parallel", "parallel", "arbitrary")),
        **extra,
    )(*((a, b) if into is None else (a, b, into)), *extras, *(hosted.ins if hosted else ()))
    return res if (hosted or epilogue) else res[0]


def _rows(fn, tiled, params, outs, accs=(), *, ts, name):
    s = tiled[0].shape[0]
    ts = min(ts, s)
    assert s % ts == 0
    nt, npar, no, na = len(tiled), len(params), len(outs), len(accs)

    def body(*refs):
        i = pl.program_id(0)
        vals = [r[...] for r in refs[:nt + npar]]
        res = fn(*vals)
        if not isinstance(res, (tuple, list)):
            res = (res,)
        orefs = refs[nt + npar:nt + npar + no]
        arefs = refs[nt + npar + no:]
        for r_, val in zip(orefs, res[:no]):
            r_[...] = val.astype(r_.dtype)
        if na:
            @pl.when(i == 0)
            def _():
                for r_ in arefs:
                    r_[...] = jnp.zeros_like(r_)

            for r_, val in zip(arefs, res[no:]):
                r_[...] += jnp.broadcast_to(val, r_.shape)

    in_specs = [pl.BlockSpec((ts, a.shape[1]), lambda i: (i, 0)) for a in tiled]
    in_specs += [pl.BlockSpec(p.shape, lambda i: (0, 0)) for p in params]
    out_specs = [pl.BlockSpec((ts, w), lambda i: (i, 0)) for (w, _) in outs]
    out_specs += [pl.BlockSpec(shape, lambda i: (0, 0)) for shape in accs]
    out_shape = [jax.ShapeDtypeStruct((s, w), dt) for (w, dt) in outs]
    out_shape += [jax.ShapeDtypeStruct(shape, F32) for shape in accs]
    res = pl.pallas_call(
        body, name=name, grid=(s // ts,),
        in_specs=in_specs, out_specs=out_specs, out_shape=out_shape,
        compiler_params=_cparams(("arbitrary",)),
    )(*tiled, *params)
    return res


def _sb_block(qs, kb, diag):
    tq, bk = qs.shape[0], kb.shape[0]
    z = _dot(qs, kb, NT)
    lb = jnp.minimum(z, 0.0) - jnp.log(1.0 + jnp.exp(-jnp.abs(z)))
    lk = lb - z
    if diag is None:
        return None, lb, lk
    causal = (diag + _iota((tq, bk), 1)) < _iota((tq, bk), 0)
    return causal, lb, jnp.where(causal, lk, 0.0)


def _fused_exchange(scatter, ncols, nsteps):
    def hooks(ins, outs, sems):
        step = pl.program_id(0) * ncols + pl.program_id(1)

        @pl.when(step == 0)
        def _():
            for cp in _exchange_copies(ins, outs, *sems, scatter):
                cp.start()

        def finish():
            @pl.when(step == nsteps - 1)
            def _():
                for cp in _exchange_copies(ins, outs, *sems, scatter):
                    cp.wait()
        return finish
    return hooks


def _sb_fwd(q, k, v, comm=()):
    h, s, dh = q.shape
    tq, bk = min(SB_TQ, s), min(SB_BK, s)
    scale = dh ** -0.5
    nc = len(comm)
    hooks = _fused_exchange(False, s // tq, h * (s // tq))

    def body(q_ref, k_ref, v_ref, *rest):
        o_ref, lt_ref = rest[nc:nc + 2]
        if nc:
            finish = hooks(rest[:nc], rest[nc + 2:2 * nc + 2], rest[2 * nc + 2:])
        i = pl.program_id(1)
        q0 = i * tq
        qs = q_ref[...] * scale
        tri = (_iota((bk, bk), 0) > _iota((bk, bk), 1)).astype(CDT)

        def step(k0, carry, diag, r0=0):
            cf, acc = carry
            kb = k_ref[pl.ds(k0, bk), :]
            vb = v_ref[pl.ds(k0, bk), :]
            causal, lb, lk = _sb_block(qs[r0:], kb, diag)
            w = jnp.exp(lb + cf + _split_dot(lk, tri, False, 2))
            if causal is not None:
                w = jnp.where(causal, w, 0.0)
            return cf + jnp.sum(lk, axis=1, keepdims=True), acc + _dot(w, vb)

        carry = (jnp.zeros((tq, 1), F32), jnp.zeros((tq, dh), F32))
        for d in reversed(range(tq // bk)):
            r0 = d * bk
            sub = step(pl.multiple_of(q0 + r0, bk), tuple(t[r0:] for t in carry), 0, r0)
            carry = tuple(jnp.concatenate([t[:r0], u], axis=0) if r0 else u for t, u in zip(carry, sub))
        nfull = q0 // bk
        cf, acc = lax.fori_loop(
            0, nfull, lambda jj, c: step(pl.multiple_of((nfull - 1 - jj) * bk, bk), c, None), carry)
        o_ref[...] = acc
        lt_ref[...] = cf
        if nc:
            finish()

    return pl.pallas_call(
        body, name="sb_fwd", grid=(h, s // tq),
        in_specs=[pl.BlockSpec((None, tq, dh), lambda a, i: (a, i, 0)),
                  pl.BlockSpec((None, s, dh), lambda a, i: (a, 0, 0)),
                  pl.BlockSpec((None, s, dh), lambda a, i: (a, 0, 0))] + [HBM] * nc,
        out_specs=[pl.BlockSpec((None, tq, dh), lambda a, i: (a, i, 0)),
                   pl.BlockSpec((None, tq, 1), lambda a, i: (a, i, 0))] + [HBM] * nc,
        out_shape=[jax.ShapeDtypeStruct((h, s, dh), F32), jax.ShapeDtypeStruct((h, s, 1), F32)]
        + _exchange_shapes(comm, False),
        scratch_shapes=_exchange_sems(nc) if nc else [],
        compiler_params=_cparams(("arbitrary", "arbitrary")),
    )(q, k, v, *comm)


def _sb_bwd(q, k, v, ltot, do, comm=()):
    h, s, dh = q.shape
    tq, bk = min(SB_TQ, s), min(SB_BK, s)
    scale = dh ** -0.5
    nc = len(comm)
    hooks = _fused_exchange(True, s // tq, h * (s // tq))

    def body(q_ref, k_ref, v_ref, lt_ref, do_ref, *rest):
        dq_ref, dk_ref, dv_ref = rest[nc:nc + 3]
        if nc:
            finish = hooks(rest[:nc], rest[nc + 3:2 * nc + 3], rest[2 * nc + 3:])
        i = pl.program_id(1)

        @pl.when(i == 0)
        def _():
            dk_ref[...] = jnp.zeros_like(dk_ref)
            dv_ref[...] = jnp.zeros_like(dv_ref)

        q0 = i * tq
        qs = q_ref[...] * scale
        dov = do_ref[...].astype(CDT)
        ltot = lt_ref[...]
        tri_le = (_iota((bk, bk), 0) <= _iota((bk, bk), 1)).astype(CDT)
        tri_lt = (_iota((bk, bk), 0) < _iota((bk, bk), 1)).astype(CDT)

        def step(k0, carry, diag, r0=0):
            cf, cg, dq = carry
            kb = k_ref[pl.ds(k0, bk), :]
            vb = v_ref[pl.ds(k0, bk), :]
            causal, lb, lk = _sb_block(qs[r0:], kb, diag)
            w = jnp.exp(lb + ltot[r0:] - (cf + _split_dot(lk, tri_le, False, 2)))
            if causal is not None:
                w = jnp.where(causal, w, 0.0)
            g = w * _dot(dov[r0:], vb, NT)
            gsum = cg + _split_dot(g, tri_lt, False, 2)
            dz = g - (g + gsum) * jnp.exp(lb)
            if causal is not None:
                dz = jnp.where(causal, dz, 0.0)
            dzc = dz.astype(CDT)
            dk_ref[pl.ds(k0, bk), :] += _dot(dzc, qs[r0:], TN)
            dv_ref[pl.ds(k0, bk), :] += _dot(w, dov[r0:], TN)
            return (cf + jnp.sum(lk, axis=1, keepdims=True), cg + jnp.sum(g, axis=1, keepdims=True),
                    dq + _dot(dzc, kb))

        carry = (jnp.zeros((tq, 1), F32), jnp.zeros((tq, 1), F32), jnp.zeros((tq, dh), F32))
        carry = lax.fori_loop(0, q0 // bk, lambda jj, c: step(pl.multiple_of(jj * bk, bk), c, None), carry)
        for d in range(tq // bk):
            r0 = d * bk
            sub = step(pl.multiple_of(q0 + r0, bk), tuple(t[r0:] for t in carry), 0, r0)
            carry = tuple(jnp.concatenate([t[:r0], u], axis=0) if r0 else u for t, u in zip(carry, sub))
        dq_ref[...] = carry[2] * scale
        if nc:
            finish()

    tile = pl.BlockSpec((None, tq, dh), lambda a, i: (a, i, 0))
    full = pl.BlockSpec((None, s, dh), lambda a, i: (a, 0, 0))
    shp = jax.ShapeDtypeStruct((h, s, dh), F32)
    return pl.pallas_call(
        body, name="sb_bwd", grid=(h, s // tq),
        in_specs=[tile, full, full, pl.BlockSpec((None, tq, 1), lambda a, i: (a, i, 0)), tile] + [HBM] * nc,
        out_specs=[tile, full, full] + [HBM] * nc,
        out_shape=[shp, shp, shp] + _exchange_shapes(comm, True),
        scratch_shapes=_exchange_sems(nc) if nc else [],
        compiler_params=_cparams(("arbitrary", "arbitrary")),
    )(q, k, v, ltot, do, *comm)


SB_PAIR = 128


def _sb_fwd_pairs(qkv, comm=()):
    s, d3 = qkv.shape
    d = d3 // 3
    npair = d // SB_PAIR
    tq, bk = min(SB_TQ, s), min(SB_BK, s)
    scale = SB_DH ** -0.5
    nc = len(comm)
    hooks = _fused_exchange(False, s // tq, npair * (s // tq))

    def body(q_ref, k_ref, v_ref, *rest):
        y_ref, lt_ref = rest[nc:nc + 2]
        if nc:
            finish = hooks(rest[:nc], rest[nc + 2:2 * nc + 2], rest[2 * nc + 2:])
        i = pl.program_id(1)
        q0 = i * tq
        q2 = q_ref[...] * scale
        lane_head = (_iota((1, SB_PAIR), 1) >= SB_DH).astype(jnp.int32)
        tri = (_iota((bk, bk), 0) > _iota((bk, bk), 1)).astype(CDT)

        def head(hh, y):
            mine = lane_head == hh
            qs = jnp.where(mine, q2, jnp.zeros_like(q2))

            def step(k0, carry, diag, r0=0):
                cf, acc = carry
                kb = k_ref[pl.ds(k0, bk), :]
                vb = v_ref[pl.ds(k0, bk), :]
                causal, lb, lk = _sb_block(qs[r0:], kb, diag)
                w = jnp.exp(lb + cf + _split_dot(lk, tri, False, 2))
                if causal is not None:
                    w = jnp.where(causal, w, 0.0)
                return cf + jnp.sum(lk, axis=1, keepdims=True), acc + _dot(w, vb)

            carry = (jnp.zeros((tq, 1), F32), jnp.zeros((tq, SB_PAIR), F32))
            for dd in reversed(range(tq // bk)):
                r0 = dd * bk
                sub = step(pl.multiple_of(q0 + r0, bk), tuple(t[r0:] for t in carry), 0, r0)
                carry = tuple(jnp.concatenate([t[:r0], u], axis=0) if r0 else u for t, u in zip(carry, sub))
            nfull = q0 // bk
            cf, acc = lax.fori_loop(
                0, nfull, lambda jj, c: step(pl.multiple_of((nfull - 1 - jj) * bk, bk), c, None), carry)
            lt_ref[hh] = cf
            return jnp.where(mine, acc, y)

        y_ref[...] = lax.fori_loop(0, 2, head, jnp.zeros((tq, SB_PAIR), F32)).astype(y_ref.dtype)
        if nc:
            finish()

    return pl.pallas_call(
        body, name="sb_fwd", grid=(npair, s // tq),
        in_specs=[pl.BlockSpec((tq, SB_PAIR), lambda a, i: (i, a)),
                  pl.BlockSpec((s, SB_PAIR), lambda a, i: (0, npair + a)),
                  pl.BlockSpec((s, SB_PAIR), lambda a, i: (0, 2 * npair + a))] + [HBM] * nc,
        out_specs=[pl.BlockSpec((tq, SB_PAIR), lambda a, i: (i, a)),
                   pl.BlockSpec((2, tq, 1), lambda a, i: (a, i, 0))] + [HBM] * nc,
        out_shape=[jax.ShapeDtypeStruct((s, d), CDT), jax.ShapeDtypeStruct((2 * npair, s, 1), F32)]
        + _exchange_shapes(comm, False),
        scratch_shapes=_exchange_sems(nc) if nc else [],
        compiler_params=_cparams(("arbitrary", "arbitrary")),
    )(qkv, qkv, qkv, *comm)


def _sb_bwd_pairs(qkv, ltot, dy, comm=()):
    s, d3 = qkv.shape
    d = d3 // 3
    npair = d // SB_PAIR
    tq, bk = min(SB_TQ, s), min(SB_BK, s)
    scale = SB_DH ** -0.5
    nc = len(comm)
    hooks = _fused_exchange(True, s // tq, npair * (s // tq))

    def body(q_ref, k_ref, v_ref, lt_ref, dy_ref, *rest):
        dq_ref, dk_ref, dv_ref = rest[nc:nc + 3]
        if nc:
            finish = hooks(rest[:nc], rest[nc + 3:2 * nc + 3], rest[2 * nc + 3:])
        i = pl.program_id(1)

        @pl.when(i == 0)
        def _():
            dk_ref[...] = jnp.zeros_like(dk_ref)
            dv_ref[...] = jnp.zeros_like(dv_ref)

        q0 = i * tq
        q2 = q_ref[...] * scale
        do2 = dy_ref[...].astype(CDT)
        lane_head = (_iota((1, SB_PAIR), 1) >= SB_DH).astype(jnp.int32)
        tri_le = (_iota((bk, bk), 0) <= _iota((bk, bk), 1)).astype(CDT)
        tri_lt = (_iota((bk, bk), 0) < _iota((bk, bk), 1)).astype(CDT)

        def head(hh, dq_all):
            mine = lane_head == hh
            qs = jnp.where(mine, q2, jnp.zeros_like(q2))
            dov = jnp.where(mine, do2, jnp.zeros_like(do2))
            ltot_h = lt_ref[hh]

            def step(k0, carry, diag, r0=0):
                cf, cg, dq = carry
                kb = k_ref[pl.ds(k0, bk), :]
                vb = v_ref[pl.ds(k0, bk), :]
                causal, lb, lk = _sb_block(qs[r0:], kb, diag)
                w = jnp.exp(lb + ltot_h[r0:] - (cf + _split_dot(lk, tri_le, False, 2)))
                if causal is not None:
                    w = jnp.where(causal, w, 0.0)
                g = w * _dot(dov[r0:], vb, NT)
                gsum = cg + _split_dot(g, tri_lt, False, 2)
                dz = g - (g + gsum) * jnp.exp(lb)
                if causal is not None:
                    dz = jnp.where(causal, dz, 0.0)
                dzc = dz.astype(CDT)
                dk_ref[pl.ds(k0, bk), :] += _dot(dzc, qs[r0:], TN)
                dv_ref[pl.ds(k0, bk), :] += _dot(w, dov[r0:], TN)
                kbm = jnp.where(mine, kb, jnp.zeros_like(kb))
                return (cf + jnp.sum(lk, axis=1, keepdims=True), cg + jnp.sum(g, axis=1, keepdims=True),
                        dq + _dot(dzc, kbm))

            carry = (jnp.zeros((tq, 1), F32), jnp.zeros((tq, 1), F32), jnp.zeros((tq, SB_PAIR), F32))
            carry = lax.fori_loop(0, q0 // bk, lambda jj, c: step(pl.multiple_of(jj * bk, bk), c, None), carry)
            for dd in range(tq // bk):
                r0 = dd * bk
                sub = step(pl.multiple_of(q0 + r0, bk), tuple(t[r0:] for t in carry), 0, r0)
                carry = tuple(jnp.concatenate([t[:r0], u], axis=0) if r0 else u for t, u in zip(carry, sub))
            return dq_all + carry[2]

        dq_ref[...] = lax.fori_loop(0, 2, head, jnp.zeros((tq, SB_PAIR), F32)) * scale
        if nc:
            finish()

    tile = pl.BlockSpec((tq, SB_PAIR), lambda a, i: (i, a))
    acc = pl.BlockSpec((s, SB_PAIR), lambda a, i: (0, a))
    shp = jax.ShapeDtypeStruct((s, d), F32)
    return pl.pallas_call(
        body, name="sb_bwd", grid=(npair, s // tq),
        in_specs=[tile, pl.BlockSpec((s, SB_PAIR), lambda a, i: (0, npair + a)),
                  pl.BlockSpec((s, SB_PAIR), lambda a, i: (0, 2 * npair + a)),
                  pl.BlockSpec((2, tq, 1), lambda a, i: (a, i, 0)), tile] + [HBM] * nc,
        out_specs=[tile, acc, acc] + [HBM] * nc,
        out_shape=[shp, shp, shp] + _exchange_shapes(comm, True),
        scratch_shapes=_exchange_sems(nc) if nc else [],
        compiler_params=_cparams(("arbitrary", "arbitrary")),
    )(qkv, qkv, qkv, ltot, dy, *comm)


def _pick_lane(tile, r):
    return jnp.sum(jnp.where(_iota(tile.shape, 1) == r, tile, 0.0), axis=1, keepdims=True)


def _pick_row(tile, r):
    return jnp.sum(jnp.where(_iota(tile.shape, 0) == r, tile, 0.0), axis=0, keepdims=True)


def _ssd_chunk_setup(c_ref, b_ref, dac_ref, dar_ref, cb_ref, acsc_ref, acsr_ref):
    l = SSD_L
    tdt = F32 if CDT == F32 else BF16
    lower = (_iota((l, l), 1) <= _iota((l, l), 0)).astype(tdt)
    upper = (_iota((l, l), 0) <= _iota((l, l), 1)).astype(tdt)
    cb_ref[...] = _dot(c_ref[...], b_ref[...], NT)
    acsc_ref[...] = _split_dot(dac_ref[...], lower, True, 3)
    acsr_ref[...] = _split_dot(dar_ref[...], upper, False, 3)


def _ssd_fwd(xh, dtc, dac, dar, dsk, xbc):
    hh, s, p = xh.shape
    l, n, g_, r_ = SSD_L, SSD_N, SSD_G, SSD_R
    nc = s // l
    boff = SSD_INNER // n
    coff = boff + g_

    def body(x_ref, dtc_ref, dac_ref, dar_ref, dsk_ref, b_ref, c_ref, y_ref, st_ref,
             state_ref, cb_ref, acsc_ref, acsr_ref):
        c = pl.program_id(1)
        r = pl.program_id(2)

        @pl.when(r == 0)
        def _():
            _ssd_chunk_setup(c_ref, b_ref, dac_ref, dar_ref, cb_ref, acsc_ref, acsr_ref)

        @pl.when(c == 0)
        def _():
            state_ref[r] = jnp.zeros((n, p), F32)

        a_col = _pick_lane(acsc_ref[...], r)
        a_row = _pick_row(acsr_ref[...], r)
        dt_col = _pick_lane(dtc_ref[...], r)
        dsk_h = _pick_lane(dsk_ref[...], r)
        xv = x_ref[...]
        xd = xv * dt_col
        mask = _iota((l, l), 1) <= _iota((l, l), 0)
        decay = jnp.where(mask, jnp.exp(jnp.minimum(a_col - a_row, 0.0)), 0.0)
        w = cb_ref[...] * decay
        hprev = state_ref[r]
        cv = c_ref[...]
        y = _dot(w, xd) + jnp.exp(a_col) * _dot(cv, hprev)
        y_ref[...] = y + dsk_h * xv
        a_end = a_col[l - 1:l, :]
        dte = jnp.exp(a_end - a_col)
        st_ref[...] = hprev
        state_ref[r] = hprev * jnp.exp(a_end) + _dot(b_ref[...], xd * dte, TN)

    return pl.pallas_call(
        body, name="ssd_fwd", grid=(g_, nc, r_),
        in_specs=[pl.BlockSpec((None, l, p), lambda g, c, r: (g * r_ + r, c, 0)),
                  pl.BlockSpec((None, l, r_), lambda g, c, r: (g, c, 0)),
                  pl.BlockSpec((None, l, r_), lambda g, c, r: (g, c, 0)),
                  pl.BlockSpec((None, r_, l), lambda g, c, r: (g, 0, c)),
                  pl.BlockSpec((None, 1, r_), lambda g, c, r: (g, 0, 0)),
                  pl.BlockSpec((l, n), lambda g, c, r: (c, boff + g)),
                  pl.BlockSpec((l, n), lambda g, c, r: (c, coff + g))],
        out_specs=[pl.BlockSpec((None, l, p), lambda g, c, r: (g * r_ + r, c, 0)),
                   pl.BlockSpec((None, None, n, p), lambda g, c, r: (g * r_ + r, c, 0, 0))],
        out_shape=[jax.ShapeDtypeStruct((hh, s, p), F32),
                   jax.ShapeDtypeStruct((hh, nc, n, p), F32)],
        scratch_shapes=[pltpu.VMEM((r_, n, p), F32), pltpu.VMEM((l, l), F32),
                        pltpu.VMEM((l, r_), F32), pltpu.VMEM((r_, l), F32)],
        compiler_params=_cparams(("parallel", "arbitrary", "arbitrary")),
    )(xh, dtc, dac, dar, dsk, xbc, xbc)


def _ssd_bwd(xh, dtc, dac, dar, dsk, xbc, st, dy):
    hh, s, p = xh.shape
    l, n, g_, r_ = SSD_L, SSD_N, SSD_G, SSD_R
    nc = s // l
    boff = SSD_INNER // n
    coff = boff + g_

    def body(x_ref, dtc_ref, dac_ref, dar_ref, dsk_ref, b_ref, c_ref, st_ref, dy_ref,
             dx_ref, dda_ref, dxx_ref, db_ref, dc_ref, dah_ref, ddsk_ref,
             dstate_ref, cb_ref, acsc_ref, acsr_ref):
        c = pl.program_id(1)
        r = pl.program_id(2)

        @pl.when(r == 0)
        def _():
            _ssd_chunk_setup(c_ref, b_ref, dac_ref, dar_ref, cb_ref, acsc_ref, acsr_ref)
            db_ref[...] = jnp.zeros_like(db_ref)
            dc_ref[...] = jnp.zeros_like(dc_ref)
            dda_ref[...] = jnp.zeros_like(dda_ref)
            dxx_ref[...] = jnp.zeros_like(dxx_ref)

        @pl.when(jnp.logical_and(c == 0, r == 0))
        def _():
            dah_ref[...] = jnp.zeros_like(dah_ref)
            ddsk_ref[...] = jnp.zeros_like(ddsk_ref)

        @pl.when(c == 0)
        def _():
            dstate_ref[r] = jnp.zeros((n, p), F32)

        a_col = _pick_lane(acsc_ref[...], r)
        a_row = _pick_row(acsr_ref[...], r)
        dt_col = _pick_lane(dtc_ref[...], r)
        dsk_h = _pick_lane(dsk_ref[...], r)
        xv = x_ref[...]
        dyv = dy_ref[...]
        xd = xv * dt_col
        il = _iota((l, l), 0)
        isx = _iota((l, l), 1)
        decay = jnp.where(isx <= il, jnp.exp(jnp.minimum(a_col - a_row, 0.0)), 0.0)
        cb = cb_ref[...]
        w = cb * decay
        dhn = dstate_ref[r]
        hc = st_ref[...]
        bv = b_ref[...]
        cv = c_ref[...]
        a_end = a_col[l - 1:l, :]
        ea = jnp.exp(a_col)
        dte = jnp.exp(a_end - a_col)

        dx_state = dte * _dot(bv, dhn)
        dxd = _dot(w, dyv, TN) + dx_state
        md = decay * _dot(dyv, xd, NT)
        m = md * cb
        dc_ref[...] += _dot(md, bv) + ea * _dot(dyv, hc, NT)
        db_ref[...] += _dot(md, cv, TN) + dte * _dot(xd, dhn, NT)
        dstate_ref[r] = jnp.exp(a_end) * dhn + _dot(cv, dyv * ea, TN)

        tdt = F32 if CDT == F32 else BF16
        t1 = (isx >= il).astype(tdt)
        yoff = ea * _dot(cv, hc)
        xdx = jnp.sum(xd * dx_state, axis=1, keepdims=True)
        vec = jnp.sum(dyv * yoff, axis=1, keepdims=True) - xdx
        end_term = jnp.sum(xdx, axis=0, keepdims=True) + \
            jnp.exp(a_end) * jnp.sum(jnp.sum(hc * dhn, axis=1, keepdims=True), axis=0, keepdims=True)
        zmat = _split_dot(m, t1, True, 2)
        span = jnp.sum(jnp.where(isx < il, zmat, 0.0), axis=1, keepdims=True)
        rc = _split_dot(jnp.broadcast_to(vec, (l, 128)), t1, True, 2)[:, :1]
        dda = span + rc + end_term
        dxx = jnp.sum(dxd * xv, axis=1, keepdims=True)

        lane = _iota((l, r_), 1) == r
        dda_ref[...] += jnp.where(lane, dda, 0.0)
        dxx_ref[...] += jnp.where(lane, dxx, 0.0)
        dx_ref[...] = dxd * dt_col + dsk_h * dyv
        lane1 = _iota((1, r_), 1) == r
        dah_ref[...] += jnp.where(lane1, jnp.sum(dda * dt_col, axis=0, keepdims=True), 0.0)
        ddsk_ref[...] += jnp.where(
            lane1, jnp.sum(jnp.sum(dyv * xv, axis=1, keepdims=True), axis=0, keepdims=True), 0.0)

    rev = lambda c: nc - 1 - c
    xspec = pl.BlockSpec((None, l, p), lambda g, c, r: (g * r_ + r, rev(c), 0))
    cspec = pl.BlockSpec((None, l, r_), lambda g, c, r: (g, rev(c), 0))
    hspec = pl.BlockSpec((None, 1, r_), lambda g, c, r: (g, 0, 0))
    return pl.pallas_call(
        body, name="ssd_bwd", grid=(g_, nc, r_),
        in_specs=[xspec, cspec, cspec,
                  pl.BlockSpec((None, r_, l), lambda g, c, r: (g, 0, rev(c))),
                  hspec,
                  pl.BlockSpec((l, n), lambda g, c, r: (rev(c), boff + g)),
                  pl.BlockSpec((l, n), lambda g, c, r: (rev(c), coff + g)),
                  pl.BlockSpec((None, None, n, p), lambda g, c, r: (g * r_ + r, rev(c), 0, 0)),
                  xspec],
        out_specs=[xspec, cspec, cspec,
                   pl.BlockSpec((l, n), lambda g, c, r: (rev(c), g)),
                   pl.BlockSpec((l, n), lambda g, c, r: (rev(c), g)),
                   hspec, hspec],
        out_shape=[jax.ShapeDtypeStruct((hh, s, p), F32),
                   jax.ShapeDtypeStruct((g_, s, r_), F32),
                   jax.ShapeDtypeStruct((g_, s, r_), F32),
                   jax.ShapeDtypeStruct((s, g_ * n), F32),
                   jax.ShapeDtypeStruct((s, g_ * n), F32),
                   jax.ShapeDtypeStruct((g_, 1, r_), F32),
                   jax.ShapeDtypeStruct((g_, 1, r_), F32)],
        scratch_shapes=[pltpu.VMEM((r_, n, p), F32), pltpu.VMEM((l, l), F32),
                        pltpu.VMEM((l, r_), F32), pltpu.VMEM((r_, l), F32)],
        compiler_params=_cparams(("parallel", "arbitrary", "arbitrary")),
    )(xh, dtc, dac, dar, dsk, xbc, xbc, st, dy)


def _ssd_chunk_common(c_ref, b_ref, dac_ref, dar_ref):
    l = SSD_L
    tdt = F32 if CDT == F32 else BF16
    lower = (_iota((l, l), 1) <= _iota((l, l), 0)).astype(tdt)
    upper = (_iota((l, l), 0) <= _iota((l, l), 1)).astype(tdt)
    cb = _dot(c_ref[...], b_ref[...], NT)
    return cb, _split_dot(dac_ref[...], lower, True, 3), _split_dot(dar_ref[...], upper, False, 3)


def _ssd_fwd_g(xh, dtc, dac, dar, dsk, xbc):
    hh, s, p = xh.shape
    l, n, g_, r_ = SSD_L, SSD_N, SSD_G, SSD_R
    nc = s // l
    boff = SSD_INNER // n
    coff = boff + g_

    def body(x_ref, dtc_ref, dac_ref, dar_ref, dsk_ref, b_ref, c_ref, y_ref, st_ref, state_ref):
        c = pl.program_id(1)

        @pl.when(c == 0)
        def _():
            state_ref[...] = jnp.zeros_like(state_ref)

        cb, acs_c, acs_r = _ssd_chunk_common(c_ref, b_ref, dac_ref, dar_ref)
        mask = _iota((l, l), 1) <= _iota((l, l), 0)
        cv, bv = c_ref[...], b_ref[...]
        dtcv, dskv = dtc_ref[...], dsk_ref[...]
        for r in range(r_):
            a_col = _pick_lane(acs_c, r)
            a_row = _pick_row(acs_r, r)
            dt_col = _pick_lane(dtcv, r)
            dsk_h = _pick_lane(dskv, r)
            xv = x_ref[r]
            xd = xv * dt_col
            decay = jnp.where(mask, jnp.exp(jnp.minimum(a_col - a_row, 0.0)), 0.0)
            hprev = state_ref[r]
            y = _dot(cb * decay, xd) + jnp.exp(a_col) * _dot(cv, hprev)
            y_ref[r] = y + dsk_h * xv
            a_end = a_col[l - 1:l, :]
            st_ref[r] = hprev
            state_ref[r] = hprev * jnp.exp(a_end) + _dot(bv, xd * jnp.exp(a_end - a_col), TN)

    return pl.pallas_call(
        body, name="ssd_fwd", grid=(g_, nc),
        in_specs=[pl.BlockSpec((r_, l, p), lambda g, c: (g, c, 0)),
                  pl.BlockSpec((None, l, r_), lambda g, c: (g, c, 0)),
                  pl.BlockSpec((None, l, r_), lambda g, c: (g, c, 0)),
                  pl.BlockSpec((None, r_, l), lambda g, c: (g, 0, c)),
                  pl.BlockSpec((None, 1, r_), lambda g, c: (g, 0, 0)),
                  pl.BlockSpec((l, n), lambda g, c: (c, boff + g)),
                  pl.BlockSpec((l, n), lambda g, c: (c, coff + g))],
        out_specs=[pl.BlockSpec((r_, l, p), lambda g, c: (g, c, 0)),
                   pl.BlockSpec((r_, None, n, p), lambda g, c: (g, c, 0, 0))],
        out_shape=[jax.ShapeDtypeStruct((hh, s, p), F32),
                   jax.ShapeDtypeStruct((hh, nc, n, p), F32)],
        scratch_shapes=[pltpu.VMEM((r_, n, p), F32)],
        compiler_params=_cparams(("parallel", "arbitrary")),
    )(xh, dtc, dac, dar, dsk, xbc, xbc)


def _ssd_bwd_g(xh, dtc, dac, dar, dsk, xbc, st, dy):
    hh, s, p = xh.shape
    l, n, g_, r_ = SSD_L, SSD_N, SSD_G, SSD_R
    nc = s // l
    boff = SSD_INNER // n
    coff = boff + g_

    def body(x_ref, dtc_ref, dac_ref, dar_ref, dsk_ref, b_ref, c_ref, st_ref, dy_ref,
             dx_ref, dda_ref, dxx_ref, db_ref, dc_ref, dah_ref, ddsk_ref, dstate_ref):
        c = pl.program_id(1)

        @pl.when(c == 0)
        def _():
            dstate_ref[...] = jnp.zeros_like(dstate_ref)
            dah_ref[...] = jnp.zeros_like(dah_ref)
            ddsk_ref[...] = jnp.zeros_like(ddsk_ref)

        cb, acs_c, acs_r = _ssd_chunk_common(c_ref, b_ref, dac_ref, dar_ref)
        il = _iota((l, l), 0)
        isx = _iota((l, l), 1)
        tdt = F32 if CDT == F32 else BF16
        t1 = (isx >= il).astype(tdt)
        cv, bv = c_ref[...], b_ref[...]
        dtcv, dskv = dtc_ref[...], dsk_ref[...]
        lane = _iota((l, r_), 1)
        lane1 = _iota((1, r_), 1)
        dda_all = jnp.zeros((l, r_), F32)
        dxx_all = jnp.zeros((l, r_), F32)
        dah_all = jnp.zeros((1, r_), F32)
        ddsk_all = jnp.zeros((1, r_), F32)
        db_acc = jnp.zeros((l, n), F32)
        dc_acc = jnp.zeros((l, n), F32)
        md_sum = jnp.zeros((l, l), F32)
        cb_t = _dot(bv, cv, NT)
        cv_t = cv.T
        for r in range(r_):
            a_col = _pick_lane(acs_c, r)
            a_row = _pick_row(acs_r, r)
            dt_col = _pick_lane(dtcv, r)
            dsk_h = _pick_lane(dskv, r)
            xv = x_ref[r]
            dyv = dy_ref[r]
            xd = xv * dt_col
            decay = jnp.where(isx <= il, jnp.exp(jnp.minimum(a_col - a_row, 0.0)), 0.0)
            decay_t = jnp.where(isx >= il, jnp.exp(jnp.minimum(a_row - a_col, 0.0)), 0.0)
            dhn = dstate_ref[r]
            hc = st_ref[r]
            a_end = a_col[l - 1:l, :]
            ea = jnp.exp(a_col)
            dte = jnp.exp(a_end - a_col)

            dx_state = dte * _dot(bv, dhn)
            dxd = _dot(cb_t * decay_t, dyv) + dx_state
            md = decay * _dot(dyv, xd, NT)
            md_sum = md_sum + md
            dc_acc = dc_acc + ea * _dot(dyv, hc, NT)
            db_acc = db_acc + dte * _dot(xd, dhn, NT)
            dstate_ref[r] = jnp.exp(a_end) * dhn + _dot(cv_t, dyv * ea)

            yoff = ea * _dot(cv, hc)
            xdx = jnp.sum(xd * dx_state, axis=1, keepdims=True)
            vec = jnp.sum(dyv * yoff, axis=1, keepdims=True) - xdx
            end_term = jnp.sum(xdx, axis=0, keepdims=True) + \
                jnp.exp(a_end) * jnp.sum(jnp.sum(hc * dhn, axis=1, keepdims=True), axis=0, keepdims=True)
            zmat = _split_dot(md * cb, t1, True, 2)
            span = jnp.sum(jnp.where(isx < il, zmat, 0.0), axis=1, keepdims=True)
            rc = _split_dot(jnp.broadcast_to(vec, (l, 128)), t1, True, 2)[:, :1]
            dda = span + rc + end_term
            dda_all = jnp.where(lane == r, dda, dda_all)
            dxx_all = jnp.where(lane == r, jnp.sum(dxd * xv, axis=1, keepdims=True), dxx_all)
            dx_ref[r] = dxd * dt_col + dsk_h * dyv
            dah_all = jnp.where(lane1 == r, jnp.sum(dda * dt_col, axis=0, keepdims=True), dah_all)
            ddsk_all = jnp.where(
                lane1 == r, jnp.sum(jnp.sum(dyv * xv, axis=1, keepdims=True), axis=0, keepdims=True), ddsk_all)
        dda_ref[...] = dda_all
        dxx_ref[...] = dxx_all
        db_ref[...] = db_acc + _dot(md_sum, cv, TN)
        dc_ref[...] = dc_acc + _dot(md_sum, bv)
        dah_ref[...] += dah_all
        ddsk_ref[...] += ddsk_all

    rev = lambda c: nc - 1 - c
    xspec = pl.BlockSpec((r_, l, p), lambda g, c: (g, rev(c), 0))
    cspec = pl.BlockSpec((None, l, r_), lambda g, c: (g, rev(c), 0))
    hspec = pl.BlockSpec((None, 1, r_), lambda g, c: (g, 0, 0))
    return pl.pallas_call(
        body, name="ssd_bwd", grid=(g_, nc),
        in_specs=[xspec, cspec, cspec,
                  pl.BlockSpec((None, r_, l), lambda g, c: (g, 0, rev(c))),
                  hspec,
                  pl.BlockSpec((l, n), lambda g, c: (rev(c), boff + g)),
                  pl.BlockSpec((l, n), lambda g, c: (rev(c), coff + g)),
                  pl.BlockSpec((r_, None, n, p), lambda g, c: (g, rev(c), 0, 0)),
                  xspec],
        out_specs=[xspec, cspec, cspec,
                   pl.BlockSpec((l, n), lambda g, c: (rev(c), g)),
                   pl.BlockSpec((l, n), lambda g, c: (rev(c), g)),
                   hspec, hspec],
        out_shape=[jax.ShapeDtypeStruct((hh, s, p), F32),
                   jax.ShapeDtypeStruct((g_, s, r_), F32),
                   jax.ShapeDtypeStruct((g_, s, r_), F32),
                   jax.ShapeDtypeStruct((s, g_ * n), F32),
                   jax.ShapeDtypeStruct((s, g_ * n), F32),
                   jax.ShapeDtypeStruct((g_, 1, r_), F32),
                   jax.ShapeDtypeStruct((g_, 1, r_), F32)],
        scratch_shapes=[pltpu.VMEM((r_, n, p), F32)],
        compiler_params=_cparams(("parallel", "arbitrary")),
    )(xh, dtc, dac, dar, dsk, xbc, xbc, st, dy)


CONV_TC = 256
CONV_RC = 512


def _conv_taps(x_ref, head_ref, rc):
    base = CONV_PAD - (CONV_K - 1)
    head_ref[pl.ds(0, CONV_PAD), :] = jnp.zeros((CONV_PAD, head_ref.shape[1]), F32)
    head_ref[pl.ds(CONV_PAD, rc), :] = x_ref[pl.ds(0, rc), :]

    def tap(t0, kk):
        if t0 == 0:
            return head_ref[pl.ds(base + kk, rc), :]
        return x_ref[pl.ds(t0 - (CONV_K - 1) + kk, rc), :]
    return tap


def _conv_fwd(x, w, b):
    s, ch = x.shape
    rc = min(CONV_RC, s)

    def body(x_ref, w_ref, b_ref, pre_ref, act_ref, head_ref):
        wv = w_ref[...]
        tap = _conv_taps(x_ref, head_ref, rc)
        for t0 in range(0, s, rc):
            acc = jnp.broadcast_to(b_ref[...], (rc, CONV_TC))
            for kk in range(CONV_K):
                acc = acc + wv[kk:kk + 1, :] * tap(t0, kk)
            pre_ref[pl.ds(t0, rc), :] = acc
            act_ref[pl.ds(t0, rc), :] = _silu(acc)

    col = pl.BlockSpec((s, CONV_TC), lambda j: (0, j))
    shp = jax.ShapeDtypeStruct((s, ch), F32)
    return pl.pallas_call(
        body, name="conv_fwd", grid=(ch // CONV_TC,),
        in_specs=[col, pl.BlockSpec((CONV_K, CONV_TC), lambda j: (0, j)),
                  pl.BlockSpec((1, CONV_TC), lambda j: (0, j))],
        out_specs=[col, col],
        out_shape=[shp, shp],
        scratch_shapes=[pltpu.VMEM((CONV_PAD + rc, CONV_TC), F32)],
        compiler_params=_cparams(("parallel",)),
    )(x, w, b)


def _conv_bwd(x, pre, dact, w):
    s, ch = x.shape
    rc = min(CONV_RC, s)

    def body(x_ref, pre_ref, da_ref, w_ref, dx_ref, dw_ref, db_ref, dpre_ref, head_ref):
        wv = w_ref[...]
        tap = _conv_taps(x_ref, head_ref, rc)
        for t0 in range(0, s, rc):
            dpre_ref[pl.ds(t0, rc), :] = da_ref[pl.ds(t0, rc), :] * _dsilu(pre_ref[pl.ds(t0, rc), :])
        dpre_ref[pl.ds(s, CONV_PAD), :] = jnp.zeros((CONV_PAD, CONV_TC), F32)
        dws = [jnp.zeros((1, CONV_TC), F32) for _ in range(CONV_K)]
        dbs = jnp.zeros((1, CONV_TC), F32)
        for t0 in range(0, s, rc):
            acc = jnp.zeros((rc, CONV_TC), F32)
            dp = dpre_ref[pl.ds(t0, rc), :]
            for kk in range(CONV_K):
                acc = acc + wv[kk:kk + 1, :] * dpre_ref[pl.ds(t0 + CONV_K - 1 - kk, rc), :]
                dws[kk] = dws[kk] + jnp.sum(dp * tap(t0, kk), axis=0, keepdims=True)
            dbs = dbs + jnp.sum(dp, axis=0, keepdims=True)
            dx_ref[pl.ds(t0, rc), :] = acc.astype(dx_ref.dtype)
        for kk in range(CONV_K):
            dw_ref[kk:kk + 1, :] = dws[kk]
        db_ref[...] = dbs

    col = pl.BlockSpec((s, CONV_TC), lambda j: (0, j))
    return pl.pallas_call(
        body, name="conv_bwd", grid=(ch // CONV_TC,),
        in_specs=[col, col, col, pl.BlockSpec((CONV_K, CONV_TC), lambda j: (0, j))],
        out_specs=[col, pl.BlockSpec((CONV_K, CONV_TC), lambda j: (0, j)),
                   pl.BlockSpec((1, CONV_TC), lambda j: (0, j))],
        out_shape=[jax.ShapeDtypeStruct((s, ch), CDT),
                   jax.ShapeDtypeStruct((CONV_K, ch), F32),
                   jax.ShapeDtypeStruct((1, ch), F32)],
        scratch_shapes=[pltpu.VMEM((s + CONV_PAD, CONV_TC), F32), pltpu.VMEM((CONV_PAD + rc, CONV_TC), F32)],
        compiler_params=_cparams(("parallel",)),
    )(x, pre, dact, w)


MEM_TS = 512


def _mem_fwd(mq, kv):
    s = mq.shape[0]
    m = kv.shape[0]
    ts = min(MEM_TS, s)
    scale = MEM_DH ** -0.5

    def body(q_ref, k_ref, v_ref, o_ref):
        sc = _dot(q_ref[...], k_ref[...], NT) * scale
        e = jnp.exp(sc - jnp.max(sc, axis=1, keepdims=True))
        pr = e / jnp.sum(e, axis=1, keepdims=True)
        o_ref[...] = _dot(pr, v_ref[...]).astype(o_ref.dtype)

    return pl.pallas_call(
        body, name="mem_fwd", grid=(MEM_H, s // ts),
        in_specs=[pl.BlockSpec((ts, MEM_DH), lambda a, i: (i, a)),
                  pl.BlockSpec((m, MEM_DH), lambda a, i: (0, a)),
                  pl.BlockSpec((m, MEM_DH), lambda a, i: (0, MEM_H + a))],
        out_specs=pl.BlockSpec((ts, MEM_DH), lambda a, i: (i, a)),
        out_shape=jax.ShapeDtypeStruct((s, MEM_H * MEM_DH), CDT),
        compiler_params=_cparams(("parallel", "arbitrary")),
    )(mq, kv, kv)


def _mem_bwd(mq, kv, do):
    s = mq.shape[0]
    m = kv.shape[0]
    ts = min(MEM_TS, s)
    scale = MEM_DH ** -0.5

    def body(q_ref, k_ref, v_ref, do_ref, dq_ref, dk_ref, dv_ref):
        i = pl.program_id(1)

        @pl.when(i == 0)
        def _():
            dk_ref[...] = jnp.zeros_like(dk_ref)
            dv_ref[...] = jnp.zeros_like(dv_ref)

        qv, kb, vb, dov = q_ref[...], k_ref[...], v_ref[...], do_ref[...]
        sc = _dot(qv, kb, NT) * scale
        e = jnp.exp(sc - jnp.max(sc, axis=1, keepdims=True))
        pr = e / jnp.sum(e, axis=1, keepdims=True)
        dp = _dot(dov, vb, NT)
        ds = pr * (dp - jnp.sum(dp * pr, axis=1, keepdims=True)) * scale
        dq_ref[...] = _dot(ds, kb).astype(dq_ref.dtype)
        dk_ref[...] += _dot(ds, qv, TN)
        dv_ref[...] += _dot(pr, dov, TN)

    tile = pl.BlockSpec((ts, MEM_DH), lambda a, i: (i, a))
    kvo = pl.BlockSpec((m, MEM_DH), lambda a, i: (0, a))
    return pl.pallas_call(
        body, name="mem_bwd", grid=(MEM_H, s // ts),
        in_specs=[tile, kvo, pl.BlockSpec((m, MEM_DH), lambda a, i: (0, MEM_H + a)), tile],
        out_specs=[tile, kvo, kvo],
        out_shape=[jax.ShapeDtypeStruct((s, MEM_H * MEM_DH), CDT),
                   jax.ShapeDtypeStruct((m, MEM_H * MEM_DH), F32),
                   jax.ShapeDtypeStruct((m, MEM_H * MEM_DH), F32)],
        compiler_params=_cparams(("parallel", "arbitrary")),
    )(mq, kv, kv, do)


def _heads(t, nh, dh):
    return t.reshape(t.shape[0], nh, dh).transpose(1, 0, 2)


def _unheads(t):
    return t.transpose(1, 0, 2).reshape(t.shape[1], t.shape[0] * t.shape[2])


def _group_cols(t):
    return t.reshape(t.shape[0], SSD_G, SSD_R).transpose(1, 0, 2)


def _pad_cols(t, width):
    return jnp.pad(t, ((0, 0), (0, width - t.shape[1])))


def _full_weight(name, gathered):
    if name in COL_SHARDED:
        return gathered.transpose(1, 0, 2).reshape(gathered.shape[1], N_DEV * gathered.shape[2])
    return gathered.reshape(N_DEV * gathered.shape[1], gathered.shape[2])


def _grad_payload(name, g):
    if name in COL_SHARDED:
        return g.reshape(g.shape[0], N_DEV, g.shape[1] // N_DEV).transpose(1, 0, 2)
    return g.reshape(N_DEV, g.shape[0] // N_DEV, g.shape[1])


def _local_step(x, mem, tgt, p, wt, shards=None):
    s, d = x.shape
    wt = dict(wt)
    c1, c2, c3, c4, c5 = 3 * d, 3 * d + SSD_INNER, 3 * d + SSD_INNER + CONV_DIM, \
        3 * d + SSD_INNER + CONV_DIM + SSD_H, 3 * d + SSD_INNER + CONV_DIM + SSD_H + d
    w_t = wt["w_in"]
    w_tail = w_t[c4:]
    w_dt = jnp.pad(w_t[c3:c4], ((0, DT_PAD - SSD_H), (0, 0)))
    seg_name = ["qkv", "z", "xbc", "mq", "gl"]
    seg_dtype = [CDT, F32, F32, CDT, F32]
    seg_src = [w_t, w_t, w_t, w_tail, w_tail]
    seg_off = [0, c1, c2, 0, d]
    seg_n = [c1, c2 - c1, c3 - c2, d, 3 * d]

    u = _rows(lambda xv, g: _rms(xv, g), [x], [p["norm_mix_pre"]], [(d, CDT)], ts=512, name="f_norm_pre")[0]
    qkv, z, xbc_raw, mq, gl = [
        _mm(u, seg_src[i], tb=True, b_off=seg_off[i], n=seg_n[i], out_dtype=seg_dtype[i],
            name="f_in_" + seg_name[i]) for i in range(5)]
    dt_raw = _mm(u, w_dt, tb=True, name="f_in_dt")

    bias128 = _pad_cols(p["dt_bias"], DT_PAD)
    alog128 = _pad_cols(p["a_log"], DT_PAD)

    def dt_fn(dtr, bias, alog):
        dt = _softplus(dtr + bias)
        return dt, dt * (-jnp.exp(alog))

    dt128, da128 = _rows(dt_fn, [dt_raw], [bias128, alog128], [(DT_PAD, F32), (DT_PAD, F32)],
                         ts=512, name="f_dt")
    dtc = _group_cols(dt128[:, :SSD_H])
    dac = _group_cols(da128[:, :SSD_H])
    dar = dac.transpose(0, 2, 1)
    dsk = p["d_skip"].reshape(SSD_G, 1, SSD_R)

    conv_w, conv_b = p["conv_w"], p["conv_b"]
    pre, xbc = _conv_fwd(xbc_raw, conv_w, conv_b)
    xh = _heads(xbc[:, :SSD_INNER], SSD_H, SSD_P)
    y_h, st = _ssd_fwd_g(xh, dtc, dac, dar, dsk, xbc)
    y_core = _unheads(y_h)

    def group_norm_fwd(yv, zv, wn):
        y2 = yv * _silu(zv)
        gw = SSD_INNER // SSD_G
        outs = []
        for gi in range(SSD_G):
            seg = y2[:, gi * gw:(gi + 1) * gw]
            outs.append(_rms(seg, wn[:, gi * gw:(gi + 1) * gw]))
        return jnp.concatenate(outs, axis=1)

    y_ssd = _rows(group_norm_fwd, [y_core, z], [p["ssd_norm"]], [(SSD_INNER, CDT)], ts=256, name="f_ssd_post")[0]

    y_sb, lt_h, *late = _sb_fwd_pairs(qkv, tuple(shards) if shards is not None else ())
    for n, gth in zip(LATE_W, late):
        wt[n] = _full_weight(n, gth)

    mu = _rows(lambda mv, g: _rms(mv, g), [mem], [p["norm_mem"]], [(d, CDT)], ts=256, name="f_norm_mem")[0]
    kv = _mm(mu, wt["w_mem_kv"], out_dtype=CDT, name="f_mem_kv")
    y_mem = _mem_fwd(mq, kv)

    p_sb = _mm(y_sb, wt["w_sb_out"], name="f_sb_out")
    p_ssd = _mm(y_ssd, wt["w_ssd_out"], name="f_ssd_out")
    p_mem = _mm(y_mem, wt["w_mem_out"], name="f_mem_out")

    def merge_fn(glv, a, b, c):
        return (_sigmoid(glv[:, :d]) * a + _sigmoid(glv[:, d:2 * d]) * b + _sigmoid(glv[:, 2 * d:]) * c)

    merged = _rows(merge_fn, [gl, p_sb, p_ssd, p_mem], [], [(d, CDT)], ts=256, name="f_merge")[0]
    mix = _mm(merged, wt["w_o"], name="f_w_o")

    def mid_fn(xv, mixv, g_post, g_pre):
        h1 = xv + _rms(mixv, g_post)
        return h1, _rms(h1, g_pre)

    h1, u2 = _rows(mid_fn, [x, mix], [p["norm_mix_post"], p["norm_mlp_pre"]], [(d, F32), (d, CDT)],
                   ts=512, name="f_mid")
    a1, act = _mm(u2, wt["w_up"], name="f_up",
                  epilogue=(lambda pv: (pv, jnp.square(jnp.maximum(pv, 0.0))), [], [F32, CDT]))
    ff = _mm(act, wt["w_down"], name="f_down")

    def loss_fn(h1v, ffv, tv, g):
        diff = h1v + _rms(ffv, g) - tv
        tot = jnp.sum(jnp.sum(diff * diff, axis=1, keepdims=True), axis=0, keepdims=True)
        return diff * (1.0 / d), tot

    dh2, loss_acc = _rows(loss_fn, [h1, ff, tgt], [p["norm_mlp_post"]], [(d, F32)], [(1, 128)],
                          ts=512, name="f_loss")
    loss = loss_acc[:, :1] * (0.5 / d)

    sg = {}

    def b_post(ffv, dyv, g):
        dx, dg = _rms_bwd(ffv, g, dyv)
        return dx, dg

    d_ff, sg["norm_mlp_post"] = _rows(b_post, [ff, dh2], [p["norm_mlp_post"]], [(d, CDT)], [(1, d)],
                                      ts=512, name="b_norm_mlp_post")
    da1 = _mm(d_ff, wt["w_down"], tb=True, name="b_down_x",
              epilogue=(lambda pv, a: (pv * 2.0 * jnp.maximum(a, 0.0),), [a1], [CDT]))[0]
    gw = {"w_down": _mm(act, d_ff, ta=True, name="b_down_w")}
    du2 = _mm(da1, wt["w_up"], tb=True, name="b_up_x")
    gw["w_up"] = _mm(u2, da1, ta=True, name="b_up_w")

    def b_mid(h1v, du2v, dh2v, mixv, g_pre, g_post):
        dxa, dga = _rms_bwd(h1v, g_pre, du2v)
        dh1 = dh2v + dxa
        dmix, dgb = _rms_bwd(mixv, g_post, dh1)
        return dh1, dmix, dga, dgb

    dh1, dmix, sg["norm_mlp_pre"], sg["norm_mix_post"] = _rows(
        b_mid, [h1, du2, dh2, mix], [p["norm_mlp_pre"], p["norm_mix_post"]],
        [(d, F32), (d, CDT)], [(1, d), (1, d)], ts=256, name="b_mid")
    dmerged = _mm(dmix, wt["w_o"], tb=True, name="b_w_o_x")
    gw["w_o"] = _mm(merged, dmix, ta=True, name="b_w_o_w")

    def b_merge(dm, glv, a, b, c):
        outs, dgl = [], []
        for i, br in enumerate((a, b, c)):
            gt = _sigmoid(glv[:, i * d:(i + 1) * d])
            outs.append(gt * dm)
            dgl.append(dm * br * gt * (1.0 - gt))
        return outs[0], outs[1], outs[2], jnp.concatenate(dgl, axis=1)

    dp_sb, dp_ssd, dp_mem, dgl = _rows(b_merge, [dmerged, gl, p_sb, p_ssd, p_mem], [],
                                       [(d, CDT), (d, CDT), (d, CDT), (3 * d, CDT)], ts=256, name="b_merge")
    dy_sb = _mm(dp_sb, wt["w_sb_out"], tb=True, name="b_sb_out_x")
    gw["w_sb_out"] = _mm(y_sb, dp_sb, ta=True, name="b_sb_out_w")
    dy_ssd = _mm(dp_ssd, wt["w_ssd_out"], tb=True, name="b_ssd_out_x")
    gw["w_ssd_out"] = _mm(y_ssd, dp_ssd, ta=True, name="b_ssd_out_w")
    dy_mem = _mm(dp_mem, wt["w_mem_out"], tb=True, out_dtype=CDT, name="b_mem_out_x")
    gw["w_mem_out"] = _mm(y_mem, dp_mem, ta=True, name="b_mem_out_w")

    dmq, dk_m, dv_m = _mem_bwd(mq, kv, dy_mem)
    dkv = jnp.concatenate([dk_m, dv_m], axis=1).astype(CDT)
    gw["w_mem_kv"] = _mm(mu, dkv, ta=True, name="b_mem_kv_w")
    dmu = _mm(dkv, wt["w_mem_kv"], tb=True, name="b_mem_kv_x")
    sg["norm_mem"] = _rows(lambda mv, dv, g: _rms_bwd(mv, g, dv)[1], [mem, dmu], [p["norm_mem"]], [], [(1, d)],
                           ts=256, name="b_norm_mem")[0]

    payloads = tuple(_grad_payload(n, gw[n]) for n in LATE_W) if shards is not None else ()
    dq, dk, dv, *received = _sb_bwd_pairs(qkv, lt_h, dy_sb, payloads)
    dqkv = jnp.concatenate([dq, dk, dv], axis=1).astype(CDT)

    def group_norm_bwd(dyo, yv, zv, wn):
        sz = _silu(zv)
        y2 = yv * sz
        gw_ = SSD_INNER // SSD_G
        dy2, dwn = [], []
        for gi in range(SSD_G):
            sl = slice(gi * gw_, (gi + 1) * gw_)
            dxs, dgs = _rms_bwd(y2[:, sl], wn[:, sl], dyo[:, sl])
            dy2.append(dxs)
            dwn.append(dgs)
        dy2 = jnp.concatenate(dy2, axis=1)
        return dy2 * sz, dy2 * yv * _dsilu(zv), jnp.concatenate(dwn, axis=1)

    dy_core, dz, sg["ssd_norm"] = _rows(group_norm_bwd, [dy_ssd, y_core, z], [p["ssd_norm"]],
                                        [(SSD_INNER, F32), (SSD_INNER, CDT)], [(1, SSD_INNER)],
                                        ts=256, name="b_ssd_post")
    dxh, dda, dxx, d_b, d_c, dah, ddsk = _ssd_bwd_g(xh, dtc, dac, dar, dsk, xbc, st, _heads(dy_core, SSD_H, SSD_P))
    sg["d_skip"] = ddsk.reshape(1, SSD_H)
    sg["a_log"] = dah.reshape(1, SSD_H) * (-jnp.exp(p["a_log"]))

    def b_dt(ddav, dxxv, dtr, bias, alog):
        ddt = ddav * (-jnp.exp(alog)) + dxxv
        draw = ddt * _sigmoid(dtr + bias)
        return draw, jnp.sum(draw, axis=0, keepdims=True)

    ungroup = lambda t: _pad_cols(t.transpose(1, 0, 2).reshape(s, SSD_H), DT_PAD)
    ddt_raw, dbias128 = _rows(b_dt, [ungroup(dda), ungroup(dxx), dt_raw], [bias128, alog128],
                              [(DT_PAD, CDT)], [(1, DT_PAD)], ts=512, name="b_dt")
    sg["dt_bias"] = dbias128[:, :SSD_H]

    dxbc = jnp.concatenate([_unheads(dxh), d_b, d_c], axis=1)
    dxbc_raw, sg["conv_w"], sg["conv_b"] = _conv_bwd(xbc_raw, pre, dxbc, conv_w)

    dseg = [dqkv, dz, dxbc_raw, dmq, dgl]
    dw_bufs = [lax.empty((c3, d), F32), lax.empty((c5 - c4 + 3 * d, d), F32)]
    for i in range(5):
        bi = 0 if i < 3 else 1
        dw_bufs[bi] = _mm(dseg[i], u, ta=True, into=dw_bufs[bi], into_off=seg_off[i], into_rows=True,
                          name="b_in_w_" + seg_name[i])
    dw_dt = _mm(ddt_raw, u, ta=True, name="b_in_w_dt")
    def du(i, hosted=None):
        return _mm(dseg[i], seg_src[i], b_koff=seg_off[i], name="b_in_x_" + seg_name[i], hosted=hosted)

    if shards is None:
        dus = [du(i) for i in range(5)]
        gw["w_in"] = jnp.concatenate([dw_bufs[0], dw_dt[:SSD_H], dw_bufs[1]], axis=0).T
    else:
        du0, recv_head = du(0, _pair_swap([dw_bufs[0]]))
        du2, recv_tail, recv_dt = du(2, _pair_swap([dw_bufs[1], dw_dt]))
        tail_rows = dw_bufs[1].shape[0]
        natural = lax.empty((c4 + tail_rows, d), CDT)
        natural = _add_cast_into(dw_bufs[0], recv_head, natural, 0, c3, "sum_w_in_head")
        natural = _add_cast_into(dw_dt, recv_dt, natural, c3, SSD_H, "sum_w_in_dt")
        natural = _add_cast_into(dw_bufs[1], recv_tail, natural, c4, tail_rows, "sum_w_in_tail")
        du4, gw["w_in"] = du(4, _chip_scatter_windows(natural, natural.shape[0] // N_DEV))
        dus = [du0, du(1), du2, du(3), du4]
    dus.append(_mm(ddt_raw, w_dt, name="b_in_x_dt"))

    def b_pre(xv, dh1v, d0, d1, d2, d3, d4, d5, g):
        dx, dg = _rms_bwd(xv, g, d0 + d1 + d2 + d3 + d4 + d5)
        return dh1v + dx, dg

    grad_x, sg["norm_mix_pre"] = _rows(b_pre, [x, dh1] + dus, [p["norm_mix_pre"]], [(d, F32)], [(1, d)],
                                       ts=256, name="b_norm_pre")
    return loss, grad_x, gw, sg, (received if shards is not None else None)


HBM = pl.BlockSpec(memory_space=pltpu.HBM)
MESH = pl.DeviceIdType.MESH


def _me_and_peers():
    x, y, c = lax.axis_index("x"), lax.axis_index("y"), lax.axis_index("c")
    me = 4 * x + 2 * y + c
    peers = [(x, y, 1 - c), (1 - x, y, c), (x, 1 - y, c), (1 - x, 1 - y, c),
             (1 - x, y, 1 - c), (x, 1 - y, 1 - c), (1 - x, 1 - y, 1 - c)]
    return me, peers


def _peer_index(peer):
    return 4 * peer[0] + 2 * peer[1] + peer[2]


def _exchange_copies(ins, outs, send_sems, recv_sems, local_sems, scatter):
    me, peers = _me_and_peers()
    copies = []
    for a in range(len(ins)):
        own = ins[a].at[me] if scatter else ins[a]
        copies.append(pltpu.make_async_copy(own, outs[a].at[me], local_sems.at[a]))
        for kk, peer in enumerate(peers):
            src = ins[a].at[_peer_index(peer)] if scatter else ins[a]
            copies.append(pltpu.make_async_remote_copy(
                src_ref=src, dst_ref=outs[a].at[me],
                send_sem=send_sems.at[a, kk], recv_sem=recv_sems.at[a, kk],
                device_id=peer, device_id_type=MESH))
    return copies


def _exchange_shapes(ins, scatter):
    return [jax.ShapeDtypeStruct(t.shape if scatter else (N_DEV,) + t.shape, t.dtype) for t in ins]


def _exchange_sems(n):
    return [pltpu.SemaphoreType.DMA((n, N_DEV - 1)), pltpu.SemaphoreType.DMA((n, N_DEV - 1)),
            pltpu.SemaphoreType.DMA((n,))]


def _exchange(ins, scatter, name):
    n = len(ins)

    def body(*refs):
        copies = _exchange_copies(refs[:n], refs[n:2 * n], *refs[2 * n:], scatter)
        for cp in copies:
            cp.start()
        for cp in copies:
            cp.wait()

    return pl.pallas_call(
        body, name=name,
        in_specs=[HBM] * n, out_specs=[HBM] * n,
        out_shape=_exchange_shapes(ins, scatter),
        scratch_shapes=_exchange_sems(n),
        compiler_params=pltpu.CompilerParams(has_side_effects=True),
    )(*ins)


def _gather_two_level(shards, name):
    n = len(shards)

    def body(*refs):
        ins, outs = refs[:n], refs[n:2 * n]
        send_sems, recv_sems, local_sems = refs[2 * n:]
        x, y, c = lax.axis_index("x"), lax.axis_index("y"), lax.axis_index("c")
        me, sib = (x, y, c), (x, y, 1 - c)
        chips = [(1 - x, y), (x, 1 - y), (1 - x, 1 - y)]

        def copy(a, k, block, to, src=None):
            slot = outs[a].at[_peer_index(block)]
            return pltpu.make_async_remote_copy(
                src_ref=slot if src is None else src, dst_ref=slot,
                send_sem=send_sems.at[a, k], recv_sem=recv_sems.at[a, k], device_id=to, device_id_type=MESH)

        own = [pltpu.make_async_copy(ins[a], outs[a].at[_peer_index(me)], local_sems.at[a]) for a in range(n)]
        first = []
        for a in range(n):
            first.append(copy(a, 0, me, sib, src=ins[a]))
            first += [copy(a, 1 + j, me, (*chip, c), src=ins[a]) for j, chip in enumerate(chips)]
        for cp in own + first:
            cp.start()
        passed = []
        for j, chip in enumerate(chips):
            for a in range(n):
                copy(a, 1 + j, (*chip, c), me).wait_recv()
                fwd = copy(a, 4 + j, (*chip, c), sib)
                fwd.start()
                passed.append(fwd)
        for a in range(n):
            copy(a, 0, sib, me).wait_recv()
            for j, chip in enumerate(chips):
                copy(a, 4 + j, (*chip, 1 - c), me).wait_recv()
        for cp in first + passed:
            cp.wait_send()
        for cp in own:
            cp.wait()

    return pl.pallas_call(
        body, name=name,
        in_specs=[HBM] * n, out_specs=[HBM] * n,
        out_shape=_exchange_shapes(shards, False),
        scratch_shapes=_exchange_sems(n),
        compiler_params=pltpu.CompilerParams(has_side_effects=True),
    )(*shards)


class _Hosted:
    def __init__(self, ins, shapes, sems, copies):
        self.ins, self.shapes, self.sems, self.copies = ins, shapes, sems, copies


def _pair_swap(arrays):
    n = len(arrays)

    def copies(in_refs, out_refs, sems):
        sib = (lax.axis_index("x"), lax.axis_index("y"), 1 - lax.axis_index("c"))
        return [pltpu.make_async_remote_copy(
            src_ref=in_refs[i], dst_ref=out_refs[i], send_sem=sems[0].at[i], recv_sem=sems[1].at[i],
            device_id=sib, device_id_type=MESH) for i in range(n)]

    return _Hosted(list(arrays), [jax.ShapeDtypeStruct(t.shape, t.dtype) for t in arrays],
                   [pltpu.SemaphoreType.DMA((n,)), pltpu.SemaphoreType.DMA((n,))], copies)


def _add_cast_into(a, b, into, row_off, rows, name):
    w = a.shape[1]
    tr = _pick(rows, (512, 256, 128, rows))
    nt = rows // tr
    assert rows % tr == 0 and row_off % ROW_ALIGN == 0 and tr % ROW_ALIGN == 0
    spec = pl.BlockSpec((tr, w), lambda i: (i, 0))

    def body(a_ref, b_ref, into_ref, o_ref, slots, sems):
        i = pl.program_id(0)
        slot = i % 2

        def out_copy(step, s):
            dst = o_ref.at[pl.ds(pl.multiple_of(row_off + step * tr, ROW_ALIGN), tr)]
            return pltpu.make_async_copy(slots.at[s], dst, sems.at[s])

        @pl.when(i >= 2)
        def _():
            out_copy(i - 2, slot).wait()

        slots[slot] = (a_ref[...] + b_ref[...]).astype(slots.dtype)
        out_copy(i, slot).start()

        @pl.when(i == nt - 1)
        def _():
            out_copy(i, slot).wait()
            if nt > 1:
                out_copy(i - 1, 1 - slot).wait()

    return pl.pallas_call(
        body, name=name, grid=(nt,),
        in_specs=[spec, spec, pl.BlockSpec(memory_space=pl.ANY)],
        out_specs=pl.BlockSpec(memory_space=pl.ANY),
        out_shape=jax.ShapeDtypeStruct(into.shape, into.dtype),
        scratch_shapes=[pltpu.VMEM((2, tr, w), into.dtype), pltpu.SemaphoreType.DMA((2,))],
        input_output_aliases={2: 0},
        compiler_params=_cparams(("arbitrary",)),
    )(a, b, into)


N_CHIP = 4


def _chip_scatter(t, name):
    def body(t_ref, o_ref, send_sems, recv_sems, local_sem):
        x, y, c = lax.axis_index("x"), lax.axis_index("y"), lax.axis_index("c")
        mine = 2 * x + y
        copies = [pltpu.make_async_copy(t_ref.at[mine], o_ref.at[mine], local_sem)]
        for j, (px, py) in enumerate([(1 - x, y), (x, 1 - y), (1 - x, 1 - y)]):
            copies.append(pltpu.make_async_remote_copy(
                src_ref=t_ref.at[2 * px + py], dst_ref=o_ref.at[mine],
                send_sem=send_sems.at[j], recv_sem=recv_sems.at[j], device_id=(px, py, c), device_id_type=MESH))
        for cp in copies:
            cp.start()
        for cp in copies:
            cp.wait()

    return pl.pallas_call(
        body, name=name, in_specs=[HBM], out_specs=HBM,
        out_shape=jax.ShapeDtypeStruct(t.shape, t.dtype),
        scratch_shapes=[pltpu.SemaphoreType.DMA((N_CHIP - 1,)), pltpu.SemaphoreType.DMA((N_CHIP - 1,)),
                        pltpu.SemaphoreType.DMA],
        compiler_params=pltpu.CompilerParams(has_side_effects=True),
    )(t)


def _all_reduce_small(v, name):
    r, c = v.shape

    def body(v_ref, o_ref, buf, send_sems, recv_sems):
        me, peers = _me_and_peers()
        buf[me] = v_ref[...]
        copies = []
        for kk, peer in enumerate(peers):
            cp = pltpu.make_async_remote_copy(
                src_ref=v_ref, dst_ref=buf.at[me],
                send_sem=send_sems.at[kk], recv_sem=recv_sems.at[kk],
                device_id=peer, device_id_type=MESH)
            cp.start()
            copies.append(cp)
        for cp in copies:
            cp.wait()
        acc = buf[0]
        for i in range(1, N_DEV):
            acc = acc + buf[i]
        o_ref[...] = acc

    return pl.pallas_call(
        body, name=name,
        in_specs=[pl.BlockSpec(memory_space=pltpu.VMEM)],
        out_specs=pl.BlockSpec(memory_space=pltpu.VMEM),
        out_shape=jax.ShapeDtypeStruct((r, c), F32),
        scratch_shapes=[pltpu.VMEM((N_DEV, r, c), F32),
                        pltpu.SemaphoreType.DMA((N_DEV - 1,)), pltpu.SemaphoreType.DMA((N_DEV - 1,))],
        compiler_params=pltpu.CompilerParams(has_side_effects=True),
    )(v)


def _adamw_math(g, w, m, v):
    m2 = ADAM_B1 * m + (1.0 - ADAM_B1) * g
    v2 = ADAM_B2 * v + (1.0 - ADAM_B2) * jnp.square(g)
    m_hat = m2 / (1.0 - ADAM_B1 ** ADAM_STEP)
    v_hat = v2 / (1.0 - ADAM_B2 ** ADAM_STEP)
    delta = -ADAM_LR * (m_hat / (jnp.sqrt(v_hat) + ADAM_EPS) + ADAM_WD * w)
    return delta, m2, v2


def _adamw_reduce(parts, w, m, v, name):
    r, c = w.shape
    nparts = parts.shape[0]
    tr = _pick(r, (128, 64, 32, 16, 8))

    def body(p_ref, w_ref, m_ref, v_ref, g_ref, d_ref, m2_ref, v2_ref):
        g = p_ref[0].astype(F32)
        for i in range(1, nparts):
            g = g + p_ref[i].astype(F32)
        delta, m2, v2 = _adamw_math(g, w_ref[...], m_ref[...], v_ref[...])
        g_ref[...] = g
        d_ref[...] = delta
        m2_ref[...] = m2
        v2_ref[...] = v2

    tile = pl.BlockSpec((tr, c), lambda i: (i, 0))
    shp = jax.ShapeDtypeStruct((r, c), F32)
    return pl.pallas_call(
        body, name=name, grid=(r // tr,),
        in_specs=[pl.BlockSpec((nparts, tr, c), lambda i: (0, i, 0)), tile, tile, tile],
        out_specs=[tile] * 4, out_shape=[shp] * 4,
        compiler_params=_cparams(("parallel",)),
    )(parts, w, m, v)


ROW_ALIGN = 16


def _window(shard):
    lead = max((j * shard) % ROW_ALIGN for j in range(N_DEV))
    return -(-(lead + shard) // ROW_ALIGN) * ROW_ALIGN


def _chip_scatter_windows(t, shard):
    cols = t.shape[1]
    win = _window(shard)
    assert all((j * shard // ROW_ALIGN) * ROW_ALIGN + win <= t.shape[0] for j in range(N_DEV))

    def copies(in_refs, out_refs, sems):
        (t_ref,), (o_ref,), (send_sems, recv_sems, local_sem) = in_refs, out_refs, sems
        x, y, c = lax.axis_index("x"), lax.axis_index("y"), lax.axis_index("c")
        mine = 2 * x + y

        def window(q):
            a0 = pl.multiple_of(((2 * q + c) * shard // ROW_ALIGN) * ROW_ALIGN, ROW_ALIGN)
            return t_ref.at[pl.ds(a0, win)]

        res = [pltpu.make_async_copy(window(mine), o_ref.at[mine], local_sem)]
        for j, (px, py) in enumerate([(1 - x, y), (x, 1 - y), (1 - x, 1 - y)]):
            res.append(pltpu.make_async_remote_copy(
                src_ref=window(2 * px + py), dst_ref=o_ref.at[mine],
                send_sem=send_sems.at[j], recv_sem=recv_sems.at[j], device_id=(px, py, c), device_id_type=MESH))
        return res

    return _Hosted([t], [jax.ShapeDtypeStruct((N_CHIP, win, cols), t.dtype)],
                   [pltpu.SemaphoreType.DMA((N_CHIP - 1,)), pltpu.SemaphoreType.DMA((N_CHIP - 1,)),
                    pltpu.SemaphoreType.DMA], copies)


ADAMW_TC = 256


def _adamw_windows(parts, w, m, v, name):
    r, c = w.shape
    nparts, win, _ = parts.shape
    tc = min(ADAMW_TC, c)

    def body(p_ref, w_ref, m_ref, v_ref, g_ref, d_ref, m2_ref, v2_ref):
        me, _ = _me_and_peers()
        gw_ = p_ref[0].astype(F32)
        for i in range(1, nparts):
            gw_ = gw_ + p_ref[i].astype(F32)
        for j in range(N_DEV):
            @pl.when(me == j)
            def _():
                lead = (j * r) % ROW_ALIGN
                g = (pltpu.roll(gw_, win - lead, 0) if lead else gw_)[:r]
                delta, m2, v2 = _adamw_math(g, w_ref[...], m_ref[...], v_ref[...])
                g_ref[...] = g
                d_ref[...] = delta
                m2_ref[...] = m2
                v2_ref[...] = v2

    tile = pl.BlockSpec((r, tc), lambda i: (0, i))
    shp = jax.ShapeDtypeStruct((r, c), F32)
    return pl.pallas_call(
        body, name=name, grid=(c // tc,),
        in_specs=[pl.BlockSpec((nparts, win, tc), lambda i: (0, 0, i)), tile, tile, tile],
        out_specs=[tile] * 4, out_shape=[shp] * 4,
        compiler_params=_cparams(("parallel",)),
    )(parts, w, m, v)


def _adamw_plain(g, w, m, v, name):
    def body(g_ref, w_ref, m_ref, v_ref, d_ref, m2_ref, v2_ref):
        delta, m2, v2 = _adamw_math(g_ref[...], w_ref[...], m_ref[...], v_ref[...])
        d_ref[...] = delta
        m2_ref[...] = m2
        v2_ref[...] = v2

    spec = pl.BlockSpec(memory_space=pltpu.VMEM)
    shp = jax.ShapeDtypeStruct(g.shape, F32)
    return pl.pallas_call(
        body, name=name, in_specs=[spec] * 4, out_specs=[spec] * 3, out_shape=[shp] * 3,
    )(g, w, m, v)


SMALL_ROWS, SMALL_COLS = 16, 3072


def _small_step(sg, gcw, loss, ws, ms, vs):
    ns = len(sg)
    widths = [t.shape[1] for t in sg]
    kk_, ch = gcw.shape[1], gcw.shape[2]
    assert ns < SMALL_ROWS and max(widths) <= SMALL_COLS

    def reduce_body(*refs):
        g_refs = refs[:ns]
        gcw_ref, loss_ref, tot_ref, totc_ref = refs[ns:ns + 4]
        mine, buf, minec, bufc, send_sems, recv_sems = refs[ns + 4:]
        me, peers = _me_and_peers()

        mine[...] = jnp.zeros_like(mine)
        for i in range(ns):
            mine[i:i + 1, 0:widths[i]] = g_refs[i][...]
        mine[ns:ns + 1, 0:LANES] = jnp.broadcast_to(loss_ref[...], (1, LANES))
        minec[...] = gcw_ref[...]
        buf[me] = mine[...]
        bufc[me] = minec[...]
        copies = []
        for j, peer in enumerate(peers):
            copies.append(pltpu.make_async_remote_copy(
                src_ref=mine, dst_ref=buf.at[me], send_sem=send_sems.at[0, j], recv_sem=recv_sems.at[0, j],
                device_id=peer, device_id_type=MESH))
            copies.append(pltpu.make_async_remote_copy(
                src_ref=minec, dst_ref=bufc.at[me], send_sem=send_sems.at[1, j], recv_sem=recv_sems.at[1, j],
                device_id=peer, device_id_type=MESH))
        for cp in copies:
            cp.start()
        for cp in copies:
            cp.wait()
        tot = buf[0]
        totc = bufc[0]
        for i in range(1, N_DEV):
            tot = tot + buf[i]
            totc = totc + bufc[i]
        tot_ref[...] = tot
        totc_ref[...] = totc

    vm = pl.BlockSpec(memory_space=pltpu.VMEM)
    tot, totc = pl.pallas_call(
        reduce_body, name="small_reduce",
        in_specs=[vm] * (ns + 2), out_specs=[vm, vm],
        out_shape=[jax.ShapeDtypeStruct((SMALL_ROWS, SMALL_COLS), F32), jax.ShapeDtypeStruct((N_DEV, kk_, ch), F32)],
        scratch_shapes=[pltpu.VMEM((SMALL_ROWS, SMALL_COLS), F32), pltpu.VMEM((N_DEV, SMALL_ROWS, SMALL_COLS), F32),
                        pltpu.VMEM((N_DEV, kk_, ch), F32), pltpu.VMEM((N_DEV, N_DEV, kk_, ch), F32),
                        pltpu.SemaphoreType.DMA((2, N_DEV - 1)), pltpu.SemaphoreType.DMA((2, N_DEV - 1))],
        compiler_params=pltpu.CompilerParams(has_side_effects=True),
    )(*sg, gcw, loss)

    def adamw_body(*refs):
        tot_ref, totc_ref = refs[:2]
        w_refs, m_refs, v_refs = (refs[2 + i * (ns + 1):2 + (i + 1) * (ns + 1)] for i in range(3))
        outs = refs[3 * ns + 5:]
        loss_out = outs[0]
        go, do_, mo, vo = (outs[1 + i * (ns + 1):1 + (i + 1) * (ns + 1)] for i in range(4))
        me, _ = _me_and_peers()
        loss_out[...] = tot_ref[ns:ns + 1, 0:1]
        for i in range(ns + 1):
            g = tot_ref[i:i + 1, 0:widths[i]] if i < ns else totc_ref[me]
            delta, m2, v2 = _adamw_math(g, w_refs[i][...], m_refs[i][...], v_refs[i][...])
            go[i][...] = g
            do_[i][...] = delta
            mo[i][...] = m2
            vo[i][...] = v2

    shapes = [jax.ShapeDtypeStruct(t.shape, F32) for t in ws]
    res = pl.pallas_call(
        adamw_body, name="small_adamw",
        in_specs=[vm] * (3 * ns + 5), out_specs=[vm] * (4 * ns + 5),
        out_shape=[jax.ShapeDtypeStruct((1, 1), F32)] + shapes * 4,
    )(tot, totc, *ws, *ms, *vs)
    n1 = ns + 1
    return res[0], res[1:1 + n1], res[1 + n1:1 + 2 * n1], res[1 + 2 * n1:1 + 3 * n1], res[1 + 3 * n1:]


def _cast_shard(w, name):
    r = w.shape[0]
    return _rows(lambda t: t, [w], [], [(w.shape[1], CDT)], ts=_pick(r, (256, 128)), name=name)[0]


BIG = ["w_in", "w_mem_kv", "w_up", "w_sb_out", "w_ssd_out", "w_mem_out", "w_o", "w_down"]
LATE_W = BIG[1:]
COL_SHARDED = ("w_in", "w_mem_kv", "w_up")
SMALL = ["norm_mix_pre", "conv_b", "dt_bias", "a_log", "d_skip", "ssd_norm", "norm_mem",
         "norm_mix_post", "norm_mlp_pre", "norm_mlp_post"]
ALL_W = ["norm_mix_pre", "w_in", "conv_w", "conv_b", "dt_bias", "a_log", "d_skip", "ssd_norm", "norm_mem",
         "w_mem_kv", "w_sb_out", "w_ssd_out", "w_mem_out", "w_o", "norm_mix_post", "norm_mlp_pre", "w_up",
         "w_down", "norm_mlp_post"]
LANES = 128


def _pack_rows(vecs):
    parts, offs, off = [], [], 0
    for t in vecs:
        flat = t.reshape(-1)
        n = flat.shape[0]
        rows = -(-n // (8 * LANES)) * 8
        parts.append(jnp.pad(flat, (0, rows * LANES - n)).reshape(rows, LANES))
        offs.append((off, n))
        off += rows
    return jnp.concatenate(parts, axis=0), offs


def _unpack_rows(packed, offs, shapes):
    out = []
    for (off, n), shape in zip(offs, shapes):
        rows = -(-n // (8 * LANES)) * 8
        out.append(packed[off:off + rows].reshape(-1)[:n].reshape(shape))
    return out


def kernel(x, mem, norm_mix_pre, w_in, conv_w, conv_b, dt_bias, a_log, d_skip, ssd_norm, norm_mem, w_mem_kv, w_sb_out, w_ssd_out, w_mem_out, w_o, norm_mix_post, norm_mlp_pre, w_up, w_down, norm_mlp_post, loss_target, m_norm_mix_pre, m_w_in, m_conv_w, m_conv_b, m_dt_bias, m_a_log, m_d_skip, m_ssd_norm, m_norm_mem, m_w_mem_kv, m_w_sb_out, m_w_ssd_out, m_w_mem_out, m_w_o, m_norm_mix_post, m_norm_mlp_pre, m_w_up, m_w_down, m_norm_mlp_post, v_norm_mix_pre, v_w_in, v_conv_w, v_conv_b, v_dt_bias, v_a_log, v_d_skip, v_ssd_norm, v_norm_mem, v_w_mem_kv, v_w_sb_out, v_w_ssd_out, v_w_mem_out, v_w_o, v_norm_mix_post, v_norm_mlp_pre, v_w_up, v_w_down, v_norm_mlp_post):
    wd = dict(norm_mix_pre=norm_mix_pre, w_in=w_in, conv_w=conv_w, conv_b=conv_b, dt_bias=dt_bias, a_log=a_log,
              d_skip=d_skip, ssd_norm=ssd_norm, norm_mem=norm_mem, w_mem_kv=w_mem_kv, w_sb_out=w_sb_out,
              w_ssd_out=w_ssd_out, w_mem_out=w_mem_out, w_o=w_o, norm_mix_post=norm_mix_post,
              norm_mlp_pre=norm_mlp_pre, w_up=w_up, w_down=w_down, norm_mlp_post=norm_mlp_post)
    md = dict(norm_mix_pre=m_norm_mix_pre, w_in=m_w_in, conv_w=m_conv_w, conv_b=m_conv_b, dt_bias=m_dt_bias,
              a_log=m_a_log, d_skip=m_d_skip, ssd_norm=m_ssd_norm, norm_mem=m_norm_mem, w_mem_kv=m_w_mem_kv,
              w_sb_out=m_w_sb_out, w_ssd_out=m_w_ssd_out, w_mem_out=m_w_mem_out, w_o=m_w_o,
              norm_mix_post=m_norm_mix_post, norm_mlp_pre=m_norm_mlp_pre, w_up=m_w_up, w_down=m_w_down,
              norm_mlp_post=m_norm_mlp_post)
    vd = dict(norm_mix_pre=v_norm_mix_pre, w_in=v_w_in, conv_w=v_conv_w, conv_b=v_conv_b, dt_bias=v_dt_bias,
              a_log=v_a_log, d_skip=v_d_skip, ssd_norm=v_ssd_norm, norm_mem=v_norm_mem, w_mem_kv=v_w_mem_kv,
              w_sb_out=v_w_sb_out, w_ssd_out=v_w_ssd_out, w_mem_out=v_w_mem_out, w_o=v_w_o,
              norm_mix_post=v_norm_mix_post, norm_mlp_pre=v_norm_mlp_pre, w_up=v_w_up, w_down=v_w_down,
              norm_mlp_post=v_norm_mlp_post)
    w_in_t, m_in_t, v_in_t = (t["w_in"][0].T for t in (wd, md, vd))
    shards = {n: _cast_shard(w_in_t if n == "w_in" else wd[n][0], "cast_" + n) for n in BIG}
    w_in_g, conv_w_g = _gather_two_level([shards["w_in"], wd["conv_w"][0]], "gather_w_in")
    wt = {"w_in": w_in_g.reshape(N_DEV * w_in_g.shape[1], w_in_g.shape[2])}
    ch = conv_w_g.shape[2]

    p = {n: wd[n] for n in SMALL}
    p["conv_w"] = conv_w_g.transpose(1, 0, 2).reshape(CONV_K, N_DEV * ch)
    loss, grad_x, gw, sg, late_received = _local_step(x[0], mem[0], loss_target[0], p, wt,
                                                      [shards[n] for n in LATE_W])

    received = dict(zip(LATE_W, late_received))
    w_in_windows = gw["w_in"]

    grads, deltas, new_m, new_v = {}, {}, {}, {}
    for n in BIG:
        if n == "w_in":
            res = [t.T for t in _adamw_windows(w_in_windows, w_in_t, m_in_t, v_in_t, "adamw_" + n)]
        else:
            res = _adamw_reduce(received[n], wd[n][0], md[n][0], vd[n][0], "adamw_" + n)
        grads[n], deltas[n], new_m[n], new_v[n] = (t[None] for t in res)
    small_names = SMALL + ["conv_w"]
    gcw = sg["conv_w"].reshape(CONV_K, N_DEV, ch).transpose(1, 0, 2)
    small_of = lambda dct: [dct[n] for n in SMALL] + [dct["conv_w"][0]]
    loss_red, g_s, d_s, m_s, v_s = _small_step([sg[n] for n in SMALL], gcw, loss, small_of(wd), small_of(md),
                                               small_of(vd))
    for i, n in enumerate(small_names):
        shape = wd[n].shape
        grads[n], deltas[n], new_m[n], new_v[n] = (t.reshape(shape) for t in (g_s[i], d_s[i], m_s[i], v_s[i]))
    loss_out = loss_red.reshape(())

    return (loss_out, grad_x[None], *[grads[n] for n in ALL_W], *[deltas[n] for n in ALL_W],
            *[new_m[n] for n in ALL_W], *[new_v[n] for n in ALL_W])
```

```python
import functools

import jax
import jax.numpy as jnp
from jax import lax
from jax.experimental import pallas as pl
from jax.experimental.pallas import tpu as pltpu

F32 = jnp.float32
BF16 = jnp.bfloat16
CDT = jnp.bfloat16
EPS = 1e-6
VMEM_LIMIT = 56 * 1024 * 1024

N_DEV = 8
D_MODEL = 1024
SB_H, SB_DH = 16, 64
SSD_G, SSD_R, SSD_P, SSD_N, SSD_L = 4, 8, 64, 128, 128
SSD_H = SSD_G * SSD_R
SSD_INNER = SSD_H * SSD_P
CONV_K = 4
CONV_DIM = SSD_INNER + 2 * SSD_G * SSD_N
MEM_H, MEM_DH = 4, 256
DT_PAD = 128
SB_TQ, SB_BK = 2048, 256
CONV_PAD = 8
MM_TILE, MM_TILE_K = 1024, 2048

ADAM_LR, ADAM_B1, ADAM_B2, ADAM_EPS, ADAM_WD, ADAM_STEP = 0.001, 0.9, 0.999, 1e-08, 0.01, 10

NT = (((1,), (1,)), ((), ()))
TN = (((0,), (0,)), ((), ()))
NN = (((1,), (0,)), ((), ()))


def _cparams(sem=None):
    return pltpu.CompilerParams(dimension_semantics=sem, vmem_limit_bytes=VMEM_LIMIT)


def _pick(n, cands):
    for c in cands:
        if n % c == 0:
            return c
    return n


def _dot(a, b, dims=NN):
    return lax.dot_general(a.astype(CDT), b.astype(CDT), dims, preferred_element_type=F32)


def _split_dot(x, t, left, pieces):
    if CDT == F32:
        return lax.dot_general(t, x, NN, preferred_element_type=F32) if left else \
            lax.dot_general(x, t, NN, preferred_element_type=F32)
    acc = None
    rem = x
    for _ in range(pieces):
        hi = rem.astype(BF16)
        rem = rem - hi.astype(F32)
        d = lax.dot_general(t, hi, NN, preferred_element_type=F32) if left else \
            lax.dot_general(hi, t, NN, preferred_element_type=F32)
        acc = d if acc is None else acc + d
    return acc


def _iota(shape, dim):
    return lax.broadcasted_iota(jnp.int32, shape, dim)


def _sigmoid(x):
    return 1.0 / (1.0 + jnp.exp(-x))


def _silu(x):
    return x * _sigmoid(x)


def _dsilu(x):
    s = _sigmoid(x)
    return s * (1.0 + x * (1.0 - s))


def _softplus(x):
    return jnp.maximum(x, 0.0) + jnp.log(1.0 + jnp.exp(-jnp.abs(x)))


def _rms(x, g):
    r = lax.rsqrt(jnp.mean(x * x, axis=-1, keepdims=True) + EPS)
    return x * r * g


def _rms_bwd(x, g, dy):
    r = lax.rsqrt(jnp.mean(x * x, axis=-1, keepdims=True) + EPS)
    n = x * r
    dn = dy * g
    dx = r * (dn - n * jnp.mean(dn * n, axis=-1, keepdims=True))
    dg = jnp.sum(dy * n, axis=0, keepdims=True)
    return dx, dg


def _mm(a, b, *, ta=False, tb=False, out_dtype=F32, name, b_off=0, n=None, b_koff=0, into=None, into_off=0,
        into_rows=False, hosted=None, epilogue=None):
    m = a.shape[1] if ta else a.shape[0]
    k = a.shape[0] if ta else a.shape[1]
    if n is None:
        n = b.shape[0] if tb else b.shape[1]
    assert b_koff + k <= (b.shape[1] if tb else b.shape[0])
    bm = _pick(m, (MM_TILE, 512, 256, 128))
    bn = _pick(n, (MM_TILE, 512, 256, 128))
    bk = next(c for c in (MM_TILE_K, 1024, 512, 256, 128, k) if k % c == 0 and b_koff % c == 0)
    nk = k // bk
    assert b_off % bn == 0 and into_off % (bm if into_rows else bn) == 0
    jb, kb = b_off // bn, b_koff // bk
    io, jo = (into_off // bm, 0) if into_rows else (0, into_off // bn)
    dims = (((0 if ta else 1,), (1 if tb else 0,)), ((), ()))
    grid = (m // bm, n // bn, nk)
    off = 1 if into is not None else 0
    nh_in = len(hosted.ins) if hosted else 0
    nh_out = len(hosted.shapes) if hosted else 0
    epi_fn, extras, out_dtypes = epilogue if epilogue else (lambda p: (p,), [], [out_dtype])
    ne, no = len(extras), len(out_dtypes)
    assert not (epilogue and into is not None)

    def body(a_ref, b_ref, *rest):
        e_refs = rest[off:off + ne]
        o_refs = rest[off + ne + nh_in:off + ne + nh_in + no]
        acc_ref = rest[off + ne + nh_in + no + nh_out]

        def emit(total):
            res = epi_fn(total, *[e[...] for e in e_refs])
            for o_ref, val in zip(o_refs, res):
                o_ref[...] = val.astype(o_ref.dtype)

        if hosted:
            h_refs = (rest[off + ne:off + ne + nh_in],
                      rest[off + ne + nh_in + no:off + ne + nh_in + no + nh_out],
                      rest[off + ne + nh_in + no + nh_out + 1:])
            step = (pl.program_id(0) * grid[1] + pl.program_id(1)) * grid[2] + pl.program_id(2)

            @pl.when(step == 0)
            def _():
                for cp in hosted.copies(*h_refs):
                    cp.start()

        part = _dot(a_ref[...], b_ref[...], dims)
        if nk == 1:
            emit(part)
        else:
            kk = pl.program_id(2)

            @pl.when(kk == 0)
            def _():
                acc_ref[...] = part

            @pl.when(jnp.logical_and(kk > 0, kk < nk - 1))
            def _():
                acc_ref[...] += part

            @pl.when(kk == nk - 1)
            def _():
                emit(acc_ref[...] + part)

        if hosted:
            @pl.when(step == grid[0] * grid[1] * grid[2] - 1)
            def _():
                for cp in hosted.copies(*h_refs):
                    cp.wait()

    a_spec = pl.BlockSpec((bk, bm), lambda i, j, kk: (kk, i)) if ta else \
        pl.BlockSpec((bm, bk), lambda i, j, kk: (i, kk))
    b_spec = pl.BlockSpec((bn, bk), lambda i, j, kk: (j + jb, kk + kb)) if tb else \
        pl.BlockSpec((bk, bn), lambda i, j, kk: (kk + kb, j + jb))
    extra = {} if into is None else {"input_output_aliases": {2: 0}}
    out_shapes = [jax.ShapeDtypeStruct((m, n), dt) for dt in out_dtypes] if into is None else \
        [jax.ShapeDtypeStruct(into.shape, into.dtype)]
    block = pl.BlockSpec((bm, bn), lambda i, j, kk: (i, j))
    res = pl.pallas_call(
        body, name=name, grid=grid,
        in_specs=[a_spec, b_spec] + ([] if into is None else [pl.BlockSpec(memory_space=pl.ANY)])
        + [block] * ne + [HBM] * nh_in,
        out_specs=[pl.BlockSpec((bm, bn), lambda i, j, kk: (i + io, j + jo))] * no + [HBM] * nh_out,
        out_shape=out_shapes + (list(hosted.shapes) if hosted else []),
        scratch_shapes=[pltpu.VMEM((bm, bn) if nk > 1 else (8, 128), F32)] + (list(hosted.sems) if hosted else []),
        compiler_params=_cparams(("arbitrary",) * 3 if hosted else ("parallel", "parallel", "arbitrary")),
        **extra,
    )(*((a, b) if into is None else (a, b, into)), *extras, *(hosted.ins if hosted else ()))
    return res if (hosted or epilogue) else res[0]


def _rows(fn, tiled, params, outs, accs=(), *, ts, name):
    s = tiled[0].shape[0]
    ts = min(ts, s)
    assert s % ts == 0
    nt, npar, no, na = len(tiled), len(params), len(outs), len(accs)

    def body(*refs):
        i = pl.program_id(0)
        vals = [r[...] for r in refs[:nt + npar]]
        res = fn(*vals)
        if not isinstance(res, (tuple, list)):
            res = (res,)
        orefs = refs[nt + npar:nt + npar + no]
        arefs = refs[nt + npar + no:]
        for r_, val in zip(orefs, res[:no]):
            r_[...] = val.astype(r_.dtype)
        if na:
            @pl.when(i == 0)
            def _():
                for r_ in arefs:
                    r_[...] = jnp.zeros_like(r_)

            for r_, val in zip(arefs, res[no:]):
                r_[...] += jnp.broadcast_to(val, r_.shape)

    in_specs = [pl.BlockSpec((ts, a.shape[1]), lambda i: (i, 0)) for a in tiled]
    in_specs += [pl.BlockSpec(p.shape, lambda i: (0, 0)) for p in params]
    out_specs = [pl.BlockSpec((ts, w), lambda i: (i, 0)) for (w, _) in outs]
    out_specs += [pl.BlockSpec(shape, lambda i: (0, 0)) for shape in accs]
    out_shape = [jax.ShapeDtypeStruct((s, w), dt) for (w, dt) in outs]
    out_shape += [jax.ShapeDtypeStruct(shape, F32) for shape in accs]
    res = pl.pallas_call(
        body, name=name, grid=(s // ts,),
        in_specs=in_specs, out_specs=out_specs, out_shape=out_shape,
        compiler_params=_cparams(("arbitrary",)),
    )(*tiled, *params)
    return res


def _sb_block(qs, kb, diag):
    tq, bk = qs.shape[0], kb.shape[0]
    z = _dot(qs, kb, NT)
    lb = jnp.minimum(z, 0.0) - jnp.log(1.0 + jnp.exp(-jnp.abs(z)))
    lk = lb - z
    if diag is None:
        return None, lb, lk
    causal = (diag + _iota((tq, bk), 1)) < _iota((tq, bk), 0)
    return causal, lb, jnp.where(causal, lk, 0.0)


def _fused_exchange(scatter, ncols, nsteps):
    def hooks(ins, outs, sems):
        step = pl.program_id(0) * ncols + pl.program_id(1)

        @pl.when(step == 0)
        def _():
            for cp in _exchange_copies(ins, outs, *sems, scatter):
                cp.start()

        def finish():
            @pl.when(step == nsteps - 1)
            def _():
                for cp in _exchange_copies(ins, outs, *sems, scatter):
                    cp.wait()
        return finish
    return hooks


SB_PAIR = 128


def _sb_fwd_pairs(qkv, comm=()):
    s, d3 = qkv.shape
    d = d3 // 3
    npair = d // SB_PAIR
    tq, bk = min(SB_TQ, s), min(SB_BK, s)
    scale = SB_DH ** -0.5
    nc = len(comm)
    hooks = _fused_exchange(False, s // tq, npair * (s // tq))

    def body(q_ref, k_ref, v_ref, *rest):
        y_ref, lt_ref = rest[nc:nc + 2]
        if nc:
            finish = hooks(rest[:nc], rest[nc + 2:2 * nc + 2], rest[2 * nc + 2:])
        i = pl.program_id(1)
        q0 = i * tq
        q2 = q_ref[...] * scale
        lane_head = (_iota((1, SB_PAIR), 1) >= SB_DH).astype(jnp.int32)
        tri = (_iota((bk, bk), 0) > _iota((bk, bk), 1)).astype(CDT)

        def head(hh, y):
            mine = lane_head == hh
            qs = jnp.where(mine, q2, jnp.zeros_like(q2))

            def step(k0, carry, diag, r0=0):
                cf, acc = carry
                kb = k_ref[pl.ds(k0, bk), :]
                vb = v_ref[pl.ds(k0, bk), :]
                causal, lb, lk = _sb_block(qs[r0:], kb, diag)
                w = jnp.exp(lb + cf + _split_dot(lk, tri, False, 2))
                if causal is not None:
                    w = jnp.where(causal, w, 0.0)
                return cf + jnp.sum(lk, axis=1, keepdims=True), acc + _dot(w, vb)

            carry = (jnp.zeros((tq, 1), F32), jnp.zeros((tq, SB_PAIR), F32))
            for dd in reversed(range(tq // bk)):
                r0 = dd * bk
                sub = step(pl.multiple_of(q0 + r0, bk), tuple(t[r0:] for t in carry), 0, r0)
                carry = tuple(jnp.concatenate([t[:r0], u], axis=0) if r0 else u for t, u in zip(carry, sub))
            nfull = q0 // bk
            cf, acc = lax.fori_loop(
                0, nfull, lambda jj, c: step(pl.multiple_of((nfull - 1 - jj) * bk, bk), c, None), carry)
            lt_ref[hh] = cf
            return jnp.where(mine, acc, y)

        y_ref[...] = lax.fori_loop(0, 2, head, jnp.zeros((tq, SB_PAIR), F32)).astype(y_ref.dtype)
        if nc:
            finish()

    return pl.pallas_call(
        body, name="sb_fwd", grid=(npair, s // tq),
        in_specs=[pl.BlockSpec((tq, SB_PAIR), lambda a, i: (i, a)),
                  pl.BlockSpec((s, SB_PAIR), lambda a, i: (0, npair + a)),
                  pl.BlockSpec((s, SB_PAIR), lambda a, i: (0, 2 * npair + a))] + [HBM] * nc,
        out_specs=[pl.BlockSpec((tq, SB_PAIR), lambda a, i: (i, a)),
                   pl.BlockSpec((2, tq, 1), lambda a, i: (a, i, 0))] + [HBM] * nc,
        out_shape=[jax.ShapeDtypeStruct((s, d), CDT), jax.ShapeDtypeStruct((2 * npair, s, 1), F32)]
        + _exchange_shapes(comm, False),
        scratch_shapes=_exchange_sems(nc) if nc else [],
        compiler_params=_cparams(("arbitrary", "arbitrary")),
    )(qkv, qkv, qkv, *comm)


def _sb_bwd_pairs(qkv, ltot, dy, comm=()):
    s, d3 = qkv.shape
    d = d3 // 3
    npair = d // SB_PAIR
    tq, bk = min(SB_TQ, s), min(SB_BK, s)
    scale = SB_DH ** -0.5
    nc = len(comm)
    hooks = _fused_exchange(True, s // tq, npair * (s // tq))

    def body(q_ref, k_ref, v_ref, lt_ref, dy_ref, *rest):
        dq_ref, dk_ref, dv_ref = rest[nc:nc + 3]
        if nc:
            finish = hooks(rest[:nc], rest[nc + 3:2 * nc + 3], rest[2 * nc + 3:])
        i = pl.program_id(1)

        @pl.when(i == 0)
        def _():
            dk_ref[...] = jnp.zeros_like(dk_ref)
            dv_ref[...] = jnp.zeros_like(dv_ref)

        q0 = i * tq
        q2 = q_ref[...] * scale
        do2 = dy_ref[...].astype(CDT)
        lane_head = (_iota((1, SB_PAIR), 1) >= SB_DH).astype(jnp.int32)
        tri_le = (_iota((bk, bk), 0) <= _iota((bk, bk), 1)).astype(CDT)
        tri_lt = (_iota((bk, bk), 0) < _iota((bk, bk), 1)).astype(CDT)

        def head(hh, dq_all):
            mine = lane_head == hh
            qs = jnp.where(mine, q2, jnp.zeros_like(q2))
            dov = jnp.where(mine, do2, jnp.zeros_like(do2))
            ltot_h = lt_ref[hh]

            def step(k0, carry, diag, r0=0):
                cf, cg, dq = carry
                kb = k_ref[pl.ds(k0, bk), :]
                vb = v_ref[pl.ds(k0, bk), :]
                causal, lb, lk = _sb_block(qs[r0:], kb, diag)
                w = jnp.exp(lb + ltot_h[r0:] - (cf + _split_dot(lk, tri_le, False, 2)))
                if causal is not None:
                    w = jnp.where(causal, w, 0.0)
                g = w * _dot(dov[r0:], vb, NT)
                gsum = cg + _split_dot(g, tri_lt, False, 2)
                dz = g - (g + gsum) * jnp.exp(lb)
                if causal is not None:
                    dz = jnp.where(causal, dz, 0.0)
                dzc = dz.astype(CDT)
                dk_ref[pl.ds(k0, bk), :] += _dot(dzc, qs[r0:], TN)
                dv_ref[pl.ds(k0, bk), :] += _dot(w, dov[r0:], TN)
                kbm = jnp.where(mine, kb, jnp.zeros_like(kb))
                return (cf + jnp.sum(lk, axis=1, keepdims=True), cg + jnp.sum(g, axis=1, keepdims=True),
                        dq + _dot(dzc, kbm))

            carry = (jnp.zeros((tq, 1), F32), jnp.zeros((tq, 1), F32), jnp.zeros((tq, SB_PAIR), F32))
            carry = lax.fori_loop(0, q0 // bk, lambda jj, c: step(pl.multiple_of(jj * bk, bk), c, None), carry)
            for dd in range(tq // bk):
                r0 = dd * bk
                sub = step(pl.multiple_of(q0 + r0, bk), tuple(t[r0:] for t in carry), 0, r0)
                carry = tuple(jnp.concatenate([t[:r0], u], axis=0) if r0 else u for t, u in zip(carry, sub))
            return dq_all + carry[2]

        dq_ref[...] = lax.fori_loop(0, 2, head, jnp.zeros((tq, SB_PAIR), F32)) * scale
        if nc:
            finish()

    tile = pl.BlockSpec((tq, SB_PAIR), lambda a, i: (i, a))
    acc = pl.BlockSpec((s, SB_PAIR), lambda a, i: (0, a))
    shp = jax.ShapeDtypeStruct((s, d), F32)
    return pl.pallas_call(
        body, name="sb_bwd", grid=(npair, s // tq),
        in_specs=[tile, pl.BlockSpec((s, SB_PAIR), lambda a, i: (0, npair + a)),
                  pl.BlockSpec((s, SB_PAIR), lambda a, i: (0, 2 * npair + a)),
                  pl.BlockSpec((2, tq, 1), lambda a, i: (a, i, 0)), tile] + [HBM] * nc,
        out_specs=[tile, acc, acc] + [HBM] * nc,
        out_shape=[shp, shp, shp] + _exchange_shapes(comm, True),
        scratch_shapes=_exchange_sems(nc) if nc else [],
        compiler_params=_cparams(("arbitrary", "arbitrary")),
    )(qkv, qkv, qkv, ltot, dy, *comm)


def _pick_lane(tile, r):
    return jnp.sum(jnp.where(_iota(tile.shape, 1) == r, tile, 0.0), axis=1, keepdims=True)


def _pick_row(tile, r):
    return jnp.sum(jnp.where(_iota(tile.shape, 0) == r, tile, 0.0), axis=0, keepdims=True)


def _ssd_chunk_common(c_ref, b_ref, dac_ref, dar_ref):
    l = SSD_L
    tdt = F32 if CDT == F32 else BF16
    lower = (_iota((l, l), 1) <= _iota((l, l), 0)).astype(tdt)
    upper = (_iota((l, l), 0) <= _iota((l, l), 1)).astype(tdt)
    cb = _dot(c_ref[...], b_ref[...], NT)
    return cb, _split_dot(dac_ref[...], lower, True, 3), _split_dot(dar_ref[...], upper, False, 3)


def _ssd_fwd_g(xh, dtc, dac, dar, dsk, xbc):
    hh, s, p = xh.shape
    l, n, g_, r_ = SSD_L, SSD_N, SSD_G, SSD_R
    nc = s // l
    boff = SSD_INNER // n
    coff = boff + g_

    def body(x_ref, dtc_ref, dac_ref, dar_ref, dsk_ref, b_ref, c_ref, y_ref, st_ref, state_ref):
        c = pl.program_id(1)

        @pl.when(c == 0)
        def _():
            state_ref[...] = jnp.zeros_like(state_ref)

        cb, acs_c, acs_r = _ssd_chunk_common(c_ref, b_ref, dac_ref, dar_ref)
        mask = _iota((l, l), 1) <= _iota((l, l), 0)
        cv, bv = c_ref[...], b_ref[...]
        dtcv, dskv = dtc_ref[...], dsk_ref[...]
        for r in range(r_):
            a_col = _pick_lane(acs_c, r)
            a_row = _pick_row(acs_r, r)
            dt_col = _pick_lane(dtcv, r)
            dsk_h = _pick_lane(dskv, r)
            xv = x_ref[r]
            xd = xv * dt_col
            decay = jnp.where(mask, jnp.exp(jnp.minimum(a_col - a_row, 0.0)), 0.0)
            hprev = state_ref[r]
            y = _dot(cb * decay, xd) + jnp.exp(a_col) * _dot(cv, hprev)
            y_ref[r] = y + dsk_h * xv
            a_end = a_col[l - 1:l, :]
            st_ref[r] = hprev
            state_ref[r] = hprev * jnp.exp(a_end) + _dot(bv, xd * jnp.exp(a_end - a_col), TN)

    return pl.pallas_call(
        body, name="ssd_fwd", grid=(g_, nc),
        in_specs=[pl.BlockSpec((r_, l, p), lambda g, c: (g, c, 0)),
                  pl.BlockSpec((None, l, r_), lambda g, c: (g, c, 0)),
                  pl.BlockSpec((None, l, r_), lambda g, c: (g, c, 0)),
                  pl.BlockSpec((None, r_, l), lambda g, c: (g, 0, c)),
                  pl.BlockSpec((None, 1, r_), lambda g, c: (g, 0, 0)),
                  pl.BlockSpec((l, n), lambda g, c: (c, boff + g)),
                  pl.BlockSpec((l, n), lambda g, c: (c, coff + g))],
        out_specs=[pl.BlockSpec((r_, l, p), lambda g, c: (g, c, 0)),
                   pl.BlockSpec((r_, None, n, p), lambda g, c: (g, c, 0, 0))],
        out_shape=[jax.ShapeDtypeStruct((hh, s, p), F32),
                   jax.ShapeDtypeStruct((hh, nc, n, p), F32)],
        scratch_shapes=[pltpu.VMEM((r_, n, p), F32)],
        compiler_params=_cparams(("parallel", "arbitrary")),
    )(xh, dtc, dac, dar, dsk, xbc, xbc)


def _ssd_bwd_g(xh, dtc, dac, dar, dsk, xbc, st, dy):
    hh, s, p = xh.shape
    l, n, g_, r_ = SSD_L, SSD_N, SSD_G, SSD_R
    nc = s // l
    boff = SSD_INNER // n
    coff = boff + g_

    def body(x_ref, dtc_ref, dac_ref, dar_ref, dsk_ref, b_ref, c_ref, st_ref, dy_ref,
             dx_ref, dda_ref, dxx_ref, db_ref, dc_ref, dah_ref, ddsk_ref, dstate_ref):
        c = pl.program_id(1)

        @pl.when(c == 0)
        def _():
            dstate_ref[...] = jnp.zeros_like(dstate_ref)
            dah_ref[...] = jnp.zeros_like(dah_ref)
            ddsk_ref[...] = jnp.zeros_like(ddsk_ref)

        cb, acs_c, acs_r = _ssd_chunk_common(c_ref, b_ref, dac_ref, dar_ref)
        il = _iota((l, l), 0)
        isx = _iota((l, l), 1)
        tdt = F32 if CDT == F32 else BF16
        t1 = (isx >= il).astype(tdt)
        cv, bv = c_ref[...], b_ref[...]
        dtcv, dskv = dtc_ref[...], dsk_ref[...]
        lane = _iota((l, r_), 1)
        lane1 = _iota((1, r_), 1)
        dda_all = jnp.zeros((l, r_), F32)
        dxx_all = jnp.zeros((l, r_), F32)
        dah_all = jnp.zeros((1, r_), F32)
        ddsk_all = jnp.zeros((1, r_), F32)
        db_acc = jnp.zeros((l, n), F32)
        dc_acc = jnp.zeros((l, n), F32)
        md_sum = jnp.zeros((l, l), F32)
        cb_t = _dot(bv, cv, NT)
        cv_t = cv.T
        for r in range(r_):
            a_col = _pick_lane(acs_c, r)
            a_row = _pick_row(acs_r, r)
            dt_col = _pick_lane(dtcv, r)
            dsk_h = _pick_lane(dskv, r)
            xv = x_ref[r]
            dyv = dy_ref[r]
            xd = xv * dt_col
            decay = jnp.where(isx <= il, jnp.exp(jnp.minimum(a_col - a_row, 0.0)), 0.0)
            decay_t = jnp.where(isx >= il, jnp.exp(jnp.minimum(a_row - a_col, 0.0)), 0.0)
            dhn = dstate_ref[r]
            hc = st_ref[r]
            a_end = a_col[l - 1:l, :]
            ea = jnp.exp(a_col)
            dte = jnp.exp(a_end - a_col)

            dx_state = dte * _dot(bv, dhn)
            dxd = _dot(cb_t * decay_t, dyv) + dx_state
            md = decay * _dot(dyv, xd, NT)
            md_sum = md_sum + md
            dc_acc = dc_acc + ea * _dot(dyv, hc, NT)
            db_acc = db_acc + dte * _dot(xd, dhn, NT)
            dstate_ref[r] = jnp.exp(a_end) * dhn + _dot(cv_t, dyv * ea)

            yoff = ea * _dot(cv, hc)
            xdx = jnp.sum(xd * dx_state, axis=1, keepdims=True)
            vec = jnp.sum(dyv * yoff, axis=1, keepdims=True) - xdx
            end_term = jnp.sum(xdx, axis=0, keepdims=True) + \
                jnp.exp(a_end) * jnp.sum(jnp.sum(hc * dhn, axis=1, keepdims=True), axis=0, keepdims=True)
            zmat = _split_dot(md * cb, t1, True, 2)
            span = jnp.sum(jnp.where(isx < il, zmat, 0.0), axis=1, keepdims=True)
            rc = _split_dot(jnp.broadcast_to(vec, (l, 128)), t1, True, 2)[:, :1]
            dda = span + rc + end_term
            dda_all = jnp.where(lane == r, dda, dda_all)
            dxx_all = jnp.where(lane == r, jnp.sum(dxd * xv, axis=1, keepdims=True), dxx_all)
            dx_ref[r] = dxd * dt_col + dsk_h * dyv
            dah_all = jnp.where(lane1 == r, jnp.sum(dda * dt_col, axis=0, keepdims=True), dah_all)
            ddsk_all = jnp.where(
                lane1 == r, jnp.sum(jnp.sum(dyv * xv, axis=1, keepdims=True), axis=0, keepdims=True), ddsk_all)
        dda_ref[...] = dda_all
        dxx_ref[...] = dxx_all
        db_ref[...] = db_acc + _dot(md_sum, cv, TN)
        dc_ref[...] = dc_acc + _dot(md_sum, bv)
        dah_ref[...] += dah_all
        ddsk_ref[...] += ddsk_all

    rev = lambda c: nc - 1 - c
    xspec = pl.BlockSpec((r_, l, p), lambda g, c: (g, rev(c), 0))
    cspec = pl.BlockSpec((None, l, r_), lambda g, c: (g, rev(c), 0))
    hspec = pl.BlockSpec((None, 1, r_), lambda g, c: (g, 0, 0))
    return pl.pallas_call(
        body, name="ssd_bwd", grid=(g_, nc),
        in_specs=[xspec, cspec, cspec,
                  pl.BlockSpec((None, r_, l), lambda g, c: (g, 0, rev(c))),
                  hspec,
                  pl.BlockSpec((l, n), lambda g, c: (rev(c), boff + g)),
                  pl.BlockSpec((l, n), lambda g, c: (rev(c), coff + g)),
                  pl.BlockSpec((r_, None, n, p), lambda g, c: (g, rev(c), 0, 0)),
                  xspec],
        out_specs=[xspec, cspec, cspec,
                   pl.BlockSpec((l, n), lambda g, c: (rev(c), g)),
                   pl.BlockSpec((l, n), lambda g, c: (rev(c), g)),
                   hspec, hspec],
        out_shape=[jax.ShapeDtypeStruct((hh, s, p), F32),
                   jax.ShapeDtypeStruct((g_, s, r_), F32),
                   jax.ShapeDtypeStruct((g_, s, r_), F32),
                   jax.ShapeDtypeStruct((s, g_ * n), F32),
                   jax.ShapeDtypeStruct((s, g_ * n), F32),
                   jax.ShapeDtypeStruct((g_, 1, r_), F32),
                   jax.ShapeDtypeStruct((g_, 1, r_), F32)],
        scratch_shapes=[pltpu.VMEM((r_, n, p), F32)],
        compiler_params=_cparams(("parallel", "arbitrary")),
    )(xh, dtc, dac, dar, dsk, xbc, xbc, st, dy)


CONV_TC = 256
CONV_RC = 512


def _conv_taps(x_ref, head_ref, rc):
    base = CONV_PAD - (CONV_K - 1)
    head_ref[pl.ds(0, CONV_PAD), :] = jnp.zeros((CONV_PAD, head_ref.shape[1]), F32)
    head_ref[pl.ds(CONV_PAD, rc), :] = x_ref[pl.ds(0, rc), :]

    def tap(t0, kk):
        if t0 == 0:
            return head_ref[pl.ds(base + kk, rc), :]
        return x_ref[pl.ds(t0 - (CONV_K - 1) + kk, rc), :]
    return tap


def _conv_fwd(x, w, b):
    s, ch = x.shape
    rc = min(CONV_RC, s)

    def body(x_ref, w_ref, b_ref, pre_ref, act_ref, head_ref):
        wv = w_ref[...]
        tap = _conv_taps(x_ref, head_ref, rc)
        for t0 in range(0, s, rc):
            acc = jnp.broadcast_to(b_ref[...], (rc, CONV_TC))
            for kk in range(CONV_K):
                acc = acc + wv[kk:kk + 1, :] * tap(t0, kk)
            pre_ref[pl.ds(t0, rc), :] = acc
            act_ref[pl.ds(t0, rc), :] = _silu(acc)

    col = pl.BlockSpec((s, CONV_TC), lambda j: (0, j))
    shp = jax.ShapeDtypeStruct((s, ch), F32)
    return pl.pallas_call(
        body, name="conv_fwd", grid=(ch // CONV_TC,),
        in_specs=[col, pl.BlockSpec((CONV_K, CONV_TC), lambda j: (0, j)),
                  pl.BlockSpec((1, CONV_TC), lambda j: (0, j))],
        out_specs=[col, col],
        out_shape=[shp, shp],
        scratch_shapes=[pltpu.VMEM((CONV_PAD + rc, CONV_TC), F32)],
        compiler_params=_cparams(("parallel",)),
    )(x, w, b)


def _conv_bwd(x, pre, dact, w):
    s, ch = x.shape
    rc = min(CONV_RC, s)

    def body(x_ref, pre_ref, da_ref, w_ref, dx_ref, dw_ref, db_ref, dpre_ref, head_ref):
        wv = w_ref[...]
        tap = _conv_taps(x_ref, head_ref, rc)
        for t0 in range(0, s, rc):
            dpre_ref[pl.ds(t0, rc), :] = da_ref[pl.ds(t0, rc), :] * _dsilu(pre_ref[pl.ds(t0, rc), :])
        dpre_ref[pl.ds(s, CONV_PAD), :] = jnp.zeros((CONV_PAD, CONV_TC), F32)
        dws = [jnp.zeros((1, CONV_TC), F32) for _ in range(CONV_K)]
        dbs = jnp.zeros((1, CONV_TC), F32)
        for t0 in range(0, s, rc):
            acc = jnp.zeros((rc, CONV_TC), F32)
            dp = dpre_ref[pl.ds(t0, rc), :]
            for kk in range(CONV_K):
                acc = acc + wv[kk:kk + 1, :] * dpre_ref[pl.ds(t0 + CONV_K - 1 - kk, rc), :]
                dws[kk] = dws[kk] + jnp.sum(dp * tap(t0, kk), axis=0, keepdims=True)
            dbs = dbs + jnp.sum(dp, axis=0, keepdims=True)
            dx_ref[pl.ds(t0, rc), :] = acc.astype(dx_ref.dtype)
        for kk in range(CONV_K):
            dw_ref[kk:kk + 1, :] = dws[kk]
        db_ref[...] = dbs

    col = pl.BlockSpec((s, CONV_TC), lambda j: (0, j))
    return pl.pallas_call(
        body, name="conv_bwd", grid=(ch // CONV_TC,),
        in_specs=[col, col, col, pl.BlockSpec((CONV_K, CONV_TC), lambda j: (0, j))],
        out_specs=[col, pl.BlockSpec((CONV_K, CONV_TC), lambda j: (0, j)),
                   pl.BlockSpec((1, CONV_TC), lambda j: (0, j))],
        out_shape=[jax.ShapeDtypeStruct((s, ch), CDT),
                   jax.ShapeDtypeStruct((CONV_K, ch), F32),
                   jax.ShapeDtypeStruct((1, ch), F32)],
        scratch_shapes=[pltpu.VMEM((s + CONV_PAD, CONV_TC), F32), pltpu.VMEM((CONV_PAD + rc, CONV_TC), F32)],
        compiler_params=_cparams(("parallel",)),
    )(x, pre, dact, w)


MEM_TS = 512


def _mem_fwd(mq, kv):
    s = mq.shape[0]
    m = kv.shape[0]
    ts = min(MEM_TS, s)
    scale = MEM_DH ** -0.5

    def body(q_ref, k_ref, v_ref, o_ref):
        sc = _dot(q_ref[...], k_ref[...], NT) * scale
        e = jnp.exp(sc - jnp.max(sc, axis=1, keepdims=True))
        pr = e / jnp.sum(e, axis=1, keepdims=True)
        o_ref[...] = _dot(pr, v_ref[...]).astype(o_ref.dtype)

    return pl.pallas_call(
        body, name="mem_fwd", grid=(MEM_H, s // ts),
        in_specs=[pl.BlockSpec((ts, MEM_DH), lambda a, i: (i, a)),
                  pl.BlockSpec((m, MEM_DH), lambda a, i: (0, a)),
                  pl.BlockSpec((m, MEM_DH), lambda a, i: (0, MEM_H + a))],
        out_specs=pl.BlockSpec((ts, MEM_DH), lambda a, i: (i, a)),
        out_shape=jax.ShapeDtypeStruct((s, MEM_H * MEM_DH), CDT),
        compiler_params=_cparams(("parallel", "arbitrary")),
    )(mq, kv, kv)


def _mem_bwd(mq, kv, do):
    s = mq.shape[0]
    m = kv.shape[0]
    ts = min(MEM_TS, s)
    scale = MEM_DH ** -0.5

    def body(q_ref, k_ref, v_ref, do_ref, dq_ref, dk_ref, dv_ref):
        i = pl.program_id(1)

        @pl.when(i == 0)
        def _():
            dk_ref[...] = jnp.zeros_like(dk_ref)
            dv_ref[...] = jnp.zeros_like(dv_ref)

        qv, kb, vb, dov = q_ref[...], k_ref[...], v_ref[...], do_ref[...]
        sc = _dot(qv, kb, NT) * scale
        e = jnp.exp(sc - jnp.max(sc, axis=1, keepdims=True))
        pr = e / jnp.sum(e, axis=1, keepdims=True)
        dp = _dot(dov, vb, NT)
        ds = pr * (dp - jnp.sum(dp * pr, axis=1, keepdims=True)) * scale
        dq_ref[...] = _dot(ds, kb).astype(dq_ref.dtype)
        dk_ref[...] += _dot(ds, qv, TN)
        dv_ref[...] += _dot(pr, dov, TN)

    tile = pl.BlockSpec((ts, MEM_DH), lambda a, i: (i, a))
    kvo = pl.BlockSpec((m, MEM_DH), lambda a, i: (0, a))
    return pl.pallas_call(
        body, name="mem_bwd", grid=(MEM_H, s // ts),
        in_specs=[tile, kvo, pl.BlockSpec((m, MEM_DH), lambda a, i: (0, MEM_H + a)), tile],
        out_specs=[tile, kvo, kvo],
        out_shape=[jax.ShapeDtypeStruct((s, MEM_H * MEM_DH), CDT),
                   jax.ShapeDtypeStruct((m, MEM_H * MEM_DH), F32),
                   jax.ShapeDtypeStruct((m, MEM_H * MEM_DH), F32)],
        compiler_params=_cparams(("parallel", "arbitrary")),
    )(mq, kv, kv, do)


def _heads(t, nh, dh):
    return t.reshape(t.shape[0], nh, dh).transpose(1, 0, 2)


def _unheads(t):
    return t.transpose(1, 0, 2).reshape(t.shape[1], t.shape[0] * t.shape[2])


def _group_cols(t):
    return t.reshape(t.shape[0], SSD_G, SSD_R).transpose(1, 0, 2)


def _pad_cols(t, width):
    return jnp.pad(t, ((0, 0), (0, width - t.shape[1])))


def _full_weight(name, gathered):
    if name in COL_SHARDED:
        return gathered.transpose(1, 0, 2).reshape(gathered.shape[1], N_DEV * gathered.shape[2])
    return gathered.reshape(N_DEV * gathered.shape[1], gathered.shape[2])


def _grad_payload(name, g):
    if name in COL_SHARDED:
        return g.reshape(g.shape[0], N_DEV, g.shape[1] // N_DEV).transpose(1, 0, 2)
    return g.reshape(N_DEV, g.shape[0] // N_DEV, g.shape[1])


def _local_step(x, mem, tgt, p, wt, shards=None):
    s, d = x.shape
    wt = dict(wt)
    c1, c2, c3, c4, c5 = 3 * d, 3 * d + SSD_INNER, 3 * d + SSD_INNER + CONV_DIM, \
        3 * d + SSD_INNER + CONV_DIM + SSD_H, 3 * d + SSD_INNER + CONV_DIM + SSD_H + d
    w_t = wt["w_in"]
    w_tail = w_t[c4:]
    w_dt = jnp.pad(w_t[c3:c4], ((0, DT_PAD - SSD_H), (0, 0)))
    seg_name = ["qkv", "z", "xbc", "mq", "gl"]
    seg_dtype = [CDT, F32, F32, CDT, F32]
    seg_src = [w_t, w_t, w_t, w_tail, w_tail]
    seg_off = [0, c1, c2, 0, d]
    seg_n = [c1, c2 - c1, c3 - c2, d, 3 * d]

    u = _rows(lambda xv, g: _rms(xv, g), [x], [p["norm_mix_pre"]], [(d, CDT)], ts=512, name="f_norm_pre")[0]
    qkv, z, xbc_raw, mq, gl = [
        _mm(u, seg_src[i], tb=True, b_off=seg_off[i], n=seg_n[i], out_dtype=seg_dtype[i],
            name="f_in_" + seg_name[i]) for i in range(5)]
    dt_raw = _mm(u, w_dt, tb=True, name="f_in_dt")

    bias128 = _pad_cols(p["dt_bias"], DT_PAD)
    alog128 = _pad_cols(p["a_log"], DT_PAD)

    def dt_fn(dtr, bias, alog):
        dt = _softplus(dtr + bias)
        return dt, dt * (-jnp.exp(alog))

    dt128, da128 = _rows(dt_fn, [dt_raw], [bias128, alog128], [(DT_PAD, F32), (DT_PAD, F32)],
                         ts=512, name="f_dt")
    dtc = _group_cols(dt128[:, :SSD_H])
    dac = _group_cols(da128[:, :SSD_H])
    dar = dac.transpose(0, 2, 1)
    dsk = p["d_skip"].reshape(SSD_G, 1, SSD_R)

    conv_w, conv_b = p["conv_w"], p["conv_b"]
    pre, xbc = _conv_fwd(xbc_raw, conv_w, conv_b)
    xh = _heads(xbc[:, :SSD_INNER], SSD_H, SSD_P)
    y_h, st = _ssd_fwd_g(xh, dtc, dac, dar, dsk, xbc)
    y_core = _unheads(y_h)

    def group_norm_fwd(yv, zv, wn):
        y2 = yv * _silu(zv)
        gw = SSD_INNER // SSD_G
        outs = []
        for gi in range(SSD_G):
            seg = y2[:, gi * gw:(gi + 1) * gw]
            outs.append(_rms(seg, wn[:, gi * gw:(gi + 1) * gw]))
        return jnp.concatenate(outs, axis=1)

    y_ssd = _rows(group_norm_fwd, [y_core, z], [p["ssd_norm"]], [(SSD_INNER, CDT)], ts=256, name="f_ssd_post")[0]

    y_sb, lt_h, *late = _sb_fwd_pairs(qkv, tuple(shards) if shards is not None else ())
    for n, gth in zip(LATE_W, late):
        wt[n] = _full_weight(n, gth)

    mu = _rows(lambda mv, g: _rms(mv, g), [mem], [p["norm_mem"]], [(d, CDT)], ts=256, name="f_norm_mem")[0]
    kv = _mm(mu, wt["w_mem_kv"], out_dtype=CDT, name="f_mem_kv")
    y_mem = _mem_fwd(mq, kv)

    p_sb = _mm(y_sb, wt["w_sb_out"], name="f_sb_out")
    p_ssd = _mm(y_ssd, wt["w_ssd_out"], name="f_ssd_out")
    p_mem = _mm(y_mem, wt["w_mem_out"], name="f_mem_out")

    def merge_fn(glv, a, b, c):
        return (_sigmoid(glv[:, :d]) * a + _sigmoid(glv[:, d:2 * d]) * b + _sigmoid(glv[:, 2 * d:]) * c)

    merged = _rows(merge_fn, [gl, p_sb, p_ssd, p_mem], [], [(d, CDT)], ts=256, name="f_merge")[0]
    mix = _mm(merged, wt["w_o"], name="f_w_o")

    def mid_fn(xv, mixv, g_post, g_pre):
        h1 = xv + _rms(mixv, g_post)
        return h1, _rms(h1, g_pre)

    h1, u2 = _rows(mid_fn, [x, mix], [p["norm_mix_post"], p["norm_mlp_pre"]], [(d, F32), (d, CDT)],
                   ts=512, name="f_mid")
    a1, act = _mm(u2, wt["w_up"], name="f_up",
                  epilogue=(lambda pv: (pv, jnp.square(jnp.maximum(pv, 0.0))), [], [F32, CDT]))
    ff = _mm(act, wt["w_down"], name="f_down")

    def loss_fn(h1v, ffv, tv, g):
        diff = h1v + _rms(ffv, g) - tv
        tot = jnp.sum(jnp.sum(diff * diff, axis=1, keepdims=True), axis=0, keepdims=True)
        return diff * (1.0 / d), tot

    dh2, loss_acc = _rows(loss_fn, [h1, ff, tgt], [p["norm_mlp_post"]], [(d, F32)], [(1, 128)],
                          ts=512, name="f_loss")
    loss = loss_acc[:, :1] * (0.5 / d)

    sg = {}

    def b_post(ffv, dyv, g):
        dx, dg = _rms_bwd(ffv, g, dyv)
        return dx, dg

    d_ff, sg["norm_mlp_post"] = _rows(b_post, [ff, dh2], [p["norm_mlp_post"]], [(d, CDT)], [(1, d)],
                                      ts=512, name="b_norm_mlp_post")
    da1 = _mm(d_ff, wt["w_down"], tb=True, name="b_down_x",
              epilogue=(lambda pv, a: (pv * 2.0 * jnp.maximum(a, 0.0),), [a1], [CDT]))[0]
    gw = {"w_down": _mm(act, d_ff, ta=True, name="b_down_w")}
    du2 = _mm(da1, wt["w_up"], tb=True, name="b_up_x")
    gw["w_up"] = _mm(u2, da1, ta=True, name="b_up_w")

    def b_mid(h1v, du2v, dh2v, mixv, g_pre, g_post):
        dxa, dga = _rms_bwd(h1v, g_pre, du2v)
        dh1 = dh2v + dxa
        dmix, dgb = _rms_bwd(mixv, g_post, dh1)
        return dh1, dmix, dga, dgb

    dh1, dmix, sg["norm_mlp_pre"], sg["norm_mix_post"] = _rows(
        b_mid, [h1, du2, dh2, mix], [p["norm_mlp_pre"], p["norm_mix_post"]],
        [(d, F32), (d, CDT)], [(1, d), (1, d)], ts=256, name="b_mid")
    dmerged = _mm(dmix, wt["w_o"], tb=True, name="b_w_o_x")
    gw["w_o"] = _mm(merged, dmix, ta=True, name="b_w_o_w")

    def b_merge(dm, glv, a, b, c):
        outs, dgl = [], []
        for i, br in enumerate((a, b, c)):
            gt = _sigmoid(glv[:, i * d:(i + 1) * d])
            outs.append(gt * dm)
            dgl.append(dm * br * gt * (1.0 - gt))
        return outs[0], outs[1], outs[2], jnp.concatenate(dgl, axis=1)

    dp_sb, dp_ssd, dp_mem, dgl = _rows(b_merge, [dmerged, gl, p_sb, p_ssd, p_mem], [],
                                       [(d, CDT), (d, CDT), (d, CDT), (3 * d, CDT)], ts=256, name="b_merge")
    dy_sb = _mm(dp_sb, wt["w_sb_out"], tb=True, name="b_sb_out_x")
    gw["w_sb_out"] = _mm(y_sb, dp_sb, ta=True, name="b_sb_out_w")
    dy_ssd = _mm(dp_ssd, wt["w_ssd_out"], tb=True, name="b_ssd_out_x")
    gw["w_ssd_out"] = _mm(y_ssd, dp_ssd, ta=True, name="b_ssd_out_w")
    dy_mem = _mm(dp_mem, wt["w_mem_out"], tb=True, out_dtype=CDT, name="b_mem_out_x")
    gw["w_mem_out"] = _mm(y_mem, dp_mem, ta=True, name="b_mem_out_w")

    dmq, dk_m, dv_m = _mem_bwd(mq, kv, dy_mem)
    dkv = jnp.concatenate([dk_m, dv_m], axis=1).astype(CDT)
    gw["w_mem_kv"] = _mm(mu, dkv, ta=True, name="b_mem_kv_w")
    dmu = _mm(dkv, wt["w_mem_kv"], tb=True, name="b_mem_kv_x")
    sg["norm_mem"] = _rows(lambda mv, dv, g: _rms_bwd(mv, g, dv)[1], [mem, dmu], [p["norm_mem"]], [], [(1, d)],
                           ts=256, name="b_norm_mem")[0]

    payloads = tuple(_grad_payload(n, gw[n]) for n in LATE_W) if shards is not None else ()
    dq, dk, dv, *received = _sb_bwd_pairs(qkv, lt_h, dy_sb, payloads)
    dqkv = jnp.concatenate([dq, dk, dv], axis=1).astype(CDT)

    def group_norm_bwd(dyo, yv, zv, wn):
        sz = _silu(zv)
        y2 = yv * sz
        gw_ = SSD_INNER // SSD_G
        dy2, dwn = [], []
        for gi in range(SSD_G):
            sl = slice(gi * gw_, (gi + 1) * gw_)
            dxs, dgs = _rms_bwd(y2[:, sl], wn[:, sl], dyo[:, sl])
            dy2.append(dxs)
            dwn.append(dgs)
        dy2 = jnp.concatenate(dy2, axis=1)
        return dy2 * sz, dy2 * yv * _dsilu(zv), jnp.concatenate(dwn, axis=1)

    dy_core, dz, sg["ssd_norm"] = _rows(group_norm_bwd, [dy_ssd, y_core, z], [p["ssd_norm"]],
                                        [(SSD_INNER, F32), (SSD_INNER, CDT)], [(1, SSD_INNER)],
                                        ts=256, name="b_ssd_post")
    dxh, dda, dxx, d_b, d_c, dah, ddsk = _ssd_bwd_g(xh, dtc, dac, dar, dsk, xbc, st, _heads(dy_core, SSD_H, SSD_P))
    sg["d_skip"] = ddsk.reshape(1, SSD_H)
    sg["a_log"] = dah.reshape(1, SSD_H) * (-jnp.exp(p["a_log"]))

    def b_dt(ddav, dxxv, dtr, bias, alog):
        ddt = ddav * (-jnp.exp(alog)) + dxxv
        draw = ddt * _sigmoid(dtr + bias)
        return draw, jnp.sum(draw, axis=0, keepdims=True)

    ungroup = lambda t: _pad_cols(t.transpose(1, 0, 2).reshape(s, SSD_H), DT_PAD)
    ddt_raw, dbias128 = _rows(b_dt, [ungroup(dda), ungroup(dxx), dt_raw], [bias128, alog128],
                              [(DT_PAD, CDT)], [(1, DT_PAD)], ts=512, name="b_dt")
    sg["dt_bias"] = dbias128[:, :SSD_H]

    dxbc = jnp.concatenate([_unheads(dxh), d_b, d_c], axis=1)
    dxbc_raw, sg["conv_w"], sg["conv_b"] = _conv_bwd(xbc_raw, pre, dxbc, conv_w)

    dseg = [dqkv, dz, dxbc_raw, dmq, dgl]
    dw_bufs = [lax.empty((c3, d), F32), lax.empty((c5 - c4 + 3 * d, d), F32)]
    for i in range(5):
        bi = 0 if i < 3 else 1
        dw_bufs[bi] = _mm(dseg[i], u, ta=True, into=dw_bufs[bi], into_off=seg_off[i], into_rows=True,
                          name="b_in_w_" + seg_name[i])
    dw_dt = _mm(ddt_raw, u, ta=True, name="b_in_w_dt")
    def du(i, hosted=None):
        return _mm(dseg[i], seg_src[i], b_koff=seg_off[i], name="b_in_x_" + seg_name[i], hosted=hosted)

    if shards is None:
        dus = [du(i) for i in range(5)]
        gw["w_in"] = jnp.concatenate([dw_bufs[0], dw_dt[:SSD_H], dw_bufs[1]], axis=0).T
    else:
        du0, recv_head = du(0, _pair_swap([dw_bufs[0]]))
        du2, recv_tail, recv_dt = du(2, _pair_swap([dw_bufs[1], dw_dt]))
        tail_rows = dw_bufs[1].shape[0]
        natural = lax.empty((c4 + tail_rows, d), CDT)
        natural = _add_cast_into(dw_bufs[0], recv_head, natural, 0, c3, "sum_w_in_head")
        natural = _add_cast_into(dw_dt, recv_dt, natural, c3, SSD_H, "sum_w_in_dt")
        natural = _add_cast_into(dw_bufs[1], recv_tail, natural, c4, tail_rows, "sum_w_in_tail")
        shard = natural.shape[0] // N_DEV
        du1, win_a = du(1, _chip_scatter_windows(natural, shard, 0, d // 2))
        du4, win_b = du(4, _chip_scatter_windows(natural, shard, d // 2, d // 2))
        gw["w_in"] = [win_a, win_b]
        dus = [du0, du1, du2, du(3), du4]
    dus.append(_mm(ddt_raw, w_dt, name="b_in_x_dt"))

    def b_pre(xv, dh1v, d0, d1, d2, d3, d4, d5, g):
        dx, dg = _rms_bwd(xv, g, d0 + d1 + d2 + d3 + d4 + d5)
        return dh1v + dx, dg

    grad_x, sg["norm_mix_pre"] = _rows(b_pre, [x, dh1] + dus, [p["norm_mix_pre"]], [(d, F32)], [(1, d)],
                                       ts=256, name="b_norm_pre")
    return loss, grad_x, gw, sg, (received if shards is not None else None)


HBM = pl.BlockSpec(memory_space=pltpu.HBM)
MESH = pl.DeviceIdType.MESH


def _me_and_peers():
    x, y, c = lax.axis_index("x"), lax.axis_index("y"), lax.axis_index("c")
    me = 4 * x + 2 * y + c
    peers = [(x, y, 1 - c), (1 - x, y, c), (x, 1 - y, c), (1 - x, 1 - y, c),
             (1 - x, y, 1 - c), (x, 1 - y, 1 - c), (1 - x, 1 - y, 1 - c)]
    return me, peers


def _peer_index(peer):
    return 4 * peer[0] + 2 * peer[1] + peer[2]


def _exchange_copies(ins, outs, send_sems, recv_sems, local_sems, scatter):
    me, peers = _me_and_peers()
    copies = []
    for a in range(len(ins)):
        own = ins[a].at[me] if scatter else ins[a]
        copies.append(pltpu.make_async_copy(own, outs[a].at[me], local_sems.at[a]))
        for kk, peer in enumerate(peers):
            src = ins[a].at[_peer_index(peer)] if scatter else ins[a]
            copies.append(pltpu.make_async_remote_copy(
                src_ref=src, dst_ref=outs[a].at[me],
                send_sem=send_sems.at[a, kk], recv_sem=recv_sems.at[a, kk],
                device_id=peer, device_id_type=MESH))
    return copies


def _exchange_shapes(ins, scatter):
    return [jax.ShapeDtypeStruct(t.shape if scatter else (N_DEV,) + t.shape, t.dtype) for t in ins]


def _exchange_sems(n):
    return [pltpu.SemaphoreType.DMA((n, N_DEV - 1)), pltpu.SemaphoreType.DMA((n, N_DEV - 1)),
            pltpu.SemaphoreType.DMA((n,))]


def _gather_two_level(shards, name):
    n = len(shards)

    def body(*refs):
        ins, outs = refs[:n], refs[n:2 * n]
        send_sems, recv_sems, local_sems = refs[2 * n:]
        x, y, c = lax.axis_index("x"), lax.axis_index("y"), lax.axis_index("c")
        me, sib = (x, y, c), (x, y, 1 - c)
        chips = [(1 - x, y), (x, 1 - y), (1 - x, 1 - y)]

        def copy(a, k, block, to, src=None):
            slot = outs[a].at[_peer_index(block)]
            return pltpu.make_async_remote_copy(
                src_ref=slot if src is None else src, dst_ref=slot,
                send_sem=send_sems.at[a, k], recv_sem=recv_sems.at[a, k], device_id=to, device_id_type=MESH)

        own = [pltpu.make_async_copy(ins[a], outs[a].at[_peer_index(me)], local_sems.at[a]) for a in range(n)]
        first = []
        for a in range(n):
            first.append(copy(a, 0, me, sib, src=ins[a]))
            first += [copy(a, 1 + j, me, (*chip, c), src=ins[a]) for j, chip in enumerate(chips)]
        for cp in own + first:
            cp.start()
        passed = []
        for j, chip in enumerate(chips):
            for a in range(n):
                copy(a, 1 + j, (*chip, c), me).wait_recv()
                fwd = copy(a, 4 + j, (*chip, c), sib)
                fwd.start()
                passed.append(fwd)
        for a in range(n):
            copy(a, 0, sib, me).wait_recv()
            for j, chip in enumerate(chips):
                copy(a, 4 + j, (*chip, 1 - c), me).wait_recv()
        for cp in first + passed:
            cp.wait_send()
        for cp in own:
            cp.wait()

    return pl.pallas_call(
        body, name=name,
        in_specs=[HBM] * n, out_specs=[HBM] * n,
        out_shape=_exchange_shapes(shards, False),
        scratch_shapes=_exchange_sems(n),
        compiler_params=pltpu.CompilerParams(has_side_effects=True),
    )(*shards)


class _Hosted:
    def __init__(self, ins, shapes, sems, copies):
        self.ins, self.shapes, self.sems, self.copies = ins, shapes, sems, copies


def _pair_swap(arrays):
    n = len(arrays)

    def copies(in_refs, out_refs, sems):
        sib = (lax.axis_index("x"), lax.axis_index("y"), 1 - lax.axis_index("c"))
        return [pltpu.make_async_remote_copy(
            src_ref=in_refs[i], dst_ref=out_refs[i], send_sem=sems[0].at[i], recv_sem=sems[1].at[i],
            device_id=sib, device_id_type=MESH) for i in range(n)]

    return _Hosted(list(arrays), [jax.ShapeDtypeStruct(t.shape, t.dtype) for t in arrays],
                   [pltpu.SemaphoreType.DMA((n,)), pltpu.SemaphoreType.DMA((n,))], copies)


def _add_cast_into(a, b, into, row_off, rows, name):
    w = a.shape[1]
    tr = _pick(rows, (512, 256, 128, rows))
    nt = rows // tr
    assert rows % tr == 0 and row_off % ROW_ALIGN == 0 and tr % ROW_ALIGN == 0
    spec = pl.BlockSpec((tr, w), lambda i: (i, 0))

    def body(a_ref, b_ref, into_ref, o_ref, slots, sems):
        i = pl.program_id(0)
        slot = i % 2

        def out_copy(step, s):
            dst = o_ref.at[pl.ds(pl.multiple_of(row_off + step * tr, ROW_ALIGN), tr)]
            return pltpu.make_async_copy(slots.at[s], dst, sems.at[s])

        @pl.when(i >= 2)
        def _():
            out_copy(i - 2, slot).wait()

        slots[slot] = (a_ref[...] + b_ref[...]).astype(slots.dtype)
        out_copy(i, slot).start()

        @pl.when(i == nt - 1)
        def _():
            out_copy(i, slot).wait()
            if nt > 1:
                out_copy(i - 1, 1 - slot).wait()

    return pl.pallas_call(
        body, name=name, grid=(nt,),
        in_specs=[spec, spec, pl.BlockSpec(memory_space=pl.ANY)],
        out_specs=pl.BlockSpec(memory_space=pl.ANY),
        out_shape=jax.ShapeDtypeStruct(into.shape, into.dtype),
        scratch_shapes=[pltpu.VMEM((2, tr, w), into.dtype), pltpu.SemaphoreType.DMA((2,))],
        input_output_aliases={2: 0},
        compiler_params=_cparams(("arbitrary",)),
    )(a, b, into)


N_CHIP = 4


def _adamw_math(g, w, m, v):
    m2 = ADAM_B1 * m + (1.0 - ADAM_B1) * g
    v2 = ADAM_B2 * v + (1.0 - ADAM_B2) * jnp.square(g)
    m_hat = m2 / (1.0 - ADAM_B1 ** ADAM_STEP)
    v_hat = v2 / (1.0 - ADAM_B2 ** ADAM_STEP)
    delta = -ADAM_LR * (m_hat / (jnp.sqrt(v_hat) + ADAM_EPS) + ADAM_WD * w)
    return delta, m2, v2


def _adamw_reduce(parts, w, m, v, name):
    r, c = w.shape
    nparts = parts.shape[0]
    tr = _pick(r, (128, 64, 32, 16, 8))

    def body(p_ref, w_ref, m_ref, v_ref, g_ref, d_ref, m2_ref, v2_ref):
        g = p_ref[0].astype(F32)
        for i in range(1, nparts):
            g = g + p_ref[i].astype(F32)
        delta, m2, v2 = _adamw_math(g, w_ref[...], m_ref[...], v_ref[...])
        g_ref[...] = g
        d_ref[...] = delta
        m2_ref[...] = m2
        v2_ref[...] = v2

    tile = pl.BlockSpec((tr, c), lambda i: (i, 0))
    shp = jax.ShapeDtypeStruct((r, c), F32)
    return pl.pallas_call(
        body, name=name, grid=(r // tr,),
        in_specs=[pl.BlockSpec((nparts, tr, c), lambda i: (0, i, 0)), tile, tile, tile],
        out_specs=[tile] * 4, out_shape=[shp] * 4,
        compiler_params=_cparams(("parallel",)),
    )(parts, w, m, v)


ROW_ALIGN = 16


def _window(shard):
    lead = max((j * shard) % ROW_ALIGN for j in range(N_DEV))
    return -(-(lead + shard) // ROW_ALIGN) * ROW_ALIGN


def _chip_scatter_windows(t, shard, col0, cols):
    win = _window(shard)
    assert all((j * shard // ROW_ALIGN) * ROW_ALIGN + win <= t.shape[0] for j in range(N_DEV))

    def copies(in_refs, out_refs, sems):
        (t_ref,), (o_ref,), (send_sems, recv_sems, local_sem) = in_refs, out_refs, sems
        x, y, c = lax.axis_index("x"), lax.axis_index("y"), lax.axis_index("c")
        mine = 2 * x + y

        def window(q):
            a0 = pl.multiple_of(((2 * q + c) * shard // ROW_ALIGN) * ROW_ALIGN, ROW_ALIGN)
            return t_ref.at[pl.ds(a0, win), pl.ds(col0, cols)]

        res = [pltpu.make_async_copy(window(mine), o_ref.at[mine], local_sem)]
        for j, (px, py) in enumerate([(1 - x, y), (x, 1 - y), (1 - x, 1 - y)]):
            res.append(pltpu.make_async_remote_copy(
                src_ref=window(2 * px + py), dst_ref=o_ref.at[mine],
                send_sem=send_sems.at[j], recv_sem=recv_sems.at[j], device_id=(px, py, c), device_id_type=MESH))
        return res

    return _Hosted([t], [jax.ShapeDtypeStruct((N_CHIP, win, cols), t.dtype)],
                   [pltpu.SemaphoreType.DMA((N_CHIP - 1,)), pltpu.SemaphoreType.DMA((N_CHIP - 1,)),
                    pltpu.SemaphoreType.DMA], copies)


ADAMW_TC = 256


def _adamw_windows(parts, w, m, v, name):
    r, c = w.shape
    na = len(parts)
    nparts, win, cpart = parts[0].shape
    tc = min(ADAMW_TC, cpart)
    per = cpart // tc

    def body(*refs):
        p_refs = refs[:na]
        w_ref, m_ref, v_ref, g_ref, d_ref, m2_ref, v2_ref = refs[na:]
        me, _ = _me_and_peers()
        step = pl.program_id(0)
        gw_ = None
        for a, p_ref in enumerate(p_refs):
            tot = p_ref[0].astype(F32)
            for i in range(1, nparts):
                tot = tot + p_ref[i].astype(F32)
            gw_ = tot if gw_ is None else jnp.where(step // per == a, tot, gw_)
        for j in range(N_DEV):
            @pl.when(me == j)
            def _():
                lead = (j * r) % ROW_ALIGN
                g = (pltpu.roll(gw_, win - lead, 0) if lead else gw_)[:r]
                delta, m2, v2 = _adamw_math(g, w_ref[...], m_ref[...], v_ref[...])
                g_ref[...] = g
                d_ref[...] = delta
                m2_ref[...] = m2
                v2_ref[...] = v2

    tile = pl.BlockSpec((r, tc), lambda i: (0, i))
    shp = jax.ShapeDtypeStruct((r, c), F32)
    return pl.pallas_call(
        body, name=name, grid=(c // tc,),
        in_specs=[pl.BlockSpec((nparts, win, tc), lambda i, a=a: (0, 0, jnp.clip(i - a * per, 0, per - 1)))
                  for a in range(na)] + [tile, tile, tile],
        out_specs=[tile] * 4, out_shape=[shp] * 4,
        compiler_params=_cparams(("parallel",)),
    )(*parts, w, m, v)


SMALL_ROWS, SMALL_COLS = 16, 3072


def _small_step(sg, gcw, loss, ws, ms, vs):
    ns = len(sg)
    widths = [t.shape[1] for t in sg]
    kk_, ch = gcw.shape[1], gcw.shape[2]
    assert ns < SMALL_ROWS and max(widths) <= SMALL_COLS

    def reduce_body(*refs):
        g_refs = refs[:ns]
        gcw_ref, loss_ref, tot_ref, totc_ref = refs[ns:ns + 4]
        mine, buf, minec, bufc, send_sems, recv_sems = refs[ns + 4:]
        me, peers = _me_and_peers()

        mine[...] = jnp.zeros_like(mine)
        for i in range(ns):
            mine[i:i + 1, 0:widths[i]] = g_refs[i][...]
        mine[ns:ns + 1, 0:LANES] = jnp.broadcast_to(loss_ref[...], (1, LANES))
        minec[...] = gcw_ref[...]
        buf[me] = mine[...]
        bufc[me] = minec[...]
        copies = []
        for j, peer in enumerate(peers):
            copies.append(pltpu.make_async_remote_copy(
                src_ref=mine, dst_ref=buf.at[me], send_sem=send_sems.at[0, j], recv_sem=recv_sems.at[0, j],
                device_id=peer, device_id_type=MESH))
            copies.append(pltpu.make_async_remote_copy(
                src_ref=minec, dst_ref=bufc.at[me], send_sem=send_sems.at[1, j], recv_sem=recv_sems.at[1, j],
                device_id=peer, device_id_type=MESH))
        for cp in copies:
            cp.start()
        for cp in copies:
            cp.wait()
        tot = buf[0]
        totc = bufc[0]
        for i in range(1, N_DEV):
            tot = tot + buf[i]
            totc = totc + bufc[i]
        tot_ref[...] = tot
        totc_ref[...] = totc

    vm = pl.BlockSpec(memory_space=pltpu.VMEM)
    tot, totc = pl.pallas_call(
        reduce_body, name="small_reduce",
        in_specs=[vm] * (ns + 2), out_specs=[vm, vm],
        out_shape=[jax.ShapeDtypeStruct((SMALL_ROWS, SMALL_COLS), F32), jax.ShapeDtypeStruct((N_DEV, kk_, ch), F32)],
        scratch_shapes=[pltpu.VMEM((SMALL_ROWS, SMALL_COLS), F32), pltpu.VMEM((N_DEV, SMALL_ROWS, SMALL_COLS), F32),
                        pltpu.VMEM((N_DEV, kk_, ch), F32), pltpu.VMEM((N_DEV, N_DEV, kk_, ch), F32),
                        pltpu.SemaphoreType.DMA((2, N_DEV - 1)), pltpu.SemaphoreType.DMA((2, N_DEV - 1))],
        compiler_params=pltpu.CompilerParams(has_side_effects=True),
    )(*sg, gcw, loss)

    def adamw_body(*refs):
        tot_ref, totc_ref = refs[:2]
        w_refs, m_refs, v_refs = (refs[2 + i * (ns + 1):2 + (i + 1) * (ns + 1)] for i in range(3))
        outs = refs[3 * ns + 5:]
        loss_out = outs[0]
        go, do_, mo, vo = (outs[1 + i * (ns + 1):1 + (i + 1) * (ns + 1)] for i in range(4))
        me, _ = _me_and_peers()
        loss_out[...] = tot_ref[ns:ns + 1, 0:1]
        for i in range(ns + 1):
            g = tot_ref[i:i + 1, 0:widths[i]] if i < ns else totc_ref[me]
            delta, m2, v2 = _adamw_math(g, w_refs[i][...], m_refs[i][...], v_refs[i][...])
            go[i][...] = g
            do_[i][...] = delta
            mo[i][...] = m2
            vo[i][...] = v2

    shapes = [jax.ShapeDtypeStruct(t.shape, F32) for t in ws]
    res = pl.pallas_call(
        adamw_body, name="small_adamw",
        in_specs=[vm] * (3 * ns + 5), out_specs=[vm] * (4 * ns + 5),
        out_shape=[jax.ShapeDtypeStruct((1, 1), F32)] + shapes * 4,
    )(tot, totc, *ws, *ms, *vs)
    n1 = ns + 1
    return res[0], res[1:1 + n1], res[1 + n1:1 + 2 * n1], res[1 + 2 * n1:1 + 3 * n1], res[1 + 3 * n1:]


def _cast_shard(w, name):
    r = w.shape[0]
    return _rows(lambda t: t, [w], [], [(w.shape[1], CDT)], ts=_pick(r, (256, 128)), name=name)[0]


BIG = ["w_in", "w_mem_kv", "w_up", "w_sb_out", "w_ssd_out", "w_mem_out", "w_o", "w_down"]
LATE_W = BIG[1:]
COL_SHARDED = ("w_in", "w_mem_kv", "w_up")
SMALL = ["norm_mix_pre", "conv_b", "dt_bias", "a_log", "d_skip", "ssd_norm", "norm_mem",
         "norm_mix_post", "norm_mlp_pre", "norm_mlp_post"]
ALL_W = ["norm_mix_pre", "w_in", "conv_w", "conv_b", "dt_bias", "a_log", "d_skip", "ssd_norm", "norm_mem",
         "w_mem_kv", "w_sb_out", "w_ssd_out", "w_mem_out", "w_o", "norm_mix_post", "norm_mlp_pre", "w_up",
         "w_down", "norm_mlp_post"]
LANES = 128


def kernel(x, mem, norm_mix_pre, w_in, conv_w, conv_b, dt_bias, a_log, d_skip, ssd_norm, norm_mem, w_mem_kv, w_sb_out, w_ssd_out, w_mem_out, w_o, norm_mix_post, norm_mlp_pre, w_up, w_down, norm_mlp_post, loss_target, m_norm_mix_pre, m_w_in, m_conv_w, m_conv_b, m_dt_bias, m_a_log, m_d_skip, m_ssd_norm, m_norm_mem, m_w_mem_kv, m_w_sb_out, m_w_ssd_out, m_w_mem_out, m_w_o, m_norm_mix_post, m_norm_mlp_pre, m_w_up, m_w_down, m_norm_mlp_post, v_norm_mix_pre, v_w_in, v_conv_w, v_conv_b, v_dt_bias, v_a_log, v_d_skip, v_ssd_norm, v_norm_mem, v_w_mem_kv, v_w_sb_out, v_w_ssd_out, v_w_mem_out, v_w_o, v_norm_mix_post, v_norm_mlp_pre, v_w_up, v_w_down, v_norm_mlp_post):
    wd = dict(norm_mix_pre=norm_mix_pre, w_in=w_in, conv_w=conv_w, conv_b=conv_b, dt_bias=dt_bias, a_log=a_log,
              d_skip=d_skip, ssd_norm=ssd_norm, norm_mem=norm_mem, w_mem_kv=w_mem_kv, w_sb_out=w_sb_out,
              w_ssd_out=w_ssd_out, w_mem_out=w_mem_out, w_o=w_o, norm_mix_post=norm_mix_post,
              norm_mlp_pre=norm_mlp_pre, w_up=w_up, w_down=w_down, norm_mlp_post=norm_mlp_post)
    md = dict(norm_mix_pre=m_norm_mix_pre, w_in=m_w_in, conv_w=m_conv_w, conv_b=m_conv_b, dt_bias=m_dt_bias,
              a_log=m_a_log, d_skip=m_d_skip, ssd_norm=m_ssd_norm, norm_mem=m_norm_mem, w_mem_kv=m_w_mem_kv,
              w_sb_out=m_w_sb_out, w_ssd_out=m_w_ssd_out, w_mem_out=m_w_mem_out, w_o=m_w_o,
              norm_mix_post=m_norm_mix_post, norm_mlp_pre=m_norm_mlp_pre, w_up=m_w_up, w_down=m_w_down,
              norm_mlp_post=m_norm_mlp_post)
    vd = dict(norm_mix_pre=v_norm_mix_pre, w_in=v_w_in, conv_w=v_conv_w, conv_b=v_conv_b, dt_bias=v_dt_bias,
              a_log=v_a_log, d_skip=v_d_skip, ssd_norm=v_ssd_norm, norm_mem=v_norm_mem, w_mem_kv=v_w_mem_kv,
              w_sb_out=v_w_sb_out, w_ssd_out=v_w_ssd_out, w_mem_out=v_w_mem_out, w_o=v_w_o,
              norm_mix_post=v_norm_mix_post, norm_mlp_pre=v_norm_mlp_pre, w_up=v_w_up, w_down=v_w_down,
              norm_mlp_post=v_norm_mlp_post)
    w_in_t, m_in_t, v_in_t = (t["w_in"][0].T for t in (wd, md, vd))
    shards = {n: _cast_shard(w_in_t if n == "w_in" else wd[n][0], "cast_" + n) for n in BIG}
    w_in_g, conv_w_g = _gather_two_level([shards["w_in"], wd["conv_w"][0]], "gather_w_in")
    wt = {"w_in": w_in_g.reshape(N_DEV * w_in_g.shape[1], w_in_g.shape[2])}
    ch = conv_w_g.shape[2]

    p = {n: wd[n] for n in SMALL}
    p["conv_w"] = conv_w_g.transpose(1, 0, 2).reshape(CONV_K, N_DEV * ch)
    loss, grad_x, gw, sg, late_received = _local_step(x[0], mem[0], loss_target[0], p, wt,
                                                      [shards[n] for n in LATE_W])

    received = dict(zip(LATE_W, late_received))
    w_in_windows = gw["w_in"]

    grads, deltas, new_m, new_v = {}, {}, {}, {}
    for n in BIG:
        if n == "w_in":
            res = [t.T for t in _adamw_windows(w_in_windows, w_in_t, m_in_t, v_in_t, "adamw_" + n)]
        else:
            res = _adamw_reduce(received[n], wd[n][0], md[n][0], vd[n][0], "adamw_" + n)
        grads[n], deltas[n], new_m[n], new_v[n] = (t[None] for t in res)
    small_names = SMALL + ["conv_w"]
    gcw = sg["conv_w"].reshape(CONV_K, N_DEV, ch).transpose(1, 0, 2)
    small_of = lambda dct: [dct[n] for n in SMALL] + [dct["conv_w"][0]]
    loss_red, g_s, d_s, m_s, v_s = _small_step([sg[n] for n in SMALL], gcw, loss, small_of(wd), small_of(md),
                                               small_of(vd))
    for i, n in enumerate(small_names):
        shape = wd[n].shape
        grads[n], deltas[n], new_m[n], new_v[n] = (t.reshape(shape) for t in (g_s[i], d_s[i], m_s[i], v_s[i]))
    loss_out = loss_red.reshape(())

    return (loss_out, grad_x[None], *[grads[n] for n in ALL_W], *[deltas[n] for n in ALL_W],
            *[new_m[n] for n in ALL_W], *[new_v[n] for n in ALL_W])
```

```python
import functools

import jax
import jax.numpy as jnp
from jax import lax
from jax.experimental import pallas as pl
from jax.experimental.pallas import tpu as pltpu

F32 = jnp.float32
BF16 = jnp.bfloat16
CDT = jnp.bfloat16
EPS = 1e-6
VMEM_LIMIT = 56 * 1024 * 1024

N_DEV = 8
D_MODEL = 1024
SB_H, SB_DH = 16, 64
SSD_G, SSD_R, SSD_P, SSD_N, SSD_L = 4, 8, 64, 128, 128
SSD_H = SSD_G * SSD_R
SSD_INNER = SSD_H * SSD_P
CONV_K = 4
CONV_DIM = SSD_INNER + 2 * SSD_G * SSD_N
MEM_H, MEM_DH = 4, 256
DT_PAD = 128
SB_TQ, SB_BK = 2048, 256
CONV_PAD = 8
MM_TILE, MM_TILE_K = 1024, 2048

ADAM_LR, ADAM_B1, ADAM_B2, ADAM_EPS, ADAM_WD, ADAM_STEP = 0.001, 0.9, 0.999, 1e-08, 0.01, 10

NT = (((1,), (1,)), ((), ()))
TN = (((0,), (0,)), ((), ()))
NN = (((1,), (0,)), ((), ()))


def _cparams(sem=None):
    return pltpu.CompilerParams(dimension_semantics=sem, vmem_limit_bytes=VMEM_LIMIT)


def _pick(n, cands):
    for c in cands:
        if n % c == 0:
            return c
    return n


def _dot(a, b, dims=NN):
    return lax.dot_general(a.astype(CDT), b.astype(CDT), dims, preferred_element_type=F32)


def _split_dot(x, t, left, pieces):
    if CDT == F32:
        return lax.dot_general(t, x, NN, preferred_element_type=F32) if left else \
            lax.dot_general(x, t, NN, preferred_element_type=F32)
    acc = None
    rem = x
    for _ in range(pieces):
        hi = rem.astype(BF16)
        rem = rem - hi.astype(F32)
        d = lax.dot_general(t, hi, NN, preferred_element_type=F32) if left else \
            lax.dot_general(hi, t, NN, preferred_element_type=F32)
        acc = d if acc is None else acc + d
    return acc


def _iota(shape, dim):
    return lax.broadcasted_iota(jnp.int32, shape, dim)


def _sigmoid(x):
    return 1.0 / (1.0 + jnp.exp(-x))


def _silu(x):
    return x * _sigmoid(x)


def _dsilu(x):
    s = _sigmoid(x)
    return s * (1.0 + x * (1.0 - s))


def _softplus(x):
    return jnp.maximum(x, 0.0) + jnp.log(1.0 + jnp.exp(-jnp.abs(x)))


def _rms(x, g):
    r = lax.rsqrt(jnp.mean(x * x, axis=-1, keepdims=True) + EPS)
    return x * r * g


def _rms_bwd(x, g, dy):
    r = lax.rsqrt(jnp.mean(x * x, axis=-1, keepdims=True) + EPS)
    n = x * r
    dn = dy * g
    dx = r * (dn - n * jnp.mean(dn * n, axis=-1, keepdims=True))
    dg = jnp.sum(dy * n, axis=0, keepdims=True)
    return dx, dg


def _mm(a, b, *, ta=False, tb=False, out_dtype=F32, name, b_off=0, n=None, b_koff=0, into=None, into_off=0,
        into_rows=False, hosted=None, epilogue=None):
    m = a.shape[1] if ta else a.shape[0]
    k = a.shape[0] if ta else a.shape[1]
    if n is None:
        n = b.shape[0] if tb else b.shape[1]
    assert b_koff + k <= (b.shape[1] if tb else b.shape[0])
    bm = _pick(m, (MM_TILE, 512, 256, 128))
    bn = _pick(n, (MM_TILE, 512, 256, 128))
    bk = next(c for c in (MM_TILE_K, 1024, 512, 256, 128, k) if k % c == 0 and b_koff % c == 0)
    nk = k // bk
    assert b_off % bn == 0 and into_off % (bm if into_rows else bn) == 0
    jb, kb = b_off // bn, b_koff // bk
    io, jo = (into_off // bm, 0) if into_rows else (0, into_off // bn)
    dims = (((0 if ta else 1,), (1 if tb else 0,)), ((), ()))
    grid = (m // bm, n // bn, nk)
    off = 1 if into is not None else 0
    nh_in = len(hosted.ins) if hosted else 0
    nh_out = len(hosted.shapes) if hosted else 0
    epi_fn, extras, out_dtypes = epilogue if epilogue else (lambda p: (p,), [], [out_dtype])
    ne, no = len(extras), len(out_dtypes)
    assert not (epilogue and into is not None)

    def body(a_ref, b_ref, *rest):
        e_refs = rest[off:off + ne]
        o_refs = rest[off + ne + nh_in:off + ne + nh_in + no]
        acc_ref = rest[off + ne + nh_in + no + nh_out]

        def emit(total):
            res = epi_fn(total, *[e[...] for e in e_refs])
            for o_ref, val in zip(o_refs, res):
                o_ref[...] = val.astype(o_ref.dtype)

        if hosted:
            h_refs = (rest[off + ne:off + ne + nh_in],
                      rest[off + ne + nh_in + no:off + ne + nh_in + no + nh_out],
                      rest[off + ne + nh_in + no + nh_out + 1:])
            step = (pl.program_id(0) * grid[1] + pl.program_id(1)) * grid[2] + pl.program_id(2)

            @pl.when(step == 0)
            def _():
                for cp in hosted.copies(*h_refs):
                    cp.start()

        part = _dot(a_ref[...], b_ref[...], dims)
        if nk == 1:
            emit(part)
        else:
            kk = pl.program_id(2)

            @pl.when(kk == 0)
            def _():
                acc_ref[...] = part

            @pl.when(jnp.logical_and(kk > 0, kk < nk - 1))
            def _():
                acc_ref[...] += part

            @pl.when(kk == nk - 1)
            def _():
                emit(acc_ref[...] + part)

        if hosted:
            @pl.when(step == grid[0] * grid[1] * grid[2] - 1)
            def _():
                for cp in hosted.copies(*h_refs):
                    cp.wait()

    a_spec = pl.BlockSpec((bk, bm), lambda i, j, kk: (kk, i)) if ta else \
        pl.BlockSpec((bm, bk), lambda i, j, kk: (i, kk))
    b_spec = pl.BlockSpec((bn, bk), lambda i, j, kk: (j + jb, kk + kb)) if tb else \
        pl.BlockSpec((bk, bn), lambda i, j, kk: (kk + kb, j + jb))
    extra = {} if into is None else {"input_output_aliases": {2: 0}}
    out_shapes = [jax.ShapeDtypeStruct((m, n), dt) for dt in out_dtypes] if into is None else \
        [jax.ShapeDtypeStruct(into.shape, into.dtype)]
    block = pl.BlockSpec((bm, bn), lambda i, j, kk: (i, j))
    res = pl.pallas_call(
        body, name=name, grid=grid,
        in_specs=[a_spec, b_spec] + ([] if into is None else [pl.BlockSpec(memory_space=pl.ANY)])
        + [block] * ne + [HBM] * nh_in,
        out_specs=[pl.BlockSpec((bm, bn), lambda i, j, kk: (i + io, j + jo))] * no + [HBM] * nh_out,
        out_shape=out_shapes + (list(hosted.shapes) if hosted else []),
        scratch_shapes=[pltpu.VMEM((bm, bn) if nk > 1 else (8, 128), F32)] + (list(hosted.sems) if hosted else []),
        compiler_params=_cparams(("arbitrary",) * 3 if hosted else ("parallel", "parallel", "arbitrary")),
        **extra,
    )(*((a, b) if into is None else (a, b, into)), *extras, *(hosted.ins if hosted else ()))
    return res if (hosted or epilogue) else res[0]


def _rows(fn, tiled, params, outs, accs=(), *, ts, name):
    s = tiled[0].shape[0]
    ts = min(ts, s)
    assert s % ts == 0
    nt, npar, no, na = len(tiled), len(params), len(outs), len(accs)

    def body(*refs):
        i = pl.program_id(0)
        vals = [r[...] for r in refs[:nt + npar]]
        res = fn(*vals)
        if not isinstance(res, (tuple, list)):
            res = (res,)
        orefs = refs[nt + npar:nt + npar + no]
        arefs = refs[nt + npar + no:]
        for r_, val in zip(orefs, res[:no]):
            r_[...] = val.astype(r_.dtype)
        if na:
            @pl.when(i == 0)
            def _():
                for r_ in arefs:
                    r_[...] = jnp.zeros_like(r_)

            for r_, val in zip(arefs, res[no:]):
                r_[...] += jnp.broadcast_to(val, r_.shape)

    in_specs = [pl.BlockSpec((ts, a.shape[1]), lambda i: (i, 0)) for a in tiled]
    in_specs += [pl.BlockSpec(p.shape, lambda i: (0, 0)) for p in params]
    out_specs = [pl.BlockSpec((ts, w), lambda i: (i, 0)) for (w, _) in outs]
    out_specs += [pl.BlockSpec(shape, lambda i: (0, 0)) for shape in accs]
    out_shape = [jax.ShapeDtypeStruct((s, w), dt) for (w, dt) in outs]
    out_shape += [jax.ShapeDtypeStruct(shape, F32) for shape in accs]
    res = pl.pallas_call(
        body, name=name, grid=(s // ts,),
        in_specs=in_specs, out_specs=out_specs, out_shape=out_shape,
        compiler_params=_cparams(("arbitrary",)),
    )(*tiled, *params)
    return res


def _sb_block(qs, kb, diag):
    tq, bk = qs.shape[0], kb.shape[0]
    z = _dot(qs, kb, NT)
    lb = jnp.minimum(z, 0.0) - jnp.log(1.0 + jnp.exp(-jnp.abs(z)))
    lk = lb - z
    if diag is None:
        return None, lb, lk
    causal = (diag + _iota((tq, bk), 1)) < _iota((tq, bk), 0)
    return causal, lb, jnp.where(causal, lk, 0.0)


def _fused_exchange(scatter, ncols, nsteps):
    def hooks(ins, outs, sems):
        step = pl.program_id(0) * ncols + pl.program_id(1)

        @pl.when(step == 0)
        def _():
            for cp in _exchange_copies(ins, outs, *sems, scatter):
                cp.start()

        def finish():
            @pl.when(step == nsteps - 1)
            def _():
                for cp in _exchange_copies(ins, outs, *sems, scatter):
                    cp.wait()
        return finish
    return hooks


SB_PAIR = 128


def _sb_fwd_pairs(qkv, comm=()):
    s, d3 = qkv.shape
    d = d3 // 3
    npair = d // SB_PAIR
    tq, bk = min(SB_TQ, s), min(SB_BK, s)
    scale = SB_DH ** -0.5
    nc = len(comm)
    hooks = _fused_exchange(False, s // tq, npair * (s // tq))

    def body(q_ref, k_ref, v_ref, *rest):
        y_ref, lt_ref = rest[nc:nc + 2]
        if nc:
            finish = hooks(rest[:nc], rest[nc + 2:2 * nc + 2], rest[2 * nc + 2:])
        i = pl.program_id(1)
        q0 = i * tq
        q2 = q_ref[...] * scale
        lane_head = (_iota((1, SB_PAIR), 1) >= SB_DH).astype(jnp.int32)
        tri = (_iota((bk, bk), 0) > _iota((bk, bk), 1)).astype(CDT)

        def head(hh, y):
            mine = lane_head == hh
            qs = jnp.where(mine, q2, jnp.zeros_like(q2))

            def step(k0, carry, diag, r0=0):
                cf, acc = carry
                kb = k_ref[pl.ds(k0, bk), :]
                vb = v_ref[pl.ds(k0, bk), :]
                causal, lb, lk = _sb_block(qs[r0:], kb, diag)
                w = jnp.exp(lb + cf + _split_dot(lk, tri, False, 2))
                if causal is not None:
                    w = jnp.where(causal, w, 0.0)
                return cf + jnp.sum(lk, axis=1, keepdims=True), acc + _dot(w, vb)

            carry = (jnp.zeros((tq, 1), F32), jnp.zeros((tq, SB_PAIR), F32))
            for dd in reversed(range(tq // bk)):
                r0 = dd * bk
                sub = step(pl.multiple_of(q0 + r0, bk), tuple(t[r0:] for t in carry), 0, r0)
                carry = tuple(jnp.concatenate([t[:r0], u], axis=0) if r0 else u for t, u in zip(carry, sub))
            nfull = q0 // bk
            cf, acc = lax.fori_loop(
                0, nfull, lambda jj, c: step(pl.multiple_of((nfull - 1 - jj) * bk, bk), c, None), carry)
            lt_ref[hh] = cf
            return jnp.where(mine, acc, y)

        y_ref[...] = lax.fori_loop(0, 2, head, jnp.zeros((tq, SB_PAIR), F32)).astype(y_ref.dtype)
        if nc:
            finish()

    return pl.pallas_call(
        body, name="sb_fwd", grid=(npair, s // tq),
        in_specs=[pl.BlockSpec((tq, SB_PAIR), lambda a, i: (i, a)),
                  pl.BlockSpec((s, SB_PAIR), lambda a, i: (0, npair + a)),
                  pl.BlockSpec((s, SB_PAIR), lambda a, i: (0, 2 * npair + a))] + [HBM] * nc,
        out_specs=[pl.BlockSpec((tq, SB_PAIR), lambda a, i: (i, a)),
                   pl.BlockSpec((2, tq, 1), lambda a, i: (a, i, 0))] + [HBM] * nc,
        out_shape=[jax.ShapeDtypeStruct((s, d), CDT), jax.ShapeDtypeStruct((2 * npair, s, 1), F32)]
        + _exchange_shapes(comm, False),
        scratch_shapes=_exchange_sems(nc) if nc else [],
        compiler_params=_cparams(("arbitrary", "arbitrary")),
    )(qkv, qkv, qkv, *comm)


def _sb_bwd_pairs(qkv, ltot, dy, comm=()):
    s, d3 = qkv.shape
    d = d3 // 3
    npair = d // SB_PAIR
    tq, bk = min(SB_TQ, s), min(SB_BK, s)
    scale = SB_DH ** -0.5
    nc = len(comm)
    hooks = _fused_exchange(True, s // tq, npair * (s // tq))

    def body(q_ref, k_ref, v_ref, lt_ref, dy_ref, *rest):
        dq_ref, dk_ref, dv_ref = rest[nc:nc + 3]
        if nc:
            finish = hooks(rest[:nc], rest[nc + 3:2 * nc + 3], rest[2 * nc + 3:])
        i = pl.program_id(1)

        @pl.when(i == 0)
        def _():
            dk_ref[...] = jnp.zeros_like(dk_ref)
            dv_ref[...] = jnp.zeros_like(dv_ref)

        q0 = i * tq
        q2 = q_ref[...] * scale
        do2 = dy_ref[...].astype(CDT)
        lane_head = (_iota((1, SB_PAIR), 1) >= SB_DH).astype(jnp.int32)
        tri_le = (_iota((bk, bk), 0) <= _iota((bk, bk), 1)).astype(CDT)
        tri_lt = (_iota((bk, bk), 0) < _iota((bk, bk), 1)).astype(CDT)

        def head(hh, dq_all):
            mine = lane_head == hh
            qs = jnp.where(mine, q2, jnp.zeros_like(q2))
            dov = jnp.where(mine, do2, jnp.zeros_like(do2))
            ltot_h = lt_ref[hh]

            def step(k0, carry, diag, r0=0):
                cf, cg, dq = carry
                kb = k_ref[pl.ds(k0, bk), :]
                vb = v_ref[pl.ds(k0, bk), :]
                causal, lb, lk = _sb_block(qs[r0:], kb, diag)
                w = jnp.exp(lb + ltot_h[r0:] - (cf + _split_dot(lk, tri_le, False, 2)))
                if causal is not None:
                    w = jnp.where(causal, w, 0.0)
                g = w * _dot(dov[r0:], vb, NT)
                gsum = cg + _split_dot(g, tri_lt, False, 2)
                dz = g - (g + gsum) * jnp.exp(lb)
                if causal is not None:
                    dz = jnp.where(causal, dz, 0.0)
                dzc = dz.astype(CDT)
                dk_ref[pl.ds(k0, bk), :] += _dot(dzc, qs[r0:], TN)
                dv_ref[pl.ds(k0, bk), :] += _dot(w, dov[r0:], TN)
                kbm = jnp.where(mine, kb, jnp.zeros_like(kb))
                return (cf + jnp.sum(lk, axis=1, keepdims=True), cg + jnp.sum(g, axis=1, keepdims=True),
                        dq + _dot(dzc, kbm))

            carry = (jnp.zeros((tq, 1), F32), jnp.zeros((tq, 1), F32), jnp.zeros((tq, SB_PAIR), F32))
            carry = lax.fori_loop(0, q0 // bk, lambda jj, c: step(pl.multiple_of(jj * bk, bk), c, None), carry)
            for dd in range(tq // bk):
                r0 = dd * bk
                sub = step(pl.multiple_of(q0 + r0, bk), tuple(t[r0:] for t in carry), 0, r0)
                carry = tuple(jnp.concatenate([t[:r0], u], axis=0) if r0 else u for t, u in zip(carry, sub))
            return dq_all + carry[2]

        dq_ref[...] = lax.fori_loop(0, 2, head, jnp.zeros((tq, SB_PAIR), F32)) * scale
        if nc:
            finish()

    tile = pl.BlockSpec((tq, SB_PAIR), lambda a, i: (i, a))
    acc = pl.BlockSpec((s, SB_PAIR), lambda a, i: (0, a))
    shp = jax.ShapeDtypeStruct((s, d), F32)
    return pl.pallas_call(
        body, name="sb_bwd", grid=(npair, s // tq),
        in_specs=[tile, pl.BlockSpec((s, SB_PAIR), lambda a, i: (0, npair + a)),
                  pl.BlockSpec((s, SB_PAIR), lambda a, i: (0, 2 * npair + a)),
                  pl.BlockSpec((2, tq, 1), lambda a, i: (a, i, 0)), tile] + [HBM] * nc,
        out_specs=[tile, acc, acc] + [HBM] * nc,
        out_shape=[shp, shp, shp] + _exchange_shapes(comm, True),
        scratch_shapes=_exchange_sems(nc) if nc else [],
        compiler_params=_cparams(("arbitrary", "arbitrary")),
    )(qkv, qkv, qkv, ltot, dy, *comm)


def _pick_lane(tile, r):
    return jnp.sum(jnp.where(_iota(tile.shape, 1) == r, tile, 0.0), axis=1, keepdims=True)


def _pick_row(tile, r):
    return jnp.sum(jnp.where(_iota(tile.shape, 0) == r, tile, 0.0), axis=0, keepdims=True)


SSD_CPS = 2


def _ssd_chunk_common(cv, bv, dac, dar):
    l = SSD_L
    tdt = F32 if CDT == F32 else BF16
    lower = (_iota((l, l), 1) <= _iota((l, l), 0)).astype(tdt)
    upper = (_iota((l, l), 0) <= _iota((l, l), 1)).astype(tdt)
    return _dot(cv, bv, NT), _split_dot(dac, lower, True, 3), _split_dot(dar, upper, False, 3)


def _ssd_fwd_g(xh, dtc, dac, dar, dsk, xbc):
    hh, s, p = xh.shape
    l, n, g_, r_ = SSD_L, SSD_N, SSD_G, SSD_R
    nc = s // l
    cps = SSD_CPS if nc % SSD_CPS == 0 else 1
    lb = cps * l
    boff = SSD_INNER // n
    coff = boff + g_

    def body(x_ref, dtc_ref, dac_ref, dar_ref, dsk_ref, b_ref, c_ref, y_ref, st_ref, state_ref):
        c = pl.program_id(1)

        @pl.when(c == 0)
        def _():
            state_ref[...] = jnp.zeros_like(state_ref)

        mask = _iota((l, l), 1) <= _iota((l, l), 0)
        dskv = dsk_ref[...]
        for cc in range(cps):
            rows = pl.ds(cc * l, l)
            cv, bv, dtcv = c_ref[rows, :], b_ref[rows, :], dtc_ref[rows, :]
            cb, acs_c, acs_r = _ssd_chunk_common(cv, bv, dac_ref[rows, :], dar_ref[:, rows])
            for r in range(r_):
                a_col = _pick_lane(acs_c, r)
                a_row = _pick_row(acs_r, r)
                dsk_h = _pick_lane(dskv, r)
                a_b = jnp.broadcast_to(a_col, (l, l))
                dt_b = jnp.broadcast_to(_pick_lane(dtcv, r), (l, p))
                xv = x_ref[r, rows, :]
                xd = xv * dt_b
                decay = jnp.where(mask, jnp.exp(jnp.minimum(a_b - a_row, 0.0)), 0.0)
                hprev = state_ref[r]
                y = _dot(cb * decay, xd) + jnp.exp(a_b)[:, :p] * _dot(cv, hprev)
                y_ref[r, rows, :] = y + dsk_h * xv
                a_end = a_col[l - 1:l, :]
                st_ref[r, cc] = hprev
                state_ref[r] = hprev * jnp.exp(a_end) + _dot(bv, xd * jnp.exp(a_end - a_b)[:, :p], TN)

    return pl.pallas_call(
        body, name="ssd_fwd", grid=(g_, nc // cps),
        in_specs=[pl.BlockSpec((r_, lb, p), lambda g, c: (g, c, 0)),
                  pl.BlockSpec((None, lb, r_), lambda g, c: (g, c, 0)),
                  pl.BlockSpec((None, lb, r_), lambda g, c: (g, c, 0)),
                  pl.BlockSpec((None, r_, lb), lambda g, c: (g, 0, c)),
                  pl.BlockSpec((None, 1, r_), lambda g, c: (g, 0, 0)),
                  pl.BlockSpec((lb, n), lambda g, c: (c, (boff + g))),
                  pl.BlockSpec((lb, n), lambda g, c: (c, (coff + g)))],
        out_specs=[pl.BlockSpec((r_, lb, p), lambda g, c: (g, c, 0)),
                   pl.BlockSpec((r_, cps, n, p), lambda g, c: (g, c, 0, 0))],
        out_shape=[jax.ShapeDtypeStruct((hh, s, p), F32),
                   jax.ShapeDtypeStruct((hh, nc, n, p), F32)],
        scratch_shapes=[pltpu.VMEM((r_, n, p), F32)],
        compiler_params=_cparams(("parallel", "arbitrary")),
    )(xh, dtc, dac, dar, dsk, xbc, xbc)


def _ssd_bwd_g(xh, dtc, dac, dar, dsk, xbc, st, dy):
    hh, s, p = xh.shape
    l, n, g_, r_ = SSD_L, SSD_N, SSD_G, SSD_R
    nc = s // l
    cps = SSD_CPS if nc % SSD_CPS == 0 else 1
    lb, ncb = cps * l, nc // cps
    boff = SSD_INNER // n
    coff = boff + g_

    def body(x_ref, dtc_ref, dac_ref, dar_ref, dsk_ref, b_ref, c_ref, st_ref, dy_ref,
             dx_ref, dda_ref, dxx_ref, db_ref, dc_ref, dah_ref, ddsk_ref, dstate_ref):
        c = pl.program_id(1)

        @pl.when(c == 0)
        def _():
            dstate_ref[...] = jnp.zeros_like(dstate_ref)
            dah_ref[...] = jnp.zeros_like(dah_ref)
            ddsk_ref[...] = jnp.zeros_like(ddsk_ref)

        il = _iota((l, l), 0)
        isx = _iota((l, l), 1)
        tdt = F32 if CDT == F32 else BF16
        t1 = (isx >= il).astype(tdt)
        lane = _iota((l, r_), 1)
        lane1 = _iota((1, r_), 1)
        dskv = dsk_ref[...]
        for cc in reversed(range(cps)):
            rows = pl.ds(cc * l, l)
            cv, bv, dtcv = c_ref[rows, :], b_ref[rows, :], dtc_ref[rows, :]
            cb, acs_c, acs_r = _ssd_chunk_common(cv, bv, dac_ref[rows, :], dar_ref[:, rows])
            dda_all = jnp.zeros((l, r_), F32)
            dxx_all = jnp.zeros((l, r_), F32)
            dah_all = jnp.zeros((1, r_), F32)
            ddsk_all = jnp.zeros((1, r_), F32)
            db_acc = jnp.zeros((l, n), F32)
            dc_acc = jnp.zeros((l, n), F32)
            md_sum = jnp.zeros((l, l), F32)
            cb_t = _dot(bv, cv, NT)
            cv_t = cv.T
            for r in range(r_):
                a_col = _pick_lane(acs_c, r)
                a_row = _pick_row(acs_r, r)
                dt_col = _pick_lane(dtcv, r)
                dsk_h = _pick_lane(dskv, r)
                xv = x_ref[r, rows, :]
                dyv = dy_ref[r, rows, :]
                a_b = jnp.broadcast_to(a_col, (l, l))
                dt_b = jnp.broadcast_to(dt_col, (l, p))
                xd = xv * dt_b
                decay = jnp.where(isx <= il, jnp.exp(jnp.minimum(a_b - a_row, 0.0)), 0.0)
                decay_t = jnp.where(isx >= il, jnp.exp(jnp.minimum(a_row - a_b, 0.0)), 0.0)
                dhn = dstate_ref[r]
                hc = st_ref[r, cc]
                a_end = a_col[l - 1:l, :]
                ea_b = jnp.exp(a_b)
                dte_b = jnp.exp(a_end - a_b)

                dx_state = dte_b[:, :p] * _dot(bv, dhn)
                dxd = _dot(cb_t * decay_t, dyv) + dx_state
                md = decay * _dot(dyv, xd, NT)
                md_sum = md_sum + md
                dc_acc = dc_acc + ea_b * _dot(dyv, hc, NT)
                db_acc = db_acc + dte_b * _dot(xd, dhn, NT)
                dstate_ref[r] = jnp.exp(a_end) * dhn + _dot(cv_t, dyv * ea_b[:, :p])

                yoff = ea_b[:, :p] * _dot(cv, hc)
                xdx = xd * dx_state
                vec = jnp.sum(dyv * yoff - xdx, axis=1, keepdims=True)
                total = lambda t: jnp.sum(jnp.sum(t, axis=0, keepdims=True), axis=1, keepdims=True)
                end_term = total(xdx) + jnp.exp(a_end) * total(hc * dhn)
                zmat = _split_dot(md * cb, t1, True, 2)
                span = jnp.sum(jnp.where(isx < il, zmat, 0.0), axis=1, keepdims=True)
                rc = _split_dot(jnp.broadcast_to(vec, (l, 128)), t1, True, 2)[:, :1]
                dda = span + rc + end_term
                dda_all = jnp.where(lane == r, dda, dda_all)
                dxx_all = jnp.where(lane == r, jnp.sum(dxd * xv, axis=1, keepdims=True), dxx_all)
                dx_ref[r, rows, :] = dxd * dt_b + dsk_h * dyv
                dah_all = jnp.where(lane1 == r, jnp.sum(dda * dt_col, axis=0, keepdims=True), dah_all)
                ddsk_all = jnp.where(lane1 == r, total(dyv * xv), ddsk_all)
            dda_ref[rows, :] = dda_all
            dxx_ref[rows, :] = dxx_all
            db_ref[rows, :] = db_acc + _dot(md_sum, cv, TN)
            dc_ref[rows, :] = dc_acc + _dot(md_sum, bv)
            dah_ref[...] += dah_all
            ddsk_ref[...] += ddsk_all

    rev = lambda c: ncb - 1 - c
    xspec = pl.BlockSpec((r_, lb, p), lambda g, c: (g, rev(c), 0))
    cspec = pl.BlockSpec((None, lb, r_), lambda g, c: (g, rev(c), 0))
    hspec = pl.BlockSpec((None, 1, r_), lambda g, c: (g, 0, 0))
    return pl.pallas_call(
        body, name="ssd_bwd", grid=(g_, ncb),
        in_specs=[xspec, cspec, cspec,
                  pl.BlockSpec((None, r_, lb), lambda g, c: (g, 0, rev(c))),
                  hspec,
                  pl.BlockSpec((lb, n), lambda g, c: (rev(c), boff + g)),
                  pl.BlockSpec((lb, n), lambda g, c: (rev(c), coff + g)),
                  pl.BlockSpec((r_, cps, n, p), lambda g, c: (g, rev(c), 0, 0)),
                  xspec],
        out_specs=[xspec, cspec, cspec,
                   pl.BlockSpec((lb, n), lambda g, c: (rev(c), g)),
                   pl.BlockSpec((lb, n), lambda g, c: (rev(c), g)),
                   hspec, hspec],
        out_shape=[jax.ShapeDtypeStruct((hh, s, p), F32),
                   jax.ShapeDtypeStruct((g_, s, r_), F32),
                   jax.ShapeDtypeStruct((g_, s, r_), F32),
                   jax.ShapeDtypeStruct((s, g_ * n), F32),
                   jax.ShapeDtypeStruct((s, g_ * n), F32),
                   jax.ShapeDtypeStruct((g_, 1, r_), F32),
                   jax.ShapeDtypeStruct((g_, 1, r_), F32)],
        scratch_shapes=[pltpu.VMEM((r_, n, p), F32)],
        compiler_params=_cparams(("parallel", "arbitrary")),
    )(xh, dtc, dac, dar, dsk, xbc, xbc, st, dy)


CONV_TC = 256
CONV_RC = 512


def _conv_taps(x_ref, head_ref, rc):
    base = CONV_PAD - (CONV_K - 1)
    head_ref[pl.ds(0, CONV_PAD), :] = jnp.zeros((CONV_PAD, head_ref.shape[1]), F32)
    head_ref[pl.ds(CONV_PAD, rc), :] = x_ref[pl.ds(0, rc), :]

    def tap(t0, kk):
        if t0 == 0:
            return head_ref[pl.ds(base + kk, rc), :]
        return x_ref[pl.ds(t0 - (CONV_K - 1) + kk, rc), :]
    return tap


def _conv_fwd(x, w, b):
    s, ch = x.shape
    rc = min(CONV_RC, s)

    def body(x_ref, w_ref, b_ref, pre_ref, act_ref, head_ref):
        wv = w_ref[...]
        tap = _conv_taps(x_ref, head_ref, rc)
        for t0 in range(0, s, rc):
            acc = jnp.broadcast_to(b_ref[...], (rc, CONV_TC))
            for kk in range(CONV_K):
                acc = acc + wv[kk:kk + 1, :] * tap(t0, kk)
            pre_ref[pl.ds(t0, rc), :] = acc
            act_ref[pl.ds(t0, rc), :] = _silu(acc)

    col = pl.BlockSpec((s, CONV_TC), lambda j: (0, j))
    shp = jax.ShapeDtypeStruct((s, ch), F32)
    return pl.pallas_call(
        body, name="conv_fwd", grid=(ch // CONV_TC,),
        in_specs=[col, pl.BlockSpec((CONV_K, CONV_TC), lambda j: (0, j)),
                  pl.BlockSpec((1, CONV_TC), lambda j: (0, j))],
        out_specs=[col, col],
        out_shape=[shp, shp],
        scratch_shapes=[pltpu.VMEM((CONV_PAD + rc, CONV_TC), F32)],
        compiler_params=_cparams(("parallel",)),
    )(x, w, b)


def _conv_bwd(x, pre, dact, w):
    s, ch = x.shape
    rc = min(CONV_RC, s)

    def body(x_ref, pre_ref, da_ref, w_ref, dx_ref, dw_ref, db_ref, dpre_ref, head_ref):
        wv = w_ref[...]
        tap = _conv_taps(x_ref, head_ref, rc)
        for t0 in range(0, s, rc):
            dpre_ref[pl.ds(t0, rc), :] = da_ref[pl.ds(t0, rc), :] * _dsilu(pre_ref[pl.ds(t0, rc), :])
        dpre_ref[pl.ds(s, CONV_PAD), :] = jnp.zeros((CONV_PAD, CONV_TC), F32)
        dws = [jnp.zeros((1, CONV_TC), F32) for _ in range(CONV_K)]
        dbs = jnp.zeros((1, CONV_TC), F32)
        for t0 in range(0, s, rc):
            acc = jnp.zeros((rc, CONV_TC), F32)
            dp = dpre_ref[pl.ds(t0, rc), :]
            for kk in range(CONV_K):
                acc = acc + wv[kk:kk + 1, :] * dpre_ref[pl.ds(t0 + CONV_K - 1 - kk, rc), :]
                dws[kk] = dws[kk] + jnp.sum(dp * tap(t0, kk), axis=0, keepdims=True)
            dbs = dbs + jnp.sum(dp, axis=0, keepdims=True)
            dx_ref[pl.ds(t0, rc), :] = acc.astype(dx_ref.dtype)
        for kk in range(CONV_K):
            dw_ref[kk:kk + 1, :] = dws[kk]
        db_ref[...] = dbs

    col = pl.BlockSpec((s, CONV_TC), lambda j: (0, j))
    return pl.pallas_call(
        body, name="conv_bwd", grid=(ch // CONV_TC,),
        in_specs=[col, col, col, pl.BlockSpec((CONV_K, CONV_TC), lambda j: (0, j))],
        out_specs=[col, pl.BlockSpec((CONV_K, CONV_TC), lambda j: (0, j)),
                   pl.BlockSpec((1, CONV_TC), lambda j: (0, j))],
        out_shape=[jax.ShapeDtypeStruct((s, ch), CDT),
                   jax.ShapeDtypeStruct((CONV_K, ch), F32),
                   jax.ShapeDtypeStruct((1, ch), F32)],
        scratch_shapes=[pltpu.VMEM((s + CONV_PAD, CONV_TC), F32), pltpu.VMEM((CONV_PAD + rc, CONV_TC), F32)],
        compiler_params=_cparams(("parallel",)),
    )(x, pre, dact, w)


MEM_TS = 512


def _mem_fwd(mq, kv):
    s = mq.shape[0]
    m = kv.shape[0]
    ts = min(MEM_TS, s)
    scale = MEM_DH ** -0.5

    def body(q_ref, k_ref, v_ref, o_ref):
        sc = _dot(q_ref[...], k_ref[...], NT) * scale
        e = jnp.exp(sc - jnp.max(sc, axis=1, keepdims=True))
        pr = e / jnp.sum(e, axis=1, keepdims=True)
        o_ref[...] = _dot(pr, v_ref[...]).astype(o_ref.dtype)

    return pl.pallas_call(
        body, name="mem_fwd", grid=(MEM_H, s // ts),
        in_specs=[pl.BlockSpec((ts, MEM_DH), lambda a, i: (i, a)),
                  pl.BlockSpec((m, MEM_DH), lambda a, i: (0, a)),
                  pl.BlockSpec((m, MEM_DH), lambda a, i: (0, MEM_H + a))],
        out_specs=pl.BlockSpec((ts, MEM_DH), lambda a, i: (i, a)),
        out_shape=jax.ShapeDtypeStruct((s, MEM_H * MEM_DH), CDT),
        compiler_params=_cparams(("parallel", "arbitrary")),
    )(mq, kv, kv)


def _mem_bwd(mq, kv, do):
    s = mq.shape[0]
    m = kv.shape[0]
    ts = min(MEM_TS, s)
    scale = MEM_DH ** -0.5

    def body(q_ref, k_ref, v_ref, do_ref, dq_ref, dk_ref, dv_ref):
        i = pl.program_id(1)

        @pl.when(i == 0)
        def _():
            dk_ref[...] = jnp.zeros_like(dk_ref)
            dv_ref[...] = jnp.zeros_like(dv_ref)

        qv, kb, vb, dov = q_ref[...], k_ref[...], v_ref[...], do_ref[...]
        sc = _dot(qv, kb, NT) * scale
        e = jnp.exp(sc - jnp.max(sc, axis=1, keepdims=True))
        pr = e / jnp.sum(e, axis=1, keepdims=True)
        dp = _dot(dov, vb, NT)
        ds = pr * (dp - jnp.sum(dp * pr, axis=1, keepdims=True)) * scale
        dq_ref[...] = _dot(ds, kb).astype(dq_ref.dtype)
        dk_ref[...] += _dot(ds, qv, TN)
        dv_ref[...] += _dot(pr, dov, TN)

    tile = pl.BlockSpec((ts, MEM_DH), lambda a, i: (i, a))
    kvo = pl.BlockSpec((m, MEM_DH), lambda a, i: (0, a))
    return pl.pallas_call(
        body, name="mem_bwd", grid=(MEM_H, s // ts),
        in_specs=[tile, kvo, pl.BlockSpec((m, MEM_DH), lambda a, i: (0, MEM_H + a)), tile],
        out_specs=[tile, kvo, kvo],
        out_shape=[jax.ShapeDtypeStruct((s, MEM_H * MEM_DH), CDT),
                   jax.ShapeDtypeStruct((m, MEM_H * MEM_DH), F32),
                   jax.ShapeDtypeStruct((m, MEM_H * MEM_DH), F32)],
        compiler_params=_cparams(("parallel", "arbitrary")),
    )(mq, kv, kv, do)


def _heads(t, nh, dh):
    return t.reshape(t.shape[0], nh, dh).transpose(1, 0, 2)


def _unheads(t):
    return t.transpose(1, 0, 2).reshape(t.shape[1], t.shape[0] * t.shape[2])


def _group_cols(t):
    return t.reshape(t.shape[0], SSD_G, SSD_R).transpose(1, 0, 2)


def _pad_cols(t, width):
    return jnp.pad(t, ((0, 0), (0, width - t.shape[1])))


def _full_weight(name, gathered):
    if name in COL_SHARDED:
        return gathered.transpose(1, 0, 2).reshape(gathered.shape[1], N_DEV * gathered.shape[2])
    return gathered.reshape(N_DEV * gathered.shape[1], gathered.shape[2])


def _grad_payload(name, g):
    if name in COL_SHARDED:
        return g.reshape(g.shape[0], N_DEV, g.shape[1] // N_DEV).transpose(1, 0, 2)
    return g.reshape(N_DEV, g.shape[0] // N_DEV, g.shape[1])


def _local_step(x, mem, tgt, p, wt, shards=None):
    s, d = x.shape
    wt = dict(wt)
    c1, c2, c3, c4, c5 = 3 * d, 3 * d + SSD_INNER, 3 * d + SSD_INNER + CONV_DIM, \
        3 * d + SSD_INNER + CONV_DIM + SSD_H, 3 * d + SSD_INNER + CONV_DIM + SSD_H + d
    w_t = wt["w_in"]
    w_tail = w_t[c4:]
    w_dt = jnp.pad(w_t[c3:c4], ((0, DT_PAD - SSD_H), (0, 0)))
    seg_name = ["qkv", "z", "xbc", "mq", "gl"]
    seg_dtype = [CDT, F32, F32, CDT, F32]
    seg_src = [w_t, w_t, w_t, w_tail, w_tail]
    seg_off = [0, c1, c2, 0, d]
    seg_n = [c1, c2 - c1, c3 - c2, d, 3 * d]

    u = _rows(lambda xv, g: _rms(xv, g), [x], [p["norm_mix_pre"]], [(d, CDT)], ts=512, name="f_norm_pre")[0]
    qkv, z, xbc_raw, mq, gl = [
        _mm(u, seg_src[i], tb=True, b_off=seg_off[i], n=seg_n[i], out_dtype=seg_dtype[i],
            name="f_in_" + seg_name[i]) for i in range(5)]
    dt_raw = _mm(u, w_dt, tb=True, name="f_in_dt")

    bias128 = _pad_cols(p["dt_bias"], DT_PAD)
    alog128 = _pad_cols(p["a_log"], DT_PAD)

    def dt_fn(dtr, bias, alog):
        dt = _softplus(dtr + bias)
        return dt, dt * (-jnp.exp(alog))

    dt128, da128 = _rows(dt_fn, [dt_raw], [bias128, alog128], [(DT_PAD, F32), (DT_PAD, F32)],
                         ts=512, name="f_dt")
    dtc = _group_cols(dt128[:, :SSD_H])
    dac = _group_cols(da128[:, :SSD_H])
    dar = dac.transpose(0, 2, 1)
    dsk = p["d_skip"].reshape(SSD_G, 1, SSD_R)

    conv_w, conv_b = p["conv_w"], p["conv_b"]
    pre, xbc = _conv_fwd(xbc_raw, conv_w, conv_b)
    xh = _heads(xbc[:, :SSD_INNER], SSD_H, SSD_P)
    y_h, st = _ssd_fwd_g(xh, dtc, dac, dar, dsk, xbc)
    y_core = _unheads(y_h)

    def group_norm_fwd(yv, zv, wn):
        y2 = yv * _silu(zv)
        gw = SSD_INNER // SSD_G
        outs = []
        for gi in range(SSD_G):
            seg = y2[:, gi * gw:(gi + 1) * gw]
            outs.append(_rms(seg, wn[:, gi * gw:(gi + 1) * gw]))
        return jnp.concatenate(outs, axis=1)

    y_ssd = _rows(group_norm_fwd, [y_core, z], [p["ssd_norm"]], [(SSD_INNER, CDT)], ts=256, name="f_ssd_post")[0]

    y_sb, lt_h, *late = _sb_fwd_pairs(qkv, tuple(shards) if shards is not None else ())
    for n, gth in zip(LATE_W, late):
        wt[n] = _full_weight(n, gth)

    mu = _rows(lambda mv, g: _rms(mv, g), [mem], [p["norm_mem"]], [(d, CDT)], ts=256, name="f_norm_mem")[0]
    kv = _mm(mu, wt["w_mem_kv"], out_dtype=CDT, name="f_mem_kv")
    y_mem = _mem_fwd(mq, kv)

    p_sb = _mm(y_sb, wt["w_sb_out"], name="f_sb_out")
    p_ssd = _mm(y_ssd, wt["w_ssd_out"], name="f_ssd_out")
    p_mem = _mm(y_mem, wt["w_mem_out"], name="f_mem_out")

    def merge_fn(glv, a, b, c):
        return (_sigmoid(glv[:, :d]) * a + _sigmoid(glv[:, d:2 * d]) * b + _sigmoid(glv[:, 2 * d:]) * c)

    merged = _rows(merge_fn, [gl, p_sb, p_ssd, p_mem], [], [(d, CDT)], ts=256, name="f_merge")[0]
    mix = _mm(merged, wt["w_o"], name="f_w_o")

    def mid_fn(xv, mixv, g_post, g_pre):
        h1 = xv + _rms(mixv, g_post)
        return h1, _rms(h1, g_pre)

    h1, u2 = _rows(mid_fn, [x, mix], [p["norm_mix_post"], p["norm_mlp_pre"]], [(d, F32), (d, CDT)],
                   ts=512, name="f_mid")
    a1, act = _mm(u2, wt["w_up"], name="f_up",
                  epilogue=(lambda pv: (pv, jnp.square(jnp.maximum(pv, 0.0))), [], [F32, CDT]))
    ff = _mm(act, wt["w_down"], name="f_down")

    def loss_fn(h1v, ffv, tv, g):
        diff = h1v + _rms(ffv, g) - tv
        tot = jnp.sum(jnp.sum(diff * diff, axis=1, keepdims=True), axis=0, keepdims=True)
        return diff * (1.0 / d), tot

    dh2, loss_acc = _rows(loss_fn, [h1, ff, tgt], [p["norm_mlp_post"]], [(d, F32)], [(1, 128)],
                          ts=512, name="f_loss")
    loss = loss_acc[:, :1] * (0.5 / d)

    sg = {}

    def b_post(ffv, dyv, g):
        dx, dg = _rms_bwd(ffv, g, dyv)
        return dx, dg

    d_ff, sg["norm_mlp_post"] = _rows(b_post, [ff, dh2], [p["norm_mlp_post"]], [(d, CDT)], [(1, d)],
                                      ts=512, name="b_norm_mlp_post")
    da1 = _mm(d_ff, wt["w_down"], tb=True, name="b_down_x",
              epilogue=(lambda pv, a: (pv * 2.0 * jnp.maximum(a, 0.0),), [a1], [CDT]))[0]
    gw = {"w_down": _mm(act, d_ff, ta=True, name="b_down_w")}
    du2 = _mm(da1, wt["w_up"], tb=True, name="b_up_x")
    gw["w_up"] = _mm(u2, da1, ta=True, name="b_up_w")

    def b_mid(h1v, du2v, dh2v, mixv, g_pre, g_post):
        dxa, dga = _rms_bwd(h1v, g_pre, du2v)
        dh1 = dh2v + dxa
        dmix, dgb = _rms_bwd(mixv, g_post, dh1)
        return dh1, dmix, dga, dgb

    dh1, dmix, sg["norm_mlp_pre"], sg["norm_mix_post"] = _rows(
        b_mid, [h1, du2, dh2, mix], [p["norm_mlp_pre"], p["norm_mix_post"]],
        [(d, F32), (d, CDT)], [(1, d), (1, d)], ts=256, name="b_mid")
    dmerged = _mm(dmix, wt["w_o"], tb=True, name="b_w_o_x")
    gw["w_o"] = _mm(merged, dmix, ta=True, name="b_w_o_w")

    def b_merge(dm, glv, a, b, c):
        outs, dgl = [], []
        for i, br in enumerate((a, b, c)):
            gt = _sigmoid(glv[:, i * d:(i + 1) * d])
            outs.append(gt * dm)
            dgl.append(dm * br * gt * (1.0 - gt))
        return outs[0], outs[1], outs[2], jnp.concatenate(dgl, axis=1)

    dp_sb, dp_ssd, dp_mem, dgl = _rows(b_merge, [dmerged, gl, p_sb, p_ssd, p_mem], [],
                                       [(d, CDT), (d, CDT), (d, CDT), (3 * d, CDT)], ts=256, name="b_merge")
    dy_sb = _mm(dp_sb, wt["w_sb_out"], tb=True, name="b_sb_out_x")
    gw["w_sb_out"] = _mm(y_sb, dp_sb, ta=True, name="b_sb_out_w")
    dy_ssd = _mm(dp_ssd, wt["w_ssd_out"], tb=True, name="b_ssd_out_x")
    gw["w_ssd_out"] = _mm(y_ssd, dp_ssd, ta=True, name="b_ssd_out_w")
    dy_mem = _mm(dp_mem, wt["w_mem_out"], tb=True, out_dtype=CDT, name="b_mem_out_x")
    gw["w_mem_out"] = _mm(y_mem, dp_mem, ta=True, name="b_mem_out_w")

    dmq, dk_m, dv_m = _mem_bwd(mq, kv, dy_mem)
    dkv = jnp.concatenate([dk_m, dv_m], axis=1).astype(CDT)
    gw["w_mem_kv"] = _mm(mu, dkv, ta=True, name="b_mem_kv_w")
    dmu = _mm(dkv, wt["w_mem_kv"], tb=True, name="b_mem_kv_x")
    sg["norm_mem"] = _rows(lambda mv, dv, g: _rms_bwd(mv, g, dv)[1], [mem, dmu], [p["norm_mem"]], [], [(1, d)],
                           ts=256, name="b_norm_mem")[0]

    payloads = tuple(_grad_payload(n, gw[n]) for n in LATE_W) if shards is not None else ()
    dq, dk, dv, *received = _sb_bwd_pairs(qkv, lt_h, dy_sb, payloads)
    dqkv = jnp.concatenate([dq, dk, dv], axis=1).astype(CDT)

    def group_norm_bwd(dyo, yv, zv, wn):
        sz = _silu(zv)
        y2 = yv * sz
        gw_ = SSD_INNER // SSD_G
        dy2, dwn = [], []
        for gi in range(SSD_G):
            sl = slice(gi * gw_, (gi + 1) * gw_)
            dxs, dgs = _rms_bwd(y2[:, sl], wn[:, sl], dyo[:, sl])
            dy2.append(dxs)
            dwn.append(dgs)
        dy2 = jnp.concatenate(dy2, axis=1)
        return dy2 * sz, dy2 * yv * _dsilu(zv), jnp.concatenate(dwn, axis=1)

    dy_core, dz, sg["ssd_norm"] = _rows(group_norm_bwd, [dy_ssd, y_core, z], [p["ssd_norm"]],
                                        [(SSD_INNER, F32), (SSD_INNER, CDT)], [(1, SSD_INNER)],
                                        ts=256, name="b_ssd_post")
    dxh, dda, dxx, d_b, d_c, dah, ddsk = _ssd_bwd_g(xh, dtc, dac, dar, dsk, xbc, st, _heads(dy_core, SSD_H, SSD_P))
    sg["d_skip"] = ddsk.reshape(1, SSD_H)
    sg["a_log"] = dah.reshape(1, SSD_H) * (-jnp.exp(p["a_log"]))

    def b_dt(ddav, dxxv, dtr, bias, alog):
        ddt = ddav * (-jnp.exp(alog)) + dxxv
        draw = ddt * _sigmoid(dtr + bias)
        return draw, jnp.sum(draw, axis=0, keepdims=True)

    ungroup = lambda t: _pad_cols(t.transpose(1, 0, 2).reshape(s, SSD_H), DT_PAD)
    ddt_raw, dbias128 = _rows(b_dt, [ungroup(dda), ungroup(dxx), dt_raw], [bias128, alog128],
                              [(DT_PAD, CDT)], [(1, DT_PAD)], ts=512, name="b_dt")
    sg["dt_bias"] = dbias128[:, :SSD_H]

    dxbc = jnp.concatenate([_unheads(dxh), d_b, d_c], axis=1)
    dxbc_raw, sg["conv_w"], sg["conv_b"] = _conv_bwd(xbc_raw, pre, dxbc, conv_w)

    dseg = [dqkv, dz, dxbc_raw, dmq, dgl]
    dw_bufs = [lax.empty((c3, d), F32), lax.empty((c5 - c4 + 3 * d, d), F32)]
    for i in range(5):
        bi = 0 if i < 3 else 1
        dw_bufs[bi] = _mm(dseg[i], u, ta=True, into=dw_bufs[bi], into_off=seg_off[i], into_rows=True,
                          name="b_in_w_" + seg_name[i])
    dw_dt = _mm(ddt_raw, u, ta=True, name="b_in_w_dt")
    def du(i, hosted=None):
        return _mm(dseg[i], seg_src[i], b_koff=seg_off[i], name="b_in_x_" + seg_name[i], hosted=hosted)

    if shards is None:
        dus = [du(i) for i in range(5)]
        gw["w_in"] = jnp.concatenate([dw_bufs[0], dw_dt[:SSD_H], dw_bufs[1]], axis=0).T
    else:
        du0, recv_head = du(0, _pair_swap([dw_bufs[0]]))
        du2, recv_tail, recv_dt = du(2, _pair_swap([dw_bufs[1], dw_dt]))
        tail_rows = dw_bufs[1].shape[0]
        natural = lax.empty((c4 + tail_rows, d), CDT)
        natural = _add_cast_into(dw_bufs[0], recv_head, natural, 0, c3, "sum_w_in_head")
        natural = _add_cast_into(dw_dt, recv_dt, natural, c3, SSD_H, "sum_w_in_dt")
        natural = _add_cast_into(dw_bufs[1], recv_tail, natural, c4, tail_rows, "sum_w_in_tail")
        shard = natural.shape[0] // N_DEV
        du1, win_a = du(1, _chip_scatter_windows(natural, shard, 0, d // 2))
        du4, win_b = du(4, _chip_scatter_windows(natural, shard, d // 2, d // 2))
        gw["w_in"] = [win_a, win_b]
        dus = [du0, du1, du2, du(3), du4]
    dus.append(_mm(ddt_raw, w_dt, name="b_in_x_dt"))

    def b_pre(xv, dh1v, d0, d1, d2, d3, d4, d5, g):
        dx, dg = _rms_bwd(xv, g, d0 + d1 + d2 + d3 + d4 + d5)
        return dh1v + dx, dg

    grad_x, sg["norm_mix_pre"] = _rows(b_pre, [x, dh1] + dus, [p["norm_mix_pre"]], [(d, F32)], [(1, d)],
                                       ts=256, name="b_norm_pre")
    return loss, grad_x, gw, sg, (received if shards is not None else None)


HBM = pl.BlockSpec(memory_space=pltpu.HBM)
MESH = pl.DeviceIdType.MESH


def _me_and_peers():
    x, y, c = lax.axis_index("x"), lax.axis_index("y"), lax.axis_index("c")
    me = 4 * x + 2 * y + c
    peers = [(x, y, 1 - c), (1 - x, y, c), (x, 1 - y, c), (1 - x, 1 - y, c),
             (1 - x, y, 1 - c), (x, 1 - y, 1 - c), (1 - x, 1 - y, 1 - c)]
    return me, peers


def _peer_index(peer):
    return 4 * peer[0] + 2 * peer[1] + peer[2]


def _exchange_copies(ins, outs, send_sems, recv_sems, local_sems, scatter):
    me, peers = _me_and_peers()
    copies = []
    for a in range(len(ins)):
        own = ins[a].at[me] if scatter else ins[a]
        copies.append(pltpu.make_async_copy(own, outs[a].at[me], local_sems.at[a]))
        for kk, peer in enumerate(peers):
            src = ins[a].at[_peer_index(peer)] if scatter else ins[a]
            copies.append(pltpu.make_async_remote_copy(
                src_ref=src, dst_ref=outs[a].at[me],
                send_sem=send_sems.at[a, kk], recv_sem=recv_sems.at[a, kk],
                device_id=peer, device_id_type=MESH))
    return copies


def _exchange_shapes(ins, scatter):
    return [jax.ShapeDtypeStruct(t.shape if scatter else (N_DEV,) + t.shape, t.dtype) for t in ins]


def _exchange_sems(n):
    return [pltpu.SemaphoreType.DMA((n, N_DEV - 1)), pltpu.SemaphoreType.DMA((n, N_DEV - 1)),
            pltpu.SemaphoreType.DMA((n,))]


def _gather_two_level(shards, name):
    n = len(shards)

    def body(*refs):
        ins, outs = refs[:n], refs[n:2 * n]
        send_sems, recv_sems, local_sems = refs[2 * n:]
        x, y, c = lax.axis_index("x"), lax.axis_index("y"), lax.axis_index("c")
        me, sib = (x, y, c), (x, y, 1 - c)
        chips = [(1 - x, y), (x, 1 - y), (1 - x, 1 - y)]

        def copy(a, k, block, to, src=None):
            slot = outs[a].at[_peer_index(block)]
            return pltpu.make_async_remote_copy(
                src_ref=slot if src is None else src, dst_ref=slot,
                send_sem=send_sems.at[a, k], recv_sem=recv_sems.at[a, k], device_id=to, device_id_type=MESH)

        own = [pltpu.make_async_copy(ins[a], outs[a].at[_peer_index(me)], local_sems.at[a]) for a in range(n)]
        first = []
        for a in range(n):
            first.append(copy(a, 0, me, sib, src=ins[a]))
            first += [copy(a, 1 + j, me, (*chip, c), src=ins[a]) for j, chip in enumerate(chips)]
        for cp in own + first:
            cp.start()
        passed = []
        for j, chip in enumerate(chips):
            for a in range(n):
                copy(a, 1 + j, (*chip, c), me).wait_recv()
                fwd = copy(a, 4 + j, (*chip, c), sib)
                fwd.start()
                passed.append(fwd)
        for a in range(n):
            copy(a, 0, sib, me).wait_recv()
            for j, chip in enumerate(chips):
                copy(a, 4 + j, (*chip, 1 - c), me).wait_recv()
        for cp in first + passed:
            cp.wait_send()
        for cp in own:
            cp.wait()

    return pl.pallas_call(
        body, name=name,
        in_specs=[HBM] * n, out_specs=[HBM] * n,
        out_shape=_exchange_shapes(shards, False),
        scratch_shapes=_exchange_sems(n),
        compiler_params=pltpu.CompilerParams(has_side_effects=True),
    )(*shards)


class _Hosted:
    def __init__(self, ins, shapes, sems, copies):
        self.ins, self.shapes, self.sems, self.copies = ins, shapes, sems, copies


def _pair_swap(arrays):
    n = len(arrays)

    def copies(in_refs, out_refs, sems):
        sib = (lax.axis_index("x"), lax.axis_index("y"), 1 - lax.axis_index("c"))
        return [pltpu.make_async_remote_copy(
            src_ref=in_refs[i], dst_ref=out_refs[i], send_sem=sems[0].at[i], recv_sem=sems[1].at[i],
            device_id=sib, device_id_type=MESH) for i in range(n)]

    return _Hosted(list(arrays), [jax.ShapeDtypeStruct(t.shape, t.dtype) for t in arrays],
                   [pltpu.SemaphoreType.DMA((n,)), pltpu.SemaphoreType.DMA((n,))], copies)


def _add_cast_into(a, b, into, row_off, rows, name):
    w = a.shape[1]
    tr = _pick(rows, (512, 256, 128, rows))
    nt = rows // tr
    assert rows % tr == 0 and row_off % ROW_ALIGN == 0 and tr % ROW_ALIGN == 0
    spec = pl.BlockSpec((tr, w), lambda i: (i, 0))

    def body(a_ref, b_ref, into_ref, o_ref, slots, sems):
        i = pl.program_id(0)
        slot = i % 2

        def out_copy(step, s):
            dst = o_ref.at[pl.ds(pl.multiple_of(row_off + step * tr, ROW_ALIGN), tr)]
            return pltpu.make_async_copy(slots.at[s], dst, sems.at[s])

        @pl.when(i >= 2)
        def _():
            out_copy(i - 2, slot).wait()

        slots[slot] = (a_ref[...] + b_ref[...]).astype(slots.dtype)
        out_copy(i, slot).start()

        @pl.when(i == nt - 1)
        def _():
            out_copy(i, slot).wait()
            if nt > 1:
                out_copy(i - 1, 1 - slot).wait()

    return pl.pallas_call(
        body, name=name, grid=(nt,),
        in_specs=[spec, spec, pl.BlockSpec(memory_space=pl.ANY)],
        out_specs=pl.BlockSpec(memory_space=pl.ANY),
        out_shape=jax.ShapeDtypeStruct(into.shape, into.dtype),
        scratch_shapes=[pltpu.VMEM((2, tr, w), into.dtype), pltpu.SemaphoreType.DMA((2,))],
        input_output_aliases={2: 0},
        compiler_params=_cparams(("arbitrary",)),
    )(a, b, into)


N_CHIP = 4


def _adamw_math(g, w, m, v):
    m2 = ADAM_B1 * m + (1.0 - ADAM_B1) * g
    v2 = ADAM_B2 * v + (1.0 - ADAM_B2) * jnp.square(g)
    m_hat = m2 / (1.0 - ADAM_B1 ** ADAM_STEP)
    v_hat = v2 / (1.0 - ADAM_B2 ** ADAM_STEP)
    delta = -ADAM_LR * (m_hat / (jnp.sqrt(v_hat) + ADAM_EPS) + ADAM_WD * w)
    return delta, m2, v2


def _adamw_reduce(parts, w, m, v, name):
    r, c = w.shape
    nparts = parts.shape[0]
    tr = _pick(r, (128, 64, 32, 16, 8))

    def body(p_ref, w_ref, m_ref, v_ref, g_ref, d_ref, m2_ref, v2_ref):
        g = p_ref[0].astype(F32)
        for i in range(1, nparts):
            g = g + p_ref[i].astype(F32)
        delta, m2, v2 = _adamw_math(g, w_ref[...], m_ref[...], v_ref[...])
        g_ref[...] = g
        d_ref[...] = delta
        m2_ref[...] = m2
        v2_ref[...] = v2

    tile = pl.BlockSpec((tr, c), lambda i: (i, 0))
    shp = jax.ShapeDtypeStruct((r, c), F32)
    return pl.pallas_call(
        body, name=name, grid=(r // tr,),
        in_specs=[pl.BlockSpec((nparts, tr, c), lambda i: (0, i, 0)), tile, tile, tile],
        out_specs=[tile] * 4, out_shape=[shp] * 4,
        compiler_params=_cparams(("parallel",)),
    )(parts, w, m, v)


ROW_ALIGN = 16


def _window(shard):
    lead = max((j * shard) % ROW_ALIGN for j in range(N_DEV))
    return -(-(lead + shard) // ROW_ALIGN) * ROW_ALIGN


def _chip_scatter_windows(t, shard, col0, cols):
    win = _window(shard)
    assert all((j * shard // ROW_ALIGN) * ROW_ALIGN + win <= t.shape[0] for j in range(N_DEV))

    def copies(in_refs, out_refs, sems):
        (t_ref,), (o_ref,), (send_sems, recv_sems, local_sem) = in_refs, out_refs, sems
        x, y, c = lax.axis_index("x"), lax.axis_index("y"), lax.axis_index("c")
        mine = 2 * x + y

        def window(q):
            a0 = pl.multiple_of(((2 * q + c) * shard // ROW_ALIGN) * ROW_ALIGN, ROW_ALIGN)
            return t_ref.at[pl.ds(a0, win), pl.ds(col0, cols)]

        res = [pltpu.make_async_copy(window(mine), o_ref.at[mine], local_sem)]
        for j, (px, py) in enumerate([(1 - x, y), (x, 1 - y), (1 - x, 1 - y)]):
            res.append(pltpu.make_async_remote_copy(
                src_ref=window(2 * px + py), dst_ref=o_ref.at[mine],
                send_sem=send_sems.at[j], recv_sem=recv_sems.at[j], device_id=(px, py, c), device_id_type=MESH))
        return res

    return _Hosted([t], [jax.ShapeDtypeStruct((N_CHIP, win, cols), t.dtype)],
                   [pltpu.SemaphoreType.DMA((N_CHIP - 1,)), pltpu.SemaphoreType.DMA((N_CHIP - 1,)),
                    pltpu.SemaphoreType.DMA], copies)


ADAMW_TC = 256


def _adamw_windows(parts, w, m, v, name):
    r, c = w.shape
    na = len(parts)
    nparts, win, cpart = parts[0].shape
    tc = min(ADAMW_TC, cpart)
    per = cpart // tc

    def body(*refs):
        p_refs = refs[:na]
        w_ref, m_ref, v_ref, g_ref, d_ref, m2_ref, v2_ref = refs[na:]
        me, _ = _me_and_peers()
        step = pl.program_id(0)
        gw_ = None
        for a, p_ref in enumerate(p_refs):
            tot = p_ref[0].astype(F32)
            for i in range(1, nparts):
                tot = tot + p_ref[i].astype(F32)
            gw_ = tot if gw_ is None else jnp.where(step // per == a, tot, gw_)
        for j in range(N_DEV):
            @pl.when(me == j)
            def _():
                lead = (j * r) % ROW_ALIGN
                g = (pltpu.roll(gw_, win - lead, 0) if lead else gw_)[:r]
                delta, m2, v2 = _adamw_math(g, w_ref[...], m_ref[...], v_ref[...])
                g_ref[...] = g
                d_ref[...] = delta
                m2_ref[...] = m2
                v2_ref[...] = v2

    tile = pl.BlockSpec((r, tc), lambda i: (0, i))
    shp = jax.ShapeDtypeStruct((r, c), F32)
    return pl.pallas_call(
        body, name=name, grid=(c // tc,),
        in_specs=[pl.BlockSpec((nparts, win, tc), lambda i, a=a: (0, 0, jnp.clip(i - a * per, 0, per - 1)))
                  for a in range(na)] + [tile, tile, tile],
        out_specs=[tile] * 4, out_shape=[shp] * 4,
        compiler_params=_cparams(("parallel",)),
    )(*parts, w, m, v)


SMALL_ROWS, SMALL_COLS = 16, 3072


def _small_step(sg, gcw, loss, ws, ms, vs):
    ns = len(sg)
    widths = [t.shape[1] for t in sg]
    kk_, ch = gcw.shape[1], gcw.shape[2]
    assert ns < SMALL_ROWS and max(widths) <= SMALL_COLS

    def reduce_body(*refs):
        g_refs = refs[:ns]
        gcw_ref, loss_ref, tot_ref, totc_ref = refs[ns:ns + 4]
        mine, buf, minec, bufc, send_sems, recv_sems = refs[ns + 4:]
        me, peers = _me_and_peers()

        mine[...] = jnp.zeros_like(mine)
        for i in range(ns):
            mine[i:i + 1, 0:widths[i]] = g_refs[i][...]
        mine[ns:ns + 1, 0:LANES] = jnp.broadcast_to(loss_ref[...], (1, LANES))
        minec[...] = gcw_ref[...]
        buf[me] = mine[...]
        bufc[me] = minec[...]
        copies = []
        for j, peer in enumerate(peers):
            copies.append(pltpu.make_async_remote_copy(
                src_ref=mine, dst_ref=buf.at[me], send_sem=send_sems.at[0, j], recv_sem=recv_sems.at[0, j],
                device_id=peer, device_id_type=MESH))
            copies.append(pltpu.make_async_remote_copy(
                src_ref=minec, dst_ref=bufc.at[me], send_sem=send_sems.at[1, j], recv_sem=recv_sems.at[1, j],
                device_id=peer, device_id_type=MESH))
        for cp in copies:
            cp.start()
        for cp in copies:
            cp.wait()
        tot = buf[0]
        totc = bufc[0]
        for i in range(1, N_DEV):
            tot = tot + buf[i]
            totc = totc + bufc[i]
        tot_ref[...] = tot
        totc_ref[...] = totc

    vm = pl.BlockSpec(memory_space=pltpu.VMEM)
    tot, totc = pl.pallas_call(
        reduce_body, name="small_reduce",
        in_specs=[vm] * (ns + 2), out_specs=[vm, vm],
        out_shape=[jax.ShapeDtypeStruct((SMALL_ROWS, SMALL_COLS), F32), jax.ShapeDtypeStruct((N_DEV, kk_, ch), F32)],
        scratch_shapes=[pltpu.VMEM((SMALL_ROWS, SMALL_COLS), F32), pltpu.VMEM((N_DEV, SMALL_ROWS, SMALL_COLS), F32),
                        pltpu.VMEM((N_DEV, kk_, ch), F32), pltpu.VMEM((N_DEV, N_DEV, kk_, ch), F32),
                        pltpu.SemaphoreType.DMA((2, N_DEV - 1)), pltpu.SemaphoreType.DMA((2, N_DEV - 1))],
        compiler_params=pltpu.CompilerParams(has_side_effects=True),
    )(*sg, gcw, loss)

    def adamw_body(*refs):
        tot_ref, totc_ref = refs[:2]
        w_refs, m_refs, v_refs = (refs[2 + i * (ns + 1):2 + (i + 1) * (ns + 1)] for i in range(3))
        outs = refs[3 * ns + 5:]
        loss_out = outs[0]
        go, do_, mo, vo = (outs[1 + i * (ns + 1):1 + (i + 1) * (ns + 1)] for i in range(4))
        me, _ = _me_and_peers()
        loss_out[...] = tot_ref[ns:ns + 1, 0:1]
        for i in range(ns + 1):
            g = tot_ref[i:i + 1, 0:widths[i]] if i < ns else totc_ref[me]
            delta, m2, v2 = _adamw_math(g, w_refs[i][...], m_refs[i][...], v_refs[i][...])
            go[i][...] = g
            do_[i][...] = delta
            mo[i][...] = m2
            vo[i][...] = v2

    shapes = [jax.ShapeDtypeStruct(t.shape, F32) for t in ws]
    res = pl.pallas_call(
        adamw_body, name="small_adamw",
        in_specs=[vm] * (3 * ns + 5), out_specs=[vm] * (4 * ns + 5),
        out_shape=[jax.ShapeDtypeStruct((1, 1), F32)] + shapes * 4,
    )(tot, totc, *ws, *ms, *vs)
    n1 = ns + 1
    return res[0], res[1:1 + n1], res[1 + n1:1 + 2 * n1], res[1 + 2 * n1:1 + 3 * n1], res[1 + 3 * n1:]


def _cast_shard(w, name):
    r = w.shape[0]
    return _rows(lambda t: t, [w], [], [(w.shape[1], CDT)], ts=_pick(r, (256, 128)), name=name)[0]


BIG = ["w_in", "w_mem_kv", "w_up", "w_sb_out", "w_ssd_out", "w_mem_out", "w_o", "w_down"]
LATE_W = BIG[1:]
COL_SHARDED = ("w_in", "w_mem_kv", "w_up")
SMALL = ["norm_mix_pre", "conv_b", "dt_bias", "a_log", "d_skip", "ssd_norm", "norm_mem",
         "norm_mix_post", "norm_mlp_pre", "norm_mlp_post"]
ALL_W = ["norm_mix_pre", "w_in", "conv_w", "conv_b", "dt_bias", "a_log", "d_skip", "ssd_norm", "norm_mem",
         "w_mem_kv", "w_sb_out", "w_ssd_out", "w_mem_out", "w_o", "norm_mix_post", "norm_mlp_pre", "w_up",
         "w_down", "norm_mlp_post"]
LANES = 128


def kernel(x, mem, norm_mix_pre, w_in, conv_w, conv_b, dt_bias, a_log, d_skip, ssd_norm, norm_mem, w_mem_kv, w_sb_out, w_ssd_out, w_mem_out, w_o, norm_mix_post, norm_mlp_pre, w_up, w_down, norm_mlp_post, loss_target, m_norm_mix_pre, m_w_in, m_conv_w, m_conv_b, m_dt_bias, m_a_log, m_d_skip, m_ssd_norm, m_norm_mem, m_w_mem_kv, m_w_sb_out, m_w_ssd_out, m_w_mem_out, m_w_o, m_norm_mix_post, m_norm_mlp_pre, m_w_up, m_w_down, m_norm_mlp_post, v_norm_mix_pre, v_w_in, v_conv_w, v_conv_b, v_dt_bias, v_a_log, v_d_skip, v_ssd_norm, v_norm_mem, v_w_mem_kv, v_w_sb_out, v_w_ssd_out, v_w_mem_out, v_w_o, v_norm_mix_post, v_norm_mlp_pre, v_w_up, v_w_down, v_norm_mlp_post):
    wd = dict(norm_mix_pre=norm_mix_pre, w_in=w_in, conv_w=conv_w, conv_b=conv_b, dt_bias=dt_bias, a_log=a_log,
              d_skip=d_skip, ssd_norm=ssd_norm, norm_mem=norm_mem, w_mem_kv=w_mem_kv, w_sb_out=w_sb_out,
              w_ssd_out=w_ssd_out, w_mem_out=w_mem_out, w_o=w_o, norm_mix_post=norm_mix_post,
              norm_mlp_pre=norm_mlp_pre, w_up=w_up, w_down=w_down, norm_mlp_post=norm_mlp_post)
    md = dict(norm_mix_pre=m_norm_mix_pre, w_in=m_w_in, conv_w=m_conv_w, conv_b=m_conv_b, dt_bias=m_dt_bias,
              a_log=m_a_log, d_skip=m_d_skip, ssd_norm=m_ssd_norm, norm_mem=m_norm_mem, w_mem_kv=m_w_mem_kv,
              w_sb_out=m_w_sb_out, w_ssd_out=m_w_ssd_out, w_mem_out=m_w_mem_out, w_o=m_w_o,
              norm_mix_post=m_norm_mix_post, norm_mlp_pre=m_norm_mlp_pre, w_up=m_w_up, w_down=m_w_down,
              norm_mlp_post=m_norm_mlp_post)
    vd = dict(norm_mix_pre=v_norm_mix_pre, w_in=v_w_in, conv_w=v_conv_w, conv_b=v_conv_b, dt_bias=v_dt_bias,
              a_log=v_a_log, d_skip=v_d_skip, ssd_norm=v_ssd_norm, norm_mem=v_norm_mem, w_mem_kv=v_w_mem_kv,
              w_sb_out=v_w_sb_out, w_ssd_out=v_w_ssd_out, w_mem_out=v_w_mem_out, w_o=v_w_o,
              norm_mix_post=v_norm_mix_post, norm_mlp_pre=v_norm_mlp_pre, w_up=v_w_up, w_down=v_w_down,
              norm_mlp_post=v_norm_mlp_post)
    w_in_t, m_in_t, v_in_t = (t["w_in"][0].T for t in (wd, md, vd))
    shards = {n: _cast_shard(w_in_t if n == "w_in" else wd[n][0], "cast_" + n) for n in BIG}
    w_in_g, conv_w_g = _gather_two_level([shards["w_in"], wd["conv_w"][0]], "gather_w_in")
    wt = {"w_in": w_in_g.reshape(N_DEV * w_in_g.shape[1], w_in_g.shape[2])}
    ch = conv_w_g.shape[2]

    p = {n: wd[n] for n in SMALL}
    p["conv_w"] = conv_w_g.transpose(1, 0, 2).reshape(CONV_K, N_DEV * ch)
    loss, grad_x, gw, sg, late_received = _local_step(x[0], mem[0], loss_target[0], p, wt,
                                                      [shards[n] for n in LATE_W])

    received = dict(zip(LATE_W, late_received))
    w_in_windows = gw["w_in"]

    grads, deltas, new_m, new_v = {}, {}, {}, {}
    for n in BIG:
        if n == "w_in":
            res = [t.T for t in _adamw_windows(w_in_windows, w_in_t, m_in_t, v_in_t, "adamw_" + n)]
        else:
            res = _adamw_reduce(received[n], wd[n][0], md[n][0], vd[n][0], "adamw_" + n)
        grads[n], deltas[n], new_m[n], new_v[n] = (t[None] for t in res)
    small_names = SMALL + ["conv_w"]
    gcw = sg["conv_w"].reshape(CONV_K, N_DEV, ch).transpose(1, 0, 2)
    small_of = lambda dct: [dct[n] for n in SMALL] + [dct["conv_w"][0]]
    loss_red, g_s, d_s, m_s, v_s = _small_step([sg[n] for n in SMALL], gcw, loss, small_of(wd), small_of(md),
                                               small_of(vd))
    for i, n in enumerate(small_names):
        shape = wd[n].shape
        grads[n], deltas[n], new_m[n], new_v[n] = (t.reshape(shape) for t in (g_s[i], d_s[i], m_s[i], v_s[i]))
    loss_out = loss_red.reshape(())

    return (loss_out, grad_x[None], *[grads[n] for n in ALL_W], *[deltas[n] for n in ALL_W],
            *[new_m[n] for n in ALL_W], *[new_v[n] for n in ALL_W])
```

```python
import functools

import jax
import jax.numpy as jnp
from jax import lax
from jax.experimental import pallas as pl
from jax.experimental.pallas import tpu as pltpu

F32 = jnp.float32
BF16 = jnp.bfloat16
CDT = jnp.bfloat16
EPS = 1e-6
VMEM_LIMIT = 56 * 1024 * 1024

N_DEV = 8
D_MODEL = 1024
SB_H, SB_DH = 16, 64
SSD_G, SSD_R, SSD_P, SSD_N, SSD_L = 4, 8, 64, 128, 128
SSD_H = SSD_G * SSD_R
SSD_INNER = SSD_H * SSD_P
CONV_K = 4
CONV_DIM = SSD_INNER + 2 * SSD_G * SSD_N
MEM_H, MEM_DH = 4, 256
DT_PAD = 128
SB_TQ, SB_BK = 2048, 256
CONV_PAD = 8
MM_TILE, MM_TILE_K = 1024, 2048

ADAM_LR, ADAM_B1, ADAM_B2, ADAM_EPS, ADAM_WD, ADAM_STEP = 0.001, 0.9, 0.999, 1e-08, 0.01, 10

NT = (((1,), (1,)), ((), ()))
TN = (((0,), (0,)), ((), ()))
NN = (((1,), (0,)), ((), ()))


def _cparams(sem=None):
    return pltpu.CompilerParams(dimension_semantics=sem, vmem_limit_bytes=VMEM_LIMIT)


def _pick(n, cands):
    for c in cands:
        if n % c == 0:
            return c
    return n


def _dot(a, b, dims=NN):
    return lax.dot_general(a.astype(CDT), b.astype(CDT), dims, preferred_element_type=F32)


def _split_dot(x, t, left, pieces):
    if CDT == F32:
        return lax.dot_general(t, x, NN, preferred_element_type=F32) if left else \
            lax.dot_general(x, t, NN, preferred_element_type=F32)
    acc = None
    rem = x
    for _ in range(pieces):
        hi = rem.astype(BF16)
        rem = rem - hi.astype(F32)
        d = lax.dot_general(t, hi, NN, preferred_element_type=F32) if left else \
            lax.dot_general(hi, t, NN, preferred_element_type=F32)
        acc = d if acc is None else acc + d
    return acc


def _iota(shape, dim):
    return lax.broadcasted_iota(jnp.int32, shape, dim)


def _sigmoid(x):
    return 1.0 / (1.0 + jnp.exp(-x))


def _silu(x):
    return x * _sigmoid(x)


def _dsilu(x):
    s = _sigmoid(x)
    return s * (1.0 + x * (1.0 - s))


def _softplus(x):
    return jnp.maximum(x, 0.0) + jnp.log(1.0 + jnp.exp(-jnp.abs(x)))


def _rms(x, g):
    r = lax.rsqrt(jnp.mean(x * x, axis=-1, keepdims=True) + EPS)
    return x * r * g


def _rms_bwd(x, g, dy):
    r = lax.rsqrt(jnp.mean(x * x, axis=-1, keepdims=True) + EPS)
    n = x * r
    dn = dy * g
    dx = r * (dn - n * jnp.mean(dn * n, axis=-1, keepdims=True))
    dg = jnp.sum(dy * n, axis=0, keepdims=True)
    return dx, dg


def _mm(a, b, *, ta=False, tb=False, out_dtype=F32, name, b_off=0, n=None, b_koff=0, into=None, into_off=0,
        into_rows=False, hosted=None, epilogue=None):
    m = a.shape[1] if ta else a.shape[0]
    k = a.shape[0] if ta else a.shape[1]
    if n is None:
        n = b.shape[0] if tb else b.shape[1]
    assert b_koff + k <= (b.shape[1] if tb else b.shape[0])
    bm = _pick(m, (MM_TILE, 512, 256, 128))
    bn = _pick(n, (MM_TILE, 512, 256, 128))
    bk = next(c for c in (MM_TILE_K, 1024, 512, 256, 128, k) if k % c == 0 and b_koff % c == 0)
    nk = k // bk
    assert b_off % bn == 0 and into_off % (bm if into_rows else bn) == 0
    jb, kb = b_off // bn, b_koff // bk
    io, jo = (into_off // bm, 0) if into_rows else (0, into_off // bn)
    dims = (((0 if ta else 1,), (1 if tb else 0,)), ((), ()))
    grid = (m // bm, n // bn, nk)
    off = 1 if into is not None else 0
    nh_in = len(hosted.ins) if hosted else 0
    nh_out = len(hosted.shapes) if hosted else 0
    epi_fn, extras, out_dtypes = epilogue if epilogue else (lambda p: (p,), [], [out_dtype])
    ne, no = len(extras), len(out_dtypes)
    assert not (epilogue and into is not None)

    def body(a_ref, b_ref, *rest):
        e_refs = rest[off:off + ne]
        o_refs = rest[off + ne + nh_in:off + ne + nh_in + no]
        acc_ref = rest[off + ne + nh_in + no + nh_out]

        def emit(total):
            res = epi_fn(total, *[e[...] for e in e_refs])
            for o_ref, val in zip(o_refs, res):
                o_ref[...] = val.astype(o_ref.dtype)

        if hosted:
            h_refs = (rest[off + ne:off + ne + nh_in],
                      rest[off + ne + nh_in + no:off + ne + nh_in + no + nh_out],
                      rest[off + ne + nh_in + no + nh_out + 1:])
            step = (pl.program_id(0) * grid[1] + pl.program_id(1)) * grid[2] + pl.program_id(2)

            @pl.when(step == 0)
            def _():
                for cp in hosted.copies(*h_refs):
                    cp.start()

        part = _dot(a_ref[...], b_ref[...], dims)
        if nk == 1:
            emit(part)
        else:
            kk = pl.program_id(2)

            @pl.when(kk == 0)
            def _():
                acc_ref[...] = part

            @pl.when(jnp.logical_and(kk > 0, kk < nk - 1))
            def _():
                acc_ref[...] += part

            @pl.when(kk == nk - 1)
            def _():
                emit(acc_ref[...] + part)

        if hosted:
            @pl.when(step == grid[0] * grid[1] * grid[2] - 1)
            def _():
                for cp in hosted.copies(*h_refs):
                    cp.wait()

    a_spec = pl.BlockSpec((bk, bm), lambda i, j, kk: (kk, i)) if ta else \
        pl.BlockSpec((bm, bk), lambda i, j, kk: (i, kk))
    b_spec = pl.BlockSpec((bn, bk), lambda i, j, kk: (j + jb, kk + kb)) if tb else \
        pl.BlockSpec((bk, bn), lambda i, j, kk: (kk + kb, j + jb))
    extra = {} if into is None else {"input_output_aliases": {2: 0}}
    out_shapes = [jax.ShapeDtypeStruct((m, n), dt) for dt in out_dtypes] if into is None else \
        [jax.ShapeDtypeStruct(into.shape, into.dtype)]
    block = pl.BlockSpec((bm, bn), lambda i, j, kk: (i, j))
    res = pl.pallas_call(
        body, name=name, grid=grid,
        in_specs=[a_spec, b_spec] + ([] if into is None else [pl.BlockSpec(memory_space=pl.ANY)])
        + [block] * ne + [HBM] * nh_in,
        out_specs=[pl.BlockSpec((bm, bn), lambda i, j, kk: (i + io, j + jo))] * no + [HBM] * nh_out,
        out_shape=out_shapes + (list(hosted.shapes) if hosted else []),
        scratch_shapes=[pltpu.VMEM((bm, bn) if nk > 1 else (8, 128), F32)] + (list(hosted.sems) if hosted else []),
        compiler_params=_cparams(("arbitrary",) * 3 if hosted else ("parallel", "parallel", "arbitrary")),
        **extra,
    )(*((a, b) if into is None else (a, b, into)), *extras, *(hosted.ins if hosted else ()))
    return res if (hosted or epilogue) else res[0]


def _rows(fn, tiled, params, outs, accs=(), *, ts, name):
    s = tiled[0].shape[0]
    ts = min(ts, s)
    assert s % ts == 0
    nt, npar, no, na = len(tiled), len(params), len(outs), len(accs)

    def body(*refs):
        i = pl.program_id(0)
        vals = [r[...] for r in refs[:nt + npar]]
        res = fn(*vals)
        if not isinstance(res, (tuple, list)):
            res = (res,)
        orefs = refs[nt + npar:nt + npar + no]
        arefs = refs[nt + npar + no:]
        for r_, val in zip(orefs, res[:no]):
            r_[...] = val.astype(r_.dtype)
        if na:
            @pl.when(i == 0)
            def _():
                for r_ in arefs:
                    r_[...] = jnp.zeros_like(r_)

            for r_, val in zip(arefs, res[no:]):
                r_[...] += jnp.broadcast_to(val, r_.shape)

    in_specs = [pl.BlockSpec((ts, a.shape[1]), lambda i: (i, 0)) for a in tiled]
    in_specs += [pl.BlockSpec(p.shape, lambda i: (0, 0)) for p in params]
    out_specs = [pl.BlockSpec((ts, w), lambda i: (i, 0)) for (w, _) in outs]
    out_specs += [pl.BlockSpec(shape, lambda i: (0, 0)) for shape in accs]
    out_shape = [jax.ShapeDtypeStruct((s, w), dt) for (w, dt) in outs]
    out_shape += [jax.ShapeDtypeStruct(shape, F32) for shape in accs]
    res = pl.pallas_call(
        body, name=name, grid=(s // ts,),
        in_specs=in_specs, out_specs=out_specs, out_shape=out_shape,
        compiler_params=_cparams(("arbitrary",)),
    )(*tiled, *params)
    return res


def _sb_block(qs, kb, diag):
    tq, bk = qs.shape[0], kb.shape[0]
    z = _dot(qs, kb, NT)
    lb = jnp.minimum(z, 0.0) - jnp.log(1.0 + jnp.exp(-jnp.abs(z)))
    lk = lb - z
    if diag is None:
        return None, lb, lk
    causal = (diag + _iota((tq, bk), 1)) < _iota((tq, bk), 0)
    return causal, lb, jnp.where(causal, lk, 0.0)


def _fused_exchange(scatter, ncols, nsteps):
    def hooks(ins, outs, sems):
        step = pl.program_id(0) * ncols + pl.program_id(1)

        @pl.when(step == 0)
        def _():
            for cp in _exchange_copies(ins, outs, *sems, scatter):
                cp.start()

        def finish():
            @pl.when(step == nsteps - 1)
            def _():
                for cp in _exchange_copies(ins, outs, *sems, scatter):
                    cp.wait()
        return finish
    return hooks


SB_PAIR = 128


def _sb_fwd_pairs(qkv, comm=()):
    s, d3 = qkv.shape
    d = d3 // 3
    npair = d // SB_PAIR
    tq, bk = min(SB_TQ, s), min(SB_BK, s)
    scale = SB_DH ** -0.5
    nc = len(comm)
    hooks = _fused_exchange(False, s // tq, npair * (s // tq))

    def body(q_ref, k_ref, v_ref, *rest):
        y_ref, lt_ref = rest[nc:nc + 2]
        if nc:
            finish = hooks(rest[:nc], rest[nc + 2:2 * nc + 2], rest[2 * nc + 2:])
        i = pl.program_id(1)
        q0 = i * tq
        q2 = q_ref[...] * scale
        lane_head = (_iota((1, SB_PAIR), 1) >= SB_DH).astype(jnp.int32)
        tri = (_iota((bk, bk), 0) > _iota((bk, bk), 1)).astype(CDT)

        def head(hh, y):
            mine = lane_head == hh
            qs = jnp.where(mine, q2, jnp.zeros_like(q2))

            def step(k0, carry, diag, r0=0):
                cf, acc = carry
                kb = k_ref[pl.ds(k0, bk), :]
                vb = v_ref[pl.ds(k0, bk), :]
                causal, lb, lk = _sb_block(qs[r0:], kb, diag)
                w = jnp.exp(lb + cf + _split_dot(lk, tri, False, 2))
                if causal is not None:
                    w = jnp.where(causal, w, 0.0)
                return cf + jnp.sum(lk, axis=1, keepdims=True), acc + _dot(w, vb)

            carry = (jnp.zeros((tq, 1), F32), jnp.zeros((tq, SB_PAIR), F32))
            for dd in reversed(range(tq // bk)):
                r0 = dd * bk
                sub = step(pl.multiple_of(q0 + r0, bk), tuple(t[r0:] for t in carry), 0, r0)
                carry = tuple(jnp.concatenate([t[:r0], u], axis=0) if r0 else u for t, u in zip(carry, sub))
            nfull = q0 // bk
            cf, acc = lax.fori_loop(
                0, nfull, lambda jj, c: step(pl.multiple_of((nfull - 1 - jj) * bk, bk), c, None), carry)
            lt_ref[hh] = cf
            return jnp.where(mine, acc, y)

        y_ref[...] = lax.fori_loop(0, 2, head, jnp.zeros((tq, SB_PAIR), F32)).astype(y_ref.dtype)
        if nc:
            finish()

    return pl.pallas_call(
        body, name="sb_fwd", grid=(npair, s // tq),
        in_specs=[pl.BlockSpec((tq, SB_PAIR), lambda a, i: (i, a)),
                  pl.BlockSpec((s, SB_PAIR), lambda a, i: (0, npair + a)),
                  pl.BlockSpec((s, SB_PAIR), lambda a, i: (0, 2 * npair + a))] + [HBM] * nc,
        out_specs=[pl.BlockSpec((tq, SB_PAIR), lambda a, i: (i, a)),
                   pl.BlockSpec((2, tq, 1), lambda a, i: (a, i, 0))] + [HBM] * nc,
        out_shape=[jax.ShapeDtypeStruct((s, d), CDT), jax.ShapeDtypeStruct((2 * npair, s, 1), F32)]
        + _exchange_shapes(comm, False),
        scratch_shapes=_exchange_sems(nc) if nc else [],
        compiler_params=_cparams(("arbitrary", "arbitrary")),
    )(qkv, qkv, qkv, *comm)


def _sb_bwd_pairs(qkv, ltot, dy, comm=()):
    s, d3 = qkv.shape
    d = d3 // 3
    npair = d // SB_PAIR
    tq, bk = min(SB_TQ, s), min(SB_BK, s)
    scale = SB_DH ** -0.5
    nc = len(comm)
    hooks = _fused_exchange(True, s // tq, npair * (s // tq))

    def body(q_ref, k_ref, v_ref, lt_ref, dy_ref, *rest):
        dq_ref, dk_ref, dv_ref = rest[nc:nc + 3]
        if nc:
            finish = hooks(rest[:nc], rest[nc + 3:2 * nc + 3], rest[2 * nc + 3:])
        i = pl.program_id(1)

        @pl.when(i == 0)
        def _():
            dk_ref[...] = jnp.zeros_like(dk_ref)
            dv_ref[...] = jnp.zeros_like(dv_ref)

        q0 = i * tq
        q2 = q_ref[...] * scale
        do2 = dy_ref[...].astype(CDT)
        lane_head = (_iota((1, SB_PAIR), 1) >= SB_DH).astype(jnp.int32)
        tri_le = (_iota((bk, bk), 0) <= _iota((bk, bk), 1)).astype(CDT)
        tri_lt = (_iota((bk, bk), 0) < _iota((bk, bk), 1)).astype(CDT)

        def head(hh, dq_all):
            mine = lane_head == hh
            qs = jnp.where(mine, q2, jnp.zeros_like(q2))
            dov = jnp.where(mine, do2, jnp.zeros_like(do2))
            ltot_h = lt_ref[hh]

            def step(k0, carry, diag, r0=0):
                cf, cg, dq = carry
                kb = k_ref[pl.ds(k0, bk), :]
                vb = v_ref[pl.ds(k0, bk), :]
                causal, lb, lk = _sb_block(qs[r0:], kb, diag)
                w = jnp.exp(lb + ltot_h[r0:] - (cf + _split_dot(lk, tri_le, False, 2)))
                if causal is not None:
                    w = jnp.where(causal, w, 0.0)
                g = w * _dot(dov[r0:], vb, NT)
                gsum = cg + _split_dot(g, tri_lt, False, 2)
                dz = g - (g + gsum) * jnp.exp(lb)
                if causal is not None:
                    dz = jnp.where(causal, dz, 0.0)
                dzc = dz.astype(CDT)
                dk_ref[pl.ds(k0, bk), :] += _dot(dzc, qs[r0:], TN)
                dv_ref[pl.ds(k0, bk), :] += _dot(w, dov[r0:], TN)
                kbm = jnp.where(mine, kb, jnp.zeros_like(kb))
                return (cf + jnp.sum(lk, axis=1, keepdims=True), cg + jnp.sum(g, axis=1, keepdims=True),
                        dq + _dot(dzc, kbm))

            carry = (jnp.zeros((tq, 1), F32), jnp.zeros((tq, 1), F32), jnp.zeros((tq, SB_PAIR), F32))
            carry = lax.fori_loop(0, q0 // bk, lambda jj, c: step(pl.multiple_of(jj * bk, bk), c, None), carry)
            for dd in range(tq // bk):
                r0 = dd * bk
                sub = step(pl.multiple_of(q0 + r0, bk), tuple(t[r0:] for t in carry), 0, r0)
                carry = tuple(jnp.concatenate([t[:r0], u], axis=0) if r0 else u for t, u in zip(carry, sub))
            return dq_all + carry[2]

        dq_ref[...] = lax.fori_loop(0, 2, head, jnp.zeros((tq, SB_PAIR), F32)) * scale
        if nc:
            finish()

    tile = pl.BlockSpec((tq, SB_PAIR), lambda a, i: (i, a))
    acc = pl.BlockSpec((s, SB_PAIR), lambda a, i: (0, a))
    shp = jax.ShapeDtypeStruct((s, d), F32)
    return pl.pallas_call(
        body, name="sb_bwd", grid=(npair, s // tq),
        in_specs=[tile, pl.BlockSpec((s, SB_PAIR), lambda a, i: (0, npair + a)),
                  pl.BlockSpec((s, SB_PAIR), lambda a, i: (0, 2 * npair + a)),
                  pl.BlockSpec((2, tq, 1), lambda a, i: (a, i, 0)), tile] + [HBM] * nc,
        out_specs=[tile, acc, acc] + [HBM] * nc,
        out_shape=[shp, shp, shp] + _exchange_shapes(comm, True),
        scratch_shapes=_exchange_sems(nc) if nc else [],
        compiler_params=_cparams(("arbitrary", "arbitrary")),
    )(qkv, qkv, qkv, ltot, dy, *comm)


def _pick_lane(tile, r):
    return jnp.sum(jnp.where(_iota(tile.shape, 1) == r, tile, 0.0), axis=1, keepdims=True)


def _pick_row(tile, r):
    return jnp.sum(jnp.where(_iota(tile.shape, 0) == r, tile, 0.0), axis=0, keepdims=True)


SSD_CPS = 4


def _ssd_chunk_common(cv, bv, dac, dar):
    l = SSD_L
    tdt = F32 if CDT == F32 else BF16
    lower = (_iota((l, l), 1) <= _iota((l, l), 0)).astype(tdt)
    upper = (_iota((l, l), 0) <= _iota((l, l), 1)).astype(tdt)
    return _dot(cv, bv, NT), _split_dot(dac, lower, True, 3), _split_dot(dar, upper, False, 3)


def _ssd_fwd_g(xh, dtc, dac, dar, dsk, xbc):
    hh, s, p = xh.shape
    l, n, g_, r_ = SSD_L, SSD_N, SSD_G, SSD_R
    nc = s // l
    cps = SSD_CPS if nc % SSD_CPS == 0 else 1
    lb = cps * l
    boff = SSD_INNER // n
    coff = boff + g_

    def body(x_ref, dtc_ref, dac_ref, dar_ref, dsk_ref, b_ref, c_ref, y_ref, st_ref, state_ref):
        c = pl.program_id(1)

        @pl.when(c == 0)
        def _():
            state_ref[...] = jnp.zeros_like(state_ref)

        mask = _iota((l, l), 1) <= _iota((l, l), 0)
        dskv = dsk_ref[...]
        for cc in range(cps):
            rows = pl.ds(cc * l, l)
            cv, bv, dtcv = c_ref[rows, :], b_ref[rows, :], dtc_ref[rows, :]
            cb, acs_c, acs_r = _ssd_chunk_common(cv, bv, dac_ref[rows, :], dar_ref[:, rows])
            for r in range(r_):
                a_col = _pick_lane(acs_c, r)
                a_row = _pick_row(acs_r, r)
                dsk_h = _pick_lane(dskv, r)
                a_b = jnp.broadcast_to(a_col, (l, l))
                dt_b = jnp.broadcast_to(_pick_lane(dtcv, r), (l, p))
                xv = x_ref[r, rows, :]
                xd = xv * dt_b
                decay = jnp.where(mask, jnp.exp(jnp.minimum(a_b - a_row, 0.0)), 0.0)
                hprev = state_ref[r]
                y = _dot(cb * decay, xd) + jnp.exp(a_b)[:, :p] * _dot(cv, hprev)
                y_ref[r, rows, :] = y + dsk_h * xv
                a_end = a_col[l - 1:l, :]
                st_ref[r, cc] = hprev
                state_ref[r] = hprev * jnp.exp(a_end) + _dot(bv, xd * jnp.exp(a_end - a_b)[:, :p], TN)

    return pl.pallas_call(
        body, name="ssd_fwd", grid=(g_, nc // cps),
        in_specs=[pl.BlockSpec((r_, lb, p), lambda g, c: (g, c, 0)),
                  pl.BlockSpec((None, lb, r_), lambda g, c: (g, c, 0)),
                  pl.BlockSpec((None, lb, r_), lambda g, c: (g, c, 0)),
                  pl.BlockSpec((None, r_, lb), lambda g, c: (g, 0, c)),
                  pl.BlockSpec((None, 1, r_), lambda g, c: (g, 0, 0)),
                  pl.BlockSpec((lb, n), lambda g, c: (c, (boff + g))),
                  pl.BlockSpec((lb, n), lambda g, c: (c, (coff + g)))],
        out_specs=[pl.BlockSpec((r_, lb, p), lambda g, c: (g, c, 0)),
                   pl.BlockSpec((r_, cps, n, p), lambda g, c: (g, c, 0, 0))],
        out_shape=[jax.ShapeDtypeStruct((hh, s, p), F32),
                   jax.ShapeDtypeStruct((hh, nc, n, p), F32)],
        scratch_shapes=[pltpu.VMEM((r_, n, p), F32)],
        compiler_params=_cparams(("parallel", "arbitrary")),
    )(xh, dtc, dac, dar, dsk, xbc, xbc)


def _ssd_bwd_g(xh, dtc, dac, dar, dsk, xbc, st, dy):
    hh, s, p = xh.shape
    l, n, g_, r_ = SSD_L, SSD_N, SSD_G, SSD_R
    nc = s // l
    cps = SSD_CPS if nc % SSD_CPS == 0 else 1
    lb, ncb = cps * l, nc // cps
    boff = SSD_INNER // n
    coff = boff + g_

    def body(x_ref, dtc_ref, dac_ref, dar_ref, dsk_ref, b_ref, c_ref, st_ref, dy_ref,
             dx_ref, dda_ref, dxx_ref, db_ref, dc_ref, dah_ref, ddsk_ref, dstate_ref):
        c = pl.program_id(1)

        @pl.when(c == 0)
        def _():
            dstate_ref[...] = jnp.zeros_like(dstate_ref)
            dah_ref[...] = jnp.zeros_like(dah_ref)
            ddsk_ref[...] = jnp.zeros_like(ddsk_ref)

        il = _iota((l, l), 0)
        isx = _iota((l, l), 1)
        tdt = F32 if CDT == F32 else BF16
        t1 = (isx >= il).astype(tdt)
        lane = _iota((l, r_), 1)
        lane1 = _iota((1, r_), 1)
        dskv = dsk_ref[...]
        for cc in reversed(range(cps)):
            rows = pl.ds(cc * l, l)
            cv, bv, dtcv = c_ref[rows, :], b_ref[rows, :], dtc_ref[rows, :]
            cb, acs_c, acs_r = _ssd_chunk_common(cv, bv, dac_ref[rows, :], dar_ref[:, rows])
            dda_all = jnp.zeros((l, r_), F32)
            dxx_all = jnp.zeros((l, r_), F32)
            dah_all = jnp.zeros((1, r_), F32)
            ddsk_all = jnp.zeros((1, r_), F32)
            db_acc = jnp.zeros((l, n), F32)
            dc_acc = jnp.zeros((l, n), F32)
            md_sum = jnp.zeros((l, l), F32)
            cb_t = _dot(bv, cv, NT)
            cv_t = cv.T
            for r in range(r_):
                a_col = _pick_lane(acs_c, r)
                a_row = _pick_row(acs_r, r)
                dt_col = _pick_lane(dtcv, r)
                dsk_h = _pick_lane(dskv, r)
                xv = x_ref[r, rows, :]
                dyv = dy_ref[r, rows, :]
                a_b = jnp.broadcast_to(a_col, (l, l))
                dt_b = jnp.broadcast_to(dt_col, (l, p))
                xd = xv * dt_b
                decay = jnp.where(isx <= il, jnp.exp(jnp.minimum(a_b - a_row, 0.0)), 0.0)
                decay_t = jnp.where(isx >= il, jnp.exp(jnp.minimum(a_row - a_b, 0.0)), 0.0)
                dhn = dstate_ref[r]
                hc = st_ref[r, cc]
                a_end = a_col[l - 1:l, :]
                ea_b = jnp.exp(a_b)
                dte_b = jnp.exp(a_end - a_b)

                dx_state = dte_b[:, :p] * _dot(bv, dhn)
                dxd = _dot(cb_t * decay_t, dyv) + dx_state
                md = decay * _dot(dyv, xd, NT)
                md_sum = md_sum + md
                dc_acc = dc_acc + ea_b * _dot(dyv, hc, NT)
                db_acc = db_acc + dte_b * _dot(xd, dhn, NT)
                dstate_ref[r] = jnp.exp(a_end) * dhn + _dot(cv_t, dyv * ea_b[:, :p])

                yoff = ea_b[:, :p] * _dot(cv, hc)
                xdx = xd * dx_state
                vec = jnp.sum(dyv * yoff - xdx, axis=1, keepdims=True)
                total = lambda t: jnp.sum(jnp.sum(t, axis=0, keepdims=True), axis=1, keepdims=True)
                end_term = total(xdx) + jnp.exp(a_end) * total(hc * dhn)
                zmat = _split_dot(md * cb, t1, True, 2)
                span = jnp.sum(jnp.where(isx < il, zmat, 0.0), axis=1, keepdims=True)
                rc = _split_dot(jnp.broadcast_to(vec, (l, 128)), t1, True, 2)[:, :1]
                dda = span + rc + end_term
                dda_all = jnp.where(lane == r, dda, dda_all)
                dxx_all = jnp.where(lane == r, jnp.sum(dxd * xv, axis=1, keepdims=True), dxx_all)
                dx_ref[r, rows, :] = dxd * dt_b + dsk_h * dyv
                dah_all = jnp.where(lane1 == r, jnp.sum(dda * dt_col, axis=0, keepdims=True), dah_all)
                ddsk_all = jnp.where(lane1 == r, total(dyv * xv), ddsk_all)
            dda_ref[rows, :] = dda_all
            dxx_ref[rows, :] = dxx_all
            db_ref[rows, :] = db_acc + _dot(md_sum, cv, TN)
            dc_ref[rows, :] = dc_acc + _dot(md_sum, bv)
            dah_ref[...] += dah_all
            ddsk_ref[...] += ddsk_all

    rev = lambda c: ncb - 1 - c
    xspec = pl.BlockSpec((r_, lb, p), lambda g, c: (g, rev(c), 0))
    cspec = pl.BlockSpec((None, lb, r_), lambda g, c: (g, rev(c), 0))
    hspec = pl.BlockSpec((None, 1, r_), lambda g, c: (g, 0, 0))
    return pl.pallas_call(
        body, name="ssd_bwd", grid=(g_, ncb),
        in_specs=[xspec, cspec, cspec,
                  pl.BlockSpec((None, r_, lb), lambda g, c: (g, 0, rev(c))),
                  hspec,
                  pl.BlockSpec((lb, n), lambda g, c: (rev(c), boff + g)),
                  pl.BlockSpec((lb, n), lambda g, c: (rev(c), coff + g)),
                  pl.BlockSpec((r_, cps, n, p), lambda g, c: (g, rev(c), 0, 0)),
                  xspec],
        out_specs=[xspec, cspec, cspec,
                   pl.BlockSpec((lb, n), lambda g, c: (rev(c), g)),
                   pl.BlockSpec((lb, n), lambda g, c: (rev(c), g)),
                   hspec, hspec],
        out_shape=[jax.ShapeDtypeStruct((hh, s, p), F32),
                   jax.ShapeDtypeStruct((g_, s, r_), F32),
                   jax.ShapeDtypeStruct((g_, s, r_), F32),
                   jax.ShapeDtypeStruct((s, g_ * n), F32),
                   jax.ShapeDtypeStruct((s, g_ * n), F32),
                   jax.ShapeDtypeStruct((g_, 1, r_), F32),
                   jax.ShapeDtypeStruct((g_, 1, r_), F32)],
        scratch_shapes=[pltpu.VMEM((r_, n, p), F32)],
        compiler_params=_cparams(("parallel", "arbitrary")),
    )(xh, dtc, dac, dar, dsk, xbc, xbc, st, dy)


CONV_TC = 256
CONV_RC = 512


def _conv_taps(x_ref, head_ref, rc):
    base = CONV_PAD - (CONV_K - 1)
    head_ref[pl.ds(0, CONV_PAD), :] = jnp.zeros((CONV_PAD, head_ref.shape[1]), F32)
    head_ref[pl.ds(CONV_PAD, rc), :] = x_ref[pl.ds(0, rc), :]

    def tap(t0, kk):
        if t0 == 0:
            return head_ref[pl.ds(base + kk, rc), :]
        return x_ref[pl.ds(t0 - (CONV_K - 1) + kk, rc), :]
    return tap


def _conv_fwd(x, w, b):
    s, ch = x.shape
    rc = min(CONV_RC, s)

    def body(x_ref, w_ref, b_ref, pre_ref, act_ref, head_ref):
        wv = w_ref[...]
        tap = _conv_taps(x_ref, head_ref, rc)
        for t0 in range(0, s, rc):
            acc = jnp.broadcast_to(b_ref[...], (rc, CONV_TC))
            for kk in range(CONV_K):
                acc = acc + wv[kk:kk + 1, :] * tap(t0, kk)
            pre_ref[pl.ds(t0, rc), :] = acc
            act_ref[pl.ds(t0, rc), :] = _silu(acc)

    col = pl.BlockSpec((s, CONV_TC), lambda j: (0, j))
    shp = jax.ShapeDtypeStruct((s, ch), F32)
    return pl.pallas_call(
        body, name="conv_fwd", grid=(ch // CONV_TC,),
        in_specs=[col, pl.BlockSpec((CONV_K, CONV_TC), lambda j: (0, j)),
                  pl.BlockSpec((1, CONV_TC), lambda j: (0, j))],
        out_specs=[col, col],
        out_shape=[shp, shp],
        scratch_shapes=[pltpu.VMEM((CONV_PAD + rc, CONV_TC), F32)],
        compiler_params=_cparams(("parallel",)),
    )(x, w, b)


def _conv_bwd(x, pre, dact, w):
    s, ch = x.shape
    rc = min(CONV_RC, s)

    def body(x_ref, pre_ref, da_ref, w_ref, dx_ref, dw_ref, db_ref, dpre_ref, head_ref):
        wv = w_ref[...]
        tap = _conv_taps(x_ref, head_ref, rc)
        for t0 in range(0, s, rc):
            dpre_ref[pl.ds(t0, rc), :] = da_ref[pl.ds(t0, rc), :] * _dsilu(pre_ref[pl.ds(t0, rc), :])
        dpre_ref[pl.ds(s, CONV_PAD), :] = jnp.zeros((CONV_PAD, CONV_TC), F32)
        dws = [jnp.zeros((1, CONV_TC), F32) for _ in range(CONV_K)]
        dbs = jnp.zeros((1, CONV_TC), F32)
        for t0 in range(0, s, rc):
            acc = jnp.zeros((rc, CONV_TC), F32)
            dp = dpre_ref[pl.ds(t0, rc), :]
            for kk in range(CONV_K):
                acc = acc + wv[kk:kk + 1, :] * dpre_ref[pl.ds(t0 + CONV_K - 1 - kk, rc), :]
                dws[kk] = dws[kk] + jnp.sum(dp * tap(t0, kk), axis=0, keepdims=True)
            dbs = dbs + jnp.sum(dp, axis=0, keepdims=True)
            dx_ref[pl.ds(t0, rc), :] = acc.astype(dx_ref.dtype)
        for kk in range(CONV_K):
            dw_ref[kk:kk + 1, :] = dws[kk]
        db_ref[...] = dbs

    col = pl.BlockSpec((s, CONV_TC), lambda j: (0, j))
    return pl.pallas_call(
        body, name="conv_bwd", grid=(ch // CONV_TC,),
        in_specs=[col, col, col, pl.BlockSpec((CONV_K, CONV_TC), lambda j: (0, j))],
        out_specs=[col, pl.BlockSpec((CONV_K, CONV_TC), lambda j: (0, j)),
                   pl.BlockSpec((1, CONV_TC), lambda j: (0, j))],
        out_shape=[jax.ShapeDtypeStruct((s, ch), CDT),
                   jax.ShapeDtypeStruct((CONV_K, ch), F32),
                   jax.ShapeDtypeStruct((1, ch), F32)],
        scratch_shapes=[pltpu.VMEM((s + CONV_PAD, CONV_TC), F32), pltpu.VMEM((CONV_PAD + rc, CONV_TC), F32)],
        compiler_params=_cparams(("parallel",)),
    )(x, pre, dact, w)


MEM_TS = 512


def _mem_fwd(mq, kv):
    s = mq.shape[0]
    m = kv.shape[0]
    ts = min(MEM_TS, s)
    scale = MEM_DH ** -0.5

    def body(q_ref, k_ref, v_ref, o_ref):
        sc = _dot(q_ref[...], k_ref[...], NT) * scale
        e = jnp.exp(sc - jnp.max(sc, axis=1, keepdims=True))
        pr = e / jnp.sum(e, axis=1, keepdims=True)
        o_ref[...] = _dot(pr, v_ref[...]).astype(o_ref.dtype)

    return pl.pallas_call(
        body, name="mem_fwd", grid=(MEM_H, s // ts),
        in_specs=[pl.BlockSpec((ts, MEM_DH), lambda a, i: (i, a)),
                  pl.BlockSpec((m, MEM_DH), lambda a, i: (0, a)),
                  pl.BlockSpec((m, MEM_DH), lambda a, i: (0, MEM_H + a))],
        out_specs=pl.BlockSpec((ts, MEM_DH), lambda a, i: (i, a)),
        out_shape=jax.ShapeDtypeStruct((s, MEM_H * MEM_DH), CDT),
        compiler_params=_cparams(("parallel", "arbitrary")),
    )(mq, kv, kv)


def _mem_bwd(mq, kv, do):
    s = mq.shape[0]
    m = kv.shape[0]
    ts = min(MEM_TS, s)
    scale = MEM_DH ** -0.5

    def body(q_ref, k_ref, v_ref, do_ref, dq_ref, dk_ref, dv_ref):
        i = pl.program_id(1)

        @pl.when(i == 0)
        def _():
            dk_ref[...] = jnp.zeros_like(dk_ref)
            dv_ref[...] = jnp.zeros_like(dv_ref)

        qv, kb, vb, dov = q_ref[...], k_ref[...], v_ref[...], do_ref[...]
        sc = _dot(qv, kb, NT) * scale
        e = jnp.exp(sc - jnp.max(sc, axis=1, keepdims=True))
        pr = e / jnp.sum(e, axis=1, keepdims=True)
        dp = _dot(dov, vb, NT)
        ds = pr * (dp - jnp.sum(dp * pr, axis=1, keepdims=True)) * scale
        dq_ref[...] = _dot(ds, kb).astype(dq_ref.dtype)
        dk_ref[...] += _dot(ds, qv, TN)
        dv_ref[...] += _dot(pr, dov, TN)

    tile = pl.BlockSpec((ts, MEM_DH), lambda a, i: (i, a))
    kvo = pl.BlockSpec((m, MEM_DH), lambda a, i: (0, a))
    return pl.pallas_call(
        body, name="mem_bwd", grid=(MEM_H, s // ts),
        in_specs=[tile, kvo, pl.BlockSpec((m, MEM_DH), lambda a, i: (0, MEM_H + a)), tile],
        out_specs=[tile, kvo, kvo],
        out_shape=[jax.ShapeDtypeStruct((s, MEM_H * MEM_DH), CDT),
                   jax.ShapeDtypeStruct((m, MEM_H * MEM_DH), F32),
                   jax.ShapeDtypeStruct((m, MEM_H * MEM_DH), F32)],
        compiler_params=_cparams(("parallel", "arbitrary")),
    )(mq, kv, kv, do)


def _heads(t, nh, dh):
    return t.reshape(t.shape[0], nh, dh).transpose(1, 0, 2)


def _unheads(t):
    return t.transpose(1, 0, 2).reshape(t.shape[1], t.shape[0] * t.shape[2])


def _group_cols(t):
    return t.reshape(t.shape[0], SSD_G, SSD_R).transpose(1, 0, 2)


def _pad_cols(t, width):
    return jnp.pad(t, ((0, 0), (0, width - t.shape[1])))


def _full_weight(name, gathered):
    if name in COL_SHARDED:
        return gathered.transpose(1, 0, 2).reshape(gathered.shape[1], N_DEV * gathered.shape[2])
    return gathered.reshape(N_DEV * gathered.shape[1], gathered.shape[2])


def _grad_payload(name, g):
    if name in COL_SHARDED:
        return g.reshape(g.shape[0], N_DEV, g.shape[1] // N_DEV).transpose(1, 0, 2)
    return g.reshape(N_DEV, g.shape[0] // N_DEV, g.shape[1])


def _local_step(x, mem, tgt, p, wt, shards=None):
    s, d = x.shape
    wt = dict(wt)
    c1, c2, c3, c4, c5 = 3 * d, 3 * d + SSD_INNER, 3 * d + SSD_INNER + CONV_DIM, \
        3 * d + SSD_INNER + CONV_DIM + SSD_H, 3 * d + SSD_INNER + CONV_DIM + SSD_H + d
    w_t = wt["w_in"]
    w_tail = w_t[c4:]
    w_dt = jnp.pad(w_t[c3:c4], ((0, DT_PAD - SSD_H), (0, 0)))
    seg_name = ["qkv", "z", "xbc", "mq", "gl"]
    seg_dtype = [CDT, F32, F32, CDT, F32]
    seg_src = [w_t, w_t, w_t, w_tail, w_tail]
    seg_off = [0, c1, c2, 0, d]
    seg_n = [c1, c2 - c1, c3 - c2, d, 3 * d]

    u = _rows(lambda xv, g: _rms(xv, g), [x], [p["norm_mix_pre"]], [(d, CDT)], ts=512, name="f_norm_pre")[0]
    qkv, z, xbc_raw, mq, gl = [
        _mm(u, seg_src[i], tb=True, b_off=seg_off[i], n=seg_n[i], out_dtype=seg_dtype[i],
            name="f_in_" + seg_name[i]) for i in range(5)]
    dt_raw = _mm(u, w_dt, tb=True, name="f_in_dt")

    bias128 = _pad_cols(p["dt_bias"], DT_PAD)
    alog128 = _pad_cols(p["a_log"], DT_PAD)

    def dt_fn(dtr, bias, alog):
        dt = _softplus(dtr + bias)
        return dt, dt * (-jnp.exp(alog))

    dt128, da128 = _rows(dt_fn, [dt_raw], [bias128, alog128], [(DT_PAD, F32), (DT_PAD, F32)],
                         ts=512, name="f_dt")
    dtc = _group_cols(dt128[:, :SSD_H])
    dac = _group_cols(da128[:, :SSD_H])
    dar = dac.transpose(0, 2, 1)
    dsk = p["d_skip"].reshape(SSD_G, 1, SSD_R)

    conv_w, conv_b = p["conv_w"], p["conv_b"]
    pre, xbc = _conv_fwd(xbc_raw, conv_w, conv_b)
    xh = _heads(xbc[:, :SSD_INNER], SSD_H, SSD_P)
    y_h, st = _ssd_fwd_g(xh, dtc, dac, dar, dsk, xbc)
    y_core = _unheads(y_h)

    def group_norm_fwd(yv, zv, wn):
        y2 = yv * _silu(zv)
        gw = SSD_INNER // SSD_G
        outs = []
        for gi in range(SSD_G):
            seg = y2[:, gi * gw:(gi + 1) * gw]
            outs.append(_rms(seg, wn[:, gi * gw:(gi + 1) * gw]))
        return jnp.concatenate(outs, axis=1)

    y_ssd = _rows(group_norm_fwd, [y_core, z], [p["ssd_norm"]], [(SSD_INNER, CDT)], ts=256, name="f_ssd_post")[0]

    y_sb, lt_h, *late = _sb_fwd_pairs(qkv, tuple(shards) if shards is not None else ())
    for n, gth in zip(LATE_W, late):
        wt[n] = _full_weight(n, gth)

    mu = _rows(lambda mv, g: _rms(mv, g), [mem], [p["norm_mem"]], [(d, CDT)], ts=256, name="f_norm_mem")[0]
    kv = _mm(mu, wt["w_mem_kv"], out_dtype=CDT, name="f_mem_kv")
    y_mem = _mem_fwd(mq, kv)

    p_sb = _mm(y_sb, wt["w_sb_out"], name="f_sb_out")
    p_ssd = _mm(y_ssd, wt["w_ssd_out"], name="f_ssd_out")
    p_mem = _mm(y_mem, wt["w_mem_out"], name="f_mem_out")

    def merge_fn(glv, a, b, c):
        return (_sigmoid(glv[:, :d]) * a + _sigmoid(glv[:, d:2 * d]) * b + _sigmoid(glv[:, 2 * d:]) * c)

    merged = _rows(merge_fn, [gl, p_sb, p_ssd, p_mem], [], [(d, CDT)], ts=256, name="f_merge")[0]
    mix = _mm(merged, wt["w_o"], name="f_w_o")

    def mid_fn(xv, mixv, g_post, g_pre):
        h1 = xv + _rms(mixv, g_post)
        return h1, _rms(h1, g_pre)

    h1, u2 = _rows(mid_fn, [x, mix], [p["norm_mix_post"], p["norm_mlp_pre"]], [(d, F32), (d, CDT)],
                   ts=512, name="f_mid")
    a1, act = _mm(u2, wt["w_up"], name="f_up",
                  epilogue=(lambda pv: (pv, jnp.square(jnp.maximum(pv, 0.0))), [], [F32, CDT]))
    ff = _mm(act, wt["w_down"], name="f_down")

    def loss_fn(h1v, ffv, tv, g):
        diff = h1v + _rms(ffv, g) - tv
        tot = jnp.sum(jnp.sum(diff * diff, axis=1, keepdims=True), axis=0, keepdims=True)
        return diff * (1.0 / d), tot

    dh2, loss_acc = _rows(loss_fn, [h1, ff, tgt], [p["norm_mlp_post"]], [(d, F32)], [(1, 128)],
                          ts=512, name="f_loss")
    loss = loss_acc[:, :1] * (0.5 / d)

    sg = {}

    def b_post(ffv, dyv, g):
        dx, dg = _rms_bwd(ffv, g, dyv)
        return dx, dg

    d_ff, sg["norm_mlp_post"] = _rows(b_post, [ff, dh2], [p["norm_mlp_post"]], [(d, CDT)], [(1, d)],
                                      ts=512, name="b_norm_mlp_post")
    da1 = _mm(d_ff, wt["w_down"], tb=True, name="b_down_x",
              epilogue=(lambda pv, a: (pv * 2.0 * jnp.maximum(a, 0.0),), [a1], [CDT]))[0]
    gw = {"w_down": _mm(act, d_ff, ta=True, name="b_down_w")}
    du2 = _mm(da1, wt["w_up"], tb=True, name="b_up_x")
    gw["w_up"] = _mm(u2, da1, ta=True, name="b_up_w")

    def b_mid(h1v, du2v, dh2v, mixv, g_pre, g_post):
        dxa, dga = _rms_bwd(h1v, g_pre, du2v)
        dh1 = dh2v + dxa
        dmix, dgb = _rms_bwd(mixv, g_post, dh1)
        return dh1, dmix, dga, dgb

    dh1, dmix, sg["norm_mlp_pre"], sg["norm_mix_post"] = _rows(
        b_mid, [h1, du2, dh2, mix], [p["norm_mlp_pre"], p["norm_mix_post"]],
        [(d, F32), (d, CDT)], [(1, d), (1, d)], ts=256, name="b_mid")
    dmerged = _mm(dmix, wt["w_o"], tb=True, name="b_w_o_x")
    gw["w_o"] = _mm(merged, dmix, ta=True, name="b_w_o_w")

    def b_merge(dm, glv, a, b, c):
        outs, dgl = [], []
        for i, br in enumerate((a, b, c)):
            gt = _sigmoid(glv[:, i * d:(i + 1) * d])
            outs.append(gt * dm)
            dgl.append(dm * br * gt * (1.0 - gt))
        return outs[0], outs[1], outs[2], jnp.concatenate(dgl, axis=1)

    dp_sb, dp_ssd, dp_mem, dgl = _rows(b_merge, [dmerged, gl, p_sb, p_ssd, p_mem], [],
                                       [(d, CDT), (d, CDT), (d, CDT), (3 * d, CDT)], ts=256, name="b_merge")
    dy_sb = _mm(dp_sb, wt["w_sb_out"], tb=True, name="b_sb_out_x")
    gw["w_sb_out"] = _mm(y_sb, dp_sb, ta=True, name="b_sb_out_w")
    dy_ssd = _mm(dp_ssd, wt["w_ssd_out"], tb=True, name="b_ssd_out_x")
    gw["w_ssd_out"] = _mm(y_ssd, dp_ssd, ta=True, name="b_ssd_out_w")
    dy_mem = _mm(dp_mem, wt["w_mem_out"], tb=True, out_dtype=CDT, name="b_mem_out_x")
    gw["w_mem_out"] = _mm(y_mem, dp_mem, ta=True, name="b_mem_out_w")

    dmq, dk_m, dv_m = _mem_bwd(mq, kv, dy_mem)
    dkv = jnp.concatenate([dk_m, dv_m], axis=1).astype(CDT)
    gw["w_mem_kv"] = _mm(mu, dkv, ta=True, name="b_mem_kv_w")
    dmu = _mm(dkv, wt["w_mem_kv"], tb=True, name="b_mem_kv_x")
    sg["norm_mem"] = _rows(lambda mv, dv, g: _rms_bwd(mv, g, dv)[1], [mem, dmu], [p["norm_mem"]], [], [(1, d)],
                           ts=256, name="b_norm_mem")[0]

    payloads = tuple(_grad_payload(n, gw[n]) for n in LATE_W) if shards is not None else ()
    dq, dk, dv, *received = _sb_bwd_pairs(qkv, lt_h, dy_sb, payloads)
    dqkv = jnp.concatenate([dq, dk, dv], axis=1).astype(CDT)

    def group_norm_bwd(dyo, yv, zv, wn):
        sz = _silu(zv)
        y2 = yv * sz
        gw_ = SSD_INNER // SSD_G
        dy2, dwn = [], []
        for gi in range(SSD_G):
            sl = slice(gi * gw_, (gi + 1) * gw_)
            dxs, dgs = _rms_bwd(y2[:, sl], wn[:, sl], dyo[:, sl])
            dy2.append(dxs)
            dwn.append(dgs)
        dy2 = jnp.concatenate(dy2, axis=1)
        return dy2 * sz, dy2 * yv * _dsilu(zv), jnp.concatenate(dwn, axis=1)

    dy_core, dz, sg["ssd_norm"] = _rows(group_norm_bwd, [dy_ssd, y_core, z], [p["ssd_norm"]],
                                        [(SSD_INNER, F32), (SSD_INNER, CDT)], [(1, SSD_INNER)],
                                        ts=256, name="b_ssd_post")
    dxh, dda, dxx, d_b, d_c, dah, ddsk = _ssd_bwd_g(xh, dtc, dac, dar, dsk, xbc, st, _heads(dy_core, SSD_H, SSD_P))
    sg["d_skip"] = ddsk.reshape(1, SSD_H)
    sg["a_log"] = dah.reshape(1, SSD_H) * (-jnp.exp(p["a_log"]))

    def b_dt(ddav, dxxv, dtr, bias, alog):
        ddt = ddav * (-jnp.exp(alog)) + dxxv
        draw = ddt * _sigmoid(dtr + bias)
        return draw, jnp.sum(draw, axis=0, keepdims=True)

    ungroup = lambda t: _pad_cols(t.transpose(1, 0, 2).reshape(s, SSD_H), DT_PAD)
    ddt_raw, dbias128 = _rows(b_dt, [ungroup(dda), ungroup(dxx), dt_raw], [bias128, alog128],
                              [(DT_PAD, CDT)], [(1, DT_PAD)], ts=512, name="b_dt")
    sg["dt_bias"] = dbias128[:, :SSD_H]

    dxbc = jnp.concatenate([_unheads(dxh), d_b, d_c], axis=1)
    dxbc_raw, sg["conv_w"], sg["conv_b"] = _conv_bwd(xbc_raw, pre, dxbc, conv_w)

    dseg = [dqkv, dz, dxbc_raw, dmq, dgl]
    dw_bufs = [lax.empty((c3, d), F32), lax.empty((c5 - c4 + 3 * d, d), F32)]
    for i in range(5):
        bi = 0 if i < 3 else 1
        dw_bufs[bi] = _mm(dseg[i], u, ta=True, into=dw_bufs[bi], into_off=seg_off[i], into_rows=True,
                          name="b_in_w_" + seg_name[i])
    dw_dt = _mm(ddt_raw, u, ta=True, name="b_in_w_dt")
    def du(i, hosted=None):
        return _mm(dseg[i], seg_src[i], b_koff=seg_off[i], name="b_in_x_" + seg_name[i], hosted=hosted)

    if shards is None:
        dus = [du(i) for i in range(5)]
        gw["w_in"] = jnp.concatenate([dw_bufs[0], dw_dt[:SSD_H], dw_bufs[1]], axis=0).T
    else:
        du0, recv_head = du(0, _pair_swap([dw_bufs[0]]))
        du2, recv_tail, recv_dt = du(2, _pair_swap([dw_bufs[1], dw_dt]))
        tail_rows = dw_bufs[1].shape[0]
        natural = lax.empty((c4 + tail_rows, d), CDT)
        natural = _add_cast_into(dw_bufs[0], recv_head, natural, 0, c3, "sum_w_in_head")
        natural = _add_cast_into(dw_dt, recv_dt, natural, c3, SSD_H, "sum_w_in_dt")
        natural = _add_cast_into(dw_bufs[1], recv_tail, natural, c4, tail_rows, "sum_w_in_tail")
        shard = natural.shape[0] // N_DEV
        du1, win_a = du(1, _chip_scatter_windows(natural, shard, 0, d // 2))
        du4, win_b = du(4, _chip_scatter_windows(natural, shard, d // 2, d // 2))
        gw["w_in"] = [win_a, win_b]
        dus = [du0, du1, du2, du(3), du4]
    dus.append(_mm(ddt_raw, w_dt, name="b_in_x_dt"))

    def b_pre(xv, dh1v, d0, d1, d2, d3, d4, d5, g):
        dx, dg = _rms_bwd(xv, g, d0 + d1 + d2 + d3 + d4 + d5)
        return dh1v + dx, dg

    grad_x, sg["norm_mix_pre"] = _rows(b_pre, [x, dh1] + dus, [p["norm_mix_pre"]], [(d, F32)], [(1, d)],
                                       ts=256, name="b_norm_pre")
    return loss, grad_x, gw, sg, (received if shards is not None else None)


HBM = pl.BlockSpec(memory_space=pltpu.HBM)
MESH = pl.DeviceIdType.MESH


def _me_and_peers():
    x, y, c = lax.axis_index("x"), lax.axis_index("y"), lax.axis_index("c")
    me = 4 * x + 2 * y + c
    peers = [(x, y, 1 - c), (1 - x, y, c), (x, 1 - y, c), (1 - x, 1 - y, c),
             (1 - x, y, 1 - c), (x, 1 - y, 1 - c), (1 - x, 1 - y, 1 - c)]
    return me, peers


def _peer_index(peer):
    return 4 * peer[0] + 2 * peer[1] + peer[2]


def _exchange_copies(ins, outs, send_sems, recv_sems, local_sems, scatter):
    me, peers = _me_and_peers()
    copies = []
    for a in range(len(ins)):
        own = ins[a].at[me] if scatter else ins[a]
        copies.append(pltpu.make_async_copy(own, outs[a].at[me], local_sems.at[a]))
        for kk, peer in enumerate(peers):
            src = ins[a].at[_peer_index(peer)] if scatter else ins[a]
            copies.append(pltpu.make_async_remote_copy(
                src_ref=src, dst_ref=outs[a].at[me],
                send_sem=send_sems.at[a, kk], recv_sem=recv_sems.at[a, kk],
                device_id=peer, device_id_type=MESH))
    return copies


def _exchange_shapes(ins, scatter):
    return [jax.ShapeDtypeStruct(t.shape if scatter else (N_DEV,) + t.shape, t.dtype) for t in ins]


def _exchange_sems(n):
    return [pltpu.SemaphoreType.DMA((n, N_DEV - 1)), pltpu.SemaphoreType.DMA((n, N_DEV - 1)),
            pltpu.SemaphoreType.DMA((n,))]


def _gather_two_level(shards, name):
    n = len(shards)

    def body(*refs):
        ins, outs = refs[:n], refs[n:2 * n]
        send_sems, recv_sems, local_sems = refs[2 * n:]
        x, y, c = lax.axis_index("x"), lax.axis_index("y"), lax.axis_index("c")
        me, sib = (x, y, c), (x, y, 1 - c)
        chips = [(1 - x, y), (x, 1 - y), (1 - x, 1 - y)]

        def copy(a, k, block, to, src=None):
            slot = outs[a].at[_peer_index(block)]
            return pltpu.make_async_remote_copy(
                src_ref=slot if src is None else src, dst_ref=slot,
                send_sem=send_sems.at[a, k], recv_sem=recv_sems.at[a, k], device_id=to, device_id_type=MESH)

        own = [pltpu.make_async_copy(ins[a], outs[a].at[_peer_index(me)], local_sems.at[a]) for a in range(n)]
        first = []
        for a in range(n):
            first.append(copy(a, 0, me, sib, src=ins[a]))
            first += [copy(a, 1 + j, me, (*chip, c), src=ins[a]) for j, chip in enumerate(chips)]
        for cp in own + first:
            cp.start()
        passed = []
        for j, chip in enumerate(chips):
            for a in range(n):
                copy(a, 1 + j, (*chip, c), me).wait_recv()
                fwd = copy(a, 4 + j, (*chip, c), sib)
                fwd.start()
                passed.append(fwd)
        for a in range(n):
            copy(a, 0, sib, me).wait_recv()
            for j, chip in enumerate(chips):
                copy(a, 4 + j, (*chip, 1 - c), me).wait_recv()
        for cp in first + passed:
            cp.wait_send()
        for cp in own:
            cp.wait()

    return pl.pallas_call(
        body, name=name,
        in_specs=[HBM] * n, out_specs=[HBM] * n,
        out_shape=_exchange_shapes(shards, False),
        scratch_shapes=_exchange_sems(n),
        compiler_params=pltpu.CompilerParams(has_side_effects=True),
    )(*shards)


class _Hosted:
    def __init__(self, ins, shapes, sems, copies):
        self.ins, self.shapes, self.sems, self.copies = ins, shapes, sems, copies


def _pair_swap(arrays):
    n = len(arrays)

    def copies(in_refs, out_refs, sems):
        sib = (lax.axis_index("x"), lax.axis_index("y"), 1 - lax.axis_index("c"))
        return [pltpu.make_async_remote_copy(
            src_ref=in_refs[i], dst_ref=out_refs[i], send_sem=sems[0].at[i], recv_sem=sems[1].at[i],
            device_id=sib, device_id_type=MESH) for i in range(n)]

    return _Hosted(list(arrays), [jax.ShapeDtypeStruct(t.shape, t.dtype) for t in arrays],
                   [pltpu.SemaphoreType.DMA((n,)), pltpu.SemaphoreType.DMA((n,))], copies)


def _add_cast_into(a, b, into, row_off, rows, name):
    w = a.shape[1]
    tr = _pick(rows, (512, 256, 128, rows))
    nt = rows // tr
    assert rows % tr == 0 and row_off % ROW_ALIGN == 0 and tr % ROW_ALIGN == 0
    spec = pl.BlockSpec((tr, w), lambda i: (i, 0))

    def body(a_ref, b_ref, into_ref, o_ref, slots, sems):
        i = pl.program_id(0)
        slot = i % 2

        def out_copy(step, s):
            dst = o_ref.at[pl.ds(pl.multiple_of(row_off + step * tr, ROW_ALIGN), tr)]
            return pltpu.make_async_copy(slots.at[s], dst, sems.at[s])

        @pl.when(i >= 2)
        def _():
            out_copy(i - 2, slot).wait()

        slots[slot] = (a_ref[...] + b_ref[...]).astype(slots.dtype)
        out_copy(i, slot).start()

        @pl.when(i == nt - 1)
        def _():
            out_copy(i, slot).wait()
            if nt > 1:
                out_copy(i - 1, 1 - slot).wait()

    return pl.pallas_call(
        body, name=name, grid=(nt,),
        in_specs=[spec, spec, pl.BlockSpec(memory_space=pl.ANY)],
        out_specs=pl.BlockSpec(memory_space=pl.ANY),
        out_shape=jax.ShapeDtypeStruct(into.shape, into.dtype),
        scratch_shapes=[pltpu.VMEM((2, tr, w), into.dtype), pltpu.SemaphoreType.DMA((2,))],
        input_output_aliases={2: 0},
        compiler_params=_cparams(("arbitrary",)),
    )(a, b, into)


N_CHIP = 4


def _adamw_math(g, w, m, v):
    m2 = ADAM_B1 * m + (1.0 - ADAM_B1) * g
    v2 = ADAM_B2 * v + (1.0 - ADAM_B2) * jnp.square(g)
    m_hat = m2 / (1.0 - ADAM_B1 ** ADAM_STEP)
    v_hat = v2 / (1.0 - ADAM_B2 ** ADAM_STEP)
    delta = -ADAM_LR * (m_hat / (jnp.sqrt(v_hat) + ADAM_EPS) + ADAM_WD * w)
    return delta, m2, v2


def _adamw_reduce(parts, w, m, v, name):
    r, c = w.shape
    nparts = parts.shape[0]
    tr = _pick(r, (128, 64, 32, 16, 8))

    def body(p_ref, w_ref, m_ref, v_ref, g_ref, d_ref, m2_ref, v2_ref):
        g = p_ref[0].astype(F32)
        for i in range(1, nparts):
            g = g + p_ref[i].astype(F32)
        delta, m2, v2 = _adamw_math(g, w_ref[...], m_ref[...], v_ref[...])
        g_ref[...] = g
        d_ref[...] = delta
        m2_ref[...] = m2
        v2_ref[...] = v2

    tile = pl.BlockSpec((tr, c), lambda i: (i, 0))
    shp = jax.ShapeDtypeStruct((r, c), F32)
    return pl.pallas_call(
        body, name=name, grid=(r // tr,),
        in_specs=[pl.BlockSpec((nparts, tr, c), lambda i: (0, i, 0)), tile, tile, tile],
        out_specs=[tile] * 4, out_shape=[shp] * 4,
        compiler_params=_cparams(("parallel",)),
    )(parts, w, m, v)


ROW_ALIGN = 16


def _window(shard):
    lead = max((j * shard) % ROW_ALIGN for j in range(N_DEV))
    return -(-(lead + shard) // ROW_ALIGN) * ROW_ALIGN


def _chip_scatter_windows(t, shard, col0, cols):
    win = _window(shard)
    assert all((j * shard // ROW_ALIGN) * ROW_ALIGN + win <= t.shape[0] for j in range(N_DEV))

    def copies(in_refs, out_refs, sems):
        (t_ref,), (o_ref,), (send_sems, recv_sems, local_sem) = in_refs, out_refs, sems
        x, y, c = lax.axis_index("x"), lax.axis_index("y"), lax.axis_index("c")
        mine = 2 * x + y

        def window(q):
            a0 = pl.multiple_of(((2 * q + c) * shard // ROW_ALIGN) * ROW_ALIGN, ROW_ALIGN)
            return t_ref.at[pl.ds(a0, win), pl.ds(col0, cols)]

        res = [pltpu.make_async_copy(window(mine), o_ref.at[mine], local_sem)]
        for j, (px, py) in enumerate([(1 - x, y), (x, 1 - y), (1 - x, 1 - y)]):
            res.append(pltpu.make_async_remote_copy(
                src_ref=window(2 * px + py), dst_ref=o_ref.at[mine],
                send_sem=send_sems.at[j], recv_sem=recv_sems.at[j], device_id=(px, py, c), device_id_type=MESH))
        return res

    return _Hosted([t], [jax.ShapeDtypeStruct((N_CHIP, win, cols), t.dtype)],
                   [pltpu.SemaphoreType.DMA((N_CHIP - 1,)), pltpu.SemaphoreType.DMA((N_CHIP - 1,)),
                    pltpu.SemaphoreType.DMA], copies)


ADAMW_TC = 256


def _adamw_windows(parts, w, m, v, name):
    r, c = w.shape
    na = len(parts)
    nparts, win, cpart = parts[0].shape
    tc = min(ADAMW_TC, cpart)
    per = cpart // tc

    def body(*refs):
        p_refs = refs[:na]
        w_ref, m_ref, v_ref, g_ref, d_ref, m2_ref, v2_ref = refs[na:]
        me, _ = _me_and_peers()
        step = pl.program_id(0)
        gw_ = None
        for a, p_ref in enumerate(p_refs):
            tot = p_ref[0].astype(F32)
            for i in range(1, nparts):
                tot = tot + p_ref[i].astype(F32)
            gw_ = tot if gw_ is None else jnp.where(step // per == a, tot, gw_)
        for j in range(N_DEV):
            @pl.when(me == j)
            def _():
                lead = (j * r) % ROW_ALIGN
                g = (pltpu.roll(gw_, win - lead, 0) if lead else gw_)[:r]
                delta, m2, v2 = _adamw_math(g, w_ref[...], m_ref[...], v_ref[...])
                g_ref[...] = g
                d_ref[...] = delta
                m2_ref[...] = m2
                v2_ref[...] = v2

    tile = pl.BlockSpec((r, tc), lambda i: (0, i))
    shp = jax.ShapeDtypeStruct((r, c), F32)
    return pl.pallas_call(
        body, name=name, grid=(c // tc,),
        in_specs=[pl.BlockSpec((nparts, win, tc), lambda i, a=a: (0, 0, jnp.clip(i - a * per, 0, per - 1)))
                  for a in range(na)] + [tile, tile, tile],
        out_specs=[tile] * 4, out_shape=[shp] * 4,
        compiler_params=_cparams(("parallel",)),
    )(*parts, w, m, v)


SMALL_ROWS, SMALL_COLS = 16, 3072


def _small_step(sg, gcw, loss, ws, ms, vs):
    ns = len(sg)
    widths = [t.shape[1] for t in sg]
    kk_, ch = gcw.shape[1], gcw.shape[2]
    assert ns < SMALL_ROWS and max(widths) <= SMALL_COLS

    def reduce_body(*refs):
        g_refs = refs[:ns]
        gcw_ref, loss_ref, tot_ref, totc_ref = refs[ns:ns + 4]
        mine, buf, minec, bufc, send_sems, recv_sems = refs[ns + 4:]
        me, peers = _me_and_peers()

        mine[...] = jnp.zeros_like(mine)
        for i in range(ns):
            mine[i:i + 1, 0:widths[i]] = g_refs[i][...]
        mine[ns:ns + 1, 0:LANES] = jnp.broadcast_to(loss_ref[...], (1, LANES))
        minec[...] = gcw_ref[...]
        buf[me] = mine[...]
        bufc[me] = minec[...]
        copies = []
        for j, peer in enumerate(peers):
            copies.append(pltpu.make_async_remote_copy(
                src_ref=mine, dst_ref=buf.at[me], send_sem=send_sems.at[0, j], recv_sem=recv_sems.at[0, j],
                device_id=peer, device_id_type=MESH))
            copies.append(pltpu.make_async_remote_copy(
                src_ref=minec, dst_ref=bufc.at[me], send_sem=send_sems.at[1, j], recv_sem=recv_sems.at[1, j],
                device_id=peer, device_id_type=MESH))
        for cp in copies:
            cp.start()
        for cp in copies:
            cp.wait()
        tot = buf[0]
        totc = bufc[0]
        for i in range(1, N_DEV):
            tot = tot + buf[i]
            totc = totc + bufc[i]
        tot_ref[...] = tot
        totc_ref[...] = totc

    vm = pl.BlockSpec(memory_space=pltpu.VMEM)
    tot, totc = pl.pallas_call(
        reduce_body, name="small_reduce",
        in_specs=[vm] * (ns + 2), out_specs=[vm, vm],
        out_shape=[jax.ShapeDtypeStruct((SMALL_ROWS, SMALL_COLS), F32), jax.ShapeDtypeStruct((N_DEV, kk_, ch), F32)],
        scratch_shapes=[pltpu.VMEM((SMALL_ROWS, SMALL_COLS), F32), pltpu.VMEM((N_DEV, SMALL_ROWS, SMALL_COLS), F32),
                        pltpu.VMEM((N_DEV, kk_, ch), F32), pltpu.VMEM((N_DEV, N_DEV, kk_, ch), F32),
                        pltpu.SemaphoreType.DMA((2, N_DEV - 1)), pltpu.SemaphoreType.DMA((2, N_DEV - 1))],
        compiler_params=pltpu.CompilerParams(has_side_effects=True),
    )(*sg, gcw, loss)

    def adamw_body(*refs):
        tot_ref, totc_ref = refs[:2]
        w_refs, m_refs, v_refs = (refs[2 + i * (ns + 1):2 + (i + 1) * (ns + 1)] for i in range(3))
        outs = refs[3 * ns + 5:]
        loss_out = outs[0]
        go, do_, mo, vo = (outs[1 + i * (ns + 1):1 + (i + 1) * (ns + 1)] for i in range(4))
        me, _ = _me_and_peers()
        loss_out[...] = tot_ref[ns:ns + 1, 0:1]
        for i in range(ns + 1):
            g = tot_ref[i:i + 1, 0:widths[i]] if i < ns else totc_ref[me]
            delta, m2, v2 = _adamw_math(g, w_refs[i][...], m_refs[i][...], v_refs[i][...])
            go[i][...] = g
            do_[i][...] = delta
            mo[i][...] = m2
            vo[i][...] = v2

    shapes = [jax.ShapeDtypeStruct(t.shape, F32) for t in ws]
    res = pl.pallas_call(
        adamw_body, name="small_adamw",
        in_specs=[vm] * (3 * ns + 5), out_specs=[vm] * (4 * ns + 5),
        out_shape=[jax.ShapeDtypeStruct((1, 1), F32)] + shapes * 4,
    )(tot, totc, *ws, *ms, *vs)
    n1 = ns + 1
    return res[0], res[1:1 + n1], res[1 + n1:1 + 2 * n1], res[1 + 2 * n1:1 + 3 * n1], res[1 + 3 * n1:]


def _cast_shard(w, name):
    r = w.shape[0]
    return _rows(lambda t: t, [w], [], [(w.shape[1], CDT)], ts=_pick(r, (256, 128)), name=name)[0]


BIG = ["w_in", "w_mem_kv", "w_up", "w_sb_out", "w_ssd_out", "w_mem_out", "w_o", "w_down"]
LATE_W = BIG[1:]
COL_SHARDED = ("w_in", "w_mem_kv", "w_up")
SMALL = ["norm_mix_pre", "conv_b", "dt_bias", "a_log", "d_skip", "ssd_norm", "norm_mem",
         "norm_mix_post", "norm_mlp_pre", "norm_mlp_post"]
ALL_W = ["norm_mix_pre", "w_in", "conv_w", "conv_b", "dt_bias", "a_log", "d_skip", "ssd_norm", "norm_mem",
         "w_mem_kv", "w_sb_out", "w_ssd_out", "w_mem_out", "w_o", "norm_mix_post", "norm_mlp_pre", "w_up",
         "w_down", "norm_mlp_post"]
LANES = 128


def kernel(x, mem, norm_mix_pre, w_in, conv_w, conv_b, dt_bias, a_log, d_skip, ssd_norm, norm_mem, w_mem_kv, w_sb_out, w_ssd_out, w_mem_out, w_o, norm_mix_post, norm_mlp_pre, w_up, w_down, norm_mlp_post, loss_target, m_norm_mix_pre, m_w_in, m_conv_w, m_conv_b, m_dt_bias, m_a_log, m_d_skip, m_ssd_norm, m_norm_mem, m_w_mem_kv, m_w_sb_out, m_w_ssd_out, m_w_mem_out, m_w_o, m_norm_mix_post, m_norm_mlp_pre, m_w_up, m_w_down, m_norm_mlp_post, v_norm_mix_pre, v_w_in, v_conv_w, v_conv_b, v_dt_bias, v_a_log, v_d_skip, v_ssd_norm, v_norm_mem, v_w_mem_kv, v_w_sb_out, v_w_ssd_out, v_w_mem_out, v_w_o, v_norm_mix_post, v_norm_mlp_pre, v_w_up, v_w_down, v_norm_mlp_post):
    wd = dict(norm_mix_pre=norm_mix_pre, w_in=w_in, conv_w=conv_w, conv_b=conv_b, dt_bias=dt_bias, a_log=a_log,
              d_skip=d_skip, ssd_norm=ssd_norm, norm_mem=norm_mem, w_mem_kv=w_mem_kv, w_sb_out=w_sb_out,
              w_ssd_out=w_ssd_out, w_mem_out=w_mem_out, w_o=w_o, norm_mix_post=norm_mix_post,
              norm_mlp_pre=norm_mlp_pre, w_up=w_up, w_down=w_down, norm_mlp_post=norm_mlp_post)
    md = dict(norm_mix_pre=m_norm_mix_pre, w_in=m_w_in, conv_w=m_conv_w, conv_b=m_conv_b, dt_bias=m_dt_bias,
              a_log=m_a_log, d_skip=m_d_skip, ssd_norm=m_ssd_norm, norm_mem=m_norm_mem, w_mem_kv=m_w_mem_kv,
              w_sb_out=m_w_sb_out, w_ssd_out=m_w_ssd_out, w_mem_out=m_w_mem_out, w_o=m_w_o,
              norm_mix_post=m_norm_mix_post, norm_mlp_pre=m_norm_mlp_pre, w_up=m_w_up, w_down=m_w_down,
              norm_mlp_post=m_norm_mlp_post)
    vd = dict(norm_mix_pre=v_norm_mix_pre, w_in=v_w_in, conv_w=v_conv_w, conv_b=v_conv_b, dt_bias=v_dt_bias,
              a_log=v_a_log, d_skip=v_d_skip, ssd_norm=v_ssd_norm, norm_mem=v_norm_mem, w_mem_kv=v_w_mem_kv,
              w_sb_out=v_w_sb_out, w_ssd_out=v_w_ssd_out, w_mem_out=v_w_mem_out, w_o=v_w_o,
              norm_mix_post=v_norm_mix_post, norm_mlp_pre=v_norm_mlp_pre, w_up=v_w_up, w_down=v_w_down,
              norm_mlp_post=v_norm_mlp_post)
    w_in_t, m_in_t, v_in_t = (t["w_in"][0].T for t in (wd, md, vd))
    shards = {n: _cast_shard(w_in_t if n == "w_in" else wd[n][0], "cast_" + n) for n in BIG}
    w_in_g, conv_w_g = _gather_two_level([shards["w_in"], wd["conv_w"][0]], "gather_w_in")
    wt = {"w_in": w_in_g.reshape(N_DEV * w_in_g.shape[1], w_in_g.shape[2])}
    ch = conv_w_g.shape[2]

    p = {n: wd[n] for n in SMALL}
    p["conv_w"] = conv_w_g.transpose(1, 0, 2).reshape(CONV_K, N_DEV * ch)
    loss, grad_x, gw, sg, late_received = _local_step(x[0], mem[0], loss_target[0], p, wt,
                                                      [shards[n] for n in LATE_W])

    received = dict(zip(LATE_W, late_received))
    w_in_windows = gw["w_in"]

    grads, deltas, new_m, new_v = {}, {}, {}, {}
    for n in BIG:
        if n == "w_in":
            res = [t.T for t in _adamw_windows(w_in_windows, w_in_t, m_in_t, v_in_t, "adamw_" + n)]
        else:
            res = _adamw_reduce(received[n], wd[n][0], md[n][0], vd[n][0], "adamw_" + n)
        grads[n], deltas[n], new_m[n], new_v[n] = (t[None] for t in res)
    small_names = SMALL + ["conv_w"]
    gcw = sg["conv_w"].reshape(CONV_K, N_DEV, ch).transpose(1, 0, 2)
    small_of = lambda dct: [dct[n] for n in SMALL] + [dct["conv_w"][0]]
    loss_red, g_s, d_s, m_s, v_s = _small_step([sg[n] for n in SMALL], gcw, loss, small_of(wd), small_of(md),
                                               small_of(vd))
    for i, n in enumerate(small_names):
        shape = wd[n].shape
        grads[n], deltas[n], new_m[n], new_v[n] = (t.reshape(shape) for t in (g_s[i], d_s[i], m_s[i], v_s[i]))
    loss_out = loss_red.reshape(())

    return (loss_out, grad_x[None], *[grads[n] for n in ALL_W], *[deltas[n] for n in ALL_W],
            *[new_m[n] for n in ALL_W], *[new_v[n] for n in ALL_W])
```

```python
import functools

import jax
import jax.numpy as jnp
from jax import lax
from jax.experimental import pallas as pl
from jax.experimental.pallas import tpu as pltpu

F32 = jnp.float32
BF16 = jnp.bfloat16
CDT = jnp.bfloat16
EPS = 1e-6
VMEM_LIMIT = 56 * 1024 * 1024

N_DEV = 8
D_MODEL = 1024
SB_H, SB_DH = 16, 64
SSD_G, SSD_R, SSD_P, SSD_N, SSD_L = 4, 8, 64, 128, 128
SSD_H = SSD_G * SSD_R
SSD_INNER = SSD_H * SSD_P
CONV_K = 4
CONV_DIM = SSD_INNER + 2 * SSD_G * SSD_N
MEM_H, MEM_DH = 4, 256
DT_PAD = 128
SB_TQ, SB_BK = 2048, 256
CONV_PAD = 8
MM_TILE, MM_TILE_K = 1024, 2048

ADAM_LR, ADAM_B1, ADAM_B2, ADAM_EPS, ADAM_WD, ADAM_STEP = 0.001, 0.9, 0.999, 1e-08, 0.01, 10

NT = (((1,), (1,)), ((), ()))
TN = (((0,), (0,)), ((), ()))
NN = (((1,), (0,)), ((), ()))


def _cparams(sem=None):
    return pltpu.CompilerParams(dimension_semantics=sem, vmem_limit_bytes=VMEM_LIMIT)


def _pick(n, cands):
    for c in cands:
        if n % c == 0:
            return c
    return n


def _dot(a, b, dims=NN):
    return lax.dot_general(a.astype(CDT), b.astype(CDT), dims, preferred_element_type=F32)


def _split_dot(x, t, left, pieces):
    if CDT == F32:
        return lax.dot_general(t, x, NN, preferred_element_type=F32) if left else \
            lax.dot_general(x, t, NN, preferred_element_type=F32)
    acc = None
    rem = x
    for _ in range(pieces):
        hi = rem.astype(BF16)
        rem = rem - hi.astype(F32)
        d = lax.dot_general(t, hi, NN, preferred_element_type=F32) if left else \
            lax.dot_general(hi, t, NN, preferred_element_type=F32)
        acc = d if acc is None else acc + d
    return acc


def _iota(shape, dim):
    return lax.broadcasted_iota(jnp.int32, shape, dim)


def _sigmoid(x):
    return 1.0 / (1.0 + jnp.exp(-x))


def _silu(x):
    return x * _sigmoid(x)


def _dsilu(x):
    s = _sigmoid(x)
    return s * (1.0 + x * (1.0 - s))


def _softplus(x):
    return jnp.maximum(x, 0.0) + jnp.log(1.0 + jnp.exp(-jnp.abs(x)))


def _rms(x, g):
    r = lax.rsqrt(jnp.mean(x * x, axis=-1, keepdims=True) + EPS)
    return x * r * g


def _rms_bwd(x, g, dy):
    r = lax.rsqrt(jnp.mean(x * x, axis=-1, keepdims=True) + EPS)
    n = x * r
    dn = dy * g
    dx = r * (dn - n * jnp.mean(dn * n, axis=-1, keepdims=True))
    dg = jnp.sum(dy * n, axis=0, keepdims=True)
    return dx, dg


def _mm(a, b, *, ta=False, tb=False, out_dtype=F32, name, b_off=0, n=None, b_koff=0, into=None, into_off=0,
        into_rows=False, hosted=None, epilogue=None):
    m = a.shape[1] if ta else a.shape[0]
    k = a.shape[0] if ta else a.shape[1]
    if n is None:
        n = b.shape[0] if tb else b.shape[1]
    assert b_koff + k <= (b.shape[1] if tb else b.shape[0])
    bm = _pick(m, (MM_TILE, 512, 256, 128))
    bn = _pick(n, (MM_TILE, 512, 256, 128))
    bk = next(c for c in (MM_TILE_K, 1024, 512, 256, 128, k) if k % c == 0 and b_koff % c == 0)
    nk = k // bk
    assert b_off % bn == 0 and into_off % (bm if into_rows else bn) == 0
    jb, kb = b_off // bn, b_koff // bk
    io, jo = (into_off // bm, 0) if into_rows else (0, into_off // bn)
    dims = (((0 if ta else 1,), (1 if tb else 0,)), ((), ()))
    grid = (m // bm, n // bn, nk)
    off = 1 if into is not None else 0
    nh_in = len(hosted.ins) if hosted else 0
    nh_out = len(hosted.shapes) if hosted else 0
    epi_fn, extras, out_dtypes = epilogue if epilogue else (lambda p: (p,), [], [out_dtype])
    ne, no = len(extras), len(out_dtypes)
    assert not (epilogue and into is not None)

    def body(a_ref, b_ref, *rest):
        e_refs = rest[off:off + ne]
        o_refs = rest[off + ne + nh_in:off + ne + nh_in + no]
        acc_ref = rest[off + ne + nh_in + no + nh_out]

        def emit(total):
            res = epi_fn(total, *[e[...] for e in e_refs])
            for o_ref, val in zip(o_refs, res):
                o_ref[...] = val.astype(o_ref.dtype)

        if hosted:
            h_refs = (rest[off + ne:off + ne + nh_in],
                      rest[off + ne + nh_in + no:off + ne + nh_in + no + nh_out],
                      rest[off + ne + nh_in + no + nh_out + 1:])
            step = (pl.program_id(0) * grid[1] + pl.program_id(1)) * grid[2] + pl.program_id(2)

            @pl.when(step == 0)
            def _():
                for cp in hosted.copies(*h_refs):
                    cp.start()

        part = _dot(a_ref[...], b_ref[...], dims)
        if nk == 1:
            emit(part)
        else:
            kk = pl.program_id(2)

            @pl.when(kk == 0)
            def _():
                acc_ref[...] = part

            @pl.when(jnp.logical_and(kk > 0, kk < nk - 1))
            def _():
                acc_ref[...] += part

            @pl.when(kk == nk - 1)
            def _():
                emit(acc_ref[...] + part)

        if hosted:
            @pl.when(step == grid[0] * grid[1] * grid[2] - 1)
            def _():
                for cp in hosted.copies(*h_refs):
                    cp.wait()

    a_spec = pl.BlockSpec((bk, bm), lambda i, j, kk: (kk, i)) if ta else \
        pl.BlockSpec((bm, bk), lambda i, j, kk: (i, kk))
    b_spec = pl.BlockSpec((bn, bk), lambda i, j, kk: (j + jb, kk + kb)) if tb else \
        pl.BlockSpec((bk, bn), lambda i, j, kk: (kk + kb, j + jb))
    extra = {} if into is None else {"input_output_aliases": {2: 0}}
    out_shapes = [jax.ShapeDtypeStruct((m, n), dt) for dt in out_dtypes] if into is None else \
        [jax.ShapeDtypeStruct(into.shape, into.dtype)]
    block = pl.BlockSpec((bm, bn), lambda i, j, kk: (i, j))
    res = pl.pallas_call(
        body, name=name, grid=grid,
        in_specs=[a_spec, b_spec] + ([] if into is None else [pl.BlockSpec(memory_space=pl.ANY)])
        + [block] * ne + [HBM] * nh_in,
        out_specs=[pl.BlockSpec((bm, bn), lambda i, j, kk: (i + io, j + jo))] * no + [HBM] * nh_out,
        out_shape=out_shapes + (list(hosted.shapes) if hosted else []),
        scratch_shapes=[pltpu.VMEM((bm, bn) if nk > 1 else (8, 128), F32)] + (list(hosted.sems) if hosted else []),
        compiler_params=_cparams(("arbitrary",) * 3 if hosted else ("parallel", "parallel", "arbitrary")),
        **extra,
    )(*((a, b) if into is None else (a, b, into)), *extras, *(hosted.ins if hosted else ()))
    return res if (hosted or epilogue) else res[0]


def _rows(fn, tiled, params, outs, accs=(), *, ts, name):
    s = tiled[0].shape[0]
    ts = min(ts, s)
    assert s % ts == 0
    nt, npar, no, na = len(tiled), len(params), len(outs), len(accs)

    def body(*refs):
        i = pl.program_id(0)
        vals = [r[...] for r in refs[:nt + npar]]
        res = fn(*vals)
        if not isinstance(res, (tuple, list)):
            res = (res,)
        orefs = refs[nt + npar:nt + npar + no]
        arefs = refs[nt + npar + no:]
        for r_, val in zip(orefs, res[:no]):
            r_[...] = val.astype(r_.dtype)
        if na:
            @pl.when(i == 0)
            def _():
                for r_ in arefs:
                    r_[...] = jnp.zeros_like(r_)

            for r_, val in zip(arefs, res[no:]):
                r_[...] += jnp.broadcast_to(val, r_.shape)

    in_specs = [pl.BlockSpec((ts, a.shape[1]), lambda i: (i, 0)) for a in tiled]
    in_specs += [pl.BlockSpec(p.shape, lambda i: (0, 0)) for p in params]
    out_specs = [pl.BlockSpec((ts, w), lambda i: (i, 0)) for (w, _) in outs]
    out_specs += [pl.BlockSpec(shape, lambda i: (0, 0)) for shape in accs]
    out_shape = [jax.ShapeDtypeStruct((s, w), dt) for (w, dt) in outs]
    out_shape += [jax.ShapeDtypeStruct(shape, F32) for shape in accs]
    res = pl.pallas_call(
        body, name=name, grid=(s // ts,),
        in_specs=in_specs, out_specs=out_specs, out_shape=out_shape,
        compiler_params=_cparams(("arbitrary",)),
    )(*tiled, *params)
    return res


def _sb_block(qs, kb, diag):
    tq, bk = qs.shape[0], kb.shape[0]
    z = _dot(qs, kb, NT)
    lb = jnp.minimum(z, 0.0) - jnp.log(1.0 + jnp.exp(-jnp.abs(z)))
    lk = lb - z
    if diag is None:
        return None, lb, lk
    causal = (diag + _iota((tq, bk), 1)) < _iota((tq, bk), 0)
    return causal, lb, jnp.where(causal, lk, 0.0)


def _fused_exchange(scatter, ncols, nsteps):
    def hooks(ins, outs, sems):
        step = pl.program_id(0) * ncols + pl.program_id(1)

        @pl.when(step == 0)
        def _():
            for cp in _exchange_copies(ins, outs, *sems, scatter):
                cp.start()

        def finish():
            @pl.when(step == nsteps - 1)
            def _():
                for cp in _exchange_copies(ins, outs, *sems, scatter):
                    cp.wait()
        return finish
    return hooks


SB_PAIR = 128


def _sb_fwd_pairs(qkv, comm=()):
    s, d3 = qkv.shape
    d = d3 // 3
    npair = d // SB_PAIR
    tq, bk = min(SB_TQ, s), min(SB_BK, s)
    scale = SB_DH ** -0.5
    nc = len(comm)
    hooks = _fused_exchange(False, s // tq, npair * (s // tq))

    def body(q_ref, k_ref, v_ref, *rest):
        y_ref, lt_ref = rest[nc:nc + 2]
        if nc:
            finish = hooks(rest[:nc], rest[nc + 2:2 * nc + 2], rest[2 * nc + 2:])
        i = pl.program_id(1)
        q0 = i * tq
        q2 = q_ref[...] * scale
        lane_head = (_iota((1, SB_PAIR), 1) >= SB_DH).astype(jnp.int32)
        tri = (_iota((bk, bk), 0) > _iota((bk, bk), 1)).astype(CDT)

        def head(hh, y):
            mine = lane_head == hh
            qs = jnp.where(mine, q2, jnp.zeros_like(q2))

            def step(k0, carry, diag, r0=0):
                cf, acc = carry
                kb = k_ref[pl.ds(k0, bk), :]
                vb = v_ref[pl.ds(k0, bk), :]
                causal, lb, lk = _sb_block(qs[r0:], kb, diag)
                w = jnp.exp(lb + cf + _split_dot(lk, tri, False, 2))
                if causal is not None:
                    w = jnp.where(causal, w, 0.0)
                return cf + jnp.sum(lk, axis=1, keepdims=True), acc + _dot(w, vb)

            carry = (jnp.zeros((tq, 1), F32), jnp.zeros((tq, SB_PAIR), F32))
            for dd in reversed(range(tq // bk)):
                r0 = dd * bk
                sub = step(pl.multiple_of(q0 + r0, bk), tuple(t[r0:] for t in carry), 0, r0)
                carry = tuple(jnp.concatenate([t[:r0], u], axis=0) if r0 else u for t, u in zip(carry, sub))
            nfull = q0 // bk
            cf, acc = lax.fori_loop(
                0, nfull, lambda jj, c: step(pl.multiple_of((nfull - 1 - jj) * bk, bk), c, None), carry)
            lt_ref[hh] = cf
            return jnp.where(mine, acc, y)

        y_ref[...] = lax.fori_loop(0, 2, head, jnp.zeros((tq, SB_PAIR), F32)).astype(y_ref.dtype)
        if nc:
            finish()

    return pl.pallas_call(
        body, name="sb_fwd", grid=(npair, s // tq),
        in_specs=[pl.BlockSpec((tq, SB_PAIR), lambda a, i: (i, a)),
                  pl.BlockSpec((s, SB_PAIR), lambda a, i: (0, npair + a)),
                  pl.BlockSpec((s, SB_PAIR), lambda a, i: (0, 2 * npair + a))] + [HBM] * nc,
        out_specs=[pl.BlockSpec((tq, SB_PAIR), lambda a, i: (i, a)),
                   pl.BlockSpec((2, tq, 1), lambda a, i: (a, i, 0))] + [HBM] * nc,
        out_shape=[jax.ShapeDtypeStruct((s, d), CDT), jax.ShapeDtypeStruct((2 * npair, s, 1), F32)]
        + _exchange_shapes(comm, False),
        scratch_shapes=_exchange_sems(nc) if nc else [],
        compiler_params=_cparams(("arbitrary", "arbitrary")),
    )(qkv, qkv, qkv, *comm)


def _sb_bwd_pairs(qkv, ltot, dy, comm=()):
    s, d3 = qkv.shape
    d = d3 // 3
    npair = d // SB_PAIR
    tq, bk = min(SB_TQ, s), min(SB_BK, s)
    scale = SB_DH ** -0.5
    nc = len(comm)
    hooks = _fused_exchange(True, s // tq, npair * (s // tq))

    def body(q_ref, k_ref, v_ref, lt_ref, dy_ref, *rest):
        dq_ref, dk_ref, dv_ref = rest[nc:nc + 3]
        if nc:
            finish = hooks(rest[:nc], rest[nc + 3:2 * nc + 3], rest[2 * nc + 3:])
        i = pl.program_id(1)

        @pl.when(i == 0)
        def _():
            dk_ref[...] = jnp.zeros_like(dk_ref)
            dv_ref[...] = jnp.zeros_like(dv_ref)

        q0 = i * tq
        q2 = q_ref[...] * scale
        do2 = dy_ref[...].astype(CDT)
        lane_head = (_iota((1, SB_PAIR), 1) >= SB_DH).astype(jnp.int32)
        tri_le = (_iota((bk, bk), 0) <= _iota((bk, bk), 1)).astype(CDT)
        tri_lt = (_iota((bk, bk), 0) < _iota((bk, bk), 1)).astype(CDT)

        def head(hh, dq_all):
            mine = lane_head == hh
            qs = jnp.where(mine, q2, jnp.zeros_like(q2))
            dov = jnp.where(mine, do2, jnp.zeros_like(do2))
            ltot_h = lt_ref[hh]

            def step(k0, carry, diag, r0=0):
                cf, cg, dq = carry
                kb = k_ref[pl.ds(k0, bk), :]
                vb = v_ref[pl.ds(k0, bk), :]
                causal, lb, lk = _sb_block(qs[r0:], kb, diag)
                w = jnp.exp(lb + ltot_h[r0:] - (cf + _split_dot(lk, tri_le, False, 2)))
                if causal is not None:
                    w = jnp.where(causal, w, 0.0)
                g = w * _dot(dov[r0:], vb, NT)
                gsum = cg + _split_dot(g, tri_lt, False, 2)
                dz = g - (g + gsum) * jnp.exp(lb)
                if causal is not None:
                    dz = jnp.where(causal, dz, 0.0)
                dzc = dz.astype(CDT)
                dk_ref[pl.ds(k0, bk), :] += _dot(dzc, qs[r0:], TN)
                dv_ref[pl.ds(k0, bk), :] += _dot(w, dov[r0:], TN)
                kbm = jnp.where(mine, kb, jnp.zeros_like(kb))
                return (cf + jnp.sum(lk, axis=1, keepdims=True), cg + jnp.sum(g, axis=1, keepdims=True),
                        dq + _dot(dzc, kbm))

            carry = (jnp.zeros((tq, 1), F32), jnp.zeros((tq, 1), F32), jnp.zeros((tq, SB_PAIR), F32))
            carry = lax.fori_loop(0, q0 // bk, lambda jj, c: step(pl.multiple_of(jj * bk, bk), c, None), carry)
            for dd in range(tq // bk):
                r0 = dd * bk
                sub = step(pl.multiple_of(q0 + r0, bk), tuple(t[r0:] for t in carry), 0, r0)
                carry = tuple(jnp.concatenate([t[:r0], u], axis=0) if r0 else u for t, u in zip(carry, sub))
            return dq_all + carry[2]

        dq_ref[...] = lax.fori_loop(0, 2, head, jnp.zeros((tq, SB_PAIR), F32)) * scale
        if nc:
            finish()

    tile = pl.BlockSpec((tq, SB_PAIR), lambda a, i: (i, a))
    acc = pl.BlockSpec((s, SB_PAIR), lambda a, i: (0, a))
    shp = jax.ShapeDtypeStruct((s, d), F32)
    return pl.pallas_call(
        body, name="sb_bwd", grid=(npair, s // tq),
        in_specs=[tile, pl.BlockSpec((s, SB_PAIR), lambda a, i: (0, npair + a)),
                  pl.BlockSpec((s, SB_PAIR), lambda a, i: (0, 2 * npair + a)),
                  pl.BlockSpec((2, tq, 1), lambda a, i: (a, i, 0)), tile] + [HBM] * nc,
        out_specs=[tile, acc, acc] + [HBM] * nc,
        out_shape=[shp, shp, shp] + _exchange_shapes(comm, True),
        scratch_shapes=_exchange_sems(nc) if nc else [],
        compiler_params=_cparams(("arbitrary", "arbitrary")),
    )(qkv, qkv, qkv, ltot, dy, *comm)


def _pick_lane(tile, r):
    return jnp.sum(jnp.where(_iota(tile.shape, 1) == r, tile, 0.0), axis=1, keepdims=True)


def _pick_row(tile, r):
    return jnp.sum(jnp.where(_iota(tile.shape, 0) == r, tile, 0.0), axis=0, keepdims=True)


SSD_CPS = 4


def _ssd_chunk_common(cv, bv, dac, dar):
    l = SSD_L
    tdt = F32 if CDT == F32 else BF16
    lower = (_iota((l, l), 1) <= _iota((l, l), 0)).astype(tdt)
    upper = (_iota((l, l), 0) <= _iota((l, l), 1)).astype(tdt)
    return _dot(cv, bv, NT), _split_dot(dac, lower, True, 3), _split_dot(dar, upper, False, 3)


def _ssd_fwd_g(xh, dtc, dac, dar, dsk, xbc):
    hh, s, p = xh.shape
    l, n, g_, r_ = SSD_L, SSD_N, SSD_G, SSD_R
    nc = s // l
    cps = SSD_CPS if nc % SSD_CPS == 0 else 1
    lb = cps * l
    boff = SSD_INNER // n
    coff = boff + g_

    def body(x_ref, dtc_ref, dac_ref, dar_ref, dsk_ref, b_ref, c_ref, y_ref, st_ref, state_ref):
        c = pl.program_id(1)

        @pl.when(c == 0)
        def _():
            state_ref[...] = jnp.zeros_like(state_ref)

        mask = _iota((l, l), 1) <= _iota((l, l), 0)
        dskv = dsk_ref[...]
        for cc in range(cps):
            rows = pl.ds(cc * l, l)
            cv, bv, dtcv = c_ref[rows, :], b_ref[rows, :], dtc_ref[rows, :]
            cb, acs_c, acs_r = _ssd_chunk_common(cv, bv, dac_ref[rows, :], dar_ref[:, rows])
            for r in range(r_):
                a_col = _pick_lane(acs_c, r)
                a_row = _pick_row(acs_r, r)
                dsk_h = _pick_lane(dskv, r)
                a_b = jnp.broadcast_to(a_col, (l, l))
                dt_b = jnp.broadcast_to(_pick_lane(dtcv, r), (l, p))
                xv = x_ref[r, rows, :]
                xd = xv * dt_b
                decay = jnp.where(mask, jnp.exp(jnp.minimum(a_b - a_row, 0.0)), 0.0)
                hprev = state_ref[r]
                y = _dot(cb * decay, xd) + jnp.exp(a_b)[:, :p] * _dot(cv, hprev)
                y_ref[r, rows, :] = y + dsk_h * xv
                a_end = a_col[l - 1:l, :]
                st_ref[r, cc] = hprev
                state_ref[r] = hprev * jnp.exp(a_end) + _dot(bv, xd * jnp.exp(a_end - a_b)[:, :p], TN)

    return pl.pallas_call(
        body, name="ssd_fwd", grid=(g_, nc // cps),
        in_specs=[pl.BlockSpec((r_, lb, p), lambda g, c: (g, c, 0)),
                  pl.BlockSpec((None, lb, r_), lambda g, c: (g, c, 0)),
                  pl.BlockSpec((None, lb, r_), lambda g, c: (g, c, 0)),
                  pl.BlockSpec((None, r_, lb), lambda g, c: (g, 0, c)),
                  pl.BlockSpec((None, 1, r_), lambda g, c: (g, 0, 0)),
                  pl.BlockSpec((lb, n), lambda g, c: (c, (boff + g))),
                  pl.BlockSpec((lb, n), lambda g, c: (c, (coff + g)))],
        out_specs=[pl.BlockSpec((r_, lb, p), lambda g, c: (g, c, 0)),
                   pl.BlockSpec((r_, cps, n, p), lambda g, c: (g, c, 0, 0))],
        out_shape=[jax.ShapeDtypeStruct((hh, s, p), F32),
                   jax.ShapeDtypeStruct((hh, nc, n, p), F32)],
        scratch_shapes=[pltpu.VMEM((r_, n, p), F32)],
        compiler_params=_cparams(("parallel", "arbitrary")),
    )(xh, dtc, dac, dar, dsk, xbc, xbc)


def _ssd_bwd_g(xh, dtc, dac, dar, dsk, xbc, st, dy):
    hh, s, p = xh.shape
    l, n, g_, r_ = SSD_L, SSD_N, SSD_G, SSD_R
    nc = s // l
    cps = SSD_CPS if nc % SSD_CPS == 0 else 1
    lb, ncb = cps * l, nc // cps
    boff = SSD_INNER // n
    coff = boff + g_

    def body(x_ref, dtc_ref, dac_ref, dar_ref, dsk_ref, b_ref, c_ref, st_ref, dy_ref,
             dx_ref, dda_ref, dxx_ref, db_ref, dc_ref, dah_ref, ddsk_ref, dstate_ref):
        c = pl.program_id(1)

        @pl.when(c == 0)
        def _():
            dstate_ref[...] = jnp.zeros_like(dstate_ref)
            dah_ref[...] = jnp.zeros_like(dah_ref)
            ddsk_ref[...] = jnp.zeros_like(ddsk_ref)

        il = _iota((l, l), 0)
        isx = _iota((l, l), 1)
        tdt = F32 if CDT == F32 else BF16
        t1 = (isx >= il).astype(tdt)
        lane = _iota((l, r_), 1)
        lane1 = _iota((1, r_), 1)
        dskv = dsk_ref[...]
        for cc in reversed(range(cps)):
            rows = pl.ds(cc * l, l)
            cv, bv, dtcv = c_ref[rows, :], b_ref[rows, :], dtc_ref[rows, :]
            cb, acs_c, acs_r = _ssd_chunk_common(cv, bv, dac_ref[rows, :], dar_ref[:, rows])
            dda_all = jnp.zeros((l, r_), F32)
            dxx_all = jnp.zeros((l, r_), F32)
            dah_all = jnp.zeros((1, r_), F32)
            ddsk_all = jnp.zeros((1, r_), F32)
            db_acc = jnp.zeros((l, n), F32)
            dc_acc = jnp.zeros((l, n), F32)
            md_sum = jnp.zeros((l, l), F32)
            cb_t = _dot(bv, cv, NT)
            cv_t = cv.T
            for r in range(r_):
                a_col = _pick_lane(acs_c, r)
                a_row = _pick_row(acs_r, r)
                dt_col = _pick_lane(dtcv, r)
                dsk_h = _pick_lane(dskv, r)
                xv = x_ref[r, rows, :]
                dyv = dy_ref[r, rows, :]
                a_b = jnp.broadcast_to(a_col, (l, l))
                dt_b = jnp.broadcast_to(dt_col, (l, p))
                xd = xv * dt_b
                decay = jnp.where(isx <= il, jnp.exp(jnp.minimum(a_b - a_row, 0.0)), 0.0)
                decay_t = jnp.where(isx >= il, jnp.exp(jnp.minimum(a_row - a_b, 0.0)), 0.0)
                dhn = dstate_ref[r]
                hc = st_ref[r, cc]
                a_end = a_col[l - 1:l, :]
                ea_b = jnp.exp(a_b)
                dte_b = jnp.exp(a_end - a_b)

                dx_state = dte_b[:, :p] * _dot(bv, dhn)
                dxd = _dot(cb_t * decay_t, dyv) + dx_state
                md = decay * _dot(dyv, xd, NT)
                md_sum = md_sum + md
                dc_acc = dc_acc + ea_b * _dot(dyv, hc, NT)
                db_acc = db_acc + dte_b * _dot(xd, dhn, NT)
                dstate_ref[r] = jnp.exp(a_end) * dhn + _dot(cv_t, dyv * ea_b[:, :p])

                yoff = ea_b[:, :p] * _dot(cv, hc)
                xdx = xd * dx_state
                vec = jnp.sum(dyv * yoff - xdx, axis=1, keepdims=True)
                total = lambda t: jnp.sum(jnp.sum(t, axis=0, keepdims=True), axis=1, keepdims=True)
                end_term = total(xdx) + jnp.exp(a_end) * total(hc * dhn)
                zmat = _split_dot(md * cb, t1, True, 2)
                span = jnp.sum(jnp.where(isx < il, zmat, 0.0), axis=1, keepdims=True)
                rc = _split_dot(jnp.broadcast_to(vec, (l, 128)), t1, True, 2)[:, :1]
                dda = span + rc + end_term
                dda_all = jnp.where(lane == r, dda, dda_all)
                dxx_all = jnp.where(lane == r, jnp.sum(dxd * xv, axis=1, keepdims=True), dxx_all)
                dx_ref[r, rows, :] = dxd * dt_b + dsk_h * dyv
                dah_all = jnp.where(lane1 == r, jnp.sum(dda * dt_col, axis=0, keepdims=True), dah_all)
                ddsk_all = jnp.where(lane1 == r, total(dyv * xv), ddsk_all)
            dda_ref[rows, :] = dda_all
            dxx_ref[rows, :] = dxx_all
            db_ref[rows, :] = db_acc + _dot(md_sum, cv, TN)
            dc_ref[rows, :] = dc_acc + _dot(md_sum, bv)
            dah_ref[...] += dah_all
            ddsk_ref[...] += ddsk_all

    rev = lambda c: ncb - 1 - c
    xspec = pl.BlockSpec((r_, lb, p), lambda g, c: (g, rev(c), 0))
    cspec = pl.BlockSpec((None, lb, r_), lambda g, c: (g, rev(c), 0))
    hspec = pl.BlockSpec((None, 1, r_), lambda g, c: (g, 0, 0))
    return pl.pallas_call(
        body, name="ssd_bwd", grid=(g_, ncb),
        in_specs=[xspec, cspec, cspec,
                  pl.BlockSpec((None, r_, lb), lambda g, c: (g, 0, rev(c))),
                  hspec,
                  pl.BlockSpec((lb, n), lambda g, c: (rev(c), boff + g)),
                  pl.BlockSpec((lb, n), lambda g, c: (rev(c), coff + g)),
                  pl.BlockSpec((r_, cps, n, p), lambda g, c: (g, rev(c), 0, 0)),
                  xspec],
        out_specs=[xspec, cspec, cspec,
                   pl.BlockSpec((lb, n), lambda g, c: (rev(c), g)),
                   pl.BlockSpec((lb, n), lambda g, c: (rev(c), g)),
                   hspec, hspec],
        out_shape=[jax.ShapeDtypeStruct((hh, s, p), F32),
                   jax.ShapeDtypeStruct((g_, s, r_), F32),
                   jax.ShapeDtypeStruct((g_, s, r_), F32),
                   jax.ShapeDtypeStruct((s, g_ * n), F32),
                   jax.ShapeDtypeStruct((s, g_ * n), F32),
                   jax.ShapeDtypeStruct((g_, 1, r_), F32),
                   jax.ShapeDtypeStruct((g_, 1, r_), F32)],
        scratch_shapes=[pltpu.VMEM((r_, n, p), F32)],
        compiler_params=_cparams(("parallel", "arbitrary")),
    )(xh, dtc, dac, dar, dsk, xbc, xbc, st, dy)


CONV_TC = 256
CONV_RC = 128


def _conv_taps(x_ref, head_ref, rc):
    base = CONV_PAD - (CONV_K - 1)
    head_ref[pl.ds(0, CONV_PAD), :] = jnp.zeros((CONV_PAD, head_ref.shape[1]), F32)
    head_ref[pl.ds(CONV_PAD, rc), :] = x_ref[pl.ds(0, rc), :]

    def tap(t0, kk):
        if t0 == 0:
            return head_ref[pl.ds(base + kk, rc), :]
        return x_ref[pl.ds(t0 - (CONV_K - 1) + kk, rc), :]
    return tap


def _conv_fwd(x, w, b):
    s, ch = x.shape
    rc = min(CONV_RC, s)

    def body(x_ref, w_ref, b_ref, pre_ref, act_ref, head_ref):
        wv = w_ref[...]
        tap = _conv_taps(x_ref, head_ref, rc)
        for t0 in range(0, s, rc):
            acc = jnp.broadcast_to(b_ref[...], (rc, CONV_TC))
            for kk in range(CONV_K):
                acc = acc + wv[kk:kk + 1, :] * tap(t0, kk)
            pre_ref[pl.ds(t0, rc), :] = acc
            act_ref[pl.ds(t0, rc), :] = _silu(acc)

    col = pl.BlockSpec((s, CONV_TC), lambda j: (0, j))
    shp = jax.ShapeDtypeStruct((s, ch), F32)
    return pl.pallas_call(
        body, name="conv_fwd", grid=(ch // CONV_TC,),
        in_specs=[col, pl.BlockSpec((CONV_K, CONV_TC), lambda j: (0, j)),
                  pl.BlockSpec((1, CONV_TC), lambda j: (0, j))],
        out_specs=[col, col],
        out_shape=[shp, shp],
        scratch_shapes=[pltpu.VMEM((CONV_PAD + rc, CONV_TC), F32)],
        compiler_params=_cparams(("parallel",)),
    )(x, w, b)


def _conv_bwd(x, pre, dact, w):
    s, ch = x.shape
    rc = min(CONV_RC, s)

    def body(x_ref, pre_ref, da_ref, w_ref, dx_ref, dw_ref, db_ref, dpre_ref, head_ref):
        wv = w_ref[...]
        tap = _conv_taps(x_ref, head_ref, rc)
        for t0 in range(0, s, rc):
            dpre_ref[pl.ds(t0, rc), :] = da_ref[pl.ds(t0, rc), :] * _dsilu(pre_ref[pl.ds(t0, rc), :])
        dpre_ref[pl.ds(s, CONV_PAD), :] = jnp.zeros((CONV_PAD, CONV_TC), F32)
        dws = [jnp.zeros((1, CONV_TC), F32) for _ in range(CONV_K)]
        dbs = jnp.zeros((1, CONV_TC), F32)
        for t0 in range(0, s, rc):
            acc = jnp.zeros((rc, CONV_TC), F32)
            dp = dpre_ref[pl.ds(t0, rc), :]
            for kk in range(CONV_K):
                acc = acc + wv[kk:kk + 1, :] * dpre_ref[pl.ds(t0 + CONV_K - 1 - kk, rc), :]
                dws[kk] = dws[kk] + jnp.sum(dp * tap(t0, kk), axis=0, keepdims=True)
            dbs = dbs + jnp.sum(dp, axis=0, keepdims=True)
            dx_ref[pl.ds(t0, rc), :] = acc.astype(dx_ref.dtype)
        for kk in range(CONV_K):
            dw_ref[kk:kk + 1, :] = dws[kk]
        db_ref[...] = dbs

    col = pl.BlockSpec((s, CONV_TC), lambda j: (0, j))
    return pl.pallas_call(
        body, name="conv_bwd", grid=(ch // CONV_TC,),
        in_specs=[col, col, col, pl.BlockSpec((CONV_K, CONV_TC), lambda j: (0, j))],
        out_specs=[col, pl.BlockSpec((CONV_K, CONV_TC), lambda j: (0, j)),
                   pl.BlockSpec((1, CONV_TC), lambda j: (0, j))],
        out_shape=[jax.ShapeDtypeStruct((s, ch), CDT),
                   jax.ShapeDtypeStruct((CONV_K, ch), F32),
                   jax.ShapeDtypeStruct((1, ch), F32)],
        scratch_shapes=[pltpu.VMEM((s + CONV_PAD, CONV_TC), F32), pltpu.VMEM((CONV_PAD + rc, CONV_TC), F32)],
        compiler_params=_cparams(("parallel",)),
    )(x, pre, dact, w)


MEM_TS = 512


def _mem_fwd(mq, kv):
    s = mq.shape[0]
    m = kv.shape[0]
    ts = min(MEM_TS, s)
    scale = MEM_DH ** -0.5

    def body(q_ref, k_ref, v_ref, o_ref):
        sc = _dot(q_ref[...], k_ref[...], NT) * scale
        e = jnp.exp(sc - jnp.max(sc, axis=1, keepdims=True))
        pr = e / jnp.sum(e, axis=1, keepdims=True)
        o_ref[...] = _dot(pr, v_ref[...]).astype(o_ref.dtype)

    return pl.pallas_call(
        body, name="mem_fwd", grid=(MEM_H, s // ts),
        in_specs=[pl.BlockSpec((ts, MEM_DH), lambda a, i: (i, a)),
                  pl.BlockSpec((m, MEM_DH), lambda a, i: (0, a)),
                  pl.BlockSpec((m, MEM_DH), lambda a, i: (0, MEM_H + a))],
        out_specs=pl.BlockSpec((ts, MEM_DH), lambda a, i: (i, a)),
        out_shape=jax.ShapeDtypeStruct((s, MEM_H * MEM_DH), CDT),
        compiler_params=_cparams(("parallel", "arbitrary")),
    )(mq, kv, kv)


def _mem_bwd(mq, kv, do):
    s = mq.shape[0]
    m = kv.shape[0]
    ts = min(MEM_TS, s)
    scale = MEM_DH ** -0.5

    def body(q_ref, k_ref, v_ref, do_ref, dq_ref, dk_ref, dv_ref):
        i = pl.program_id(1)

        @pl.when(i == 0)
        def _():
            dk_ref[...] = jnp.zeros_like(dk_ref)
            dv_ref[...] = jnp.zeros_like(dv_ref)

        qv, kb, vb, dov = q_ref[...], k_ref[...], v_ref[...], do_ref[...]
        sc = _dot(qv, kb, NT) * scale
        e = jnp.exp(sc - jnp.max(sc, axis=1, keepdims=True))
        pr = e / jnp.sum(e, axis=1, keepdims=True)
        dp = _dot(dov, vb, NT)
        ds = pr * (dp - jnp.sum(dp * pr, axis=1, keepdims=True)) * scale
        dq_ref[...] = _dot(ds, kb).astype(dq_ref.dtype)
        dk_ref[...] += _dot(ds, qv, TN)
        dv_ref[...] += _dot(pr, dov, TN)

    tile = pl.BlockSpec((ts, MEM_DH), lambda a, i: (i, a))
    kvo = pl.BlockSpec((m, MEM_DH), lambda a, i: (0, a))
    return pl.pallas_call(
        body, name="mem_bwd", grid=(MEM_H, s // ts),
        in_specs=[tile, kvo, pl.BlockSpec((m, MEM_DH), lambda a, i: (0, MEM_H + a)), tile],
        out_specs=[tile, kvo, kvo],
        out_shape=[jax.ShapeDtypeStruct((s, MEM_H * MEM_DH), CDT),
                   jax.ShapeDtypeStruct((m, MEM_H * MEM_DH), F32),
                   jax.ShapeDtypeStruct((m, MEM_H * MEM_DH), F32)],
        compiler_params=_cparams(("parallel", "arbitrary")),
    )(mq, kv, kv, do)


def _heads(t, nh, dh):
    return t.reshape(t.shape[0], nh, dh).transpose(1, 0, 2)


def _unheads(t):
    return t.transpose(1, 0, 2).reshape(t.shape[1], t.shape[0] * t.shape[2])


def _group_cols(t):
    return t.reshape(t.shape[0], SSD_G, SSD_R).transpose(1, 0, 2)


def _pad_cols(t, width):
    return jnp.pad(t, ((0, 0), (0, width - t.shape[1])))


def _full_weight(name, gathered):
    if name in COL_SHARDED:
        return gathered.transpose(1, 0, 2).reshape(gathered.shape[1], N_DEV * gathered.shape[2])
    return gathered.reshape(N_DEV * gathered.shape[1], gathered.shape[2])


def _grad_payload(name, g):
    if name in COL_SHARDED:
        return g.reshape(g.shape[0], N_DEV, g.shape[1] // N_DEV).transpose(1, 0, 2)
    return g.reshape(N_DEV, g.shape[0] // N_DEV, g.shape[1])


def _local_step(x, mem, tgt, p, wt, shards=None):
    s, d = x.shape
    wt = dict(wt)
    c1, c2, c3, c4, c5 = 3 * d, 3 * d + SSD_INNER, 3 * d + SSD_INNER + CONV_DIM, \
        3 * d + SSD_INNER + CONV_DIM + SSD_H, 3 * d + SSD_INNER + CONV_DIM + SSD_H + d
    w_t = wt["w_in"]
    w_tail = w_t[c4:]
    w_dt = jnp.pad(w_t[c3:c4], ((0, DT_PAD - SSD_H), (0, 0)))
    seg_name = ["qkv", "z", "xbc", "mq", "gl"]
    seg_dtype = [CDT, F32, F32, CDT, F32]
    seg_src = [w_t, w_t, w_t, w_tail, w_tail]
    seg_off = [0, c1, c2, 0, d]
    seg_n = [c1, c2 - c1, c3 - c2, d, 3 * d]

    u = _rows(lambda xv, g: _rms(xv, g), [x], [p["norm_mix_pre"]], [(d, CDT)], ts=512, name="f_norm_pre")[0]
    qkv, z, xbc_raw, mq, gl = [
        _mm(u, seg_src[i], tb=True, b_off=seg_off[i], n=seg_n[i], out_dtype=seg_dtype[i],
            name="f_in_" + seg_name[i]) for i in range(5)]
    dt_raw = _mm(u, w_dt, tb=True, name="f_in_dt")

    bias128 = _pad_cols(p["dt_bias"], DT_PAD)
    alog128 = _pad_cols(p["a_log"], DT_PAD)

    def dt_fn(dtr, bias, alog):
        dt = _softplus(dtr + bias)
        return dt, dt * (-jnp.exp(alog))

    dt128, da128 = _rows(dt_fn, [dt_raw], [bias128, alog128], [(DT_PAD, F32), (DT_PAD, F32)],
                         ts=512, name="f_dt")
    dtc = _group_cols(dt128[:, :SSD_H])
    dac = _group_cols(da128[:, :SSD_H])
    dar = dac.transpose(0, 2, 1)
    dsk = p["d_skip"].reshape(SSD_G, 1, SSD_R)

    conv_w, conv_b = p["conv_w"], p["conv_b"]
    pre, xbc = _conv_fwd(xbc_raw, conv_w, conv_b)
    xh = _heads(xbc[:, :SSD_INNER], SSD_H, SSD_P)
    y_h, st = _ssd_fwd_g(xh, dtc, dac, dar, dsk, xbc)
    y_core = _unheads(y_h)

    def group_norm_fwd(yv, zv, wn):
        y2 = yv * _silu(zv)
        gw = SSD_INNER // SSD_G
        outs = []
        for gi in range(SSD_G):
            seg = y2[:, gi * gw:(gi + 1) * gw]
            outs.append(_rms(seg, wn[:, gi * gw:(gi + 1) * gw]))
        return jnp.concatenate(outs, axis=1)

    y_ssd = _rows(group_norm_fwd, [y_core, z], [p["ssd_norm"]], [(SSD_INNER, CDT)], ts=256, name="f_ssd_post")[0]

    y_sb, lt_h, *late = _sb_fwd_pairs(qkv, tuple(shards) if shards is not None else ())
    for n, gth in zip(LATE_W, late):
        wt[n] = _full_weight(n, gth)

    mu = _rows(lambda mv, g: _rms(mv, g), [mem], [p["norm_mem"]], [(d, CDT)], ts=256, name="f_norm_mem")[0]
    kv = _mm(mu, wt["w_mem_kv"], out_dtype=CDT, name="f_mem_kv")
    y_mem = _mem_fwd(mq, kv)

    p_sb = _mm(y_sb, wt["w_sb_out"], name="f_sb_out")
    p_ssd = _mm(y_ssd, wt["w_ssd_out"], name="f_ssd_out")
    p_mem = _mm(y_mem, wt["w_mem_out"], name="f_mem_out")

    def merge_fn(glv, a, b, c):
        return (_sigmoid(glv[:, :d]) * a + _sigmoid(glv[:, d:2 * d]) * b + _sigmoid(glv[:, 2 * d:]) * c)

    merged = _rows(merge_fn, [gl, p_sb, p_ssd, p_mem], [], [(d, CDT)], ts=256, name="f_merge")[0]
    mix = _mm(merged, wt["w_o"], name="f_w_o")

    def mid_fn(xv, mixv, g_post, g_pre):
        h1 = xv + _rms(mixv, g_post)
        return h1, _rms(h1, g_pre)

    h1, u2 = _rows(mid_fn, [x, mix], [p["norm_mix_post"], p["norm_mlp_pre"]], [(d, F32), (d, CDT)],
                   ts=512, name="f_mid")
    a1, act = _mm(u2, wt["w_up"], name="f_up",
                  epilogue=(lambda pv: (pv, jnp.square(jnp.maximum(pv, 0.0))), [], [F32, CDT]))
    ff = _mm(act, wt["w_down"], name="f_down")

    def loss_fn(h1v, ffv, tv, g):
        diff = h1v + _rms(ffv, g) - tv
        tot = jnp.sum(jnp.sum(diff * diff, axis=1, keepdims=True), axis=0, keepdims=True)
        return diff * (1.0 / d), tot

    dh2, loss_acc = _rows(loss_fn, [h1, ff, tgt], [p["norm_mlp_post"]], [(d, F32)], [(1, 128)],
                          ts=512, name="f_loss")
    loss = loss_acc[:, :1] * (0.5 / d)

    sg = {}

    def b_post(ffv, dyv, g):
        dx, dg = _rms_bwd(ffv, g, dyv)
        return dx, dg

    d_ff, sg["norm_mlp_post"] = _rows(b_post, [ff, dh2], [p["norm_mlp_post"]], [(d, CDT)], [(1, d)],
                                      ts=512, name="b_norm_mlp_post")
    da1 = _mm(d_ff, wt["w_down"], tb=True, name="b_down_x",
              epilogue=(lambda pv, a: (pv * 2.0 * jnp.maximum(a, 0.0),), [a1], [CDT]))[0]
    gw = {"w_down": _mm(act, d_ff, ta=True, name="b_down_w")}
    du2 = _mm(da1, wt["w_up"], tb=True, name="b_up_x")
    gw["w_up"] = _mm(u2, da1, ta=True, name="b_up_w")

    def b_mid(h1v, du2v, dh2v, mixv, g_pre, g_post):
        dxa, dga = _rms_bwd(h1v, g_pre, du2v)
        dh1 = dh2v + dxa
        dmix, dgb = _rms_bwd(mixv, g_post, dh1)
        return dh1, dmix, dga, dgb

    dh1, dmix, sg["norm_mlp_pre"], sg["norm_mix_post"] = _rows(
        b_mid, [h1, du2, dh2, mix], [p["norm_mlp_pre"], p["norm_mix_post"]],
        [(d, F32), (d, CDT)], [(1, d), (1, d)], ts=256, name="b_mid")
    dmerged = _mm(dmix, wt["w_o"], tb=True, name="b_w_o_x")
    gw["w_o"] = _mm(merged, dmix, ta=True, name="b_w_o_w")

    def b_merge(dm, glv, a, b, c):
        outs, dgl = [], []
        for i, br in enumerate((a, b, c)):
            gt = _sigmoid(glv[:, i * d:(i + 1) * d])
            outs.append(gt * dm)
            dgl.append(dm * br * gt * (1.0 - gt))
        return outs[0], outs[1], outs[2], jnp.concatenate(dgl, axis=1)

    dp_sb, dp_ssd, dp_mem, dgl = _rows(b_merge, [dmerged, gl, p_sb, p_ssd, p_mem], [],
                                       [(d, CDT), (d, CDT), (d, CDT), (3 * d, CDT)], ts=256, name="b_merge")
    dy_sb = _mm(dp_sb, wt["w_sb_out"], tb=True, name="b_sb_out_x")
    gw["w_sb_out"] = _mm(y_sb, dp_sb, ta=True, name="b_sb_out_w")
    dy_ssd = _mm(dp_ssd, wt["w_ssd_out"], tb=True, name="b_ssd_out_x")
    gw["w_ssd_out"] = _mm(y_ssd, dp_ssd, ta=True, name="b_ssd_out_w")
    dy_mem = _mm(dp_mem, wt["w_mem_out"], tb=True, out_dtype=CDT, name="b_mem_out_x")
    gw["w_mem_out"] = _mm(y_mem, dp_mem, ta=True, name="b_mem_out_w")

    dmq, dk_m, dv_m = _mem_bwd(mq, kv, dy_mem)
    dkv = jnp.concatenate([dk_m, dv_m], axis=1).astype(CDT)
    gw["w_mem_kv"] = _mm(mu, dkv, ta=True, name="b_mem_kv_w")
    dmu = _mm(dkv, wt["w_mem_kv"], tb=True, name="b_mem_kv_x")
    sg["norm_mem"] = _rows(lambda mv, dv, g: _rms_bwd(mv, g, dv)[1], [mem, dmu], [p["norm_mem"]], [], [(1, d)],
                           ts=256, name="b_norm_mem")[0]

    payloads = tuple(_grad_payload(n, gw[n]) for n in LATE_W) if shards is not None else ()
    dq, dk, dv, *received = _sb_bwd_pairs(qkv, lt_h, dy_sb, payloads)
    dqkv = jnp.concatenate([dq, dk, dv], axis=1).astype(CDT)

    def group_norm_bwd(dyo, yv, zv, wn):
        sz = _silu(zv)
        y2 = yv * sz
        gw_ = SSD_INNER // SSD_G
        dy2, dwn = [], []
        for gi in range(SSD_G):
            sl = slice(gi * gw_, (gi + 1) * gw_)
            dxs, dgs = _rms_bwd(y2[:, sl], wn[:, sl], dyo[:, sl])
            dy2.append(dxs)
            dwn.append(dgs)
        dy2 = jnp.concatenate(dy2, axis=1)
        return dy2 * sz, dy2 * yv * _dsilu(zv), jnp.concatenate(dwn, axis=1)

    dy_core, dz, sg["ssd_norm"] = _rows(group_norm_bwd, [dy_ssd, y_core, z], [p["ssd_norm"]],
                                        [(SSD_INNER, F32), (SSD_INNER, CDT)], [(1, SSD_INNER)],
                                        ts=256, name="b_ssd_post")
    dxh, dda, dxx, d_b, d_c, dah, ddsk = _ssd_bwd_g(xh, dtc, dac, dar, dsk, xbc, st, _heads(dy_core, SSD_H, SSD_P))
    sg["d_skip"] = ddsk.reshape(1, SSD_H)
    sg["a_log"] = dah.reshape(1, SSD_H) * (-jnp.exp(p["a_log"]))

    def b_dt(ddav, dxxv, dtr, bias, alog):
        ddt = ddav * (-jnp.exp(alog)) + dxxv
        draw = ddt * _sigmoid(dtr + bias)
        return draw, jnp.sum(draw, axis=0, keepdims=True)

    ungroup = lambda t: _pad_cols(t.transpose(1, 0, 2).reshape(s, SSD_H), DT_PAD)
    ddt_raw, dbias128 = _rows(b_dt, [ungroup(dda), ungroup(dxx), dt_raw], [bias128, alog128],
                              [(DT_PAD, CDT)], [(1, DT_PAD)], ts=512, name="b_dt")
    sg["dt_bias"] = dbias128[:, :SSD_H]

    dxbc = jnp.concatenate([_unheads(dxh), d_b, d_c], axis=1)
    dxbc_raw, sg["conv_w"], sg["conv_b"] = _conv_bwd(xbc_raw, pre, dxbc, conv_w)

    dseg = [dqkv, dz, dxbc_raw, dmq, dgl]
    dw_bufs = [lax.empty((c3, d), F32), lax.empty((c5 - c4 + 3 * d, d), F32)]
    for i in range(5):
        bi = 0 if i < 3 else 1
        dw_bufs[bi] = _mm(dseg[i], u, ta=True, into=dw_bufs[bi], into_off=seg_off[i], into_rows=True,
                          name="b_in_w_" + seg_name[i])
    dw_dt = _mm(ddt_raw, u, ta=True, name="b_in_w_dt")
    def du(i, hosted=None):
        return _mm(dseg[i], seg_src[i], b_koff=seg_off[i], name="b_in_x_" + seg_name[i], hosted=hosted)

    if shards is None:
        dus = [du(i) for i in range(5)]
        gw["w_in"] = jnp.concatenate([dw_bufs[0], dw_dt[:SSD_H], dw_bufs[1]], axis=0).T
    else:
        du0, recv_head = du(0, _pair_swap([dw_bufs[0]]))
        du2, recv_tail, recv_dt = du(2, _pair_swap([dw_bufs[1], dw_dt]))
        tail_rows = dw_bufs[1].shape[0]
        natural = lax.empty((c4 + tail_rows, d), CDT)
        natural = _add_cast_into(dw_bufs[0], recv_head, natural, 0, c3, "sum_w_in_head")
        natural = _add_cast_into(dw_dt, recv_dt, natural, c3, SSD_H, "sum_w_in_dt")
        natural = _add_cast_into(dw_bufs[1], recv_tail, natural, c4, tail_rows, "sum_w_in_tail")
        shard = natural.shape[0] // N_DEV
        du1, win_a = du(1, _chip_scatter_windows(natural, shard, 0, d // 2))
        du4, win_b = du(4, _chip_scatter_windows(natural, shard, d // 2, d // 2))
        gw["w_in"] = [win_a, win_b]
        dus = [du0, du1, du2, du(3), du4]
    dus.append(_mm(ddt_raw, w_dt, name="b_in_x_dt"))

    def b_pre(xv, dh1v, d0, d1, d2, d3, d4, d5, g):
        dx, dg = _rms_bwd(xv, g, d0 + d1 + d2 + d3 + d4 + d5)
        return dh1v + dx, dg

    grad_x, sg["norm_mix_pre"] = _rows(b_pre, [x, dh1] + dus, [p["norm_mix_pre"]], [(d, F32)], [(1, d)],
                                       ts=256, name="b_norm_pre")
    return loss, grad_x, gw, sg, (received if shards is not None else None)


HBM = pl.BlockSpec(memory_space=pltpu.HBM)
MESH = pl.DeviceIdType.MESH


def _me_and_peers():
    x, y, c = lax.axis_index("x"), lax.axis_index("y"), lax.axis_index("c")
    me = 4 * x + 2 * y + c
    peers = [(x, y, 1 - c), (1 - x, y, c), (x, 1 - y, c), (1 - x, 1 - y, c),
             (1 - x, y, 1 - c), (x, 1 - y, 1 - c), (1 - x, 1 - y, 1 - c)]
    return me, peers


def _peer_index(peer):
    return 4 * peer[0] + 2 * peer[1] + peer[2]


def _exchange_copies(ins, outs, send_sems, recv_sems, local_sems, scatter):
    me, peers = _me_and_peers()
    copies = []
    for a in range(len(ins)):
        own = ins[a].at[me] if scatter else ins[a]
        copies.append(pltpu.make_async_copy(own, outs[a].at[me], local_sems.at[a]))
        for kk, peer in enumerate(peers):
            src = ins[a].at[_peer_index(peer)] if scatter else ins[a]
            copies.append(pltpu.make_async_remote_copy(
                src_ref=src, dst_ref=outs[a].at[me],
                send_sem=send_sems.at[a, kk], recv_sem=recv_sems.at[a, kk],
                device_id=peer, device_id_type=MESH))
    return copies


def _exchange_shapes(ins, scatter):
    return [jax.ShapeDtypeStruct(t.shape if scatter else (N_DEV,) + t.shape, t.dtype) for t in ins]


def _exchange_sems(n):
    return [pltpu.SemaphoreType.DMA((n, N_DEV - 1)), pltpu.SemaphoreType.DMA((n, N_DEV - 1)),
            pltpu.SemaphoreType.DMA((n,))]


def _gather_two_level(shards, name):
    n = len(shards)

    def body(*refs):
        ins, outs = refs[:n], refs[n:2 * n]
        send_sems, recv_sems, local_sems = refs[2 * n:]
        x, y, c = lax.axis_index("x"), lax.axis_index("y"), lax.axis_index("c")
        me, sib = (x, y, c), (x, y, 1 - c)
        chips = [(1 - x, y), (x, 1 - y), (1 - x, 1 - y)]

        def copy(a, k, block, to, src=None):
            slot = outs[a].at[_peer_index(block)]
            return pltpu.make_async_remote_copy(
                src_ref=slot if src is None else src, dst_ref=slot,
                send_sem=send_sems.at[a, k], recv_sem=recv_sems.at[a, k], device_id=to, device_id_type=MESH)

        own = [pltpu.make_async_copy(ins[a], outs[a].at[_peer_index(me)], local_sems.at[a]) for a in range(n)]
        first = []
        for a in range(n):
            first.append(copy(a, 0, me, sib, src=ins[a]))
            first += [copy(a, 1 + j, me, (*chip, c), src=ins[a]) for j, chip in enumerate(chips)]
        for cp in own + first:
            cp.start()
        passed = []
        for j, chip in enumerate(chips):
            for a in range(n):
                copy(a, 1 + j, (*chip, c), me).wait_recv()
                fwd = copy(a, 4 + j, (*chip, c), sib)
                fwd.start()
                passed.append(fwd)
        for a in range(n):
            copy(a, 0, sib, me).wait_recv()
            for j, chip in enumerate(chips):
                copy(a, 4 + j, (*chip, 1 - c), me).wait_recv()
        for cp in first + passed:
            cp.wait_send()
        for cp in own:
            cp.wait()

    return pl.pallas_call(
        body, name=name,
        in_specs=[HBM] * n, out_specs=[HBM] * n,
        out_shape=_exchange_shapes(shards, False),
        scratch_shapes=_exchange_sems(n),
        compiler_params=pltpu.CompilerParams(has_side_effects=True),
    )(*shards)


class _Hosted:
    def __init__(self, ins, shapes, sems, copies):
        self.ins, self.shapes, self.sems, self.copies = ins, shapes, sems, copies


def _pair_swap(arrays):
    n = len(arrays)

    def copies(in_refs, out_refs, sems):
        sib = (lax.axis_index("x"), lax.axis_index("y"), 1 - lax.axis_index("c"))
        return [pltpu.make_async_remote_copy(
            src_ref=in_refs[i], dst_ref=out_refs[i], send_sem=sems[0].at[i], recv_sem=sems[1].at[i],
            device_id=sib, device_id_type=MESH) for i in range(n)]

    return _Hosted(list(arrays), [jax.ShapeDtypeStruct(t.shape, t.dtype) for t in arrays],
                   [pltpu.SemaphoreType.DMA((n,)), pltpu.SemaphoreType.DMA((n,))], copies)


def _add_cast_into(a, b, into, row_off, rows, name):
    w = a.shape[1]
    tr = _pick(rows, (512, 256, 128, rows))
    nt = rows // tr
    assert rows % tr == 0 and row_off % ROW_ALIGN == 0 and tr % ROW_ALIGN == 0
    spec = pl.BlockSpec((tr, w), lambda i: (i, 0))

    def body(a_ref, b_ref, into_ref, o_ref, slots, sems):
        i = pl.program_id(0)
        slot = i % 2

        def out_copy(step, s):
            dst = o_ref.at[pl.ds(pl.multiple_of(row_off + step * tr, ROW_ALIGN), tr)]
            return pltpu.make_async_copy(slots.at[s], dst, sems.at[s])

        @pl.when(i >= 2)
        def _():
            out_copy(i - 2, slot).wait()

        slots[slot] = (a_ref[...] + b_ref[...]).astype(slots.dtype)
        out_copy(i, slot).start()

        @pl.when(i == nt - 1)
        def _():
            out_copy(i, slot).wait()
            if nt > 1:
                out_copy(i - 1, 1 - slot).wait()

    return pl.pallas_call(
        body, name=name, grid=(nt,),
        in_specs=[spec, spec, pl.BlockSpec(memory_space=pl.ANY)],
        out_specs=pl.BlockSpec(memory_space=pl.ANY),
        out_shape=jax.ShapeDtypeStruct(into.shape, into.dtype),
        scratch_shapes=[pltpu.VMEM((2, tr, w), into.dtype), pltpu.SemaphoreType.DMA((2,))],
        input_output_aliases={2: 0},
        compiler_params=_cparams(("arbitrary",)),
    )(a, b, into)


N_CHIP = 4


def _adamw_math(g, w, m, v):
    m2 = ADAM_B1 * m + (1.0 - ADAM_B1) * g
    v2 = ADAM_B2 * v + (1.0 - ADAM_B2) * jnp.square(g)
    m_hat = m2 / (1.0 - ADAM_B1 ** ADAM_STEP)
    v_hat = v2 / (1.0 - ADAM_B2 ** ADAM_STEP)
    delta = -ADAM_LR * (m_hat / (jnp.sqrt(v_hat) + ADAM_EPS) + ADAM_WD * w)
    return delta, m2, v2


def _adamw_reduce(parts, w, m, v, name):
    r, c = w.shape
    nparts = parts.shape[0]
    tr = _pick(r, (128, 64, 32, 16, 8))

    def body(p_ref, w_ref, m_ref, v_ref, g_ref, d_ref, m2_ref, v2_ref):
        g = p_ref[0].astype(F32)
        for i in range(1, nparts):
            g = g + p_ref[i].astype(F32)
        delta, m2, v2 = _adamw_math(g, w_ref[...], m_ref[...], v_ref[...])
        g_ref[...] = g
        d_ref[...] = delta
        m2_ref[...] = m2
        v2_ref[...] = v2

    tile = pl.BlockSpec((tr, c), lambda i: (i, 0))
    shp = jax.ShapeDtypeStruct((r, c), F32)
    return pl.pallas_call(
        body, name=name, grid=(r // tr,),
        in_specs=[pl.BlockSpec((nparts, tr, c), lambda i: (0, i, 0)), tile, tile, tile],
        out_specs=[tile] * 4, out_shape=[shp] * 4,
        compiler_params=_cparams(("parallel",)),
    )(parts, w, m, v)


ROW_ALIGN = 16


def _window(shard):
    lead = max((j * shard) % ROW_ALIGN for j in range(N_DEV))
    return -(-(lead + shard) // ROW_ALIGN) * ROW_ALIGN


def _chip_scatter_windows(t, shard, col0, cols):
    win = _window(shard)
    assert all((j * shard // ROW_ALIGN) * ROW_ALIGN + win <= t.shape[0] for j in range(N_DEV))

    def copies(in_refs, out_refs, sems):
        (t_ref,), (o_ref,), (send_sems, recv_sems, local_sem) = in_refs, out_refs, sems
        x, y, c = lax.axis_index("x"), lax.axis_index("y"), lax.axis_index("c")
        mine = 2 * x + y

        def window(q):
            a0 = pl.multiple_of(((2 * q + c) * shard // ROW_ALIGN) * ROW_ALIGN, ROW_ALIGN)
            return t_ref.at[pl.ds(a0, win), pl.ds(col0, cols)]

        res = [pltpu.make_async_copy(window(mine), o_ref.at[mine], local_sem)]
        for j, (px, py) in enumerate([(1 - x, y), (x, 1 - y), (1 - x, 1 - y)]):
            res.append(pltpu.make_async_remote_copy(
                src_ref=window(2 * px + py), dst_ref=o_ref.at[mine],
                send_sem=send_sems.at[j], recv_sem=recv_sems.at[j], device_id=(px, py, c), device_id_type=MESH))
        return res

    return _Hosted([t], [jax.ShapeDtypeStruct((N_CHIP, win, cols), t.dtype)],
                   [pltpu.SemaphoreType.DMA((N_CHIP - 1,)), pltpu.SemaphoreType.DMA((N_CHIP - 1,)),
                    pltpu.SemaphoreType.DMA], copies)


ADAMW_TC = 256


def _adamw_windows(parts, w, m, v, name):
    r, c = w.shape
    na = len(parts)
    nparts, win, cpart = parts[0].shape
    tc = min(ADAMW_TC, cpart)
    per = cpart // tc

    def body(*refs):
        p_refs = refs[:na]
        w_ref, m_ref, v_ref, g_ref, d_ref, m2_ref, v2_ref = refs[na:]
        me, _ = _me_and_peers()
        step = pl.program_id(0)
        gw_ = None
        for a, p_ref in enumerate(p_refs):
            tot = p_ref[0].astype(F32)
            for i in range(1, nparts):
                tot = tot + p_ref[i].astype(F32)
            gw_ = tot if gw_ is None else jnp.where(step // per == a, tot, gw_)
        for j in range(N_DEV):
            @pl.when(me == j)
            def _():
                lead = (j * r) % ROW_ALIGN
                g = (pltpu.roll(gw_, win - lead, 0) if lead else gw_)[:r]
                delta, m2, v2 = _adamw_math(g, w_ref[...], m_ref[...], v_ref[...])
                g_ref[...] = g
                d_ref[...] = delta
                m2_ref[...] = m2
                v2_ref[...] = v2

    tile = pl.BlockSpec((r, tc), lambda i: (0, i))
    shp = jax.ShapeDtypeStruct((r, c), F32)
    return pl.pallas_call(
        body, name=name, grid=(c // tc,),
        in_specs=[pl.BlockSpec((nparts, win, tc), lambda i, a=a: (0, 0, jnp.clip(i - a * per, 0, per - 1)))
                  for a in range(na)] + [tile, tile, tile],
        out_specs=[tile] * 4, out_shape=[shp] * 4,
        compiler_params=_cparams(("parallel",)),
    )(*parts, w, m, v)


SMALL_ROWS, SMALL_COLS = 16, 3072


def _small_step(sg, gcw, loss, ws, ms, vs):
    ns = len(sg)
    widths = [t.shape[1] for t in sg]
    kk_, ch = gcw.shape[1], gcw.shape[2]
    assert ns < SMALL_ROWS and max(widths) <= SMALL_COLS

    def reduce_body(*refs):
        g_refs = refs[:ns]
        gcw_ref, loss_ref, tot_ref, totc_ref = refs[ns:ns + 4]
        mine, buf, minec, bufc, send_sems, recv_sems = refs[ns + 4:]
        me, peers = _me_and_peers()

        mine[...] = jnp.zeros_like(mine)
        for i in range(ns):
            mine[i:i + 1, 0:widths[i]] = g_refs[i][...]
        mine[ns:ns + 1, 0:LANES] = jnp.broadcast_to(loss_ref[...], (1, LANES))
        minec[...] = gcw_ref[...]
        buf[me] = mine[...]
        bufc[me] = minec[...]
        copies = []
        for j, peer in enumerate(peers):
            copies.append(pltpu.make_async_remote_copy(
                src_ref=mine, dst_ref=buf.at[me], send_sem=send_sems.at[0, j], recv_sem=recv_sems.at[0, j],
                device_id=peer, device_id_type=MESH))
            copies.append(pltpu.make_async_remote_copy(
                src_ref=minec, dst_ref=bufc.at[me], send_sem=send_sems.at[1, j], recv_sem=recv_sems.at[1, j],
                device_id=peer, device_id_type=MESH))
        for cp in copies:
            cp.start()
        for cp in copies:
            cp.wait()
        tot = buf[0]
        totc = bufc[0]
        for i in range(1, N_DEV):
            tot = tot + buf[i]
            totc = totc + bufc[i]
        tot_ref[...] = tot
        totc_ref[...] = totc

    vm = pl.BlockSpec(memory_space=pltpu.VMEM)
    tot, totc = pl.pallas_call(
        reduce_body, name="small_reduce",
        in_specs=[vm] * (ns + 2), out_specs=[vm, vm],
        out_shape=[jax.ShapeDtypeStruct((SMALL_ROWS, SMALL_COLS), F32), jax.ShapeDtypeStruct((N_DEV, kk_, ch), F32)],
        scratch_shapes=[pltpu.VMEM((SMALL_ROWS, SMALL_COLS), F32), pltpu.VMEM((N_DEV, SMALL_ROWS, SMALL_COLS), F32),
                        pltpu.VMEM((N_DEV, kk_, ch), F32), pltpu.VMEM((N_DEV, N_DEV, kk_, ch), F32),
                        pltpu.SemaphoreType.DMA((2, N_DEV - 1)), pltpu.SemaphoreType.DMA((2, N_DEV - 1))],
        compiler_params=pltpu.CompilerParams(has_side_effects=True),
    )(*sg, gcw, loss)

    def adamw_body(*refs):
        tot_ref, totc_ref = refs[:2]
        w_refs, m_refs, v_refs = (refs[2 + i * (ns + 1):2 + (i + 1) * (ns + 1)] for i in range(3))
        outs = refs[3 * ns + 5:]
        loss_out = outs[0]
        go, do_, mo, vo = (outs[1 + i * (ns + 1):1 + (i + 1) * (ns + 1)] for i in range(4))
        me, _ = _me_and_peers()
        loss_out[...] = tot_ref[ns:ns + 1, 0:1]
        for i in range(ns + 1):
            g = tot_ref[i:i + 1, 0:widths[i]] if i < ns else totc_ref[me]
            delta, m2, v2 = _adamw_math(g, w_refs[i][...], m_refs[i][...], v_refs[i][...])
            go[i][...] = g
            do_[i][...] = delta
            mo[i][...] = m2
            vo[i][...] = v2

    shapes = [jax.ShapeDtypeStruct(t.shape, F32) for t in ws]
    res = pl.pallas_call(
        adamw_body, name="small_adamw",
        in_specs=[vm] * (3 * ns + 5), out_specs=[vm] * (4 * ns + 5),
        out_shape=[jax.ShapeDtypeStruct((1, 1), F32)] + shapes * 4,
    )(tot, totc, *ws, *ms, *vs)
    n1 = ns + 1
    return res[0], res[1:1 + n1], res[1 + n1:1 + 2 * n1], res[1 + 2 * n1:1 + 3 * n1], res[1 + 3 * n1:]


def _cast_shard(w, name):
    r = w.shape[0]
    return _rows(lambda t: t, [w], [], [(w.shape[1], CDT)], ts=_pick(r, (256, 128)), name=name)[0]


BIG = ["w_in", "w_mem_kv", "w_up", "w_sb_out", "w_ssd_out", "w_mem_out", "w_o", "w_down"]
LATE_W = BIG[1:]
COL_SHARDED = ("w_in", "w_mem_kv", "w_up")
SMALL = ["norm_mix_pre", "conv_b", "dt_bias", "a_log", "d_skip", "ssd_norm", "norm_mem",
         "norm_mix_post", "norm_mlp_pre", "norm_mlp_post"]
ALL_W = ["norm_mix_pre", "w_in", "conv_w", "conv_b", "dt_bias", "a_log", "d_skip", "ssd_norm", "norm_mem",
         "w_mem_kv", "w_sb_out", "w_ssd_out", "w_mem_out", "w_o", "norm_mix_post", "norm_mlp_pre", "w_up",
         "w_down", "norm_mlp_post"]
LANES = 128


def kernel(x, mem, norm_mix_pre, w_in, conv_w, conv_b, dt_bias, a_log, d_skip, ssd_norm, norm_mem, w_mem_kv, w_sb_out, w_ssd_out, w_mem_out, w_o, norm_mix_post, norm_mlp_pre, w_up, w_down, norm_mlp_post, loss_target, m_norm_mix_pre, m_w_in, m_conv_w, m_conv_b, m_dt_bias, m_a_log, m_d_skip, m_ssd_norm, m_norm_mem, m_w_mem_kv, m_w_sb_out, m_w_ssd_out, m_w_mem_out, m_w_o, m_norm_mix_post, m_norm_mlp_pre, m_w_up, m_w_down, m_norm_mlp_post, v_norm_mix_pre, v_w_in, v_conv_w, v_conv_b, v_dt_bias, v_a_log, v_d_skip, v_ssd_norm, v_norm_mem, v_w_mem_kv, v_w_sb_out, v_w_ssd_out, v_w_mem_out, v_w_o, v_norm_mix_post, v_norm_mlp_pre, v_w_up, v_w_down, v_norm_mlp_post):
    wd = dict(norm_mix_pre=norm_mix_pre, w_in=w_in, conv_w=conv_w, conv_b=conv_b, dt_bias=dt_bias, a_log=a_log,
              d_skip=d_skip, ssd_norm=ssd_norm, norm_mem=norm_mem, w_mem_kv=w_mem_kv, w_sb_out=w_sb_out,
              w_ssd_out=w_ssd_out, w_mem_out=w_mem_out, w_o=w_o, norm_mix_post=norm_mix_post,
              norm_mlp_pre=norm_mlp_pre, w_up=w_up, w_down=w_down, norm_mlp_post=norm_mlp_post)
    md = dict(norm_mix_pre=m_norm_mix_pre, w_in=m_w_in, conv_w=m_conv_w, conv_b=m_conv_b, dt_bias=m_dt_bias,
              a_log=m_a_log, d_skip=m_d_skip, ssd_norm=m_ssd_norm, norm_mem=m_norm_mem, w_mem_kv=m_w_mem_kv,
              w_sb_out=m_w_sb_out, w_ssd_out=m_w_ssd_out, w_mem_out=m_w_mem_out, w_o=m_w_o,
              norm_mix_post=m_norm_mix_post, norm_mlp_pre=m_norm_mlp_pre, w_up=m_w_up, w_down=m_w_down,
              norm_mlp_post=m_norm_mlp_post)
    vd = dict(norm_mix_pre=v_norm_mix_pre, w_in=v_w_in, conv_w=v_conv_w, conv_b=v_conv_b, dt_bias=v_dt_bias,
              a_log=v_a_log, d_skip=v_d_skip, ssd_norm=v_ssd_norm, norm_mem=v_norm_mem, w_mem_kv=v_w_mem_kv,
              w_sb_out=v_w_sb_out, w_ssd_out=v_w_ssd_out, w_mem_out=v_w_mem_out, w_o=v_w_o,
              norm_mix_post=v_norm_mix_post, norm_mlp_pre=v_norm_mlp_pre, w_up=v_w_up, w_down=v_w_down,
              norm_mlp_post=v_norm_mlp_post)
    w_in_t, m_in_t, v_in_t = (t["w_in"][0].T for t in (wd, md, vd))
    shards = {n: _cast_shard(w_in_t if n == "w_in" else wd[n][0], "cast_" + n) for n in BIG}
    w_in_g, conv_w_g = _gather_two_level([shards["w_in"], wd["conv_w"][0]], "gather_w_in")
    wt = {"w_in": w_in_g.reshape(N_DEV * w_in_g.shape[1], w_in_g.shape[2])}
    ch = conv_w_g.shape[2]

    p = {n: wd[n] for n in SMALL}
    p["conv_w"] = conv_w_g.transpose(1, 0, 2).reshape(CONV_K, N_DEV * ch)
    loss, grad_x, gw, sg, late_received = _local_step(x[0], mem[0], loss_target[0], p, wt,
                                                      [shards[n] for n in LATE_W])

    received = dict(zip(LATE_W, late_received))
    w_in_windows = gw["w_in"]

    grads, deltas, new_m, new_v = {}, {}, {}, {}
    for n in BIG:
        if n == "w_in":
            res = [t.T for t in _adamw_windows(w_in_windows, w_in_t, m_in_t, v_in_t, "adamw_" + n)]
        else:
            res = _adamw_reduce(received[n], wd[n][0], md[n][0], vd[n][0], "adamw_" + n)
        grads[n], deltas[n], new_m[n], new_v[n] = (t[None] for t in res)
    small_names = SMALL + ["conv_w"]
    gcw = sg["conv_w"].reshape(CONV_K, N_DEV, ch).transpose(1, 0, 2)
    small_of = lambda dct: [dct[n] for n in SMALL] + [dct["conv_w"][0]]
    loss_red, g_s, d_s, m_s, v_s = _small_step([sg[n] for n in SMALL], gcw, loss, small_of(wd), small_of(md),
                                               small_of(vd))
    for i, n in enumerate(small_names):
        shape = wd[n].shape
        grads[n], deltas[n], new_m[n], new_v[n] = (t.reshape(shape) for t in (g_s[i], d_s[i], m_s[i], v_s[i]))
    loss_out = loss_red.reshape(())

    return (loss_out, grad_x[None], *[grads[n] for n in ALL_W], *[deltas[n] for n in ALL_W],
            *[new_m[n] for n in ALL_W], *[new_v[n] for n in ALL_W])
```

```python
import functools

import jax
import jax.numpy as jnp
from jax import lax
from jax.experimental import pallas as pl
from jax.experimental.pallas import tpu as pltpu

F32 = jnp.float32
BF16 = jnp.bfloat16
CDT = jnp.bfloat16
EPS = 1e-6
VMEM_LIMIT = 56 * 1024 * 1024

N_DEV = 8
D_MODEL = 1024
SB_H, SB_DH = 16, 64
SSD_G, SSD_R, SSD_P, SSD_N, SSD_L = 4, 8, 64, 128, 128
SSD_H = SSD_G * SSD_R
SSD_INNER = SSD_H * SSD_P
CONV_K = 4
CONV_DIM = SSD_INNER + 2 * SSD_G * SSD_N
MEM_H, MEM_DH = 4, 256
DT_PAD = 128
SB_TQ, SB_BK = 2048, 256
CONV_PAD = 8
MM_TILE, MM_TILE_K = 1024, 2048

ADAM_LR, ADAM_B1, ADAM_B2, ADAM_EPS, ADAM_WD, ADAM_STEP = 0.001, 0.9, 0.999, 1e-08, 0.01, 10

NT = (((1,), (1,)), ((), ()))
TN = (((0,), (0,)), ((), ()))
NN = (((1,), (0,)), ((), ()))


def _cparams(sem=None):
    return pltpu.CompilerParams(dimension_semantics=sem, vmem_limit_bytes=VMEM_LIMIT)


def _pick(n, cands):
    for c in cands:
        if n % c == 0:
            return c
    return n


def _dot(a, b, dims=NN):
    return lax.dot_general(a.astype(CDT), b.astype(CDT), dims, preferred_element_type=F32)


def _split_dot(x, t, left, pieces):
    if CDT == F32:
        return lax.dot_general(t, x, NN, preferred_element_type=F32) if left else \
            lax.dot_general(x, t, NN, preferred_element_type=F32)
    acc = None
    rem = x
    for _ in range(pieces):
        hi = rem.astype(BF16)
        rem = rem - hi.astype(F32)
        d = lax.dot_general(t, hi, NN, preferred_element_type=F32) if left else \
            lax.dot_general(hi, t, NN, preferred_element_type=F32)
        acc = d if acc is None else acc + d
    return acc


def _iota(shape, dim):
    return lax.broadcasted_iota(jnp.int32, shape, dim)


def _sigmoid(x):
    return 1.0 / (1.0 + jnp.exp(-x))


def _silu(x):
    return x * _sigmoid(x)


def _dsilu(x):
    s = _sigmoid(x)
    return s * (1.0 + x * (1.0 - s))


def _softplus(x):
    return jnp.maximum(x, 0.0) + jnp.log(1.0 + jnp.exp(-jnp.abs(x)))


def _rms(x, g):
    r = lax.rsqrt(jnp.mean(x * x, axis=-1, keepdims=True) + EPS)
    return x * r * g


def _rms_bwd(x, g, dy):
    r = lax.rsqrt(jnp.mean(x * x, axis=-1, keepdims=True) + EPS)
    n = x * r
    dn = dy * g
    dx = r * (dn - n * jnp.mean(dn * n, axis=-1, keepdims=True))
    dg = jnp.sum(dy * n, axis=0, keepdims=True)
    return dx, dg


def _mm(a, b, *, ta=False, tb=False, out_dtype=F32, name, b_off=0, n=None, b_koff=0, into=None, into_off=0,
        into_rows=False, hosted=None, epilogue=None):
    m = a.shape[1] if ta else a.shape[0]
    k = a.shape[0] if ta else a.shape[1]
    if n is None:
        n = b.shape[0] if tb else b.shape[1]
    assert b_koff + k <= (b.shape[1] if tb else b.shape[0])
    bm = _pick(m, (MM_TILE, 512, 256, 128))
    bn = _pick(n, (MM_TILE, 512, 256, 128))
    bk = next(c for c in (MM_TILE_K, 1024, 512, 256, 128, k) if k % c == 0 and b_koff % c == 0)
    nk = k // bk
    assert b_off % bn == 0 and into_off % (bm if into_rows else bn) == 0
    jb, kb = b_off // bn, b_koff // bk
    io, jo = (into_off // bm, 0) if into_rows else (0, into_off // bn)
    dims = (((0 if ta else 1,), (1 if tb else 0,)), ((), ()))
    grid = (m // bm, n // bn, nk)
    off = 1 if into is not None else 0
    nh_in = len(hosted.ins) if hosted else 0
    nh_out = len(hosted.shapes) if hosted else 0
    epi_fn, extras, out_dtypes = epilogue if epilogue else (lambda p: (p,), [], [out_dtype])
    ne, no = len(extras), len(out_dtypes)
    assert not (epilogue and into is not None)

    def body(a_ref, b_ref, *rest):
        e_refs = rest[off:off + ne]
        o_refs = rest[off + ne + nh_in:off + ne + nh_in + no]
        acc_ref = rest[off + ne + nh_in + no + nh_out]

        def emit(total):
            res = epi_fn(total, *[e[...] for e in e_refs])
            for o_ref, val in zip(o_refs, res):
                o_ref[...] = val.astype(o_ref.dtype)

        if hosted:
            h_refs = (rest[off + ne:off + ne + nh_in],
                      rest[off + ne + nh_in + no:off + ne + nh_in + no + nh_out],
                      rest[off + ne + nh_in + no + nh_out + 1:])
            step = (pl.program_id(0) * grid[1] + pl.program_id(1)) * grid[2] + pl.program_id(2)

            @pl.when(step == 0)
            def _():
                for cp in hosted.copies(*h_refs):
                    cp.start()

        part = _dot(a_ref[...], b_ref[...], dims)
        if nk == 1:
            emit(part)
        else:
            kk = pl.program_id(2)

            @pl.when(kk == 0)
            def _():
                acc_ref[...] = part

            @pl.when(jnp.logical_and(kk > 0, kk < nk - 1))
            def _():
                acc_ref[...] += part

            @pl.when(kk == nk - 1)
            def _():
                emit(acc_ref[...] + part)

        if hosted:
            @pl.when(step == grid[0] * grid[1] * grid[2] - 1)
            def _():
                for cp in hosted.copies(*h_refs):
                    cp.wait()

    a_spec = pl.BlockSpec((bk, bm), lambda i, j, kk: (kk, i)) if ta else \
        pl.BlockSpec((bm, bk), lambda i, j, kk: (i, kk))
    b_spec = pl.BlockSpec((bn, bk), lambda i, j, kk: (j + jb, kk + kb)) if tb else \
        pl.BlockSpec((bk, bn), lambda i, j, kk: (kk + kb, j + jb))
    extra = {} if into is None else {"input_output_aliases": {2: 0}}
    out_shapes = [jax.ShapeDtypeStruct((m, n), dt) for dt in out_dtypes] if into is None else \
        [jax.ShapeDtypeStruct(into.shape, into.dtype)]
    block = pl.BlockSpec((bm, bn), lambda i, j, kk: (i, j))
    res = pl.pallas_call(
        body, name=name, grid=grid,
        in_specs=[a_spec, b_spec] + ([] if into is None else [pl.BlockSpec(memory_space=pl.ANY)])
        + [block] * ne + [HBM] * nh_in,
        out_specs=[pl.BlockSpec((bm, bn), lambda i, j, kk: (i + io, j + jo))] * no + [HBM] * nh_out,
        out_shape=out_shapes + (list(hosted.shapes) if hosted else []),
        scratch_shapes=[pltpu.VMEM((bm, bn) if nk > 1 else (8, 128), F32)] + (list(hosted.sems) if hosted else []),
        compiler_params=_cparams(("arbitrary",) * 3 if hosted else ("parallel", "parallel", "arbitrary")),
        **extra,
    )(*((a, b) if into is None else (a, b, into)), *extras, *(hosted.ins if hosted else ()))
    return res if (hosted or epilogue) else res[0]


def _rows(fn, tiled, params, outs, accs=(), *, ts, name):
    s = tiled[0].shape[0]
    ts = min(ts, s)
    assert s % ts == 0
    nt, npar, no, na = len(tiled), len(params), len(outs), len(accs)

    def body(*refs):
        i = pl.program_id(0)
        vals = [r[...] for r in refs[:nt + npar]]
        res = fn(*vals)
        if not isinstance(res, (tuple, list)):
            res = (res,)
        orefs = refs[nt + npar:nt + npar + no]
        arefs = refs[nt + npar + no:]
        for r_, val in zip(orefs, res[:no]):
            r_[...] = val.astype(r_.dtype)
        if na:
            @pl.when(i == 0)
            def _():
                for r_ in arefs:
                    r_[...] = jnp.zeros_like(r_)

            for r_, val in zip(arefs, res[no:]):
                r_[...] += jnp.broadcast_to(val, r_.shape)

    in_specs = [pl.BlockSpec((ts, a.shape[1]), lambda i: (i, 0)) for a in tiled]
    in_specs += [pl.BlockSpec(p.shape, lambda i: (0, 0)) for p in params]
    out_specs = [pl.BlockSpec((ts, w), lambda i: (i, 0)) for (w, _) in outs]
    out_specs += [pl.BlockSpec(shape, lambda i: (0, 0)) for shape in accs]
    out_shape = [jax.ShapeDtypeStruct((s, w), dt) for (w, dt) in outs]
    out_shape += [jax.ShapeDtypeStruct(shape, F32) for shape in accs]
    res = pl.pallas_call(
        body, name=name, grid=(s // ts,),
        in_specs=in_specs, out_specs=out_specs, out_shape=out_shape,
        compiler_params=_cparams(("arbitrary",)),
    )(*tiled, *params)
    return res


def _sb_block(qs, kb, diag):
    tq, bk = qs.shape[0], kb.shape[0]
    z = _dot(qs, kb, NT)
    lb = jnp.minimum(z, 0.0) - jnp.log(1.0 + jnp.exp(-jnp.abs(z)))
    lk = lb - z
    if diag is None:
        return None, lb, lk
    causal = (diag + _iota((tq, bk), 1)) < _iota((tq, bk), 0)
    return causal, lb, jnp.where(causal, lk, 0.0)


def _fused_exchange(scatter, ncols, nsteps):
    def hooks(ins, outs, sems):
        step = pl.program_id(0) * ncols + pl.program_id(1)

        @pl.when(step == 0)
        def _():
            for cp in _exchange_copies(ins, outs, *sems, scatter):
                cp.start()

        def finish():
            @pl.when(step == nsteps - 1)
            def _():
                for cp in _exchange_copies(ins, outs, *sems, scatter):
                    cp.wait()
        return finish
    return hooks


SB_PAIR = 128


def _sb_fwd_pairs(qkv, comm=()):
    s, d3 = qkv.shape
    d = d3 // 3
    npair = d // SB_PAIR
    tq, bk = min(SB_TQ, s), min(SB_BK, s)
    scale = SB_DH ** -0.5
    nc = len(comm)
    hooks = _fused_exchange(False, s // tq, npair * (s // tq))

    def body(q_ref, k_ref, v_ref, *rest):
        y_ref, lt_ref = rest[nc:nc + 2]
        if nc:
            finish = hooks(rest[:nc], rest[nc + 2:2 * nc + 2], rest[2 * nc + 2:])
        i = pl.program_id(1)
        q0 = i * tq
        q2 = q_ref[...] * scale
        lane_head = (_iota((1, SB_PAIR), 1) >= SB_DH).astype(jnp.int32)
        tri = (_iota((bk, bk), 0) > _iota((bk, bk), 1)).astype(CDT)

        def head(hh, y):
            mine = lane_head == hh
            qs = jnp.where(mine, q2, jnp.zeros_like(q2))

            def step(k0, carry, diag, r0=0):
                cf, acc = carry
                kb = k_ref[pl.ds(k0, bk), :]
                vb = v_ref[pl.ds(k0, bk), :]
                causal, lb, lk = _sb_block(qs[r0:], kb, diag)
                w = jnp.exp(lb + cf + _split_dot(lk, tri, False, 2))
                if causal is not None:
                    w = jnp.where(causal, w, 0.0)
                return cf + jnp.sum(lk, axis=1, keepdims=True), acc + _dot(w, vb)

            carry = (jnp.zeros((tq, 1), F32), jnp.zeros((tq, SB_PAIR), F32))
            for dd in reversed(range(tq // bk)):
                r0 = dd * bk
                sub = step(pl.multiple_of(q0 + r0, bk), tuple(t[r0:] for t in carry), 0, r0)
                carry = tuple(jnp.concatenate([t[:r0], u], axis=0) if r0 else u for t, u in zip(carry, sub))
            nfull = q0 // bk
            cf, acc = lax.fori_loop(
                0, nfull, lambda jj, c: step(pl.multiple_of((nfull - 1 - jj) * bk, bk), c, None), carry)
            lt_ref[hh] = cf
            return jnp.where(mine, acc, y)

        y_ref[...] = lax.fori_loop(0, 2, head, jnp.zeros((tq, SB_PAIR), F32)).astype(y_ref.dtype)
        if nc:
            finish()

    return pl.pallas_call(
        body, name="sb_fwd", grid=(npair, s // tq),
        in_specs=[pl.BlockSpec((tq, SB_PAIR), lambda a, i: (i, a)),
                  pl.BlockSpec((s, SB_PAIR), lambda a, i: (0, npair + a)),
                  pl.BlockSpec((s, SB_PAIR), lambda a, i: (0, 2 * npair + a))] + [HBM] * nc,
        out_specs=[pl.BlockSpec((tq, SB_PAIR), lambda a, i: (i, a)),
                   pl.BlockSpec((2, tq, 1), lambda a, i: (a, i, 0))] + [HBM] * nc,
        out_shape=[jax.ShapeDtypeStruct((s, d), CDT), jax.ShapeDtypeStruct((2 * npair, s, 1), F32)]
        + _exchange_shapes(comm, False),
        scratch_shapes=_exchange_sems(nc) if nc else [],
        compiler_params=_cparams(("arbitrary", "arbitrary")),
    )(qkv, qkv, qkv, *comm)


def _sb_bwd_pairs(qkv, ltot, dy, comm=()):
    s, d3 = qkv.shape
    d = d3 // 3
    npair = d // SB_PAIR
    tq, bk = min(SB_TQ, s), min(SB_BK, s)
    scale = SB_DH ** -0.5
    nc = len(comm)
    hooks = _fused_exchange(True, s // tq, npair * (s // tq))

    def body(q_ref, k_ref, v_ref, lt_ref, dy_ref, *rest):
        dq_ref, dk_ref, dv_ref = rest[nc:nc + 3]
        if nc:
            finish = hooks(rest[:nc], rest[nc + 3:2 * nc + 3], rest[2 * nc + 3:])
        i = pl.program_id(1)

        @pl.when(i == 0)
        def _():
            dk_ref[...] = jnp.zeros_like(dk_ref)
            dv_ref[...] = jnp.zeros_like(dv_ref)

        q0 = i * tq
        q2 = q_ref[...] * scale
        do2 = dy_ref[...].astype(CDT)
        lane_head = (_iota((1, SB_PAIR), 1) >= SB_DH).astype(jnp.int32)
        tri_le = (_iota((bk, bk), 0) <= _iota((bk, bk), 1)).astype(CDT)
        tri_lt = (_iota((bk, bk), 0) < _iota((bk, bk), 1)).astype(CDT)

        def head(hh, dq_all):
            mine = lane_head == hh
            qs = jnp.where(mine, q2, jnp.zeros_like(q2))
            dov = jnp.where(mine, do2, jnp.zeros_like(do2))
            ltot_h = lt_ref[hh]

            def step(k0, carry, diag, r0=0):
                cf, cg, dq = carry
                kb = k_ref[pl.ds(k0, bk), :]
                vb = v_ref[pl.ds(k0, bk), :]
                causal, lb, lk = _sb_block(qs[r0:], kb, diag)
                w = jnp.exp(lb + ltot_h[r0:] - (cf + _split_dot(lk, tri_le, False, 2)))
                if causal is not None:
                    w = jnp.where(causal, w, 0.0)
                g = w * _dot(dov[r0:], vb, NT)
                gsum = cg + _split_dot(g, tri_lt, False, 2)
                dz = g - (g + gsum) * jnp.exp(lb)
                if causal is not None:
                    dz = jnp.where(causal, dz, 0.0)
                dzc = dz.astype(CDT)
                dk_ref[pl.ds(k0, bk), :] += _dot(dzc, qs[r0:], TN)
                dv_ref[pl.ds(k0, bk), :] += _dot(w, dov[r0:], TN)
                kbm = jnp.where(mine, kb, jnp.zeros_like(kb))
                return (cf + jnp.sum(lk, axis=1, keepdims=True), cg + jnp.sum(g, axis=1, keepdims=True),
                        dq + _dot(dzc, kbm))

            carry = (jnp.zeros((tq, 1), F32), jnp.zeros((tq, 1), F32), jnp.zeros((tq, SB_PAIR), F32))
            carry = lax.fori_loop(0, q0 // bk, lambda jj, c: step(pl.multiple_of(jj * bk, bk), c, None), carry)
            for dd in range(tq // bk):
                r0 = dd * bk
                sub = step(pl.multiple_of(q0 + r0, bk), tuple(t[r0:] for t in carry), 0, r0)
                carry = tuple(jnp.concatenate([t[:r0], u], axis=0) if r0 else u for t, u in zip(carry, sub))
            return dq_all + carry[2]

        dq_ref[...] = lax.fori_loop(0, 2, head, jnp.zeros((tq, SB_PAIR), F32)) * scale
        if nc:
            finish()

    tile = pl.BlockSpec((tq, SB_PAIR), lambda a, i: (i, a))
    acc = pl.BlockSpec((s, SB_PAIR), lambda a, i: (0, a))
    shp = jax.ShapeDtypeStruct((s, d), F32)
    return pl.pallas_call(
        body, name="sb_bwd", grid=(npair, s // tq),
        in_specs=[tile, pl.BlockSpec((s, SB_PAIR), lambda a, i: (0, npair + a)),
                  pl.BlockSpec((s, SB_PAIR), lambda a, i: (0, 2 * npair + a)),
                  pl.BlockSpec((2, tq, 1), lambda a, i: (a, i, 0)), tile] + [HBM] * nc,
        out_specs=[tile, acc, acc] + [HBM] * nc,
        out_shape=[shp, shp, shp] + _exchange_shapes(comm, True),
        scratch_shapes=_exchange_sems(nc) if nc else [],
        compiler_params=_cparams(("arbitrary", "arbitrary")),
    )(qkv, qkv, qkv, ltot, dy, *comm)


def _pick_lane(tile, r):
    return jnp.sum(jnp.where(_iota(tile.shape, 1) == r, tile, 0.0), axis=1, keepdims=True)


def _pick_row(tile, r):
    return jnp.sum(jnp.where(_iota(tile.shape, 0) == r, tile, 0.0), axis=0, keepdims=True)


SSD_CPS = 4


def _ssd_chunk_common(cv, bv, dac, dar):
    l = SSD_L
    tdt = F32 if CDT == F32 else BF16
    lower = (_iota((l, l), 1) <= _iota((l, l), 0)).astype(tdt)
    upper = (_iota((l, l), 0) <= _iota((l, l), 1)).astype(tdt)
    return _dot(cv, bv, NT), _split_dot(dac, lower, True, 3), _split_dot(dar, upper, False, 3)


def _ssd_fwd_g(xh, dtc, dac, dar, dsk, xbc):
    hh, s, p = xh.shape
    l, n, g_, r_ = SSD_L, SSD_N, SSD_G, SSD_R
    nc = s // l
    cps = SSD_CPS if nc % SSD_CPS == 0 else 1
    lb = cps * l
    boff = SSD_INNER // n
    coff = boff + g_

    def body(x_ref, dtc_ref, dac_ref, dar_ref, dsk_ref, b_ref, c_ref, y_ref, st_ref, state_ref):
        c = pl.program_id(1)

        @pl.when(c == 0)
        def _():
            state_ref[...] = jnp.zeros_like(state_ref)

        mask = _iota((l, l), 1) <= _iota((l, l), 0)
        dskv = dsk_ref[...]
        for cc in range(cps):
            rows = pl.ds(cc * l, l)
            cv, bv, dtcv = c_ref[rows, :], b_ref[rows, :], dtc_ref[rows, :]
            cb, acs_c, acs_r = _ssd_chunk_common(cv, bv, dac_ref[rows, :], dar_ref[:, rows])
            for r in range(r_):
                a_col = _pick_lane(acs_c, r)
                a_row = _pick_row(acs_r, r)
                dsk_h = _pick_lane(dskv, r)
                a_b = jnp.broadcast_to(a_col, (l, l))
                dt_b = jnp.broadcast_to(_pick_lane(dtcv, r), (l, p))
                xv = x_ref[r, rows, :]
                xd = xv * dt_b
                decay = jnp.where(mask, jnp.exp(jnp.minimum(a_b - a_row, 0.0)), 0.0)
                hprev = state_ref[r]
                y = _dot(cb * decay, xd) + jnp.exp(a_b)[:, :p] * _dot(cv, hprev)
                y_ref[r, rows, :] = y + dsk_h * xv
                a_end = a_col[l - 1:l, :]
                st_ref[r, cc] = hprev
                state_ref[r] = hprev * jnp.exp(a_end) + _dot(bv, xd * jnp.exp(a_end - a_b)[:, :p], TN)

    return pl.pallas_call(
        body, name="ssd_fwd", grid=(g_, nc // cps),
        in_specs=[pl.BlockSpec((r_, lb, p), lambda g, c: (g, c, 0)),
                  pl.BlockSpec((None, lb, r_), lambda g, c: (g, c, 0)),
                  pl.BlockSpec((None, lb, r_), lambda g, c: (g, c, 0)),
                  pl.BlockSpec((None, r_, lb), lambda g, c: (g, 0, c)),
                  pl.BlockSpec((None, 1, r_), lambda g, c: (g, 0, 0)),
                  pl.BlockSpec((lb, n), lambda g, c: (c, (boff + g))),
                  pl.BlockSpec((lb, n), lambda g, c: (c, (coff + g)))],
        out_specs=[pl.BlockSpec((r_, lb, p), lambda g, c: (g, c, 0)),
                   pl.BlockSpec((r_, cps, n, p), lambda g, c: (g, c, 0, 0))],
        out_shape=[jax.ShapeDtypeStruct((hh, s, p), F32),
                   jax.ShapeDtypeStruct((hh, nc, n, p), F32)],
        scratch_shapes=[pltpu.VMEM((r_, n, p), F32)],
        compiler_params=_cparams(("parallel", "arbitrary")),
    )(xh, dtc, dac, dar, dsk, xbc, xbc)


def _ssd_bwd_g(xh, dtc, dac, dar, dsk, xbc, st, dy):
    hh, s, p = xh.shape
    l, n, g_, r_ = SSD_L, SSD_N, SSD_G, SSD_R
    nc = s // l
    cps = SSD_CPS if nc % SSD_CPS == 0 else 1
    lb, ncb = cps * l, nc // cps
    boff = SSD_INNER // n
    coff = boff + g_

    def body(x_ref, dtc_ref, dac_ref, dar_ref, dsk_ref, b_ref, c_ref, st_ref, dy_ref,
             dx_ref, dda_ref, dxx_ref, db_ref, dc_ref, dah_ref, ddsk_ref, dstate_ref):
        c = pl.program_id(1)

        @pl.when(c == 0)
        def _():
            dstate_ref[...] = jnp.zeros_like(dstate_ref)
            dah_ref[...] = jnp.zeros_like(dah_ref)
            ddsk_ref[...] = jnp.zeros_like(ddsk_ref)

        il = _iota((l, l), 0)
        isx = _iota((l, l), 1)
        tdt = F32 if CDT == F32 else BF16
        t1 = (isx >= il).astype(tdt)
        lane = _iota((l, r_), 1)
        lane1 = _iota((1, r_), 1)
        dskv = dsk_ref[...]
        for cc in reversed(range(cps)):
            rows = pl.ds(cc * l, l)
            cv, bv, dtcv = c_ref[rows, :], b_ref[rows, :], dtc_ref[rows, :]
            cb, acs_c, acs_r = _ssd_chunk_common(cv, bv, dac_ref[rows, :], dar_ref[:, rows])
            dda_all = jnp.zeros((l, r_), F32)
            dxx_all = jnp.zeros((l, r_), F32)
            dah_all = jnp.zeros((1, r_), F32)
            ddsk_all = jnp.zeros((1, r_), F32)
            db_acc = jnp.zeros((l, n), F32)
            dc_acc = jnp.zeros((l, n), F32)
            md_sum = jnp.zeros((l, l), F32)
            cb_t = _dot(bv, cv, NT)
            cv_t = cv.T
            for r in range(r_):
                a_col = _pick_lane(acs_c, r)
                a_row = _pick_row(acs_r, r)
                dt_col = _pick_lane(dtcv, r)
                dsk_h = _pick_lane(dskv, r)
                xv = x_ref[r, rows, :]
                dyv = dy_ref[r, rows, :]
                a_b = jnp.broadcast_to(a_col, (l, l))
                dt_b = jnp.broadcast_to(dt_col, (l, p))
                xd = xv * dt_b
                decay = jnp.where(isx <= il, jnp.exp(jnp.minimum(a_b - a_row, 0.0)), 0.0)
                decay_t = jnp.where(isx >= il, jnp.exp(jnp.minimum(a_row - a_b, 0.0)), 0.0)
                dhn = dstate_ref[r]
                hc = st_ref[r, cc]
                a_end = a_col[l - 1:l, :]
                ea_b = jnp.exp(a_b)
                dte_b = jnp.exp(a_end - a_b)

                dx_state = dte_b[:, :p] * _dot(bv, dhn)
                dxd = _dot(cb_t * decay_t, dyv) + dx_state
                md = decay * _dot(dyv, xd, NT)
                md_sum = md_sum + md
                dc_acc = dc_acc + ea_b * _dot(dyv, hc, NT)
                db_acc = db_acc + dte_b * _dot(xd, dhn, NT)
                dstate_ref[r] = jnp.exp(a_end) * dhn + _dot(cv_t, dyv * ea_b[:, :p])

                yoff = ea_b[:, :p] * _dot(cv, hc)
                xdx = xd * dx_state
                vec = jnp.sum(dyv * yoff - xdx, axis=1, keepdims=True)
                total = lambda t: jnp.sum(jnp.sum(t, axis=0, keepdims=True), axis=1, keepdims=True)
                end_term = total(xdx) + jnp.exp(a_end) * total(hc * dhn)
                zmat = _split_dot(md * cb, t1, True, 2)
                span = jnp.sum(jnp.where(isx < il, zmat, 0.0), axis=1, keepdims=True)
                rc = _split_dot(jnp.broadcast_to(vec, (l, 128)), t1, True, 2)[:, :1]
                dda = span + rc + end_term
                dda_all = jnp.where(lane == r, dda, dda_all)
                dxx_all = jnp.where(lane == r, jnp.sum(dxd * xv, axis=1, keepdims=True), dxx_all)
                dx_ref[r, rows, :] = dxd * dt_b + dsk_h * dyv
                dah_all = jnp.where(lane1 == r, jnp.sum(dda * dt_col, axis=0, keepdims=True), dah_all)
                ddsk_all = jnp.where(lane1 == r, total(dyv * xv), ddsk_all)
            dda_ref[rows, :] = dda_all
            dxx_ref[rows, :] = dxx_all
            db_ref[rows, :] = db_acc + _dot(md_sum, cv, TN)
            dc_ref[rows, :] = dc_acc + _dot(md_sum, bv)
            dah_ref[...] += dah_all
            ddsk_ref[...] += ddsk_all

    rev = lambda c: ncb - 1 - c
    xspec = pl.BlockSpec((r_, lb, p), lambda g, c: (g, rev(c), 0))
    cspec = pl.BlockSpec((None, lb, r_), lambda g, c: (g, rev(c), 0))
    hspec = pl.BlockSpec((None, 1, r_), lambda g, c: (g, 0, 0))
    return pl.pallas_call(
        body, name="ssd_bwd", grid=(g_, ncb),
        in_specs=[xspec, cspec, cspec,
                  pl.BlockSpec((None, r_, lb), lambda g, c: (g, 0, rev(c))),
                  hspec,
                  pl.BlockSpec((lb, n), lambda g, c: (rev(c), boff + g)),
                  pl.BlockSpec((lb, n), lambda g, c: (rev(c), coff + g)),
                  pl.BlockSpec((r_, cps, n, p), lambda g, c: (g, rev(c), 0, 0)),
                  xspec],
        out_specs=[xspec, cspec, cspec,
                   pl.BlockSpec((lb, n), lambda g, c: (rev(c), g)),
                   pl.BlockSpec((lb, n), lambda g, c: (rev(c), g)),
                   hspec, hspec],
        out_shape=[jax.ShapeDtypeStruct((hh, s, p), F32),
                   jax.ShapeDtypeStruct((g_, s, r_), F32),
                   jax.ShapeDtypeStruct((g_, s, r_), F32),
                   jax.ShapeDtypeStruct((s, g_ * n), F32),
                   jax.ShapeDtypeStruct((s, g_ * n), F32),
                   jax.ShapeDtypeStruct((g_, 1, r_), F32),
                   jax.ShapeDtypeStruct((g_, 1, r_), F32)],
        scratch_shapes=[pltpu.VMEM((r_, n, p), F32)],
        compiler_params=_cparams(("parallel", "arbitrary")),
    )(xh, dtc, dac, dar, dsk, xbc, xbc, st, dy)


CONV_TC = 256
CONV_RC = 128


def _conv_taps(x_ref, head_ref, rc):
    base = CONV_PAD - (CONV_K - 1)
    head_ref[pl.ds(0, CONV_PAD), :] = jnp.zeros((CONV_PAD, head_ref.shape[1]), F32)
    head_ref[pl.ds(CONV_PAD, rc), :] = x_ref[pl.ds(0, rc), :]

    def tap(t0, kk):
        if t0 == 0:
            return head_ref[pl.ds(base + kk, rc), :]
        return x_ref[pl.ds(t0 - (CONV_K - 1) + kk, rc), :]
    return tap


def _conv_fwd(x, w, b):
    s, ch = x.shape
    rc = min(CONV_RC, s)

    def body(x_ref, w_ref, b_ref, pre_ref, act_ref, head_ref):
        wv = w_ref[...]
        tap = _conv_taps(x_ref, head_ref, rc)
        for t0 in range(0, s, rc):
            acc = jnp.broadcast_to(b_ref[...], (rc, CONV_TC))
            for kk in range(CONV_K):
                acc = acc + wv[kk:kk + 1, :] * tap(t0, kk)
            pre_ref[pl.ds(t0, rc), :] = acc
            act_ref[pl.ds(t0, rc), :] = _silu(acc)

    col = pl.BlockSpec((s, CONV_TC), lambda j: (0, j))
    shp = jax.ShapeDtypeStruct((s, ch), F32)
    return pl.pallas_call(
        body, name="conv_fwd", grid=(ch // CONV_TC,),
        in_specs=[col, pl.BlockSpec((CONV_K, CONV_TC), lambda j: (0, j)),
                  pl.BlockSpec((1, CONV_TC), lambda j: (0, j))],
        out_specs=[col, col],
        out_shape=[shp, shp],
        scratch_shapes=[pltpu.VMEM((CONV_PAD + rc, CONV_TC), F32)],
        compiler_params=_cparams(("parallel",)),
    )(x, w, b)


def _conv_bwd(x, pre, dact, w):
    s, ch = x.shape
    rc = min(CONV_RC, s)

    def body(x_ref, pre_ref, da_ref, w_ref, dx_ref, dw_ref, db_ref, dpre_ref, head_ref):
        wv = w_ref[...]
        tap = _conv_taps(x_ref, head_ref, rc)
        for t0 in range(0, s, rc):
            dpre_ref[pl.ds(t0, rc), :] = da_ref[pl.ds(t0, rc), :] * _dsilu(pre_ref[pl.ds(t0, rc), :])
        dpre_ref[pl.ds(s, CONV_PAD), :] = jnp.zeros((CONV_PAD, CONV_TC), F32)
        dws = [jnp.zeros((1, CONV_TC), F32) for _ in range(CONV_K)]
        dbs = jnp.zeros((1, CONV_TC), F32)
        for t0 in range(0, s, rc):
            acc = jnp.zeros((rc, CONV_TC), F32)
            dp = dpre_ref[pl.ds(t0, rc), :]
            for kk in range(CONV_K):
                acc = acc + wv[kk:kk + 1, :] * dpre_ref[pl.ds(t0 + CONV_K - 1 - kk, rc), :]
                dws[kk] = dws[kk] + jnp.sum(dp * tap(t0, kk), axis=0, keepdims=True)
            dbs = dbs + jnp.sum(dp, axis=0, keepdims=True)
            dx_ref[pl.ds(t0, rc), :] = acc.astype(dx_ref.dtype)
        for kk in range(CONV_K):
            dw_ref[kk:kk + 1, :] = dws[kk]
        db_ref[...] = dbs

    col = pl.BlockSpec((s, CONV_TC), lambda j: (0, j))
    return pl.pallas_call(
        body, name="conv_bwd", grid=(ch // CONV_TC,),
        in_specs=[col, col, col, pl.BlockSpec((CONV_K, CONV_TC), lambda j: (0, j))],
        out_specs=[col, pl.BlockSpec((CONV_K, CONV_TC), lambda j: (0, j)),
                   pl.BlockSpec((1, CONV_TC), lambda j: (0, j))],
        out_shape=[jax.ShapeDtypeStruct((s, ch), CDT),
                   jax.ShapeDtypeStruct((CONV_K, ch), F32),
                   jax.ShapeDtypeStruct((1, ch), F32)],
        scratch_shapes=[pltpu.VMEM((s + CONV_PAD, CONV_TC), F32), pltpu.VMEM((CONV_PAD + rc, CONV_TC), F32)],
        compiler_params=_cparams(("parallel",)),
    )(x, pre, dact, w)


MEM_TS = 512


def _mem_fwd(mq, kv):
    s = mq.shape[0]
    m = kv.shape[0]
    ts = min(MEM_TS, s)
    scale = MEM_DH ** -0.5

    def body(q_ref, k_ref, v_ref, o_ref):
        sc = _dot(q_ref[...], k_ref[...], NT) * scale
        e = jnp.exp(sc - jnp.max(sc, axis=1, keepdims=True))
        pr = e / jnp.sum(e, axis=1, keepdims=True)
        o_ref[...] = _dot(pr, v_ref[...]).astype(o_ref.dtype)

    return pl.pallas_call(
        body, name="mem_fwd", grid=(MEM_H, s // ts),
        in_specs=[pl.BlockSpec((ts, MEM_DH), lambda a, i: (i, a)),
                  pl.BlockSpec((m, MEM_DH), lambda a, i: (0, a)),
                  pl.BlockSpec((m, MEM_DH), lambda a, i: (0, MEM_H + a))],
        out_specs=pl.BlockSpec((ts, MEM_DH), lambda a, i: (i, a)),
        out_shape=jax.ShapeDtypeStruct((s, MEM_H * MEM_DH), CDT),
        compiler_params=_cparams(("parallel", "arbitrary")),
    )(mq, kv, kv)


def _mem_bwd(mq, kv, do):
    s = mq.shape[0]
    m = kv.shape[0]
    ts = min(MEM_TS, s)
    scale = MEM_DH ** -0.5

    def body(q_ref, k_ref, v_ref, do_ref, dq_ref, dk_ref, dv_ref):
        i = pl.program_id(1)

        @pl.when(i == 0)
        def _():
            dk_ref[...] = jnp.zeros_like(dk_ref)
            dv_ref[...] = jnp.zeros_like(dv_ref)

        qv, kb, vb, dov = q_ref[...], k_ref[...], v_ref[...], do_ref[...]
        sc = _dot(qv, kb, NT) * scale
        e = jnp.exp(sc - jnp.max(sc, axis=1, keepdims=True))
        pr = e / jnp.sum(e, axis=1, keepdims=True)
        dp = _dot(dov, vb, NT)
        ds = pr * (dp - jnp.sum(dp * pr, axis=1, keepdims=True)) * scale
        dq_ref[...] = _dot(ds, kb).astype(dq_ref.dtype)
        dk_ref[...] += _dot(ds, qv, TN)
        dv_ref[...] += _dot(pr, dov, TN)

    tile = pl.BlockSpec((ts, MEM_DH), lambda a, i: (i, a))
    kvo = pl.BlockSpec((m, MEM_DH), lambda a, i: (0, a))
    return pl.pallas_call(
        body, name="mem_bwd", grid=(MEM_H, s // ts),
        in_specs=[tile, kvo, pl.BlockSpec((m, MEM_DH), lambda a, i: (0, MEM_H + a)), tile],
        out_specs=[tile, kvo, kvo],
        out_shape=[jax.ShapeDtypeStruct((s, MEM_H * MEM_DH), CDT),
                   jax.ShapeDtypeStruct((m, MEM_H * MEM_DH), F32),
                   jax.ShapeDtypeStruct((m, MEM_H * MEM_DH), F32)],
        compiler_params=_cparams(("parallel", "arbitrary")),
    )(mq, kv, kv, do)


def _heads(t, nh, dh):
    return t.reshape(t.shape[0], nh, dh).transpose(1, 0, 2)


def _unheads(t):
    return t.transpose(1, 0, 2).reshape(t.shape[1], t.shape[0] * t.shape[2])


def _group_cols(t):
    return t.reshape(t.shape[0], SSD_G, SSD_R).transpose(1, 0, 2)


def _pad_cols(t, width):
    return jnp.pad(t, ((0, 0), (0, width - t.shape[1])))


def _full_weight(name, gathered):
    if name in COL_SHARDED:
        return gathered.transpose(1, 0, 2).reshape(gathered.shape[1], N_DEV * gathered.shape[2])
    return gathered.reshape(N_DEV * gathered.shape[1], gathered.shape[2])


def _grad_payload(name, g):
    if name in COL_SHARDED:
        return g.reshape(g.shape[0], N_DEV, g.shape[1] // N_DEV).transpose(1, 0, 2)
    return g.reshape(N_DEV, g.shape[0] // N_DEV, g.shape[1])


def _local_step(x, mem, tgt, p, wt, shards=None):
    s, d = x.shape
    wt = dict(wt)
    c1, c2, c3, c4, c5 = 3 * d, 3 * d + SSD_INNER, 3 * d + SSD_INNER + CONV_DIM, \
        3 * d + SSD_INNER + CONV_DIM + SSD_H, 3 * d + SSD_INNER + CONV_DIM + SSD_H + d
    w_t = wt["w_in"]
    w_tail = w_t[c4:]
    w_dt = jnp.pad(w_t[c3:c4], ((0, DT_PAD - SSD_H), (0, 0)))
    seg_name = ["qkv", "z", "xbc", "mq", "gl"]
    seg_dtype = [CDT, F32, F32, CDT, F32]
    seg_src = [w_t, w_t, w_t, w_tail, w_tail]
    seg_off = [0, c1, c2, 0, d]
    seg_n = [c1, c2 - c1, c3 - c2, d, 3 * d]

    u = _rows(lambda xv, g: _rms(xv, g), [x], [p["norm_mix_pre"]], [(d, CDT)], ts=512, name="f_norm_pre")[0]
    qkv, z, xbc_raw, mq, gl = [
        _mm(u, seg_src[i], tb=True, b_off=seg_off[i], n=seg_n[i], out_dtype=seg_dtype[i],
            name="f_in_" + seg_name[i]) for i in range(5)]
    dt_raw = _mm(u, w_dt, tb=True, name="f_in_dt")

    bias128 = _pad_cols(p["dt_bias"], DT_PAD)
    alog128 = _pad_cols(p["a_log"], DT_PAD)

    def dt_fn(dtr, bias, alog):
        dt = _softplus(dtr + bias)
        return dt, dt * (-jnp.exp(alog))

    dt128, da128 = _rows(dt_fn, [dt_raw], [bias128, alog128], [(DT_PAD, F32), (DT_PAD, F32)],
                         ts=512, name="f_dt")
    dtc = _group_cols(dt128[:, :SSD_H])
    dac = _group_cols(da128[:, :SSD_H])
    dar = dac.transpose(0, 2, 1)
    dsk = p["d_skip"].reshape(SSD_G, 1, SSD_R)

    conv_w, conv_b = p["conv_w"], p["conv_b"]
    pre, xbc = _conv_fwd(xbc_raw, conv_w, conv_b)
    xh = _heads(xbc[:, :SSD_INNER], SSD_H, SSD_P)
    y_h, st = _ssd_fwd_g(xh, dtc, dac, dar, dsk, xbc)
    y_core = _unheads(y_h)

    def group_norm_fwd(yv, zv, wn):
        y2 = yv * _silu(zv)
        gw = SSD_INNER // SSD_G
        outs = []
        for gi in range(SSD_G):
            seg = y2[:, gi * gw:(gi + 1) * gw]
            outs.append(_rms(seg, wn[:, gi * gw:(gi + 1) * gw]))
        return jnp.concatenate(outs, axis=1)

    y_ssd = _rows(group_norm_fwd, [y_core, z], [p["ssd_norm"]], [(SSD_INNER, CDT)], ts=256, name="f_ssd_post")[0]

    y_sb, lt_h, *late = _sb_fwd_pairs(qkv, tuple(shards) if shards is not None else ())
    for n, gth in zip(LATE_W, late):
        wt[n] = _full_weight(n, gth)

    mu = _rows(lambda mv, g: _rms(mv, g), [mem], [p["norm_mem"]], [(d, CDT)], ts=256, name="f_norm_mem")[0]
    kv = _mm(mu, wt["w_mem_kv"], out_dtype=CDT, name="f_mem_kv")
    y_mem = _mem_fwd(mq, kv)

    p_sb = _mm(y_sb, wt["w_sb_out"], name="f_sb_out")
    p_ssd = _mm(y_ssd, wt["w_ssd_out"], name="f_ssd_out")
    p_mem = _mm(y_mem, wt["w_mem_out"], name="f_mem_out")

    def merge_fn(glv, a, b, c):
        return (_sigmoid(glv[:, :d]) * a + _sigmoid(glv[:, d:2 * d]) * b + _sigmoid(glv[:, 2 * d:]) * c)

    merged = _rows(merge_fn, [gl, p_sb, p_ssd, p_mem], [], [(d, CDT)], ts=256, name="f_merge")[0]
    mix = _mm(merged, wt["w_o"], name="f_w_o")

    def mid_fn(xv, mixv, g_post, g_pre):
        h1 = xv + _rms(mixv, g_post)
        return h1, _rms(h1, g_pre)

    h1, u2 = _rows(mid_fn, [x, mix], [p["norm_mix_post"], p["norm_mlp_pre"]], [(d, F32), (d, CDT)],
                   ts=512, name="f_mid")
    a1, act = _mm(u2, wt["w_up"], name="f_up",
                  epilogue=(lambda pv: (pv, jnp.square(jnp.maximum(pv, 0.0))), [], [F32, CDT]))
    ff = _mm(act, wt["w_down"], name="f_down")

    def loss_fn(h1v, ffv, tv, g):
        diff = h1v + _rms(ffv, g) - tv
        tot = jnp.sum(jnp.sum(diff * diff, axis=1, keepdims=True), axis=0, keepdims=True)
        return diff * (1.0 / d), tot

    dh2, loss_acc = _rows(loss_fn, [h1, ff, tgt], [p["norm_mlp_post"]], [(d, F32)], [(1, 128)],
                          ts=512, name="f_loss")
    loss = loss_acc[:, :1] * (0.5 / d)

    sg = {}

    def b_post(ffv, dyv, g):
        dx, dg = _rms_bwd(ffv, g, dyv)
        return dx, dg

    d_ff, sg["norm_mlp_post"] = _rows(b_post, [ff, dh2], [p["norm_mlp_post"]], [(d, CDT)], [(1, d)],
                                      ts=512, name="b_norm_mlp_post")
    da1 = _mm(d_ff, wt["w_down"], tb=True, name="b_down_x",
              epilogue=(lambda pv, a: (pv * 2.0 * jnp.maximum(a, 0.0),), [a1], [CDT]))[0]
    gw = {"w_down": _mm(act, d_ff, ta=True, name="b_down_w")}
    du2 = _mm(da1, wt["w_up"], tb=True, name="b_up_x")
    gw["w_up"] = _mm(u2, da1, ta=True, name="b_up_w")

    def b_mid(h1v, du2v, dh2v, mixv, g_pre, g_post):
        dxa, dga = _rms_bwd(h1v, g_pre, du2v)
        dh1 = dh2v + dxa
        dmix, dgb = _rms_bwd(mixv, g_post, dh1)
        return dh1, dmix, dga, dgb

    dh1, dmix, sg["norm_mlp_pre"], sg["norm_mix_post"] = _rows(
        b_mid, [h1, du2, dh2, mix], [p["norm_mlp_pre"], p["norm_mix_post"]],
        [(d, F32), (d, CDT)], [(1, d), (1, d)], ts=256, name="b_mid")
    dmerged = _mm(dmix, wt["w_o"], tb=True, name="b_w_o_x")
    gw["w_o"] = _mm(merged, dmix, ta=True, name="b_w_o_w")

    def b_merge(dm, glv, a, b, c):
        outs, dgl = [], []
        for i, br in enumerate((a, b, c)):
            gt = _sigmoid(glv[:, i * d:(i + 1) * d])
            outs.append(gt * dm)
            dgl.append(dm * br * gt * (1.0 - gt))
        return outs[0], outs[1], outs[2], jnp.concatenate(dgl, axis=1)

    dp_sb, dp_ssd, dp_mem, dgl = _rows(b_merge, [dmerged, gl, p_sb, p_ssd, p_mem], [],
                                       [(d, CDT), (d, CDT), (d, CDT), (3 * d, CDT)], ts=256, name="b_merge")
    dy_sb = _mm(dp_sb, wt["w_sb_out"], tb=True, name="b_sb_out_x")
    gw["w_sb_out"] = _mm(y_sb, dp_sb, ta=True, name="b_sb_out_w")
    dy_ssd = _mm(dp_ssd, wt["w_ssd_out"], tb=True, name="b_ssd_out_x")
    gw["w_ssd_out"] = _mm(y_ssd, dp_ssd, ta=True, name="b_ssd_out_w")
    dy_mem = _mm(dp_mem, wt["w_mem_out"], tb=True, out_dtype=CDT, name="b_mem_out_x")
    gw["w_mem_out"] = _mm(y_mem, dp_mem, ta=True, name="b_mem_out_w")

    dmq, dk_m, dv_m = _mem_bwd(mq, kv, dy_mem)
    dkv = jnp.concatenate([dk_m, dv_m], axis=1).astype(CDT)
    gw["w_mem_kv"] = _mm(mu, dkv, ta=True, name="b_mem_kv_w")
    dmu = _mm(dkv, wt["w_mem_kv"], tb=True, name="b_mem_kv_x")
    sg["norm_mem"] = _rows(lambda mv, dv, g: _rms_bwd(mv, g, dv)[1], [mem, dmu], [p["norm_mem"]], [], [(1, d)],
                           ts=256, name="b_norm_mem")[0]

    payloads = tuple(_grad_payload(n, gw[n]) for n in LATE_W) if shards is not None else ()
    dq, dk, dv, *received = _sb_bwd_pairs(qkv, lt_h, dy_sb, payloads)
    dqkv = jnp.concatenate([dq, dk, dv], axis=1).astype(CDT)

    def group_norm_bwd(dyo, yv, zv, wn):
        sz = _silu(zv)
        y2 = yv * sz
        gw_ = SSD_INNER // SSD_G
        dy2, dwn = [], []
        for gi in range(SSD_G):
            sl = slice(gi * gw_, (gi + 1) * gw_)
            dxs, dgs = _rms_bwd(y2[:, sl], wn[:, sl], dyo[:, sl])
            dy2.append(dxs)
            dwn.append(dgs)
        dy2 = jnp.concatenate(dy2, axis=1)
        return dy2 * sz, dy2 * yv * _dsilu(zv), jnp.concatenate(dwn, axis=1)

    dy_core, dz, sg["ssd_norm"] = _rows(group_norm_bwd, [dy_ssd, y_core, z], [p["ssd_norm"]],
                                        [(SSD_INNER, F32), (SSD_INNER, CDT)], [(1, SSD_INNER)],
                                        ts=256, name="b_ssd_post")
    dxh, dda, dxx, d_b, d_c, dah, ddsk = _ssd_bwd_g(xh, dtc, dac, dar, dsk, xbc, st, _heads(dy_core, SSD_H, SSD_P))
    sg["d_skip"] = ddsk.reshape(1, SSD_H)
    sg["a_log"] = dah.reshape(1, SSD_H) * (-jnp.exp(p["a_log"]))

    def b_dt(ddav, dxxv, dtr, bias, alog):
        ddt = ddav * (-jnp.exp(alog)) + dxxv
        draw = ddt * _sigmoid(dtr + bias)
        return draw, jnp.sum(draw, axis=0, keepdims=True)

    ungroup = lambda t: _pad_cols(t.transpose(1, 0, 2).reshape(s, SSD_H), DT_PAD)
    ddt_raw, dbias128 = _rows(b_dt, [ungroup(dda), ungroup(dxx), dt_raw], [bias128, alog128],
                              [(DT_PAD, CDT)], [(1, DT_PAD)], ts=512, name="b_dt")
    sg["dt_bias"] = dbias128[:, :SSD_H]

    dxbc = jnp.concatenate([_unheads(dxh), d_b, d_c], axis=1)
    dxbc_raw, sg["conv_w"], sg["conv_b"] = _conv_bwd(xbc_raw, pre, dxbc, conv_w)

    dseg = [dqkv, dz, dxbc_raw, dmq, dgl]
    dw_bufs = [lax.empty((c3, d), F32), lax.empty((c5 - c4 + 3 * d, d), F32)]
    for i in range(5):
        bi = 0 if i < 3 else 1
        dw_bufs[bi] = _mm(dseg[i], u, ta=True, into=dw_bufs[bi], into_off=seg_off[i], into_rows=True,
                          name="b_in_w_" + seg_name[i])
    dw_dt = _mm(ddt_raw, u, ta=True, name="b_in_w_dt")
    def du(i, hosted=None):
        return _mm(dseg[i], seg_src[i], b_koff=seg_off[i], name="b_in_x_" + seg_name[i], hosted=hosted)

    if shards is None:
        dus = [du(i) for i in range(5)]
        gw["w_in"] = jnp.concatenate([dw_bufs[0], dw_dt[:SSD_H], dw_bufs[1]], axis=0).T
    else:
        tail_rows, half = dw_bufs[1].shape[0], c3 // 2
        du0, recv_a = du(0, _pair_swap([(dw_bufs[0], 0, half)]))
        du2, recv_b, recv_dt = du(2, _pair_swap([(dw_bufs[0], half, half), (dw_dt, 0, DT_PAD)]))
        du3, recv_tail = du(3, _pair_swap([(dw_bufs[1], 0, tail_rows)]))
        natural = lax.empty((c4 + tail_rows, d), CDT)
        natural = _add_cast_into(dw_bufs[0], recv_a, natural, 0, half, "sum_w_in_head_a")
        natural = _add_cast_into(dw_bufs[0], recv_b, natural, half, half, "sum_w_in_head_b", a_off=half)
        natural = _add_cast_into(dw_dt, recv_dt, natural, c3, SSD_H, "sum_w_in_dt")
        natural = _add_cast_into(dw_bufs[1], recv_tail, natural, c4, tail_rows, "sum_w_in_tail")
        shard = natural.shape[0] // N_DEV
        du1, win_a = du(1, _chip_scatter_windows(natural, shard, 0, d // 2))
        du4, win_b = du(4, _chip_scatter_windows(natural, shard, d // 2, d // 2))
        gw["w_in"] = [win_a, win_b]
        dus = [du0, du1, du2, du3, du4]
    dus.append(_mm(ddt_raw, w_dt, name="b_in_x_dt"))

    def b_pre(xv, dh1v, d0, d1, d2, d3, d4, d5, g):
        dx, dg = _rms_bwd(xv, g, d0 + d1 + d2 + d3 + d4 + d5)
        return dh1v + dx, dg

    grad_x, sg["norm_mix_pre"] = _rows(b_pre, [x, dh1] + dus, [p["norm_mix_pre"]], [(d, F32)], [(1, d)],
                                       ts=256, name="b_norm_pre")
    return loss, grad_x, gw, sg, (received if shards is not None else None)


HBM = pl.BlockSpec(memory_space=pltpu.HBM)
MESH = pl.DeviceIdType.MESH


def _me_and_peers():
    x, y, c = lax.axis_index("x"), lax.axis_index("y"), lax.axis_index("c")
    me = 4 * x + 2 * y + c
    peers = [(x, y, 1 - c), (1 - x, y, c), (x, 1 - y, c), (1 - x, 1 - y, c),
             (1 - x, y, 1 - c), (x, 1 - y, 1 - c), (1 - x, 1 - y, 1 - c)]
    return me, peers


def _peer_index(peer):
    return 4 * peer[0] + 2 * peer[1] + peer[2]


def _exchange_copies(ins, outs, send_sems, recv_sems, local_sems, scatter):
    me, peers = _me_and_peers()
    copies = []
    for a in range(len(ins)):
        own = ins[a].at[me] if scatter else ins[a]
        copies.append(pltpu.make_async_copy(own, outs[a].at[me], local_sems.at[a]))
        for kk, peer in enumerate(peers):
            src = ins[a].at[_peer_index(peer)] if scatter else ins[a]
            copies.append(pltpu.make_async_remote_copy(
                src_ref=src, dst_ref=outs[a].at[me],
                send_sem=send_sems.at[a, kk], recv_sem=recv_sems.at[a, kk],
                device_id=peer, device_id_type=MESH))
    return copies


def _exchange_shapes(ins, scatter):
    return [jax.ShapeDtypeStruct(t.shape if scatter else (N_DEV,) + t.shape, t.dtype) for t in ins]


def _exchange_sems(n):
    return [pltpu.SemaphoreType.DMA((n, N_DEV - 1)), pltpu.SemaphoreType.DMA((n, N_DEV - 1)),
            pltpu.SemaphoreType.DMA((n,))]


def _gather_two_level(shards, name):
    n = len(shards)

    def body(*refs):
        ins, outs = refs[:n], refs[n:2 * n]
        send_sems, recv_sems, local_sems = refs[2 * n:]
        x, y, c = lax.axis_index("x"), lax.axis_index("y"), lax.axis_index("c")
        me, sib = (x, y, c), (x, y, 1 - c)
        chips = [(1 - x, y), (x, 1 - y), (1 - x, 1 - y)]

        def copy(a, k, block, to, src=None):
            slot = outs[a].at[_peer_index(block)]
            return pltpu.make_async_remote_copy(
                src_ref=slot if src is None else src, dst_ref=slot,
                send_sem=send_sems.at[a, k], recv_sem=recv_sems.at[a, k], device_id=to, device_id_type=MESH)

        own = [pltpu.make_async_copy(ins[a], outs[a].at[_peer_index(me)], local_sems.at[a]) for a in range(n)]
        first = []
        for a in range(n):
            first.append(copy(a, 0, me, sib, src=ins[a]))
            first += [copy(a, 1 + j, me, (*chip, c), src=ins[a]) for j, chip in enumerate(chips)]
        for cp in own + first:
            cp.start()
        passed = []
        for j, chip in enumerate(chips):
            for a in range(n):
                copy(a, 1 + j, (*chip, c), me).wait_recv()
                fwd = copy(a, 4 + j, (*chip, c), sib)
                fwd.start()
                passed.append(fwd)
        for a in range(n):
            copy(a, 0, sib, me).wait_recv()
            for j, chip in enumerate(chips):
                copy(a, 4 + j, (*chip, 1 - c), me).wait_recv()
        for cp in first + passed:
            cp.wait_send()
        for cp in own:
            cp.wait()

    return pl.pallas_call(
        body, name=name,
        in_specs=[HBM] * n, out_specs=[HBM] * n,
        out_shape=_exchange_shapes(shards, False),
        scratch_shapes=_exchange_sems(n),
        compiler_params=pltpu.CompilerParams(has_side_effects=True),
    )(*shards)


class _Hosted:
    def __init__(self, ins, shapes, sems, copies):
        self.ins, self.shapes, self.sems, self.copies = ins, shapes, sems, copies


def _pair_swap(pieces):
    n = len(pieces)

    def copies(in_refs, out_refs, sems):
        sib = (lax.axis_index("x"), lax.axis_index("y"), 1 - lax.axis_index("c"))
        return [pltpu.make_async_remote_copy(
            src_ref=in_refs[i].at[pl.ds(r0, nr)], dst_ref=out_refs[i], send_sem=sems[0].at[i],
            recv_sem=sems[1].at[i], device_id=sib, device_id_type=MESH) for i, (_, r0, nr) in enumerate(pieces)]

    return _Hosted([t for t, _, _ in pieces],
                   [jax.ShapeDtypeStruct((nr, t.shape[1]), t.dtype) for t, _, nr in pieces],
                   [pltpu.SemaphoreType.DMA((n,)), pltpu.SemaphoreType.DMA((n,))], copies)


def _add_cast_into(a, b, into, row_off, rows, name, a_off=0):
    w = a.shape[1]
    tr = _pick(rows, (512, 256, 128, rows))
    nt = rows // tr
    assert rows % tr == 0 and row_off % ROW_ALIGN == 0 and tr % ROW_ALIGN == 0 and a_off % tr == 0
    spec = pl.BlockSpec((tr, w), lambda i: (i, 0))

    def body(a_ref, b_ref, into_ref, o_ref, slots, sems):
        i = pl.program_id(0)
        slot = i % 2

        def out_copy(step, s):
            dst = o_ref.at[pl.ds(pl.multiple_of(row_off + step * tr, ROW_ALIGN), tr)]
            return pltpu.make_async_copy(slots.at[s], dst, sems.at[s])

        @pl.when(i >= 2)
        def _():
            out_copy(i - 2, slot).wait()

        slots[slot] = (a_ref[...] + b_ref[...]).astype(slots.dtype)
        out_copy(i, slot).start()

        @pl.when(i == nt - 1)
        def _():
            out_copy(i, slot).wait()
            if nt > 1:
                out_copy(i - 1, 1 - slot).wait()

    return pl.pallas_call(
        body, name=name, grid=(nt,),
        in_specs=[pl.BlockSpec((tr, w), lambda i: (i + a_off // tr, 0)), spec, pl.BlockSpec(memory_space=pl.ANY)],
        out_specs=pl.BlockSpec(memory_space=pl.ANY),
        out_shape=jax.ShapeDtypeStruct(into.shape, into.dtype),
        scratch_shapes=[pltpu.VMEM((2, tr, w), into.dtype), pltpu.SemaphoreType.DMA((2,))],
        input_output_aliases={2: 0},
        compiler_params=_cparams(("arbitrary",)),
    )(a, b, into)


N_CHIP = 4


def _adamw_math(g, w, m, v):
    m2 = ADAM_B1 * m + (1.0 - ADAM_B1) * g
    v2 = ADAM_B2 * v + (1.0 - ADAM_B2) * jnp.square(g)
    m_hat = m2 / (1.0 - ADAM_B1 ** ADAM_STEP)
    v_hat = v2 / (1.0 - ADAM_B2 ** ADAM_STEP)
    delta = -ADAM_LR * (m_hat / (jnp.sqrt(v_hat) + ADAM_EPS) + ADAM_WD * w)
    return delta, m2, v2


def _adamw_reduce(parts, w, m, v, name):
    r, c = w.shape
    nparts = parts.shape[0]
    tr = _pick(r, (128, 64, 32, 16, 8))

    def body(p_ref, w_ref, m_ref, v_ref, g_ref, d_ref, m2_ref, v2_ref):
        g = p_ref[0].astype(F32)
        for i in range(1, nparts):
            g = g + p_ref[i].astype(F32)
        delta, m2, v2 = _adamw_math(g, w_ref[...], m_ref[...], v_ref[...])
        g_ref[...] = g
        d_ref[...] = delta
        m2_ref[...] = m2
        v2_ref[...] = v2

    tile = pl.BlockSpec((tr, c), lambda i: (i, 0))
    shp = jax.ShapeDtypeStruct((r, c), F32)
    return pl.pallas_call(
        body, name=name, grid=(r // tr,),
        in_specs=[pl.BlockSpec((nparts, tr, c), lambda i: (0, i, 0)), tile, tile, tile],
        out_specs=[tile] * 4, out_shape=[shp] * 4,
        compiler_params=_cparams(("parallel",)),
    )(parts, w, m, v)


ROW_ALIGN = 16


def _window(shard):
    lead = max((j * shard) % ROW_ALIGN for j in range(N_DEV))
    return -(-(lead + shard) // ROW_ALIGN) * ROW_ALIGN


def _chip_scatter_windows(t, shard, col0, cols):
    win = _window(shard)
    assert all((j * shard // ROW_ALIGN) * ROW_ALIGN + win <= t.shape[0] for j in range(N_DEV))

    def copies(in_refs, out_refs, sems):
        (t_ref,), (o_ref,), (send_sems, recv_sems, local_sem) = in_refs, out_refs, sems
        x, y, c = lax.axis_index("x"), lax.axis_index("y"), lax.axis_index("c")
        mine = 2 * x + y

        def window(q):
            a0 = pl.multiple_of(((2 * q + c) * shard // ROW_ALIGN) * ROW_ALIGN, ROW_ALIGN)
            return t_ref.at[pl.ds(a0, win), pl.ds(col0, cols)]

        res = [pltpu.make_async_copy(window(mine), o_ref.at[mine], local_sem)]
        for j, (px, py) in enumerate([(1 - x, y), (x, 1 - y), (1 - x, 1 - y)]):
            res.append(pltpu.make_async_remote_copy(
                src_ref=window(2 * px + py), dst_ref=o_ref.at[mine],
                send_sem=send_sems.at[j], recv_sem=recv_sems.at[j], device_id=(px, py, c), device_id_type=MESH))
        return res

    return _Hosted([t], [jax.ShapeDtypeStruct((N_CHIP, win, cols), t.dtype)],
                   [pltpu.SemaphoreType.DMA((N_CHIP - 1,)), pltpu.SemaphoreType.DMA((N_CHIP - 1,)),
                    pltpu.SemaphoreType.DMA], copies)


ADAMW_TC = 256


def _adamw_windows(parts, w, m, v, name):
    r, c = w.shape
    na = len(parts)
    nparts, win, cpart = parts[0].shape
    tc = min(ADAMW_TC, cpart)
    per = cpart // tc

    def body(*refs):
        p_refs = refs[:na]
        w_ref, m_ref, v_ref, g_ref, d_ref, m2_ref, v2_ref = refs[na:]
        me, _ = _me_and_peers()
        step = pl.program_id(0)
        gw_ = None
        for a, p_ref in enumerate(p_refs):
            tot = p_ref[0].astype(F32)
            for i in range(1, nparts):
                tot = tot + p_ref[i].astype(F32)
            gw_ = tot if gw_ is None else jnp.where(step // per == a, tot, gw_)
        for j in range(N_DEV):
            @pl.when(me == j)
            def _():
                lead = (j * r) % ROW_ALIGN
                g = (pltpu.roll(gw_, win - lead, 0) if lead else gw_)[:r]
                delta, m2, v2 = _adamw_math(g, w_ref[...], m_ref[...], v_ref[...])
                g_ref[...] = g
                d_ref[...] = delta
                m2_ref[...] = m2
                v2_ref[...] = v2

    tile = pl.BlockSpec((r, tc), lambda i: (0, i))
    shp = jax.ShapeDtypeStruct((r, c), F32)
    return pl.pallas_call(
        body, name=name, grid=(c // tc,),
        in_specs=[pl.BlockSpec((nparts, win, tc), lambda i, a=a: (0, 0, jnp.clip(i - a * per, 0, per - 1)))
                  for a in range(na)] + [tile, tile, tile],
        out_specs=[tile] * 4, out_shape=[shp] * 4,
        compiler_params=_cparams(("parallel",)),
    )(*parts, w, m, v)


SMALL_ROWS, SMALL_COLS = 16, 3072


def _small_step(sg, gcw, loss, ws, ms, vs):
    ns = len(sg)
    widths = [t.shape[1] for t in sg]
    kk_, ch = gcw.shape[1], gcw.shape[2]
    assert ns < SMALL_ROWS and max(widths) <= SMALL_COLS

    def reduce_body(*refs):
        g_refs = refs[:ns]
        gcw_ref, loss_ref, tot_ref, totc_ref = refs[ns:ns + 4]
        mine, buf, minec, bufc, send_sems, recv_sems = refs[ns + 4:]
        me, peers = _me_and_peers()

        mine[...] = jnp.zeros_like(mine)
        for i in range(ns):
            mine[i:i + 1, 0:widths[i]] = g_refs[i][...]
        mine[ns:ns + 1, 0:LANES] = jnp.broadcast_to(loss_ref[...], (1, LANES))
        minec[...] = gcw_ref[...]
        buf[me] = mine[...]
        bufc[me] = minec[...]
        copies = []
        for j, peer in enumerate(peers):
            copies.append(pltpu.make_async_remote_copy(
                src_ref=mine, dst_ref=buf.at[me], send_sem=send_sems.at[0, j], recv_sem=recv_sems.at[0, j],
                device_id=peer, device_id_type=MESH))
            copies.append(pltpu.make_async_remote_copy(
                src_ref=minec, dst_ref=bufc.at[me], send_sem=send_sems.at[1, j], recv_sem=recv_sems.at[1, j],
                device_id=peer, device_id_type=MESH))
        for cp in copies:
            cp.start()
        for cp in copies:
            cp.wait()
        tot = buf[0]
        totc = bufc[0]
        for i in range(1, N_DEV):
            tot = tot + buf[i]
            totc = totc + bufc[i]
        tot_ref[...] = tot
        totc_ref[...] = totc

    vm = pl.BlockSpec(memory_space=pltpu.VMEM)
    tot, totc = pl.pallas_call(
        reduce_body, name="small_reduce",
        in_specs=[vm] * (ns + 2), out_specs=[vm, vm],
        out_shape=[jax.ShapeDtypeStruct((SMALL_ROWS, SMALL_COLS), F32), jax.ShapeDtypeStruct((N_DEV, kk_, ch), F32)],
        scratch_shapes=[pltpu.VMEM((SMALL_ROWS, SMALL_COLS), F32), pltpu.VMEM((N_DEV, SMALL_ROWS, SMALL_COLS), F32),
                        pltpu.VMEM((N_DEV, kk_, ch), F32), pltpu.VMEM((N_DEV, N_DEV, kk_, ch), F32),
                        pltpu.SemaphoreType.DMA((2, N_DEV - 1)), pltpu.SemaphoreType.DMA((2, N_DEV - 1))],
        compiler_params=pltpu.CompilerParams(has_side_effects=True),
    )(*sg, gcw, loss)

    def adamw_body(*refs):
        tot_ref, totc_ref = refs[:2]
        w_refs, m_refs, v_refs = (refs[2 + i * (ns + 1):2 + (i + 1) * (ns + 1)] for i in range(3))
        outs = refs[3 * ns + 5:]
        loss_out = outs[0]
        go, do_, mo, vo = (outs[1 + i * (ns + 1):1 + (i + 1) * (ns + 1)] for i in range(4))
        me, _ = _me_and_peers()
        loss_out[...] = tot_ref[ns:ns + 1, 0:1]
        for i in range(ns + 1):
            g = tot_ref[i:i + 1, 0:widths[i]] if i < ns else totc_ref[me]
            delta, m2, v2 = _adamw_math(g, w_refs[i][...], m_refs[i][...], v_refs[i][...])
            go[i][...] = g
            do_[i][...] = delta
            mo[i][...] = m2
            vo[i][...] = v2

    shapes = [jax.ShapeDtypeStruct(t.shape, F32) for t in ws]
    res = pl.pallas_call(
        adamw_body, name="small_adamw",
        in_specs=[vm] * (3 * ns + 5), out_specs=[vm] * (4 * ns + 5),
        out_shape=[jax.ShapeDtypeStruct((1, 1), F32)] + shapes * 4,
    )(tot, totc, *ws, *ms, *vs)
    n1 = ns + 1
    return res[0], res[1:1 + n1], res[1 + n1:1 + 2 * n1], res[1 + 2 * n1:1 + 3 * n1], res[1 + 3 * n1:]


def _cast_shard(w, name):
    r = w.shape[0]
    return _rows(lambda t: t, [w], [], [(w.shape[1], CDT)], ts=_pick(r, (256, 128)), name=name)[0]


BIG = ["w_in", "w_mem_kv", "w_up", "w_sb_out", "w_ssd_out", "w_mem_out", "w_o", "w_down"]
LATE_W = BIG[1:]
COL_SHARDED = ("w_in", "w_mem_kv", "w_up")
SMALL = ["norm_mix_pre", "conv_b", "dt_bias", "a_log", "d_skip", "ssd_norm", "norm_mem",
         "norm_mix_post", "norm_mlp_pre", "norm_mlp_post"]
ALL_W = ["norm_mix_pre", "w_in", "conv_w", "conv_b", "dt_bias", "a_log", "d_skip", "ssd_norm", "norm_mem",
         "w_mem_kv", "w_sb_out", "w_ssd_out", "w_mem_out", "w_o", "norm_mix_post", "norm_mlp_pre", "w_up",
         "w_down", "norm_mlp_post"]
LANES = 128


def kernel(x, mem, norm_mix_pre, w_in, conv_w, conv_b, dt_bias, a_log, d_skip, ssd_norm, norm_mem, w_mem_kv, w_sb_out, w_ssd_out, w_mem_out, w_o, norm_mix_post, norm_mlp_pre, w_up, w_down, norm_mlp_post, loss_target, m_norm_mix_pre, m_w_in, m_conv_w, m_conv_b, m_dt_bias, m_a_log, m_d_skip, m_ssd_norm, m_norm_mem, m_w_mem_kv, m_w_sb_out, m_w_ssd_out, m_w_mem_out, m_w_o, m_norm_mix_post, m_norm_mlp_pre, m_w_up, m_w_down, m_norm_mlp_post, v_norm_mix_pre, v_w_in, v_conv_w, v_conv_b, v_dt_bias, v_a_log, v_d_skip, v_ssd_norm, v_norm_mem, v_w_mem_kv, v_w_sb_out, v_w_ssd_out, v_w_mem_out, v_w_o, v_norm_mix_post, v_norm_mlp_pre, v_w_up, v_w_down, v_norm_mlp_post):
    wd = dict(norm_mix_pre=norm_mix_pre, w_in=w_in, conv_w=conv_w, conv_b=conv_b, dt_bias=dt_bias, a_log=a_log,
              d_skip=d_skip, ssd_norm=ssd_norm, norm_mem=norm_mem, w_mem_kv=w_mem_kv, w_sb_out=w_sb_out,
              w_ssd_out=w_ssd_out, w_mem_out=w_mem_out, w_o=w_o, norm_mix_post=norm_mix_post,
              norm_mlp_pre=norm_mlp_pre, w_up=w_up, w_down=w_down, norm_mlp_post=norm_mlp_post)
    md = dict(norm_mix_pre=m_norm_mix_pre, w_in=m_w_in, conv_w=m_conv_w, conv_b=m_conv_b, dt_bias=m_dt_bias,
              a_log=m_a_log, d_skip=m_d_skip, ssd_norm=m_ssd_norm, norm_mem=m_norm_mem, w_mem_kv=m_w_mem_kv,
              w_sb_out=m_w_sb_out, w_ssd_out=m_w_ssd_out, w_mem_out=m_w_mem_out, w_o=m_w_o,
              norm_mix_post=m_norm_mix_post, norm_mlp_pre=m_norm_mlp_pre, w_up=m_w_up, w_down=m_w_down,
              norm_mlp_post=m_norm_mlp_post)
    vd = dict(norm_mix_pre=v_norm_mix_pre, w_in=v_w_in, conv_w=v_conv_w, conv_b=v_conv_b, dt_bias=v_dt_bias,
              a_log=v_a_log, d_skip=v_d_skip, ssd_norm=v_ssd_norm, norm_mem=v_norm_mem, w_mem_kv=v_w_mem_kv,
              w_sb_out=v_w_sb_out, w_ssd_out=v_w_ssd_out, w_mem_out=v_w_mem_out, w_o=v_w_o,
              norm_mix_post=v_norm_mix_post, norm_mlp_pre=v_norm_mlp_pre, w_up=v_w_up, w_down=v_w_down,
              norm_mlp_post=v_norm_mlp_post)
    w_in_t, m_in_t, v_in_t = (t["w_in"][0].T for t in (wd, md, vd))
    shards = {n: _cast_shard(w_in_t if n == "w_in" else wd[n][0], "cast_" + n) for n in BIG}
    w_in_g, conv_w_g = _gather_two_level([shards["w_in"], wd["conv_w"][0]], "gather_w_in")
    wt = {"w_in": w_in_g.reshape(N_DEV * w_in_g.shape[1], w_in_g.shape[2])}
    ch = conv_w_g.shape[2]

    p = {n: wd[n] for n in SMALL}
    p["conv_w"] = conv_w_g.transpose(1, 0, 2).reshape(CONV_K, N_DEV * ch)
    loss, grad_x, gw, sg, late_received = _local_step(x[0], mem[0], loss_target[0], p, wt,
                                                      [shards[n] for n in LATE_W])

    received = dict(zip(LATE_W, late_received))
    w_in_windows = gw["w_in"]

    grads, deltas, new_m, new_v = {}, {}, {}, {}
    for n in BIG:
        if n == "w_in":
            res = [t.T for t in _adamw_windows(w_in_windows, w_in_t, m_in_t, v_in_t, "adamw_" + n)]
        else:
            res = _adamw_reduce(received[n], wd[n][0], md[n][0], vd[n][0], "adamw_" + n)
        grads[n], deltas[n], new_m[n], new_v[n] = (t[None] for t in res)
    small_names = SMALL + ["conv_w"]
    gcw = sg["conv_w"].reshape(CONV_K, N_DEV, ch).transpose(1, 0, 2)
    small_of = lambda dct: [dct[n] for n in SMALL] + [dct["conv_w"][0]]
    loss_red, g_s, d_s, m_s, v_s = _small_step([sg[n] for n in SMALL], gcw, loss, small_of(wd), small_of(md),
                                               small_of(vd))
    for i, n in enumerate(small_names):
        shape = wd[n].shape
        grads[n], deltas[n], new_m[n], new_v[n] = (t.reshape(shape) for t in (g_s[i], d_s[i], m_s[i], v_s[i]))
    loss_out = loss_red.reshape(())

    return (loss_out, grad_x[None], *[grads[n] for n in ALL_W], *[deltas[n] for n in ALL_W],
            *[new_m[n] for n in ALL_W], *[new_v[n] for n in ALL_W])
```

```python
import functools

import jax
import jax.numpy as jnp
from jax import lax
from jax.experimental import pallas as pl
from jax.experimental.pallas import tpu as pltpu

F32 = jnp.float32
BF16 = jnp.bfloat16
CDT = jnp.bfloat16
EPS = 1e-6
VMEM_LIMIT = 56 * 1024 * 1024

N_DEV = 8
D_MODEL = 1024
SB_H, SB_DH = 16, 64
SSD_G, SSD_R, SSD_P, SSD_N, SSD_L = 4, 8, 64, 128, 128
SSD_H = SSD_G * SSD_R
SSD_INNER = SSD_H * SSD_P
CONV_K = 4
CONV_DIM = SSD_INNER + 2 * SSD_G * SSD_N
MEM_H, MEM_DH = 4, 256
DT_PAD = 128
SB_TQ, SB_BK = 2048, 256
CONV_PAD = 8
MM_TILE, MM_TILE_K = 1024, 2048

ADAM_LR, ADAM_B1, ADAM_B2, ADAM_EPS, ADAM_WD, ADAM_STEP = 0.001, 0.9, 0.999, 1e-08, 0.01, 10

NT = (((1,), (1,)), ((), ()))
TN = (((0,), (0,)), ((), ()))
NN = (((1,), (0,)), ((), ()))


def _cparams(sem=None):
    return pltpu.CompilerParams(dimension_semantics=sem, vmem_limit_bytes=VMEM_LIMIT)


def _pick(n, cands):
    for c in cands:
        if n % c == 0:
            return c
    return n


def _dot(a, b, dims=NN):
    return lax.dot_general(a.astype(CDT), b.astype(CDT), dims, preferred_element_type=F32)


def _split_dot(x, t, left, pieces):
    if CDT == F32:
        return lax.dot_general(t, x, NN, preferred_element_type=F32) if left else \
            lax.dot_general(x, t, NN, preferred_element_type=F32)
    acc = None
    rem = x
    for _ in range(pieces):
        hi = rem.astype(BF16)
        rem = rem - hi.astype(F32)
        d = lax.dot_general(t, hi, NN, preferred_element_type=F32) if left else \
            lax.dot_general(hi, t, NN, preferred_element_type=F32)
        acc = d if acc is None else acc + d
    return acc


def _iota(shape, dim):
    return lax.broadcasted_iota(jnp.int32, shape, dim)


def _sigmoid(x):
    return 1.0 / (1.0 + jnp.exp(-x))


def _silu(x):
    return x * _sigmoid(x)


def _dsilu(x):
    s = _sigmoid(x)
    return s * (1.0 + x * (1.0 - s))


def _softplus(x):
    return jnp.maximum(x, 0.0) + jnp.log(1.0 + jnp.exp(-jnp.abs(x)))


def _rms(x, g):
    r = lax.rsqrt(jnp.mean(x * x, axis=-1, keepdims=True) + EPS)
    return x * r * g


def _rms_bwd(x, g, dy):
    r = lax.rsqrt(jnp.mean(x * x, axis=-1, keepdims=True) + EPS)
    n = x * r
    dn = dy * g
    dx = r * (dn - n * jnp.mean(dn * n, axis=-1, keepdims=True))
    dg = jnp.sum(dy * n, axis=0, keepdims=True)
    return dx, dg


def _mm(a, b, *, ta=False, tb=False, out_dtype=F32, name, b_off=0, n=None, b_koff=0, into=None, into_off=0,
        into_rows=False, hosted=None, epilogue=None):
    m = a.shape[1] if ta else a.shape[0]
    k = a.shape[0] if ta else a.shape[1]
    if n is None:
        n = b.shape[0] if tb else b.shape[1]
    assert b_koff + k <= (b.shape[1] if tb else b.shape[0])
    bm = _pick(m, (MM_TILE, 512, 256, 128))
    bn = _pick(n, (MM_TILE, 512, 256, 128))
    bk = next(c for c in (MM_TILE_K, 1024, 512, 256, 128, k) if k % c == 0 and b_koff % c == 0)
    nk = k // bk
    assert b_off % bn == 0 and into_off % (bm if into_rows else bn) == 0
    jb, kb = b_off // bn, b_koff // bk
    io, jo = (into_off // bm, 0) if into_rows else (0, into_off // bn)
    dims = (((0 if ta else 1,), (1 if tb else 0,)), ((), ()))
    grid = (m // bm, n // bn, nk)
    off = 1 if into is not None else 0
    nh_in = len(hosted.ins) if hosted else 0
    nh_out = len(hosted.shapes) if hosted else 0
    epi_fn, extras, out_dtypes = epilogue if epilogue else (lambda p: (p,), [], [out_dtype])
    ne, no = len(extras), len(out_dtypes)
    assert not (epilogue and into is not None)

    def body(a_ref, b_ref, *rest):
        e_refs = rest[off:off + ne]
        o_refs = rest[off + ne + nh_in:off + ne + nh_in + no]
        acc_ref = rest[off + ne + nh_in + no + nh_out]

        def emit(total):
            res = epi_fn(total, *[e[...] for e in e_refs])
            for o_ref, val in zip(o_refs, res):
                o_ref[...] = val.astype(o_ref.dtype)

        if hosted:
            h_refs = (rest[off + ne:off + ne + nh_in],
                      rest[off + ne + nh_in + no:off + ne + nh_in + no + nh_out],
                      rest[off + ne + nh_in + no + nh_out + 1:])
            step = (pl.program_id(0) * grid[1] + pl.program_id(1)) * grid[2] + pl.program_id(2)

            @pl.when(step == 0)
            def _():
                for cp in hosted.copies(*h_refs):
                    cp.start()

        part = _dot(a_ref[...], b_ref[...], dims)
        if nk == 1:
            emit(part)
        else:
            kk = pl.program_id(2)

            @pl.when(kk == 0)
            def _():
                acc_ref[...] = part

            @pl.when(jnp.logical_and(kk > 0, kk < nk - 1))
            def _():
                acc_ref[...] += part

            @pl.when(kk == nk - 1)
            def _():
                emit(acc_ref[...] + part)

        if hosted:
            @pl.when(step == grid[0] * grid[1] * grid[2] - 1)
            def _():
                for cp in hosted.copies(*h_refs):
                    cp.wait()

    a_spec = pl.BlockSpec((bk, bm), lambda i, j, kk: (kk, i)) if ta else \
        pl.BlockSpec((bm, bk), lambda i, j, kk: (i, kk))
    b_spec = pl.BlockSpec((bn, bk), lambda i, j, kk: (j + jb, kk + kb)) if tb else \
        pl.BlockSpec((bk, bn), lambda i, j, kk: (kk + kb, j + jb))
    extra = {} if into is None else {"input_output_aliases": {2: 0}}
    out_shapes = [jax.ShapeDtypeStruct((m, n), dt) for dt in out_dtypes] if into is None else \
        [jax.ShapeDtypeStruct(into.shape, into.dtype)]
    block = pl.BlockSpec((bm, bn), lambda i, j, kk: (i, j))
    res = pl.pallas_call(
        body, name=name, grid=grid,
        in_specs=[a_spec, b_spec] + ([] if into is None else [pl.BlockSpec(memory_space=pl.ANY)])
        + [block] * ne + [HBM] * nh_in,
        out_specs=[pl.BlockSpec((bm, bn), lambda i, j, kk: (i + io, j + jo))] * no + [HBM] * nh_out,
        out_shape=out_shapes + (list(hosted.shapes) if hosted else []),
        scratch_shapes=[pltpu.VMEM((bm, bn) if nk > 1 else (8, 128), F32)] + (list(hosted.sems) if hosted else []),
        compiler_params=_cparams(("arbitrary",) * 3 if hosted else ("parallel", "parallel", "arbitrary")),
        **extra,
    )(*((a, b) if into is None else (a, b, into)), *extras, *(hosted.ins if hosted else ()))
    return res if (hosted or epilogue) else res[0]


def _rows(fn, tiled, params, outs, accs=(), *, ts, name, hosted=None):
    s = tiled[0].shape[0]
    ts = min(ts, s)
    assert s % ts == 0
    nt, npar, no, na = len(tiled), len(params), len(outs), len(accs)
    nh_in = len(hosted.ins) if hosted else 0
    nh_out = len(hosted.shapes) if hosted else 0

    def body(*all_refs):
        i = pl.program_id(0)
        n_in, n_out = nt + npar, no + na
        refs = all_refs[:n_in] + all_refs[n_in + nh_in:n_in + nh_in + n_out]
        if hosted:
            h_refs = (all_refs[n_in:n_in + nh_in], all_refs[n_in + nh_in + n_out:n_in + nh_in + n_out + nh_out],
                      all_refs[n_in + nh_in + n_out + nh_out:])

            @pl.when(i == 0)
            def _():
                for cp in hosted.copies(*h_refs):
                    cp.start()

        vals = [r[...] for r in refs[:nt + npar]]
        res = fn(*vals)
        if not isinstance(res, (tuple, list)):
            res = (res,)
        orefs = refs[nt + npar:nt + npar + no]
        arefs = refs[nt + npar + no:]
        for r_, val in zip(orefs, res[:no]):
            r_[...] = val.astype(r_.dtype)
        if na:
            @pl.when(i == 0)
            def _():
                for r_ in arefs:
                    r_[...] = jnp.zeros_like(r_)

            for r_, val in zip(arefs, res[no:]):
                r_[...] += jnp.broadcast_to(val, r_.shape)

        if hosted:
            @pl.when(i == s // ts - 1)
            def _():
                for cp in hosted.copies(*h_refs):
                    cp.wait()

    in_specs = [pl.BlockSpec((ts, a.shape[1]), lambda i: (i, 0)) for a in tiled]
    in_specs += [pl.BlockSpec(p.shape, lambda i: (0, 0)) for p in params]
    out_specs = [pl.BlockSpec((ts, w), lambda i: (i, 0)) for (w, _) in outs]
    out_specs += [pl.BlockSpec(shape, lambda i: (0, 0)) for shape in accs]
    out_shape = [jax.ShapeDtypeStruct((s, w), dt) for (w, dt) in outs]
    out_shape += [jax.ShapeDtypeStruct(shape, F32) for shape in accs]
    res = pl.pallas_call(
        body, name=name, grid=(s // ts,),
        in_specs=in_specs + [HBM] * nh_in, out_specs=out_specs + [HBM] * nh_out,
        out_shape=out_shape + (list(hosted.shapes) if hosted else []),
        scratch_shapes=list(hosted.sems) if hosted else [],
        compiler_params=_cparams(("arbitrary",)),
    )(*tiled, *params, *(hosted.ins if hosted else ()))
    return res


def _sb_block(qs, kb, diag):
    tq, bk = qs.shape[0], kb.shape[0]
    z = _dot(qs, kb, NT)
    lb = jnp.minimum(z, 0.0) - jnp.log(1.0 + jnp.exp(-jnp.abs(z)))
    lk = lb - z
    if diag is None:
        return None, lb, lk
    causal = (diag + _iota((tq, bk), 1)) < _iota((tq, bk), 0)
    return causal, lb, jnp.where(causal, lk, 0.0)


def _fused_exchange(scatter, ncols, nsteps):
    def hooks(ins, outs, sems):
        step = pl.program_id(0) * ncols + pl.program_id(1)

        @pl.when(step == 0)
        def _():
            for cp in _exchange_copies(ins, outs, *sems, scatter):
                cp.start()

        def finish():
            @pl.when(step == nsteps - 1)
            def _():
                for cp in _exchange_copies(ins, outs, *sems, scatter):
                    cp.wait()
        return finish
    return hooks


SB_PAIR = 128


def _sb_fwd_pairs(qkv, comm=()):
    s, d3 = qkv.shape
    d = d3 // 3
    npair = d // SB_PAIR
    tq, bk = min(SB_TQ, s), min(SB_BK, s)
    scale = SB_DH ** -0.5
    nc = len(comm)
    hooks = _fused_exchange(False, s // tq, npair * (s // tq))

    def body(q_ref, k_ref, v_ref, *rest):
        y_ref, lt_ref = rest[nc:nc + 2]
        if nc:
            finish = hooks(rest[:nc], rest[nc + 2:2 * nc + 2], rest[2 * nc + 2:])
        i = pl.program_id(1)
        q0 = i * tq
        q2 = q_ref[...] * scale
        lane_head = (_iota((1, SB_PAIR), 1) >= SB_DH).astype(jnp.int32)
        tri = (_iota((bk, bk), 0) > _iota((bk, bk), 1)).astype(CDT)

        def head(hh, y):
            mine = lane_head == hh
            qs = jnp.where(mine, q2, jnp.zeros_like(q2))

            def step(k0, carry, diag, r0=0):
                cf, acc = carry
                kb = k_ref[pl.ds(k0, bk), :]
                vb = v_ref[pl.ds(k0, bk), :]
                causal, lb, lk = _sb_block(qs[r0:], kb, diag)
                w = jnp.exp(lb + cf + _split_dot(lk, tri, False, 2))
                if causal is not None:
                    w = jnp.where(causal, w, 0.0)
                return cf + jnp.sum(lk, axis=1, keepdims=True), acc + _dot(w, vb)

            carry = (jnp.zeros((tq, 1), F32), jnp.zeros((tq, SB_PAIR), F32))
            for dd in reversed(range(tq // bk)):
                r0 = dd * bk
                sub = step(pl.multiple_of(q0 + r0, bk), tuple(t[r0:] for t in carry), 0, r0)
                carry = tuple(jnp.concatenate([t[:r0], u], axis=0) if r0 else u for t, u in zip(carry, sub))
            nfull = q0 // bk
            cf, acc = lax.fori_loop(
                0, nfull, lambda jj, c: step(pl.multiple_of((nfull - 1 - jj) * bk, bk), c, None), carry)
            lt_ref[hh] = cf
            return jnp.where(mine, acc, y)

        y_ref[...] = lax.fori_loop(0, 2, head, jnp.zeros((tq, SB_PAIR), F32)).astype(y_ref.dtype)
        if nc:
            finish()

    return pl.pallas_call(
        body, name="sb_fwd", grid=(npair, s // tq),
        in_specs=[pl.BlockSpec((tq, SB_PAIR), lambda a, i: (i, a)),
                  pl.BlockSpec((s, SB_PAIR), lambda a, i: (0, npair + a)),
                  pl.BlockSpec((s, SB_PAIR), lambda a, i: (0, 2 * npair + a))] + [HBM] * nc,
        out_specs=[pl.BlockSpec((tq, SB_PAIR), lambda a, i: (i, a)),
                   pl.BlockSpec((2, tq, 1), lambda a, i: (a, i, 0))] + [HBM] * nc,
        out_shape=[jax.ShapeDtypeStruct((s, d), CDT), jax.ShapeDtypeStruct((2 * npair, s, 1), F32)]
        + _exchange_shapes(comm, False),
        scratch_shapes=_exchange_sems(nc) if nc else [],
        compiler_params=_cparams(("arbitrary", "arbitrary")),
    )(qkv, qkv, qkv, *comm)


def _sb_bwd_pairs(qkv, ltot, dy, comm=()):
    s, d3 = qkv.shape
    d = d3 // 3
    npair = d // SB_PAIR
    tq, bk = min(SB_TQ, s), min(SB_BK, s)
    scale = SB_DH ** -0.5
    nc = len(comm)
    hooks = _fused_exchange(True, s // tq, npair * (s // tq))

    def body(q_ref, k_ref, v_ref, lt_ref, dy_ref, *rest):
        dq_ref, dk_ref, dv_ref = rest[nc:nc + 3]
        if nc:
            finish = hooks(rest[:nc], rest[nc + 3:2 * nc + 3], rest[2 * nc + 3:])
        i = pl.program_id(1)

        @pl.when(i == 0)
        def _():
            dk_ref[...] = jnp.zeros_like(dk_ref)
            dv_ref[...] = jnp.zeros_like(dv_ref)

        q0 = i * tq
        q2 = q_ref[...] * scale
        do2 = dy_ref[...].astype(CDT)
        lane_head = (_iota((1, SB_PAIR), 1) >= SB_DH).astype(jnp.int32)
        tri_le = (_iota((bk, bk), 0) <= _iota((bk, bk), 1)).astype(CDT)
        tri_lt = (_iota((bk, bk), 0) < _iota((bk, bk), 1)).astype(CDT)

        def head(hh, dq_all):
            mine = lane_head == hh
            qs = jnp.where(mine, q2, jnp.zeros_like(q2))
            dov = jnp.where(mine, do2, jnp.zeros_like(do2))
            ltot_h = lt_ref[hh]

            def step(k0, carry, diag, r0=0):
                cf, cg, dq = carry
                kb = k_ref[pl.ds(k0, bk), :]
                vb = v_ref[pl.ds(k0, bk), :]
                causal, lb, lk = _sb_block(qs[r0:], kb, diag)
                w = jnp.exp(lb + ltot_h[r0:] - (cf + _split_dot(lk, tri_le, False, 2)))
                if causal is not None:
                    w = jnp.where(causal, w, 0.0)
                g = w * _dot(dov[r0:], vb, NT)
                gsum = cg + _split_dot(g, tri_lt, False, 2)
                dz = g - (g + gsum) * jnp.exp(lb)
                if causal is not None:
                    dz = jnp.where(causal, dz, 0.0)
                dzc = dz.astype(CDT)
                dk_ref[pl.ds(k0, bk), :] += _dot(dzc, qs[r0:], TN)
                dv_ref[pl.ds(k0, bk), :] += _dot(w, dov[r0:], TN)
                kbm = jnp.where(mine, kb, jnp.zeros_like(kb))
                return (cf + jnp.sum(lk, axis=1, keepdims=True), cg + jnp.sum(g, axis=1, keepdims=True),
                        dq + _dot(dzc, kbm))

            carry = (jnp.zeros((tq, 1), F32), jnp.zeros((tq, 1), F32), jnp.zeros((tq, SB_PAIR), F32))
            carry = lax.fori_loop(0, q0 // bk, lambda jj, c: step(pl.multiple_of(jj * bk, bk), c, None), carry)
            for dd in range(tq // bk):
                r0 = dd * bk
                sub = step(pl.multiple_of(q0 + r0, bk), tuple(t[r0:] for t in carry), 0, r0)
                carry = tuple(jnp.concatenate([t[:r0], u], axis=0) if r0 else u for t, u in zip(carry, sub))
            return dq_all + carry[2]

        dq_ref[...] = lax.fori_loop(0, 2, head, jnp.zeros((tq, SB_PAIR), F32)) * scale
        if nc:
            finish()

    tile = pl.BlockSpec((tq, SB_PAIR), lambda a, i: (i, a))
    acc = pl.BlockSpec((s, SB_PAIR), lambda a, i: (0, a))
    shp = jax.ShapeDtypeStruct((s, d), F32)
    return pl.pallas_call(
        body, name="sb_bwd", grid=(npair, s // tq),
        in_specs=[tile, pl.BlockSpec((s, SB_PAIR), lambda a, i: (0, npair + a)),
                  pl.BlockSpec((s, SB_PAIR), lambda a, i: (0, 2 * npair + a)),
                  pl.BlockSpec((2, tq, 1), lambda a, i: (a, i, 0)), tile] + [HBM] * nc,
        out_specs=[tile, acc, acc] + [HBM] * nc,
        out_shape=[shp, shp, shp] + _exchange_shapes(comm, True),
        scratch_shapes=_exchange_sems(nc) if nc else [],
        compiler_params=_cparams(("arbitrary", "arbitrary")),
    )(qkv, qkv, qkv, ltot, dy, *comm)


def _pick_lane(tile, r):
    return jnp.sum(jnp.where(_iota(tile.shape, 1) == r, tile, 0.0), axis=1, keepdims=True)


def _pick_row(tile, r):
    return jnp.sum(jnp.where(_iota(tile.shape, 0) == r, tile, 0.0), axis=0, keepdims=True)


SSD_CPS = 4


def _ssd_chunk_common(cv, bv, dac, dar):
    l = SSD_L
    tdt = F32 if CDT == F32 else BF16
    lower = (_iota((l, l), 1) <= _iota((l, l), 0)).astype(tdt)
    upper = (_iota((l, l), 0) <= _iota((l, l), 1)).astype(tdt)
    return _dot(cv, bv, NT), _split_dot(dac, lower, True, 3), _split_dot(dar, upper, False, 3)


def _ssd_fwd_g(xh, dtc, dac, dar, dsk, xbc):
    hh, s, p = xh.shape
    l, n, g_, r_ = SSD_L, SSD_N, SSD_G, SSD_R
    nc = s // l
    cps = SSD_CPS if nc % SSD_CPS == 0 else 1
    lb = cps * l
    boff = SSD_INNER // n
    coff = boff + g_

    def body(x_ref, dtc_ref, dac_ref, dar_ref, dsk_ref, b_ref, c_ref, y_ref, st_ref, state_ref):
        c = pl.program_id(1)

        @pl.when(c == 0)
        def _():
            state_ref[...] = jnp.zeros_like(state_ref)

        mask = _iota((l, l), 1) <= _iota((l, l), 0)
        dskv = dsk_ref[...]
        for cc in range(cps):
            rows = pl.ds(cc * l, l)
            cv, bv, dtcv = c_ref[rows, :], b_ref[rows, :], dtc_ref[rows, :]
            cb, acs_c, acs_r = _ssd_chunk_common(cv, bv, dac_ref[rows, :], dar_ref[:, rows])
            for r in range(r_):
                a_col = _pick_lane(acs_c, r)
                a_row = _pick_row(acs_r, r)
                dsk_h = _pick_lane(dskv, r)
                a_b = jnp.broadcast_to(a_col, (l, l))
                dt_b = jnp.broadcast_to(_pick_lane(dtcv, r), (l, p))
                xv = x_ref[r, rows, :]
                xd = xv * dt_b
                decay = jnp.where(mask, jnp.exp(jnp.minimum(a_b - a_row, 0.0)), 0.0)
                hprev = state_ref[r]
                y = _dot(cb * decay, xd) + jnp.exp(a_b)[:, :p] * _dot(cv, hprev)
                y_ref[r, rows, :] = y + dsk_h * xv
                a_end = a_col[l - 1:l, :]
                st_ref[r, cc] = hprev
                state_ref[r] = hprev * jnp.exp(a_end) + _dot(bv, xd * jnp.exp(a_end - a_b)[:, :p], TN)

    return pl.pallas_call(
        body, name="ssd_fwd", grid=(g_, nc // cps),
        in_specs=[pl.BlockSpec((r_, lb, p), lambda g, c: (g, c, 0)),
                  pl.BlockSpec((None, lb, r_), lambda g, c: (g, c, 0)),
                  pl.BlockSpec((None, lb, r_), lambda g, c: (g, c, 0)),
                  pl.BlockSpec((None, r_, lb), lambda g, c: (g, 0, c)),
                  pl.BlockSpec((None, 1, r_), lambda g, c: (g, 0, 0)),
                  pl.BlockSpec((lb, n), lambda g, c: (c, (boff + g))),
                  pl.BlockSpec((lb, n), lambda g, c: (c, (coff + g)))],
        out_specs=[pl.BlockSpec((r_, lb, p), lambda g, c: (g, c, 0)),
                   pl.BlockSpec((r_, cps, n, p), lambda g, c: (g, c, 0, 0))],
        out_shape=[jax.ShapeDtypeStruct((hh, s, p), F32),
                   jax.ShapeDtypeStruct((hh, nc, n, p), F32)],
        scratch_shapes=[pltpu.VMEM((r_, n, p), F32)],
        compiler_params=_cparams(("parallel", "arbitrary")),
    )(xh, dtc, dac, dar, dsk, xbc, xbc)


def _ssd_bwd_g(xh, dtc, dac, dar, dsk, xbc, st, dy):
    hh, s, p = xh.shape
    l, n, g_, r_ = SSD_L, SSD_N, SSD_G, SSD_R
    nc = s // l
    cps = SSD_CPS if nc % SSD_CPS == 0 else 1
    lb, ncb = cps * l, nc // cps
    boff = SSD_INNER // n
    coff = boff + g_

    def body(x_ref, dtc_ref, dac_ref, dar_ref, dsk_ref, b_ref, c_ref, st_ref, dy_ref,
             dx_ref, dda_ref, dxx_ref, db_ref, dc_ref, dah_ref, ddsk_ref, dstate_ref):
        c = pl.program_id(1)

        @pl.when(c == 0)
        def _():
            dstate_ref[...] = jnp.zeros_like(dstate_ref)
            dah_ref[...] = jnp.zeros_like(dah_ref)
            ddsk_ref[...] = jnp.zeros_like(ddsk_ref)

        il = _iota((l, l), 0)
        isx = _iota((l, l), 1)
        tdt = F32 if CDT == F32 else BF16
        t1 = (isx >= il).astype(tdt)
        lane = _iota((l, r_), 1)
        lane1 = _iota((1, r_), 1)
        dskv = dsk_ref[...]
        for cc in reversed(range(cps)):
            rows = pl.ds(cc * l, l)
            cv, bv, dtcv = c_ref[rows, :], b_ref[rows, :], dtc_ref[rows, :]
            cb, acs_c, acs_r = _ssd_chunk_common(cv, bv, dac_ref[rows, :], dar_ref[:, rows])
            dda_all = jnp.zeros((l, r_), F32)
            dxx_all = jnp.zeros((l, r_), F32)
            dah_all = jnp.zeros((1, r_), F32)
            ddsk_all = jnp.zeros((1, r_), F32)
            db_acc = jnp.zeros((l, n), F32)
            dc_acc = jnp.zeros((l, n), F32)
            md_sum = jnp.zeros((l, l), F32)
            cb_t = _dot(bv, cv, NT)
            cv_t = cv.T
            for r in range(r_):
                a_col = _pick_lane(acs_c, r)
                a_row = _pick_row(acs_r, r)
                dt_col = _pick_lane(dtcv, r)
                dsk_h = _pick_lane(dskv, r)
                xv = x_ref[r, rows, :]
                dyv = dy_ref[r, rows, :]
                a_b = jnp.broadcast_to(a_col, (l, l))
                dt_b = jnp.broadcast_to(dt_col, (l, p))
                xd = xv * dt_b
                decay = jnp.where(isx <= il, jnp.exp(jnp.minimum(a_b - a_row, 0.0)), 0.0)
                decay_t = jnp.where(isx >= il, jnp.exp(jnp.minimum(a_row - a_b, 0.0)), 0.0)
                dhn = dstate_ref[r]
                hc = st_ref[r, cc]
                a_end = a_col[l - 1:l, :]
                ea_b = jnp.exp(a_b)
                dte_b = jnp.exp(a_end - a_b)

                dx_state = dte_b[:, :p] * _dot(bv, dhn)
                dxd = _dot(cb_t * decay_t, dyv) + dx_state
                md = decay * _dot(dyv, xd, NT)
                md_sum = md_sum + md
                dc_acc = dc_acc + ea_b * _dot(dyv, hc, NT)
                db_acc = db_acc + dte_b * _dot(xd, dhn, NT)
                dstate_ref[r] = jnp.exp(a_end) * dhn + _dot(cv_t, dyv * ea_b[:, :p])

                yoff = ea_b[:, :p] * _dot(cv, hc)
                xdx = xd * dx_state
                vec = jnp.sum(dyv * yoff - xdx, axis=1, keepdims=True)
                total = lambda t: jnp.sum(jnp.sum(t, axis=0, keepdims=True), axis=1, keepdims=True)
                end_term = total(xdx) + jnp.exp(a_end) * total(hc * dhn)
                zmat = _split_dot(md * cb, t1, True, 2)
                span = jnp.sum(jnp.where(isx < il, zmat, 0.0), axis=1, keepdims=True)
                rc = _split_dot(jnp.broadcast_to(vec, (l, 128)), t1, True, 2)[:, :1]
                dda = span + rc + end_term
                dda_all = jnp.where(lane == r, dda, dda_all)
                dxx_all = jnp.where(lane == r, jnp.sum(dxd * xv, axis=1, keepdims=True), dxx_all)
                dx_ref[r, rows, :] = dxd * dt_b + dsk_h * dyv
                dah_all = jnp.where(lane1 == r, jnp.sum(dda * dt_col, axis=0, keepdims=True), dah_all)
                ddsk_all = jnp.where(lane1 == r, total(dyv * xv), ddsk_all)
            dda_ref[rows, :] = dda_all
            dxx_ref[rows, :] = dxx_all
            db_ref[rows, :] = db_acc + _dot(md_sum, cv, TN)
            dc_ref[rows, :] = dc_acc + _dot(md_sum, bv)
            dah_ref[...] += dah_all
            ddsk_ref[...] += ddsk_all

    rev = lambda c: ncb - 1 - c
    xspec = pl.BlockSpec((r_, lb, p), lambda g, c: (g, rev(c), 0))
    cspec = pl.BlockSpec((None, lb, r_), lambda g, c: (g, rev(c), 0))
    hspec = pl.BlockSpec((None, 1, r_), lambda g, c: (g, 0, 0))
    return pl.pallas_call(
        body, name="ssd_bwd", grid=(g_, ncb),
        in_specs=[xspec, cspec, cspec,
                  pl.BlockSpec((None, r_, lb), lambda g, c: (g, 0, rev(c))),
                  hspec,
                  pl.BlockSpec((lb, n), lambda g, c: (rev(c), boff + g)),
                  pl.BlockSpec((lb, n), lambda g, c: (rev(c), coff + g)),
                  pl.BlockSpec((r_, cps, n, p), lambda g, c: (g, rev(c), 0, 0)),
                  xspec],
        out_specs=[xspec, cspec, cspec,
                   pl.BlockSpec((lb, n), lambda g, c: (rev(c), g)),
                   pl.BlockSpec((lb, n), lambda g, c: (rev(c), g)),
                   hspec, hspec],
        out_shape=[jax.ShapeDtypeStruct((hh, s, p), F32),
                   jax.ShapeDtypeStruct((g_, s, r_), F32),
                   jax.ShapeDtypeStruct((g_, s, r_), F32),
                   jax.ShapeDtypeStruct((s, g_ * n), F32),
                   jax.ShapeDtypeStruct((s, g_ * n), F32),
                   jax.ShapeDtypeStruct((g_, 1, r_), F32),
                   jax.ShapeDtypeStruct((g_, 1, r_), F32)],
        scratch_shapes=[pltpu.VMEM((r_, n, p), F32)],
        compiler_params=_cparams(("parallel", "arbitrary")),
    )(xh, dtc, dac, dar, dsk, xbc, xbc, st, dy)


CONV_TC = 256
CONV_RC = 128


def _conv_taps(x_ref, head_ref, rc):
    base = CONV_PAD - (CONV_K - 1)
    head_ref[pl.ds(0, CONV_PAD), :] = jnp.zeros((CONV_PAD, head_ref.shape[1]), F32)
    head_ref[pl.ds(CONV_PAD, rc), :] = x_ref[pl.ds(0, rc), :]

    def tap(t0, kk):
        if t0 == 0:
            return head_ref[pl.ds(base + kk, rc), :]
        return x_ref[pl.ds(t0 - (CONV_K - 1) + kk, rc), :]
    return tap


def _conv_fwd(x, w, b):
    s, ch = x.shape
    rc = min(CONV_RC, s)

    def body(x_ref, w_ref, b_ref, pre_ref, act_ref, head_ref):
        wv = w_ref[...]
        tap = _conv_taps(x_ref, head_ref, rc)
        for t0 in range(0, s, rc):
            acc = jnp.broadcast_to(b_ref[...], (rc, CONV_TC))
            for kk in range(CONV_K):
                acc = acc + wv[kk:kk + 1, :] * tap(t0, kk)
            pre_ref[pl.ds(t0, rc), :] = acc
            act_ref[pl.ds(t0, rc), :] = _silu(acc)

    col = pl.BlockSpec((s, CONV_TC), lambda j: (0, j))
    shp = jax.ShapeDtypeStruct((s, ch), F32)
    return pl.pallas_call(
        body, name="conv_fwd", grid=(ch // CONV_TC,),
        in_specs=[col, pl.BlockSpec((CONV_K, CONV_TC), lambda j: (0, j)),
                  pl.BlockSpec((1, CONV_TC), lambda j: (0, j))],
        out_specs=[col, col],
        out_shape=[shp, shp],
        scratch_shapes=[pltpu.VMEM((CONV_PAD + rc, CONV_TC), F32)],
        compiler_params=_cparams(("parallel",)),
    )(x, w, b)


def _conv_bwd(x, pre, dact, w):
    s, ch = x.shape
    rc = min(CONV_RC, s)

    def body(x_ref, pre_ref, da_ref, w_ref, dx_ref, dw_ref, db_ref, dpre_ref, head_ref):
        wv = w_ref[...]
        tap = _conv_taps(x_ref, head_ref, rc)
        for t0 in range(0, s, rc):
            dpre_ref[pl.ds(t0, rc), :] = da_ref[pl.ds(t0, rc), :] * _dsilu(pre_ref[pl.ds(t0, rc), :])
        dpre_ref[pl.ds(s, CONV_PAD), :] = jnp.zeros((CONV_PAD, CONV_TC), F32)
        dws = [jnp.zeros((1, CONV_TC), F32) for _ in range(CONV_K)]
        dbs = jnp.zeros((1, CONV_TC), F32)
        for t0 in range(0, s, rc):
            acc = jnp.zeros((rc, CONV_TC), F32)
            dp = dpre_ref[pl.ds(t0, rc), :]
            for kk in range(CONV_K):
                acc = acc + wv[kk:kk + 1, :] * dpre_ref[pl.ds(t0 + CONV_K - 1 - kk, rc), :]
                dws[kk] = dws[kk] + jnp.sum(dp * tap(t0, kk), axis=0, keepdims=True)
            dbs = dbs + jnp.sum(dp, axis=0, keepdims=True)
            dx_ref[pl.ds(t0, rc), :] = acc.astype(dx_ref.dtype)
        for kk in range(CONV_K):
            dw_ref[kk:kk + 1, :] = dws[kk]
        db_ref[...] = dbs

    col = pl.BlockSpec((s, CONV_TC), lambda j: (0, j))
    return pl.pallas_call(
        body, name="conv_bwd", grid=(ch // CONV_TC,),
        in_specs=[col, col, col, pl.BlockSpec((CONV_K, CONV_TC), lambda j: (0, j))],
        out_specs=[col, pl.BlockSpec((CONV_K, CONV_TC), lambda j: (0, j)),
                   pl.BlockSpec((1, CONV_TC), lambda j: (0, j))],
        out_shape=[jax.ShapeDtypeStruct((s, ch), CDT),
                   jax.ShapeDtypeStruct((CONV_K, ch), F32),
                   jax.ShapeDtypeStruct((1, ch), F32)],
        scratch_shapes=[pltpu.VMEM((s + CONV_PAD, CONV_TC), F32), pltpu.VMEM((CONV_PAD + rc, CONV_TC), F32)],
        compiler_params=_cparams(("parallel",)),
    )(x, pre, dact, w)


MEM_TS = 512


def _mem_fwd(mq, kv):
    s = mq.shape[0]
    m = kv.shape[0]
    ts = min(MEM_TS, s)
    scale = MEM_DH ** -0.5

    def body(q_ref, k_ref, v_ref, o_ref):
        sc = _dot(q_ref[...], k_ref[...], NT) * scale
        e = jnp.exp(sc - jnp.max(sc, axis=1, keepdims=True))
        pr = e / jnp.sum(e, axis=1, keepdims=True)
        o_ref[...] = _dot(pr, v_ref[...]).astype(o_ref.dtype)

    return pl.pallas_call(
        body, name="mem_fwd", grid=(MEM_H, s // ts),
        in_specs=[pl.BlockSpec((ts, MEM_DH), lambda a, i: (i, a)),
                  pl.BlockSpec((m, MEM_DH), lambda a, i: (0, a)),
                  pl.BlockSpec((m, MEM_DH), lambda a, i: (0, MEM_H + a))],
        out_specs=pl.BlockSpec((ts, MEM_DH), lambda a, i: (i, a)),
        out_shape=jax.ShapeDtypeStruct((s, MEM_H * MEM_DH), CDT),
        compiler_params=_cparams(("parallel", "arbitrary")),
    )(mq, kv, kv)


def _mem_bwd(mq, kv, do):
    s = mq.shape[0]
    m = kv.shape[0]
    ts = min(MEM_TS, s)
    scale = MEM_DH ** -0.5

    def body(q_ref, k_ref, v_ref, do_ref, dq_ref, dk_ref, dv_ref):
        i = pl.program_id(1)

        @pl.when(i == 0)
        def _():
            dk_ref[...] = jnp.zeros_like(dk_ref)
            dv_ref[...] = jnp.zeros_like(dv_ref)

        qv, kb, vb, dov = q_ref[...], k_ref[...], v_ref[...], do_ref[...]
        sc = _dot(qv, kb, NT) * scale
        e = jnp.exp(sc - jnp.max(sc, axis=1, keepdims=True))
        pr = e / jnp.sum(e, axis=1, keepdims=True)
        dp = _dot(dov, vb, NT)
        ds = pr * (dp - jnp.sum(dp * pr, axis=1, keepdims=True)) * scale
        dq_ref[...] = _dot(ds, kb).astype(dq_ref.dtype)
        dk_ref[...] += _dot(ds, qv, TN)
        dv_ref[...] += _dot(pr, dov, TN)

    tile = pl.BlockSpec((ts, MEM_DH), lambda a, i: (i, a))
    kvo = pl.BlockSpec((m, MEM_DH), lambda a, i: (0, a))
    return pl.pallas_call(
        body, name="mem_bwd", grid=(MEM_H, s // ts),
        in_specs=[tile, kvo, pl.BlockSpec((m, MEM_DH), lambda a, i: (0, MEM_H + a)), tile],
        out_specs=[tile, kvo, kvo],
        out_shape=[jax.ShapeDtypeStruct((s, MEM_H * MEM_DH), CDT),
                   jax.ShapeDtypeStruct((m, MEM_H * MEM_DH), F32),
                   jax.ShapeDtypeStruct((m, MEM_H * MEM_DH), F32)],
        compiler_params=_cparams(("parallel", "arbitrary")),
    )(mq, kv, kv, do)


def _heads(t, nh, dh):
    return t.reshape(t.shape[0], nh, dh).transpose(1, 0, 2)


def _unheads(t):
    return t.transpose(1, 0, 2).reshape(t.shape[1], t.shape[0] * t.shape[2])


def _group_cols(t):
    return t.reshape(t.shape[0], SSD_G, SSD_R).transpose(1, 0, 2)


def _pad_cols(t, width):
    return jnp.pad(t, ((0, 0), (0, width - t.shape[1])))


def _full_weight(name, gathered):
    if name in COL_SHARDED:
        return gathered.transpose(1, 0, 2).reshape(gathered.shape[1], N_DEV * gathered.shape[2])
    return gathered.reshape(N_DEV * gathered.shape[1], gathered.shape[2])


def _grad_payload(name, g):
    if name in COL_SHARDED:
        return g.reshape(g.shape[0], N_DEV, g.shape[1] // N_DEV).transpose(1, 0, 2)
    return g.reshape(N_DEV, g.shape[0] // N_DEV, g.shape[1])


def _local_step(x, mem, tgt, p, wt, shards=None):
    s, d = x.shape
    wt = dict(wt)
    c1, c2, c3, c4, c5 = 3 * d, 3 * d + SSD_INNER, 3 * d + SSD_INNER + CONV_DIM, \
        3 * d + SSD_INNER + CONV_DIM + SSD_H, 3 * d + SSD_INNER + CONV_DIM + SSD_H + d
    w_t = wt["w_in"]
    w_tail = w_t[c4:]
    w_dt = jnp.pad(w_t[c3:c4], ((0, DT_PAD - SSD_H), (0, 0)))
    seg_name = ["qkv", "z", "xbc", "mq", "gl"]
    seg_dtype = [CDT, F32, F32, CDT, F32]
    seg_src = [w_t, w_t, w_t, w_tail, w_tail]
    seg_off = [0, c1, c2, 0, d]
    seg_n = [c1, c2 - c1, c3 - c2, d, 3 * d]

    u = _rows(lambda xv, g: _rms(xv, g), [x], [p["norm_mix_pre"]], [(d, CDT)], ts=512, name="f_norm_pre")[0]
    qkv, z, xbc_raw, mq, gl = [
        _mm(u, seg_src[i], tb=True, b_off=seg_off[i], n=seg_n[i], out_dtype=seg_dtype[i],
            name="f_in_" + seg_name[i]) for i in range(5)]
    dt_raw = _mm(u, w_dt, tb=True, name="f_in_dt")

    bias128 = _pad_cols(p["dt_bias"], DT_PAD)
    alog128 = _pad_cols(p["a_log"], DT_PAD)

    def dt_fn(dtr, bias, alog):
        dt = _softplus(dtr + bias)
        return dt, dt * (-jnp.exp(alog))

    dt128, da128 = _rows(dt_fn, [dt_raw], [bias128, alog128], [(DT_PAD, F32), (DT_PAD, F32)],
                         ts=512, name="f_dt")
    dtc = _group_cols(dt128[:, :SSD_H])
    dac = _group_cols(da128[:, :SSD_H])
    dar = dac.transpose(0, 2, 1)
    dsk = p["d_skip"].reshape(SSD_G, 1, SSD_R)

    conv_w, conv_b = p["conv_w"], p["conv_b"]
    pre, xbc = _conv_fwd(xbc_raw, conv_w, conv_b)
    xh = _heads(xbc[:, :SSD_INNER], SSD_H, SSD_P)
    y_h, st = _ssd_fwd_g(xh, dtc, dac, dar, dsk, xbc)
    y_core = _unheads(y_h)

    def group_norm_fwd(yv, zv, wn):
        y2 = yv * _silu(zv)
        gw = SSD_INNER // SSD_G
        outs = []
        for gi in range(SSD_G):
            seg = y2[:, gi * gw:(gi + 1) * gw]
            outs.append(_rms(seg, wn[:, gi * gw:(gi + 1) * gw]))
        return jnp.concatenate(outs, axis=1)

    y_ssd = _rows(group_norm_fwd, [y_core, z], [p["ssd_norm"]], [(SSD_INNER, CDT)], ts=256, name="f_ssd_post")[0]

    y_sb, lt_h, *late = _sb_fwd_pairs(qkv, tuple(shards) if shards is not None else ())
    for n, gth in zip(LATE_W, late):
        wt[n] = _full_weight(n, gth)

    mu = _rows(lambda mv, g: _rms(mv, g), [mem], [p["norm_mem"]], [(d, CDT)], ts=256, name="f_norm_mem")[0]
    kv = _mm(mu, wt["w_mem_kv"], out_dtype=CDT, name="f_mem_kv")
    y_mem = _mem_fwd(mq, kv)

    p_sb = _mm(y_sb, wt["w_sb_out"], name="f_sb_out")
    p_ssd = _mm(y_ssd, wt["w_ssd_out"], name="f_ssd_out")
    p_mem = _mm(y_mem, wt["w_mem_out"], name="f_mem_out")

    def merge_fn(glv, a, b, c):
        return (_sigmoid(glv[:, :d]) * a + _sigmoid(glv[:, d:2 * d]) * b + _sigmoid(glv[:, 2 * d:]) * c)

    merged = _rows(merge_fn, [gl, p_sb, p_ssd, p_mem], [], [(d, CDT)], ts=256, name="f_merge")[0]
    mix = _mm(merged, wt["w_o"], name="f_w_o")

    def mid_fn(xv, mixv, g_post, g_pre):
        h1 = xv + _rms(mixv, g_post)
        return h1, _rms(h1, g_pre)

    h1, u2 = _rows(mid_fn, [x, mix], [p["norm_mix_post"], p["norm_mlp_pre"]], [(d, F32), (d, CDT)],
                   ts=512, name="f_mid")
    a1, act = _mm(u2, wt["w_up"], name="f_up",
                  epilogue=(lambda pv: (pv, jnp.square(jnp.maximum(pv, 0.0))), [], [F32, CDT]))
    ff = _mm(act, wt["w_down"], name="f_down")

    def loss_fn(h1v, ffv, tv, g):
        diff = h1v + _rms(ffv, g) - tv
        tot = jnp.sum(jnp.sum(diff * diff, axis=1, keepdims=True), axis=0, keepdims=True)
        return diff * (1.0 / d), tot

    dh2, loss_acc = _rows(loss_fn, [h1, ff, tgt], [p["norm_mlp_post"]], [(d, F32)], [(1, 128)],
                          ts=512, name="f_loss")
    loss = loss_acc[:, :1] * (0.5 / d)

    sg = {}

    def b_post(ffv, dyv, g):
        dx, dg = _rms_bwd(ffv, g, dyv)
        return dx, dg

    d_ff, sg["norm_mlp_post"] = _rows(b_post, [ff, dh2], [p["norm_mlp_post"]], [(d, CDT)], [(1, d)],
                                      ts=512, name="b_norm_mlp_post")
    da1 = _mm(d_ff, wt["w_down"], tb=True, name="b_down_x",
              epilogue=(lambda pv, a: (pv * 2.0 * jnp.maximum(a, 0.0),), [a1], [CDT]))[0]
    gw = {"w_down": _mm(act, d_ff, ta=True, name="b_down_w")}
    du2 = _mm(da1, wt["w_up"], tb=True, name="b_up_x")
    gw["w_up"] = _mm(u2, da1, ta=True, name="b_up_w")

    def b_mid(h1v, du2v, dh2v, mixv, g_pre, g_post):
        dxa, dga = _rms_bwd(h1v, g_pre, du2v)
        dh1 = dh2v + dxa
        dmix, dgb = _rms_bwd(mixv, g_post, dh1)
        return dh1, dmix, dga, dgb

    dh1, dmix, sg["norm_mlp_pre"], sg["norm_mix_post"] = _rows(
        b_mid, [h1, du2, dh2, mix], [p["norm_mlp_pre"], p["norm_mix_post"]],
        [(d, F32), (d, CDT)], [(1, d), (1, d)], ts=256, name="b_mid")
    dmerged = _mm(dmix, wt["w_o"], tb=True, name="b_w_o_x")
    gw["w_o"] = _mm(merged, dmix, ta=True, name="b_w_o_w")

    def b_merge(dm, glv, a, b, c):
        outs, dgl = [], []
        for i, br in enumerate((a, b, c)):
            gt = _sigmoid(glv[:, i * d:(i + 1) * d])
            outs.append(gt * dm)
            dgl.append(dm * br * gt * (1.0 - gt))
        return outs[0], outs[1], outs[2], jnp.concatenate(dgl, axis=1)

    dp_sb, dp_ssd, dp_mem, dgl = _rows(b_merge, [dmerged, gl, p_sb, p_ssd, p_mem], [],
                                       [(d, CDT), (d, CDT), (d, CDT), (3 * d, CDT)], ts=256, name="b_merge")
    dy_sb = _mm(dp_sb, wt["w_sb_out"], tb=True, name="b_sb_out_x")
    gw["w_sb_out"] = _mm(y_sb, dp_sb, ta=True, name="b_sb_out_w")
    dy_ssd = _mm(dp_ssd, wt["w_ssd_out"], tb=True, name="b_ssd_out_x")
    gw["w_ssd_out"] = _mm(y_ssd, dp_ssd, ta=True, name="b_ssd_out_w")
    dy_mem = _mm(dp_mem, wt["w_mem_out"], tb=True, out_dtype=CDT, name="b_mem_out_x")
    gw["w_mem_out"] = _mm(y_mem, dp_mem, ta=True, name="b_mem_out_w")

    dmq, dk_m, dv_m = _mem_bwd(mq, kv, dy_mem)
    dkv = jnp.concatenate([dk_m, dv_m], axis=1).astype(CDT)
    gw["w_mem_kv"] = _mm(mu, dkv, ta=True, name="b_mem_kv_w")
    dmu = _mm(dkv, wt["w_mem_kv"], tb=True, name="b_mem_kv_x")
    sg["norm_mem"] = _rows(lambda mv, dv, g: _rms_bwd(mv, g, dv)[1], [mem, dmu], [p["norm_mem"]], [], [(1, d)],
                           ts=256, name="b_norm_mem")[0]

    payloads = tuple(_grad_payload(n, gw[n]) for n in LATE_W) if shards is not None else ()
    dq, dk, dv, *received = _sb_bwd_pairs(qkv, lt_h, dy_sb, payloads)
    dqkv = jnp.concatenate([dq, dk, dv], axis=1).astype(CDT)

    def group_norm_bwd(dyo, yv, zv, wn):
        sz = _silu(zv)
        y2 = yv * sz
        gw_ = SSD_INNER // SSD_G
        dy2, dwn = [], []
        for gi in range(SSD_G):
            sl = slice(gi * gw_, (gi + 1) * gw_)
            dxs, dgs = _rms_bwd(y2[:, sl], wn[:, sl], dyo[:, sl])
            dy2.append(dxs)
            dwn.append(dgs)
        dy2 = jnp.concatenate(dy2, axis=1)
        return dy2 * sz, dy2 * yv * _dsilu(zv), jnp.concatenate(dwn, axis=1)

    dy_core, dz, sg["ssd_norm"] = _rows(group_norm_bwd, [dy_ssd, y_core, z], [p["ssd_norm"]],
                                        [(SSD_INNER, F32), (SSD_INNER, CDT)], [(1, SSD_INNER)],
                                        ts=256, name="b_ssd_post")
    dxh, dda, dxx, d_b, d_c, dah, ddsk = _ssd_bwd_g(xh, dtc, dac, dar, dsk, xbc, st, _heads(dy_core, SSD_H, SSD_P))
    sg["d_skip"] = ddsk.reshape(1, SSD_H)
    sg["a_log"] = dah.reshape(1, SSD_H) * (-jnp.exp(p["a_log"]))

    def b_dt(ddav, dxxv, dtr, bias, alog):
        ddt = ddav * (-jnp.exp(alog)) + dxxv
        draw = ddt * _sigmoid(dtr + bias)
        return draw, jnp.sum(draw, axis=0, keepdims=True)

    ungroup = lambda t: _pad_cols(t.transpose(1, 0, 2).reshape(s, SSD_H), DT_PAD)
    ddt_raw, dbias128 = _rows(b_dt, [ungroup(dda), ungroup(dxx), dt_raw], [bias128, alog128],
                              [(DT_PAD, CDT)], [(1, DT_PAD)], ts=512, name="b_dt")
    sg["dt_bias"] = dbias128[:, :SSD_H]

    dxbc = jnp.concatenate([_unheads(dxh), d_b, d_c], axis=1)
    dxbc_raw, sg["conv_w"], sg["conv_b"] = _conv_bwd(xbc_raw, pre, dxbc, conv_w)

    dseg = [dqkv, dz, dxbc_raw, dmq, dgl]
    dw_bufs = [lax.empty((c3, d), F32), lax.empty((c5 - c4 + 3 * d, d), F32)]
    for i in range(5):
        bi = 0 if i < 3 else 1
        dw_bufs[bi] = _mm(dseg[i], u, ta=True, into=dw_bufs[bi], into_off=seg_off[i], into_rows=True,
                          name="b_in_w_" + seg_name[i])
    dw_dt = _mm(ddt_raw, u, ta=True, name="b_in_w_dt")
    def du(i, hosted=None):
        return _mm(dseg[i], seg_src[i], b_koff=seg_off[i], name="b_in_x_" + seg_name[i], hosted=hosted)

    if shards is None:
        dus = [du(i) for i in range(5)]
        gw["w_in"] = jnp.concatenate([dw_bufs[0], dw_dt[:SSD_H], dw_bufs[1]], axis=0).T
    else:
        tail_rows, half = dw_bufs[1].shape[0], c3 // 2
        du0, recv_a = du(0, _pair_swap([(dw_bufs[0], 0, half)]))
        du2, recv_b, recv_dt = du(2, _pair_swap([(dw_bufs[0], half, half), (dw_dt, 0, DT_PAD)]))
        du3, recv_tail = du(3, _pair_swap([(dw_bufs[1], 0, tail_rows)]))
        natural = lax.empty((c4 + tail_rows, d), CDT)
        natural = _add_cast_into(dw_bufs[0], recv_a, natural, 0, half, "sum_w_in_head_a")
        natural = _add_cast_into(dw_bufs[0], recv_b, natural, half, half, "sum_w_in_head_b", a_off=half)
        natural = _add_cast_into(dw_dt, recv_dt, natural, c3, SSD_H, "sum_w_in_dt")
        natural = _add_cast_into(dw_bufs[1], recv_tail, natural, c4, tail_rows, "sum_w_in_tail")
        shard, qd = natural.shape[0] // N_DEV, d // 4
        du1, win_0 = du(1, _chip_scatter_windows(natural, shard, [(0, qd)]))
        du4, win_1 = du(4, _chip_scatter_windows(natural, shard, [(qd, qd)]))
        last_host = _chip_scatter_windows(natural, shard, [(2 * qd, qd), (3 * qd, qd)])
        dus = [du0, du1, du2, du3, du4]
    dus.append(_mm(ddt_raw, w_dt, name="b_in_x_dt"))

    def b_pre(xv, dh1v, d0, d1, d2, d3, d4, d5, g):
        dx, dg = _rms_bwd(xv, g, d0 + d1 + d2 + d3 + d4 + d5)
        return dh1v + dx, dg

    grad_x, sg["norm_mix_pre"], *wins = _rows(b_pre, [x, dh1] + dus, [p["norm_mix_pre"]], [(d, F32)], [(1, d)],
                                              ts=256, name="b_norm_pre",
                                              hosted=last_host if shards is not None else None)
    if shards is not None:
        gw["w_in"] = [win_0, win_1] + wins
    return loss, grad_x, gw, sg, (received if shards is not None else None)


HBM = pl.BlockSpec(memory_space=pltpu.HBM)
MESH = pl.DeviceIdType.MESH


def _me_and_peers():
    x, y, c = lax.axis_index("x"), lax.axis_index("y"), lax.axis_index("c")
    me = 4 * x + 2 * y + c
    peers = [(x, y, 1 - c), (1 - x, y, c), (x, 1 - y, c), (1 - x, 1 - y, c),
             (1 - x, y, 1 - c), (x, 1 - y, 1 - c), (1 - x, 1 - y, 1 - c)]
    return me, peers


def _peer_index(peer):
    return 4 * peer[0] + 2 * peer[1] + peer[2]


def _exchange_copies(ins, outs, send_sems, recv_sems, local_sems, scatter):
    me, peers = _me_and_peers()
    copies = []
    for a in range(len(ins)):
        own = ins[a].at[me] if scatter else ins[a]
        copies.append(pltpu.make_async_copy(own, outs[a].at[me], local_sems.at[a]))
        for kk, peer in enumerate(peers):
            src = ins[a].at[_peer_index(peer)] if scatter else ins[a]
            copies.append(pltpu.make_async_remote_copy(
                src_ref=src, dst_ref=outs[a].at[me],
                send_sem=send_sems.at[a, kk], recv_sem=recv_sems.at[a, kk],
                device_id=peer, device_id_type=MESH))
    return copies


def _exchange_shapes(ins, scatter):
    return [jax.ShapeDtypeStruct(t.shape if scatter else (N_DEV,) + t.shape, t.dtype) for t in ins]


def _exchange_sems(n):
    return [pltpu.SemaphoreType.DMA((n, N_DEV - 1)), pltpu.SemaphoreType.DMA((n, N_DEV - 1)),
            pltpu.SemaphoreType.DMA((n,))]


def _gather_two_level(shards, name):
    n = len(shards)

    def body(*refs):
        ins, outs = refs[:n], refs[n:2 * n]
        send_sems, recv_sems, local_sems = refs[2 * n:]
        x, y, c = lax.axis_index("x"), lax.axis_index("y"), lax.axis_index("c")
        me, sib = (x, y, c), (x, y, 1 - c)
        chips = [(1 - x, y), (x, 1 - y), (1 - x, 1 - y)]

        def copy(a, k, block, to, src=None):
            slot = outs[a].at[_peer_index(block)]
            return pltpu.make_async_remote_copy(
                src_ref=slot if src is None else src, dst_ref=slot,
                send_sem=send_sems.at[a, k], recv_sem=recv_sems.at[a, k], device_id=to, device_id_type=MESH)

        own = [pltpu.make_async_copy(ins[a], outs[a].at[_peer_index(me)], local_sems.at[a]) for a in range(n)]
        first = []
        for a in range(n):
            first.append(copy(a, 0, me, sib, src=ins[a]))
            first += [copy(a, 1 + j, me, (*chip, c), src=ins[a]) for j, chip in enumerate(chips)]
        for cp in own + first:
            cp.start()
        passed = []
        for j, chip in enumerate(chips):
            for a in range(n):
                copy(a, 1 + j, (*chip, c), me).wait_recv()
                fwd = copy(a, 4 + j, (*chip, c), sib)
                fwd.start()
                passed.append(fwd)
        for a in range(n):
            copy(a, 0, sib, me).wait_recv()
            for j, chip in enumerate(chips):
                copy(a, 4 + j, (*chip, 1 - c), me).wait_recv()
        for cp in first + passed:
            cp.wait_send()
        for cp in own:
            cp.wait()

    return pl.pallas_call(
        body, name=name,
        in_specs=[HBM] * n, out_specs=[HBM] * n,
        out_shape=_exchange_shapes(shards, False),
        scratch_shapes=_exchange_sems(n),
        compiler_params=pltpu.CompilerParams(has_side_effects=True),
    )(*shards)


class _Hosted:
    def __init__(self, ins, shapes, sems, copies):
        self.ins, self.shapes, self.sems, self.copies = ins, shapes, sems, copies


def _pair_swap(pieces):
    n = len(pieces)

    def copies(in_refs, out_refs, sems):
        sib = (lax.axis_index("x"), lax.axis_index("y"), 1 - lax.axis_index("c"))
        return [pltpu.make_async_remote_copy(
            src_ref=in_refs[i].at[pl.ds(r0, nr)], dst_ref=out_refs[i], send_sem=sems[0].at[i],
            recv_sem=sems[1].at[i], device_id=sib, device_id_type=MESH) for i, (_, r0, nr) in enumerate(pieces)]

    return _Hosted([t for t, _, _ in pieces],
                   [jax.ShapeDtypeStruct((nr, t.shape[1]), t.dtype) for t, _, nr in pieces],
                   [pltpu.SemaphoreType.DMA((n,)), pltpu.SemaphoreType.DMA((n,))], copies)


def _add_cast_into(a, b, into, row_off, rows, name, a_off=0):
    w = a.shape[1]
    tr = _pick(rows, (512, 256, 128, rows))
    nt = rows // tr
    assert rows % tr == 0 and row_off % ROW_ALIGN == 0 and tr % ROW_ALIGN == 0 and a_off % tr == 0
    spec = pl.BlockSpec((tr, w), lambda i: (i, 0))

    def body(a_ref, b_ref, into_ref, o_ref, slots, sems):
        i = pl.program_id(0)
        slot = i % 2

        def out_copy(step, s):
            dst = o_ref.at[pl.ds(pl.multiple_of(row_off + step * tr, ROW_ALIGN), tr)]
            return pltpu.make_async_copy(slots.at[s], dst, sems.at[s])

        @pl.when(i >= 2)
        def _():
            out_copy(i - 2, slot).wait()

        slots[slot] = (a_ref[...] + b_ref[...]).astype(slots.dtype)
        out_copy(i, slot).start()

        @pl.when(i == nt - 1)
        def _():
            out_copy(i, slot).wait()
            if nt > 1:
                out_copy(i - 1, 1 - slot).wait()

    return pl.pallas_call(
        body, name=name, grid=(nt,),
        in_specs=[pl.BlockSpec((tr, w), lambda i: (i + a_off // tr, 0)), spec, pl.BlockSpec(memory_space=pl.ANY)],
        out_specs=pl.BlockSpec(memory_space=pl.ANY),
        out_shape=jax.ShapeDtypeStruct(into.shape, into.dtype),
        scratch_shapes=[pltpu.VMEM((2, tr, w), into.dtype), pltpu.SemaphoreType.DMA((2,))],
        input_output_aliases={2: 0},
        compiler_params=_cparams(("arbitrary",)),
    )(a, b, into)


N_CHIP = 4


def _adamw_math(g, w, m, v):
    m2 = ADAM_B1 * m + (1.0 - ADAM_B1) * g
    v2 = ADAM_B2 * v + (1.0 - ADAM_B2) * jnp.square(g)
    m_hat = m2 / (1.0 - ADAM_B1 ** ADAM_STEP)
    v_hat = v2 / (1.0 - ADAM_B2 ** ADAM_STEP)
    delta = -ADAM_LR * (m_hat / (jnp.sqrt(v_hat) + ADAM_EPS) + ADAM_WD * w)
    return delta, m2, v2


def _adamw_reduce(parts, w, m, v, name):
    r, c = w.shape
    nparts = parts.shape[0]
    tr = _pick(r, (128, 64, 32, 16, 8))

    def body(p_ref, w_ref, m_ref, v_ref, g_ref, d_ref, m2_ref, v2_ref):
        g = p_ref[0].astype(F32)
        for i in range(1, nparts):
            g = g + p_ref[i].astype(F32)
        delta, m2, v2 = _adamw_math(g, w_ref[...], m_ref[...], v_ref[...])
        g_ref[...] = g
        d_ref[...] = delta
        m2_ref[...] = m2
        v2_ref[...] = v2

    tile = pl.BlockSpec((tr, c), lambda i: (i, 0))
    shp = jax.ShapeDtypeStruct((r, c), F32)
    return pl.pallas_call(
        body, name=name, grid=(r // tr,),
        in_specs=[pl.BlockSpec((nparts, tr, c), lambda i: (0, i, 0)), tile, tile, tile],
        out_specs=[tile] * 4, out_shape=[shp] * 4,
        compiler_params=_cparams(("parallel",)),
    )(parts, w, m, v)


ROW_ALIGN = 16


def _window(shard):
    lead = max((j * shard) % ROW_ALIGN for j in range(N_DEV))
    return -(-(lead + shard) // ROW_ALIGN) * ROW_ALIGN


def _chip_scatter_windows(t, shard, ranges):
    win = _window(shard)
    nr = len(ranges)
    assert all((j * shard // ROW_ALIGN) * ROW_ALIGN + win <= t.shape[0] for j in range(N_DEV))

    def copies(in_refs, out_refs, sems):
        (t_ref,), (send_sems, recv_sems, local_sems) = in_refs, sems
        x, y, c = lax.axis_index("x"), lax.axis_index("y"), lax.axis_index("c")
        mine = 2 * x + y
        res = []
        for k, (col0, cols) in enumerate(ranges):
            def window(q):
                a0 = pl.multiple_of(((2 * q + c) * shard // ROW_ALIGN) * ROW_ALIGN, ROW_ALIGN)
                return t_ref.at[pl.ds(a0, win), pl.ds(col0, cols)]

            res.append(pltpu.make_async_copy(window(mine), out_refs[k].at[mine], local_sems.at[k]))
            for j, (px, py) in enumerate([(1 - x, y), (x, 1 - y), (1 - x, 1 - y)]):
                res.append(pltpu.make_async_remote_copy(
                    src_ref=window(2 * px + py), dst_ref=out_refs[k].at[mine], send_sem=send_sems.at[k, j],
                    recv_sem=recv_sems.at[k, j], device_id=(px, py, c), device_id_type=MESH))
        return res

    return _Hosted([t], [jax.ShapeDtypeStruct((N_CHIP, win, cols), t.dtype) for _, cols in ranges],
                   [pltpu.SemaphoreType.DMA((nr, N_CHIP - 1)), pltpu.SemaphoreType.DMA((nr, N_CHIP - 1)),
                    pltpu.SemaphoreType.DMA((nr,))], copies)


ADAMW_TC = 256


def _adamw_windows(parts, w, m, v, name):
    r, c = w.shape
    na = len(parts)
    nparts, win, cpart = parts[0].shape
    tc = min(ADAMW_TC, cpart)
    per = cpart // tc

    def body(*refs):
        p_refs = refs[:na]
        w_ref, m_ref, v_ref, g_ref, d_ref, m2_ref, v2_ref = refs[na:]
        me, _ = _me_and_peers()
        step = pl.program_id(0)
        gw_ = None
        for a, p_ref in enumerate(p_refs):
            tot = p_ref[0].astype(F32)
            for i in range(1, nparts):
                tot = tot + p_ref[i].astype(F32)
            gw_ = tot if gw_ is None else jnp.where(step // per == a, tot, gw_)
        for j in range(N_DEV):
            @pl.when(me == j)
            def _():
                lead = (j * r) % ROW_ALIGN
                g = (pltpu.roll(gw_, win - lead, 0) if lead else gw_)[:r]
                delta, m2, v2 = _adamw_math(g, w_ref[...], m_ref[...], v_ref[...])
                g_ref[...] = g
                d_ref[...] = delta
                m2_ref[...] = m2
                v2_ref[...] = v2

    tile = pl.BlockSpec((r, tc), lambda i: (0, i))
    shp = jax.ShapeDtypeStruct((r, c), F32)
    return pl.pallas_call(
        body, name=name, grid=(c // tc,),
        in_specs=[pl.BlockSpec((nparts, win, tc), lambda i, a=a: (0, 0, jnp.clip(i - a * per, 0, per - 1)))
                  for a in range(na)] + [tile, tile, tile],
        out_specs=[tile] * 4, out_shape=[shp] * 4,
        compiler_params=_cparams(("parallel",)),
    )(*parts, w, m, v)


SMALL_ROWS, SMALL_COLS = 16, 3072


def _small_step(sg, gcw, loss, ws, ms, vs):
    ns = len(sg)
    widths = [t.shape[1] for t in sg]
    kk_, ch = gcw.shape[1], gcw.shape[2]
    assert ns < SMALL_ROWS and max(widths) <= SMALL_COLS

    def reduce_body(*refs):
        g_refs = refs[:ns]
        gcw_ref, loss_ref, tot_ref, totc_ref = refs[ns:ns + 4]
        mine, buf, minec, bufc, send_sems, recv_sems = refs[ns + 4:]
        me, peers = _me_and_peers()

        mine[...] = jnp.zeros_like(mine)
        for i in range(ns):
            mine[i:i + 1, 0:widths[i]] = g_refs[i][...]
        mine[ns:ns + 1, 0:LANES] = jnp.broadcast_to(loss_ref[...], (1, LANES))
        minec[...] = gcw_ref[...]
        buf[me] = mine[...]
        bufc[me] = minec[...]
        copies = []
        for j, peer in enumerate(peers):
            copies.append(pltpu.make_async_remote_copy(
                src_ref=mine, dst_ref=buf.at[me], send_sem=send_sems.at[0, j], recv_sem=recv_sems.at[0, j],
                device_id=peer, device_id_type=MESH))
            copies.append(pltpu.make_async_remote_copy(
                src_ref=minec, dst_ref=bufc.at[me], send_sem=send_sems.at[1, j], recv_sem=recv_sems.at[1, j],
                device_id=peer, device_id_type=MESH))
        for cp in copies:
            cp.start()
        for cp in copies:
            cp.wait()
        tot = buf[0]
        totc = bufc[0]
        for i in range(1, N_DEV):
            tot = tot + buf[i]
            totc = totc + bufc[i]
        tot_ref[...] = tot
        totc_ref[...] = totc

    vm = pl.BlockSpec(memory_space=pltpu.VMEM)
    tot, totc = pl.pallas_call(
        reduce_body, name="small_reduce",
        in_specs=[vm] * (ns + 2), out_specs=[vm, vm],
        out_shape=[jax.ShapeDtypeStruct((SMALL_ROWS, SMALL_COLS), F32), jax.ShapeDtypeStruct((N_DEV, kk_, ch), F32)],
        scratch_shapes=[pltpu.VMEM((SMALL_ROWS, SMALL_COLS), F32), pltpu.VMEM((N_DEV, SMALL_ROWS, SMALL_COLS), F32),
                        pltpu.VMEM((N_DEV, kk_, ch), F32), pltpu.VMEM((N_DEV, N_DEV, kk_, ch), F32),
                        pltpu.SemaphoreType.DMA((2, N_DEV - 1)), pltpu.SemaphoreType.DMA((2, N_DEV - 1))],
        compiler_params=pltpu.CompilerParams(has_side_effects=True),
    )(*sg, gcw, loss)

    def adamw_body(*refs):
        tot_ref, totc_ref = refs[:2]
        w_refs, m_refs, v_refs = (refs[2 + i * (ns + 1):2 + (i + 1) * (ns + 1)] for i in range(3))
        outs = refs[3 * ns + 5:]
        loss_out = outs[0]
        go, do_, mo, vo = (outs[1 + i * (ns + 1):1 + (i + 1) * (ns + 1)] for i in range(4))
        me, _ = _me_and_peers()
        loss_out[...] = tot_ref[ns:ns + 1, 0:1]
        for i in range(ns + 1):
            g = tot_ref[i:i + 1, 0:widths[i]] if i < ns else totc_ref[me]
            delta, m2, v2 = _adamw_math(g, w_refs[i][...], m_refs[i][...], v_refs[i][...])
            go[i][...] = g
            do_[i][...] = delta
            mo[i][...] = m2
            vo[i][...] = v2

    shapes = [jax.ShapeDtypeStruct(t.shape, F32) for t in ws]
    res = pl.pallas_call(
        adamw_body, name="small_adamw",
        in_specs=[vm] * (3 * ns + 5), out_specs=[vm] * (4 * ns + 5),
        out_shape=[jax.ShapeDtypeStruct((1, 1), F32)] + shapes * 4,
    )(tot, totc, *ws, *ms, *vs)
    n1 = ns + 1
    return res[0], res[1:1 + n1], res[1 + n1:1 + 2 * n1], res[1 + 2 * n1:1 + 3 * n1], res[1 + 3 * n1:]


def _cast_shard(w, name):
    r = w.shape[0]
    return _rows(lambda t: t, [w], [], [(w.shape[1], CDT)], ts=_pick(r, (256, 128)), name=name)[0]


BIG = ["w_in", "w_mem_kv", "w_up", "w_sb_out", "w_ssd_out", "w_mem_out", "w_o", "w_down"]
LATE_W = BIG[1:]
COL_SHARDED = ("w_in", "w_mem_kv", "w_up")
SMALL = ["norm_mix_pre", "conv_b", "dt_bias", "a_log", "d_skip", "ssd_norm", "norm_mem",
         "norm_mix_post", "norm_mlp_pre", "norm_mlp_post"]
ALL_W = ["norm_mix_pre", "w_in", "conv_w", "conv_b", "dt_bias", "a_log", "d_skip", "ssd_norm", "norm_mem",
         "w_mem_kv", "w_sb_out", "w_ssd_out", "w_mem_out", "w_o", "norm_mix_post", "norm_mlp_pre", "w_up",
         "w_down", "norm_mlp_post"]
LANES = 128


def kernel(x, mem, norm_mix_pre, w_in, conv_w, conv_b, dt_bias, a_log, d_skip, ssd_norm, norm_mem, w_mem_kv, w_sb_out, w_ssd_out, w_mem_out, w_o, norm_mix_post, norm_mlp_pre, w_up, w_down, norm_mlp_post, loss_target, m_norm_mix_pre, m_w_in, m_conv_w, m_conv_b, m_dt_bias, m_a_log, m_d_skip, m_ssd_norm, m_norm_mem, m_w_mem_kv, m_w_sb_out, m_w_ssd_out, m_w_mem_out, m_w_o, m_norm_mix_post, m_norm_mlp_pre, m_w_up, m_w_down, m_norm_mlp_post, v_norm_mix_pre, v_w_in, v_conv_w, v_conv_b, v_dt_bias, v_a_log, v_d_skip, v_ssd_norm, v_norm_mem, v_w_mem_kv, v_w_sb_out, v_w_ssd_out, v_w_mem_out, v_w_o, v_norm_mix_post, v_norm_mlp_pre, v_w_up, v_w_down, v_norm_mlp_post):
    wd = dict(norm_mix_pre=norm_mix_pre, w_in=w_in, conv_w=conv_w, conv_b=conv_b, dt_bias=dt_bias, a_log=a_log,
              d_skip=d_skip, ssd_norm=ssd_norm, norm_mem=norm_mem, w_mem_kv=w_mem_kv, w_sb_out=w_sb_out,
              w_ssd_out=w_ssd_out, w_mem_out=w_mem_out, w_o=w_o, norm_mix_post=norm_mix_post,
              norm_mlp_pre=norm_mlp_pre, w_up=w_up, w_down=w_down, norm_mlp_post=norm_mlp_post)
    md = dict(norm_mix_pre=m_norm_mix_pre, w_in=m_w_in, conv_w=m_conv_w, conv_b=m_conv_b, dt_bias=m_dt_bias,
              a_log=m_a_log, d_skip=m_d_skip, ssd_norm=m_ssd_norm, norm_mem=m_norm_mem, w_mem_kv=m_w_mem_kv,
              w_sb_out=m_w_sb_out, w_ssd_out=m_w_ssd_out, w_mem_out=m_w_mem_out, w_o=m_w_o,
              norm_mix_post=m_norm_mix_post, norm_mlp_pre=m_norm_mlp_pre, w_up=m_w_up, w_down=m_w_down,
              norm_mlp_post=m_norm_mlp_post)
    vd = dict(norm_mix_pre=v_norm_mix_pre, w_in=v_w_in, conv_w=v_conv_w, conv_b=v_conv_b, dt_bias=v_dt_bias,
              a_log=v_a_log, d_skip=v_d_skip, ssd_norm=v_ssd_norm, norm_mem=v_norm_mem, w_mem_kv=v_w_mem_kv,
              w_sb_out=v_w_sb_out, w_ssd_out=v_w_ssd_out, w_mem_out=v_w_mem_out, w_o=v_w_o,
              norm_mix_post=v_norm_mix_post, norm_mlp_pre=v_norm_mlp_pre, w_up=v_w_up, w_down=v_w_down,
              norm_mlp_post=v_norm_mlp_post)
    w_in_t, m_in_t, v_in_t = (t["w_in"][0].T for t in (wd, md, vd))
    shards = {n: _cast_shard(w_in_t if n == "w_in" else wd[n][0], "cast_" + n) for n in BIG}
    w_in_g, conv_w_g = _gather_two_level([shards["w_in"], wd["conv_w"][0]], "gather_w_in")
    wt = {"w_in": w_in_g.reshape(N_DEV * w_in_g.shape[1], w_in_g.shape[2])}
    ch = conv_w_g.shape[2]

    p = {n: wd[n] for n in SMALL}
    p["conv_w"] = conv_w_g.transpose(1, 0, 2).reshape(CONV_K, N_DEV * ch)
    loss, grad_x, gw, sg, late_received = _local_step(x[0], mem[0], loss_target[0], p, wt,
                                                      [shards[n] for n in LATE_W])

    received = dict(zip(LATE_W, late_received))
    w_in_windows = gw["w_in"]

    grads, deltas, new_m, new_v = {}, {}, {}, {}
    for n in BIG:
        if n == "w_in":
            res = [t.T for t in _adamw_windows(w_in_windows, w_in_t, m_in_t, v_in_t, "adamw_" + n)]
        else:
            res = _adamw_reduce(received[n], wd[n][0], md[n][0], vd[n][0], "adamw_" + n)
        grads[n], deltas[n], new_m[n], new_v[n] = (t[None] for t in res)
    small_names = SMALL + ["conv_w"]
    gcw = sg["conv_w"].reshape(CONV_K, N_DEV, ch).transpose(1, 0, 2)
    small_of = lambda dct: [dct[n] for n in SMALL] + [dct["conv_w"][0]]
    loss_red, g_s, d_s, m_s, v_s = _small_step([sg[n] for n in SMALL], gcw, loss, small_of(wd), small_of(md),
                                               small_of(vd))
    for i, n in enumerate(small_names):
        shape = wd[n].shape
        grads[n], deltas[n], new_m[n], new_v[n] = (t.reshape(shape) for t in (g_s[i], d_s[i], m_s[i], v_s[i]))
    loss_out = loss_red.reshape(())

    return (loss_out, grad_x[None], *[grads[n] for n in ALL_W], *[deltas[n] for n in ALL_W],
            *[new_m[n] for n in ALL_W], *[new_v[n] for n in ALL_W])
```
